```python
import math
import jax, jax.numpy as jnp
from jax import lax
import numpy as np

D_MODEL = 1024
BATCH = 4
SEQ = 8192
DEPTH = 1

N_HEADS = 8
N_KV_HEADS = 2
HEAD_DIM = 64
ATTN_WIDTH = N_HEADS * HEAD_DIM
KV_WIDTH = N_KV_HEADS * HEAD_DIM
WINDOW = 128
BLOCK_Q = 128
HYENA_WIDTH = 512
HYENA_ORDER = 2
SHORT_CONV = 3
FILTER_EMB = 33
FILTER_BANDS = (FILTER_EMB - 1) // 2
FILTER_HIDDEN = 64
WINDOW_SHIFT = 0.05
FAST_DECAY_PCT = 0.3
SLOW_DECAY_PCT = 1.5
DECAY_TARGET = 1e-2
N_FILTERS = HYENA_ORDER * 2 * HYENA_WIDTH
N_GROUPS = 8
EXPERTS_PER_GROUP = 8
N_EXPERTS = N_GROUPS * EXPERTS_PER_GROUP
TOP_K = 2
D_EXPERT = 512
MOE_BLOCK = 256
LN_EPS = 1e-5
DN_ALPHA = (2.0 * DEPTH) ** 0.25
DN_BETA = (8.0 * DEPTH) ** -0.25
IN_WIDTH = ATTN_WIDTH + 2 * KV_WIDTH + (HYENA_ORDER + 1) * HYENA_WIDTH + 2 * D_MODEL

kernel_name = "hybrid_hyena_swa_hmoe_deepnorm_encoder"


def layer_norm(x, g, b):
    xf = x.astype(jnp.float32)
    mu = jnp.mean(xf, axis=-1, keepdims=True)
    var = jnp.mean(jnp.square(xf - mu), axis=-1, keepdims=True)
    y = (xf - mu) * lax.rsqrt(var + LN_EPS) * g.astype(jnp.float32) + b.astype(jnp.float32)
    return y.astype(x.dtype)


def windowed_attention(q, k, v, sink):
    B, S = q.shape[0], q.shape[1]
    nb = S // BLOCK_Q
    G = N_HEADS // N_KV_HEADS
    qb = q.reshape(B, nb, BLOCK_Q, N_KV_HEADS, G, HEAD_DIM)
    k = k.reshape(B, S, N_KV_HEADS, HEAD_DIM)
    v = v.reshape(B, S, N_KV_HEADS, HEAD_DIM)

    def band(t):
        tp = jnp.pad(t, ((0, 0), (BLOCK_Q, BLOCK_Q), (0, 0), (0, 0)))
        tp = tp.reshape(B, nb + 2, BLOCK_Q, N_KV_HEADS, HEAD_DIM)
        return jnp.concatenate([tp[:, :-2], tp[:, 1:-1], tp[:, 2:]], axis=2)

    kw, vw = band(k), band(v)
    s = jnp.einsum('bnqkgd,bnjkd->bnkgqj', qb, kw).astype(jnp.float32) * (HEAD_DIM ** -0.5)
    a = jnp.arange(BLOCK_Q)[:, None]
    j = jnp.arange(3 * BLOCK_Q)[None, :]
    rel = j - BLOCK_Q - a
    kpos = jnp.arange(nb)[:, None, None] * BLOCK_Q - BLOCK_Q + j[None]
    valid = (jnp.abs(rel) <= WINDOW)[None] & (kpos >= 0) & (kpos < S)
    slopes = (2.0 ** (-8.0 * jnp.arange(1, N_HEADS + 1, dtype=jnp.float32) / N_HEADS)).reshape(N_KV_HEADS, G)
    alibi = -slopes[:, :, None, None] * jnp.abs(rel).astype(jnp.float32)
    s = jnp.where(valid[None, :, None, None], s + alibi[None, None], -jnp.inf)
    snk = sink.astype(jnp.float32).reshape(N_KV_HEADS, G)[None, None, :, :, None, None]
    m = jnp.maximum(jnp.max(s, axis=-1, keepdims=True), snk)
    p = jnp.exp(s - m)
    p = p / (jnp.sum(p, axis=-1, keepdims=True) + jnp.exp(snk - m))
    o = jnp.einsum('bnkgqj,bnjkd->bnqkgd', p.astype(vw.dtype), vw)
    return o.reshape(B, S, ATTN_WIDTH)


def hyena_filters(L, w1, b1, fr1, w2, b2, fr2, w3, decay):
    f32 = jnp.float32
    t = jnp.linspace(0.0, 1.0, L, dtype=f32)[:, None]
    w = 2.0 * math.pi * jnp.arange(L, dtype=f32)[:, None] / L
    bands = jnp.linspace(1e-4, FILTER_BANDS - 1, FILTER_BANDS, dtype=f32)[None, :]
    z = jnp.concatenate([t, jnp.cos(bands * w), -jnp.sin(bands * w)], axis=-1)
    hdn = jnp.sin(fr1.astype(f32) * (z @ w1.astype(f32) + b1.astype(f32)))
    hdn = jnp.sin(fr2.astype(f32) * (hdn @ w2.astype(f32) + b2.astype(f32)))
    k = hdn @ w3.astype(f32)
    k = k * (jnp.exp(-t * jnp.abs(decay.astype(f32))) + WINDOW_SHIFT)
    k = k.reshape(L, HYENA_ORDER, 2, HYENA_WIDTH)
    k = k / jnp.sum(jnp.abs(k), axis=(0, 2), keepdims=True)
    fwd, bwd = k[:, :, 0], k[:, :, 1]
    return jnp.concatenate([fwd, jnp.zeros_like(fwd[:1]), bwd[:0:-1]], axis=0)


def fft_long_conv(z, kc, skip):
    L = z.shape[1]
    zf = z.astype(jnp.float32)
    y = jnp.fft.irfft(jnp.fft.rfft(zf, n=2 * L, axis=1) * jnp.fft.rfft(kc, axis=0)[None], n=2 * L, axis=1)[:, :L]
    return (y + zf * skip.astype(jnp.float32)).astype(z.dtype)


def hyena_branch(u, conv_w, conv_b, fw1, fb1, ff1, fw2, fb2, ff2, fw3, decay, skip):
    L = u.shape[1]
    u = lax.conv_general_dilated(u, conv_w[:, None, :].astype(u.dtype), (1,),
                                 [(SHORT_CONV // 2, SHORT_CONV // 2)],
                                 dimension_numbers=('NWC', 'WIO', 'NWC'),
                                 feature_group_count=u.shape[-1]) + conv_b
    v, x1, x2 = jnp.split(u, HYENA_ORDER + 1, axis=-1)
    kc = hyena_filters(L, fw1, fb1, ff1, fw2, fb2, ff2, fw3, decay)
    z = v
    for o, gate in enumerate((x1, x2)):
        z = gate * fft_long_conv(z, kc[:, o], skip[o])
    return z


def token_mixer(h, w_in, conv_w, conv_b, fw1, fb1, ff1, fw2, fb2, ff2, fw3, decay, skip,
                w_hy_o, w_attn_o, attn_sink, w_out):
    proj = h @ w_in
    cuts = np.cumsum([ATTN_WIDTH, KV_WIDTH, KV_WIDTH, (HYENA_ORDER + 1) * HYENA_WIDTH, D_MODEL]).tolist()
    q, k, v, hy_u, g_attn, g_hy = jnp.split(proj, cuts, axis=-1)
    attn = windowed_attention(q, k, v, attn_sink) @ w_attn_o
    hy = hyena_branch(hy_u, conv_w, conv_b, fw1, fb1, ff1, fw2, fb2, ff2, fw3, decay, skip) @ w_hy_o
    merged = jax.nn.sigmoid(g_attn) * attn + jax.nn.sigmoid(g_hy) * hy
    return merged @ w_out


def hierarchical_moe(h, wg, bg, we, be, w1, w3, w2):
    B, S, D = h.shape
    T = B * S
    hf = h.reshape(T, D)
    tid = jnp.arange(T)
    gl = (hf @ wg + bg).astype(jnp.float32)
    g = jnp.argmax(gl, axis=-1)
    pg = jax.nn.softmax(gl, axis=-1)[tid, g][:, None]
    el = (hf @ we + be).astype(jnp.float32).reshape(T, N_GROUPS, EXPERTS_PER_GROUP)[tid, g]
    topv, topi = lax.top_k(el, TOP_K)
    wts = (jax.nn.softmax(topv, axis=-1) * pg).reshape(-1)
    eid = (g[:, None] * EXPERTS_PER_GROUP + topi).reshape(-1)
    tok = jnp.repeat(tid, TOP_K)
    M = T * TOP_K
    order = jnp.argsort(eid)
    se, st, sw = eid[order], tok[order], wts[order]
    sizes = jnp.bincount(eid, length=N_EXPERTS)
    offs = jnp.cumsum(sizes) - sizes
    psizes = (sizes + MOE_BLOCK - 1) // MOE_BLOCK * MOE_BLOCK
    pends = jnp.cumsum(psizes)
    poffs = pends - psizes
    dest = poffs[se] + (jnp.arange(M) - offs[se])
    P = M + N_EXPERTS * MOE_BLOCK
    nblk = P // MOE_BLOCK
    buf_tok = jnp.full((P,), T, jnp.int32).at[dest].set(st.astype(jnp.int32))
    buf_w = jnp.zeros((P,), jnp.float32).at[dest].set(sw)
    blk_e = jnp.clip(jnp.searchsorted(pends, jnp.arange(nblk) * MOE_BLOCK, side='right'), 0, N_EXPERTS - 1)
    hpad = jnp.concatenate([hf, jnp.zeros((1, D), hf.dtype)], axis=0)
    xb = hpad[buf_tok].reshape(nblk, MOE_BLOCK, D)

    def expert_block(args):
        xi, e = args
        return (jax.nn.silu(xi @ w1[e]) * (xi @ w3[e])) @ w2[e]

    yb = lax.map(expert_block, (xb, blk_e)).reshape(P, D)
    y = jax.ops.segment_sum(yb * buf_w[:, None].astype(yb.dtype), buf_tok, num_segments=T + 1)[:T]
    return y.reshape(B, S, D)


def encoder_layer(x, c, w_ada, b_ada, w_in, conv_w, conv_b, fw1, fb1, ff1, fw2, fb2, ff2, fw3,
                  decay, skip, w_hy_o, w_attn_o, attn_sink, w_out, ln1_g, ln1_b,
                  rg_w, rg_b, re_w, re_b, ew1, ew3, ew2, ln2_g, ln2_b):
    mod = (jax.nn.silu(c) @ w_ada + b_ada)[:, None, :]
    shift1, scale1, gate1, shift2, scale2, gate2 = jnp.split(mod, 6, axis=-1)
    h = x * (1.0 + scale1) + shift1
    y = token_mixer(h, w_in, conv_w, conv_b, fw1, fb1, ff1, fw2, fb2, ff2, fw3, decay, skip,
                    w_hy_o, w_attn_o, attn_sink, w_out)
    x = layer_norm(DN_ALPHA * x + gate1 * y, ln1_g, ln1_b)
    h = x * (1.0 + scale2) + shift2
    y = hierarchical_moe(h, rg_w, rg_b, re_w, re_b, ew1, ew3, ew2)
    return layer_norm(DN_ALPHA * x + gate2 * y, ln2_g, ln2_b)


def setup_inputs(seed: int = 0) -> dict:
    key = jax.random.key(seed)
    ks = jax.random.split(key, 32)
    f32 = jnp.float32
    nrm = lambda k, shape, scale: jax.random.normal(k, shape, f32) * scale
    Dp, C, D = DEPTH, HYENA_WIDTH, D_MODEL
    col_scale = jnp.concatenate([jnp.ones((ATTN_WIDTH + KV_WIDTH,), f32),
                                 jnp.full((KV_WIDTH + C,), DN_BETA, f32),
                                 jnp.ones((2 * C + 2 * D,), f32)])
    d_min = -math.log(DECAY_TARGET) / SLOW_DECAY_PCT
    d_max = -math.log(DECAY_TARGET) / FAST_DECAY_PCT
    base_decay = jnp.tile(jnp.linspace(d_min, d_max, C, dtype=f32), HYENA_ORDER * 2)
    return {
        "x": nrm(ks[0], (BATCH, SEQ, D), 1.0),
        "c": nrm(ks[1], (BATCH, D), 1.0),
        "w_ada": nrm(ks[2], (Dp, D, 6 * D), D ** -0.5),
        "b_ada": nrm(ks[3], (Dp, 6 * D), 0.02),
        "w_in": nrm(ks[4], (Dp, D, IN_WIDTH), D ** -0.5) * col_scale,
        "conv_w": nrm(ks[5], (Dp, SHORT_CONV, (HYENA_ORDER + 1) * C), SHORT_CONV ** -0.5),
        "conv_b": nrm(ks[6], (Dp, (HYENA_ORDER + 1) * C), 0.02),
        "filt_w1": nrm(ks[7], (Dp, FILTER_EMB, FILTER_HIDDEN), FILTER_EMB ** -0.5),
        "filt_b1": nrm(ks[8], (Dp, FILTER_HIDDEN), 0.1),
        "filt_freq1": 1.0 + nrm(ks[9], (Dp, FILTER_HIDDEN), 0.1),
        "filt_w2": nrm(ks[10], (Dp, FILTER_HIDDEN, FILTER_HIDDEN), FILTER_HIDDEN ** -0.5),
        "filt_b2": nrm(ks[11], (Dp, FILTER_HIDDEN), 0.1),
        "filt_freq2": 1.0 + nrm(ks[12], (Dp, FILTER_HIDDEN), 0.1),
        "filt_w3": nrm(ks[13], (Dp, FILTER_HIDDEN, N_FILTERS), FILTER_HIDDEN ** -0.5),
        "filt_decay": base_decay[None] + nrm(ks[14], (Dp, N_FILTERS), 0.01),
        "hy_skip": nrm(ks[15], (Dp, HYENA_ORDER, C), 0.5),
        "w_hy_o": nrm(ks[16], (Dp, C, D), C ** -0.5 * DN_BETA),
        "w_attn_o": nrm(ks[17], (Dp, ATTN_WIDTH, D), ATTN_WIDTH ** -0.5 * DN_BETA),
        "attn_sink": nrm(ks[18], (Dp, N_HEADS), 0.5),
        "w_out": nrm(ks[19], (Dp, D, D), D ** -0.5 * DN_BETA),
        "ln1_g": 1.0 + nrm(ks[20], (Dp, D), 0.02),
        "ln1_b": nrm(ks[21], (Dp, D), 0.02),
        "router_group_w": nrm(ks[22], (Dp, D, N_GROUPS), D ** -0.5),
        "router_group_b": nrm(ks[23], (Dp, N_GROUPS), 0.01),
        "router_expert_w": nrm(ks[24], (Dp, D, N_EXPERTS), D ** -0.5),
        "router_expert_b": nrm(ks[25], (Dp, N_EXPERTS), 0.01),
        "exp_w1": nrm(ks[26], (Dp, N_EXPERTS, D, D_EXPERT), D ** -0.5),
        "exp_w3": nrm(ks[27], (Dp, N_EXPERTS, D, D_EXPERT), D ** -0.5),
        "exp_w2": nrm(ks[28], (Dp, N_EXPERTS, D_EXPERT, D), D_EXPERT ** -0.5 * DN_BETA),
        "ln2_g": 1.0 + nrm(ks[29], (Dp, D), 0.02),
        "ln2_b": nrm(ks[30], (Dp, D), 0.02),
    }


def reference(x, c, w_ada, b_ada, w_in, conv_w, conv_b, filt_w1, filt_b1, filt_freq1, filt_w2,
              filt_b2, filt_freq2, filt_w3, filt_decay, hy_skip, w_hy_o, w_attn_o, attn_sink,
              w_out, ln1_g, ln1_b, router_group_w, router_group_b, router_expert_w,
              router_expert_b, exp_w1, exp_w3, exp_w2, ln2_g, ln2_b):
    for l in range(DEPTH):
        x = encoder_layer(x, c, w_ada[l], b_ada[l], w_in[l], conv_w[l], conv_b[l], filt_w1[l],
                          filt_b1[l], filt_freq1[l], filt_w2[l], filt_b2[l], filt_freq2[l],
                          filt_w3[l], filt_decay[l], hy_skip[l], w_hy_o[l], w_attn_o[l],
                          attn_sink[l], w_out[l], ln1_g[l], ln1_b[l], router_group_w[l],
                          router_group_b[l], router_expert_w[l], router_expert_b[l],
                          exp_w1[l], exp_w3[l], exp_w2[l], ln2_g[l], ln2_b[l])
    return x
```

```python
import functools
import math

import numpy as np
import jax
import jax.numpy as jnp
from jax import lax
from jax.experimental import pallas as pl
from jax.experimental.pallas import tpu as pltpu

F32 = jnp.float32
BF16 = jnp.bfloat16

N_HEADS = 8
N_KV_HEADS = 2
HEAD_DIM = 64
ATTN_WIDTH = N_HEADS * HEAD_DIM
KV_WIDTH = N_KV_HEADS * HEAD_DIM
WINDOW = 128
BLOCK_Q = 128
HYENA_WIDTH = 512
FILTER_EMB = 33
FILTER_BANDS = (FILTER_EMB - 1) // 2
WINDOW_SHIFT = 0.05
N_GROUPS = 8
EXPERTS_PER_GROUP = 8
N_EXPERTS = N_GROUPS * EXPERTS_PER_GROUP
D_EXPERT = 512
MOE_BLOCK = 256
LN_EPS = 1e-5
DEPTH = 1
DN_ALPHA = (2.0 * DEPTH) ** 0.25
NEG = -1e30

LANES = 128
ROUTE_OFF = N_GROUPS
VMEM_LIMIT = 56 * 1024 * 1024


def _params(n_axes, vmem=VMEM_LIMIT):
    return pltpu.CompilerParams(dimension_semantics=("arbitrary",) * n_axes, vmem_limit_bytes=vmem)


def _split(a):
    hi = a.astype(BF16)
    lo = (a - hi.astype(F32)).astype(BF16)
    return hi, lo


def _dot3(a, b_hi, b_lo):
    a_hi, a_lo = _split(a)
    acc = jnp.dot(a_hi, b_hi, preferred_element_type=F32)
    acc = acc + jnp.dot(a_hi, b_lo, preferred_element_type=F32)
    acc = acc + jnp.dot(a_lo, b_hi, preferred_element_type=F32)
    return acc


def _ldot3(w_hi, w_lo, a):
    a_hi, a_lo = _split(a)
    acc = jnp.dot(w_hi, a_hi, preferred_element_type=F32)
    acc = acc + jnp.dot(w_hi, a_lo, preferred_element_type=F32)
    acc = acc + jnp.dot(w_lo, a_hi, preferred_element_type=F32)
    return acc


def _sigmoid(x):
    return 1.0 / (1.0 + jnp.exp(-x))


def _layer_norm(r, g, b):
    mu = jnp.mean(r, axis=-1, keepdims=True)
    d = r - mu
    var = jnp.mean(d * d, axis=-1, keepdims=True)
    return d * lax.rsqrt(var + LN_EPS) * g + b


def _ada_kernel(c_ref, wh_ref, wl_ref, b_ref, o_ref):
    c = c_ref[...]
    s = c * _sigmoid(c)
    o_ref[...] = _dot3(s, wh_ref[...], wl_ref[...]) + b_ref[...]


def _ada(c, w_ada, b_ada):
    B, D = c.shape
    n_out = w_ada.shape[1]
    rows = 8
    cp = jnp.zeros((rows, D), F32).at[:B].set(c)
    wh, wl = _split(w_ada)
    tn = 1024
    out = pl.pallas_call(
        _ada_kernel,
        out_shape=jax.ShapeDtypeStruct((rows, n_out), F32),
        grid=(n_out // tn,),
        in_specs=[pl.BlockSpec((rows, D), lambda j: (0, 0)),
                  pl.BlockSpec((D, tn), lambda j: (0, j)),
                  pl.BlockSpec((D, tn), lambda j: (0, j)),
                  pl.BlockSpec((1, tn), lambda j: (0, j))],
        out_specs=pl.BlockSpec((rows, tn), lambda j: (0, j)),
        compiler_params=_params(1),
        name="ada",
    )(cp, wh, wl, b_ada.reshape(1, n_out))
    return out[:B].reshape(B, 6, D)


def _inproj_kernel(x_ref, mod_ref, w_ref, q_ref, k_ref, v_ref, hy_ref, ga_ref, gh_ref):
    C = HYENA_WIDTH
    x = x_ref[0]
    shift = mod_ref[0, 0:1, :]
    scale = mod_ref[0, 1:2, :]
    h = (x * (1.0 + scale) + shift).astype(BF16)

    def seg(lo, hi):
        return jnp.dot(h, w_ref[:, lo:hi], preferred_element_type=F32)

    o = 0
    q_ref[0] = (seg(o, o + ATTN_WIDTH) * (HEAD_DIM ** -0.5)).astype(BF16)
    o += ATTN_WIDTH
    k_ref[0] = seg(o, o + KV_WIDTH).astype(BF16)
    o += KV_WIDTH
    v_ref[0] = seg(o, o + KV_WIDTH).astype(BF16)
    o += KV_WIDTH
    hy_ref[0] = seg(o, o + 3 * C)
    o += 3 * C
    D = x.shape[-1]
    ga_ref[0] = _sigmoid(seg(o, o + D)).astype(BF16)
    o += D
    gh_ref[0] = _sigmoid(seg(o, o + D)).astype(BF16)


def _in_proj(x, mod, w_in):
    B, S, D = x.shape
    C = HYENA_WIDTH
    tm = min(512, S)
    wb = w_in.astype(BF16)
    nw = wb.shape[1]
    row = lambda b, i: (b, i, 0)
    shapes = [(ATTN_WIDTH, BF16), (KV_WIDTH, BF16), (KV_WIDTH, BF16), (3 * C, F32), (D, BF16), (D, BF16)]
    return pl.pallas_call(
        _inproj_kernel,
        out_shape=[jax.ShapeDtypeStruct((B, S, w), dt) for w, dt in shapes],
        grid=(B, S // tm),
        in_specs=[pl.BlockSpec((1, tm, D), row),
                  pl.BlockSpec((1, 6, D), lambda b, i: (b, 0, 0)),
                  pl.BlockSpec((D, nw), lambda b, i: (0, 0))],
        out_specs=[pl.BlockSpec((1, tm, w), row) for w, _ in shapes],
        compiler_params=_params(2),
        name="in_proj",
    )(x, mod, wb)


def _attn_kernel(sink_ref, q_ref, kp_ref, kc_ref, kn_ref, vp_ref, vc_ref, vn_ref, bias_ref, o_ref):
    i = pl.program_id(1)
    nb = pl.num_programs(1)
    Q = BLOCK_Q
    col = lax.broadcasted_iota(jnp.int32, (1, 3 * Q), 1)
    lo = jnp.where(i == 0, Q, 0)
    hi = jnp.where(i == nb - 1, 2 * Q, 3 * Q)
    colbias = jnp.where((col >= lo) & (col < hi), 0.0, NEG)
    kb = jnp.concatenate([kp_ref[0], kc_ref[0], kn_ref[0]], axis=0)
    vb = jnp.concatenate([vp_ref[0], vc_ref[0], vn_ref[0]], axis=0)
    G = N_HEADS // N_KV_HEADS
    outs = []
    for kv in range(N_KV_HEADS):
        kk = kb[:, kv * HEAD_DIM:(kv + 1) * HEAD_DIM]
        vv = vb[:, kv * HEAD_DIM:(kv + 1) * HEAD_DIM]
        for g in range(G):
            h = kv * G + g
            qh = q_ref[0, :, h * HEAD_DIM:(h + 1) * HEAD_DIM]
            s = lax.dot_general(qh, kk, (((1,), (1,)), ((), ())), preferred_element_type=F32)
            s = s + bias_ref[h] + colbias
            snk = sink_ref[h]
            m = jnp.maximum(jnp.max(s, axis=1, keepdims=True), snk)
            p = jnp.exp(s - m)
            den = jnp.sum(p, axis=1, keepdims=True) + jnp.exp(snk - m)
            o = jnp.dot(p.astype(BF16), vv, preferred_element_type=F32)
            outs.append(o / den)
    o_ref[0] = jnp.concatenate(outs, axis=1).astype(BF16)


def _attention(q, k, v, sink):
    B, S, _ = q.shape
    Q = BLOCK_Q
    nb = S // Q
    a = jnp.arange(Q)[:, None]
    j = jnp.arange(3 * Q)[None, :]
    rel = jnp.abs(j - Q - a).astype(F32)
    slopes = 2.0 ** (-8.0 * jnp.arange(1, N_HEADS + 1, dtype=F32) / N_HEADS)
    bias = jnp.where(rel[None] <= WINDOW, -slopes[:, None, None] * rel[None], NEG).astype(F32)
    prev = lambda b, i: (b, jnp.maximum(i - 1, 0), 0)
    cur = lambda b, i: (b, i, 0)
    nxt = lambda b, i: (b, jnp.minimum(i + 1, nb - 1), 0)
    kvspec = lambda f: pl.BlockSpec((1, Q, KV_WIDTH), f)
    return pl.pallas_call(
        _attn_kernel,
        out_shape=jax.ShapeDtypeStruct((B, S, ATTN_WIDTH), BF16),
        grid=(B, nb),
        in_specs=[pl.BlockSpec(memory_space=pltpu.SMEM),
                  pl.BlockSpec((1, Q, ATTN_WIDTH), cur),
                  kvspec(prev), kvspec(cur), kvspec(nxt),
                  kvspec(prev), kvspec(cur), kvspec(nxt),
                  pl.BlockSpec((N_HEADS, Q, 3 * Q), lambda b, i: (0, 0, 0))],
        out_specs=pl.BlockSpec((1, Q, ATTN_WIDTH), cur),
        compiler_params=_params(2),
        name="attn",
    )(sink.astype(F32), q, k, k, k, v, v, v, bias)


def _shortconv_kernel(u_ref, p_ref, n_ref, w_ref, b_ref, v_ref, x1_ref, x2_ref):
    i = pl.program_id(1)
    n = pl.num_programs(1)
    C = HYENA_WIDTH
    u = u_ref[0]
    tr = u.shape[0]
    prow = jnp.where(i > 0, p_ref[0, 7:8, :], 0.0)
    nrow = jnp.where(i < n - 1, n_ref[0, 0:1, :], 0.0)
    rid = lax.broadcasted_iota(jnp.int32, (tr, 1), 0)
    up = jnp.where(rid == 0, prow, pltpu.roll(u, 1, 0))
    dn = jnp.where(rid == tr - 1, nrow, pltpu.roll(u, tr - 1, 0))
    out = w_ref[0:1, :] * up + w_ref[1:2, :] * u + w_ref[2:3, :] * dn + b_ref[...]
    v_ref[0] = out[:, :C]
    x1_ref[0] = out[:, C:2 * C]
    x2_ref[0] = out[:, 2 * C:]


def _short_conv(hy_u, conv_w, conv_b):
    B, S, W = hy_u.shape
    C = HYENA_WIDTH
    tr = min(512, S)
    r8 = tr // 8
    nb8 = S // 8
    return pl.pallas_call(
        _shortconv_kernel,
        out_shape=[jax.ShapeDtypeStruct((B, S, C), F32)] * 3,
        grid=(B, S // tr),
        in_specs=[pl.BlockSpec((1, tr, W), lambda b, i: (b, i, 0)),
                  pl.BlockSpec((1, 8, W), lambda b, i: (b, jnp.maximum(i * r8 - 1, 0), 0)),
                  pl.BlockSpec((1, 8, W), lambda b, i: (b, jnp.minimum((i + 1) * r8, nb8 - 1), 0)),
                  pl.BlockSpec((3, W), lambda b, i: (0, 0)),
                  pl.BlockSpec((1, W), lambda b, i: (0, 0))],
        out_specs=[pl.BlockSpec((1, tr, C), lambda b, i: (b, i, 0))] * 3,
        compiler_params=_params(2),
        name="shortconv",
    )(hy_u, hy_u, hy_u, conv_w, conv_b.reshape(1, W))


def _filter_kernel(z_ref, w1h, w1l, b1_ref, f1_ref, w2h, w2l, b2_ref, f2_ref, w3h, w3l, dec_ref,
                   k_ref, s_ref):
    i = pl.program_id(0)
    z = z_ref[...]
    h1 = jnp.sin(f1_ref[...] * (_dot3(z, w1h[...], w1l[...]) + b1_ref[...]))
    h2 = jnp.sin(f2_ref[...] * (_dot3(h1, w2h[...], w2l[...]) + b2_ref[...]))
    k = _dot3(h2, w3h[...], w3l[...])
    t = z[:, 0:1]
    k = k * (jnp.exp(-t * jnp.abs(dec_ref[...])) + WINDOW_SHIFT)
    k_ref[...] = k

    @pl.when(i == 0)
    def _():
        s_ref[...] = jnp.zeros_like(s_ref)

    s_ref[...] += jnp.sum(jnp.abs(k), axis=0, keepdims=True)


def _filter_embedding(L):
    t = np.linspace(0.0, 1.0, L, dtype=np.float32).astype(np.float64)[:, None]
    w = (2.0 * math.pi * np.arange(L, dtype=np.float32) / np.float32(L)).astype(np.float64)[:, None]
    bands = np.linspace(1e-4, FILTER_BANDS - 1, FILTER_BANDS, dtype=np.float32).astype(np.float64)[None, :]
    bw = (bands.astype(np.float32) * w.astype(np.float32)).astype(np.float64)
    z = np.concatenate([t, np.cos(bw), -np.sin(bw)], axis=-1)
    zp = np.zeros((L, LANES), np.float32)
    zp[:, :FILTER_EMB] = z.astype(np.float32)
    return jnp.asarray(zp)


def _pad2(a, r, c):
    return jnp.zeros((r, c), F32).at[:a.shape[0], :a.shape[1]].set(a.astype(F32))


def _filters(L, fw1, fb1, ff1, fw2, fb2, ff2, fw3, decay):
    H = LANES
    nf = fw3.shape[1]
    z = _filter_embedding(L)
    w1h, w1l = _split(_pad2(fw1, H, H))
    w2h, w2l = _split(_pad2(fw2, H, H))
    w3h, w3l = _split(_pad2(fw3, H, nf))
    b1 = _pad2(fb1[None], 1, H)
    f1 = _pad2(ff1[None], 1, H)
    b2 = _pad2(fb2[None], 1, H)
    f2 = _pad2(ff2[None], 1, H)
    tr = min(512, L)
    full = lambda r, c: pl.BlockSpec((r, c), lambda i: (0, 0))
    return pl.pallas_call(
        _filter_kernel,
        out_shape=[jax.ShapeDtypeStruct((L, nf), F32), jax.ShapeDtypeStruct((1, nf), F32)],
        grid=(L // tr,),
        in_specs=[pl.BlockSpec((tr, H), lambda i: (i, 0)),
                  full(H, H), full(H, H), full(1, H), full(1, H),
                  full(H, H), full(H, H), full(1, H), full(1, H),
                  full(H, nf), full(H, nf), full(1, nf)],
        out_specs=[pl.BlockSpec((tr, nf), lambda i: (i, 0)), full(1, nf)],
        compiler_params=_params(1),
        name="filter",
    )(z, w1h, w1l, b1, f1, w2h, w2l, b2, f2, w3h, w3l, decay.reshape(1, nf).astype(F32))


def _np_split(m64):
    hi = m64.astype(np.float32).astype(BF16)
    lo = (m64 - hi.astype(np.float64)).astype(np.float32).astype(BF16)
    return jnp.asarray(hi), jnp.asarray(lo)


def _dft_constants(L):
    N = 2 * L
    n2 = LANES
    n1 = N // n2
    h1 = n1 // 2
    k1 = np.arange(n1)[:, None]
    s1 = np.arange(h1)[None, :]
    ang = -2.0 * np.pi * ((k1 * s1) % n1) / n1
    wr, wi = np.cos(ang), np.sin(ang)
    w1_real = np.concatenate([wr, wi], axis=0)
    w1_cplx = np.block([[wr, -wi], [wi, wr]])
    vr, vi = wr.T / N, -wi.T / N
    w3 = np.block([[vr, -vi], [vi, vr]])
    k2 = np.arange(n2)[:, None]
    s2 = np.arange(n2)[None, :]
    a2 = -2.0 * np.pi * ((k2 * s2) % n2) / n2
    w2r, w2i = jnp.asarray(np.cos(a2), F32), jnp.asarray(np.sin(a2), F32)
    at = -2.0 * np.pi * ((np.arange(n1)[:, None] * s2) % N) / N
    twr, twi = jnp.asarray(np.cos(at), F32), jnp.asarray(np.sin(at), F32)
    mr = w2r[None] * twr[:, None, :] - w2i[None] * twi[:, None, :]
    mi = w2r[None] * twi[:, None, :] + w2i[None] * twr[:, None, :]
    fwd = jnp.concatenate([jnp.concatenate([mr, -mi], axis=2),
                           jnp.concatenate([mi, mr], axis=2)], axis=1)
    inv = jnp.swapaxes(fwd, 1, 2)
    return dict(n1=n1, w1_real=_np_split(w1_real), w1_cplx=_np_split(w1_cplx), w3=_np_split(w3),
                fwd=_split(fwd), inv=_split(inv))


SCH = 8


def _dft1_kernel(x_ref, wh_ref, wl_ref, a_ref, *, n1, packed):
    wh, wl = wh_ref[...], wl_ref[...]
    for j in range(SCH):
        if packed:
            rhs = jnp.concatenate([x_ref[0, 0, :, j, :], x_ref[0, 1, :, j, :]], axis=0)
        else:
            rhs = x_ref[:, j, :]
        res = _ldot3(wh, wl, rhs)
        a_ref[0, :, 0, j, :] = res[:n1]
        a_ref[0, :, 1, j, :] = res[n1:]


def _dft1_data(x, consts):
    B, L, C = x.shape
    n1 = consts["n1"]
    h1 = n1 // 2
    xv = x.reshape(B // 2, 2, h1, LANES, C)
    wh, wl = consts["w1_cplx"]
    return pl.pallas_call(
        functools.partial(_dft1_kernel, n1=n1, packed=True),
        out_shape=jax.ShapeDtypeStruct((B // 2, n1, 2, LANES, C), F32),
        grid=(B // 2, LANES // SCH),
        in_specs=[pl.BlockSpec((1, 2, h1, SCH, C), lambda p, j: (p, 0, 0, j, 0)),
                  pl.BlockSpec((2 * n1, n1), lambda p, j: (0, 0)),
                  pl.BlockSpec((2 * n1, n1), lambda p, j: (0, 0))],
        out_specs=pl.BlockSpec((1, n1, 2, SCH, C), lambda p, j: (p, 0, 0, j, 0)),
        compiler_params=_params(2),
        name="dft1",
    )(xv, wh, wl)


def _dft1_filter(kraw, consts):
    L, nf = kraw.shape
    C = HYENA_WIDTH
    nq = nf // C
    n1 = consts["n1"]
    h1 = n1 // 2
    kv = kraw.reshape(h1, LANES, nf)
    wh, wl = consts["w1_real"]
    return pl.pallas_call(
        functools.partial(_dft1_kernel, n1=n1, packed=False),
        out_shape=jax.ShapeDtypeStruct((nq, n1, 2, LANES, C), F32),
        grid=(nq, LANES // SCH),
        in_specs=[pl.BlockSpec((h1, SCH, C), lambda q, j: (0, j, q)),
                  pl.BlockSpec((2 * n1, h1), lambda q, j: (0, 0)),
                  pl.BlockSpec((2 * n1, h1), lambda q, j: (0, 0))],
        out_specs=pl.BlockSpec((1, n1, 2, SCH, C), lambda q, j: (q, 0, 0, j, 0)),
        compiler_params=_params(2),
        name="dft1f",
    )(kv, wh, wl)


KCH = 4


def _midf_kernel(af_ref, ab_ref, fh_ref, fl_ref, inv_ref, b0_ref, h_ref):
    n2 = LANES
    for k in range(KCH):
        xf = _ldot3(fh_ref[k], fl_ref[k], af_ref[0, k])
        xb = _ldot3(fh_ref[k], fl_ref[k], ab_ref[0, k])
        sc = inv_ref[0]
        h_ref[0, k, :n2, :] = (xf[:n2] + xb[:n2] - b0_ref[0]) * sc
        h_ref[0, k, n2:, :] = (xf[n2:] - xb[n2:]) * sc


def _filter_spectrum(af, inv_den, bwd0, consts):
    nq, n1, _, n2, C = af.shape
    a = af.reshape(nq, n1, 2 * n2, C)
    fh, fl = consts["fwd"]
    blk = lambda f: pl.BlockSpec((1, KCH, 2 * n2, C), f)
    tab = pl.BlockSpec((KCH, 2 * n2, 2 * n2), lambda k, o: (k, 0, 0))
    vec = pl.BlockSpec((1, 1, C), lambda k, o: (o, 0, 0))
    return pl.pallas_call(
        _midf_kernel,
        out_shape=jax.ShapeDtypeStruct((nq // 2, n1, 2 * n2, C), F32),
        grid=(n1 // KCH, nq // 2),
        in_specs=[blk(lambda k, o: (2 * o, k, 0, 0)), blk(lambda k, o: (2 * o + 1, k, 0, 0)),
                  tab, tab, vec, vec],
        out_specs=blk(lambda k, o: (o, k, 0, 0)),
        compiler_params=_params(2),
        name="midf",
    )(a, a, fh, fl, inv_den, bwd0)


def _mid_kernel(a_ref, fh_ref, fl_ref, ih_ref, il_ref, h_ref, b_ref):
    n2 = LANES
    for k in range(KCH):
        x = _ldot3(fh_ref[k], fl_ref[k], a_ref[0, k])
        xr, xi = x[:n2], x[n2:]
        hr, hi = h_ref[0, k, :n2, :], h_ref[0, k, n2:, :]
        y = jnp.concatenate([xr * hr - xi * hi, xr * hi + xi * hr], axis=0)
        b_ref[0, k] = _ldot3(ih_ref[k], il_ref[k], y)


def _mid(a5, hspec, order, consts):
    P, n1, _, n2, C = a5.shape
    a = a5.reshape(P, n1, 2 * n2, C)
    fh, fl = consts["fwd"]
    ih, il = consts["inv"]
    tab = pl.BlockSpec((KCH, 2 * n2, 2 * n2), lambda k, p: (k, 0, 0))
    out = pl.pallas_call(
        _mid_kernel,
        out_shape=jax.ShapeDtypeStruct((P, n1, 2 * n2, C), F32),
        grid=(n1 // KCH, P),
        in_specs=[pl.BlockSpec((1, KCH, 2 * n2, C), lambda k, p: (p, k, 0, 0)),
                  tab, tab, tab, tab,
                  pl.BlockSpec((1, KCH, 2 * n2, C), lambda k, p: (order, k, 0, 0))],
        out_specs=pl.BlockSpec((1, KCH, 2 * n2, C), lambda k, p: (p, k, 0, 0)),
        compiler_params=_params(2),
        name="mid",
    )(a, fh, fl, ih, il, hspec)
    return out.reshape(P, n1, 2, n2, C)


def _dft3_kernel(b_ref, wh_ref, wl_ref, v_ref, g_ref, skip_ref, z_ref, *, h1):
    wh, wl = wh_ref[...], wl_ref[...]
    skip = skip_ref[0]
    for j in range(SCH):
        rhs = jnp.concatenate([b_ref[0, :, 0, j, :], b_ref[0, :, 1, j, :]], axis=0)
        y = _ldot3(wh, wl, rhs)
        for r in range(2):
            yr = y[r * h1:(r + 1) * h1]
            z_ref[0, r, :, j, :] = g_ref[0, r, :, j, :] * (yr + v_ref[0, r, :, j, :] * skip)


def _dft3_gate(b5, v, gate, skip, consts):
    P, n1, _, n2, C = b5.shape
    h1 = n1 // 2
    B, L, _ = v.shape
    wh, wl = consts["w3"]
    five = lambda t: t.reshape(P, 2, h1, n2, C)
    dat = pl.BlockSpec((1, 2, h1, SCH, C), lambda p, j: (p, 0, 0, j, 0))
    out = pl.pallas_call(
        functools.partial(_dft3_kernel, h1=h1),
        out_shape=jax.ShapeDtypeStruct((P, 2, h1, n2, C), F32),
        grid=(P, n2 // SCH),
        in_specs=[pl.BlockSpec((1, n1, 2, SCH, C), lambda p, j: (p, 0, 0, j, 0)),
                  pl.BlockSpec((n1, 2 * n1), lambda p, j: (0, 0)),
                  pl.BlockSpec((n1, 2 * n1), lambda p, j: (0, 0)),
                  dat, dat,
                  pl.BlockSpec((1, C), lambda p, j: (0, 0))],
        out_specs=dat,
        compiler_params=_params(2),
        name="dft3",
    )(b5, wh, wl, five(v), five(gate), skip.reshape(1, C).astype(F32))
    return out.reshape(B, L, C)


def _hyena(hy_u, conv_w, conv_b, fw1, fb1, ff1, fw2, fb2, ff2, fw3, decay, skip):
    B, L, _ = hy_u.shape
    C = HYENA_WIDTH
    consts = _dft_constants(L)
    v, x1, x2 = _short_conv(hy_u, conv_w, conv_b)
    kraw, ksum = _filters(L, fw1, fb1, ff1, fw2, fb2, ff2, fw3, decay)
    ks = ksum.reshape(2, 2, C)
    inv_den = (1.0 / (ks[:, 0] + ks[:, 1])).reshape(2, 1, C)
    bwd0 = kraw[0].reshape(2, 2, C)[:, 1].reshape(2, 1, C)
    hspec = _filter_spectrum(_dft1_filter(kraw, consts), inv_den, bwd0, consts)
    z = v
    for o, gate in enumerate((x1, x2)):
        a5 = _dft1_data(z, consts)
        b5 = _mid(a5, hspec, o, consts)
        z = _dft3_gate(b5, z, gate, skip[o], consts)
    return z


def _merge_kernel(attn_ref, hy_ref, ga_ref, gh_ref, x_ref, mod_ref, wa_ref, wh_ref, wo_ref,
                  g1_ref, b1_ref, rwh_ref, rwl_ref, rb_ref, tri_ref,
                  x1_ref, h2_ref, eid_ref, wts_ref, rank_ref, cnt_ref, carry_ref):
    first = (pl.program_id(0) == 0) & (pl.program_id(1) == 0)

    @pl.when(first)
    def _():
        carry_ref[...] = jnp.zeros_like(carry_ref)

    a = jnp.dot(attn_ref[0], wa_ref[...], preferred_element_type=F32)
    hy = jnp.dot(hy_ref[0].astype(BF16), wh_ref[...], preferred_element_type=F32)
    merged = ga_ref[0].astype(F32) * a + gh_ref[0].astype(F32) * hy
    y = jnp.dot(merged.astype(BF16), wo_ref[...], preferred_element_type=F32)
    gate1 = mod_ref[0, 2:3, :]
    shift2 = mod_ref[0, 3:4, :]
    scale2 = mod_ref[0, 4:5, :]
    x1 = _layer_norm(DN_ALPHA * x_ref[0] + gate1 * y, g1_ref[...], b1_ref[...])
    x1_ref[0] = x1
    h2 = x1 * (1.0 + scale2) + shift2
    h2_ref[0] = h2

    logits = _dot3(h2, rwh_ref[...], rwl_ref[...]) + rb_ref[...]
    tm = logits.shape[0]
    lane = lax.broadcasted_iota(jnp.int32, (tm, LANES), 1)
    lanef = lane.astype(F32)
    big = float(LANES)

    def first_lane(mask):
        return jnp.min(jnp.where(mask, lanef, big), axis=1, keepdims=True).astype(jnp.int32)

    gmask = lane < N_GROUPS
    gl = jnp.where(gmask, logits, NEG)
    gmax = jnp.max(gl, axis=1, keepdims=True)
    gidx = first_lane(gl == gmax)
    pg = 1.0 / jnp.sum(jnp.exp(gl - gmax), axis=1, keepdims=True)
    lo = ROUTE_OFF + gidx * EXPERTS_PER_GROUP
    emask = (lane >= lo) & (lane < lo + EXPERTS_PER_GROUP)
    el = jnp.where(emask, logits, NEG)
    v1 = jnp.max(el, axis=1, keepdims=True)
    i1 = first_lane(el == v1)
    el2 = jnp.where(emask & (lane != i1), logits, NEG)
    v2 = jnp.max(el2, axis=1, keepdims=True)
    i2 = first_lane(el2 == v2)
    e21 = jnp.exp(v2 - v1)
    w1 = pg / (1.0 + e21)
    w2 = pg * e21 / (1.0 + e21)

    sel1 = lane == i1
    sel2 = lane == i2
    onehot = jnp.where(sel1 | sel2, 1.0, 0.0)
    prefix = jnp.dot(tri_ref[...], onehot.astype(BF16), preferred_element_type=F32) + carry_ref[...]
    r1 = jnp.sum(jnp.where(sel1, prefix, 0.0), axis=1, keepdims=True)
    r2 = jnp.sum(jnp.where(sel2, prefix, 0.0), axis=1, keepdims=True)
    carry_ref[...] += jnp.sum(onehot, axis=0, keepdims=True)
    cnt_ref[...] = carry_ref[...]

    eid_ref[0] = jnp.where(lane == 0, i1 - ROUTE_OFF, jnp.where(lane == 1, i2 - ROUTE_OFF, 0))
    wts_ref[0] = jnp.where(lane == 0, w1, jnp.where(lane == 1, w2, 0.0))
    rank_ref[0] = jnp.where(lane == 0, r1, jnp.where(lane == 1, r2, 0.0)).astype(jnp.int32)


def _merge(attn, hy, ga, gh, x, mod, w_attn_o, w_hy_o, w_out, ln1_g, ln1_b, rg_w, rg_b, re_w, re_b):
    B, S, D = x.shape
    tm = min(512, S)
    rw = jnp.zeros((D, LANES), F32).at[:, :N_GROUPS].set(rg_w).at[:, ROUTE_OFF:ROUTE_OFF + N_EXPERTS].set(re_w)
    rb = jnp.zeros((1, LANES), F32).at[0, :N_GROUPS].set(rg_b).at[0, ROUTE_OFF:ROUTE_OFF + N_EXPERTS].set(re_b)
    rwh, rwl = _split(rw)
    tri = (jnp.arange(tm)[:, None] > jnp.arange(tm)[None, :]).astype(BF16)
    row = lambda b, i: (b, i, 0)
    full = lambda r, c: pl.BlockSpec((r, c), lambda b, i: (0, 0))
    outs = [jax.ShapeDtypeStruct((B, S, D), F32), jax.ShapeDtypeStruct((B, S, D), F32),
            jax.ShapeDtypeStruct((B, S, LANES), jnp.int32), jax.ShapeDtypeStruct((B, S, LANES), F32),
            jax.ShapeDtypeStruct((B, S, LANES), jnp.int32), jax.ShapeDtypeStruct((1, LANES), F32)]
    return pl.pallas_call(
        _merge_kernel,
        out_shape=outs,
        grid=(B, S // tm),
        in_specs=[pl.BlockSpec((1, tm, ATTN_WIDTH), row), pl.BlockSpec((1, tm, HYENA_WIDTH), row),
                  pl.BlockSpec((1, tm, D), row), pl.BlockSpec((1, tm, D), row), pl.BlockSpec((1, tm, D), row),
                  pl.BlockSpec((1, 6, D), lambda b, i: (b, 0, 0)),
                  full(ATTN_WIDTH, D), full(HYENA_WIDTH, D), full(D, D),
                  full(1, D), full(1, D), full(D, LANES), full(D, LANES), full(1, LANES), full(tm, tm)],
        out_specs=[pl.BlockSpec((1, tm, D), row), pl.BlockSpec((1, tm, D), row),
                   pl.BlockSpec((1, tm, LANES), row), pl.BlockSpec((1, tm, LANES), row),
                   pl.BlockSpec((1, tm, LANES), row), full(1, LANES)],
        scratch_shapes=[pltpu.VMEM((1, LANES), F32)],
        compiler_params=_params(2),
        name="merge",
    )(attn, hy, ga, gh, x, mod, w_attn_o.astype(BF16), w_hy_o.astype(BF16), w_out.astype(BF16),
      ln1_g.reshape(1, D), ln1_b.reshape(1, D), rwh, rwl, rb, tri)


TOK_CH = 256


def _dispatch_kernel(dest_ref, h_ref, xin_ref, xb_ref, sem):
    del xin_ref
    base = pl.program_id(0) * TOK_CH

    def row_copy(t, d):
        return pltpu.make_async_copy(h_ref.at[pl.ds(t, 1)], xb_ref.at[pl.ds(d, 1)], sem)

    def issue(r, c):
        row_copy(base + r, dest_ref[0, 0, 2 * r]).start()
        row_copy(base + r, dest_ref[0, 0, 2 * r + 1]).start()
        return c

    lax.fori_loop(0, TOK_CH, issue, 0)

    def drain(r, c):
        row_copy(0, 0).wait()
        return c

    lax.fori_loop(0, 2 * TOK_CH, drain, 0)


def _dispatch(h2, dest, P):
    T, D = h2.shape
    nch = T // TOK_CH
    return pl.pallas_call(
        _dispatch_kernel,
        out_shape=jax.ShapeDtypeStruct((P, D), F32),
        grid=(nch,),
        in_specs=[pl.BlockSpec((1, 1, 2 * TOK_CH), lambda i: (i, 0, 0), memory_space=pltpu.SMEM),
                  pl.BlockSpec(memory_space=pl.ANY),
                  pl.BlockSpec(memory_space=pl.ANY)],
        out_specs=pl.BlockSpec(memory_space=pl.ANY),
        scratch_shapes=[pltpu.SemaphoreType.DMA(())],
        input_output_aliases={2: 0},
        compiler_params=_params(1),
        name="dispatch",
    )(dest.reshape(nch, 1, 2 * TOK_CH), h2, jnp.zeros((P, D), F32))


def _expert_kernel(be_ref, nu_ref, x_ref, w1_ref, w3_ref, w2_ref, y_ref, c1_ref, c3_ref, c2_ref):
    i = pl.program_id(0)
    used = i < nu_ref[0]
    fresh = (i == 0) | (be_ref[i] != be_ref[jnp.maximum(i - 1, 0)])

    @pl.when(used & fresh)
    def _():
        c1_ref[...] = w1_ref[0].astype(BF16)
        c3_ref[...] = w3_ref[0].astype(BF16)
        c2_ref[...] = w2_ref[0].astype(BF16)

    @pl.when(used)
    def _():
        x = x_ref[...].astype(BF16)
        a = jnp.dot(x, c1_ref[...], preferred_element_type=F32)
        g = jnp.dot(x, c3_ref[...], preferred_element_type=F32)
        hmid = (a * _sigmoid(a) * g).astype(BF16)
        y_ref[...] = jnp.dot(hmid, c2_ref[...], preferred_element_type=F32)

    @pl.when(jnp.logical_not(used))
    def _():
        y_ref[...] = jnp.zeros_like(y_ref)


def _experts(xb, blk_e, n_used, w1, w3, w2):
    P, D = xb.shape
    E, _, DE = w1.shape
    nblk = P // MOE_BLOCK
    grid_spec = pltpu.PrefetchScalarGridSpec(
        num_scalar_prefetch=2,
        grid=(nblk,),
        in_specs=[pl.BlockSpec((MOE_BLOCK, D), lambda i, be, nu: (jnp.minimum(i, jnp.maximum(nu[0] - 1, 0)), 0)),
                  pl.BlockSpec((1, D, DE), lambda i, be, nu: (be[i], 0, 0)),
                  pl.BlockSpec((1, D, DE), lambda i, be, nu: (be[i], 0, 0)),
                  pl.BlockSpec((1, DE, D), lambda i, be, nu: (be[i], 0, 0))],
        out_specs=pl.BlockSpec((MOE_BLOCK, D), lambda i, be, nu: (i, 0)),
        scratch_shapes=[pltpu.VMEM((D, DE), BF16), pltpu.VMEM((D, DE), BF16), pltpu.VMEM((DE, D), BF16)],
    )
    return pl.pallas_call(
        _expert_kernel,
        out_shape=jax.ShapeDtypeStruct((P, D), F32),
        grid_spec=grid_spec,
        compiler_params=_params(1),
        name="experts",
    )(blk_e, n_used, xb, w1, w3, w2)


def _combine_kernel(dest_ref, wts_ref, x1_ref, mod_ref, g_ref, b_ref, yb_ref, o_ref, rows_ref, sem):
    def row_copy(d, j, r):
        return pltpu.make_async_copy(yb_ref.at[pl.ds(d, 1)], rows_ref.at[j, pl.ds(r, 1)], sem)

    def issue(r, c):
        row_copy(dest_ref[0, 0, 2 * r], 0, r).start()
        row_copy(dest_ref[0, 0, 2 * r + 1], 1, r).start()
        return c

    lax.fori_loop(0, TOK_CH, issue, 0)

    def drain(r, c):
        row_copy(0, 0, 0).wait()
        return c

    lax.fori_loop(0, 2 * TOK_CH, drain, 0)
    w = wts_ref[...]
    y = w[:, 0:1] * rows_ref[0] + w[:, 1:2] * rows_ref[1]
    gate2 = mod_ref[0, 5:6, :]
    o_ref[...] = _layer_norm(DN_ALPHA * x1_ref[...] + gate2 * y, g_ref[...], b_ref[...])


def _combine(yb, dest, wts, x1, mod, ln2_g, ln2_b, S):
    T, D = x1.shape
    nch = T // TOK_CH
    per_b = S // TOK_CH
    return pl.pallas_call(
        _combine_kernel,
        out_shape=jax.ShapeDtypeStruct((T, D), F32),
        grid=(nch,),
        in_specs=[pl.BlockSpec((1, 1, 2 * TOK_CH), lambda i: (i, 0, 0), memory_space=pltpu.SMEM),
                  pl.BlockSpec((TOK_CH, LANES), lambda i: (i, 0)),
                  pl.BlockSpec((TOK_CH, D), lambda i: (i, 0)),
                  pl.BlockSpec((1, 6, D), lambda i: (i // per_b, 0, 0)),
                  pl.BlockSpec((1, D), lambda i: (0, 0)),
                  pl.BlockSpec((1, D), lambda i: (0, 0)),
                  pl.BlockSpec(memory_space=pl.ANY)],
        out_specs=pl.BlockSpec((TOK_CH, D), lambda i: (i, 0)),
        scratch_shapes=[pltpu.VMEM((2, TOK_CH, D), F32), pltpu.SemaphoreType.DMA(())],
        compiler_params=_params(1),
        name="combine",
    )(dest.reshape(nch, 1, 2 * TOK_CH), wts, x1, mod, ln2_g.reshape(1, D), ln2_b.reshape(1, D), yb)


def _moe(h2, x1, eid, wts, rank, counts, mod, w1, w3, w2, ln2_g, ln2_b):
    B, S, D = h2.shape
    T = B * S
    M = 2 * T
    P = M + N_EXPERTS * MOE_BLOCK
    nblk = P // MOE_BLOCK
    sizes = counts[0, ROUTE_OFF:ROUTE_OFF + N_EXPERTS].astype(jnp.int32)
    psizes = (sizes + MOE_BLOCK - 1) // MOE_BLOCK * MOE_BLOCK
    pends = jnp.cumsum(psizes)
    poffs = pends - psizes
    e2 = eid.reshape(T, LANES)[:, :2]
    dest = (poffs[e2] + rank.reshape(T, LANES)[:, :2]).astype(jnp.int32).reshape(M)
    blk_e = jnp.clip(jnp.searchsorted(pends, jnp.arange(nblk) * MOE_BLOCK, side='right'),
                     0, N_EXPERTS - 1).astype(jnp.int32)
    n_used = (pends[-1:] // MOE_BLOCK).astype(jnp.int32)
    xb = _dispatch(h2.reshape(T, D), dest, P)
    yb = _experts(xb, blk_e, n_used, w1, w3, w2)
    out = _combine(yb, dest, wts.reshape(T, LANES), x1.reshape(T, D), mod, ln2_g, ln2_b, S)
    return out.reshape(B, S, D)


def _layer(x, c, w_ada, b_ada, w_in, conv_w, conv_b, fw1, fb1, ff1, fw2, fb2, ff2, fw3, decay, skip,
           w_hy_o, w_attn_o, attn_sink, w_out, ln1_g, ln1_b, rg_w, rg_b, re_w, re_b, ew1, ew3, ew2,
           ln2_g, ln2_b):
    mod = _ada(c, w_ada, b_ada)
    q, k, v, hy_u, ga, gh = _in_proj(x, mod, w_in)
    attn = _attention(q, k, v, attn_sink)
    hy = _hyena(hy_u, conv_w, conv_b, fw1, fb1, ff1, fw2, fb2, ff2, fw3, decay, skip)
    x1, h2, eid, wts, rank, counts = _merge(attn, hy, ga, gh, x, mod, w_attn_o, w_hy_o, w_out,
                                            ln1_g, ln1_b, rg_w, rg_b, re_w, re_b)
    return _moe(h2, x1, eid, wts, rank, counts, mod, ew1, ew3, ew2, ln2_g, ln2_b)


def kernel(x, c, w_ada, b_ada, w_in, conv_w, conv_b, filt_w1, filt_b1, filt_freq1, filt_w2, filt_b2, filt_freq2, filt_w3, filt_decay, hy_skip, w_hy_o, w_attn_o, attn_sink, w_out, ln1_g, ln1_b, router_group_w, router_group_b, router_expert_w, router_expert_b, exp_w1, exp_w3, exp_w2, ln2_g, ln2_b):
    for l in range(w_ada.shape[0]):
        x = _layer(x, c, w_ada[l], b_ada[l], w_in[l], conv_w[l], conv_b[l], filt_w1[l], filt_b1[l],
                   filt_freq1[l], filt_w2[l], filt_b2[l], filt_freq2[l], filt_w3[l], filt_decay[l],
                   hy_skip[l], w_hy_o[l], w_attn_o[l], attn_sink[l], w_out[l], ln1_g[l], ln1_b[l],
                   router_group_w[l], router_group_b[l], router_expert_w[l], router_expert_b[l],
                   exp_w1[l], exp_w3[l], exp_w2[l], ln2_g[l], ln2_b[l])
    return x
```

```python
import functools
import math

import numpy as np
import jax
import jax.numpy as jnp
from jax import lax
from jax.experimental import pallas as pl
from jax.experimental.pallas import tpu as pltpu

F32 = jnp.float32
BF16 = jnp.bfloat16

N_HEADS = 8
N_KV_HEADS = 2
HEAD_DIM = 64
ATTN_WIDTH = N_HEADS * HEAD_DIM
KV_WIDTH = N_KV_HEADS * HEAD_DIM
WINDOW = 128
BLOCK_Q = 128
HYENA_WIDTH = 512
FILTER_EMB = 33
FILTER_BANDS = (FILTER_EMB - 1) // 2
WINDOW_SHIFT = 0.05
N_GROUPS = 8
EXPERTS_PER_GROUP = 8
N_EXPERTS = N_GROUPS * EXPERTS_PER_GROUP
D_EXPERT = 512
MOE_BLOCK = 256
LN_EPS = 1e-5
DEPTH = 1
DN_ALPHA = (2.0 * DEPTH) ** 0.25
NEG = -1e30

LANES = 128
ROUTE_OFF = N_GROUPS
VMEM_LIMIT = 56 * 1024 * 1024


def _params(n_axes, vmem=VMEM_LIMIT):
    return pltpu.CompilerParams(dimension_semantics=("arbitrary",) * n_axes, vmem_limit_bytes=vmem)


def _split(a):
    hi = a.astype(BF16)
    lo = (a - hi.astype(F32)).astype(BF16)
    return hi, lo


def _dot3(a, b_hi, b_lo):
    a_hi, a_lo = _split(a)
    acc = jnp.dot(a_hi, b_hi, preferred_element_type=F32)
    acc = acc + jnp.dot(a_hi, b_lo, preferred_element_type=F32)
    acc = acc + jnp.dot(a_lo, b_hi, preferred_element_type=F32)
    return acc


def _ldot3(w_hi, w_lo, a):
    a_hi, a_lo = _split(a)
    acc = jnp.dot(w_hi, a_hi, preferred_element_type=F32)
    acc = acc + jnp.dot(w_hi, a_lo, preferred_element_type=F32)
    acc = acc + jnp.dot(w_lo, a_hi, preferred_element_type=F32)
    return acc


def _sigmoid(x):
    return 1.0 / (1.0 + jnp.exp(-x))


def _layer_norm(r, g, b):
    mu = jnp.mean(r, axis=-1, keepdims=True)
    d = r - mu
    var = jnp.mean(d * d, axis=-1, keepdims=True)
    return d * lax.rsqrt(var + LN_EPS) * g + b


def _ada_kernel(c_ref, wh_ref, wl_ref, b_ref, o_ref):
    c = c_ref[...]
    s = c * _sigmoid(c)
    o_ref[...] = _dot3(s, wh_ref[...], wl_ref[...]) + b_ref[...]


def _ada(c, w_ada, b_ada):
    B, D = c.shape
    n_out = w_ada.shape[1]
    rows = 8
    cp = jnp.zeros((rows, D), F32).at[:B].set(c)
    wh, wl = _split(w_ada)
    tn = 1024
    out = pl.pallas_call(
        _ada_kernel,
        out_shape=jax.ShapeDtypeStruct((rows, n_out), F32),
        grid=(n_out // tn,),
        in_specs=[pl.BlockSpec((rows, D), lambda j: (0, 0)),
                  pl.BlockSpec((D, tn), lambda j: (0, j)),
                  pl.BlockSpec((D, tn), lambda j: (0, j)),
                  pl.BlockSpec((1, tn), lambda j: (0, j))],
        out_specs=pl.BlockSpec((rows, tn), lambda j: (0, j)),
        compiler_params=_params(1),
        name="ada",
    )(cp, wh, wl, b_ada.reshape(1, n_out))
    return out[:B].reshape(B, 6, D)


def _inproj_kernel(x_ref, mod_ref, w_ref, q_ref, k_ref, v_ref, hy_ref, ga_ref, gh_ref):
    C = HYENA_WIDTH
    x = x_ref[0]
    shift = mod_ref[0, 0:1, :]
    scale = mod_ref[0, 1:2, :]
    h = (x * (1.0 + scale) + shift).astype(BF16)

    def seg(lo, hi):
        return jnp.dot(h, w_ref[:, lo:hi], preferred_element_type=F32)

    o = 0
    q_ref[0] = (seg(o, o + ATTN_WIDTH) * (HEAD_DIM ** -0.5)).astype(BF16)
    o += ATTN_WIDTH
    k_ref[0] = seg(o, o + KV_WIDTH).astype(BF16)
    o += KV_WIDTH
    v_ref[0] = seg(o, o + KV_WIDTH).astype(BF16)
    o += KV_WIDTH
    hy_ref[0] = seg(o, o + 3 * C)
    o += 3 * C
    D = x.shape[-1]
    ga_ref[0] = _sigmoid(seg(o, o + D)).astype(BF16)
    o += D
    gh_ref[0] = _sigmoid(seg(o, o + D)).astype(BF16)


def _in_proj(x, mod, w_in):
    B, S, D = x.shape
    C = HYENA_WIDTH
    tm = min(512, S)
    wb = w_in.astype(BF16)
    nw = wb.shape[1]
    row = lambda b, i: (b, i, 0)
    shapes = [(ATTN_WIDTH, BF16), (KV_WIDTH, BF16), (KV_WIDTH, BF16), (3 * C, F32), (D, BF16), (D, BF16)]
    return pl.pallas_call(
        _inproj_kernel,
        out_shape=[jax.ShapeDtypeStruct((B, S, w), dt) for w, dt in shapes],
        grid=(B, S // tm),
        in_specs=[pl.BlockSpec((1, tm, D), row),
                  pl.BlockSpec((1, 6, D), lambda b, i: (b, 0, 0)),
                  pl.BlockSpec((D, nw), lambda b, i: (0, 0))],
        out_specs=[pl.BlockSpec((1, tm, w), row) for w, _ in shapes],
        compiler_params=_params(2),
        name="in_proj",
    )(x, mod, wb)


def _attn_kernel(sink_ref, q_ref, kp_ref, kc_ref, kn_ref, vp_ref, vc_ref, vn_ref, bias_ref, o_ref):
    i = pl.program_id(1)
    nb = pl.num_programs(1)
    Q = BLOCK_Q
    col = lax.broadcasted_iota(jnp.int32, (1, 3 * Q), 1)
    lo = jnp.where(i == 0, Q, 0)
    hi = jnp.where(i == nb - 1, 2 * Q, 3 * Q)
    colbias = jnp.where((col >= lo) & (col < hi), 0.0, NEG)
    kb = jnp.concatenate([kp_ref[0], kc_ref[0], kn_ref[0]], axis=0)
    vb = jnp.concatenate([vp_ref[0], vc_ref[0], vn_ref[0]], axis=0)
    G = N_HEADS // N_KV_HEADS
    outs = []
    for kv in range(N_KV_HEADS):
        kk = kb[:, kv * HEAD_DIM:(kv + 1) * HEAD_DIM]
        vv = vb[:, kv * HEAD_DIM:(kv + 1) * HEAD_DIM]
        for g in range(G):
            h = kv * G + g
            qh = q_ref[0, :, h * HEAD_DIM:(h + 1) * HEAD_DIM]
            s = lax.dot_general(qh, kk, (((1,), (1,)), ((), ())), preferred_element_type=F32)
            s = s + bias_ref[h] + colbias
            snk = sink_ref[h]
            m = jnp.maximum(jnp.max(s, axis=1, keepdims=True), snk)
            p = jnp.exp(s - m)
            den = jnp.sum(p, axis=1, keepdims=True) + jnp.exp(snk - m)
            o = jnp.dot(p.astype(BF16), vv, preferred_element_type=F32)
            outs.append(o / den)
    o_ref[0] = jnp.concatenate(outs, axis=1).astype(BF16)


def _attention(q, k, v, sink):
    B, S, _ = q.shape
    Q = BLOCK_Q
    nb = S // Q
    a = jnp.arange(Q)[:, None]
    j = jnp.arange(3 * Q)[None, :]
    rel = jnp.abs(j - Q - a).astype(F32)
    slopes = 2.0 ** (-8.0 * jnp.arange(1, N_HEADS + 1, dtype=F32) / N_HEADS)
    bias = jnp.where(rel[None] <= WINDOW, -slopes[:, None, None] * rel[None], NEG).astype(F32)
    prev = lambda b, i: (b, jnp.maximum(i - 1, 0), 0)
    cur = lambda b, i: (b, i, 0)
    nxt = lambda b, i: (b, jnp.minimum(i + 1, nb - 1), 0)
    kvspec = lambda f: pl.BlockSpec((1, Q, KV_WIDTH), f)
    return pl.pallas_call(
        _attn_kernel,
        out_shape=jax.ShapeDtypeStruct((B, S, ATTN_WIDTH), BF16),
        grid=(B, nb),
        in_specs=[pl.BlockSpec(memory_space=pltpu.SMEM),
                  pl.BlockSpec((1, Q, ATTN_WIDTH), cur),
                  kvspec(prev), kvspec(cur), kvspec(nxt),
                  kvspec(prev), kvspec(cur), kvspec(nxt),
                  pl.BlockSpec((N_HEADS, Q, 3 * Q), lambda b, i: (0, 0, 0))],
        out_specs=pl.BlockSpec((1, Q, ATTN_WIDTH), cur),
        compiler_params=_params(2),
        name="attn",
    )(sink.astype(F32), q, k, k, k, v, v, v, bias)


def _shortconv_kernel(u_ref, p_ref, n_ref, w_ref, b_ref, v_ref, x1_ref, x2_ref):
    i = pl.program_id(1)
    n = pl.num_programs(1)
    C = HYENA_WIDTH
    u = u_ref[0]
    tr = u.shape[0]
    prow = jnp.where(i > 0, p_ref[0, 7:8, :], 0.0)
    nrow = jnp.where(i < n - 1, n_ref[0, 0:1, :], 0.0)
    rid = lax.broadcasted_iota(jnp.int32, (tr, 1), 0)
    up = jnp.where(rid == 0, prow, pltpu.roll(u, 1, 0))
    dn = jnp.where(rid == tr - 1, nrow, pltpu.roll(u, tr - 1, 0))
    out = w_ref[0:1, :] * up + w_ref[1:2, :] * u + w_ref[2:3, :] * dn + b_ref[...]
    v_ref[0] = out[:, :C]
    x1_ref[0] = out[:, C:2 * C]
    x2_ref[0] = out[:, 2 * C:]


def _short_conv(hy_u, conv_w, conv_b):
    B, S, W = hy_u.shape
    C = HYENA_WIDTH
    tr = min(512, S)
    r8 = tr // 8
    nb8 = S // 8
    return pl.pallas_call(
        _shortconv_kernel,
        out_shape=[jax.ShapeDtypeStruct((B, S, C), F32)] * 3,
        grid=(B, S // tr),
        in_specs=[pl.BlockSpec((1, tr, W), lambda b, i: (b, i, 0)),
                  pl.BlockSpec((1, 8, W), lambda b, i: (b, jnp.maximum(i * r8 - 1, 0), 0)),
                  pl.BlockSpec((1, 8, W), lambda b, i: (b, jnp.minimum((i + 1) * r8, nb8 - 1), 0)),
                  pl.BlockSpec((3, W), lambda b, i: (0, 0)),
                  pl.BlockSpec((1, W), lambda b, i: (0, 0))],
        out_specs=[pl.BlockSpec((1, tr, C), lambda b, i: (b, i, 0))] * 3,
        compiler_params=_params(2),
        name="shortconv",
    )(hy_u, hy_u, hy_u, conv_w, conv_b.reshape(1, W))


def _filter_kernel(z_ref, w1h, w1l, b1_ref, f1_ref, w2h, w2l, b2_ref, f2_ref, w3h, w3l, dec_ref,
                   k_ref, s_ref):
    i = pl.program_id(0)
    z = z_ref[...]
    h1 = jnp.sin(f1_ref[...] * (_dot3(z, w1h[...], w1l[...]) + b1_ref[...]))
    h2 = jnp.sin(f2_ref[...] * (_dot3(h1, w2h[...], w2l[...]) + b2_ref[...]))
    k = _dot3(h2, w3h[...], w3l[...])
    t = z[:, 0:1]
    k = k * (jnp.exp(-t * jnp.abs(dec_ref[...])) + WINDOW_SHIFT)
    k_ref[...] = k

    @pl.when(i == 0)
    def _():
        s_ref[...] = jnp.zeros_like(s_ref)

    s_ref[...] += jnp.sum(jnp.abs(k), axis=0, keepdims=True)


def _filter_embedding(L):
    t = np.linspace(0.0, 1.0, L, dtype=np.float32).astype(np.float64)[:, None]
    w = (2.0 * math.pi * np.arange(L, dtype=np.float32) / np.float32(L)).astype(np.float64)[:, None]
    bands = np.linspace(1e-4, FILTER_BANDS - 1, FILTER_BANDS, dtype=np.float32).astype(np.float64)[None, :]
    bw = (bands.astype(np.float32) * w.astype(np.float32)).astype(np.float64)
    z = np.concatenate([t, np.cos(bw), -np.sin(bw)], axis=-1)
    zp = np.zeros((L, LANES), np.float32)
    zp[:, :FILTER_EMB] = z.astype(np.float32)
    return jnp.asarray(zp)


def _pad2(a, r, c):
    return jnp.zeros((r, c), F32).at[:a.shape[0], :a.shape[1]].set(a.astype(F32))


def _filters(L, fw1, fb1, ff1, fw2, fb2, ff2, fw3, decay):
    H = LANES
    nf = fw3.shape[1]
    z = _filter_embedding(L)
    w1h, w1l = _split(_pad2(fw1, H, H))
    w2h, w2l = _split(_pad2(fw2, H, H))
    w3h, w3l = _split(_pad2(fw3, H, nf))
    b1 = _pad2(fb1[None], 1, H)
    f1 = _pad2(ff1[None], 1, H)
    b2 = _pad2(fb2[None], 1, H)
    f2 = _pad2(ff2[None], 1, H)
    tr = min(512, L)
    full = lambda r, c: pl.BlockSpec((r, c), lambda i: (0, 0))
    return pl.pallas_call(
        _filter_kernel,
        out_shape=[jax.ShapeDtypeStruct((L, nf), F32), jax.ShapeDtypeStruct((1, nf), F32)],
        grid=(L // tr,),
        in_specs=[pl.BlockSpec((tr, H), lambda i: (i, 0)),
                  full(H, H), full(H, H), full(1, H), full(1, H),
                  full(H, H), full(H, H), full(1, H), full(1, H),
                  full(H, nf), full(H, nf), full(1, nf)],
        out_specs=[pl.BlockSpec((tr, nf), lambda i: (i, 0)), full(1, nf)],
        compiler_params=_params(1),
        name="filter",
    )(z, w1h, w1l, b1, f1, w2h, w2l, b2, f2, w3h, w3l, decay.reshape(1, nf).astype(F32))


def _np_split(m64):
    hi = m64.astype(np.float32).astype(BF16)
    lo = (m64 - hi.astype(np.float64)).astype(np.float32).astype(BF16)
    return jnp.asarray(hi), jnp.asarray(lo)


def _dft_constants(L):
    N = 2 * L
    n2 = LANES
    n1 = N // n2
    h1 = n1 // 2
    k1 = np.arange(n1)[:, None]
    s1 = np.arange(h1)[None, :]
    ang = -2.0 * np.pi * ((k1 * s1) % n1) / n1
    wr, wi = np.cos(ang), np.sin(ang)
    w1_real = np.concatenate([wr, wi], axis=0)
    w1_cplx = np.block([[wr, -wi], [wi, wr]])
    vr, vi = wr.T / N, -wi.T / N
    w3 = np.block([[vr, -vi], [vi, vr]])
    k2 = np.arange(n2)[:, None]
    s2 = np.arange(n2)[None, :]
    a2 = -2.0 * np.pi * ((k2 * s2) % n2) / n2
    w2r, w2i = jnp.asarray(np.cos(a2), F32), jnp.asarray(np.sin(a2), F32)
    at = -2.0 * np.pi * ((np.arange(n1)[:, None] * s2) % N) / N
    twr, twi = jnp.asarray(np.cos(at), F32), jnp.asarray(np.sin(at), F32)
    mr = w2r[None] * twr[:, None, :] - w2i[None] * twi[:, None, :]
    mi = w2r[None] * twi[:, None, :] + w2i[None] * twr[:, None, :]
    fwd = jnp.concatenate([jnp.concatenate([mr, -mi], axis=2),
                           jnp.concatenate([mi, mr], axis=2)], axis=1)
    inv = jnp.swapaxes(fwd, 1, 2)
    return dict(n1=n1, w1_real=_np_split(w1_real), w1_cplx=_np_split(w1_cplx), w3=_np_split(w3),
                fwd=_split(fwd), inv=_split(inv))


SCH = 8


def _dft1_kernel(x_ref, wh_ref, wl_ref, a_ref, *, n1, packed):
    wh, wl = wh_ref[...], wl_ref[...]
    for j in range(SCH):
        if packed:
            rhs = jnp.concatenate([x_ref[0, 0, :, j, :], x_ref[0, 1, :, j, :]], axis=0)
        else:
            rhs = x_ref[:, j, :]
        res = _ldot3(wh, wl, rhs)
        a_ref[0, :, 0, j, :] = res[:n1]
        a_ref[0, :, 1, j, :] = res[n1:]


def _dft1_data(x, consts):
    B, L, C = x.shape
    n1 = consts["n1"]
    h1 = n1 // 2
    xv = x.reshape(B // 2, 2, h1, LANES, C)
    wh, wl = consts["w1_cplx"]
    return pl.pallas_call(
        functools.partial(_dft1_kernel, n1=n1, packed=True),
        out_shape=jax.ShapeDtypeStruct((B // 2, n1, 2, LANES, C), F32),
        grid=(B // 2, LANES // SCH),
        in_specs=[pl.BlockSpec((1, 2, h1, SCH, C), lambda p, j: (p, 0, 0, j, 0)),
                  pl.BlockSpec((2 * n1, n1), lambda p, j: (0, 0)),
                  pl.BlockSpec((2 * n1, n1), lambda p, j: (0, 0))],
        out_specs=pl.BlockSpec((1, n1, 2, SCH, C), lambda p, j: (p, 0, 0, j, 0)),
        compiler_params=_params(2),
        name="dft1",
    )(xv, wh, wl)


def _dft1_filter(kraw, consts):
    L, nf = kraw.shape
    C = HYENA_WIDTH
    nq = nf // C
    n1 = consts["n1"]
    h1 = n1 // 2
    kv = kraw.reshape(h1, LANES, nf)
    wh, wl = consts["w1_real"]
    return pl.pallas_call(
        functools.partial(_dft1_kernel, n1=n1, packed=False),
        out_shape=jax.ShapeDtypeStruct((nq, n1, 2, LANES, C), F32),
        grid=(nq, LANES // SCH),
        in_specs=[pl.BlockSpec((h1, SCH, C), lambda q, j: (0, j, q)),
                  pl.BlockSpec((2 * n1, h1), lambda q, j: (0, 0)),
                  pl.BlockSpec((2 * n1, h1), lambda q, j: (0, 0))],
        out_specs=pl.BlockSpec((1, n1, 2, SCH, C), lambda q, j: (q, 0, 0, j, 0)),
        compiler_params=_params(2),
        name="dft1f",
    )(kv, wh, wl)


KCH = 4


def _midf_kernel(af_ref, ab_ref, fh_ref, fl_ref, inv_ref, b0_ref, h_ref):
    n2 = LANES
    for k in range(KCH):
        xf = _ldot3(fh_ref[k], fl_ref[k], af_ref[0, k])
        xb = _ldot3(fh_ref[k], fl_ref[k], ab_ref[0, k])
        sc = inv_ref[0]
        h_ref[0, k, :n2, :] = (xf[:n2] + xb[:n2] - b0_ref[0]) * sc
        h_ref[0, k, n2:, :] = (xf[n2:] - xb[n2:]) * sc


def _filter_spectrum(af, inv_den, bwd0, consts):
    nq, n1, _, n2, C = af.shape
    a = af.reshape(nq, n1, 2 * n2, C)
    fh, fl = consts["fwd"]
    blk = lambda f: pl.BlockSpec((1, KCH, 2 * n2, C), f)
    tab = pl.BlockSpec((KCH, 2 * n2, 2 * n2), lambda k, o: (k, 0, 0))
    vec = pl.BlockSpec((1, 1, C), lambda k, o: (o, 0, 0))
    return pl.pallas_call(
        _midf_kernel,
        out_shape=jax.ShapeDtypeStruct((nq // 2, n1, 2 * n2, C), F32),
        grid=(n1 // KCH, nq // 2),
        in_specs=[blk(lambda k, o: (2 * o, k, 0, 0)), blk(lambda k, o: (2 * o + 1, k, 0, 0)),
                  tab, tab, vec, vec],
        out_specs=blk(lambda k, o: (o, k, 0, 0)),
        compiler_params=_params(2),
        name="midf",
    )(a, a, fh, fl, inv_den, bwd0)


def _mid_kernel(a_ref, fh_ref, fl_ref, ih_ref, il_ref, h_ref, b_ref):
    n2 = LANES
    for k in range(KCH):
        x = _ldot3(fh_ref[k], fl_ref[k], a_ref[0, k])
        xr, xi = x[:n2], x[n2:]
        hr, hi = h_ref[0, k, :n2, :], h_ref[0, k, n2:, :]
        y = jnp.concatenate([xr * hr - xi * hi, xr * hi + xi * hr], axis=0)
        b_ref[0, k] = _ldot3(ih_ref[k], il_ref[k], y)


def _mid(a5, hspec, order, consts):
    P, n1, _, n2, C = a5.shape
    a = a5.reshape(P, n1, 2 * n2, C)
    fh, fl = consts["fwd"]
    ih, il = consts["inv"]
    tab = pl.BlockSpec((KCH, 2 * n2, 2 * n2), lambda k, p: (k, 0, 0))
    out = pl.pallas_call(
        _mid_kernel,
        out_shape=jax.ShapeDtypeStruct((P, n1, 2 * n2, C), F32),
        grid=(n1 // KCH, P),
        in_specs=[pl.BlockSpec((1, KCH, 2 * n2, C), lambda k, p: (p, k, 0, 0)),
                  tab, tab, tab, tab,
                  pl.BlockSpec((1, KCH, 2 * n2, C), lambda k, p: (order, k, 0, 0))],
        out_specs=pl.BlockSpec((1, KCH, 2 * n2, C), lambda k, p: (p, k, 0, 0)),
        compiler_params=_params(2),
        name="mid",
    )(a, fh, fl, ih, il, hspec)
    return out.reshape(P, n1, 2, n2, C)


def _dft3_kernel(b_ref, wh_ref, wl_ref, v_ref, g_ref, skip_ref, z_ref, *, h1):
    wh, wl = wh_ref[...], wl_ref[...]
    skip = skip_ref[0]
    for j in range(SCH):
        rhs = jnp.concatenate([b_ref[0, :, 0, j, :], b_ref[0, :, 1, j, :]], axis=0)
        y = _ldot3(wh, wl, rhs)
        for r in range(2):
            yr = y[r * h1:(r + 1) * h1]
            z_ref[0, r, :, j, :] = g_ref[0, r, :, j, :] * (yr + v_ref[0, r, :, j, :] * skip)


def _dft3_gate(b5, v, gate, skip, consts):
    P, n1, _, n2, C = b5.shape
    h1 = n1 // 2
    B, L, _ = v.shape
    wh, wl = consts["w3"]
    five = lambda t: t.reshape(P, 2, h1, n2, C)
    dat = pl.BlockSpec((1, 2, h1, SCH, C), lambda p, j: (p, 0, 0, j, 0))
    out = pl.pallas_call(
        functools.partial(_dft3_kernel, h1=h1),
        out_shape=jax.ShapeDtypeStruct((P, 2, h1, n2, C), F32),
        grid=(P, n2 // SCH),
        in_specs=[pl.BlockSpec((1, n1, 2, SCH, C), lambda p, j: (p, 0, 0, j, 0)),
                  pl.BlockSpec((n1, 2 * n1), lambda p, j: (0, 0)),
                  pl.BlockSpec((n1, 2 * n1), lambda p, j: (0, 0)),
                  dat, dat,
                  pl.BlockSpec((1, C), lambda p, j: (0, 0))],
        out_specs=dat,
        compiler_params=_params(2),
        name="dft3",
    )(b5, wh, wl, five(v), five(gate), skip.reshape(1, C).astype(F32))
    return out.reshape(B, L, C)


def _hyena(hy_u, conv_w, conv_b, fw1, fb1, ff1, fw2, fb2, ff2, fw3, decay, skip):
    B, L, _ = hy_u.shape
    C = HYENA_WIDTH
    consts = _dft_constants(L)
    v, x1, x2 = _short_conv(hy_u, conv_w, conv_b)
    kraw, ksum = _filters(L, fw1, fb1, ff1, fw2, fb2, ff2, fw3, decay)
    ks = ksum.reshape(2, 2, C)
    inv_den = (1.0 / (ks[:, 0] + ks[:, 1])).reshape(2, 1, C)
    bwd0 = kraw[0].reshape(2, 2, C)[:, 1].reshape(2, 1, C)
    hspec = _filter_spectrum(_dft1_filter(kraw, consts), inv_den, bwd0, consts)
    z = v
    for o, gate in enumerate((x1, x2)):
        a5 = _dft1_data(z, consts)
        b5 = _mid(a5, hspec, o, consts)
        z = _dft3_gate(b5, z, gate, skip[o], consts)
    return z


def _merge_kernel(attn_ref, hy_ref, ga_ref, gh_ref, x_ref, mod_ref, wa_ref, wh_ref, wo_ref,
                  g1_ref, b1_ref, rwh_ref, rwl_ref, rb_ref, tri_ref,
                  x1_ref, h2_ref, route_ref, wts_ref, cnt_ref, carry_ref):
    first = (pl.program_id(0) == 0) & (pl.program_id(1) == 0)

    @pl.when(first)
    def _():
        carry_ref[...] = jnp.zeros_like(carry_ref)

    a = jnp.dot(attn_ref[0], wa_ref[...], preferred_element_type=F32)
    hy = jnp.dot(hy_ref[0].astype(BF16), wh_ref[...], preferred_element_type=F32)
    merged = ga_ref[0].astype(F32) * a + gh_ref[0].astype(F32) * hy
    y = jnp.dot(merged.astype(BF16), wo_ref[...], preferred_element_type=F32)
    gate1 = mod_ref[0, 2:3, :]
    shift2 = mod_ref[0, 3:4, :]
    scale2 = mod_ref[0, 4:5, :]
    x1 = _layer_norm(DN_ALPHA * x_ref[0] + gate1 * y, g1_ref[...], b1_ref[...])
    x1_ref[0] = x1
    h2 = x1 * (1.0 + scale2) + shift2
    h2_ref[0] = h2

    logits = _dot3(h2, rwh_ref[...], rwl_ref[...]) + rb_ref[...]
    tm = logits.shape[0]
    lane = lax.broadcasted_iota(jnp.int32, (tm, LANES), 1)
    lanef = lane.astype(F32)
    big = float(LANES)

    def first_lane(mask):
        return jnp.min(jnp.where(mask, lanef, big), axis=1, keepdims=True).astype(jnp.int32)

    gmask = lane < N_GROUPS
    gl = jnp.where(gmask, logits, NEG)
    gmax = jnp.max(gl, axis=1, keepdims=True)
    gidx = first_lane(gl == gmax)
    pg = 1.0 / jnp.sum(jnp.exp(gl - gmax), axis=1, keepdims=True)
    lo = ROUTE_OFF + gidx * EXPERTS_PER_GROUP
    emask = (lane >= lo) & (lane < lo + EXPERTS_PER_GROUP)
    el = jnp.where(emask, logits, NEG)
    v1 = jnp.max(el, axis=1, keepdims=True)
    i1 = first_lane(el == v1)
    el2 = jnp.where(emask & (lane != i1), logits, NEG)
    v2 = jnp.max(el2, axis=1, keepdims=True)
    i2 = first_lane(el2 == v2)
    e21 = jnp.exp(v2 - v1)
    w1 = pg / (1.0 + e21)
    w2 = pg * e21 / (1.0 + e21)

    sel1 = lane == i1
    sel2 = lane == i2
    onehot = jnp.where(sel1 | sel2, 1.0, 0.0)
    prefix = jnp.dot(tri_ref[...], onehot.astype(BF16), preferred_element_type=F32) + carry_ref[...]
    r1 = jnp.sum(jnp.where(sel1, prefix, 0.0), axis=1, keepdims=True)
    r2 = jnp.sum(jnp.where(sel2, prefix, 0.0), axis=1, keepdims=True)
    carry_ref[...] += jnp.sum(onehot, axis=0, keepdims=True)
    cnt_ref[...] = carry_ref[...]

    ranks = jnp.where(lane == 2, r1, jnp.where(lane == 3, r2, 0.0)).astype(jnp.int32)
    route_ref[0] = jnp.where(lane == 0, i1 - ROUTE_OFF, jnp.where(lane == 1, i2 - ROUTE_OFF, ranks))
    wts_ref[0] = jnp.where(lane == 0, w1, jnp.where(lane == 1, w2, 0.0))


def _merge(attn, hy, ga, gh, x, mod, w_attn_o, w_hy_o, w_out, ln1_g, ln1_b, rg_w, rg_b, re_w, re_b):
    B, S, D = x.shape
    tm = min(512, S)
    rw = jnp.zeros((D, LANES), F32).at[:, :N_GROUPS].set(rg_w).at[:, ROUTE_OFF:ROUTE_OFF + N_EXPERTS].set(re_w)
    rb = jnp.zeros((1, LANES), F32).at[0, :N_GROUPS].set(rg_b).at[0, ROUTE_OFF:ROUTE_OFF + N_EXPERTS].set(re_b)
    rwh, rwl = _split(rw)
    tri = (jnp.arange(tm)[:, None] > jnp.arange(tm)[None, :]).astype(BF16)
    row = lambda b, i: (b, i, 0)
    full = lambda r, c: pl.BlockSpec((r, c), lambda b, i: (0, 0))
    outs = [jax.ShapeDtypeStruct((B, S, D), F32), jax.ShapeDtypeStruct((B, S, D), F32),
            jax.ShapeDtypeStruct((B, S, LANES), jnp.int32), jax.ShapeDtypeStruct((B, S, LANES), F32),
            jax.ShapeDtypeStruct((1, LANES), F32)]
    return pl.pallas_call(
        _merge_kernel,
        out_shape=outs,
        grid=(B, S // tm),
        in_specs=[pl.BlockSpec((1, tm, ATTN_WIDTH), row), pl.BlockSpec((1, tm, HYENA_WIDTH), row),
                  pl.BlockSpec((1, tm, D), row), pl.BlockSpec((1, tm, D), row), pl.BlockSpec((1, tm, D), row),
                  pl.BlockSpec((1, 6, D), lambda b, i: (b, 0, 0)),
                  full(ATTN_WIDTH, D), full(HYENA_WIDTH, D), full(D, D),
                  full(1, D), full(1, D), full(D, LANES), full(D, LANES), full(1, LANES), full(tm, tm)],
        out_specs=[pl.BlockSpec((1, tm, D), row), pl.BlockSpec((1, tm, D), row),
                   pl.BlockSpec((1, tm, LANES), row), pl.BlockSpec((1, tm, LANES), row), full(1, LANES)],
        scratch_shapes=[pltpu.VMEM((1, LANES), F32)],
        compiler_params=_params(2),
        name="merge",
    )(attn, hy, ga, gh, x, mod, w_attn_o.astype(BF16), w_hy_o.astype(BF16), w_out.astype(BF16),
      ln1_g.reshape(1, D), ln1_b.reshape(1, D), rwh, rwl, rb, tri)


TOK_CH = 256


def _dest(route_ref, poffs_ref, r, j):
    return poffs_ref[route_ref[0, 0, 4 * r + j]] + route_ref[0, 0, 4 * r + 2 + j]


def _dispatch_kernel(poffs_ref, psz_ref, route_ref, h_ref, xb_ref, zero_ref, sem, zsem):
    @pl.when(pl.program_id(0) == 0)
    def _():
        zero_ref[...] = jnp.zeros_like(zero_ref)

        def zcopy(e):
            start = pl.multiple_of(poffs_ref[e] + psz_ref[e] - MOE_BLOCK, MOE_BLOCK)
            return pltpu.make_async_copy(zero_ref, xb_ref.at[pl.ds(start, MOE_BLOCK)], zsem)

        def zissue(e, c):
            @pl.when(psz_ref[e] > 0)
            def _():
                zcopy(e).start()
            return c

        def zdrain(e, c):
            @pl.when(psz_ref[e] > 0)
            def _():
                zcopy(e).wait()
            return c

        lax.fori_loop(0, N_EXPERTS, zissue, 0)
        lax.fori_loop(0, N_EXPERTS, zdrain, 0)

        def tcopy(b):
            return pltpu.make_async_copy(zero_ref, xb_ref.at[pl.ds(pl.multiple_of(b * MOE_BLOCK, MOE_BLOCK), MOE_BLOCK)], zsem)

        first_free = (poffs_ref[N_EXPERTS - 1] + psz_ref[N_EXPERTS - 1]) // MOE_BLOCK
        n_blocks = xb_ref.shape[0] // MOE_BLOCK
        lax.fori_loop(first_free, n_blocks, lambda b, c: (tcopy(b).start(), c)[1], 0)
        lax.fori_loop(first_free, n_blocks, lambda b, c: (tcopy(b).wait(), c)[1], 0)

    def row_copy(r, d):
        return pltpu.make_async_copy(h_ref.at[pl.ds(r, 1)], xb_ref.at[pl.ds(d, 1)], sem)

    def issue(r, c):
        row_copy(r, _dest(route_ref, poffs_ref, r, 0)).start()
        row_copy(r, _dest(route_ref, poffs_ref, r, 1)).start()
        return c

    lax.fori_loop(0, TOK_CH, issue, 0)

    def drain(r, c):
        row_copy(0, 0).wait()
        return c

    lax.fori_loop(0, 2 * TOK_CH, drain, 0)


def _dispatch(h2, route, poffs, psizes, P):
    T, D = h2.shape
    nch = T // TOK_CH
    grid_spec = pltpu.PrefetchScalarGridSpec(
        num_scalar_prefetch=2,
        grid=(nch,),
        in_specs=[pl.BlockSpec((1, 1, 4 * TOK_CH), lambda i, po, ps: (i, 0, 0), memory_space=pltpu.SMEM),
                  pl.BlockSpec((TOK_CH, D), lambda i, po, ps: (i, 0))],
        out_specs=pl.BlockSpec(memory_space=pl.ANY),
        scratch_shapes=[pltpu.VMEM((MOE_BLOCK, D), F32), pltpu.SemaphoreType.DMA(()), pltpu.SemaphoreType.DMA(())],
    )
    return pl.pallas_call(
        _dispatch_kernel,
        out_shape=jax.ShapeDtypeStruct((P, D), F32),
        grid_spec=grid_spec,
        compiler_params=_params(1),
        name="dispatch",
    )(poffs, psizes, route, h2)


def _expert_kernel(be_ref, nu_ref, x_ref, w1_ref, w3_ref, w2_ref, y_ref, c1_ref, c3_ref, c2_ref):
    i = pl.program_id(0)
    used = i < nu_ref[0]
    fresh = (i == 0) | (be_ref[i] != be_ref[jnp.maximum(i - 1, 0)])

    @pl.when(used & fresh)
    def _():
        c1_ref[...] = w1_ref[0].astype(BF16)
        c3_ref[...] = w3_ref[0].astype(BF16)
        c2_ref[...] = w2_ref[0].astype(BF16)

    @pl.when(used)
    def _():
        x = x_ref[...].astype(BF16)
        a = jnp.dot(x, c1_ref[...], preferred_element_type=F32)
        g = jnp.dot(x, c3_ref[...], preferred_element_type=F32)
        hmid = (a * _sigmoid(a) * g).astype(BF16)
        y_ref[...] = jnp.dot(hmid, c2_ref[...], preferred_element_type=F32)

    @pl.when(jnp.logical_not(used))
    def _():
        y_ref[...] = jnp.zeros_like(y_ref)


def _experts(xb, blk_e, n_used, w1, w3, w2):
    P, D = xb.shape
    E, _, DE = w1.shape
    nblk = P // MOE_BLOCK
    grid_spec = pltpu.PrefetchScalarGridSpec(
        num_scalar_prefetch=2,
        grid=(nblk,),
        in_specs=[pl.BlockSpec((MOE_BLOCK, D), lambda i, be, nu: (jnp.minimum(i, jnp.maximum(nu[0] - 1, 0)), 0)),
                  pl.BlockSpec((1, D, DE), lambda i, be, nu: (be[i], 0, 0)),
                  pl.BlockSpec((1, D, DE), lambda i, be, nu: (be[i], 0, 0)),
                  pl.BlockSpec((1, DE, D), lambda i, be, nu: (be[i], 0, 0))],
        out_specs=pl.BlockSpec((MOE_BLOCK, D), lambda i, be, nu: (i, 0)),
        scratch_shapes=[pltpu.VMEM((D, DE), BF16), pltpu.VMEM((D, DE), BF16), pltpu.VMEM((DE, D), BF16)],
    )
    return pl.pallas_call(
        _expert_kernel,
        out_shape=jax.ShapeDtypeStruct((P, D), F32),
        grid_spec=grid_spec,
        compiler_params=_params(1),
        name="experts",
    )(blk_e, n_used, xb, w1, w3, w2)


def _combine_kernel(poffs_ref, route_ref, wts_ref, x1_ref, mod_ref, g_ref, b_ref, yb_ref, o_ref, rows_ref, sem):
    def row_copy(d, j, r):
        return pltpu.make_async_copy(yb_ref.at[pl.ds(d, 1)], rows_ref.at[j, pl.ds(r, 1)], sem)

    def issue(r, c):
        row_copy(_dest(route_ref, poffs_ref, r, 0), 0, r).start()
        row_copy(_dest(route_ref, poffs_ref, r, 1), 1, r).start()
        return c

    lax.fori_loop(0, TOK_CH, issue, 0)

    def drain(r, c):
        row_copy(0, 0, 0).wait()
        return c

    lax.fori_loop(0, 2 * TOK_CH, drain, 0)
    w = wts_ref[...]
    y = w[:, 0:1] * rows_ref[0] + w[:, 1:2] * rows_ref[1]
    gate2 = mod_ref[0, 5:6, :]
    o_ref[...] = _layer_norm(DN_ALPHA * x1_ref[...] + gate2 * y, g_ref[...], b_ref[...])


def _combine(yb, route, poffs, wts, x1, mod, ln2_g, ln2_b, S):
    T, D = x1.shape
    nch = T // TOK_CH
    per_b = S // TOK_CH
    grid_spec = pltpu.PrefetchScalarGridSpec(
        num_scalar_prefetch=1,
        grid=(nch,),
        in_specs=[pl.BlockSpec((1, 1, 4 * TOK_CH), lambda i, po: (i, 0, 0), memory_space=pltpu.SMEM),
                  pl.BlockSpec((TOK_CH, LANES), lambda i, po: (i, 0)),
                  pl.BlockSpec((TOK_CH, D), lambda i, po: (i, 0)),
                  pl.BlockSpec((1, 6, D), lambda i, po: (i // per_b, 0, 0)),
                  pl.BlockSpec((1, D), lambda i, po: (0, 0)),
                  pl.BlockSpec((1, D), lambda i, po: (0, 0)),
                  pl.BlockSpec(memory_space=pl.ANY)],
        out_specs=pl.BlockSpec((TOK_CH, D), lambda i, po: (i, 0)),
        scratch_shapes=[pltpu.VMEM((2, TOK_CH, D), F32), pltpu.SemaphoreType.DMA(())],
    )
    return pl.pallas_call(
        _combine_kernel,
        out_shape=jax.ShapeDtypeStruct((T, D), F32),
        grid_spec=grid_spec,
        compiler_params=_params(1),
        name="combine",
    )(poffs, route, wts, x1, mod, ln2_g.reshape(1, D), ln2_b.reshape(1, D), yb)


def _moe(h2, x1, route, wts, counts, mod, w1, w3, w2, ln2_g, ln2_b):
    B, S, D = h2.shape
    T = B * S
    P = 2 * T + N_EXPERTS * MOE_BLOCK
    nblk = P // MOE_BLOCK
    nch = T // TOK_CH
    sizes = counts[0, ROUTE_OFF:ROUTE_OFF + N_EXPERTS].astype(jnp.int32)
    psizes = (sizes + MOE_BLOCK - 1) // MOE_BLOCK * MOE_BLOCK
    pends = jnp.cumsum(psizes)
    poffs = pends - psizes
    starts = jnp.arange(nblk, dtype=jnp.int32) * MOE_BLOCK
    blk_e = jnp.minimum(jnp.sum((pends[None, :] <= starts[:, None]).astype(jnp.int32), axis=1), N_EXPERTS - 1)
    n_used = pends[-1:] // MOE_BLOCK
    route4 = route.reshape(T, LANES)[:, :4].reshape(nch, 1, 4 * TOK_CH)
    xb = _dispatch(h2.reshape(T, D), route4, poffs, psizes, P)
    yb = _experts(xb, blk_e, n_used, w1, w3, w2)
    out = _combine(yb, route4, poffs, wts.reshape(T, LANES), x1.reshape(T, D), mod, ln2_g, ln2_b, S)
    return out.reshape(B, S, D)


def _layer(x, c, w_ada, b_ada, w_in, conv_w, conv_b, fw1, fb1, ff1, fw2, fb2, ff2, fw3, decay, skip,
           w_hy_o, w_attn_o, attn_sink, w_out, ln1_g, ln1_b, rg_w, rg_b, re_w, re_b, ew1, ew3, ew2,
           ln2_g, ln2_b):
    mod = _ada(c, w_ada, b_ada)
    q, k, v, hy_u, ga, gh = _in_proj(x, mod, w_in)
    attn = _attention(q, k, v, attn_sink)
    hy = _hyena(hy_u, conv_w, conv_b, fw1, fb1, ff1, fw2, fb2, ff2, fw3, decay, skip)
    x1, h2, route, wts, counts = _merge(attn, hy, ga, gh, x, mod, w_attn_o, w_hy_o, w_out,
                                        ln1_g, ln1_b, rg_w, rg_b, re_w, re_b)
    return _moe(h2, x1, route, wts, counts, mod, ew1, ew3, ew2, ln2_g, ln2_b)


def kernel(x, c, w_ada, b_ada, w_in, conv_w, conv_b, filt_w1, filt_b1, filt_freq1, filt_w2, filt_b2, filt_freq2, filt_w3, filt_decay, hy_skip, w_hy_o, w_attn_o, attn_sink, w_out, ln1_g, ln1_b, router_group_w, router_group_b, router_expert_w, router_expert_b, exp_w1, exp_w3, exp_w2, ln2_g, ln2_b):
    for l in range(w_ada.shape[0]):
        x = _layer(x, c, w_ada[l], b_ada[l], w_in[l], conv_w[l], conv_b[l], filt_w1[l], filt_b1[l],
                   filt_freq1[l], filt_w2[l], filt_b2[l], filt_freq2[l], filt_w3[l], filt_decay[l],
                   hy_skip[l], w_hy_o[l], w_attn_o[l], attn_sink[l], w_out[l], ln1_g[l], ln1_b[l],
                   router_group_w[l], router_group_b[l], router_expert_w[l], router_expert_b[l],
                   exp_w1[l], exp_w3[l], exp_w2[l], ln2_g[l], ln2_b[l])
    return x
```

```python
import functools
import math

import numpy as np
import jax
import jax.numpy as jnp
from jax import lax
from jax.experimental import pallas as pl
from jax.experimental.pallas import tpu as pltpu

F32 = jnp.float32
BF16 = jnp.bfloat16

N_HEADS = 8
N_KV_HEADS = 2
HEAD_DIM = 64
ATTN_WIDTH = N_HEADS * HEAD_DIM
KV_WIDTH = N_KV_HEADS * HEAD_DIM
WINDOW = 128
BLOCK_Q = 128
HYENA_WIDTH = 512
FILTER_EMB = 33
FILTER_BANDS = (FILTER_EMB - 1) // 2
WINDOW_SHIFT = 0.05
N_GROUPS = 8
EXPERTS_PER_GROUP = 8
N_EXPERTS = N_GROUPS * EXPERTS_PER_GROUP
D_EXPERT = 512
MOE_BLOCK = 256
LN_EPS = 1e-5
DEPTH = 1
DN_ALPHA = (2.0 * DEPTH) ** 0.25
NEG = -1e30

LANES = 128
ROUTE_OFF = N_GROUPS
VMEM_LIMIT = 56 * 1024 * 1024


def _params(n_axes, vmem=VMEM_LIMIT):
    return pltpu.CompilerParams(dimension_semantics=("arbitrary",) * n_axes, vmem_limit_bytes=vmem)


def _split(a):
    hi = a.astype(BF16)
    lo = (a - hi.astype(F32)).astype(BF16)
    return hi, lo


def _dot3(a, b_hi, b_lo):
    a_hi, a_lo = _split(a)
    acc = jnp.dot(a_hi, b_hi, preferred_element_type=F32)
    acc = acc + jnp.dot(a_hi, b_lo, preferred_element_type=F32)
    acc = acc + jnp.dot(a_lo, b_hi, preferred_element_type=F32)
    return acc


def _ldot3(w_hi, w_lo, a):
    a_hi, a_lo = _split(a)
    acc = jnp.dot(w_hi, a_hi, preferred_element_type=F32)
    acc = acc + jnp.dot(w_hi, a_lo, preferred_element_type=F32)
    acc = acc + jnp.dot(w_lo, a_hi, preferred_element_type=F32)
    return acc


def _sigmoid(x):
    return 1.0 / (1.0 + jnp.exp(-x))


def _layer_norm(r, g, b):
    mu = jnp.mean(r, axis=-1, keepdims=True)
    d = r - mu
    var = jnp.mean(d * d, axis=-1, keepdims=True)
    return d * lax.rsqrt(var + LN_EPS) * g + b


def _ada_kernel(c_ref, wh_ref, wl_ref, b_ref, o_ref):
    c = c_ref[...]
    s = c * _sigmoid(c)
    o_ref[...] = _dot3(s, wh_ref[...], wl_ref[...]) + b_ref[...]


def _ada(c, w_ada, b_ada):
    B, D = c.shape
    n_out = w_ada.shape[1]
    rows = 8
    cp = jnp.zeros((rows, D), F32).at[:B].set(c)
    wh, wl = _split(w_ada)
    tn = 1024
    out = pl.pallas_call(
        _ada_kernel,
        out_shape=jax.ShapeDtypeStruct((rows, n_out), F32),
        grid=(n_out // tn,),
        in_specs=[pl.BlockSpec((rows, D), lambda j: (0, 0)),
                  pl.BlockSpec((D, tn), lambda j: (0, j)),
                  pl.BlockSpec((D, tn), lambda j: (0, j)),
                  pl.BlockSpec((1, tn), lambda j: (0, j))],
        out_specs=pl.BlockSpec((rows, tn), lambda j: (0, j)),
        compiler_params=_params(1),
        name="ada",
    )(cp, wh, wl, b_ada.reshape(1, n_out))
    return out[:B].reshape(B, 6, D)


def _inproj_kernel(x_ref, mod_ref, w_ref, q_ref, k_ref, v_ref, hy_ref, ga_ref, gh_ref):
    C = HYENA_WIDTH
    x = x_ref[0]
    shift = mod_ref[0, 0:1, :]
    scale = mod_ref[0, 1:2, :]
    h = (x * (1.0 + scale) + shift).astype(BF16)

    def seg(lo, hi):
        return jnp.dot(h, w_ref[:, lo:hi], preferred_element_type=F32)

    o = 0
    q_ref[0] = (seg(o, o + ATTN_WIDTH) * (HEAD_DIM ** -0.5)).astype(BF16)
    o += ATTN_WIDTH
    k_ref[0] = seg(o, o + KV_WIDTH).astype(BF16)
    o += KV_WIDTH
    v_ref[0] = seg(o, o + KV_WIDTH).astype(BF16)
    o += KV_WIDTH
    hy_ref[0] = seg(o, o + 3 * C)
    o += 3 * C
    D = x.shape[-1]
    ga_ref[0] = _sigmoid(seg(o, o + D)).astype(BF16)
    o += D
    gh_ref[0] = _sigmoid(seg(o, o + D)).astype(BF16)


def _in_proj(x, mod, w_in):
    B, S, D = x.shape
    C = HYENA_WIDTH
    tm = min(512, S)
    wb = w_in.astype(BF16)
    nw = wb.shape[1]
    row = lambda b, i: (b, i, 0)
    shapes = [(ATTN_WIDTH, BF16), (KV_WIDTH, BF16), (KV_WIDTH, BF16), (3 * C, F32), (D, BF16), (D, BF16)]
    return pl.pallas_call(
        _inproj_kernel,
        out_shape=[jax.ShapeDtypeStruct((B, S, w), dt) for w, dt in shapes],
        grid=(B, S // tm),
        in_specs=[pl.BlockSpec((1, tm, D), row),
                  pl.BlockSpec((1, 6, D), lambda b, i: (b, 0, 0)),
                  pl.BlockSpec((D, nw), lambda b, i: (0, 0))],
        out_specs=[pl.BlockSpec((1, tm, w), row) for w, _ in shapes],
        compiler_params=_params(2),
        name="in_proj",
    )(x, mod, wb)


def _attn_kernel(sink_ref, q_ref, kp_ref, kc_ref, kn_ref, vp_ref, vc_ref, vn_ref, bias_ref, o_ref):
    i = pl.program_id(1)
    nb = pl.num_programs(1)
    Q = BLOCK_Q
    col = lax.broadcasted_iota(jnp.int32, (1, 3 * Q), 1)
    lo = jnp.where(i == 0, Q, 0)
    hi = jnp.where(i == nb - 1, 2 * Q, 3 * Q)
    colbias = jnp.where((col >= lo) & (col < hi), 0.0, NEG)
    kb = jnp.concatenate([kp_ref[0], kc_ref[0], kn_ref[0]], axis=0)
    vb = jnp.concatenate([vp_ref[0], vc_ref[0], vn_ref[0]], axis=0)
    G = N_HEADS // N_KV_HEADS
    outs = []
    for kv in range(N_KV_HEADS):
        kk = kb[:, kv * HEAD_DIM:(kv + 1) * HEAD_DIM]
        vv = vb[:, kv * HEAD_DIM:(kv + 1) * HEAD_DIM]
        for g in range(G):
            h = kv * G + g
            qh = q_ref[0, :, h * HEAD_DIM:(h + 1) * HEAD_DIM]
            s = lax.dot_general(qh, kk, (((1,), (1,)), ((), ())), preferred_element_type=F32)
            s = s + bias_ref[h] + colbias
            snk = sink_ref[h]
            m = jnp.maximum(jnp.max(s, axis=1, keepdims=True), snk)
            p = jnp.exp(s - m)
            den = jnp.sum(p, axis=1, keepdims=True) + jnp.exp(snk - m)
            o = jnp.dot(p.astype(BF16), vv, preferred_element_type=F32)
            outs.append(o / den)
    o_ref[0] = jnp.concatenate(outs, axis=1).astype(BF16)


def _attention(q, k, v, sink):
    B, S, _ = q.shape
    Q = BLOCK_Q
    nb = S // Q
    a = jnp.arange(Q)[:, None]
    j = jnp.arange(3 * Q)[None, :]
    rel = jnp.abs(j - Q - a).astype(F32)
    slopes = 2.0 ** (-8.0 * jnp.arange(1, N_HEADS + 1, dtype=F32) / N_HEADS)
    bias = jnp.where(rel[None] <= WINDOW, -slopes[:, None, None] * rel[None], NEG).astype(F32)
    prev = lambda b, i: (b, jnp.maximum(i - 1, 0), 0)
    cur = lambda b, i: (b, i, 0)
    nxt = lambda b, i: (b, jnp.minimum(i + 1, nb - 1), 0)
    kvspec = lambda f: pl.BlockSpec((1, Q, KV_WIDTH), f)
    return pl.pallas_call(
        _attn_kernel,
        out_shape=jax.ShapeDtypeStruct((B, S, ATTN_WIDTH), BF16),
        grid=(B, nb),
        in_specs=[pl.BlockSpec(memory_space=pltpu.SMEM),
                  pl.BlockSpec((1, Q, ATTN_WIDTH), cur),
                  kvspec(prev), kvspec(cur), kvspec(nxt),
                  kvspec(prev), kvspec(cur), kvspec(nxt),
                  pl.BlockSpec((N_HEADS, Q, 3 * Q), lambda b, i: (0, 0, 0))],
        out_specs=pl.BlockSpec((1, Q, ATTN_WIDTH), cur),
        compiler_params=_params(2),
        name="attn",
    )(sink.astype(F32), q, k, k, k, v, v, v, bias)


def _shortconv_kernel(u_ref, p_ref, n_ref, w_ref, b_ref, v_ref, x1_ref, x2_ref):
    i = pl.program_id(1)
    n = pl.num_programs(1)
    C = HYENA_WIDTH
    u = u_ref[0]
    tr = u.shape[0]
    prow = jnp.where(i > 0, p_ref[0, 7:8, :], 0.0)
    nrow = jnp.where(i < n - 1, n_ref[0, 0:1, :], 0.0)
    rid = lax.broadcasted_iota(jnp.int32, (tr, 1), 0)
    up = jnp.where(rid == 0, prow, pltpu.roll(u, 1, 0))
    dn = jnp.where(rid == tr - 1, nrow, pltpu.roll(u, tr - 1, 0))
    out = w_ref[0:1, :] * up + w_ref[1:2, :] * u + w_ref[2:3, :] * dn + b_ref[...]
    v_ref[0] = out[:, :C]
    x1_ref[0] = out[:, C:2 * C]
    x2_ref[0] = out[:, 2 * C:]


def _short_conv(hy_u, conv_w, conv_b):
    B, S, W = hy_u.shape
    C = HYENA_WIDTH
    tr = min(512, S)
    r8 = tr // 8
    nb8 = S // 8
    return pl.pallas_call(
        _shortconv_kernel,
        out_shape=[jax.ShapeDtypeStruct((B, S, C), F32)] * 3,
        grid=(B, S // tr),
        in_specs=[pl.BlockSpec((1, tr, W), lambda b, i: (b, i, 0)),
                  pl.BlockSpec((1, 8, W), lambda b, i: (b, jnp.maximum(i * r8 - 1, 0), 0)),
                  pl.BlockSpec((1, 8, W), lambda b, i: (b, jnp.minimum((i + 1) * r8, nb8 - 1), 0)),
                  pl.BlockSpec((3, W), lambda b, i: (0, 0)),
                  pl.BlockSpec((1, W), lambda b, i: (0, 0))],
        out_specs=[pl.BlockSpec((1, tr, C), lambda b, i: (b, i, 0))] * 3,
        compiler_params=_params(2),
        name="shortconv",
    )(hy_u, hy_u, hy_u, conv_w, conv_b.reshape(1, W))


def _filter_kernel(z_ref, w1h, w1l, b1_ref, f1_ref, w2h, w2l, b2_ref, f2_ref, w3h, w3l, dec_ref,
                   k_ref, s_ref):
    i = pl.program_id(0)
    z = z_ref[...]
    h1 = jnp.sin(f1_ref[...] * (_dot3(z, w1h[...], w1l[...]) + b1_ref[...]))
    h2 = jnp.sin(f2_ref[...] * (_dot3(h1, w2h[...], w2l[...]) + b2_ref[...]))
    k = _dot3(h2, w3h[...], w3l[...])
    t = z[:, 0:1]
    k = k * (jnp.exp(-t * jnp.abs(dec_ref[...])) + WINDOW_SHIFT)
    k_ref[...] = k

    @pl.when(i == 0)
    def _():
        s_ref[...] = jnp.zeros_like(s_ref)

    s_ref[...] += jnp.sum(jnp.abs(k), axis=0, keepdims=True)


def _filter_embedding(L):
    t = np.linspace(0.0, 1.0, L, dtype=np.float32).astype(np.float64)[:, None]
    w = (2.0 * math.pi * np.arange(L, dtype=np.float32) / np.float32(L)).astype(np.float64)[:, None]
    bands = np.linspace(1e-4, FILTER_BANDS - 1, FILTER_BANDS, dtype=np.float32).astype(np.float64)[None, :]
    bw = (bands.astype(np.float32) * w.astype(np.float32)).astype(np.float64)
    z = np.concatenate([t, np.cos(bw), -np.sin(bw)], axis=-1)
    zp = np.zeros((L, LANES), np.float32)
    zp[:, :FILTER_EMB] = z.astype(np.float32)
    return jnp.asarray(zp)


def _pad2(a, r, c):
    return jnp.zeros((r, c), F32).at[:a.shape[0], :a.shape[1]].set(a.astype(F32))


def _filters(L, fw1, fb1, ff1, fw2, fb2, ff2, fw3, decay):
    H = LANES
    nf = fw3.shape[1]
    z = _filter_embedding(L)
    w1h, w1l = _split(_pad2(fw1, H, H))
    w2h, w2l = _split(_pad2(fw2, H, H))
    w3h, w3l = _split(_pad2(fw3, H, nf))
    b1 = _pad2(fb1[None], 1, H)
    f1 = _pad2(ff1[None], 1, H)
    b2 = _pad2(fb2[None], 1, H)
    f2 = _pad2(ff2[None], 1, H)
    tr = min(512, L)
    full = lambda r, c: pl.BlockSpec((r, c), lambda i: (0, 0))
    return pl.pallas_call(
        _filter_kernel,
        out_shape=[jax.ShapeDtypeStruct((L, nf), F32), jax.ShapeDtypeStruct((1, nf), F32)],
        grid=(L // tr,),
        in_specs=[pl.BlockSpec((tr, H), lambda i: (i, 0)),
                  full(H, H), full(H, H), full(1, H), full(1, H),
                  full(H, H), full(H, H), full(1, H), full(1, H),
                  full(H, nf), full(H, nf), full(1, nf)],
        out_specs=[pl.BlockSpec((tr, nf), lambda i: (i, 0)), full(1, nf)],
        compiler_params=_params(1),
        name="filter",
    )(z, w1h, w1l, b1, f1, w2h, w2l, b2, f2, w3h, w3l, decay.reshape(1, nf).astype(F32))


def _np_split(m64):
    hi = m64.astype(np.float32).astype(BF16)
    lo = (m64 - hi.astype(np.float64)).astype(np.float32).astype(BF16)
    return jnp.asarray(hi), jnp.asarray(lo)


def _dft_constants(L):
    N = 2 * L
    n2 = LANES
    n1 = N // n2
    h1 = n1 // 2
    k1 = np.arange(n1)[:, None]
    s1 = np.arange(h1)[None, :]
    ang = -2.0 * np.pi * ((k1 * s1) % n1) / n1
    wr, wi = np.cos(ang), np.sin(ang)
    w1_filt = np.block([[wr, wr], [wi, wi], [wr, -wr], [wi, -wi]])
    w1_cplx = np.block([[wr, -wi], [wi, wr]])
    vr, vi = wr.T / N, -wi.T / N
    w3 = np.block([[vr, -vi], [vi, vr]])
    k2 = np.arange(n2)[:, None]
    s2 = np.arange(n2)[None, :]
    a2 = -2.0 * np.pi * ((k2 * s2) % n2) / n2
    w2r, w2i = jnp.asarray(np.cos(a2), F32), jnp.asarray(np.sin(a2), F32)
    at = -2.0 * np.pi * ((np.arange(n1)[:, None] * s2) % N) / N
    twr, twi = jnp.asarray(np.cos(at), F32), jnp.asarray(np.sin(at), F32)
    mr = w2r[None] * twr[:, None, :] - w2i[None] * twi[:, None, :]
    mi = w2r[None] * twi[:, None, :] + w2i[None] * twr[:, None, :]
    fwd = jnp.concatenate([jnp.concatenate([mr, -mi], axis=2),
                           jnp.concatenate([mi, mr], axis=2)], axis=1)
    fwd_hi, fwd_lo = _split(fwd)
    return dict(n1=n1, w1_filt=_np_split(w1_filt), w1_cplx=_np_split(w1_cplx)[0], w3=_np_split(w3)[0],
                fwd=(fwd_hi, fwd_lo), inv_hi=jnp.swapaxes(fwd_hi, 1, 2))


SCH = 8


def _dft1_kernel(x_ref, w_ref, a_ref, *, n1):
    w = w_ref[...]
    for j in range(SCH):
        rhs = jnp.concatenate([x_ref[0, 0, :, j, :], x_ref[0, 1, :, j, :]], axis=0)
        res = jnp.dot(w, rhs.astype(BF16), preferred_element_type=F32)
        a_ref[0, :, 0, j, :] = res[:n1]
        a_ref[0, :, 1, j, :] = res[n1:]


def _dft1_data(x, consts):
    B, L, C = x.shape
    n1 = consts["n1"]
    h1 = n1 // 2
    xv = x.reshape(B // 2, 2, h1, LANES, C)
    return pl.pallas_call(
        functools.partial(_dft1_kernel, n1=n1),
        out_shape=jax.ShapeDtypeStruct((B // 2, n1, 2, LANES, C), F32),
        grid=(B // 2, LANES // SCH),
        in_specs=[pl.BlockSpec((1, 2, h1, SCH, C), lambda p, j: (p, 0, 0, j, 0)),
                  pl.BlockSpec((2 * n1, n1), lambda p, j: (0, 0))],
        out_specs=pl.BlockSpec((1, n1, 2, SCH, C), lambda p, j: (p, 0, 0, j, 0)),
        compiler_params=_params(2),
        name="dft1",
    )(xv, consts["w1_cplx"])


def _dft1f_kernel(x_ref, wh_ref, wl_ref, a_ref, *, n1):
    C = HYENA_WIDTH
    wh, wl = wh_ref[...], wl_ref[...]
    for j in range(SCH):
        rhs = jnp.concatenate([x_ref[:, j, :C], x_ref[:, j, C:]], axis=0)
        res = _ldot3(wh, wl, rhs)
        for p in range(4):
            a_ref[0, :, p, j, :] = res[p * n1:(p + 1) * n1]


def _dft1_filter(kraw, consts):
    L, nf = kraw.shape
    C = HYENA_WIDTH
    n_ord = nf // (2 * C)
    n1 = consts["n1"]
    h1 = n1 // 2
    kv = kraw.reshape(h1, LANES, nf)
    wh, wl = consts["w1_filt"]
    return pl.pallas_call(
        functools.partial(_dft1f_kernel, n1=n1),
        out_shape=jax.ShapeDtypeStruct((n_ord, n1, 4, LANES, C), F32),
        grid=(n_ord, LANES // SCH),
        in_specs=[pl.BlockSpec((h1, SCH, 2 * C), lambda o, j: (0, j, o)),
                  pl.BlockSpec((4 * n1, n1), lambda o, j: (0, 0)),
                  pl.BlockSpec((4 * n1, n1), lambda o, j: (0, 0))],
        out_specs=pl.BlockSpec((1, n1, 4, SCH, C), lambda o, j: (o, 0, 0, j, 0)),
        compiler_params=_params(2),
        name="dft1f",
    )(kv, wh, wl)


KCH = 4


def _midf_kernel(a_ref, fh_ref, fl_ref, inv_ref, b0_ref, h_ref):
    n2 = LANES
    sc = inv_ref[0]
    for k in range(KCH):
        h_re = _ldot3(fh_ref[k, :n2, :], fl_ref[k, :n2, :], a_ref[0, k, :2 * n2, :])
        h_im = _ldot3(fh_ref[k, n2:, :], fl_ref[k, n2:, :], a_ref[0, k, 2 * n2:, :])
        h_ref[0, k, :n2, :] = (h_re - b0_ref[0]) * sc
        h_ref[0, k, n2:, :] = h_im * sc


def _filter_spectrum(af, inv_den, bwd0, consts):
    n_ord, n1, _, n2, C = af.shape
    a = af.reshape(n_ord, n1, 4 * n2, C)
    fh, fl = consts["fwd"]
    tab = pl.BlockSpec((KCH, 2 * n2, 2 * n2), lambda k, o: (k, 0, 0))
    vec = pl.BlockSpec((1, 1, C), lambda k, o: (o, 0, 0))
    return pl.pallas_call(
        _midf_kernel,
        out_shape=jax.ShapeDtypeStruct((n_ord, n1, 2 * n2, C), F32),
        grid=(n1 // KCH, n_ord),
        in_specs=[pl.BlockSpec((1, KCH, 4 * n2, C), lambda k, o: (o, k, 0, 0)), tab, tab, vec, vec],
        out_specs=pl.BlockSpec((1, KCH, 2 * n2, C), lambda k, o: (o, k, 0, 0)),
        compiler_params=_params(2),
        name="midf",
    )(a, fh, fl, inv_den, bwd0)


def _mid_kernel(a_ref, f_ref, i_ref, h_ref, b_ref):
    n2 = LANES
    for k in range(KCH):
        x = jnp.dot(f_ref[k], a_ref[0, k].astype(BF16), preferred_element_type=F32)
        xr, xi = x[:n2], x[n2:]
        hr, hi = h_ref[0, k, :n2, :], h_ref[0, k, n2:, :]
        y = jnp.concatenate([xr * hr - xi * hi, xr * hi + xi * hr], axis=0)
        b_ref[0, k] = jnp.dot(i_ref[k], y.astype(BF16), preferred_element_type=F32)


def _mid(a5, hspec, order, consts):
    P, n1, _, n2, C = a5.shape
    a = a5.reshape(P, n1, 2 * n2, C)
    tab = pl.BlockSpec((KCH, 2 * n2, 2 * n2), lambda k, p: (k, 0, 0))
    out = pl.pallas_call(
        _mid_kernel,
        out_shape=jax.ShapeDtypeStruct((P, n1, 2 * n2, C), F32),
        grid=(n1 // KCH, P),
        in_specs=[pl.BlockSpec((1, KCH, 2 * n2, C), lambda k, p: (p, k, 0, 0)),
                  tab, tab,
                  pl.BlockSpec((1, KCH, 2 * n2, C), lambda k, p: (order, k, 0, 0))],
        out_specs=pl.BlockSpec((1, KCH, 2 * n2, C), lambda k, p: (p, k, 0, 0)),
        compiler_params=_params(2),
        name="mid",
    )(a, consts["fwd"][0], consts["inv_hi"], hspec)
    return out.reshape(P, n1, 2, n2, C)


def _dft3_kernel(b_ref, w_ref, v_ref, g_ref, skip_ref, z_ref, *, h1):
    w = w_ref[...]
    skip = skip_ref[0]
    for j in range(SCH):
        rhs = jnp.concatenate([b_ref[0, :, 0, j, :], b_ref[0, :, 1, j, :]], axis=0)
        y = jnp.dot(w, rhs.astype(BF16), preferred_element_type=F32)
        for r in range(2):
            yr = y[r * h1:(r + 1) * h1]
            z_ref[0, r, :, j, :] = g_ref[0, r, :, j, :] * (yr + v_ref[0, r, :, j, :] * skip)


def _dft3_gate(b5, v, gate, skip, consts):
    P, n1, _, n2, C = b5.shape
    h1 = n1 // 2
    B, L, _ = v.shape
    five = lambda t: t.reshape(P, 2, h1, n2, C)
    dat = pl.BlockSpec((1, 2, h1, SCH, C), lambda p, j: (p, 0, 0, j, 0))
    out = pl.pallas_call(
        functools.partial(_dft3_kernel, h1=h1),
        out_shape=jax.ShapeDtypeStruct((P, 2, h1, n2, C), F32),
        grid=(P, n2 // SCH),
        in_specs=[pl.BlockSpec((1, n1, 2, SCH, C), lambda p, j: (p, 0, 0, j, 0)),
                  pl.BlockSpec((n1, 2 * n1), lambda p, j: (0, 0)),
                  dat, dat,
                  pl.BlockSpec((1, C), lambda p, j: (0, 0))],
        out_specs=dat,
        compiler_params=_params(2),
        name="dft3",
    )(b5, consts["w3"], five(v), five(gate), skip.reshape(1, C).astype(F32))
    return out.reshape(B, L, C)


def _hyena(hy_u, conv_w, conv_b, fw1, fb1, ff1, fw2, fb2, ff2, fw3, decay, skip):
    B, L, _ = hy_u.shape
    C = HYENA_WIDTH
    consts = _dft_constants(L)
    v, x1, x2 = _short_conv(hy_u, conv_w, conv_b)
    kraw, ksum = _filters(L, fw1, fb1, ff1, fw2, fb2, ff2, fw3, decay)
    ks = ksum.reshape(2, 2, C)
    inv_den = (1.0 / (ks[:, 0] + ks[:, 1])).reshape(2, 1, C)
    bwd0 = kraw[0].reshape(2, 2, C)[:, 1].reshape(2, 1, C)
    hspec = _filter_spectrum(_dft1_filter(kraw, consts), inv_den, bwd0, consts)
    z = v
    for o, gate in enumerate((x1, x2)):
        a5 = _dft1_data(z, consts)
        b5 = _mid(a5, hspec, o, consts)
        z = _dft3_gate(b5, z, gate, skip[o], consts)
    return z


def _merge_kernel(attn_ref, hy_ref, ga_ref, gh_ref, x_ref, mod_ref, wa_ref, wh_ref, wo_ref,
                  g1_ref, b1_ref, rwh_ref, rwl_ref, rb_ref, tri_ref,
                  x1_ref, h2_ref, route_ref, wts_ref, cnt_ref, carry_ref):
    first = (pl.program_id(0) == 0) & (pl.program_id(1) == 0)

    @pl.when(first)
    def _():
        carry_ref[...] = jnp.zeros_like(carry_ref)

    a = jnp.dot(attn_ref[0], wa_ref[...], preferred_element_type=F32)
    hy = jnp.dot(hy_ref[0].astype(BF16), wh_ref[...], preferred_element_type=F32)
    merged = ga_ref[0].astype(F32) * a + gh_ref[0].astype(F32) * hy
    y = jnp.dot(merged.astype(BF16), wo_ref[...], preferred_element_type=F32)
    gate1 = mod_ref[0, 2:3, :]
    shift2 = mod_ref[0, 3:4, :]
    scale2 = mod_ref[0, 4:5, :]
    x1 = _layer_norm(DN_ALPHA * x_ref[0] + gate1 * y, g1_ref[...], b1_ref[...])
    x1_ref[0] = x1
    h2 = x1 * (1.0 + scale2) + shift2
    h2_ref[0] = h2

    logits = _dot3(h2, rwh_ref[...], rwl_ref[...]) + rb_ref[...]
    tm = logits.shape[0]
    lane = lax.broadcasted_iota(jnp.int32, (tm, LANES), 1)
    lanef = lane.astype(F32)
    big = float(LANES)

    def first_lane(mask):
        return jnp.min(jnp.where(mask, lanef, big), axis=1, keepdims=True).astype(jnp.int32)

    gmask = lane < N_GROUPS
    gl = jnp.where(gmask, logits, NEG)
    gmax = jnp.max(gl, axis=1, keepdims=True)
    gidx = first_lane(gl == gmax)
    pg = 1.0 / jnp.sum(jnp.exp(gl - gmax), axis=1, keepdims=True)
    lo = ROUTE_OFF + gidx * EXPERTS_PER_GROUP
    emask = (lane >= lo) & (lane < lo + EXPERTS_PER_GROUP)
    el = jnp.where(emask, logits, NEG)
    v1 = jnp.max(el, axis=1, keepdims=True)
    i1 = first_lane(el == v1)
    el2 = jnp.where(emask & (lane != i1), logits, NEG)
    v2 = jnp.max(el2, axis=1, keepdims=True)
    i2 = first_lane(el2 == v2)
    e21 = jnp.exp(v2 - v1)
    w1 = pg / (1.0 + e21)
    w2 = pg * e21 / (1.0 + e21)

    sel1 = lane == i1
    sel2 = lane == i2
    onehot = jnp.where(sel1 | sel2, 1.0, 0.0)
    prefix = jnp.dot(tri_ref[...], onehot.astype(BF16), preferred_element_type=F32) + carry_ref[...]
    r1 = jnp.sum(jnp.where(sel1, prefix, 0.0), axis=1, keepdims=True)
    r2 = jnp.sum(jnp.where(sel2, prefix, 0.0), axis=1, keepdims=True)
    carry_ref[...] += jnp.sum(onehot, axis=0, keepdims=True)
    cnt_ref[...] = carry_ref[...]

    ranks = jnp.where(lane == 2, r1, jnp.where(lane == 3, r2, 0.0)).astype(jnp.int32)
    route_ref[0] = jnp.where(lane == 0, i1 - ROUTE_OFF, jnp.where(lane == 1, i2 - ROUTE_OFF, ranks))
    wts_ref[0] = jnp.where(lane == 0, w1, jnp.where(lane == 1, w2, 0.0))


def _merge(attn, hy, ga, gh, x, mod, w_attn_o, w_hy_o, w_out, ln1_g, ln1_b, rg_w, rg_b, re_w, re_b):
    B, S, D = x.shape
    tm = min(512, S)
    rw = jnp.zeros((D, LANES), F32).at[:, :N_GROUPS].set(rg_w).at[:, ROUTE_OFF:ROUTE_OFF + N_EXPERTS].set(re_w)
    rb = jnp.zeros((1, LANES), F32).at[0, :N_GROUPS].set(rg_b).at[0, ROUTE_OFF:ROUTE_OFF + N_EXPERTS].set(re_b)
    rwh, rwl = _split(rw)
    tri = (jnp.arange(tm)[:, None] > jnp.arange(tm)[None, :]).astype(BF16)
    row = lambda b, i: (b, i, 0)
    full = lambda r, c: pl.BlockSpec((r, c), lambda b, i: (0, 0))
    outs = [jax.ShapeDtypeStruct((B, S, D), F32), jax.ShapeDtypeStruct((B, S, D), F32),
            jax.ShapeDtypeStruct((B, S, LANES), jnp.int32), jax.ShapeDtypeStruct((B, S, LANES), F32),
            jax.ShapeDtypeStruct((1, LANES), F32)]
    return pl.pallas_call(
        _merge_kernel,
        out_shape=outs,
        grid=(B, S // tm),
        in_specs=[pl.BlockSpec((1, tm, ATTN_WIDTH), row), pl.BlockSpec((1, tm, HYENA_WIDTH), row),
                  pl.BlockSpec((1, tm, D), row), pl.BlockSpec((1, tm, D), row), pl.BlockSpec((1, tm, D), row),
                  pl.BlockSpec((1, 6, D), lambda b, i: (b, 0, 0)),
                  full(ATTN_WIDTH, D), full(HYENA_WIDTH, D), full(D, D),
                  full(1, D), full(1, D), full(D, LANES), full(D, LANES), full(1, LANES), full(tm, tm)],
        out_specs=[pl.BlockSpec((1, tm, D), row), pl.BlockSpec((1, tm, D), row),
                   pl.BlockSpec((1, tm, LANES), row), pl.BlockSpec((1, tm, LANES), row), full(1, LANES)],
        scratch_shapes=[pltpu.VMEM((1, LANES), F32)],
        compiler_params=_params(2),
        name="merge",
    )(attn, hy, ga, gh, x, mod, w_attn_o.astype(BF16), w_hy_o.astype(BF16), w_out.astype(BF16),
      ln1_g.reshape(1, D), ln1_b.reshape(1, D), rwh, rwl, rb, tri)


TOK_CH = 256


def _dest(route_ref, poffs_ref, r, j):
    return poffs_ref[route_ref[0, 0, 4 * r + j]] + route_ref[0, 0, 4 * r + 2 + j]


def _dispatch_kernel(poffs_ref, psz_ref, route_ref, h_ref, xb_ref, zero_ref, sem, zsem):
    @pl.when(pl.program_id(0) == 0)
    def _():
        zero_ref[...] = jnp.zeros_like(zero_ref)

        def zcopy(e):
            start = pl.multiple_of(poffs_ref[e] + psz_ref[e] - MOE_BLOCK, MOE_BLOCK)
            return pltpu.make_async_copy(zero_ref, xb_ref.at[pl.ds(start, MOE_BLOCK)], zsem)

        def zissue(e, c):
            @pl.when(psz_ref[e] > 0)
            def _():
                zcopy(e).start()
            return c

        def zdrain(e, c):
            @pl.when(psz_ref[e] > 0)
            def _():
                zcopy(e).wait()
            return c

        lax.fori_loop(0, N_EXPERTS, zissue, 0)
        lax.fori_loop(0, N_EXPERTS, zdrain, 0)

        def tcopy(b):
            return pltpu.make_async_copy(zero_ref, xb_ref.at[pl.ds(pl.multiple_of(b * MOE_BLOCK, MOE_BLOCK), MOE_BLOCK)], zsem)

        first_free = (poffs_ref[N_EXPERTS - 1] + psz_ref[N_EXPERTS - 1]) // MOE_BLOCK
        n_blocks = xb_ref.shape[0] // MOE_BLOCK
        lax.fori_loop(first_free, n_blocks, lambda b, c: (tcopy(b).start(), c)[1], 0)
        lax.fori_loop(first_free, n_blocks, lambda b, c: (tcopy(b).wait(), c)[1], 0)

    def row_copy(r, d):
        return pltpu.make_async_copy(h_ref.at[pl.ds(r, 1)], xb_ref.at[pl.ds(d, 1)], sem)

    def issue(r, c):
        row_copy(r, _dest(route_ref, poffs_ref, r, 0)).start()
        row_copy(r, _dest(route_ref, poffs_ref, r, 1)).start(priority=1)
        return c

    lax.fori_loop(0, TOK_CH, issue, 0)

    def drain(r, c):
        row_copy(0, 0).wait()
        return c

    lax.fori_loop(0, 2 * TOK_CH, drain, 0)


def _dispatch(h2, route, poffs, psizes, P):
    T, D = h2.shape
    nch = T // TOK_CH
    grid_spec = pltpu.PrefetchScalarGridSpec(
        num_scalar_prefetch=2,
        grid=(nch,),
        in_specs=[pl.BlockSpec((1, 1, 4 * TOK_CH), lambda i, po, ps: (i, 0, 0), memory_space=pltpu.SMEM),
                  pl.BlockSpec((TOK_CH, D), lambda i, po, ps: (i, 0))],
        out_specs=pl.BlockSpec(memory_space=pl.ANY),
        scratch_shapes=[pltpu.VMEM((MOE_BLOCK, D), F32), pltpu.SemaphoreType.DMA(()), pltpu.SemaphoreType.DMA(())],
    )
    return pl.pallas_call(
        _dispatch_kernel,
        out_shape=jax.ShapeDtypeStruct((P, D), F32),
        grid_spec=grid_spec,
        compiler_params=_params(1),
        name="dispatch",
    )(poffs, psizes, route, h2)


def _expert_kernel(be_ref, nu_ref, x_ref, w1_ref, w3_ref, w2_ref, y_ref, c1_ref, c3_ref, c2_ref):
    i = pl.program_id(0)
    used = i < nu_ref[0]
    fresh = (i == 0) | (be_ref[i] != be_ref[jnp.maximum(i - 1, 0)])

    @pl.when(used & fresh)
    def _():
        c1_ref[...] = w1_ref[0].astype(BF16)
        c3_ref[...] = w3_ref[0].astype(BF16)
        c2_ref[...] = w2_ref[0].astype(BF16)

    @pl.when(used)
    def _():
        x = x_ref[...].astype(BF16)
        a = jnp.dot(x, c1_ref[...], preferred_element_type=F32)
        g = jnp.dot(x, c3_ref[...], preferred_element_type=F32)
        hmid = (a * _sigmoid(a) * g).astype(BF16)
        y_ref[...] = jnp.dot(hmid, c2_ref[...], preferred_element_type=F32)

    @pl.when(jnp.logical_not(used))
    def _():
        y_ref[...] = jnp.zeros_like(y_ref)


def _experts(xb, blk_e, n_used, w1, w3, w2):
    P, D = xb.shape
    E, _, DE = w1.shape
    nblk = P // MOE_BLOCK
    grid_spec = pltpu.PrefetchScalarGridSpec(
        num_scalar_prefetch=2,
        grid=(nblk,),
        in_specs=[pl.BlockSpec((MOE_BLOCK, D), lambda i, be, nu: (jnp.minimum(i, jnp.maximum(nu[0] - 1, 0)), 0)),
                  pl.BlockSpec((1, D, DE), lambda i, be, nu: (be[i], 0, 0)),
                  pl.BlockSpec((1, D, DE), lambda i, be, nu: (be[i], 0, 0)),
                  pl.BlockSpec((1, DE, D), lambda i, be, nu: (be[i], 0, 0))],
        out_specs=pl.BlockSpec((MOE_BLOCK, D), lambda i, be, nu: (i, 0)),
        scratch_shapes=[pltpu.VMEM((D, DE), BF16), pltpu.VMEM((D, DE), BF16), pltpu.VMEM((DE, D), BF16)],
    )
    return pl.pallas_call(
        _expert_kernel,
        out_shape=jax.ShapeDtypeStruct((P, D), F32),
        grid_spec=grid_spec,
        compiler_params=_params(1),
        name="experts",
    )(blk_e, n_used, xb, w1, w3, w2)


def _combine_kernel(poffs_ref, route_ref, wts_ref, x1_ref, mod_ref, g_ref, b_ref, yb_ref, o_ref, rows_ref, sem):
    def row_copy(d, j, r):
        return pltpu.make_async_copy(yb_ref.at[pl.ds(d, 1)], rows_ref.at[j, pl.ds(r, 1)], sem)

    def issue(r, c):
        row_copy(_dest(route_ref, poffs_ref, r, 0), 0, r).start()
        row_copy(_dest(route_ref, poffs_ref, r, 1), 1, r).start(priority=1)
        return c

    lax.fori_loop(0, TOK_CH, issue, 0)

    def drain(r, c):
        row_copy(0, 0, 0).wait()
        return c

    lax.fori_loop(0, 2 * TOK_CH, drain, 0)
    w = wts_ref[...]
    y = w[:, 0:1] * rows_ref[0] + w[:, 1:2] * rows_ref[1]
    gate2 = mod_ref[0, 5:6, :]
    o_ref[...] = _layer_norm(DN_ALPHA * x1_ref[...] + gate2 * y, g_ref[...], b_ref[...])


def _combine(yb, route, poffs, wts, x1, mod, ln2_g, ln2_b, S):
    T, D = x1.shape
    nch = T // TOK_CH
    per_b = S // TOK_CH
    grid_spec = pltpu.PrefetchScalarGridSpec(
        num_scalar_prefetch=1,
        grid=(nch,),
        in_specs=[pl.BlockSpec((1, 1, 4 * TOK_CH), lambda i, po: (i, 0, 0), memory_space=pltpu.SMEM),
                  pl.BlockSpec((TOK_CH, LANES), lambda i, po: (i, 0)),
                  pl.BlockSpec((TOK_CH, D), lambda i, po: (i, 0)),
                  pl.BlockSpec((1, 6, D), lambda i, po: (i // per_b, 0, 0)),
                  pl.BlockSpec((1, D), lambda i, po: (0, 0)),
                  pl.BlockSpec((1, D), lambda i, po: (0, 0)),
                  pl.BlockSpec(memory_space=pl.ANY)],
        out_specs=pl.BlockSpec((TOK_CH, D), lambda i, po: (i, 0)),
        scratch_shapes=[pltpu.VMEM((2, TOK_CH, D), F32), pltpu.SemaphoreType.DMA(())],
    )
    return pl.pallas_call(
        _combine_kernel,
        out_shape=jax.ShapeDtypeStruct((T, D), F32),
        grid_spec=grid_spec,
        compiler_params=_params(1),
        name="combine",
    )(poffs, route, wts, x1, mod, ln2_g.reshape(1, D), ln2_b.reshape(1, D), yb)


def _moe(h2, x1, route, wts, counts, mod, w1, w3, w2, ln2_g, ln2_b):
    B, S, D = h2.shape
    T = B * S
    P = 2 * T + N_EXPERTS * MOE_BLOCK
    nblk = P // MOE_BLOCK
    nch = T // TOK_CH
    sizes = counts[0, ROUTE_OFF:ROUTE_OFF + N_EXPERTS].astype(jnp.int32)
    psizes = (sizes + MOE_BLOCK - 1) // MOE_BLOCK * MOE_BLOCK
    pends = jnp.cumsum(psizes)
    poffs = pends - psizes
    starts = jnp.arange(nblk, dtype=jnp.int32) * MOE_BLOCK
    blk_e = jnp.minimum(jnp.sum((pends[None, :] <= starts[:, None]).astype(jnp.int32), axis=1), N_EXPERTS - 1)
    n_used = pends[-1:] // MOE_BLOCK
    route4 = route.reshape(T, LANES)[:, :4].reshape(nch, 1, 4 * TOK_CH)
    xb = _dispatch(h2.reshape(T, D), route4, poffs, psizes, P)
    yb = _experts(xb, blk_e, n_used, w1, w3, w2)
    out = _combine(yb, route4, poffs, wts.reshape(T, LANES), x1.reshape(T, D), mod, ln2_g, ln2_b, S)
    return out.reshape(B, S, D)


def _layer(x, c, w_ada, b_ada, w_in, conv_w, conv_b, fw1, fb1, ff1, fw2, fb2, ff2, fw3, decay, skip,
           w_hy_o, w_attn_o, attn_sink, w_out, ln1_g, ln1_b, rg_w, rg_b, re_w, re_b, ew1, ew3, ew2,
           ln2_g, ln2_b):
    mod = _ada(c, w_ada, b_ada)
    q, k, v, hy_u, ga, gh = _in_proj(x, mod, w_in)
    attn = _attention(q, k, v, attn_sink)
    hy = _hyena(hy_u, conv_w, conv_b, fw1, fb1, ff1, fw2, fb2, ff2, fw3, decay, skip)
    x1, h2, route, wts, counts = _merge(attn, hy, ga, gh, x, mod, w_attn_o, w_hy_o, w_out,
                                        ln1_g, ln1_b, rg_w, rg_b, re_w, re_b)
    return _moe(h2, x1, route, wts, counts, mod, ew1, ew3, ew2, ln2_g, ln2_b)


def kernel(x, c, w_ada, b_ada, w_in, conv_w, conv_b, filt_w1, filt_b1, filt_freq1, filt_w2, filt_b2, filt_freq2, filt_w3, filt_decay, hy_skip, w_hy_o, w_attn_o, attn_sink, w_out, ln1_g, ln1_b, router_group_w, router_group_b, router_expert_w, router_expert_b, exp_w1, exp_w3, exp_w2, ln2_g, ln2_b):
    for l in range(w_ada.shape[0]):
        x = _layer(x, c, w_ada[l], b_ada[l], w_in[l], conv_w[l], conv_b[l], filt_w1[l], filt_b1[l],
                   filt_freq1[l], filt_w2[l], filt_b2[l], filt_freq2[l], filt_w3[l], filt_decay[l],
                   hy_skip[l], w_hy_o[l], w_attn_o[l], attn_sink[l], w_out[l], ln1_g[l], ln1_b[l],
                   router_group_w[l], router_group_b[l], router_expert_w[l], router_expert_b[l],
                   exp_w1[l], exp_w3[l], exp_w2[l], ln2_g[l], ln2_b[l])
    return x
```

```python
import functools
import math

import numpy as np
import jax
import jax.numpy as jnp
from jax import lax
from jax.experimental import pallas as pl
from jax.experimental.pallas import tpu as pltpu

F32 = jnp.float32
BF16 = jnp.bfloat16

N_HEADS = 8
N_KV_HEADS = 2
HEAD_DIM = 64
ATTN_WIDTH = N_HEADS * HEAD_DIM
KV_WIDTH = N_KV_HEADS * HEAD_DIM
WINDOW = 128
BLOCK_Q = 128
HYENA_WIDTH = 512
FILTER_EMB = 33
FILTER_BANDS = (FILTER_EMB - 1) // 2
WINDOW_SHIFT = 0.05
N_GROUPS = 8
EXPERTS_PER_GROUP = 8
N_EXPERTS = N_GROUPS * EXPERTS_PER_GROUP
D_EXPERT = 512
MOE_BLOCK = 256
LN_EPS = 1e-5
DEPTH = 1
DN_ALPHA = (2.0 * DEPTH) ** 0.25
NEG = -1e30

LANES = 128
ROUTE_OFF = N_GROUPS
VMEM_LIMIT = 56 * 1024 * 1024


def _params(n_axes, vmem=VMEM_LIMIT):
    return pltpu.CompilerParams(dimension_semantics=("arbitrary",) * n_axes, vmem_limit_bytes=vmem)


def _split(a):
    hi = a.astype(BF16)
    lo = (a - hi.astype(F32)).astype(BF16)
    return hi, lo


def _dot3(a, b_hi, b_lo):
    a_hi, a_lo = _split(a)
    acc = jnp.dot(a_hi, b_hi, preferred_element_type=F32)
    acc = acc + jnp.dot(a_hi, b_lo, preferred_element_type=F32)
    acc = acc + jnp.dot(a_lo, b_hi, preferred_element_type=F32)
    return acc


def _ldot3(w_hi, w_lo, a):
    a_hi, a_lo = _split(a)
    acc = jnp.dot(w_hi, a_hi, preferred_element_type=F32)
    acc = acc + jnp.dot(w_hi, a_lo, preferred_element_type=F32)
    acc = acc + jnp.dot(w_lo, a_hi, preferred_element_type=F32)
    return acc


def _sigmoid(x):
    return 1.0 / (1.0 + jnp.exp(-x))


def _layer_norm(r, g, b):
    mu = jnp.mean(r, axis=-1, keepdims=True)
    d = r - mu
    var = jnp.mean(d * d, axis=-1, keepdims=True)
    return d * lax.rsqrt(var + LN_EPS) * g + b


def _ada_kernel(c_ref, wh_ref, wl_ref, b_ref, o_ref):
    c = c_ref[...]
    s = c * _sigmoid(c)
    o_ref[...] = _dot3(s, wh_ref[...], wl_ref[...]) + b_ref[...]


def _ada(c, w_ada, b_ada):
    B, D = c.shape
    n_out = w_ada.shape[1]
    rows = 8
    cp = jnp.zeros((rows, D), F32).at[:B].set(c)
    wh, wl = _split(w_ada)
    tn = 1024
    out = pl.pallas_call(
        _ada_kernel,
        out_shape=jax.ShapeDtypeStruct((rows, n_out), F32),
        grid=(n_out // tn,),
        in_specs=[pl.BlockSpec((rows, D), lambda j: (0, 0)),
                  pl.BlockSpec((D, tn), lambda j: (0, j)),
                  pl.BlockSpec((D, tn), lambda j: (0, j)),
                  pl.BlockSpec((1, tn), lambda j: (0, j))],
        out_specs=pl.BlockSpec((rows, tn), lambda j: (0, j)),
        compiler_params=_params(1),
        name="ada",
    )(cp, wh, wl, b_ada.reshape(1, n_out))
    return out[:B].reshape(B, 6, D)


def _inproj_kernel(x_ref, mod_ref, w_ref, q_ref, kv_ref, hy_ref, ga_ref, gh_ref):
    C = HYENA_WIDTH
    x = x_ref[0]
    shift = mod_ref[0, 0:1, :]
    scale = mod_ref[0, 1:2, :]
    h = (x * (1.0 + scale) + shift).astype(BF16)

    def seg(lo, hi):
        return jnp.dot(h, w_ref[:, lo:hi], preferred_element_type=F32)

    o = 0
    q_ref[0] = (seg(o, o + ATTN_WIDTH) * (HEAD_DIM ** -0.5)).astype(BF16)
    o += ATTN_WIDTH
    kv_ref[0] = seg(o, o + 2 * KV_WIDTH).astype(BF16)
    o += 2 * KV_WIDTH
    hy_ref[0] = seg(o, o + 3 * C)
    o += 3 * C
    D = x.shape[-1]
    ga_ref[0] = _sigmoid(seg(o, o + D)).astype(BF16)
    o += D
    gh_ref[0] = _sigmoid(seg(o, o + D)).astype(BF16)


def _in_proj(x, mod, w_in):
    B, S, D = x.shape
    C = HYENA_WIDTH
    tm = min(512, S)
    wb = w_in.astype(BF16)
    nw = wb.shape[1]
    row = lambda b, i: (b, i, 0)
    shapes = [(ATTN_WIDTH, BF16), (2 * KV_WIDTH, BF16), (3 * C, F32), (D, BF16), (D, BF16)]
    return pl.pallas_call(
        _inproj_kernel,
        out_shape=[jax.ShapeDtypeStruct((B, S, w), dt) for w, dt in shapes],
        grid=(B, S // tm),
        in_specs=[pl.BlockSpec((1, tm, D), row),
                  pl.BlockSpec((1, 6, D), lambda b, i: (b, 0, 0)),
                  pl.BlockSpec((D, nw), lambda b, i: (0, 0))],
        out_specs=[pl.BlockSpec((1, tm, w), row) for w, _ in shapes],
        compiler_params=_params(2),
        name="in_proj",
    )(x, mod, wb)


ATT_TQ = 512


def _attn_kernel(sink_ref, q_ref, kvp_ref, kvc_ref, kvn_ref, bias_ref, o_ref, kv_scr, *, seq_len):
    i = pl.program_id(1)
    Q = BLOCK_Q
    TQ = q_ref.shape[1]
    G = N_HEADS // N_KV_HEADS
    kv_scr[0:Q] = kvp_ref[0]
    kv_scr[Q:Q + TQ] = kvc_ref[0]
    kv_scr[Q + TQ:] = kvn_ref[0]
    col = lax.broadcasted_iota(jnp.int32, (1, 3 * Q), 1)
    rhead = lax.broadcasted_iota(jnp.int32, (G * Q, 1), 0) // Q
    for j in range(TQ // Q):
        kpos = i * TQ + (j - 1) * Q + col
        colbias = jnp.where((kpos >= 0) & (kpos < seq_len), 0.0, NEG)
        for kv in range(N_KV_HEADS):
            kk = kv_scr[j * Q:(j + 3) * Q, kv * HEAD_DIM:(kv + 1) * HEAD_DIM]
            vv = kv_scr[j * Q:(j + 3) * Q, KV_WIDTH + kv * HEAD_DIM:KV_WIDTH + (kv + 1) * HEAD_DIM]
            heads = [kv * G + g for g in range(G)]
            qg = jnp.concatenate([q_ref[0, j * Q:(j + 1) * Q, h * HEAD_DIM:(h + 1) * HEAD_DIM] for h in heads], axis=0)
            s = lax.dot_general(qg, kk, (((1,), (1,)), ((), ())), preferred_element_type=F32)
            s = s + bias_ref[kv] + colbias
            snk = jnp.where(rhead == 0, sink_ref[heads[0]],
                            jnp.where(rhead == 1, sink_ref[heads[1]],
                                      jnp.where(rhead == 2, sink_ref[heads[2]], sink_ref[heads[3]])))
            m = jnp.maximum(jnp.max(s, axis=1, keepdims=True), snk)
            p = jnp.exp(s - m)
            den = jnp.sum(p, axis=1, keepdims=True) + jnp.exp(snk - m)
            o = jnp.dot(p.astype(BF16), vv, preferred_element_type=F32) / den
            for g, h in enumerate(heads):
                o_ref[0, j * Q:(j + 1) * Q, h * HEAD_DIM:(h + 1) * HEAD_DIM] = o[g * Q:(g + 1) * Q].astype(BF16)


def _attention(q, kv, sink):
    B, S, _ = q.shape
    Q = BLOCK_Q
    TQ = min(ATT_TQ, S)
    r = TQ // Q
    nq = S // Q
    G = N_HEADS // N_KV_HEADS
    assert G == 4
    a = jnp.arange(Q)[:, None]
    j = jnp.arange(3 * Q)[None, :]
    rel = jnp.abs(j - Q - a).astype(F32)
    slopes = 2.0 ** (-8.0 * jnp.arange(1, N_HEADS + 1, dtype=F32) / N_HEADS)
    bias = jnp.where(rel[None] <= WINDOW, -slopes[:, None, None] * rel[None], NEG).astype(F32)
    bias = bias.reshape(N_KV_HEADS, G * Q, 3 * Q)
    cur = lambda b, i: (b, i, 0)
    return pl.pallas_call(
        functools.partial(_attn_kernel, seq_len=S),
        out_shape=jax.ShapeDtypeStruct((B, S, ATTN_WIDTH), BF16),
        grid=(B, S // TQ),
        in_specs=[pl.BlockSpec(memory_space=pltpu.SMEM),
                  pl.BlockSpec((1, TQ, ATTN_WIDTH), cur),
                  pl.BlockSpec((1, Q, 2 * KV_WIDTH), lambda b, i: (b, jnp.maximum(i * r - 1, 0), 0)),
                  pl.BlockSpec((1, TQ, 2 * KV_WIDTH), cur),
                  pl.BlockSpec((1, Q, 2 * KV_WIDTH), lambda b, i: (b, jnp.minimum((i + 1) * r, nq - 1), 0)),
                  pl.BlockSpec((N_KV_HEADS, G * Q, 3 * Q), lambda b, i: (0, 0, 0))],
        out_specs=pl.BlockSpec((1, TQ, ATTN_WIDTH), cur),
        scratch_shapes=[pltpu.VMEM((TQ + 2 * Q, 2 * KV_WIDTH), BF16)],
        compiler_params=_params(2),
        name="attn",
    )(sink.astype(F32), q, kv, kv, kv, bias)


def _shortconv_kernel(u_ref, p_ref, n_ref, w_ref, b_ref, v_ref, x1_ref, x2_ref):
    i = pl.program_id(1)
    n = pl.num_programs(1)
    C = HYENA_WIDTH
    u = u_ref[0]
    tr = u.shape[0]
    prow = jnp.where(i > 0, p_ref[0, 7:8, :], 0.0)
    nrow = jnp.where(i < n - 1, n_ref[0, 0:1, :], 0.0)
    rid = lax.broadcasted_iota(jnp.int32, (tr, 1), 0)
    up = jnp.where(rid == 0, prow, pltpu.roll(u, 1, 0))
    dn = jnp.where(rid == tr - 1, nrow, pltpu.roll(u, tr - 1, 0))
    out = w_ref[0:1, :] * up + w_ref[1:2, :] * u + w_ref[2:3, :] * dn + b_ref[...]
    v_ref[0] = out[:, :C]
    x1_ref[0] = out[:, C:2 * C]
    x2_ref[0] = out[:, 2 * C:]


def _short_conv(hy_u, conv_w, conv_b):
    B, S, W = hy_u.shape
    C = HYENA_WIDTH
    tr = min(512, S)
    r8 = tr // 8
    nb8 = S // 8
    return pl.pallas_call(
        _shortconv_kernel,
        out_shape=[jax.ShapeDtypeStruct((B, S, C), F32)] * 3,
        grid=(B, S // tr),
        in_specs=[pl.BlockSpec((1, tr, W), lambda b, i: (b, i, 0)),
                  pl.BlockSpec((1, 8, W), lambda b, i: (b, jnp.maximum(i * r8 - 1, 0), 0)),
                  pl.BlockSpec((1, 8, W), lambda b, i: (b, jnp.minimum((i + 1) * r8, nb8 - 1), 0)),
                  pl.BlockSpec((3, W), lambda b, i: (0, 0)),
                  pl.BlockSpec((1, W), lambda b, i: (0, 0))],
        out_specs=[pl.BlockSpec((1, tr, C), lambda b, i: (b, i, 0))] * 3,
        compiler_params=_params(2),
        name="shortconv",
    )(hy_u, hy_u, hy_u, conv_w, conv_b.reshape(1, W))


def _filter_kernel(z_ref, w1h, w1l, b1_ref, f1_ref, w2h, w2l, b2_ref, f2_ref, w3h, w3l, dec_ref,
                   k_ref, s_ref):
    i = pl.program_id(0)
    z = z_ref[...]
    h1 = jnp.sin(f1_ref[...] * (_dot3(z, w1h[...], w1l[...]) + b1_ref[...]))
    h2 = jnp.sin(f2_ref[...] * (_dot3(h1, w2h[...], w2l[...]) + b2_ref[...]))
    k = _dot3(h2, w3h[...], w3l[...])
    t = z[:, 0:1]
    k = k * (jnp.exp(-t * jnp.abs(dec_ref[...])) + WINDOW_SHIFT)
    k_ref[...] = k

    @pl.when(i == 0)
    def _():
        s_ref[...] = jnp.zeros_like(s_ref)

    s_ref[...] += jnp.sum(jnp.abs(k), axis=0, keepdims=True)


def _filter_embedding(L):
    t = np.linspace(0.0, 1.0, L, dtype=np.float32).astype(np.float64)[:, None]
    w = (2.0 * math.pi * np.arange(L, dtype=np.float32) / np.float32(L)).astype(np.float64)[:, None]
    bands = np.linspace(1e-4, FILTER_BANDS - 1, FILTER_BANDS, dtype=np.float32).astype(np.float64)[None, :]
    bw = (bands.astype(np.float32) * w.astype(np.float32)).astype(np.float64)
    z = np.concatenate([t, np.cos(bw), -np.sin(bw)], axis=-1)
    zp = np.zeros((L, LANES), np.float32)
    zp[:, :FILTER_EMB] = z.astype(np.float32)
    return jnp.asarray(zp)


def _pad2(a, r, c):
    return jnp.zeros((r, c), F32).at[:a.shape[0], :a.shape[1]].set(a.astype(F32))


def _filters(L, fw1, fb1, ff1, fw2, fb2, ff2, fw3, decay):
    H = LANES
    nf = fw3.shape[1]
    z = _filter_embedding(L)
    w1h, w1l = _split(_pad2(fw1, H, H))
    w2h, w2l = _split(_pad2(fw2, H, H))
    w3h, w3l = _split(_pad2(fw3, H, nf))
    b1 = _pad2(fb1[None], 1, H)
    f1 = _pad2(ff1[None], 1, H)
    b2 = _pad2(fb2[None], 1, H)
    f2 = _pad2(ff2[None], 1, H)
    tr = min(512, L)
    full = lambda r, c: pl.BlockSpec((r, c), lambda i: (0, 0))
    return pl.pallas_call(
        _filter_kernel,
        out_shape=[jax.ShapeDtypeStruct((L, nf), F32), jax.ShapeDtypeStruct((1, nf), F32)],
        grid=(L // tr,),
        in_specs=[pl.BlockSpec((tr, H), lambda i: (i, 0)),
                  full(H, H), full(H, H), full(1, H), full(1, H),
                  full(H, H), full(H, H), full(1, H), full(1, H),
                  full(H, nf), full(H, nf), full(1, nf)],
        out_specs=[pl.BlockSpec((tr, nf), lambda i: (i, 0)), full(1, nf)],
        compiler_params=_params(1),
        name="filter",
    )(z, w1h, w1l, b1, f1, w2h, w2l, b2, f2, w3h, w3l, decay.reshape(1, nf).astype(F32))


def _np_split(m64):
    hi = m64.astype(np.float32).astype(BF16)
    lo = (m64 - hi.astype(np.float64)).astype(np.float32).astype(BF16)
    return jnp.asarray(hi), jnp.asarray(lo)


def _dft_constants(L):
    N = 2 * L
    n2 = LANES
    n1 = N // n2
    h1 = n1 // 2
    k1 = np.arange(n1)[:, None]
    s1 = np.arange(h1)[None, :]
    ang = -2.0 * np.pi * ((k1 * s1) % n1) / n1
    wr, wi = np.cos(ang), np.sin(ang)
    w1_filt = np.block([[wr, wr], [wi, wi], [wr, -wr], [wi, -wi]])
    w1_cplx = np.block([[wr, -wi], [wi, wr]])
    vr, vi = wr.T / N, -wi.T / N
    w3 = np.block([[vr, -vi], [vi, vr]])
    k2 = np.arange(n2)[:, None]
    s2 = np.arange(n2)[None, :]
    a2 = -2.0 * np.pi * ((k2 * s2) % n2) / n2
    w2r, w2i = jnp.asarray(np.cos(a2), F32), jnp.asarray(np.sin(a2), F32)
    at = -2.0 * np.pi * ((np.arange(n1)[:, None] * s2) % N) / N
    twr, twi = jnp.asarray(np.cos(at), F32), jnp.asarray(np.sin(at), F32)
    mr = w2r[None] * twr[:, None, :] - w2i[None] * twi[:, None, :]
    mi = w2r[None] * twi[:, None, :] + w2i[None] * twr[:, None, :]
    fwd = jnp.concatenate([jnp.concatenate([mr, -mi], axis=2),
                           jnp.concatenate([mi, mr], axis=2)], axis=1)
    fwd_hi, fwd_lo = _split(fwd)
    return dict(n1=n1, w1_filt=_np_split(w1_filt), w1_cplx=_np_split(w1_cplx)[0], w3=_np_split(w3)[0],
                fwd=(fwd_hi, fwd_lo), inv_hi=jnp.swapaxes(fwd_hi, 1, 2))


SCH = 8


def _dft1_kernel(x_ref, w_ref, a_ref, *, n1):
    w = w_ref[...]
    for j in range(SCH):
        rhs = jnp.concatenate([x_ref[0, 0, :, j, :], x_ref[0, 1, :, j, :]], axis=0)
        res = jnp.dot(w, rhs.astype(BF16), preferred_element_type=F32)
        a_ref[0, :, 0, j, :] = res[:n1]
        a_ref[0, :, 1, j, :] = res[n1:]


def _dft1_data(x, consts):
    B, L, C = x.shape
    n1 = consts["n1"]
    h1 = n1 // 2
    xv = x.reshape(B // 2, 2, h1, LANES, C)
    return pl.pallas_call(
        functools.partial(_dft1_kernel, n1=n1),
        out_shape=jax.ShapeDtypeStruct((B // 2, n1, 2, LANES, C), F32),
        grid=(B // 2, LANES // SCH),
        in_specs=[pl.BlockSpec((1, 2, h1, SCH, C), lambda p, j: (p, 0, 0, j, 0)),
                  pl.BlockSpec((2 * n1, n1), lambda p, j: (0, 0))],
        out_specs=pl.BlockSpec((1, n1, 2, SCH, C), lambda p, j: (p, 0, 0, j, 0)),
        compiler_params=_params(2),
        name="dft1",
    )(xv, consts["w1_cplx"])


def _dft1f_kernel(x_ref, wh_ref, wl_ref, a_ref, *, n1):
    C = HYENA_WIDTH
    wh, wl = wh_ref[...], wl_ref[...]
    for j in range(SCH):
        rhs = jnp.concatenate([x_ref[:, j, :C], x_ref[:, j, C:]], axis=0)
        res = _ldot3(wh, wl, rhs)
        for p in range(4):
            a_ref[0, :, p, j, :] = res[p * n1:(p + 1) * n1]


def _dft1_filter(kraw, consts):
    L, nf = kraw.shape
    C = HYENA_WIDTH
    n_ord = nf // (2 * C)
    n1 = consts["n1"]
    h1 = n1 // 2
    kv = kraw.reshape(h1, LANES, nf)
    wh, wl = consts["w1_filt"]
    return pl.pallas_call(
        functools.partial(_dft1f_kernel, n1=n1),
        out_shape=jax.ShapeDtypeStruct((n_ord, n1, 4, LANES, C), F32),
        grid=(n_ord, LANES // SCH),
        in_specs=[pl.BlockSpec((h1, SCH, 2 * C), lambda o, j: (0, j, o)),
                  pl.BlockSpec((4 * n1, n1), lambda o, j: (0, 0)),
                  pl.BlockSpec((4 * n1, n1), lambda o, j: (0, 0))],
        out_specs=pl.BlockSpec((1, n1, 4, SCH, C), lambda o, j: (o, 0, 0, j, 0)),
        compiler_params=_params(2),
        name="dft1f",
    )(kv, wh, wl)


KCH = 4


def _midf_kernel(a_ref, fh_ref, fl_ref, inv_ref, b0_ref, h_ref):
    n2 = LANES
    sc = inv_ref[0]
    for k in range(KCH):
        h_re = _ldot3(fh_ref[k, :n2, :], fl_ref[k, :n2, :], a_ref[0, k, :2 * n2, :])
        h_im = _ldot3(fh_ref[k, n2:, :], fl_ref[k, n2:, :], a_ref[0, k, 2 * n2:, :])
        h_ref[0, k, :n2, :] = (h_re - b0_ref[0]) * sc
        h_ref[0, k, n2:, :] = h_im * sc


def _filter_spectrum(af, inv_den, bwd0, consts):
    n_ord, n1, _, n2, C = af.shape
    a = af.reshape(n_ord, n1, 4 * n2, C)
    fh, fl = consts["fwd"]
    tab = pl.BlockSpec((KCH, 2 * n2, 2 * n2), lambda k, o: (k, 0, 0))
    vec = pl.BlockSpec((1, 1, C), lambda k, o: (o, 0, 0))
    return pl.pallas_call(
        _midf_kernel,
        out_shape=jax.ShapeDtypeStruct((n_ord, n1, 2 * n2, C), F32),
        grid=(n1 // KCH, n_ord),
        in_specs=[pl.BlockSpec((1, KCH, 4 * n2, C), lambda k, o: (o, k, 0, 0)), tab, tab, vec, vec],
        out_specs=pl.BlockSpec((1, KCH, 2 * n2, C), lambda k, o: (o, k, 0, 0)),
        compiler_params=_params(2),
        name="midf",
    )(a, fh, fl, inv_den, bwd0)


def _mid_kernel(a_ref, f_ref, i_ref, h_ref, b_ref):
    n2 = LANES
    for k in range(KCH):
        x = jnp.dot(f_ref[k], a_ref[0, k].astype(BF16), preferred_element_type=F32)
        xr, xi = x[:n2], x[n2:]
        hr, hi = h_ref[0, k, :n2, :], h_ref[0, k, n2:, :]
        y = jnp.concatenate([xr * hr - xi * hi, xr * hi + xi * hr], axis=0)
        b_ref[0, k] = jnp.dot(i_ref[k], y.astype(BF16), preferred_element_type=F32)


def _mid(a5, hspec, order, consts):
    P, n1, _, n2, C = a5.shape
    a = a5.reshape(P, n1, 2 * n2, C)
    tab = pl.BlockSpec((KCH, 2 * n2, 2 * n2), lambda k, p: (k, 0, 0))
    out = pl.pallas_call(
        _mid_kernel,
        out_shape=jax.ShapeDtypeStruct((P, n1, 2 * n2, C), F32),
        grid=(n1 // KCH, P),
        in_specs=[pl.BlockSpec((1, KCH, 2 * n2, C), lambda k, p: (p, k, 0, 0)),
                  tab, tab,
                  pl.BlockSpec((1, KCH, 2 * n2, C), lambda k, p: (order, k, 0, 0))],
        out_specs=pl.BlockSpec((1, KCH, 2 * n2, C), lambda k, p: (p, k, 0, 0)),
        compiler_params=_params(2),
        name="mid",
    )(a, consts["fwd"][0], consts["inv_hi"], hspec)
    return out.reshape(P, n1, 2, n2, C)


def _dft3_kernel(b_ref, w_ref, v_ref, g_ref, skip_ref, z_ref, *, h1):
    w = w_ref[...]
    skip = skip_ref[0]
    for j in range(SCH):
        rhs = jnp.concatenate([b_ref[0, :, 0, j, :], b_ref[0, :, 1, j, :]], axis=0)
        y = jnp.dot(w, rhs.astype(BF16), preferred_element_type=F32)
        for r in range(2):
            yr = y[r * h1:(r + 1) * h1]
            z_ref[0, r, :, j, :] = g_ref[0, r, :, j, :] * (yr + v_ref[0, r, :, j, :] * skip)


def _dft3_gate(b5, v, gate, skip, consts):
    P, n1, _, n2, C = b5.shape
    h1 = n1 // 2
    B, L, _ = v.shape
    five = lambda t: t.reshape(P, 2, h1, n2, C)
    dat = pl.BlockSpec((1, 2, h1, SCH, C), lambda p, j: (p, 0, 0, j, 0))
    out = pl.pallas_call(
        functools.partial(_dft3_kernel, h1=h1),
        out_shape=jax.ShapeDtypeStruct((P, 2, h1, n2, C), F32),
        grid=(P, n2 // SCH),
        in_specs=[pl.BlockSpec((1, n1, 2, SCH, C), lambda p, j: (p, 0, 0, j, 0)),
                  pl.BlockSpec((n1, 2 * n1), lambda p, j: (0, 0)),
                  dat, dat,
                  pl.BlockSpec((1, C), lambda p, j: (0, 0))],
        out_specs=dat,
        compiler_params=_params(2),
        name="dft3",
    )(b5, consts["w3"], five(v), five(gate), skip.reshape(1, C).astype(F32))
    return out.reshape(B, L, C)


def _hyena(hy_u, conv_w, conv_b, fw1, fb1, ff1, fw2, fb2, ff2, fw3, decay, skip):
    B, L, _ = hy_u.shape
    C = HYENA_WIDTH
    consts = _dft_constants(L)
    v, x1, x2 = _short_conv(hy_u, conv_w, conv_b)
    kraw, ksum = _filters(L, fw1, fb1, ff1, fw2, fb2, ff2, fw3, decay)
    ks = ksum.reshape(2, 2, C)
    inv_den = (1.0 / (ks[:, 0] + ks[:, 1])).reshape(2, 1, C)
    bwd0 = kraw[0].reshape(2, 2, C)[:, 1].reshape(2, 1, C)
    hspec = _filter_spectrum(_dft1_filter(kraw, consts), inv_den, bwd0, consts)
    z = v
    for o, gate in enumerate((x1, x2)):
        a5 = _dft1_data(z, consts)
        b5 = _mid(a5, hspec, o, consts)
        z = _dft3_gate(b5, z, gate, skip[o], consts)
    return z


def _merge_kernel(attn_ref, hy_ref, ga_ref, gh_ref, x_ref, mod_ref, wa_ref, wh_ref, wo_ref,
                  g1_ref, b1_ref, rwh_ref, rwl_ref, rb_ref, tri_ref,
                  x1_ref, h2_ref, route_ref, wts_ref, cnt_ref, carry_ref):
    first = (pl.program_id(0) == 0) & (pl.program_id(1) == 0)

    @pl.when(first)
    def _():
        carry_ref[...] = jnp.zeros_like(carry_ref)

    a = jnp.dot(attn_ref[0], wa_ref[...], preferred_element_type=F32)
    hy = jnp.dot(hy_ref[0].astype(BF16), wh_ref[...], preferred_element_type=F32)
    merged = ga_ref[0].astype(F32) * a + gh_ref[0].astype(F32) * hy
    y = jnp.dot(merged.astype(BF16), wo_ref[...], preferred_element_type=F32)
    gate1 = mod_ref[0, 2:3, :]
    shift2 = mod_ref[0, 3:4, :]
    scale2 = mod_ref[0, 4:5, :]
    x1 = _layer_norm(DN_ALPHA * x_ref[0] + gate1 * y, g1_ref[...], b1_ref[...])
    x1_ref[0] = x1
    h2 = x1 * (1.0 + scale2) + shift2
    h2_ref[0] = h2

    logits = _dot3(h2, rwh_ref[...], rwl_ref[...]) + rb_ref[...]
    tm = logits.shape[0]
    lane = lax.broadcasted_iota(jnp.int32, (tm, LANES), 1)
    lanef = lane.astype(F32)
    big = float(LANES)

    def first_lane(mask):
        return jnp.min(jnp.where(mask, lanef, big), axis=1, keepdims=True).astype(jnp.int32)

    gmask = lane < N_GROUPS
    gl = jnp.where(gmask, logits, NEG)
    gmax = jnp.max(gl, axis=1, keepdims=True)
    gidx = first_lane(gl == gmax)
    pg = 1.0 / jnp.sum(jnp.exp(gl - gmax), axis=1, keepdims=True)
    lo = ROUTE_OFF + gidx * EXPERTS_PER_GROUP
    emask = (lane >= lo) & (lane < lo + EXPERTS_PER_GROUP)
    el = jnp.where(emask, logits, NEG)
    v1 = jnp.max(el, axis=1, keepdims=True)
    i1 = first_lane(el == v1)
    el2 = jnp.where(emask & (lane != i1), logits, NEG)
    v2 = jnp.max(el2, axis=1, keepdims=True)
    i2 = first_lane(el2 == v2)
    e21 = jnp.exp(v2 - v1)
    w1 = pg / (1.0 + e21)
    w2 = pg * e21 / (1.0 + e21)

    sel1 = lane == i1
    sel2 = lane == i2
    onehot = jnp.where(sel1 | sel2, 1.0, 0.0)
    prefix = jnp.dot(tri_ref[...], onehot.astype(BF16), preferred_element_type=F32) + carry_ref[...]
    r1 = jnp.sum(jnp.where(sel1, prefix, 0.0), axis=1, keepdims=True)
    r2 = jnp.sum(jnp.where(sel2, prefix, 0.0), axis=1, keepdims=True)
    carry_ref[...] += jnp.sum(onehot, axis=0, keepdims=True)
    cnt_ref[...] = carry_ref[...]

    ranks = jnp.where(lane == 2, r1, jnp.where(lane == 3, r2, 0.0)).astype(jnp.int32)
    route_ref[0] = jnp.where(lane == 0, i1 - ROUTE_OFF, jnp.where(lane == 1, i2 - ROUTE_OFF, ranks))
    wts_ref[0] = jnp.where(lane == 0, w1, jnp.where(lane == 1, w2, 0.0))


def _merge(attn, hy, ga, gh, x, mod, w_attn_o, w_hy_o, w_out, ln1_g, ln1_b, rg_w, rg_b, re_w, re_b):
    B, S, D = x.shape
    tm = min(512, S)
    rw = jnp.zeros((D, LANES), F32).at[:, :N_GROUPS].set(rg_w).at[:, ROUTE_OFF:ROUTE_OFF + N_EXPERTS].set(re_w)
    rb = jnp.zeros((1, LANES), F32).at[0, :N_GROUPS].set(rg_b).at[0, ROUTE_OFF:ROUTE_OFF + N_EXPERTS].set(re_b)
    rwh, rwl = _split(rw)
    tri = (jnp.arange(tm)[:, None] > jnp.arange(tm)[None, :]).astype(BF16)
    row = lambda b, i: (b, i, 0)
    full = lambda r, c: pl.BlockSpec((r, c), lambda b, i: (0, 0))
    outs = [jax.ShapeDtypeStruct((B, S, D), F32), jax.ShapeDtypeStruct((B, S, D), F32),
            jax.ShapeDtypeStruct((B, S, LANES), jnp.int32), jax.ShapeDtypeStruct((B, S, LANES), F32),
            jax.ShapeDtypeStruct((1, LANES), F32)]
    return pl.pallas_call(
        _merge_kernel,
        out_shape=outs,
        grid=(B, S // tm),
        in_specs=[pl.BlockSpec((1, tm, ATTN_WIDTH), row), pl.BlockSpec((1, tm, HYENA_WIDTH), row),
                  pl.BlockSpec((1, tm, D), row), pl.BlockSpec((1, tm, D), row), pl.BlockSpec((1, tm, D), row),
                  pl.BlockSpec((1, 6, D), lambda b, i: (b, 0, 0)),
                  full(ATTN_WIDTH, D), full(HYENA_WIDTH, D), full(D, D),
                  full(1, D), full(1, D), full(D, LANES), full(D, LANES), full(1, LANES), full(tm, tm)],
        out_specs=[pl.BlockSpec((1, tm, D), row), pl.BlockSpec((1, tm, D), row),
                   pl.BlockSpec((1, tm, LANES), row), pl.BlockSpec((1, tm, LANES), row), full(1, LANES)],
        scratch_shapes=[pltpu.VMEM((1, LANES), F32)],
        compiler_params=_params(2),
        name="merge",
    )(attn, hy, ga, gh, x, mod, w_attn_o.astype(BF16), w_hy_o.astype(BF16), w_out.astype(BF16),
      ln1_g.reshape(1, D), ln1_b.reshape(1, D), rwh, rwl, rb, tri)


TOK_CH = 1024
WAIT_ROWS = 128


def _dispatch_kernel(poffs_ref, psz_ref, dest_ref, h_ref, xb_ref, zero_ref, sem, zsem):
    @pl.when(pl.program_id(0) == 0)
    def _():
        zero_ref[...] = jnp.zeros_like(zero_ref)

        def zcopy(e):
            start = pl.multiple_of(poffs_ref[e] + psz_ref[e] - MOE_BLOCK, MOE_BLOCK)
            return pltpu.make_async_copy(zero_ref, xb_ref.at[pl.ds(start, MOE_BLOCK)], zsem)

        def zissue(e, c):
            @pl.when(psz_ref[e] > 0)
            def _():
                zcopy(e).start()
            return c

        def zdrain(e, c):
            @pl.when(psz_ref[e] > 0)
            def _():
                zcopy(e).wait()
            return c

        lax.fori_loop(0, N_EXPERTS, zissue, 0)
        lax.fori_loop(0, N_EXPERTS, zdrain, 0)

        def tcopy(b):
            return pltpu.make_async_copy(zero_ref, xb_ref.at[pl.ds(pl.multiple_of(b * MOE_BLOCK, MOE_BLOCK), MOE_BLOCK)], zsem)

        first_free = (poffs_ref[N_EXPERTS - 1] + psz_ref[N_EXPERTS - 1]) // MOE_BLOCK
        n_blocks = xb_ref.shape[0] // MOE_BLOCK
        lax.fori_loop(first_free, n_blocks, lambda b, c: (tcopy(b).start(), c)[1], 0)
        lax.fori_loop(first_free, n_blocks, lambda b, c: (tcopy(b).wait(), c)[1], 0)

    def issue(r, c):
        for j in range(2):
            pltpu.make_async_copy(h_ref.at[pl.ds(r, 1)], xb_ref.at[pl.ds(dest_ref[0, 0, 2 * r + j], 1)], sem).start(priority=j)
        return c

    lax.fori_loop(0, TOK_CH, issue, 0, unroll=8)

    def drain(r, c):
        pltpu.make_async_copy(xb_ref.at[pl.ds(0, WAIT_ROWS)], xb_ref.at[pl.ds(0, WAIT_ROWS)], sem).wait()
        return c

    lax.fori_loop(0, 2 * TOK_CH // WAIT_ROWS, drain, 0)


def _dispatch(h2, dest, poffs, psizes, P):
    T, D = h2.shape
    nch = T // TOK_CH
    grid_spec = pltpu.PrefetchScalarGridSpec(
        num_scalar_prefetch=2,
        grid=(nch,),
        in_specs=[pl.BlockSpec((1, 1, 2 * TOK_CH), lambda i, po, ps: (i, 0, 0), memory_space=pltpu.SMEM),
                  pl.BlockSpec((TOK_CH, D), lambda i, po, ps: (i, 0))],
        out_specs=pl.BlockSpec(memory_space=pl.ANY),
        scratch_shapes=[pltpu.VMEM((MOE_BLOCK, D), F32), pltpu.SemaphoreType.DMA(()), pltpu.SemaphoreType.DMA(())],
    )
    return pl.pallas_call(
        _dispatch_kernel,
        out_shape=jax.ShapeDtypeStruct((P, D), F32),
        grid_spec=grid_spec,
        compiler_params=_params(1),
        name="dispatch",
    )(poffs, psizes, dest, h2)


def _expert_kernel(be_ref, nu_ref, x_ref, w1_ref, w3_ref, w2_ref, y_ref, c1_ref, c3_ref, c2_ref):
    i = pl.program_id(0)
    used = i < nu_ref[0]
    fresh = (i == 0) | (be_ref[i] != be_ref[jnp.maximum(i - 1, 0)])

    @pl.when(used & fresh)
    def _():
        c1_ref[...] = w1_ref[0].astype(BF16)
        c3_ref[...] = w3_ref[0].astype(BF16)
        c2_ref[...] = w2_ref[0].astype(BF16)

    @pl.when(used)
    def _():
        x = x_ref[...].astype(BF16)
        a = jnp.dot(x, c1_ref[...], preferred_element_type=F32)
        g = jnp.dot(x, c3_ref[...], preferred_element_type=F32)
        hmid = (a * _sigmoid(a) * g).astype(BF16)
        y_ref[...] = jnp.dot(hmid, c2_ref[...], preferred_element_type=F32)

    @pl.when(jnp.logical_not(used))
    def _():
        y_ref[...] = jnp.zeros_like(y_ref)


def _experts(xb, blk_e, n_used, w1, w3, w2):
    P, D = xb.shape
    E, _, DE = w1.shape
    nblk = P // MOE_BLOCK
    grid_spec = pltpu.PrefetchScalarGridSpec(
        num_scalar_prefetch=2,
        grid=(nblk,),
        in_specs=[pl.BlockSpec((MOE_BLOCK, D), lambda i, be, nu: (jnp.minimum(i, jnp.maximum(nu[0] - 1, 0)), 0)),
                  pl.BlockSpec((1, D, DE), lambda i, be, nu: (be[i], 0, 0)),
                  pl.BlockSpec((1, D, DE), lambda i, be, nu: (be[i], 0, 0)),
                  pl.BlockSpec((1, DE, D), lambda i, be, nu: (be[i], 0, 0))],
        out_specs=pl.BlockSpec((MOE_BLOCK, D), lambda i, be, nu: (i, 0)),
        scratch_shapes=[pltpu.VMEM((D, DE), BF16), pltpu.VMEM((D, DE), BF16), pltpu.VMEM((DE, D), BF16)],
    )
    return pl.pallas_call(
        _expert_kernel,
        out_shape=jax.ShapeDtypeStruct((P, D), F32),
        grid_spec=grid_spec,
        compiler_params=_params(1),
        name="experts",
    )(blk_e, n_used, xb, w1, w3, w2)


def _combine_kernel(dest_ref, wts_ref, x1_ref, mod_ref, g_ref, b_ref, yb_ref, o_ref, rows_ref, sem):
    def issue(r, c):
        for j in range(2):
            pltpu.make_async_copy(yb_ref.at[pl.ds(dest_ref[0, 0, 2 * r + j], 1)], rows_ref.at[j, pl.ds(r, 1)], sem).start(priority=j)
        return c

    lax.fori_loop(0, TOK_CH, issue, 0, unroll=8)

    def drain(r, c):
        pltpu.make_async_copy(yb_ref.at[pl.ds(0, WAIT_ROWS)], rows_ref.at[0, pl.ds(0, WAIT_ROWS)], sem).wait()
        return c

    lax.fori_loop(0, 2 * TOK_CH // WAIT_ROWS, drain, 0)
    w = wts_ref[...]
    y = w[:, 0:1] * rows_ref[0] + w[:, 1:2] * rows_ref[1]
    gate2 = mod_ref[0, 5:6, :]
    o_ref[...] = _layer_norm(DN_ALPHA * x1_ref[...] + gate2 * y, g_ref[...], b_ref[...])


def _combine(yb, dest, wts, x1, mod, ln2_g, ln2_b, S):
    T, D = x1.shape
    nch = T // TOK_CH
    per_b = S // TOK_CH
    return pl.pallas_call(
        _combine_kernel,
        out_shape=jax.ShapeDtypeStruct((T, D), F32),
        grid=(nch,),
        in_specs=[pl.BlockSpec((1, 1, 2 * TOK_CH), lambda i: (i, 0, 0), memory_space=pltpu.SMEM),
                  pl.BlockSpec((TOK_CH, LANES), lambda i: (i, 0)),
                  pl.BlockSpec((TOK_CH, D), lambda i: (i, 0)),
                  pl.BlockSpec((1, 6, D), lambda i: (i // per_b, 0, 0)),
                  pl.BlockSpec((1, D), lambda i: (0, 0)),
                  pl.BlockSpec((1, D), lambda i: (0, 0)),
                  pl.BlockSpec(memory_space=pl.ANY)],
        out_specs=pl.BlockSpec((TOK_CH, D), lambda i: (i, 0)),
        scratch_shapes=[pltpu.VMEM((2, TOK_CH, D), F32), pltpu.SemaphoreType.DMA(())],
        compiler_params=_params(1),
        name="combine",
    )(dest, wts, x1, mod, ln2_g.reshape(1, D), ln2_b.reshape(1, D), yb)


def _moe(h2, x1, route, wts, counts, mod, w1, w3, w2, ln2_g, ln2_b):
    B, S, D = h2.shape
    T = B * S
    P = 2 * T + N_EXPERTS * MOE_BLOCK
    nblk = P // MOE_BLOCK
    nch = T // TOK_CH
    sizes = counts[0, ROUTE_OFF:ROUTE_OFF + N_EXPERTS].astype(jnp.int32)
    psizes = (sizes + MOE_BLOCK - 1) // MOE_BLOCK * MOE_BLOCK
    pends = jnp.cumsum(psizes)
    poffs = pends - psizes
    starts = jnp.arange(nblk, dtype=jnp.int32) * MOE_BLOCK
    blk_e = jnp.minimum(jnp.sum((pends[None, :] <= starts[:, None]).astype(jnp.int32), axis=1), N_EXPERTS - 1)
    n_used = pends[-1:] // MOE_BLOCK
    r4 = route.reshape(T, LANES)[:, :4]
    sel = r4[:, :2, None] == jnp.arange(N_EXPERTS, dtype=jnp.int32)[None, None, :]
    dest = (r4[:, 2:4] + jnp.sum(jnp.where(sel, poffs[None, None, :], 0), axis=-1)).reshape(nch, 1, 2 * TOK_CH)
    xb = _dispatch(h2.reshape(T, D), dest, poffs, psizes, P)
    yb = _experts(xb, blk_e, n_used, w1, w3, w2)
    out = _combine(yb, dest, wts.reshape(T, LANES), x1.reshape(T, D), mod, ln2_g, ln2_b, S)
    return out.reshape(B, S, D)


def _layer(x, c, w_ada, b_ada, w_in, conv_w, conv_b, fw1, fb1, ff1, fw2, fb2, ff2, fw3, decay, skip,
           w_hy_o, w_attn_o, attn_sink, w_out, ln1_g, ln1_b, rg_w, rg_b, re_w, re_b, ew1, ew3, ew2,
           ln2_g, ln2_b):
    mod = _ada(c, w_ada, b_ada)
    q, kv, hy_u, ga, gh = _in_proj(x, mod, w_in)
    attn = _attention(q, kv, attn_sink)
    hy = _hyena(hy_u, conv_w, conv_b, fw1, fb1, ff1, fw2, fb2, ff2, fw3, decay, skip)
    x1, h2, route, wts, counts = _merge(attn, hy, ga, gh, x, mod, w_attn_o, w_hy_o, w_out,
                                        ln1_g, ln1_b, rg_w, rg_b, re_w, re_b)
    return _moe(h2, x1, route, wts, counts, mod, ew1, ew3, ew2, ln2_g, ln2_b)


def kernel(x, c, w_ada, b_ada, w_in, conv_w, conv_b, filt_w1, filt_b1, filt_freq1, filt_w2, filt_b2, filt_freq2, filt_w3, filt_decay, hy_skip, w_hy_o, w_attn_o, attn_sink, w_out, ln1_g, ln1_b, router_group_w, router_group_b, router_expert_w, router_expert_b, exp_w1, exp_w3, exp_w2, ln2_g, ln2_b):
    for l in range(w_ada.shape[0]):
        x = _layer(x, c, w_ada[l], b_ada[l], w_in[l], conv_w[l], conv_b[l], filt_w1[l], filt_b1[l],
                   filt_freq1[l], filt_w2[l], filt_b2[l], filt_freq2[l], filt_w3[l], filt_decay[l],
                   hy_skip[l], w_hy_o[l], w_attn_o[l], attn_sink[l], w_out[l], ln1_g[l], ln1_b[l],
                   router_group_w[l], router_group_b[l], router_expert_w[l], router_expert_b[l],
                   exp_w1[l], exp_w3[l], exp_w2[l], ln2_g[l], ln2_b[l])
    return x
```

```python
import functools
import math

import numpy as np
import jax
import jax.numpy as jnp
from jax import lax
from jax.experimental import pallas as pl
from jax.experimental.pallas import tpu as pltpu

F32 = jnp.float32
BF16 = jnp.bfloat16

N_HEADS = 8
N_KV_HEADS = 2
HEAD_DIM = 64
ATTN_WIDTH = N_HEADS * HEAD_DIM
KV_WIDTH = N_KV_HEADS * HEAD_DIM
WINDOW = 128
BLOCK_Q = 128
HYENA_WIDTH = 512
FILTER_EMB = 33
FILTER_BANDS = (FILTER_EMB - 1) // 2
WINDOW_SHIFT = 0.05
N_GROUPS = 8
EXPERTS_PER_GROUP = 8
N_EXPERTS = N_GROUPS * EXPERTS_PER_GROUP
D_EXPERT = 512
MOE_BLOCK = 256
LN_EPS = 1e-5
DEPTH = 1
DN_ALPHA = (2.0 * DEPTH) ** 0.25
NEG = -1e30

LANES = 128
ROUTE_OFF = N_GROUPS
VMEM_LIMIT = 56 * 1024 * 1024


def _params(n_axes, vmem=VMEM_LIMIT):
    return pltpu.CompilerParams(dimension_semantics=("arbitrary",) * n_axes, vmem_limit_bytes=vmem)


def _split(a):
    hi = a.astype(BF16)
    lo = (a - hi.astype(F32)).astype(BF16)
    return hi, lo


def _dot3(a, b_hi, b_lo):
    a_hi, a_lo = _split(a)
    acc = jnp.dot(a_hi, b_hi, preferred_element_type=F32)
    acc = acc + jnp.dot(a_hi, b_lo, preferred_element_type=F32)
    acc = acc + jnp.dot(a_lo, b_hi, preferred_element_type=F32)
    return acc


def _ldot3(w_hi, w_lo, a):
    a_hi, a_lo = _split(a)
    acc = jnp.dot(w_hi, a_hi, preferred_element_type=F32)
    acc = acc + jnp.dot(w_hi, a_lo, preferred_element_type=F32)
    acc = acc + jnp.dot(w_lo, a_hi, preferred_element_type=F32)
    return acc


def _pack2(a, b):
    ia = lax.bitcast_convert_type(a.astype(BF16).astype(F32), jnp.int32)
    ib = lax.bitcast_convert_type(b.astype(BF16).astype(F32), jnp.int32)
    return ia | lax.shift_right_logical(ib, 16)


def _unpack2(p):
    a = lax.bitcast_convert_type(p & jnp.int32(-65536), F32)
    b = lax.bitcast_convert_type(lax.shift_left(p, 16), F32)
    return a, b


def _sigmoid(x):
    return 1.0 / (1.0 + jnp.exp(-x))


def _layer_norm(r, g, b):
    mu = jnp.mean(r, axis=-1, keepdims=True)
    d = r - mu
    var = jnp.mean(d * d, axis=-1, keepdims=True)
    return d * lax.rsqrt(var + LN_EPS) * g + b


def _ada_kernel(c_ref, wh_ref, wl_ref, b_ref, o_ref):
    c = c_ref[...]
    s = c * _sigmoid(c)
    o_ref[...] = _dot3(s, wh_ref[...], wl_ref[...]) + b_ref[...]


def _ada(c, w_ada, b_ada):
    B, D = c.shape
    n_out = w_ada.shape[1]
    rows = 8
    cp = jnp.zeros((rows, D), F32).at[:B].set(c)
    wh, wl = _split(w_ada)
    tn = 1024
    out = pl.pallas_call(
        _ada_kernel,
        out_shape=jax.ShapeDtypeStruct((rows, n_out), F32),
        grid=(n_out // tn,),
        in_specs=[pl.BlockSpec((rows, D), lambda j: (0, 0)),
                  pl.BlockSpec((D, tn), lambda j: (0, j)),
                  pl.BlockSpec((D, tn), lambda j: (0, j)),
                  pl.BlockSpec((1, tn), lambda j: (0, j))],
        out_specs=pl.BlockSpec((rows, tn), lambda j: (0, j)),
        compiler_params=_params(1),
        name="ada",
    )(cp, wh, wl, b_ada.reshape(1, n_out))
    return out[:B].reshape(B, 6, D)


def _inproj_kernel(x_ref, mod_ref, w_ref, q_ref, kv_ref, hy_ref, ga_ref, gh_ref):
    C = HYENA_WIDTH
    x = x_ref[0]
    shift = mod_ref[0, 0:1, :]
    scale = mod_ref[0, 1:2, :]
    h = (x * (1.0 + scale) + shift).astype(BF16)

    def seg(lo, hi):
        return jnp.dot(h, w_ref[:, lo:hi], preferred_element_type=F32)

    o = 0
    q_ref[0] = (seg(o, o + ATTN_WIDTH) * (HEAD_DIM ** -0.5)).astype(BF16)
    o += ATTN_WIDTH
    kv_ref[0] = seg(o, o + 2 * KV_WIDTH).astype(BF16)
    o += 2 * KV_WIDTH
    hy_ref[0] = seg(o, o + 3 * C)
    o += 3 * C
    D = x.shape[-1]
    ga_ref[0] = _sigmoid(seg(o, o + D)).astype(BF16)
    o += D
    gh_ref[0] = _sigmoid(seg(o, o + D)).astype(BF16)


def _in_proj(x, mod, w_in):
    B, S, D = x.shape
    C = HYENA_WIDTH
    tm = min(512, S)
    wb = w_in.astype(BF16)
    nw = wb.shape[1]
    row = lambda b, i: (b, i, 0)
    shapes = [(ATTN_WIDTH, BF16), (2 * KV_WIDTH, BF16), (3 * C, F32), (D, BF16), (D, BF16)]
    return pl.pallas_call(
        _inproj_kernel,
        out_shape=[jax.ShapeDtypeStruct((B, S, w), dt) for w, dt in shapes],
        grid=(B, S // tm),
        in_specs=[pl.BlockSpec((1, tm, D), row),
                  pl.BlockSpec((1, 6, D), lambda b, i: (b, 0, 0)),
                  pl.BlockSpec((D, nw), lambda b, i: (0, 0))],
        out_specs=[pl.BlockSpec((1, tm, w), row) for w, _ in shapes],
        compiler_params=_params(2),
        name="in_proj",
    )(x, mod, wb)


ATT_TQ = 512


def _attn_kernel(sink_ref, q_ref, kvp_ref, kvc_ref, kvn_ref, bias_ref, o_ref, kv_scr, *, seq_len):
    i = pl.program_id(1)
    Q = BLOCK_Q
    TQ = q_ref.shape[1]
    G = N_HEADS // N_KV_HEADS
    kv_scr[0:Q] = kvp_ref[0]
    kv_scr[Q:Q + TQ] = kvc_ref[0]
    kv_scr[Q + TQ:] = kvn_ref[0]
    col = lax.broadcasted_iota(jnp.int32, (1, 3 * Q), 1)
    rhead = lax.broadcasted_iota(jnp.int32, (G * Q, 1), 0) // Q
    for j in range(TQ // Q):
        kpos = i * TQ + (j - 1) * Q + col
        colbias = jnp.where((kpos >= 0) & (kpos < seq_len), 0.0, NEG)
        for kv in range(N_KV_HEADS):
            kk = kv_scr[j * Q:(j + 3) * Q, kv * HEAD_DIM:(kv + 1) * HEAD_DIM]
            vv = kv_scr[j * Q:(j + 3) * Q, KV_WIDTH + kv * HEAD_DIM:KV_WIDTH + (kv + 1) * HEAD_DIM]
            heads = [kv * G + g for g in range(G)]
            qg = jnp.concatenate([q_ref[0, j * Q:(j + 1) * Q, h * HEAD_DIM:(h + 1) * HEAD_DIM] for h in heads], axis=0)
            s = lax.dot_general(qg, kk, (((1,), (1,)), ((), ())), preferred_element_type=F32)
            s = s + bias_ref[kv] + colbias
            snk = jnp.where(rhead == 0, sink_ref[heads[0]],
                            jnp.where(rhead == 1, sink_ref[heads[1]],
                                      jnp.where(rhead == 2, sink_ref[heads[2]], sink_ref[heads[3]])))
            m = jnp.maximum(jnp.max(s, axis=1, keepdims=True), snk)
            p = jnp.exp(s - m)
            den = jnp.sum(p, axis=1, keepdims=True) + jnp.exp(snk - m)
            o = jnp.dot(p.astype(BF16), vv, preferred_element_type=F32) / den
            for g, h in enumerate(heads):
                o_ref[0, j * Q:(j + 1) * Q, h * HEAD_DIM:(h + 1) * HEAD_DIM] = o[g * Q:(g + 1) * Q].astype(BF16)


def _attention(q, kv, sink):
    B, S, _ = q.shape
    Q = BLOCK_Q
    TQ = min(ATT_TQ, S)
    r = TQ // Q
    nq = S // Q
    G = N_HEADS // N_KV_HEADS
    assert G == 4
    a = jnp.arange(Q)[:, None]
    j = jnp.arange(3 * Q)[None, :]
    rel = jnp.abs(j - Q - a).astype(F32)
    slopes = 2.0 ** (-8.0 * jnp.arange(1, N_HEADS + 1, dtype=F32) / N_HEADS)
    bias = jnp.where(rel[None] <= WINDOW, -slopes[:, None, None] * rel[None], NEG).astype(F32)
    bias = bias.reshape(N_KV_HEADS, G * Q, 3 * Q)
    cur = lambda b, i: (b, i, 0)
    return pl.pallas_call(
        functools.partial(_attn_kernel, seq_len=S),
        out_shape=jax.ShapeDtypeStruct((B, S, ATTN_WIDTH), BF16),
        grid=(B, S // TQ),
        in_specs=[pl.BlockSpec(memory_space=pltpu.SMEM),
                  pl.BlockSpec((1, TQ, ATTN_WIDTH), cur),
                  pl.BlockSpec((1, Q, 2 * KV_WIDTH), lambda b, i: (b, jnp.maximum(i * r - 1, 0), 0)),
                  pl.BlockSpec((1, TQ, 2 * KV_WIDTH), cur),
                  pl.BlockSpec((1, Q, 2 * KV_WIDTH), lambda b, i: (b, jnp.minimum((i + 1) * r, nq - 1), 0)),
                  pl.BlockSpec((N_KV_HEADS, G * Q, 3 * Q), lambda b, i: (0, 0, 0))],
        out_specs=pl.BlockSpec((1, TQ, ATTN_WIDTH), cur),
        scratch_shapes=[pltpu.VMEM((TQ + 2 * Q, 2 * KV_WIDTH), BF16)],
        compiler_params=_params(2),
        name="attn",
    )(sink.astype(F32), q, kv, kv, kv, bias)


def _shortconv_kernel(u_ref, p_ref, n_ref, w_ref, b_ref, v_ref, x1_ref, x2_ref):
    i = pl.program_id(1)
    n = pl.num_programs(1)
    C = HYENA_WIDTH
    u = u_ref[0]
    tr = u.shape[0]
    prow = jnp.where(i > 0, p_ref[0, 7:8, :], 0.0)
    nrow = jnp.where(i < n - 1, n_ref[0, 0:1, :], 0.0)
    rid = lax.broadcasted_iota(jnp.int32, (tr, 1), 0)
    up = jnp.where(rid == 0, prow, pltpu.roll(u, 1, 0))
    dn = jnp.where(rid == tr - 1, nrow, pltpu.roll(u, tr - 1, 0))
    out = w_ref[0:1, :] * up + w_ref[1:2, :] * u + w_ref[2:3, :] * dn + b_ref[...]
    v_ref[0] = out[:, :C]
    x1_ref[0] = out[:, C:2 * C]
    x2_ref[0] = out[:, 2 * C:]


def _short_conv(hy_u, conv_w, conv_b):
    B, S, W = hy_u.shape
    C = HYENA_WIDTH
    tr = min(512, S)
    r8 = tr // 8
    nb8 = S // 8
    return pl.pallas_call(
        _shortconv_kernel,
        out_shape=[jax.ShapeDtypeStruct((B, S, C), F32)] * 3,
        grid=(B, S // tr),
        in_specs=[pl.BlockSpec((1, tr, W), lambda b, i: (b, i, 0)),
                  pl.BlockSpec((1, 8, W), lambda b, i: (b, jnp.maximum(i * r8 - 1, 0), 0)),
                  pl.BlockSpec((1, 8, W), lambda b, i: (b, jnp.minimum((i + 1) * r8, nb8 - 1), 0)),
                  pl.BlockSpec((3, W), lambda b, i: (0, 0)),
                  pl.BlockSpec((1, W), lambda b, i: (0, 0))],
        out_specs=[pl.BlockSpec((1, tr, C), lambda b, i: (b, i, 0))] * 3,
        compiler_params=_params(2),
        name="shortconv",
    )(hy_u, hy_u, hy_u, conv_w, conv_b.reshape(1, W))


def _filter_kernel(z_ref, w1h, w1l, b1_ref, f1_ref, w2h, w2l, b2_ref, f2_ref, w3h, w3l, dec_ref,
                   k_ref, s_ref):
    i = pl.program_id(0)
    z = z_ref[...]
    h1 = jnp.sin(f1_ref[...] * (_dot3(z, w1h[...], w1l[...]) + b1_ref[...]))
    h2 = jnp.sin(f2_ref[...] * (_dot3(h1, w2h[...], w2l[...]) + b2_ref[...]))
    k = _dot3(h2, w3h[...], w3l[...])
    t = z[:, 0:1]
    k = k * (jnp.exp(-t * jnp.abs(dec_ref[...])) + WINDOW_SHIFT)
    k_ref[...] = k

    @pl.when(i == 0)
    def _():
        s_ref[...] = jnp.zeros_like(s_ref)

    s_ref[...] += jnp.sum(jnp.abs(k), axis=0, keepdims=True)


def _filter_embedding(L):
    t = np.linspace(0.0, 1.0, L, dtype=np.float32).astype(np.float64)[:, None]
    w = (2.0 * math.pi * np.arange(L, dtype=np.float32) / np.float32(L)).astype(np.float64)[:, None]
    bands = np.linspace(1e-4, FILTER_BANDS - 1, FILTER_BANDS, dtype=np.float32).astype(np.float64)[None, :]
    bw = (bands.astype(np.float32) * w.astype(np.float32)).astype(np.float64)
    z = np.concatenate([t, np.cos(bw), -np.sin(bw)], axis=-1)
    zp = np.zeros((L, LANES), np.float32)
    zp[:, :FILTER_EMB] = z.astype(np.float32)
    return jnp.asarray(zp)


def _pad2(a, r, c):
    return jnp.zeros((r, c), F32).at[:a.shape[0], :a.shape[1]].set(a.astype(F32))


def _filters(L, fw1, fb1, ff1, fw2, fb2, ff2, fw3, decay):
    H = LANES
    nf = fw3.shape[1]
    z = _filter_embedding(L)
    w1h, w1l = _split(_pad2(fw1, H, H))
    w2h, w2l = _split(_pad2(fw2, H, H))
    w3h, w3l = _split(_pad2(fw3, H, nf))
    b1 = _pad2(fb1[None], 1, H)
    f1 = _pad2(ff1[None], 1, H)
    b2 = _pad2(fb2[None], 1, H)
    f2 = _pad2(ff2[None], 1, H)
    tr = min(512, L)
    full = lambda r, c: pl.BlockSpec((r, c), lambda i: (0, 0))
    return pl.pallas_call(
        _filter_kernel,
        out_shape=[jax.ShapeDtypeStruct((L, nf), F32), jax.ShapeDtypeStruct((1, nf), F32)],
        grid=(L // tr,),
        in_specs=[pl.BlockSpec((tr, H), lambda i: (i, 0)),
                  full(H, H), full(H, H), full(1, H), full(1, H),
                  full(H, H), full(H, H), full(1, H), full(1, H),
                  full(H, nf), full(H, nf), full(1, nf)],
        out_specs=[pl.BlockSpec((tr, nf), lambda i: (i, 0)), full(1, nf)],
        compiler_params=_params(1),
        name="filter",
    )(z, w1h, w1l, b1, f1, w2h, w2l, b2, f2, w3h, w3l, decay.reshape(1, nf).astype(F32))


def _np_split(m64):
    hi = m64.astype(np.float32).astype(BF16)
    lo = (m64 - hi.astype(np.float64)).astype(np.float32).astype(BF16)
    return jnp.asarray(hi), jnp.asarray(lo)


def _dft_constants(L):
    N = 2 * L
    n2 = LANES
    n1 = N // n2
    h1 = n1 // 2
    k1 = np.arange(n1)[:, None]
    s1 = np.arange(h1)[None, :]
    ang = -2.0 * np.pi * ((k1 * s1) % n1) / n1
    wr, wi = np.cos(ang), np.sin(ang)
    w1_filt = np.block([[wr, wr], [wi, wi], [wr, -wr], [wi, -wi]])
    w1_cplx = np.block([[wr, -wi], [wi, wr]])
    vr, vi = wr.T / N, -wi.T / N
    w3 = np.block([[vr, -vi], [vi, vr]])
    k2 = np.arange(n2)[:, None]
    s2 = np.arange(n2)[None, :]
    a2 = -2.0 * np.pi * ((k2 * s2) % n2) / n2
    w2r, w2i = jnp.asarray(np.cos(a2), F32), jnp.asarray(np.sin(a2), F32)
    at = -2.0 * np.pi * ((np.arange(n1)[:, None] * s2) % N) / N
    twr, twi = jnp.asarray(np.cos(at), F32), jnp.asarray(np.sin(at), F32)
    mr = w2r[None] * twr[:, None, :] - w2i[None] * twi[:, None, :]
    mi = w2r[None] * twi[:, None, :] + w2i[None] * twr[:, None, :]
    fwd = jnp.concatenate([jnp.concatenate([mr, -mi], axis=2),
                           jnp.concatenate([mi, mr], axis=2)], axis=1)
    fwd_hi, fwd_lo = _split(fwd)
    return dict(n1=n1, w1_filt=_np_split(w1_filt), w1_cplx=_np_split(w1_cplx)[0], w3=_np_split(w3)[0],
                fwd=(fwd_hi, fwd_lo), inv_hi=jnp.swapaxes(fwd_hi, 1, 2))


SCH = 8


def _dft1_kernel(x_ref, w_ref, a_ref, *, n1):
    w = w_ref[...]
    for j in range(SCH):
        rhs = jnp.concatenate([x_ref[0, 0, :, j, :], x_ref[0, 1, :, j, :]], axis=0)
        res = jnp.dot(w, rhs.astype(BF16), preferred_element_type=F32)
        a_ref[0, :, j, :] = _pack2(res[:n1], res[n1:])


def _dft1_data(x, consts):
    B, L, C = x.shape
    n1 = consts["n1"]
    h1 = n1 // 2
    xv = x.reshape(B // 2, 2, h1, LANES, C)
    return pl.pallas_call(
        functools.partial(_dft1_kernel, n1=n1),
        out_shape=jax.ShapeDtypeStruct((B // 2, n1, LANES, C), jnp.int32),
        grid=(B // 2, LANES // SCH),
        in_specs=[pl.BlockSpec((1, 2, h1, SCH, C), lambda p, j: (p, 0, 0, j, 0)),
                  pl.BlockSpec((2 * n1, n1), lambda p, j: (0, 0))],
        out_specs=pl.BlockSpec((1, n1, SCH, C), lambda p, j: (p, 0, j, 0)),
        compiler_params=_params(2),
        name="dft1",
    )(xv, consts["w1_cplx"])


def _dft1f_kernel(x_ref, wh_ref, wl_ref, a_ref, *, n1):
    C = HYENA_WIDTH
    wh, wl = wh_ref[...], wl_ref[...]
    for j in range(SCH):
        rhs = jnp.concatenate([x_ref[:, j, :C], x_ref[:, j, C:]], axis=0)
        res = _ldot3(wh, wl, rhs)
        for p in range(4):
            a_ref[0, :, p, j, :] = res[p * n1:(p + 1) * n1]


def _dft1_filter(kraw, consts):
    L, nf = kraw.shape
    C = HYENA_WIDTH
    n_ord = nf // (2 * C)
    n1 = consts["n1"]
    h1 = n1 // 2
    kv = kraw.reshape(h1, LANES, nf)
    wh, wl = consts["w1_filt"]
    return pl.pallas_call(
        functools.partial(_dft1f_kernel, n1=n1),
        out_shape=jax.ShapeDtypeStruct((n_ord, n1, 4, LANES, C), F32),
        grid=(n_ord, LANES // SCH),
        in_specs=[pl.BlockSpec((h1, SCH, 2 * C), lambda o, j: (0, j, o)),
                  pl.BlockSpec((4 * n1, n1), lambda o, j: (0, 0)),
                  pl.BlockSpec((4 * n1, n1), lambda o, j: (0, 0))],
        out_specs=pl.BlockSpec((1, n1, 4, SCH, C), lambda o, j: (o, 0, 0, j, 0)),
        compiler_params=_params(2),
        name="dft1f",
    )(kv, wh, wl)


KCH = 8


def _midf_kernel(a_ref, fh_ref, fl_ref, inv_ref, b0_ref, h_ref):
    n2 = LANES
    sc = inv_ref[0]
    for k in range(KCH):
        h_re = _ldot3(fh_ref[k, :n2, :], fl_ref[k, :n2, :], a_ref[0, k, :2 * n2, :])
        h_im = _ldot3(fh_ref[k, n2:, :], fl_ref[k, n2:, :], a_ref[0, k, 2 * n2:, :])
        h_ref[0, k, :n2, :] = (h_re - b0_ref[0]) * sc
        h_ref[0, k, n2:, :] = h_im * sc


def _filter_spectrum(af, inv_den, bwd0, consts):
    n_ord, n1, _, n2, C = af.shape
    a = af.reshape(n_ord, n1, 4 * n2, C)
    fh, fl = consts["fwd"]
    tab = pl.BlockSpec((KCH, 2 * n2, 2 * n2), lambda k, o: (k, 0, 0))
    vec = pl.BlockSpec((1, 1, C), lambda k, o: (o, 0, 0))
    return pl.pallas_call(
        _midf_kernel,
        out_shape=jax.ShapeDtypeStruct((n_ord, n1, 2 * n2, C), F32),
        grid=(n1 // KCH, n_ord),
        in_specs=[pl.BlockSpec((1, KCH, 4 * n2, C), lambda k, o: (o, k, 0, 0)), tab, tab, vec, vec],
        out_specs=pl.BlockSpec((1, KCH, 2 * n2, C), lambda k, o: (o, k, 0, 0)),
        compiler_params=_params(2),
        name="midf",
    )(a, fh, fl, inv_den, bwd0)


def _mid_kernel(a_ref, f_ref, i_ref, h_ref, b_ref):
    n2 = LANES
    for k in range(KCH):
        a = jnp.concatenate(_unpack2(a_ref[0, k]), axis=0).astype(BF16)
        x = jnp.dot(f_ref[k], a, preferred_element_type=F32)
        xr, xi = x[:n2], x[n2:]
        hr, hi = h_ref[0, k, :n2, :], h_ref[0, k, n2:, :]
        y = jnp.concatenate([xr * hr - xi * hi, xr * hi + xi * hr], axis=0)
        b = jnp.dot(i_ref[k], y.astype(BF16), preferred_element_type=F32)
        b_ref[0, k] = _pack2(b[:n2], b[n2:])


def _mid(a, hspec, order, consts):
    P, n1, n2, C = a.shape
    tab = pl.BlockSpec((KCH, 2 * n2, 2 * n2), lambda k, p: (k, 0, 0))
    return pl.pallas_call(
        _mid_kernel,
        out_shape=jax.ShapeDtypeStruct((P, n1, n2, C), jnp.int32),
        grid=(n1 // KCH, P),
        in_specs=[pl.BlockSpec((1, KCH, n2, C), lambda k, p: (p, k, 0, 0)),
                  tab, tab,
                  pl.BlockSpec((1, KCH, 2 * n2, C), lambda k, p: (order, k, 0, 0))],
        out_specs=pl.BlockSpec((1, KCH, n2, C), lambda k, p: (p, k, 0, 0)),
        compiler_params=_params(2),
        name="mid",
    )(a, consts["fwd"][0], consts["inv_hi"], hspec)


def _dft3_kernel(b_ref, w_ref, v_ref, g_ref, skip_ref, z_ref, *, h1):
    w = w_ref[...]
    skip = skip_ref[0]
    for j in range(SCH):
        rhs = jnp.concatenate(_unpack2(b_ref[0, :, j, :]), axis=0)
        y = jnp.dot(w, rhs.astype(BF16), preferred_element_type=F32)
        for r in range(2):
            yr = y[r * h1:(r + 1) * h1]
            z_ref[0, r, :, j, :] = g_ref[0, r, :, j, :] * (yr + v_ref[0, r, :, j, :] * skip)


def _dft3_gate(b5, v, gate, skip, consts):
    P, n1, n2, C = b5.shape
    h1 = n1 // 2
    B, L, _ = v.shape
    five = lambda t: t.reshape(P, 2, h1, n2, C)
    dat = pl.BlockSpec((1, 2, h1, SCH, C), lambda p, j: (p, 0, 0, j, 0))
    out = pl.pallas_call(
        functools.partial(_dft3_kernel, h1=h1),
        out_shape=jax.ShapeDtypeStruct((P, 2, h1, n2, C), F32),
        grid=(P, n2 // SCH),
        in_specs=[pl.BlockSpec((1, n1, SCH, C), lambda p, j: (p, 0, j, 0)),
                  pl.BlockSpec((n1, 2 * n1), lambda p, j: (0, 0)),
                  dat, dat,
                  pl.BlockSpec((1, C), lambda p, j: (0, 0))],
        out_specs=dat,
        compiler_params=_params(2),
        name="dft3",
    )(b5, consts["w3"], five(v), five(gate), skip.reshape(1, C).astype(F32))
    return out.reshape(B, L, C)


def _hyena(hy_u, conv_w, conv_b, fw1, fb1, ff1, fw2, fb2, ff2, fw3, decay, skip):
    B, L, _ = hy_u.shape
    C = HYENA_WIDTH
    consts = _dft_constants(L)
    v, x1, x2 = _short_conv(hy_u, conv_w, conv_b)
    kraw, ksum = _filters(L, fw1, fb1, ff1, fw2, fb2, ff2, fw3, decay)
    ks = ksum.reshape(2, 2, C)
    inv_den = (1.0 / (ks[:, 0] + ks[:, 1])).reshape(2, 1, C)
    bwd0 = kraw[0].reshape(2, 2, C)[:, 1].reshape(2, 1, C)
    hspec = _filter_spectrum(_dft1_filter(kraw, consts), inv_den, bwd0, consts)
    z = v
    for o, gate in enumerate((x1, x2)):
        a5 = _dft1_data(z, consts)
        b5 = _mid(a5, hspec, o, consts)
        z = _dft3_gate(b5, z, gate, skip[o], consts)
    return z


def _merge_kernel(attn_ref, hy_ref, ga_ref, gh_ref, x_ref, mod_ref, wa_ref, wh_ref, wo_ref,
                  g1_ref, b1_ref, rwh_ref, rwl_ref, rb_ref, tri_ref,
                  x1_ref, h2_ref, route_ref, wts_ref, cnt_ref, carry_ref):
    first = (pl.program_id(0) == 0) & (pl.program_id(1) == 0)

    @pl.when(first)
    def _():
        carry_ref[...] = jnp.zeros_like(carry_ref)

    a = jnp.dot(attn_ref[0], wa_ref[...], preferred_element_type=F32)
    hy = jnp.dot(hy_ref[0].astype(BF16), wh_ref[...], preferred_element_type=F32)
    merged = ga_ref[0].astype(F32) * a + gh_ref[0].astype(F32) * hy
    y = jnp.dot(merged.astype(BF16), wo_ref[...], preferred_element_type=F32)
    gate1 = mod_ref[0, 2:3, :]
    shift2 = mod_ref[0, 3:4, :]
    scale2 = mod_ref[0, 4:5, :]
    x1 = _layer_norm(DN_ALPHA * x_ref[0] + gate1 * y, g1_ref[...], b1_ref[...])
    x1_ref[0] = x1
    h2 = x1 * (1.0 + scale2) + shift2
    half = h2.shape[1] // 2
    h2_ref[0] = _pack2(h2[:, :half], h2[:, half:])

    logits = _dot3(h2, rwh_ref[...], rwl_ref[...]) + rb_ref[...]
    tm = logits.shape[0]
    lane = lax.broadcasted_iota(jnp.int32, (tm, LANES), 1)
    lanef = lane.astype(F32)
    big = float(LANES)

    def first_lane(mask):
        return jnp.min(jnp.where(mask, lanef, big), axis=1, keepdims=True).astype(jnp.int32)

    gmask = lane < N_GROUPS
    gl = jnp.where(gmask, logits, NEG)
    gmax = jnp.max(gl, axis=1, keepdims=True)
    gidx = first_lane(gl == gmax)
    pg = 1.0 / jnp.sum(jnp.exp(gl - gmax), axis=1, keepdims=True)
    lo = ROUTE_OFF + gidx * EXPERTS_PER_GROUP
    emask = (lane >= lo) & (lane < lo + EXPERTS_PER_GROUP)
    el = jnp.where(emask, logits, NEG)
    v1 = jnp.max(el, axis=1, keepdims=True)
    i1 = first_lane(el == v1)
    el2 = jnp.where(emask & (lane != i1), logits, NEG)
    v2 = jnp.max(el2, axis=1, keepdims=True)
    i2 = first_lane(el2 == v2)
    e21 = jnp.exp(v2 - v1)
    w1 = pg / (1.0 + e21)
    w2 = pg * e21 / (1.0 + e21)

    sel1 = lane == i1
    sel2 = lane == i2
    onehot = jnp.where(sel1 | sel2, 1.0, 0.0)
    prefix = jnp.dot(tri_ref[...], onehot.astype(BF16), preferred_element_type=F32) + carry_ref[...]
    r1 = jnp.sum(jnp.where(sel1, prefix, 0.0), axis=1, keepdims=True)
    r2 = jnp.sum(jnp.where(sel2, prefix, 0.0), axis=1, keepdims=True)
    carry_ref[...] += jnp.sum(onehot, axis=0, keepdims=True)
    cnt_ref[...] = carry_ref[...]

    ranks = jnp.where(lane == 2, r1, jnp.where(lane == 3, r2, 0.0)).astype(jnp.int32)
    route_ref[0] = jnp.where(lane == 0, i1 - ROUTE_OFF, jnp.where(lane == 1, i2 - ROUTE_OFF, ranks))
    wts_ref[0] = jnp.where(lane == 0, w1, jnp.where(lane == 1, w2, 0.0))


def _merge(attn, hy, ga, gh, x, mod, w_attn_o, w_hy_o, w_out, ln1_g, ln1_b, rg_w, rg_b, re_w, re_b):
    B, S, D = x.shape
    tm = min(512, S)
    rw = jnp.zeros((D, LANES), F32).at[:, :N_GROUPS].set(rg_w).at[:, ROUTE_OFF:ROUTE_OFF + N_EXPERTS].set(re_w)
    rb = jnp.zeros((1, LANES), F32).at[0, :N_GROUPS].set(rg_b).at[0, ROUTE_OFF:ROUTE_OFF + N_EXPERTS].set(re_b)
    rwh, rwl = _split(rw)
    tri = (jnp.arange(tm)[:, None] > jnp.arange(tm)[None, :]).astype(BF16)
    row = lambda b, i: (b, i, 0)
    full = lambda r, c: pl.BlockSpec((r, c), lambda b, i: (0, 0))
    outs = [jax.ShapeDtypeStruct((B, S, D), F32), jax.ShapeDtypeStruct((B, S, D // 2), jnp.int32),
            jax.ShapeDtypeStruct((B, S, LANES), jnp.int32), jax.ShapeDtypeStruct((B, S, LANES), F32),
            jax.ShapeDtypeStruct((1, LANES), F32)]
    return pl.pallas_call(
        _merge_kernel,
        out_shape=outs,
        grid=(B, S // tm),
        in_specs=[pl.BlockSpec((1, tm, ATTN_WIDTH), row), pl.BlockSpec((1, tm, HYENA_WIDTH), row),
                  pl.BlockSpec((1, tm, D), row), pl.BlockSpec((1, tm, D), row), pl.BlockSpec((1, tm, D), row),
                  pl.BlockSpec((1, 6, D), lambda b, i: (b, 0, 0)),
                  full(ATTN_WIDTH, D), full(HYENA_WIDTH, D), full(D, D),
                  full(1, D), full(1, D), full(D, LANES), full(D, LANES), full(1, LANES), full(tm, tm)],
        out_specs=[pl.BlockSpec((1, tm, D), row), pl.BlockSpec((1, tm, D // 2), row),
                   pl.BlockSpec((1, tm, LANES), row), pl.BlockSpec((1, tm, LANES), row), full(1, LANES)],
        scratch_shapes=[pltpu.VMEM((1, LANES), F32)],
        compiler_params=_params(2),
        name="merge",
    )(attn, hy, ga, gh, x, mod, w_attn_o.astype(BF16), w_hy_o.astype(BF16), w_out.astype(BF16),
      ln1_g.reshape(1, D), ln1_b.reshape(1, D), rwh, rwl, rb, tri)


TOK_CH = 1024
WAIT_ROWS = 128


def _dispatch_kernel(poffs_ref, psz_ref, dest_ref, h_ref, xb_ref, zero_ref, sem, zsem):
    @pl.when(pl.program_id(0) == 0)
    def _():
        zero_ref[...] = jnp.zeros_like(zero_ref)

        def zcopy(e):
            start = pl.multiple_of(poffs_ref[e] + psz_ref[e] - MOE_BLOCK, MOE_BLOCK)
            return pltpu.make_async_copy(zero_ref, xb_ref.at[pl.ds(start, MOE_BLOCK)], zsem)

        def zissue(e, c):
            @pl.when(psz_ref[e] > 0)
            def _():
                zcopy(e).start()
            return c

        def zdrain(e, c):
            @pl.when(psz_ref[e] > 0)
            def _():
                zcopy(e).wait()
            return c

        lax.fori_loop(0, N_EXPERTS, zissue, 0)
        lax.fori_loop(0, N_EXPERTS, zdrain, 0)

        def tcopy(b):
            return pltpu.make_async_copy(zero_ref, xb_ref.at[pl.ds(pl.multiple_of(b * MOE_BLOCK, MOE_BLOCK), MOE_BLOCK)], zsem)

        first_free = (poffs_ref[N_EXPERTS - 1] + psz_ref[N_EXPERTS - 1]) // MOE_BLOCK
        n_blocks = xb_ref.shape[0] // MOE_BLOCK
        lax.fori_loop(first_free, n_blocks, lambda b, c: (tcopy(b).start(), c)[1], 0)
        lax.fori_loop(first_free, n_blocks, lambda b, c: (tcopy(b).wait(), c)[1], 0)

    def issue(r, c):
        for j in range(2):
            pltpu.make_async_copy(h_ref.at[pl.ds(r, 1)], xb_ref.at[pl.ds(dest_ref[0, 0, 2 * r + j], 1)], sem).start(priority=j)
        return c

    lax.fori_loop(0, TOK_CH, issue, 0, unroll=8)

    def drain(r, c):
        pltpu.make_async_copy(xb_ref.at[pl.ds(0, WAIT_ROWS)], xb_ref.at[pl.ds(0, WAIT_ROWS)], sem).wait()
        return c

    lax.fori_loop(0, 2 * TOK_CH // WAIT_ROWS, drain, 0)


def _dispatch(h2, dest, poffs, psizes, P):
    T, W = h2.shape
    nch = T // TOK_CH
    grid_spec = pltpu.PrefetchScalarGridSpec(
        num_scalar_prefetch=2,
        grid=(nch,),
        in_specs=[pl.BlockSpec((1, 1, 2 * TOK_CH), lambda i, po, ps: (i, 0, 0), memory_space=pltpu.SMEM),
                  pl.BlockSpec((TOK_CH, W), lambda i, po, ps: (i, 0))],
        out_specs=pl.BlockSpec(memory_space=pl.ANY),
        scratch_shapes=[pltpu.VMEM((MOE_BLOCK, W), h2.dtype), pltpu.SemaphoreType.DMA(()), pltpu.SemaphoreType.DMA(())],
    )
    return pl.pallas_call(
        _dispatch_kernel,
        out_shape=jax.ShapeDtypeStruct((P, W), h2.dtype),
        grid_spec=grid_spec,
        compiler_params=_params(1),
        name="dispatch",
    )(poffs, psizes, dest, h2)


def _expert_kernel(be_ref, nu_ref, x_ref, w1_ref, w3_ref, w2_ref, y_ref, c1_ref, c3_ref, c2_ref):
    i = pl.program_id(0)
    used = i < nu_ref[0]
    fresh = (i == 0) | (be_ref[i] != be_ref[jnp.maximum(i - 1, 0)])

    @pl.when(used & fresh)
    def _():
        c1_ref[...] = w1_ref[0].astype(BF16)
        c3_ref[...] = w3_ref[0].astype(BF16)
        c2_ref[...] = w2_ref[0].astype(BF16)

    @pl.when(used)
    def _():
        xa, xb = _unpack2(x_ref[...])
        x = jnp.concatenate([xa, xb], axis=1).astype(BF16)
        a = jnp.dot(x, c1_ref[...], preferred_element_type=F32)
        g = jnp.dot(x, c3_ref[...], preferred_element_type=F32)
        hmid = (a * _sigmoid(a) * g).astype(BF16)
        y = jnp.dot(hmid, c2_ref[...], preferred_element_type=F32)
        half = y.shape[1] // 2
        y_ref[...] = _pack2(y[:, :half], y[:, half:])

    @pl.when(jnp.logical_not(used))
    def _():
        y_ref[...] = jnp.zeros_like(y_ref)


def _experts(xb, blk_e, n_used, w1, w3, w2):
    P, W = xb.shape
    E, D, DE = w1.shape
    nblk = P // MOE_BLOCK
    grid_spec = pltpu.PrefetchScalarGridSpec(
        num_scalar_prefetch=2,
        grid=(nblk,),
        in_specs=[pl.BlockSpec((MOE_BLOCK, W), lambda i, be, nu: (jnp.minimum(i, jnp.maximum(nu[0] - 1, 0)), 0)),
                  pl.BlockSpec((1, D, DE), lambda i, be, nu: (be[i], 0, 0)),
                  pl.BlockSpec((1, D, DE), lambda i, be, nu: (be[i], 0, 0)),
                  pl.BlockSpec((1, DE, D), lambda i, be, nu: (be[i], 0, 0))],
        out_specs=pl.BlockSpec((MOE_BLOCK, W), lambda i, be, nu: (i, 0)),
        scratch_shapes=[pltpu.VMEM((D, DE), BF16), pltpu.VMEM((D, DE), BF16), pltpu.VMEM((DE, D), BF16)],
    )
    return pl.pallas_call(
        _expert_kernel,
        out_shape=jax.ShapeDtypeStruct((P, W), jnp.int32),
        grid_spec=grid_spec,
        compiler_params=_params(1),
        name="experts",
    )(blk_e, n_used, xb, w1, w3, w2)


def _combine_kernel(dest_ref, wts_ref, x1_ref, mod_ref, g_ref, b_ref, yb_ref, o_ref, rows_ref, sem):
    def issue(r, c):
        for j in range(2):
            pltpu.make_async_copy(yb_ref.at[pl.ds(dest_ref[0, 0, 2 * r + j], 1)], rows_ref.at[j, pl.ds(r, 1)], sem).start(priority=j)
        return c

    lax.fori_loop(0, TOK_CH, issue, 0, unroll=8)

    def drain(r, c):
        pltpu.make_async_copy(yb_ref.at[pl.ds(0, WAIT_ROWS)], rows_ref.at[0, pl.ds(0, WAIT_ROWS)], sem).wait()
        return c

    lax.fori_loop(0, 2 * TOK_CH // WAIT_ROWS, drain, 0)
    w = wts_ref[...]
    y0 = jnp.concatenate(_unpack2(rows_ref[0]), axis=1)
    y1 = jnp.concatenate(_unpack2(rows_ref[1]), axis=1)
    y = w[:, 0:1] * y0 + w[:, 1:2] * y1
    gate2 = mod_ref[0, 5:6, :]
    o_ref[...] = _layer_norm(DN_ALPHA * x1_ref[...] + gate2 * y, g_ref[...], b_ref[...])


def _combine(yb, dest, wts, x1, mod, ln2_g, ln2_b, S):
    T, D = x1.shape
    nch = T // TOK_CH
    per_b = S // TOK_CH
    return pl.pallas_call(
        _combine_kernel,
        out_shape=jax.ShapeDtypeStruct((T, D), F32),
        grid=(nch,),
        in_specs=[pl.BlockSpec((1, 1, 2 * TOK_CH), lambda i: (i, 0, 0), memory_space=pltpu.SMEM),
                  pl.BlockSpec((TOK_CH, LANES), lambda i: (i, 0)),
                  pl.BlockSpec((TOK_CH, D), lambda i: (i, 0)),
                  pl.BlockSpec((1, 6, D), lambda i: (i // per_b, 0, 0)),
                  pl.BlockSpec((1, D), lambda i: (0, 0)),
                  pl.BlockSpec((1, D), lambda i: (0, 0)),
                  pl.BlockSpec(memory_space=pl.ANY)],
        out_specs=pl.BlockSpec((TOK_CH, D), lambda i: (i, 0)),
        scratch_shapes=[pltpu.VMEM((2, TOK_CH, yb.shape[1]), yb.dtype), pltpu.SemaphoreType.DMA(())],
        compiler_params=_params(1),
        name="combine",
    )(dest, wts, x1, mod, ln2_g.reshape(1, D), ln2_b.reshape(1, D), yb)


def _moe(h2, x1, route, wts, counts, mod, w1, w3, w2, ln2_g, ln2_b):
    B, S, D = x1.shape
    T = B * S
    P = 2 * T + N_EXPERTS * MOE_BLOCK
    nblk = P // MOE_BLOCK
    nch = T // TOK_CH
    sizes = counts[0, ROUTE_OFF:ROUTE_OFF + N_EXPERTS].astype(jnp.int32)
    psizes = (sizes + MOE_BLOCK - 1) // MOE_BLOCK * MOE_BLOCK
    pends = jnp.cumsum(psizes)
    poffs = pends - psizes
    starts = jnp.arange(nblk, dtype=jnp.int32) * MOE_BLOCK
    blk_e = jnp.minimum(jnp.sum((pends[None, :] <= starts[:, None]).astype(jnp.int32), axis=1), N_EXPERTS - 1)
    n_used = pends[-1:] // MOE_BLOCK
    r4 = route.reshape(T, LANES)[:, :4]
    sel = r4[:, :2, None] == jnp.arange(N_EXPERTS, dtype=jnp.int32)[None, None, :]
    dest = (r4[:, 2:4] + jnp.sum(jnp.where(sel, poffs[None, None, :], 0), axis=-1)).reshape(nch, 1, 2 * TOK_CH)
    xb = _dispatch(h2.reshape(T, D // 2), dest, poffs, psizes, P)
    yb = _experts(xb, blk_e, n_used, w1, w3, w2)
    out = _combine(yb, dest, wts.reshape(T, LANES), x1.reshape(T, D), mod, ln2_g, ln2_b, S)
    return out.reshape(B, S, D)


def _layer(x, c, w_ada, b_ada, w_in, conv_w, conv_b, fw1, fb1, ff1, fw2, fb2, ff2, fw3, decay, skip,
           w_hy_o, w_attn_o, attn_sink, w_out, ln1_g, ln1_b, rg_w, rg_b, re_w, re_b, ew1, ew3, ew2,
           ln2_g, ln2_b):
    mod = _ada(c, w_ada, b_ada)
    q, kv, hy_u, ga, gh = _in_proj(x, mod, w_in)
    attn = _attention(q, kv, attn_sink)
    hy = _hyena(hy_u, conv_w, conv_b, fw1, fb1, ff1, fw2, fb2, ff2, fw3, decay, skip)
    x1, h2, route, wts, counts = _merge(attn, hy, ga, gh, x, mod, w_attn_o, w_hy_o, w_out,
                                        ln1_g, ln1_b, rg_w, rg_b, re_w, re_b)
    return _moe(h2, x1, route, wts, counts, mod, ew1, ew3, ew2, ln2_g, ln2_b)


def kernel(x, c, w_ada, b_ada, w_in, conv_w, conv_b, filt_w1, filt_b1, filt_freq1, filt_w2, filt_b2, filt_freq2, filt_w3, filt_decay, hy_skip, w_hy_o, w_attn_o, attn_sink, w_out, ln1_g, ln1_b, router_group_w, router_group_b, router_expert_w, router_expert_b, exp_w1, exp_w3, exp_w2, ln2_g, ln2_b):
    for l in range(w_ada.shape[0]):
        x = _layer(x, c, w_ada[l], b_ada[l], w_in[l], conv_w[l], conv_b[l], filt_w1[l], filt_b1[l],
                   filt_freq1[l], filt_w2[l], filt_b2[l], filt_freq2[l], filt_w3[l], filt_decay[l],
                   hy_skip[l], w_hy_o[l], w_attn_o[l], attn_sink[l], w_out[l], ln1_g[l], ln1_b[l],
                   router_group_w[l], router_group_b[l], router_expert_w[l], router_expert_b[l],
                   exp_w1[l], exp_w3[l], exp_w2[l], ln2_g[l], ln2_b[l])
    return x
```

```python
import functools
import math

import numpy as np
import jax
import jax.numpy as jnp
from jax import lax
from jax.experimental import pallas as pl
from jax.experimental.pallas import tpu as pltpu

F32 = jnp.float32
BF16 = jnp.bfloat16

N_HEADS = 8
N_KV_HEADS = 2
HEAD_DIM = 64
ATTN_WIDTH = N_HEADS * HEAD_DIM
KV_WIDTH = N_KV_HEADS * HEAD_DIM
WINDOW = 128
BLOCK_Q = 128
HYENA_WIDTH = 512
FILTER_EMB = 33
FILTER_BANDS = (FILTER_EMB - 1) // 2
WINDOW_SHIFT = 0.05
N_GROUPS = 8
EXPERTS_PER_GROUP = 8
N_EXPERTS = N_GROUPS * EXPERTS_PER_GROUP
D_EXPERT = 512
MOE_BLOCK = 512
LN_EPS = 1e-5
DEPTH = 1
DN_ALPHA = (2.0 * DEPTH) ** 0.25
NEG = -1e30

LANES = 128
ROUTE_OFF = N_GROUPS
VMEM_LIMIT = 56 * 1024 * 1024


def _params(n_axes, vmem=VMEM_LIMIT):
    return pltpu.CompilerParams(dimension_semantics=("arbitrary",) * n_axes, vmem_limit_bytes=vmem)


def _split(a):
    hi = a.astype(BF16)
    lo = (a - hi.astype(F32)).astype(BF16)
    return hi, lo


def _dot3(a, b_hi, b_lo):
    a_hi, a_lo = _split(a)
    acc = jnp.dot(a_hi, b_hi, preferred_element_type=F32)
    acc = acc + jnp.dot(a_hi, b_lo, preferred_element_type=F32)
    acc = acc + jnp.dot(a_lo, b_hi, preferred_element_type=F32)
    return acc


def _pack2(a, b):
    ia = lax.bitcast_convert_type(a.astype(BF16).astype(F32), jnp.int32)
    ib = lax.bitcast_convert_type(b.astype(BF16).astype(F32), jnp.int32)
    return lax.bitcast_convert_type(ia | lax.shift_right_logical(ib, 16), F32)


def _unpack2(p):
    p = lax.bitcast_convert_type(p, jnp.int32)
    a = lax.bitcast_convert_type(p & jnp.int32(-65536), F32)
    b = lax.bitcast_convert_type(lax.shift_left(p, 16), F32)
    return a, b


def _sigmoid(x):
    return 1.0 / (1.0 + jnp.exp(-x))


def _layer_norm(r, g, b):
    mu = jnp.mean(r, axis=-1, keepdims=True)
    d = r - mu
    var = jnp.mean(d * d, axis=-1, keepdims=True)
    return d * lax.rsqrt(var + LN_EPS) * g + b


def _ada_kernel(c_ref, wh_ref, wl_ref, b_ref, o_ref):
    c = c_ref[...]
    s = c * _sigmoid(c)
    o_ref[...] = _dot3(s, wh_ref[...], wl_ref[...]) + b_ref[...]


def _ada(c, w_ada, b_ada):
    B, D = c.shape
    n_out = w_ada.shape[1]
    rows = 8
    cp = jnp.zeros((rows, D), F32).at[:B].set(c)
    wh, wl = _split(w_ada)
    tn = 1024
    out = pl.pallas_call(
        _ada_kernel,
        out_shape=jax.ShapeDtypeStruct((rows, n_out), F32),
        grid=(n_out // tn,),
        in_specs=[pl.BlockSpec((rows, D), lambda j: (0, 0)),
                  pl.BlockSpec((D, tn), lambda j: (0, j)),
                  pl.BlockSpec((D, tn), lambda j: (0, j)),
                  pl.BlockSpec((1, tn), lambda j: (0, j))],
        out_specs=pl.BlockSpec((rows, tn), lambda j: (0, j)),
        compiler_params=_params(1),
        name="ada",
    )(cp, wh, wl, b_ada.reshape(1, n_out))
    return out[:B].reshape(B, 6, D)


def _inproj_kernel(x_ref, mod_ref, w_ref, q_ref, kv_ref, hy_ref, ga_ref, gh_ref):
    C = HYENA_WIDTH
    x = x_ref[0]
    shift = mod_ref[0, 0:1, :]
    scale = mod_ref[0, 1:2, :]
    h = (x * (1.0 + scale) + shift).astype(BF16)

    def seg(lo, hi):
        return jnp.dot(h, w_ref[:, lo:hi], preferred_element_type=F32)

    o = 0
    q_ref[0] = (seg(o, o + ATTN_WIDTH) * (HEAD_DIM ** -0.5)).astype(BF16)
    o += ATTN_WIDTH
    kv_ref[0] = seg(o, o + 2 * KV_WIDTH).astype(BF16)
    o += 2 * KV_WIDTH
    hy_ref[0] = seg(o, o + 3 * C)
    o += 3 * C
    D = x.shape[-1]
    ga_ref[0] = _sigmoid(seg(o, o + D)).astype(BF16)
    o += D
    gh_ref[0] = _sigmoid(seg(o, o + D)).astype(BF16)


def _in_proj(x, mod, w_in):
    B, S, D = x.shape
    C = HYENA_WIDTH
    tm = min(512, S)
    wb = w_in.astype(BF16)
    nw = wb.shape[1]
    row = lambda b, i: (b, i, 0)
    shapes = [(ATTN_WIDTH, BF16), (2 * KV_WIDTH, BF16), (3 * C, F32), (D, BF16), (D, BF16)]
    return pl.pallas_call(
        _inproj_kernel,
        out_shape=[jax.ShapeDtypeStruct((B, S, w), dt) for w, dt in shapes],
        grid=(B, S // tm),
        in_specs=[pl.BlockSpec((1, tm, D), row),
                  pl.BlockSpec((1, 6, D), lambda b, i: (b, 0, 0)),
                  pl.BlockSpec((D, nw), lambda b, i: (0, 0))],
        out_specs=[pl.BlockSpec((1, tm, w), row) for w, _ in shapes],
        compiler_params=_params(2),
        name="in_proj",
    )(x, mod, wb)


ATT_TQ = 512


def _attn_kernel(sink_ref, q_ref, kvp_ref, kvc_ref, kvn_ref, bias_ref, o_ref, kv_scr, *, seq_len):
    i = pl.program_id(1)
    Q = BLOCK_Q
    TQ = q_ref.shape[1]
    G = N_HEADS // N_KV_HEADS
    kv_scr[0:Q] = kvp_ref[0]
    kv_scr[Q:Q + TQ] = kvc_ref[0]
    kv_scr[Q + TQ:] = kvn_ref[0]
    col = lax.broadcasted_iota(jnp.int32, (1, 3 * Q), 1)
    rhead = lax.broadcasted_iota(jnp.int32, (G * Q, 1), 0) // Q
    for j in range(TQ // Q):
        kpos = i * TQ + (j - 1) * Q + col
        colbias = jnp.where((kpos >= 0) & (kpos < seq_len), 0.0, NEG)
        for kv in range(N_KV_HEADS):
            kk = kv_scr[j * Q:(j + 3) * Q, kv * HEAD_DIM:(kv + 1) * HEAD_DIM]
            vv = kv_scr[j * Q:(j + 3) * Q, KV_WIDTH + kv * HEAD_DIM:KV_WIDTH + (kv + 1) * HEAD_DIM]
            heads = [kv * G + g for g in range(G)]
            qg = jnp.concatenate([q_ref[0, j * Q:(j + 1) * Q, h * HEAD_DIM:(h + 1) * HEAD_DIM] for h in heads], axis=0)
            s = lax.dot_general(qg, kk, (((1,), (1,)), ((), ())), preferred_element_type=F32)
            s = s + bias_ref[kv] + colbias
            snk = jnp.where(rhead == 0, sink_ref[heads[0]],
                            jnp.where(rhead == 1, sink_ref[heads[1]],
                                      jnp.where(rhead == 2, sink_ref[heads[2]], sink_ref[heads[3]])))
            m = jnp.maximum(jnp.max(s, axis=1, keepdims=True), snk)
            p = jnp.exp(s - m)
            den = jnp.sum(p, axis=1, keepdims=True) + jnp.exp(snk - m)
            o = jnp.dot(p.astype(BF16), vv, preferred_element_type=F32) / den
            for g, h in enumerate(heads):
                o_ref[0, j * Q:(j + 1) * Q, h * HEAD_DIM:(h + 1) * HEAD_DIM] = o[g * Q:(g + 1) * Q].astype(BF16)


def _attention(q, kv, sink):
    B, S, _ = q.shape
    Q = BLOCK_Q
    TQ = min(ATT_TQ, S)
    r = TQ // Q
    nq = S // Q
    G = N_HEADS // N_KV_HEADS
    assert G == 4
    a = jnp.arange(Q)[:, None]
    j = jnp.arange(3 * Q)[None, :]
    rel = jnp.abs(j - Q - a).astype(F32)
    slopes = 2.0 ** (-8.0 * jnp.arange(1, N_HEADS + 1, dtype=F32) / N_HEADS)
    bias = jnp.where(rel[None] <= WINDOW, -slopes[:, None, None] * rel[None], NEG).astype(F32)
    bias = bias.reshape(N_KV_HEADS, G * Q, 3 * Q)
    cur = lambda b, i: (b, i, 0)
    return pl.pallas_call(
        functools.partial(_attn_kernel, seq_len=S),
        out_shape=jax.ShapeDtypeStruct((B, S, ATTN_WIDTH), BF16),
        grid=(B, S // TQ),
        in_specs=[pl.BlockSpec(memory_space=pltpu.SMEM),
                  pl.BlockSpec((1, TQ, ATTN_WIDTH), cur),
                  pl.BlockSpec((1, Q, 2 * KV_WIDTH), lambda b, i: (b, jnp.maximum(i * r - 1, 0), 0)),
                  pl.BlockSpec((1, TQ, 2 * KV_WIDTH), cur),
                  pl.BlockSpec((1, Q, 2 * KV_WIDTH), lambda b, i: (b, jnp.minimum((i + 1) * r, nq - 1), 0)),
                  pl.BlockSpec((N_KV_HEADS, G * Q, 3 * Q), lambda b, i: (0, 0, 0))],
        out_specs=pl.BlockSpec((1, TQ, ATTN_WIDTH), cur),
        scratch_shapes=[pltpu.VMEM((TQ + 2 * Q, 2 * KV_WIDTH), BF16)],
        compiler_params=_params(2),
        name="attn",
    )(sink.astype(F32), q, kv, kv, kv, bias)


def _shortconv_kernel(u_ref, p_ref, n_ref, w_ref, b_ref, v_ref, x1_ref, x2_ref):
    i = pl.program_id(1)
    n = pl.num_programs(1)
    C = HYENA_WIDTH
    u = u_ref[0]
    tr = u.shape[0]
    prow = jnp.where(i > 0, p_ref[0, 7:8, :], 0.0)
    nrow = jnp.where(i < n - 1, n_ref[0, 0:1, :], 0.0)
    rid = lax.broadcasted_iota(jnp.int32, (tr, 1), 0)
    up = jnp.where(rid == 0, prow, pltpu.roll(u, 1, 0))
    dn = jnp.where(rid == tr - 1, nrow, pltpu.roll(u, tr - 1, 0))
    out = w_ref[0:1, :] * up + w_ref[1:2, :] * u + w_ref[2:3, :] * dn + b_ref[...]
    v_ref[0] = out[:, :C]
    x1_ref[0] = out[:, C:2 * C]
    x2_ref[0] = out[:, 2 * C:]


def _short_conv(hy_u, conv_w, conv_b):
    B, S, W = hy_u.shape
    C = HYENA_WIDTH
    tr = min(512, S)
    r8 = tr // 8
    nb8 = S // 8
    return pl.pallas_call(
        _shortconv_kernel,
        out_shape=[jax.ShapeDtypeStruct((B, S, C), F32)] * 3,
        grid=(B, S // tr),
        in_specs=[pl.BlockSpec((1, tr, W), lambda b, i: (b, i, 0)),
                  pl.BlockSpec((1, 8, W), lambda b, i: (b, jnp.maximum(i * r8 - 1, 0), 0)),
                  pl.BlockSpec((1, 8, W), lambda b, i: (b, jnp.minimum((i + 1) * r8, nb8 - 1), 0)),
                  pl.BlockSpec((3, W), lambda b, i: (0, 0)),
                  pl.BlockSpec((1, W), lambda b, i: (0, 0))],
        out_specs=[pl.BlockSpec((1, tr, C), lambda b, i: (b, i, 0))] * 3,
        compiler_params=_params(2),
        name="shortconv",
    )(hy_u, hy_u, hy_u, conv_w, conv_b.reshape(1, W))


def _filter_kernel(z_ref, w1h, w1l, b1_ref, f1_ref, w2h, w2l, b2_ref, f2_ref, w3h, w3l, dec_ref,
                   k_ref, s_ref):
    i = pl.program_id(0)
    z = z_ref[...]
    h1 = jnp.sin(f1_ref[...] * (_dot3(z, w1h[...], w1l[...]) + b1_ref[...]))
    h2 = jnp.sin(f2_ref[...] * (_dot3(h1, w2h[...], w2l[...]) + b2_ref[...]))
    k = _dot3(h2, w3h[...], w3l[...])
    t = z[:, 0:1]
    k = k * (jnp.exp(-t * jnp.abs(dec_ref[...])) + WINDOW_SHIFT)
    k_ref[...] = k

    @pl.when(i == 0)
    def _():
        s_ref[...] = jnp.zeros_like(s_ref)

    s_ref[...] += jnp.sum(jnp.abs(k), axis=0, keepdims=True)


def _filter_embedding(L):
    t = np.linspace(0.0, 1.0, L, dtype=np.float32).astype(np.float64)[:, None]
    w = (2.0 * math.pi * np.arange(L, dtype=np.float32) / np.float32(L)).astype(np.float64)[:, None]
    bands = np.linspace(1e-4, FILTER_BANDS - 1, FILTER_BANDS, dtype=np.float32).astype(np.float64)[None, :]
    bw = (bands.astype(np.float32) * w.astype(np.float32)).astype(np.float64)
    z = np.concatenate([t, np.cos(bw), -np.sin(bw)], axis=-1)
    zp = np.zeros((L, LANES), np.float32)
    zp[:, :FILTER_EMB] = z.astype(np.float32)
    return jnp.asarray(zp)


def _pad2(a, r, c):
    return jnp.zeros((r, c), F32).at[:a.shape[0], :a.shape[1]].set(a.astype(F32))


def _filters(L, fw1, fb1, ff1, fw2, fb2, ff2, fw3, decay):
    H = LANES
    nf = fw3.shape[1]
    z = _filter_embedding(L)
    w1h, w1l = _split(_pad2(fw1, H, H))
    w2h, w2l = _split(_pad2(fw2, H, H))
    w3h, w3l = _split(_pad2(fw3, H, nf))
    b1 = _pad2(fb1[None], 1, H)
    f1 = _pad2(ff1[None], 1, H)
    b2 = _pad2(fb2[None], 1, H)
    f2 = _pad2(ff2[None], 1, H)
    tr = min(512, L)
    full = lambda r, c: pl.BlockSpec((r, c), lambda i: (0, 0))
    return pl.pallas_call(
        _filter_kernel,
        out_shape=[jax.ShapeDtypeStruct((L, nf), F32), jax.ShapeDtypeStruct((1, nf), F32)],
        grid=(L // tr,),
        in_specs=[pl.BlockSpec((tr, H), lambda i: (i, 0)),
                  full(H, H), full(H, H), full(1, H), full(1, H),
                  full(H, H), full(H, H), full(1, H), full(1, H),
                  full(H, nf), full(H, nf), full(1, nf)],
        out_specs=[pl.BlockSpec((tr, nf), lambda i: (i, 0)), full(1, nf)],
        compiler_params=_params(1),
        name="filter",
    )(z, w1h, w1l, b1, f1, w2h, w2l, b2, f2, w3h, w3l, decay.reshape(1, nf).astype(F32))


def _np_bf16(m64):
    return jnp.asarray(m64.astype(np.float32).astype(BF16))


def _dft_constants(L):
    N = 2 * L
    n2 = LANES
    n1 = N // n2
    h1 = n1 // 2
    k1 = np.arange(n1)[:, None]
    s1 = np.arange(h1)[None, :]
    ang = -2.0 * np.pi * ((k1 * s1) % n1) / n1
    wr, wi = np.cos(ang), np.sin(ang)
    w1_filt = np.block([[wr, wr], [wi, wi], [wr, -wr], [wi, -wi]])
    w1_cplx = np.block([[wr, -wi], [wi, wr]])
    vr, vi = wr.T / N, -wi.T / N
    w3 = np.block([[vr, -vi], [vi, vr]])
    k2 = np.arange(n2)[:, None]
    s2 = np.arange(n2)[None, :]
    a2 = -2.0 * np.pi * ((k2 * s2) % n2) / n2
    w2r, w2i = jnp.asarray(np.cos(a2), F32), jnp.asarray(np.sin(a2), F32)
    at = -2.0 * np.pi * ((np.arange(n1)[:, None] * s2) % N) / N
    twr, twi = jnp.asarray(np.cos(at), F32), jnp.asarray(np.sin(at), F32)
    mr = w2r[None] * twr[:, None, :] - w2i[None] * twi[:, None, :]
    mi = w2r[None] * twi[:, None, :] + w2i[None] * twr[:, None, :]
    fwd = jnp.concatenate([jnp.concatenate([mr, -mi], axis=2),
                           jnp.concatenate([mi, mr], axis=2)], axis=1)
    fwd = fwd.astype(BF16)
    return dict(n1=n1, w1_filt=_np_bf16(w1_filt), w1_cplx=_np_bf16(w1_cplx), w3=_np_bf16(w3),
                fwd=fwd, inv=jnp.swapaxes(fwd, 1, 2))


SCH = 8


def _dft1_kernel(x_ref, w_ref, a_ref, *, n1):
    w = w_ref[...]
    for j in range(SCH):
        rhs = jnp.concatenate([x_ref[0, 0, :, j, :], x_ref[0, 1, :, j, :]], axis=0)
        res = jnp.dot(w, rhs.astype(BF16), preferred_element_type=F32)
        a_ref[0, :, j, :] = _pack2(res[:n1], res[n1:])


def _dft1_data(x, consts):
    B, L, C = x.shape
    n1 = consts["n1"]
    h1 = n1 // 2
    xv = x.reshape(B // 2, 2, h1, LANES, C)
    return pl.pallas_call(
        functools.partial(_dft1_kernel, n1=n1),
        out_shape=jax.ShapeDtypeStruct((B // 2, n1, LANES, C), F32),
        grid=(B // 2, LANES // SCH),
        in_specs=[pl.BlockSpec((1, 2, h1, SCH, C), lambda p, j: (p, 0, 0, j, 0)),
                  pl.BlockSpec((2 * n1, n1), lambda p, j: (0, 0))],
        out_specs=pl.BlockSpec((1, n1, SCH, C), lambda p, j: (p, 0, j, 0)),
        compiler_params=_params(2),
        name="dft1",
    )(xv, consts["w1_cplx"])


def _dft1f_kernel(x_ref, w_ref, a_ref, *, n1):
    C = HYENA_WIDTH
    w = w_ref[...]
    for j in range(SCH):
        rhs = jnp.concatenate([x_ref[:, j, :C], x_ref[:, j, C:]], axis=0)
        res = jnp.dot(w, rhs.astype(BF16), preferred_element_type=F32)
        a_ref[0, :, 0, j, :] = _pack2(res[:n1], res[n1:2 * n1])
        a_ref[0, :, 1, j, :] = _pack2(res[2 * n1:3 * n1], res[3 * n1:])


def _dft1_filter(kraw, consts):
    L, nf = kraw.shape
    C = HYENA_WIDTH
    n_ord = nf // (2 * C)
    n1 = consts["n1"]
    h1 = n1 // 2
    kv = kraw.reshape(h1, LANES, nf)
    return pl.pallas_call(
        functools.partial(_dft1f_kernel, n1=n1),
        out_shape=jax.ShapeDtypeStruct((n_ord, n1, 2, LANES, C), F32),
        grid=(n_ord, LANES // SCH),
        in_specs=[pl.BlockSpec((h1, SCH, 2 * C), lambda o, j: (0, j, o)),
                  pl.BlockSpec((4 * n1, n1), lambda o, j: (0, 0))],
        out_specs=pl.BlockSpec((1, n1, 2, SCH, C), lambda o, j: (o, 0, 0, j, 0)),
        compiler_params=_params(2),
        name="dft1f",
    )(kv, consts["w1_filt"])


KCH = 8


def _midf_kernel(a_ref, f_ref, inv_ref, b0_ref, h_ref):
    n2 = LANES
    sc = inv_ref[0]
    for k in range(KCH):
        p = jnp.concatenate(_unpack2(a_ref[0, k, :n2, :]), axis=0).astype(BF16)
        q = jnp.concatenate(_unpack2(a_ref[0, k, n2:, :]), axis=0).astype(BF16)
        h_re = jnp.dot(f_ref[k, :n2, :], p, preferred_element_type=F32)
        h_im = jnp.dot(f_ref[k, n2:, :], q, preferred_element_type=F32)
        h_ref[0, k] = _pack2((h_re - b0_ref[0]) * sc, h_im * sc)


def _filter_spectrum(af, inv_den, bwd0, consts):
    n_ord, n1, _, n2, C = af.shape
    a = af.reshape(n_ord, n1, 2 * n2, C)
    tab = pl.BlockSpec((KCH, 2 * n2, 2 * n2), lambda k, o: (k, 0, 0))
    vec = pl.BlockSpec((1, 1, C), lambda k, o: (o, 0, 0))
    return pl.pallas_call(
        _midf_kernel,
        out_shape=jax.ShapeDtypeStruct((n_ord, n1, n2, C), F32),
        grid=(n1 // KCH, n_ord),
        in_specs=[pl.BlockSpec((1, KCH, 2 * n2, C), lambda k, o: (o, k, 0, 0)), tab, vec, vec],
        out_specs=pl.BlockSpec((1, KCH, n2, C), lambda k, o: (o, k, 0, 0)),
        compiler_params=_params(2),
        name="midf",
    )(a, consts["fwd"], inv_den, bwd0)


def _mid_kernel(a_ref, f_ref, i_ref, h_ref, b_ref):
    n2 = LANES
    for k in range(KCH):
        a = jnp.concatenate(_unpack2(a_ref[0, k]), axis=0).astype(BF16)
        x = jnp.dot(f_ref[k], a, preferred_element_type=F32)
        xr, xi = x[:n2], x[n2:]
        hr, hi = _unpack2(h_ref[0, k])
        y = jnp.concatenate([xr * hr - xi * hi, xr * hi + xi * hr], axis=0)
        b = jnp.dot(i_ref[k], y.astype(BF16), preferred_element_type=F32)
        b_ref[0, k] = _pack2(b[:n2], b[n2:])


def _mid(a, hspec, order, consts):
    P, n1, n2, C = a.shape
    tab = pl.BlockSpec((KCH, 2 * n2, 2 * n2), lambda k, p: (k, 0, 0))
    return pl.pallas_call(
        _mid_kernel,
        out_shape=jax.ShapeDtypeStruct((P, n1, n2, C), F32),
        grid=(n1 // KCH, P),
        in_specs=[pl.BlockSpec((1, KCH, n2, C), lambda k, p: (p, k, 0, 0)),
                  tab, tab,
                  pl.BlockSpec((1, KCH, n2, C), lambda k, p: (order, k, 0, 0))],
        out_specs=pl.BlockSpec((1, KCH, n2, C), lambda k, p: (p, k, 0, 0)),
        compiler_params=_params(2),
        name="mid",
    )(a, consts["fwd"], consts["inv"], hspec)


def _dft3_kernel(b_ref, w_ref, v_ref, g_ref, skip_ref, z_ref, slab_ref, *, h1):
    w = w_ref[...]
    skip = skip_ref[0]
    for j in range(SCH):
        slab_ref[...] = b_ref[0, :, j, :]
        rhs = jnp.concatenate(_unpack2(slab_ref[...]), axis=0)
        y = jnp.dot(w, rhs.astype(BF16), preferred_element_type=F32)
        for r in range(2):
            yr = y[r * h1:(r + 1) * h1]
            z_ref[0, r, :, j, :] = g_ref[0, r, :, j, :] * (yr + v_ref[0, r, :, j, :] * skip)


def _dft3_gate(b5, v, gate, skip, consts):
    P, n1, n2, C = b5.shape
    h1 = n1 // 2
    B, L, _ = v.shape
    five = lambda t: t.reshape(P, 2, h1, n2, C)
    dat = pl.BlockSpec((1, 2, h1, SCH, C), lambda p, j: (p, 0, 0, j, 0))
    out = pl.pallas_call(
        functools.partial(_dft3_kernel, h1=h1),
        out_shape=jax.ShapeDtypeStruct((P, 2, h1, n2, C), F32),
        grid=(P, n2 // SCH),
        in_specs=[pl.BlockSpec((1, n1, SCH, C), lambda p, j: (p, 0, j, 0)),
                  pl.BlockSpec((n1, 2 * n1), lambda p, j: (0, 0)),
                  dat, dat,
                  pl.BlockSpec((1, C), lambda p, j: (0, 0))],
        out_specs=dat,
        scratch_shapes=[pltpu.VMEM((n1, C), F32)],
        compiler_params=_params(2),
        name="dft3",
    )(b5, consts["w3"], five(v), five(gate), skip.reshape(1, C).astype(F32))
    return out.reshape(B, L, C)


def _hyena(hy_u, conv_w, conv_b, fw1, fb1, ff1, fw2, fb2, ff2, fw3, decay, skip):
    B, L, _ = hy_u.shape
    C = HYENA_WIDTH
    consts = _dft_constants(L)
    v, x1, x2 = _short_conv(hy_u, conv_w, conv_b)
    kraw, ksum = _filters(L, fw1, fb1, ff1, fw2, fb2, ff2, fw3, decay)
    ks = ksum.reshape(2, 2, C)
    inv_den = (1.0 / (ks[:, 0] + ks[:, 1])).reshape(2, 1, C)
    bwd0 = kraw[0].reshape(2, 2, C)[:, 1].reshape(2, 1, C)
    hspec = _filter_spectrum(_dft1_filter(kraw, consts), inv_den, bwd0, consts)
    z = v
    for o, gate in enumerate((x1, x2)):
        a5 = _dft1_data(z, consts)
        b5 = _mid(a5, hspec, o, consts)
        z = _dft3_gate(b5, z, gate, skip[o], consts)
    return z


def _merge_kernel(attn_ref, hy_ref, ga_ref, gh_ref, x_ref, mod_ref, wa_ref, wh_ref, wo_ref,
                  g1_ref, b1_ref, rwh_ref, rwl_ref, rb_ref, tri_ref,
                  x1_ref, h2_ref, route_ref, wts_ref, cnt_ref, carry_ref):
    first = (pl.program_id(0) == 0) & (pl.program_id(1) == 0)

    @pl.when(first)
    def _():
        carry_ref[...] = jnp.zeros_like(carry_ref)

    a = jnp.dot(attn_ref[0], wa_ref[...], preferred_element_type=F32)
    hy = jnp.dot(hy_ref[0].astype(BF16), wh_ref[...], preferred_element_type=F32)
    merged = ga_ref[0].astype(F32) * a + gh_ref[0].astype(F32) * hy
    y = jnp.dot(merged.astype(BF16), wo_ref[...], preferred_element_type=F32)
    gate1 = mod_ref[0, 2:3, :]
    shift2 = mod_ref[0, 3:4, :]
    scale2 = mod_ref[0, 4:5, :]
    x1 = _layer_norm(DN_ALPHA * x_ref[0] + gate1 * y, g1_ref[...], b1_ref[...])
    x1_ref[0] = x1
    h2 = x1 * (1.0 + scale2) + shift2
    half = h2.shape[1] // 2
    h2_ref[0] = _pack2(h2[:, :half], h2[:, half:])

    logits = _dot3(h2, rwh_ref[...], rwl_ref[...]) + rb_ref[...]
    tm = logits.shape[0]
    lane = lax.broadcasted_iota(jnp.int32, (tm, LANES), 1)
    lanef = lane.astype(F32)
    big = float(LANES)

    def first_lane(mask):
        return jnp.min(jnp.where(mask, lanef, big), axis=1, keepdims=True).astype(jnp.int32)

    gmask = lane < N_GROUPS
    gl = jnp.where(gmask, logits, NEG)
    gmax = jnp.max(gl, axis=1, keepdims=True)
    gidx = first_lane(gl == gmax)
    pg = 1.0 / jnp.sum(jnp.exp(gl - gmax), axis=1, keepdims=True)
    lo = ROUTE_OFF + gidx * EXPERTS_PER_GROUP
    emask = (lane >= lo) & (lane < lo + EXPERTS_PER_GROUP)
    el = jnp.where(emask, logits, NEG)
    v1 = jnp.max(el, axis=1, keepdims=True)
    i1 = first_lane(el == v1)
    el2 = jnp.where(emask & (lane != i1), logits, NEG)
    v2 = jnp.max(el2, axis=1, keepdims=True)
    i2 = first_lane(el2 == v2)
    e21 = jnp.exp(v2 - v1)
    w1 = pg / (1.0 + e21)
    w2 = pg * e21 / (1.0 + e21)

    sel1 = lane == i1
    sel2 = lane == i2
    onehot = jnp.where(sel1 | sel2, 1.0, 0.0)
    prefix = jnp.dot(tri_ref[...], onehot.astype(BF16), preferred_element_type=F32) + carry_ref[...]
    r1 = jnp.sum(jnp.where(sel1, prefix, 0.0), axis=1, keepdims=True)
    r2 = jnp.sum(jnp.where(sel2, prefix, 0.0), axis=1, keepdims=True)
    carry_ref[...] += jnp.sum(onehot, axis=0, keepdims=True)
    cnt_ref[...] = carry_ref[...]

    ranks = jnp.where(lane == 2, r1, jnp.where(lane == 3, r2, 0.0)).astype(jnp.int32)
    route_ref[0] = jnp.where(lane == 0, i1 - ROUTE_OFF, jnp.where(lane == 1, i2 - ROUTE_OFF, ranks))
    wts_ref[0] = jnp.where(lane == 0, w1, jnp.where(lane == 1, w2, 0.0))


def _merge(attn, hy, ga, gh, x, mod, w_attn_o, w_hy_o, w_out, ln1_g, ln1_b, rg_w, rg_b, re_w, re_b):
    B, S, D = x.shape
    tm = min(512, S)
    rw = jnp.zeros((D, LANES), F32).at[:, :N_GROUPS].set(rg_w).at[:, ROUTE_OFF:ROUTE_OFF + N_EXPERTS].set(re_w)
    rb = jnp.zeros((1, LANES), F32).at[0, :N_GROUPS].set(rg_b).at[0, ROUTE_OFF:ROUTE_OFF + N_EXPERTS].set(re_b)
    rwh, rwl = _split(rw)
    tri = (jnp.arange(tm)[:, None] > jnp.arange(tm)[None, :]).astype(BF16)
    row = lambda b, i: (b, i, 0)
    full = lambda r, c: pl.BlockSpec((r, c), lambda b, i: (0, 0))
    outs = [jax.ShapeDtypeStruct((B, S, D), F32), jax.ShapeDtypeStruct((B, S, D // 2), F32),
            jax.ShapeDtypeStruct((B, S, LANES), jnp.int32), jax.ShapeDtypeStruct((B, S, LANES), F32),
            jax.ShapeDtypeStruct((1, LANES), F32)]
    return pl.pallas_call(
        _merge_kernel,
        out_shape=outs,
        grid=(B, S // tm),
        in_specs=[pl.BlockSpec((1, tm, ATTN_WIDTH), row), pl.BlockSpec((1, tm, HYENA_WIDTH), row),
                  pl.BlockSpec((1, tm, D), row), pl.BlockSpec((1, tm, D), row), pl.BlockSpec((1, tm, D), row),
                  pl.BlockSpec((1, 6, D), lambda b, i: (b, 0, 0)),
                  full(ATTN_WIDTH, D), full(HYENA_WIDTH, D), full(D, D),
                  full(1, D), full(1, D), full(D, LANES), full(D, LANES), full(1, LANES), full(tm, tm)],
        out_specs=[pl.BlockSpec((1, tm, D), row), pl.BlockSpec((1, tm, D // 2), row),
                   pl.BlockSpec((1, tm, LANES), row), pl.BlockSpec((1, tm, LANES), row), full(1, LANES)],
        scratch_shapes=[pltpu.VMEM((1, LANES), F32)],
        compiler_params=_params(2),
        name="merge",
    )(attn, hy, ga, gh, x, mod, w_attn_o.astype(BF16), w_hy_o.astype(BF16), w_out.astype(BF16),
      ln1_g.reshape(1, D), ln1_b.reshape(1, D), rwh, rwl, rb, tri)


TOK_CH = 1024
WAIT_ROWS = 128


def _dispatch_kernel(poffs_ref, psz_ref, dest_ref, h_ref, xb_ref, zero_ref, sem, zsem):
    @pl.when(pl.program_id(0) == 0)
    def _():
        zero_ref[...] = jnp.zeros_like(zero_ref)

        def zcopy(e):
            start = pl.multiple_of(poffs_ref[e] + psz_ref[e] - MOE_BLOCK, MOE_BLOCK)
            return pltpu.make_async_copy(zero_ref, xb_ref.at[pl.ds(start, MOE_BLOCK)], zsem)

        def zissue(e, c):
            @pl.when(psz_ref[e] > 0)
            def _():
                zcopy(e).start()
            return c

        def zdrain(e, c):
            @pl.when(psz_ref[e] > 0)
            def _():
                zcopy(e).wait()
            return c

        lax.fori_loop(0, N_EXPERTS, zissue, 0)
        lax.fori_loop(0, N_EXPERTS, zdrain, 0)

        def tcopy(b):
            return pltpu.make_async_copy(zero_ref, xb_ref.at[pl.ds(pl.multiple_of(b * MOE_BLOCK, MOE_BLOCK), MOE_BLOCK)], zsem)

        first_free = (poffs_ref[N_EXPERTS - 1] + psz_ref[N_EXPERTS - 1]) // MOE_BLOCK
        n_blocks = xb_ref.shape[0] // MOE_BLOCK
        lax.fori_loop(first_free, n_blocks, lambda b, c: (tcopy(b).start(), c)[1], 0)
        lax.fori_loop(first_free, n_blocks, lambda b, c: (tcopy(b).wait(), c)[1], 0)

    def issue(r, c):
        for j in range(2):
            pltpu.make_async_copy(h_ref.at[pl.ds(r, 1)], xb_ref.at[pl.ds(dest_ref[0, 0, 2 * r + j], 1)], sem).start(priority=j)
        return c

    lax.fori_loop(0, TOK_CH, issue, 0, unroll=8)

    def drain(r, c):
        pltpu.make_async_copy(xb_ref.at[pl.ds(0, WAIT_ROWS)], xb_ref.at[pl.ds(0, WAIT_ROWS)], sem).wait()
        return c

    lax.fori_loop(0, 2 * TOK_CH // WAIT_ROWS, drain, 0)


def _dispatch(h2, dest, poffs, psizes, P):
    T, W = h2.shape
    nch = T // TOK_CH
    grid_spec = pltpu.PrefetchScalarGridSpec(
        num_scalar_prefetch=2,
        grid=(nch,),
        in_specs=[pl.BlockSpec((1, 1, 2 * TOK_CH), lambda i, po, ps: (i, 0, 0), memory_space=pltpu.SMEM),
                  pl.BlockSpec((TOK_CH, W), lambda i, po, ps: (i, 0))],
        out_specs=pl.BlockSpec(memory_space=pl.ANY),
        scratch_shapes=[pltpu.VMEM((MOE_BLOCK, W), h2.dtype), pltpu.SemaphoreType.DMA(()), pltpu.SemaphoreType.DMA(())],
    )
    return pl.pallas_call(
        _dispatch_kernel,
        out_shape=jax.ShapeDtypeStruct((P, W), h2.dtype),
        grid_spec=grid_spec,
        compiler_params=_params(1),
        name="dispatch",
    )(poffs, psizes, dest, h2)


def _expert_kernel(be_ref, nu_ref, x_ref, w1_ref, w3_ref, w2_ref, y_ref, c1_ref, c3_ref, c2_ref):
    i = pl.program_id(0)
    used = i < nu_ref[0]
    fresh = (i == 0) | (be_ref[i] != be_ref[jnp.maximum(i - 1, 0)])

    @pl.when(used & fresh)
    def _():
        c1_ref[...] = w1_ref[0].astype(BF16)
        c3_ref[...] = w3_ref[0].astype(BF16)
        c2_ref[...] = w2_ref[0].astype(BF16)

    @pl.when(used)
    def _():
        xa, xb = _unpack2(x_ref[...])
        x = jnp.concatenate([xa, xb], axis=1).astype(BF16)
        a = jnp.dot(x, c1_ref[...], preferred_element_type=F32)
        g = jnp.dot(x, c3_ref[...], preferred_element_type=F32)
        hmid = (a * _sigmoid(a) * g).astype(BF16)
        y = jnp.dot(hmid, c2_ref[...], preferred_element_type=F32)
        half = y.shape[1] // 2
        y_ref[...] = _pack2(y[:, :half], y[:, half:])

    @pl.when(jnp.logical_not(used))
    def _():
        y_ref[...] = jnp.zeros_like(y_ref)


def _experts(xb, blk_e, n_used, w1, w3, w2):
    P, W = xb.shape
    E, D, DE = w1.shape
    nblk = P // MOE_BLOCK
    grid_spec = pltpu.PrefetchScalarGridSpec(
        num_scalar_prefetch=2,
        grid=(nblk,),
        in_specs=[pl.BlockSpec((MOE_BLOCK, W), lambda i, be, nu: (jnp.minimum(i, jnp.maximum(nu[0] - 1, 0)), 0)),
                  pl.BlockSpec((1, D, DE), lambda i, be, nu: (be[i], 0, 0)),
                  pl.BlockSpec((1, D, DE), lambda i, be, nu: (be[i], 0, 0)),
                  pl.BlockSpec((1, DE, D), lambda i, be, nu: (be[i], 0, 0))],
        out_specs=pl.BlockSpec((MOE_BLOCK, W), lambda i, be, nu: (i, 0)),
        scratch_shapes=[pltpu.VMEM((D, DE), BF16), pltpu.VMEM((D, DE), BF16), pltpu.VMEM((DE, D), BF16)],
    )
    return pl.pallas_call(
        _expert_kernel,
        out_shape=jax.ShapeDtypeStruct((P, W), F32),
        grid_spec=grid_spec,
        compiler_params=_params(1),
        name="experts",
    )(blk_e, n_used, xb, w1, w3, w2)


def _combine_kernel(dest_ref, wts_ref, x1_ref, mod_ref, g_ref, b_ref, yb_ref, o_ref, rows_ref, sem):
    def issue(r, c):
        for j in range(2):
            pltpu.make_async_copy(yb_ref.at[pl.ds(dest_ref[0, 0, 2 * r + j], 1)], rows_ref.at[j, pl.ds(r, 1)], sem).start(priority=j)
        return c

    lax.fori_loop(0, TOK_CH, issue, 0, unroll=8)

    def drain(r, c):
        pltpu.make_async_copy(yb_ref.at[pl.ds(0, WAIT_ROWS)], rows_ref.at[0, pl.ds(0, WAIT_ROWS)], sem).wait()
        return c

    lax.fori_loop(0, 2 * TOK_CH // WAIT_ROWS, drain, 0)
    w = wts_ref[...]
    y0 = jnp.concatenate(_unpack2(rows_ref[0]), axis=1)
    y1 = jnp.concatenate(_unpack2(rows_ref[1]), axis=1)
    y = w[:, 0:1] * y0 + w[:, 1:2] * y1
    gate2 = mod_ref[0, 5:6, :]
    o_ref[...] = _layer_norm(DN_ALPHA * x1_ref[...] + gate2 * y, g_ref[...], b_ref[...])


def _combine(yb, dest, wts, x1, mod, ln2_g, ln2_b, S):
    T, D = x1.shape
    nch = T // TOK_CH
    per_b = S // TOK_CH
    return pl.pallas_call(
        _combine_kernel,
        out_shape=jax.ShapeDtypeStruct((T, D), F32),
        grid=(nch,),
        in_specs=[pl.BlockSpec((1, 1, 2 * TOK_CH), lambda i: (i, 0, 0), memory_space=pltpu.SMEM),
                  pl.BlockSpec((TOK_CH, LANES), lambda i: (i, 0)),
                  pl.BlockSpec((TOK_CH, D), lambda i: (i, 0)),
                  pl.BlockSpec((1, 6, D), lambda i: (i // per_b, 0, 0)),
                  pl.BlockSpec((1, D), lambda i: (0, 0)),
                  pl.BlockSpec((1, D), lambda i: (0, 0)),
                  pl.BlockSpec(memory_space=pl.ANY)],
        out_specs=pl.BlockSpec((TOK_CH, D), lambda i: (i, 0)),
        scratch_shapes=[pltpu.VMEM((2, TOK_CH, yb.shape[1]), yb.dtype), pltpu.SemaphoreType.DMA(())],
        compiler_params=_params(1),
        name="combine",
    )(dest, wts, x1, mod, ln2_g.reshape(1, D), ln2_b.reshape(1, D), yb)


def _moe(h2, x1, route, wts, counts, mod, w1, w3, w2, ln2_g, ln2_b):
    B, S, D = x1.shape
    T = B * S
    P = 2 * T + N_EXPERTS * MOE_BLOCK
    nblk = P // MOE_BLOCK
    nch = T // TOK_CH
    sizes = counts[0, ROUTE_OFF:ROUTE_OFF + N_EXPERTS].astype(jnp.int32)
    psizes = (sizes + MOE_BLOCK - 1) // MOE_BLOCK * MOE_BLOCK
    pends = jnp.cumsum(psizes)
    poffs = pends - psizes
    starts = jnp.arange(nblk, dtype=jnp.int32) * MOE_BLOCK
    blk_e = jnp.minimum(jnp.sum((pends[None, :] <= starts[:, None]).astype(jnp.int32), axis=1), N_EXPERTS - 1)
    n_used = pends[-1:] // MOE_BLOCK
    r4 = route.reshape(T, LANES)[:, :4]
    sel = r4[:, :2, None] == jnp.arange(N_EXPERTS, dtype=jnp.int32)[None, None, :]
    dest = (r4[:, 2:4] + jnp.sum(jnp.where(sel, poffs[None, None, :], 0), axis=-1)).reshape(nch, 1, 2 * TOK_CH)
    xb = _dispatch(h2.reshape(T, D // 2), dest, poffs, psizes, P)
    yb = _experts(xb, blk_e, n_used, w1, w3, w2)
    out = _combine(yb, dest, wts.reshape(T, LANES), x1.reshape(T, D), mod, ln2_g, ln2_b, S)
    return out.reshape(B, S, D)


def _layer(x, c, w_ada, b_ada, w_in, conv_w, conv_b, fw1, fb1, ff1, fw2, fb2, ff2, fw3, decay, skip,
           w_hy_o, w_attn_o, attn_sink, w_out, ln1_g, ln1_b, rg_w, rg_b, re_w, re_b, ew1, ew3, ew2,
           ln2_g, ln2_b):
    mod = _ada(c, w_ada, b_ada)
    q, kv, hy_u, ga, gh = _in_proj(x, mod, w_in)
    attn = _attention(q, kv, attn_sink)
    hy = _hyena(hy_u, conv_w, conv_b, fw1, fb1, ff1, fw2, fb2, ff2, fw3, decay, skip)
    x1, h2, route, wts, counts = _merge(attn, hy, ga, gh, x, mod, w_attn_o, w_hy_o, w_out,
                                        ln1_g, ln1_b, rg_w, rg_b, re_w, re_b)
    return _moe(h2, x1, route, wts, counts, mod, ew1, ew3, ew2, ln2_g, ln2_b)


def kernel(x, c, w_ada, b_ada, w_in, conv_w, conv_b, filt_w1, filt_b1, filt_freq1, filt_w2, filt_b2, filt_freq2, filt_w3, filt_decay, hy_skip, w_hy_o, w_attn_o, attn_sink, w_out, ln1_g, ln1_b, router_group_w, router_group_b, router_expert_w, router_expert_b, exp_w1, exp_w3, exp_w2, ln2_g, ln2_b):
    for l in range(w_ada.shape[0]):
        x = _layer(x, c, w_ada[l], b_ada[l], w_in[l], conv_w[l], conv_b[l], filt_w1[l], filt_b1[l],
                   filt_freq1[l], filt_w2[l], filt_b2[l], filt_freq2[l], filt_w3[l], filt_decay[l],
                   hy_skip[l], w_hy_o[l], w_attn_o[l], attn_sink[l], w_out[l], ln1_g[l], ln1_b[l],
                   router_group_w[l], router_group_b[l], router_expert_w[l], router_expert_b[l],
                   exp_w1[l], exp_w3[l], exp_w2[l], ln2_g[l], ln2_b[l])
    return x
```

```python
import functools
import math

import numpy as np
import jax
import jax.numpy as jnp
from jax import lax
from jax.experimental import pallas as pl
from jax.experimental.pallas import tpu as pltpu

F32 = jnp.float32
BF16 = jnp.bfloat16

N_HEADS = 8
N_KV_HEADS = 2
HEAD_DIM = 64
ATTN_WIDTH = N_HEADS * HEAD_DIM
KV_WIDTH = N_KV_HEADS * HEAD_DIM
WINDOW = 128
BLOCK_Q = 128
HYENA_WIDTH = 512
FILTER_EMB = 33
FILTER_BANDS = (FILTER_EMB - 1) // 2
WINDOW_SHIFT = 0.05
N_GROUPS = 8
EXPERTS_PER_GROUP = 8
N_EXPERTS = N_GROUPS * EXPERTS_PER_GROUP
D_EXPERT = 512
MOE_BLOCK = 512
LN_EPS = 1e-5
DEPTH = 1
DN_ALPHA = (2.0 * DEPTH) ** 0.25
NEG = -1e30

LANES = 128
SUBLANES = 8
ROUTE_OFF = N_GROUPS
VMEM_LIMIT = 56 * 1024 * 1024


def _params(n_axes, vmem=VMEM_LIMIT):
    return pltpu.CompilerParams(dimension_semantics=("arbitrary",) * n_axes, vmem_limit_bytes=vmem)


def _split(a):
    hi = a.astype(BF16)
    lo = (a - hi.astype(F32)).astype(BF16)
    return hi, lo


def _dot3(a, b_hi, b_lo):
    a_hi, a_lo = _split(a)
    acc = jnp.dot(a_hi, b_hi, preferred_element_type=F32)
    acc = acc + jnp.dot(a_hi, b_lo, preferred_element_type=F32)
    acc = acc + jnp.dot(a_lo, b_hi, preferred_element_type=F32)
    return acc


def _pack2(a, b):
    ia = lax.bitcast_convert_type(a.astype(BF16).astype(F32), jnp.int32)
    ib = lax.bitcast_convert_type(b.astype(BF16).astype(F32), jnp.int32)
    return lax.bitcast_convert_type(ia | lax.shift_right_logical(ib, 16), F32)


def _unpack2(p):
    p = lax.bitcast_convert_type(p, jnp.int32)
    a = lax.bitcast_convert_type(p & jnp.int32(-65536), F32)
    b = lax.bitcast_convert_type(lax.shift_left(p, 16), F32)
    return a, b


def _sigmoid(x):
    return 1.0 / (1.0 + jnp.exp(-x))


def _layer_norm(r, g, b):
    mu = jnp.mean(r, axis=-1, keepdims=True)
    d = r - mu
    var = jnp.mean(d * d, axis=-1, keepdims=True)
    return d * lax.rsqrt(var + LN_EPS) * g + b


def _ada_kernel(c_ref, wh_ref, wl_ref, b_ref, o_ref):
    c = c_ref[...]
    s = c * _sigmoid(c)
    o_ref[...] = _dot3(s, wh_ref[...], wl_ref[...]) + b_ref[...]


def _ada(c, w_ada, b_ada):
    B, D = c.shape
    n_out = w_ada.shape[1]
    rows = 8
    cp = jnp.zeros((rows, D), F32).at[:B].set(c)
    wh, wl = _split(w_ada)
    tn = 1024
    out = pl.pallas_call(
        _ada_kernel,
        out_shape=jax.ShapeDtypeStruct((rows, n_out), F32),
        grid=(n_out // tn,),
        in_specs=[pl.BlockSpec((rows, D), lambda j: (0, 0)),
                  pl.BlockSpec((D, tn), lambda j: (0, j)),
                  pl.BlockSpec((D, tn), lambda j: (0, j)),
                  pl.BlockSpec((1, tn), lambda j: (0, j))],
        out_specs=pl.BlockSpec((rows, tn), lambda j: (0, j)),
        compiler_params=_params(1),
        name="ada",
    )(cp, wh, wl, b_ada.reshape(1, n_out))
    return out[:B].reshape(B, 6, D)


def _inproj_kernel(x_ref, xp_ref, xn_ref, mod_ref, w_ref, cw_ref, cb_ref,
                   q_ref, kv_ref, v_ref, x1_ref, x2_ref, ga_ref, gh_ref):
    i = pl.program_id(1)
    n = pl.num_programs(1)
    C = HYENA_WIDTH
    x = x_ref[0]
    tm, D = x.shape
    shift = mod_ref[0, 0:1, :]
    scale = mod_ref[0, 1:2, :]
    h = (x * (1.0 + scale) + shift).astype(BF16)

    def seg(lo, hi):
        return jnp.dot(h, w_ref[:, lo:hi], preferred_element_type=F32)

    o = 0
    q_ref[0] = (seg(o, o + ATTN_WIDTH) * (HEAD_DIM ** -0.5)).astype(BF16)
    o += ATTN_WIDTH
    kv_ref[0] = seg(o, o + 2 * KV_WIDTH).astype(BF16)
    o += 2 * KV_WIDTH

    u = seg(o, o + 3 * C)
    xe = jnp.concatenate([xp_ref[0], xn_ref[0]], axis=0)
    he = (xe * (1.0 + scale) + shift).astype(BF16)
    ue = jnp.dot(he, w_ref[:, o:o + 3 * C], preferred_element_type=F32)
    prow = jnp.where(i > 0, ue[SUBLANES - 1:SUBLANES], 0.0)
    nrow = jnp.where(i < n - 1, ue[SUBLANES:SUBLANES + 1], 0.0)
    rid = lax.broadcasted_iota(jnp.int32, (tm, 1), 0)
    up = jnp.where(rid == 0, prow, pltpu.roll(u, 1, 0))
    dn = jnp.where(rid == tm - 1, nrow, pltpu.roll(u, tm - 1, 0))
    conv = cw_ref[0:1, :] * up + cw_ref[1:2, :] * u + cw_ref[2:3, :] * dn + cb_ref[...]
    v_ref[0] = conv[:, :C]
    x1_ref[0] = conv[:, C:2 * C]
    x2_ref[0] = conv[:, 2 * C:]
    o += 3 * C

    ga_ref[0] = _sigmoid(seg(o, o + D)).astype(BF16)
    o += D
    gh_ref[0] = _sigmoid(seg(o, o + D)).astype(BF16)


def _in_proj(x, mod, w_in, conv_w, conv_b):
    B, S, D = x.shape
    C = HYENA_WIDTH
    tm = min(512, S)
    r8 = tm // SUBLANES
    nb8 = S // SUBLANES
    wb = w_in.astype(BF16)
    nw = wb.shape[1]
    row = lambda b, i: (b, i, 0)
    shapes = [(ATTN_WIDTH, BF16), (2 * KV_WIDTH, BF16), (C, F32), (C, F32), (C, F32), (D, BF16), (D, BF16)]
    return pl.pallas_call(
        _inproj_kernel,
        out_shape=[jax.ShapeDtypeStruct((B, S, w), dt) for w, dt in shapes],
        grid=(B, S // tm),
        in_specs=[pl.BlockSpec((1, tm, D), row),
                  pl.BlockSpec((1, SUBLANES, D), lambda b, i: (b, jnp.maximum(i * r8 - 1, 0), 0)),
                  pl.BlockSpec((1, SUBLANES, D), lambda b, i: (b, jnp.minimum((i + 1) * r8, nb8 - 1), 0)),
                  pl.BlockSpec((1, 6, D), lambda b, i: (b, 0, 0)),
                  pl.BlockSpec((D, nw), lambda b, i: (0, 0)),
                  pl.BlockSpec((3, 3 * C), lambda b, i: (0, 0)),
                  pl.BlockSpec((1, 3 * C), lambda b, i: (0, 0))],
        out_specs=[pl.BlockSpec((1, tm, w), row) for w, _ in shapes],
        compiler_params=_params(2),
        name="in_proj",
    )(x, x, x, mod, wb, conv_w.astype(F32), conv_b.reshape(1, 3 * C).astype(F32))


ATT_TQ = 512


def _attn_kernel(sink_ref, q_ref, kvp_ref, kvc_ref, kvn_ref, bias_ref, o_ref, kv_scr, *, seq_len):
    i = pl.program_id(1)
    Q = BLOCK_Q
    TQ = q_ref.shape[1]
    G = N_HEADS // N_KV_HEADS
    kv_scr[0:Q] = kvp_ref[0]
    kv_scr[Q:Q + TQ] = kvc_ref[0]
    kv_scr[Q + TQ:] = kvn_ref[0]
    col = lax.broadcasted_iota(jnp.int32, (1, 3 * Q), 1)
    rhead = lax.broadcasted_iota(jnp.int32, (G * Q, 1), 0) // Q
    for j in range(TQ // Q):
        kpos = i * TQ + (j - 1) * Q + col
        colbias = jnp.where((kpos >= 0) & (kpos < seq_len), 0.0, NEG)
        for kv in range(N_KV_HEADS):
            kk = kv_scr[j * Q:(j + 3) * Q, kv * HEAD_DIM:(kv + 1) * HEAD_DIM]
            vv = kv_scr[j * Q:(j + 3) * Q, KV_WIDTH + kv * HEAD_DIM:KV_WIDTH + (kv + 1) * HEAD_DIM]
            heads = [kv * G + g for g in range(G)]
            qg = jnp.concatenate([q_ref[0, j * Q:(j + 1) * Q, h * HEAD_DIM:(h + 1) * HEAD_DIM] for h in heads], axis=0)
            s = lax.dot_general(qg, kk, (((1,), (1,)), ((), ())), preferred_element_type=F32)
            s = s + bias_ref[kv] + colbias
            snk = jnp.where(rhead == 0, sink_ref[heads[0]],
                            jnp.where(rhead == 1, sink_ref[heads[1]],
                                      jnp.where(rhead == 2, sink_ref[heads[2]], sink_ref[heads[3]])))
            m = jnp.maximum(jnp.max(s, axis=1, keepdims=True), snk)
            p = jnp.exp(s - m)
            den = jnp.sum(p, axis=1, keepdims=True) + jnp.exp(snk - m)
            o = jnp.dot(p.astype(BF16), vv, preferred_element_type=F32) / den
            for g, h in enumerate(heads):
                o_ref[0, j * Q:(j + 1) * Q, h * HEAD_DIM:(h + 1) * HEAD_DIM] = o[g * Q:(g + 1) * Q].astype(BF16)


def _attention(q, kv, sink):
    B, S, _ = q.shape
    Q = BLOCK_Q
    TQ = min(ATT_TQ, S)
    r = TQ // Q
    nq = S // Q
    G = N_HEADS // N_KV_HEADS
    assert G == 4
    a = jnp.arange(Q)[:, None]
    j = jnp.arange(3 * Q)[None, :]
    rel = jnp.abs(j - Q - a).astype(F32)
    slopes = 2.0 ** (-8.0 * jnp.arange(1, N_HEADS + 1, dtype=F32) / N_HEADS)
    bias = jnp.where(rel[None] <= WINDOW, -slopes[:, None, None] * rel[None], NEG).astype(F32)
    bias = bias.reshape(N_KV_HEADS, G * Q, 3 * Q)
    cur = lambda b, i: (b, i, 0)
    return pl.pallas_call(
        functools.partial(_attn_kernel, seq_len=S),
        out_shape=jax.ShapeDtypeStruct((B, S, ATTN_WIDTH), BF16),
        grid=(B, S // TQ),
        in_specs=[pl.BlockSpec(memory_space=pltpu.SMEM),
                  pl.BlockSpec((1, TQ, ATTN_WIDTH), cur),
                  pl.BlockSpec((1, Q, 2 * KV_WIDTH), lambda b, i: (b, jnp.maximum(i * r - 1, 0), 0)),
                  pl.BlockSpec((1, TQ, 2 * KV_WIDTH), cur),
                  pl.BlockSpec((1, Q, 2 * KV_WIDTH), lambda b, i: (b, jnp.minimum((i + 1) * r, nq - 1), 0)),
                  pl.BlockSpec((N_KV_HEADS, G * Q, 3 * Q), lambda b, i: (0, 0, 0))],
        out_specs=pl.BlockSpec((1, TQ, ATTN_WIDTH), cur),
        scratch_shapes=[pltpu.VMEM((TQ + 2 * Q, 2 * KV_WIDTH), BF16)],
        compiler_params=_params(2),
        name="attn",
    )(sink.astype(F32), q, kv, kv, kv, bias)


def _filter_kernel(z_ref, w1h, w1l, b1_ref, f1_ref, w2h, w2l, b2_ref, f2_ref, w3h, w3l, dec_ref,
                   k_ref, s_ref):
    i = pl.program_id(0)
    z = z_ref[...]
    h1 = jnp.sin(f1_ref[...] * (_dot3(z, w1h[...], w1l[...]) + b1_ref[...]))
    h2 = jnp.sin(f2_ref[...] * (_dot3(h1, w2h[...], w2l[...]) + b2_ref[...]))
    k = _dot3(h2, w3h[...], w3l[...])
    t = z[:, 0:1]
    k = k * (jnp.exp(-t * jnp.abs(dec_ref[...])) + WINDOW_SHIFT)
    k_ref[...] = k

    @pl.when(i == 0)
    def _():
        s_ref[...] = jnp.zeros_like(s_ref)

    s_ref[...] += jnp.sum(jnp.abs(k), axis=0, keepdims=True)


def _filter_embedding(L):
    t = np.linspace(0.0, 1.0, L, dtype=np.float32).astype(np.float64)[:, None]
    w = (2.0 * math.pi * np.arange(L, dtype=np.float32) / np.float32(L)).astype(np.float64)[:, None]
    bands = np.linspace(1e-4, FILTER_BANDS - 1, FILTER_BANDS, dtype=np.float32).astype(np.float64)[None, :]
    bw = (bands.astype(np.float32) * w.astype(np.float32)).astype(np.float64)
    z = np.concatenate([t, np.cos(bw), -np.sin(bw)], axis=-1)
    zp = np.zeros((L, LANES), np.float32)
    zp[:, :FILTER_EMB] = z.astype(np.float32)
    return jnp.asarray(zp)


def _pad2(a, r, c):
    return jnp.zeros((r, c), F32).at[:a.shape[0], :a.shape[1]].set(a.astype(F32))


def _filters(L, fw1, fb1, ff1, fw2, fb2, ff2, fw3, decay):
    H = LANES
    nf = fw3.shape[1]
    z = _filter_embedding(L)
    w1h, w1l = _split(_pad2(fw1, H, H))
    w2h, w2l = _split(_pad2(fw2, H, H))
    w3h, w3l = _split(_pad2(fw3, H, nf))
    b1 = _pad2(fb1[None], 1, H)
    f1 = _pad2(ff1[None], 1, H)
    b2 = _pad2(fb2[None], 1, H)
    f2 = _pad2(ff2[None], 1, H)
    tr = min(512, L)
    full = lambda r, c: pl.BlockSpec((r, c), lambda i: (0, 0))
    return pl.pallas_call(
        _filter_kernel,
        out_shape=[jax.ShapeDtypeStruct((L, nf), F32), jax.ShapeDtypeStruct((1, nf), F32)],
        grid=(L // tr,),
        in_specs=[pl.BlockSpec((tr, H), lambda i: (i, 0)),
                  full(H, H), full(H, H), full(1, H), full(1, H),
                  full(H, H), full(H, H), full(1, H), full(1, H),
                  full(H, nf), full(H, nf), full(1, nf)],
        out_specs=[pl.BlockSpec((tr, nf), lambda i: (i, 0)), full(1, nf)],
        compiler_params=_params(1),
        name="filter",
    )(z, w1h, w1l, b1, f1, w2h, w2l, b2, f2, w3h, w3l, decay.reshape(1, nf).astype(F32))


def _np_bf16(m64):
    return jnp.asarray(m64.astype(np.float32).astype(BF16))


def _dft_constants(L):
    N = 2 * L
    n2 = LANES
    n1 = N // n2
    h1 = n1 // 2
    k1 = np.arange(n1)[:, None]
    s1 = np.arange(h1)[None, :]
    ang = -2.0 * np.pi * ((k1 * s1) % n1) / n1
    wr, wi = np.cos(ang), np.sin(ang)
    w1_filt = np.block([[wr, wr], [wi, wi], [wr, -wr], [wi, -wi]])
    w1_cplx = np.block([[wr, -wi], [wi, wr]])
    vr, vi = wr.T / N, -wi.T / N
    w3 = np.block([[vr, -vi], [vi, vr]])
    k2 = np.arange(n2)[:, None]
    s2 = np.arange(n2)[None, :]
    a2 = -2.0 * np.pi * ((k2 * s2) % n2) / n2
    w2r, w2i = jnp.asarray(np.cos(a2), F32), jnp.asarray(np.sin(a2), F32)
    at = -2.0 * np.pi * ((np.arange(n1)[:, None] * s2) % N) / N
    twr, twi = jnp.asarray(np.cos(at), F32), jnp.asarray(np.sin(at), F32)
    mr = w2r[None] * twr[:, None, :] - w2i[None] * twi[:, None, :]
    mi = w2r[None] * twi[:, None, :] + w2i[None] * twr[:, None, :]
    fwd = jnp.concatenate([jnp.concatenate([mr, -mi], axis=2),
                           jnp.concatenate([mi, mr], axis=2)], axis=1)
    fwd = fwd.astype(BF16)
    return dict(n1=n1, w1_filt=_np_bf16(w1_filt), w1_cplx=_np_bf16(w1_cplx), w3=_np_bf16(w3),
                fwd=fwd, inv=jnp.swapaxes(fwd, 1, 2))


SCH = 8


def _dft1_kernel(x_ref, w_ref, a_ref, *, n1):
    w = w_ref[...]
    for j in range(SCH):
        rhs = jnp.concatenate([x_ref[0, 0, :, j, :], x_ref[0, 1, :, j, :]], axis=0)
        res = jnp.dot(w, rhs.astype(BF16), preferred_element_type=F32)
        a_ref[0, :, j, :] = _pack2(res[:n1], res[n1:])


def _dft1_data(x, consts):
    B, L, C = x.shape
    n1 = consts["n1"]
    h1 = n1 // 2
    xv = x.reshape(B // 2, 2, h1, LANES, C)
    return pl.pallas_call(
        functools.partial(_dft1_kernel, n1=n1),
        out_shape=jax.ShapeDtypeStruct((B // 2, n1, LANES, C), F32),
        grid=(B // 2, LANES // SCH),
        in_specs=[pl.BlockSpec((1, 2, h1, SCH, C), lambda p, j: (p, 0, 0, j, 0)),
                  pl.BlockSpec((2 * n1, n1), lambda p, j: (0, 0))],
        out_specs=pl.BlockSpec((1, n1, SCH, C), lambda p, j: (p, 0, j, 0)),
        compiler_params=_params(2),
        name="dft1",
    )(xv, consts["w1_cplx"])


def _dft1f_kernel(x_ref, w_ref, a_ref, *, n1):
    C = HYENA_WIDTH
    w = w_ref[...]
    for j in range(SCH):
        rhs = jnp.concatenate([x_ref[:, j, :C], x_ref[:, j, C:]], axis=0)
        res = jnp.dot(w, rhs.astype(BF16), preferred_element_type=F32)
        a_ref[0, :, 0, j, :] = _pack2(res[:n1], res[n1:2 * n1])
        a_ref[0, :, 1, j, :] = _pack2(res[2 * n1:3 * n1], res[3 * n1:])


def _dft1_filter(kraw, consts):
    L, nf = kraw.shape
    C = HYENA_WIDTH
    n_ord = nf // (2 * C)
    n1 = consts["n1"]
    h1 = n1 // 2
    kv = kraw.reshape(h1, LANES, nf)
    return pl.pallas_call(
        functools.partial(_dft1f_kernel, n1=n1),
        out_shape=jax.ShapeDtypeStruct((n_ord, n1, 2, LANES, C), F32),
        grid=(n_ord, LANES // SCH),
        in_specs=[pl.BlockSpec((h1, SCH, 2 * C), lambda o, j: (0, j, o)),
                  pl.BlockSpec((4 * n1, n1), lambda o, j: (0, 0))],
        out_specs=pl.BlockSpec((1, n1, 2, SCH, C), lambda o, j: (o, 0, 0, j, 0)),
        compiler_params=_params(2),
        name="dft1f",
    )(kv, consts["w1_filt"])


KCH = 8


def _midf_kernel(a_ref, f_ref, inv_ref, b0_ref, h_ref):
    n2 = LANES
    sc = inv_ref[0]
    for k in range(KCH):
        p = jnp.concatenate(_unpack2(a_ref[0, k, :n2, :]), axis=0).astype(BF16)
        q = jnp.concatenate(_unpack2(a_ref[0, k, n2:, :]), axis=0).astype(BF16)
        h_re = jnp.dot(f_ref[k, :n2, :], p, preferred_element_type=F32)
        h_im = jnp.dot(f_ref[k, n2:, :], q, preferred_element_type=F32)
        h_ref[0, k] = _pack2((h_re - b0_ref[0]) * sc, h_im * sc)


def _filter_spectrum(af, inv_den, bwd0, consts):
    n_ord, n1, _, n2, C = af.shape
    a = af.reshape(n_ord, n1, 2 * n2, C)
    tab = pl.BlockSpec((KCH, 2 * n2, 2 * n2), lambda k, o: (k, 0, 0))
    vec = pl.BlockSpec((1, 1, C), lambda k, o: (o, 0, 0))
    return pl.pallas_call(
        _midf_kernel,
        out_shape=jax.ShapeDtypeStruct((n_ord, n1, n2, C), F32),
        grid=(n1 // KCH, n_ord),
        in_specs=[pl.BlockSpec((1, KCH, 2 * n2, C), lambda k, o: (o, k, 0, 0)), tab, vec, vec],
        out_specs=pl.BlockSpec((1, KCH, n2, C), lambda k, o: (o, k, 0, 0)),
        compiler_params=_params(2),
        name="midf",
    )(a, consts["fwd"], inv_den, bwd0)


def _mid_kernel(a_ref, f_ref, i_ref, h_ref, b_ref):
    n2 = LANES
    for k in range(KCH):
        a = jnp.concatenate(_unpack2(a_ref[0, k]), axis=0).astype(BF16)
        x = jnp.dot(f_ref[k], a, preferred_element_type=F32)
        xr, xi = x[:n2], x[n2:]
        hr, hi = _unpack2(h_ref[0, k])
        y = jnp.concatenate([xr * hr - xi * hi, xr * hi + xi * hr], axis=0)
        b = jnp.dot(i_ref[k], y.astype(BF16), preferred_element_type=F32)
        b_ref[0, k] = _pack2(b[:n2], b[n2:])


def _mid(a, hspec, order, consts):
    P, n1, n2, C = a.shape
    tab = pl.BlockSpec((KCH, 2 * n2, 2 * n2), lambda k, p: (k, 0, 0))
    return pl.pallas_call(
        _mid_kernel,
        out_shape=jax.ShapeDtypeStruct((P, n1, n2, C), F32),
        grid=(n1 // KCH, P),
        in_specs=[pl.BlockSpec((1, KCH, n2, C), lambda k, p: (p, k, 0, 0)),
                  tab, tab,
                  pl.BlockSpec((1, KCH, n2, C), lambda k, p: (order, k, 0, 0))],
        out_specs=pl.BlockSpec((1, KCH, n2, C), lambda k, p: (p, k, 0, 0)),
        compiler_params=_params(2),
        name="mid",
    )(a, consts["fwd"], consts["inv"], hspec)


def _dft3_kernel(b_ref, w_ref, v_ref, g_ref, skip_ref, z_ref, slab_ref, *, h1):
    w = w_ref[...]
    skip = skip_ref[0]
    for j in range(SCH):
        slab_ref[...] = b_ref[0, :, j, :]
        rhs = jnp.concatenate(_unpack2(slab_ref[...]), axis=0)
        y = jnp.dot(w, rhs.astype(BF16), preferred_element_type=F32)
        for r in range(2):
            yr = y[r * h1:(r + 1) * h1]
            z_ref[0, r, :, j, :] = g_ref[0, r, :, j, :] * (yr + v_ref[0, r, :, j, :] * skip)


def _dft3_gate(b5, v, gate, skip, consts):
    P, n1, n2, C = b5.shape
    h1 = n1 // 2
    B, L, _ = v.shape
    five = lambda t: t.reshape(P, 2, h1, n2, C)
    dat = pl.BlockSpec((1, 2, h1, SCH, C), lambda p, j: (p, 0, 0, j, 0))
    out = pl.pallas_call(
        functools.partial(_dft3_kernel, h1=h1),
        out_shape=jax.ShapeDtypeStruct((P, 2, h1, n2, C), F32),
        grid=(P, n2 // SCH),
        in_specs=[pl.BlockSpec((1, n1, SCH, C), lambda p, j: (p, 0, j, 0)),
                  pl.BlockSpec((n1, 2 * n1), lambda p, j: (0, 0)),
                  dat, dat,
                  pl.BlockSpec((1, C), lambda p, j: (0, 0))],
        out_specs=dat,
        scratch_shapes=[pltpu.VMEM((n1, C), F32)],
        compiler_params=_params(2),
        name="dft3",
    )(b5, consts["w3"], five(v), five(gate), skip.reshape(1, C).astype(F32))
    return out.reshape(B, L, C)


def _hyena(v, x1, x2, fw1, fb1, ff1, fw2, fb2, ff2, fw3, decay, skip):
    B, L, C = v.shape
    consts = _dft_constants(L)
    kraw, ksum = _filters(L, fw1, fb1, ff1, fw2, fb2, ff2, fw3, decay)
    ks = ksum.reshape(2, 2, C)
    inv_den = (1.0 / (ks[:, 0] + ks[:, 1])).reshape(2, 1, C)
    bwd0 = kraw[0].reshape(2, 2, C)[:, 1].reshape(2, 1, C)
    hspec = _filter_spectrum(_dft1_filter(kraw, consts), inv_den, bwd0, consts)
    z = v
    for o, gate in enumerate((x1, x2)):
        a5 = _dft1_data(z, consts)
        b5 = _mid(a5, hspec, o, consts)
        z = _dft3_gate(b5, z, gate, skip[o], consts)
    return z


def _merge_kernel(attn_ref, hy_ref, ga_ref, gh_ref, x_ref, mod_ref, wa_ref, wh_ref, wo_ref,
                  g1_ref, b1_ref, rwh_ref, rwl_ref, rb_ref, tri_ref,
                  x1_ref, h2_ref, route_ref, wts_ref, cnt_ref, carry_ref):
    first = (pl.program_id(0) == 0) & (pl.program_id(1) == 0)

    @pl.when(first)
    def _():
        carry_ref[...] = jnp.zeros_like(carry_ref)

    a = jnp.dot(attn_ref[0], wa_ref[...], preferred_element_type=F32)
    hy = jnp.dot(hy_ref[0].astype(BF16), wh_ref[...], preferred_element_type=F32)
    merged = ga_ref[0].astype(F32) * a + gh_ref[0].astype(F32) * hy
    y = jnp.dot(merged.astype(BF16), wo_ref[...], preferred_element_type=F32)
    gate1 = mod_ref[0, 2:3, :]
    shift2 = mod_ref[0, 3:4, :]
    scale2 = mod_ref[0, 4:5, :]
    x1 = _layer_norm(DN_ALPHA * x_ref[0] + gate1 * y, g1_ref[...], b1_ref[...])
    x1_ref[0] = x1
    h2 = x1 * (1.0 + scale2) + shift2
    half = h2.shape[1] // 2
    h2_ref[0] = _pack2(h2[:, :half], h2[:, half:])

    logits = _dot3(h2, rwh_ref[...], rwl_ref[...]) + rb_ref[...]
    tm = logits.shape[0]
    lane = lax.broadcasted_iota(jnp.int32, (tm, LANES), 1)
    lanef = lane.astype(F32)
    big = float(LANES)

    def first_lane(mask):
        return jnp.min(jnp.where(mask, lanef, big), axis=1, keepdims=True).astype(jnp.int32)

    gmask = lane < N_GROUPS
    gl = jnp.where(gmask, logits, NEG)
    gmax = jnp.max(gl, axis=1, keepdims=True)
    gidx = first_lane(gl == gmax)
    pg = 1.0 / jnp.sum(jnp.exp(gl - gmax), axis=1, keepdims=True)
    lo = ROUTE_OFF + gidx * EXPERTS_PER_GROUP
    emask = (lane >= lo) & (lane < lo + EXPERTS_PER_GROUP)
    el = jnp.where(emask, logits, NEG)
    v1 = jnp.max(el, axis=1, keepdims=True)
    i1 = first_lane(el == v1)
    el2 = jnp.where(emask & (lane != i1), logits, NEG)
    v2 = jnp.max(el2, axis=1, keepdims=True)
    i2 = first_lane(el2 == v2)
    e21 = jnp.exp(v2 - v1)
    w1 = pg / (1.0 + e21)
    w2 = pg * e21 / (1.0 + e21)

    sel1 = lane == i1
    sel2 = lane == i2
    onehot = jnp.where(sel1 | sel2, 1.0, 0.0)
    prefix = jnp.dot(tri_ref[...], onehot.astype(BF16), preferred_element_type=F32) + carry_ref[...]
    r1 = jnp.sum(jnp.where(sel1, prefix, 0.0), axis=1, keepdims=True)
    r2 = jnp.sum(jnp.where(sel2, prefix, 0.0), axis=1, keepdims=True)
    carry_ref[...] += jnp.sum(onehot, axis=0, keepdims=True)
    cnt_ref[...] = carry_ref[...]

    ranks = jnp.where(lane == 2, r1, jnp.where(lane == 3, r2, 0.0)).astype(jnp.int32)
    route_ref[0] = jnp.where(lane == 0, i1 - ROUTE_OFF, jnp.where(lane == 1, i2 - ROUTE_OFF, ranks))
    wts_ref[0] = jnp.where(lane == 0, w1, jnp.where(lane == 1, w2, 0.0))


def _merge(attn, hy, ga, gh, x, mod, w_attn_o, w_hy_o, w_out, ln1_g, ln1_b, rg_w, rg_b, re_w, re_b):
    B, S, D = x.shape
    tm = min(512, S)
    rw = jnp.zeros((D, LANES), F32).at[:, :N_GROUPS].set(rg_w).at[:, ROUTE_OFF:ROUTE_OFF + N_EXPERTS].set(re_w)
    rb = jnp.zeros((1, LANES), F32).at[0, :N_GROUPS].set(rg_b).at[0, ROUTE_OFF:ROUTE_OFF + N_EXPERTS].set(re_b)
    rwh, rwl = _split(rw)
    tri = (jnp.arange(tm)[:, None] > jnp.arange(tm)[None, :]).astype(BF16)
    row = lambda b, i: (b, i, 0)
    full = lambda r, c: pl.BlockSpec((r, c), lambda b, i: (0, 0))
    outs = [jax.ShapeDtypeStruct((B, S, D), F32), jax.ShapeDtypeStruct((B, S, D // 2), F32),
            jax.ShapeDtypeStruct((B, S, LANES), jnp.int32), jax.ShapeDtypeStruct((B, S, LANES), F32),
            jax.ShapeDtypeStruct((1, LANES), F32)]
    return pl.pallas_call(
        _merge_kernel,
        out_shape=outs,
        grid=(B, S // tm),
        in_specs=[pl.BlockSpec((1, tm, ATTN_WIDTH), row), pl.BlockSpec((1, tm, HYENA_WIDTH), row),
                  pl.BlockSpec((1, tm, D), row), pl.BlockSpec((1, tm, D), row), pl.BlockSpec((1, tm, D), row),
                  pl.BlockSpec((1, 6, D), lambda b, i: (b, 0, 0)),
                  full(ATTN_WIDTH, D), full(HYENA_WIDTH, D), full(D, D),
                  full(1, D), full(1, D), full(D, LANES), full(D, LANES), full(1, LANES), full(tm, tm)],
        out_specs=[pl.BlockSpec((1, tm, D), row), pl.BlockSpec((1, tm, D // 2), row),
                   pl.BlockSpec((1, tm, LANES), row), pl.BlockSpec((1, tm, LANES), row), full(1, LANES)],
        scratch_shapes=[pltpu.VMEM((1, LANES), F32)],
        compiler_params=_params(2),
        name="merge",
    )(attn, hy, ga, gh, x, mod, w_attn_o.astype(BF16), w_hy_o.astype(BF16), w_out.astype(BF16),
      ln1_g.reshape(1, D), ln1_b.reshape(1, D), rwh, rwl, rb, tri)


TOK_CH = 1024
WAIT_ROWS = 128


def _dispatch_kernel(poffs_ref, psz_ref, dest_ref, h_ref, xb_ref, zero_ref, sem, zsem):
    @pl.when(pl.program_id(0) == 0)
    def _():
        zero_ref[...] = jnp.zeros_like(zero_ref)

        def zcopy(e):
            start = pl.multiple_of(poffs_ref[e] + psz_ref[e] - MOE_BLOCK, MOE_BLOCK)
            return pltpu.make_async_copy(zero_ref, xb_ref.at[pl.ds(start, MOE_BLOCK)], zsem)

        def zissue(e, c):
            @pl.when(psz_ref[e] > 0)
            def _():
                zcopy(e).start()
            return c

        def zdrain(e, c):
            @pl.when(psz_ref[e] > 0)
            def _():
                zcopy(e).wait()
            return c

        lax.fori_loop(0, N_EXPERTS, zissue, 0)
        lax.fori_loop(0, N_EXPERTS, zdrain, 0)

        def tcopy(b):
            return pltpu.make_async_copy(zero_ref, xb_ref.at[pl.ds(pl.multiple_of(b * MOE_BLOCK, MOE_BLOCK), MOE_BLOCK)], zsem)

        first_free = (poffs_ref[N_EXPERTS - 1] + psz_ref[N_EXPERTS - 1]) // MOE_BLOCK
        n_blocks = xb_ref.shape[0] // MOE_BLOCK
        lax.fori_loop(first_free, n_blocks, lambda b, c: (tcopy(b).start(), c)[1], 0)
        lax.fori_loop(first_free, n_blocks, lambda b, c: (tcopy(b).wait(), c)[1], 0)

    def issue(g, c):
        for u in range(SUBLANES):
            for j in range(2):
                d = dest_ref[0, 0, 2 * SUBLANES * g + 2 * u + j]
                pltpu.make_async_copy(h_ref.at[g, pl.ds(u, 1)], xb_ref.at[pl.ds(d, 1)], sem).start(priority=j)
        return c

    lax.fori_loop(0, TOK_CH // SUBLANES, issue, 0)

    def drain(r, c):
        pltpu.make_async_copy(xb_ref.at[pl.ds(0, WAIT_ROWS)], xb_ref.at[pl.ds(0, WAIT_ROWS)], sem).wait()
        return c

    lax.fori_loop(0, 2 * TOK_CH // WAIT_ROWS, drain, 0)


def _dispatch(h2, dest, poffs, psizes, P):
    T, W = h2.shape
    nch = T // TOK_CH
    grid_spec = pltpu.PrefetchScalarGridSpec(
        num_scalar_prefetch=2,
        grid=(nch,),
        in_specs=[pl.BlockSpec((1, 1, 2 * TOK_CH), lambda i, po, ps: (i, 0, 0), memory_space=pltpu.SMEM),
                  pl.BlockSpec((TOK_CH // SUBLANES, SUBLANES, W), lambda i, po, ps: (i, 0, 0))],
        out_specs=pl.BlockSpec(memory_space=pl.ANY),
        scratch_shapes=[pltpu.VMEM((MOE_BLOCK, W), h2.dtype), pltpu.SemaphoreType.DMA(()), pltpu.SemaphoreType.DMA(())],
    )
    return pl.pallas_call(
        _dispatch_kernel,
        out_shape=jax.ShapeDtypeStruct((P, W), h2.dtype),
        grid_spec=grid_spec,
        compiler_params=_params(1),
        name="dispatch",
    )(poffs, psizes, dest, h2.reshape(T // SUBLANES, SUBLANES, W))


def _expert_kernel(be_ref, nu_ref, x_ref, w1_ref, w3_ref, w2_ref, y_ref, c1_ref, c3_ref, c2_ref):
    i = pl.program_id(0)
    used = i < nu_ref[0]
    fresh = (i == 0) | (be_ref[i] != be_ref[jnp.maximum(i - 1, 0)])

    @pl.when(used & fresh)
    def _():
        c1_ref[...] = w1_ref[0].astype(BF16)
        c3_ref[...] = w3_ref[0].astype(BF16)
        c2_ref[...] = w2_ref[0].astype(BF16)

    @pl.when(used)
    def _():
        xa, xb = _unpack2(x_ref[...])
        x = jnp.concatenate([xa, xb], axis=1).astype(BF16)
        a = jnp.dot(x, c1_ref[...], preferred_element_type=F32)
        g = jnp.dot(x, c3_ref[...], preferred_element_type=F32)
        hmid = (a * _sigmoid(a) * g).astype(BF16)
        y = jnp.dot(hmid, c2_ref[...], preferred_element_type=F32)
        half = y.shape[1] // 2
        y_ref[...] = _pack2(y[:, :half], y[:, half:])

    @pl.when(jnp.logical_not(used))
    def _():
        y_ref[...] = jnp.zeros_like(y_ref)


def _experts(xb, blk_e, n_used, w1, w3, w2):
    P, W = xb.shape
    E, D, DE = w1.shape
    nblk = P // MOE_BLOCK
    grid_spec = pltpu.PrefetchScalarGridSpec(
        num_scalar_prefetch=2,
        grid=(nblk,),
        in_specs=[pl.BlockSpec((MOE_BLOCK, W), lambda i, be, nu: (jnp.minimum(i, jnp.maximum(nu[0] - 1, 0)), 0)),
                  pl.BlockSpec((1, D, DE), lambda i, be, nu: (be[i], 0, 0)),
                  pl.BlockSpec((1, D, DE), lambda i, be, nu: (be[i], 0, 0)),
                  pl.BlockSpec((1, DE, D), lambda i, be, nu: (be[i], 0, 0))],
        out_specs=pl.BlockSpec((MOE_BLOCK, W), lambda i, be, nu: (i, 0)),
        scratch_shapes=[pltpu.VMEM((D, DE), BF16), pltpu.VMEM((D, DE), BF16), pltpu.VMEM((DE, D), BF16)],
    )
    return pl.pallas_call(
        _expert_kernel,
        out_shape=jax.ShapeDtypeStruct((P, W), F32),
        grid_spec=grid_spec,
        compiler_params=_params(1),
        name="experts",
    )(blk_e, n_used, xb, w1, w3, w2)


def _combine_kernel(dest_ref, wts_ref, x1_ref, mod_ref, g_ref, b_ref, yb_ref, o_ref, rows_ref, sem):
    def issue(g, c):
        for u in range(SUBLANES):
            for j in range(2):
                d = dest_ref[0, 0, 2 * SUBLANES * g + 2 * u + j]
                pltpu.make_async_copy(yb_ref.at[pl.ds(d, 1)], rows_ref.at[j, g, pl.ds(u, 1)], sem).start(priority=j)
        return c

    lax.fori_loop(0, TOK_CH // SUBLANES, issue, 0)

    def drain(r, c):
        unit = rows_ref.at[0, pl.ds(0, WAIT_ROWS // SUBLANES)]
        pltpu.make_async_copy(unit, unit, sem).wait()
        return c

    lax.fori_loop(0, 2 * TOK_CH // WAIT_ROWS, drain, 0)
    w = wts_ref[...]
    width = yb_ref.shape[1]
    y0 = jnp.concatenate(_unpack2(rows_ref[0].reshape(TOK_CH, width)), axis=1)
    y1 = jnp.concatenate(_unpack2(rows_ref[1].reshape(TOK_CH, width)), axis=1)
    y = w[:, 0:1] * y0 + w[:, 1:2] * y1
    gate2 = mod_ref[0, 5:6, :]
    o_ref[...] = _layer_norm(DN_ALPHA * x1_ref[...] + gate2 * y, g_ref[...], b_ref[...])


def _combine(yb, dest, wts, x1, mod, ln2_g, ln2_b, S):
    T, D = x1.shape
    nch = T // TOK_CH
    per_b = S // TOK_CH
    return pl.pallas_call(
        _combine_kernel,
        out_shape=jax.ShapeDtypeStruct((T, D), F32),
        grid=(nch,),
        in_specs=[pl.BlockSpec((1, 1, 2 * TOK_CH), lambda i: (i, 0, 0), memory_space=pltpu.SMEM),
                  pl.BlockSpec((TOK_CH, LANES), lambda i: (i, 0)),
                  pl.BlockSpec((TOK_CH, D), lambda i: (i, 0)),
                  pl.BlockSpec((1, 6, D), lambda i: (i // per_b, 0, 0)),
                  pl.BlockSpec((1, D), lambda i: (0, 0)),
                  pl.BlockSpec((1, D), lambda i: (0, 0)),
                  pl.BlockSpec(memory_space=pl.ANY)],
        out_specs=pl.BlockSpec((TOK_CH, D), lambda i: (i, 0)),
        scratch_shapes=[pltpu.VMEM((2, TOK_CH // SUBLANES, SUBLANES, yb.shape[1]), yb.dtype),
                        pltpu.SemaphoreType.DMA(())],
        compiler_params=_params(1),
        name="combine",
    )(dest, wts, x1, mod, ln2_g.reshape(1, D), ln2_b.reshape(1, D), yb)


def _moe(h2, x1, route, wts, counts, mod, w1, w3, w2, ln2_g, ln2_b):
    B, S, D = x1.shape
    T = B * S
    P = 2 * T + N_EXPERTS * MOE_BLOCK
    nblk = P // MOE_BLOCK
    nch = T // TOK_CH
    sizes = counts[0, ROUTE_OFF:ROUTE_OFF + N_EXPERTS].astype(jnp.int32)
    psizes = (sizes + MOE_BLOCK - 1) // MOE_BLOCK * MOE_BLOCK
    pends = jnp.cumsum(psizes)
    poffs = pends - psizes
    starts = jnp.arange(nblk, dtype=jnp.int32) * MOE_BLOCK
    blk_e = jnp.minimum(jnp.sum((pends[None, :] <= starts[:, None]).astype(jnp.int32), axis=1), N_EXPERTS - 1)
    n_used = pends[-1:] // MOE_BLOCK
    r4 = route.reshape(T, LANES)[:, :4]
    sel = r4[:, :2, None] == jnp.arange(N_EXPERTS, dtype=jnp.int32)[None, None, :]
    dest = (r4[:, 2:4] + jnp.sum(jnp.where(sel, poffs[None, None, :], 0), axis=-1)).reshape(nch, 1, 2 * TOK_CH)
    xb = _dispatch(h2.reshape(T, D // 2), dest, poffs, psizes, P)
    yb = _experts(xb, blk_e, n_used, w1, w3, w2)
    out = _combine(yb, dest, wts.reshape(T, LANES), x1.reshape(T, D), mod, ln2_g, ln2_b, S)
    return out.reshape(B, S, D)


def _layer(x, c, w_ada, b_ada, w_in, conv_w, conv_b, fw1, fb1, ff1, fw2, fb2, ff2, fw3, decay, skip,
           w_hy_o, w_attn_o, attn_sink, w_out, ln1_g, ln1_b, rg_w, rg_b, re_w, re_b, ew1, ew3, ew2,
           ln2_g, ln2_b):
    mod = _ada(c, w_ada, b_ada)
    q, kv, hv, hx1, hx2, ga, gh = _in_proj(x, mod, w_in, conv_w, conv_b)
    attn = _attention(q, kv, attn_sink)
    hy = _hyena(hv, hx1, hx2, fw1, fb1, ff1, fw2, fb2, ff2, fw3, decay, skip)
    x1, h2, route, wts, counts = _merge(attn, hy, ga, gh, x, mod, w_attn_o, w_hy_o, w_out,
                                        ln1_g, ln1_b, rg_w, rg_b, re_w, re_b)
    return _moe(h2, x1, route, wts, counts, mod, ew1, ew3, ew2, ln2_g, ln2_b)


def kernel(x, c, w_ada, b_ada, w_in, conv_w, conv_b, filt_w1, filt_b1, filt_freq1, filt_w2, filt_b2, filt_freq2, filt_w3, filt_decay, hy_skip, w_hy_o, w_attn_o, attn_sink, w_out, ln1_g, ln1_b, router_group_w, router_group_b, router_expert_w, router_expert_b, exp_w1, exp_w3, exp_w2, ln2_g, ln2_b):
    for l in range(w_ada.shape[0]):
        x = _layer(x, c, w_ada[l], b_ada[l], w_in[l], conv_w[l], conv_b[l], filt_w1[l], filt_b1[l],
                   filt_freq1[l], filt_w2[l], filt_b2[l], filt_freq2[l], filt_w3[l], filt_decay[l],
                   hy_skip[l], w_hy_o[l], w_attn_o[l], attn_sink[l], w_out[l], ln1_g[l], ln1_b[l],
                   router_group_w[l], router_group_b[l], router_expert_w[l], router_expert_b[l],
                   exp_w1[l], exp_w3[l], exp_w2[l], ln2_g[l], ln2_b[l])
    return x
```

```python
import functools
import math

import numpy as np
import jax
import jax.numpy as jnp
from jax import lax
from jax.experimental import pallas as pl
from jax.experimental.pallas import tpu as pltpu
from jax.experimental.pallas import tpu_sc as plsc

F32 = jnp.float32
BF16 = jnp.bfloat16

N_HEADS = 8
N_KV_HEADS = 2
HEAD_DIM = 64
ATTN_WIDTH = N_HEADS * HEAD_DIM
KV_WIDTH = N_KV_HEADS * HEAD_DIM
WINDOW = 128
BLOCK_Q = 128
HYENA_WIDTH = 512
FILTER_EMB = 33
FILTER_BANDS = (FILTER_EMB - 1) // 2
WINDOW_SHIFT = 0.05
N_GROUPS = 8
EXPERTS_PER_GROUP = 8
N_EXPERTS = N_GROUPS * EXPERTS_PER_GROUP
D_EXPERT = 512
MOE_BLOCK = 512
LN_EPS = 1e-5
DEPTH = 1
DN_ALPHA = (2.0 * DEPTH) ** 0.25
NEG = -1e30

LANES = 128
SUBLANES = 8
ROUTE_OFF = N_GROUPS
VMEM_LIMIT = 56 * 1024 * 1024


def _params(n_axes, vmem=VMEM_LIMIT):
    return pltpu.CompilerParams(dimension_semantics=("arbitrary",) * n_axes, vmem_limit_bytes=vmem)


def _split(a):
    hi = a.astype(BF16)
    lo = (a - hi.astype(F32)).astype(BF16)
    return hi, lo


def _dot3(a, b_hi, b_lo):
    a_hi, a_lo = _split(a)
    acc = jnp.dot(a_hi, b_hi, preferred_element_type=F32)
    acc = acc + jnp.dot(a_hi, b_lo, preferred_element_type=F32)
    acc = acc + jnp.dot(a_lo, b_hi, preferred_element_type=F32)
    return acc


def _pack2(a, b):
    ia = lax.bitcast_convert_type(a.astype(BF16).astype(F32), jnp.int32)
    ib = lax.bitcast_convert_type(b.astype(BF16).astype(F32), jnp.int32)
    return lax.bitcast_convert_type(ia | lax.shift_right_logical(ib, 16), F32)


def _unpack2(p):
    p = lax.bitcast_convert_type(p, jnp.int32)
    a = lax.bitcast_convert_type(p & jnp.int32(-65536), F32)
    b = lax.bitcast_convert_type(lax.shift_left(p, 16), F32)
    return a, b


def _sigmoid(x):
    return 1.0 / (1.0 + jnp.exp(-x))


def _layer_norm(r, g, b):
    mu = jnp.mean(r, axis=-1, keepdims=True)
    d = r - mu
    var = jnp.mean(d * d, axis=-1, keepdims=True)
    return d * lax.rsqrt(var + LN_EPS) * g + b


def _ada_kernel(c_ref, wh_ref, wl_ref, b_ref, o_ref):
    c = c_ref[...]
    s = c * _sigmoid(c)
    o_ref[...] = _dot3(s, wh_ref[...], wl_ref[...]) + b_ref[...]


def _ada(c, w_ada, b_ada):
    B, D = c.shape
    n_out = w_ada.shape[1]
    rows = 8
    cp = jnp.zeros((rows, D), F32).at[:B].set(c)
    wh, wl = _split(w_ada)
    tn = 1024
    out = pl.pallas_call(
        _ada_kernel,
        out_shape=jax.ShapeDtypeStruct((rows, n_out), F32),
        grid=(n_out // tn,),
        in_specs=[pl.BlockSpec((rows, D), lambda j: (0, 0)),
                  pl.BlockSpec((D, tn), lambda j: (0, j)),
                  pl.BlockSpec((D, tn), lambda j: (0, j)),
                  pl.BlockSpec((1, tn), lambda j: (0, j))],
        out_specs=pl.BlockSpec((rows, tn), lambda j: (0, j)),
        compiler_params=_params(1),
        name="ada",
    )(cp, wh, wl, b_ada.reshape(1, n_out))
    return out[:B].reshape(B, 6, D)


def _inproj_kernel(x_ref, xp_ref, xn_ref, mod_ref, w_ref, cw_ref, cb_ref,
                   q_ref, kv_ref, v_ref, x1_ref, x2_ref, ga_ref, gh_ref):
    i = pl.program_id(1)
    n = pl.num_programs(1)
    C = HYENA_WIDTH
    x = x_ref[0]
    tm, D = x.shape
    shift = mod_ref[0, 0:1, :]
    scale = mod_ref[0, 1:2, :]
    h = (x * (1.0 + scale) + shift).astype(BF16)

    def seg(lo, hi):
        return jnp.dot(h, w_ref[:, lo:hi], preferred_element_type=F32)

    o = 0
    q_ref[0] = (seg(o, o + ATTN_WIDTH) * (HEAD_DIM ** -0.5)).astype(BF16)
    o += ATTN_WIDTH
    kv_ref[0] = seg(o, o + 2 * KV_WIDTH).astype(BF16)
    o += 2 * KV_WIDTH

    u = seg(o, o + 3 * C)
    xe = jnp.concatenate([xp_ref[0], xn_ref[0]], axis=0)
    he = (xe * (1.0 + scale) + shift).astype(BF16)
    ue = jnp.dot(he, w_ref[:, o:o + 3 * C], preferred_element_type=F32)
    prow = jnp.where(i > 0, ue[SUBLANES - 1:SUBLANES], 0.0)
    nrow = jnp.where(i < n - 1, ue[SUBLANES:SUBLANES + 1], 0.0)
    rid = lax.broadcasted_iota(jnp.int32, (tm, 1), 0)
    up = jnp.where(rid == 0, prow, pltpu.roll(u, 1, 0))
    dn = jnp.where(rid == tm - 1, nrow, pltpu.roll(u, tm - 1, 0))
    conv = cw_ref[0:1, :] * up + cw_ref[1:2, :] * u + cw_ref[2:3, :] * dn + cb_ref[...]
    v_ref[0] = conv[:, :C]
    x1_ref[0] = conv[:, C:2 * C]
    x2_ref[0] = conv[:, 2 * C:]
    o += 3 * C

    ga_ref[0] = _sigmoid(seg(o, o + D)).astype(BF16)
    o += D
    gh_ref[0] = _sigmoid(seg(o, o + D)).astype(BF16)


def _in_proj(x, mod, w_in, conv_w, conv_b):
    B, S, D = x.shape
    C = HYENA_WIDTH
    tm = min(512, S)
    r8 = tm // SUBLANES
    nb8 = S // SUBLANES
    wb = w_in.astype(BF16)
    nw = wb.shape[1]
    row = lambda b, i: (b, i, 0)
    shapes = [(ATTN_WIDTH, BF16), (2 * KV_WIDTH, BF16), (C, F32), (C, F32), (C, F32), (D, BF16), (D, BF16)]
    return pl.pallas_call(
        _inproj_kernel,
        out_shape=[jax.ShapeDtypeStruct((B, S, w), dt) for w, dt in shapes],
        grid=(B, S // tm),
        in_specs=[pl.BlockSpec((1, tm, D), row),
                  pl.BlockSpec((1, SUBLANES, D), lambda b, i: (b, jnp.maximum(i * r8 - 1, 0), 0)),
                  pl.BlockSpec((1, SUBLANES, D), lambda b, i: (b, jnp.minimum((i + 1) * r8, nb8 - 1), 0)),
                  pl.BlockSpec((1, 6, D), lambda b, i: (b, 0, 0)),
                  pl.BlockSpec((D, nw), lambda b, i: (0, 0)),
                  pl.BlockSpec((3, 3 * C), lambda b, i: (0, 0)),
                  pl.BlockSpec((1, 3 * C), lambda b, i: (0, 0))],
        out_specs=[pl.BlockSpec((1, tm, w), row) for w, _ in shapes],
        compiler_params=_params(2),
        name="in_proj",
    )(x, x, x, mod, wb, conv_w.astype(F32), conv_b.reshape(1, 3 * C).astype(F32))


ATT_TQ = 512


def _attn_kernel(sink_ref, q_ref, kvp_ref, kvc_ref, kvn_ref, bias_ref, o_ref, kv_scr, *, seq_len):
    i = pl.program_id(1)
    Q = BLOCK_Q
    TQ = q_ref.shape[1]
    G = N_HEADS // N_KV_HEADS
    kv_scr[0:Q] = kvp_ref[0]
    kv_scr[Q:Q + TQ] = kvc_ref[0]
    kv_scr[Q + TQ:] = kvn_ref[0]
    col = lax.broadcasted_iota(jnp.int32, (1, 3 * Q), 1)
    rhead = lax.broadcasted_iota(jnp.int32, (G * Q, 1), 0) // Q
    for j in range(TQ // Q):
        kpos = i * TQ + (j - 1) * Q + col
        colbias = jnp.where((kpos >= 0) & (kpos < seq_len), 0.0, NEG)
        for kv in range(N_KV_HEADS):
            kk = kv_scr[j * Q:(j + 3) * Q, kv * HEAD_DIM:(kv + 1) * HEAD_DIM]
            vv = kv_scr[j * Q:(j + 3) * Q, KV_WIDTH + kv * HEAD_DIM:KV_WIDTH + (kv + 1) * HEAD_DIM]
            heads = [kv * G + g for g in range(G)]
            qg = jnp.concatenate([q_ref[0, j * Q:(j + 1) * Q, h * HEAD_DIM:(h + 1) * HEAD_DIM] for h in heads], axis=0)
            s = lax.dot_general(qg, kk, (((1,), (1,)), ((), ())), preferred_element_type=F32)
            s = s + bias_ref[kv] + colbias
            snk = jnp.where(rhead == 0, sink_ref[heads[0]],
                            jnp.where(rhead == 1, sink_ref[heads[1]],
                                      jnp.where(rhead == 2, sink_ref[heads[2]], sink_ref[heads[3]])))
            m = jnp.maximum(jnp.max(s, axis=1, keepdims=True), snk)
            p = jnp.exp(s - m)
            den = jnp.sum(p, axis=1, keepdims=True) + jnp.exp(snk - m)
            o = jnp.dot(p.astype(BF16), vv, preferred_element_type=F32) / den
            for g, h in enumerate(heads):
                o_ref[0, j * Q:(j + 1) * Q, h * HEAD_DIM:(h + 1) * HEAD_DIM] = o[g * Q:(g + 1) * Q].astype(BF16)


def _attention(q, kv, sink):
    B, S, _ = q.shape
    Q = BLOCK_Q
    TQ = min(ATT_TQ, S)
    r = TQ // Q
    nq = S // Q
    G = N_HEADS // N_KV_HEADS
    assert G == 4
    a = jnp.arange(Q)[:, None]
    j = jnp.arange(3 * Q)[None, :]
    rel = jnp.abs(j - Q - a).astype(F32)
    slopes = 2.0 ** (-8.0 * jnp.arange(1, N_HEADS + 1, dtype=F32) / N_HEADS)
    bias = jnp.where(rel[None] <= WINDOW, -slopes[:, None, None] * rel[None], NEG).astype(F32)
    bias = bias.reshape(N_KV_HEADS, G * Q, 3 * Q)
    cur = lambda b, i: (b, i, 0)
    return pl.pallas_call(
        functools.partial(_attn_kernel, seq_len=S),
        out_shape=jax.ShapeDtypeStruct((B, S, ATTN_WIDTH), BF16),
        grid=(B, S // TQ),
        in_specs=[pl.BlockSpec(memory_space=pltpu.SMEM),
                  pl.BlockSpec((1, TQ, ATTN_WIDTH), cur),
                  pl.BlockSpec((1, Q, 2 * KV_WIDTH), lambda b, i: (b, jnp.maximum(i * r - 1, 0), 0)),
                  pl.BlockSpec((1, TQ, 2 * KV_WIDTH), cur),
                  pl.BlockSpec((1, Q, 2 * KV_WIDTH), lambda b, i: (b, jnp.minimum((i + 1) * r, nq - 1), 0)),
                  pl.BlockSpec((N_KV_HEADS, G * Q, 3 * Q), lambda b, i: (0, 0, 0))],
        out_specs=pl.BlockSpec((1, TQ, ATTN_WIDTH), cur),
        scratch_shapes=[pltpu.VMEM((TQ + 2 * Q, 2 * KV_WIDTH), BF16)],
        compiler_params=_params(2),
        name="attn",
    )(sink.astype(F32), q, kv, kv, kv, bias)


def _filter_kernel(z_ref, w1h, w1l, b1_ref, f1_ref, w2h, w2l, b2_ref, f2_ref, w3h, w3l, dec_ref,
                   k_ref, s_ref):
    i = pl.program_id(0)
    z = z_ref[...]
    h1 = jnp.sin(f1_ref[...] * (_dot3(z, w1h[...], w1l[...]) + b1_ref[...]))
    h2 = jnp.sin(f2_ref[...] * (_dot3(h1, w2h[...], w2l[...]) + b2_ref[...]))
    k = _dot3(h2, w3h[...], w3l[...])
    t = z[:, 0:1]
    k = k * (jnp.exp(-t * jnp.abs(dec_ref[...])) + WINDOW_SHIFT)
    k_ref[...] = k

    @pl.when(i == 0)
    def _():
        s_ref[...] = jnp.zeros_like(s_ref)

    s_ref[...] += jnp.sum(jnp.abs(k), axis=0, keepdims=True)


def _filter_embedding(L):
    t = np.linspace(0.0, 1.0, L, dtype=np.float32).astype(np.float64)[:, None]
    w = (2.0 * math.pi * np.arange(L, dtype=np.float32) / np.float32(L)).astype(np.float64)[:, None]
    bands = np.linspace(1e-4, FILTER_BANDS - 1, FILTER_BANDS, dtype=np.float32).astype(np.float64)[None, :]
    bw = (bands.astype(np.float32) * w.astype(np.float32)).astype(np.float64)
    z = np.concatenate([t, np.cos(bw), -np.sin(bw)], axis=-1)
    zp = np.zeros((L, LANES), np.float32)
    zp[:, :FILTER_EMB] = z.astype(np.float32)
    return jnp.asarray(zp)


def _pad2(a, r, c):
    return jnp.zeros((r, c), F32).at[:a.shape[0], :a.shape[1]].set(a.astype(F32))


def _filters(L, fw1, fb1, ff1, fw2, fb2, ff2, fw3, decay):
    H = LANES
    nf = fw3.shape[1]
    z = _filter_embedding(L)
    w1h, w1l = _split(_pad2(fw1, H, H))
    w2h, w2l = _split(_pad2(fw2, H, H))
    w3h, w3l = _split(_pad2(fw3, H, nf))
    b1 = _pad2(fb1[None], 1, H)
    f1 = _pad2(ff1[None], 1, H)
    b2 = _pad2(fb2[None], 1, H)
    f2 = _pad2(ff2[None], 1, H)
    tr = min(512, L)
    full = lambda r, c: pl.BlockSpec((r, c), lambda i: (0, 0))
    return pl.pallas_call(
        _filter_kernel,
        out_shape=[jax.ShapeDtypeStruct((L, nf), F32), jax.ShapeDtypeStruct((1, nf), F32)],
        grid=(L // tr,),
        in_specs=[pl.BlockSpec((tr, H), lambda i: (i, 0)),
                  full(H, H), full(H, H), full(1, H), full(1, H),
                  full(H, H), full(H, H), full(1, H), full(1, H),
                  full(H, nf), full(H, nf), full(1, nf)],
        out_specs=[pl.BlockSpec((tr, nf), lambda i: (i, 0)), full(1, nf)],
        compiler_params=_params(1),
        name="filter",
    )(z, w1h, w1l, b1, f1, w2h, w2l, b2, f2, w3h, w3l, decay.reshape(1, nf).astype(F32))


def _np_bf16(m64):
    return jnp.asarray(m64.astype(np.float32).astype(BF16))


def _dft_constants(L):
    N = 2 * L
    n2 = LANES
    n1 = N // n2
    h1 = n1 // 2
    k1 = np.arange(n1)[:, None]
    s1 = np.arange(h1)[None, :]
    ang = -2.0 * np.pi * ((k1 * s1) % n1) / n1
    wr, wi = np.cos(ang), np.sin(ang)
    w1_filt = np.block([[wr, wr], [wi, wi], [wr, -wr], [wi, -wi]])
    w1_cplx = np.block([[wr, -wi], [wi, wr]])
    vr, vi = wr.T / N, -wi.T / N
    w3 = np.block([[vr, -vi], [vi, vr]])
    k2 = np.arange(n2)[:, None]
    s2 = np.arange(n2)[None, :]
    a2 = -2.0 * np.pi * ((k2 * s2) % n2) / n2
    w2r, w2i = jnp.asarray(np.cos(a2), F32), jnp.asarray(np.sin(a2), F32)
    at = -2.0 * np.pi * ((np.arange(n1)[:, None] * s2) % N) / N
    twr, twi = jnp.asarray(np.cos(at), F32), jnp.asarray(np.sin(at), F32)
    mr = w2r[None] * twr[:, None, :] - w2i[None] * twi[:, None, :]
    mi = w2r[None] * twi[:, None, :] + w2i[None] * twr[:, None, :]
    fwd = jnp.concatenate([jnp.concatenate([mr, -mi], axis=2),
                           jnp.concatenate([mi, mr], axis=2)], axis=1)
    fwd = fwd.astype(BF16)
    return dict(n1=n1, w1_filt=_np_bf16(w1_filt), w1_cplx=_np_bf16(w1_cplx), w3=_np_bf16(w3),
                fwd=fwd, inv=jnp.swapaxes(fwd, 1, 2))


SCH = 8


def _dft1_kernel(x_ref, w_ref, a_ref, *, n1):
    w = w_ref[...]
    for j in range(SCH):
        rhs = jnp.concatenate([x_ref[0, 0, :, j, :], x_ref[0, 1, :, j, :]], axis=0)
        res = jnp.dot(w, rhs.astype(BF16), preferred_element_type=F32)
        a_ref[0, :, j, :] = _pack2(res[:n1], res[n1:])


def _dft1_data(x, consts):
    B, L, C = x.shape
    n1 = consts["n1"]
    h1 = n1 // 2
    xv = x.reshape(B // 2, 2, h1, LANES, C)
    return pl.pallas_call(
        functools.partial(_dft1_kernel, n1=n1),
        out_shape=jax.ShapeDtypeStruct((B // 2, n1, LANES, C), F32),
        grid=(B // 2, LANES // SCH),
        in_specs=[pl.BlockSpec((1, 2, h1, SCH, C), lambda p, j: (p, 0, 0, j, 0)),
                  pl.BlockSpec((2 * n1, n1), lambda p, j: (0, 0))],
        out_specs=pl.BlockSpec((1, n1, SCH, C), lambda p, j: (p, 0, j, 0)),
        compiler_params=_params(2),
        name="dft1",
    )(xv, consts["w1_cplx"])


def _dft1f_kernel(x_ref, w_ref, a_ref, *, n1):
    C = HYENA_WIDTH
    w = w_ref[...]
    for j in range(SCH):
        rhs = jnp.concatenate([x_ref[:, j, :C], x_ref[:, j, C:]], axis=0)
        res = jnp.dot(w, rhs.astype(BF16), preferred_element_type=F32)
        a_ref[0, :, 0, j, :] = _pack2(res[:n1], res[n1:2 * n1])
        a_ref[0, :, 1, j, :] = _pack2(res[2 * n1:3 * n1], res[3 * n1:])


def _dft1_filter(kraw, consts):
    L, nf = kraw.shape
    C = HYENA_WIDTH
    n_ord = nf // (2 * C)
    n1 = consts["n1"]
    h1 = n1 // 2
    kv = kraw.reshape(h1, LANES, nf)
    return pl.pallas_call(
        functools.partial(_dft1f_kernel, n1=n1),
        out_shape=jax.ShapeDtypeStruct((n_ord, n1, 2, LANES, C), F32),
        grid=(n_ord, LANES // SCH),
        in_specs=[pl.BlockSpec((h1, SCH, 2 * C), lambda o, j: (0, j, o)),
                  pl.BlockSpec((4 * n1, n1), lambda o, j: (0, 0))],
        out_specs=pl.BlockSpec((1, n1, 2, SCH, C), lambda o, j: (o, 0, 0, j, 0)),
        compiler_params=_params(2),
        name="dft1f",
    )(kv, consts["w1_filt"])


KCH = 8


def _midf_kernel(a_ref, f_ref, inv_ref, b0_ref, h_ref):
    n2 = LANES
    sc = inv_ref[0]
    for k in range(KCH):
        p = jnp.concatenate(_unpack2(a_ref[0, k, :n2, :]), axis=0).astype(BF16)
        q = jnp.concatenate(_unpack2(a_ref[0, k, n2:, :]), axis=0).astype(BF16)
        h_re = jnp.dot(f_ref[k, :n2, :], p, preferred_element_type=F32)
        h_im = jnp.dot(f_ref[k, n2:, :], q, preferred_element_type=F32)
        h_ref[0, k] = _pack2((h_re - b0_ref[0]) * sc, h_im * sc)


def _filter_spectrum(af, inv_den, bwd0, consts):
    n_ord, n1, _, n2, C = af.shape
    a = af.reshape(n_ord, n1, 2 * n2, C)
    tab = pl.BlockSpec((KCH, 2 * n2, 2 * n2), lambda k, o: (k, 0, 0))
    vec = pl.BlockSpec((1, 1, C), lambda k, o: (o, 0, 0))
    return pl.pallas_call(
        _midf_kernel,
        out_shape=jax.ShapeDtypeStruct((n_ord, n1, n2, C), F32),
        grid=(n1 // KCH, n_ord),
        in_specs=[pl.BlockSpec((1, KCH, 2 * n2, C), lambda k, o: (o, k, 0, 0)), tab, vec, vec],
        out_specs=pl.BlockSpec((1, KCH, n2, C), lambda k, o: (o, k, 0, 0)),
        compiler_params=_params(2),
        name="midf",
    )(a, consts["fwd"], inv_den, bwd0)


def _mid_kernel(a_ref, f_ref, i_ref, h_ref, b_ref):
    n2 = LANES
    for k in range(KCH):
        a = jnp.concatenate(_unpack2(a_ref[0, k]), axis=0).astype(BF16)
        x = jnp.dot(f_ref[k], a, preferred_element_type=F32)
        xr, xi = x[:n2], x[n2:]
        hr, hi = _unpack2(h_ref[0, k])
        y = jnp.concatenate([xr * hr - xi * hi, xr * hi + xi * hr], axis=0)
        b = jnp.dot(i_ref[k], y.astype(BF16), preferred_element_type=F32)
        b_ref[0, k] = _pack2(b[:n2], b[n2:])


def _mid(a, hspec, order, consts):
    P, n1, n2, C = a.shape
    tab = pl.BlockSpec((KCH, 2 * n2, 2 * n2), lambda k, p: (k, 0, 0))
    return pl.pallas_call(
        _mid_kernel,
        out_shape=jax.ShapeDtypeStruct((P, n1, n2, C), F32),
        grid=(n1 // KCH, P),
        in_specs=[pl.BlockSpec((1, KCH, n2, C), lambda k, p: (p, k, 0, 0)),
                  tab, tab,
                  pl.BlockSpec((1, KCH, n2, C), lambda k, p: (order, k, 0, 0))],
        out_specs=pl.BlockSpec((1, KCH, n2, C), lambda k, p: (p, k, 0, 0)),
        compiler_params=_params(2),
        name="mid",
    )(a, consts["fwd"], consts["inv"], hspec)


def _dft3_kernel(b_ref, w_ref, v_ref, g_ref, skip_ref, z_ref, slab_ref, *, h1):
    w = w_ref[...]
    skip = skip_ref[0]
    for j in range(SCH):
        slab_ref[...] = b_ref[0, :, j, :]
        rhs = jnp.concatenate(_unpack2(slab_ref[...]), axis=0)
        y = jnp.dot(w, rhs.astype(BF16), preferred_element_type=F32)
        for r in range(2):
            yr = y[r * h1:(r + 1) * h1]
            z_ref[0, r, :, j, :] = g_ref[0, r, :, j, :] * (yr + v_ref[0, r, :, j, :] * skip)


def _dft3_gate(b5, v, gate, skip, consts):
    P, n1, n2, C = b5.shape
    h1 = n1 // 2
    B, L, _ = v.shape
    five = lambda t: t.reshape(P, 2, h1, n2, C)
    dat = pl.BlockSpec((1, 2, h1, SCH, C), lambda p, j: (p, 0, 0, j, 0))
    out = pl.pallas_call(
        functools.partial(_dft3_kernel, h1=h1),
        out_shape=jax.ShapeDtypeStruct((P, 2, h1, n2, C), F32),
        grid=(P, n2 // SCH),
        in_specs=[pl.BlockSpec((1, n1, SCH, C), lambda p, j: (p, 0, j, 0)),
                  pl.BlockSpec((n1, 2 * n1), lambda p, j: (0, 0)),
                  dat, dat,
                  pl.BlockSpec((1, C), lambda p, j: (0, 0))],
        out_specs=dat,
        scratch_shapes=[pltpu.VMEM((n1, C), F32)],
        compiler_params=_params(2),
        name="dft3",
    )(b5, consts["w3"], five(v), five(gate), skip.reshape(1, C).astype(F32))
    return out.reshape(B, L, C)


def _hyena(v, x1, x2, fw1, fb1, ff1, fw2, fb2, ff2, fw3, decay, skip):
    B, L, C = v.shape
    consts = _dft_constants(L)
    kraw, ksum = _filters(L, fw1, fb1, ff1, fw2, fb2, ff2, fw3, decay)
    ks = ksum.reshape(2, 2, C)
    inv_den = (1.0 / (ks[:, 0] + ks[:, 1])).reshape(2, 1, C)
    bwd0 = kraw[0].reshape(2, 2, C)[:, 1].reshape(2, 1, C)
    hspec = _filter_spectrum(_dft1_filter(kraw, consts), inv_den, bwd0, consts)
    z = v
    for o, gate in enumerate((x1, x2)):
        a5 = _dft1_data(z, consts)
        b5 = _mid(a5, hspec, o, consts)
        z = _dft3_gate(b5, z, gate, skip[o], consts)
    return z


def _merge_kernel(attn_ref, hy_ref, ga_ref, gh_ref, x_ref, mod_ref, wa_ref, wh_ref, wo_ref,
                  g1_ref, b1_ref, rwh_ref, rwl_ref, rb_ref, tri_ref,
                  x1_ref, h2_ref, route_ref, wts_ref, cnt_ref, carry_ref):
    first = (pl.program_id(0) == 0) & (pl.program_id(1) == 0)

    @pl.when(first)
    def _():
        carry_ref[...] = jnp.zeros_like(carry_ref)

    a = jnp.dot(attn_ref[0], wa_ref[...], preferred_element_type=F32)
    hy = jnp.dot(hy_ref[0].astype(BF16), wh_ref[...], preferred_element_type=F32)
    merged = ga_ref[0].astype(F32) * a + gh_ref[0].astype(F32) * hy
    y = jnp.dot(merged.astype(BF16), wo_ref[...], preferred_element_type=F32)
    gate1 = mod_ref[0, 2:3, :]
    shift2 = mod_ref[0, 3:4, :]
    scale2 = mod_ref[0, 4:5, :]
    x1 = _layer_norm(DN_ALPHA * x_ref[0] + gate1 * y, g1_ref[...], b1_ref[...])
    x1_ref[0] = x1
    h2 = x1 * (1.0 + scale2) + shift2
    half = h2.shape[1] // 2
    h2_ref[0] = _pack2(h2[:, :half], h2[:, half:])

    logits = _dot3(h2, rwh_ref[...], rwl_ref[...]) + rb_ref[...]
    tm = logits.shape[0]
    lane = lax.broadcasted_iota(jnp.int32, (tm, LANES), 1)
    lanef = lane.astype(F32)
    big = float(LANES)

    def first_lane(mask):
        return jnp.min(jnp.where(mask, lanef, big), axis=1, keepdims=True).astype(jnp.int32)

    gmask = lane < N_GROUPS
    gl = jnp.where(gmask, logits, NEG)
    gmax = jnp.max(gl, axis=1, keepdims=True)
    gidx = first_lane(gl == gmax)
    pg = 1.0 / jnp.sum(jnp.exp(gl - gmax), axis=1, keepdims=True)
    lo = ROUTE_OFF + gidx * EXPERTS_PER_GROUP
    emask = (lane >= lo) & (lane < lo + EXPERTS_PER_GROUP)
    el = jnp.where(emask, logits, NEG)
    v1 = jnp.max(el, axis=1, keepdims=True)
    i1 = first_lane(el == v1)
    el2 = jnp.where(emask & (lane != i1), logits, NEG)
    v2 = jnp.max(el2, axis=1, keepdims=True)
    i2 = first_lane(el2 == v2)
    e21 = jnp.exp(v2 - v1)
    w1 = pg / (1.0 + e21)
    w2 = pg * e21 / (1.0 + e21)

    sel1 = lane == i1
    sel2 = lane == i2
    onehot = jnp.where(sel1 | sel2, 1.0, 0.0)
    prefix = jnp.dot(tri_ref[...], onehot.astype(BF16), preferred_element_type=F32) + carry_ref[...]
    r1 = jnp.sum(jnp.where(sel1, prefix, 0.0), axis=1, keepdims=True)
    r2 = jnp.sum(jnp.where(sel2, prefix, 0.0), axis=1, keepdims=True)
    carry_ref[...] += jnp.sum(onehot, axis=0, keepdims=True)
    cnt_ref[...] = carry_ref[...]

    ranks = jnp.where(lane == 2, r1, jnp.where(lane == 3, r2, 0.0)).astype(jnp.int32)
    route_ref[0] = jnp.where(lane == 0, i1 - ROUTE_OFF, jnp.where(lane == 1, i2 - ROUTE_OFF, ranks))
    wts_ref[0] = jnp.where(lane == 0, w1, jnp.where(lane == 1, w2, 0.0))


def _merge(attn, hy, ga, gh, x, mod, w_attn_o, w_hy_o, w_out, ln1_g, ln1_b, rg_w, rg_b, re_w, re_b):
    B, S, D = x.shape
    tm = min(512, S)
    rw = jnp.zeros((D, LANES), F32).at[:, :N_GROUPS].set(rg_w).at[:, ROUTE_OFF:ROUTE_OFF + N_EXPERTS].set(re_w)
    rb = jnp.zeros((1, LANES), F32).at[0, :N_GROUPS].set(rg_b).at[0, ROUTE_OFF:ROUTE_OFF + N_EXPERTS].set(re_b)
    rwh, rwl = _split(rw)
    tri = (jnp.arange(tm)[:, None] > jnp.arange(tm)[None, :]).astype(BF16)
    row = lambda b, i: (b, i, 0)
    full = lambda r, c: pl.BlockSpec((r, c), lambda b, i: (0, 0))
    outs = [jax.ShapeDtypeStruct((B, S, D), F32), jax.ShapeDtypeStruct((B, S, D // 2), F32),
            jax.ShapeDtypeStruct((B, S, LANES), jnp.int32), jax.ShapeDtypeStruct((B, S, LANES), F32),
            jax.ShapeDtypeStruct((1, LANES), F32)]
    return pl.pallas_call(
        _merge_kernel,
        out_shape=outs,
        grid=(B, S // tm),
        in_specs=[pl.BlockSpec((1, tm, ATTN_WIDTH), row), pl.BlockSpec((1, tm, HYENA_WIDTH), row),
                  pl.BlockSpec((1, tm, D), row), pl.BlockSpec((1, tm, D), row), pl.BlockSpec((1, tm, D), row),
                  pl.BlockSpec((1, 6, D), lambda b, i: (b, 0, 0)),
                  full(ATTN_WIDTH, D), full(HYENA_WIDTH, D), full(D, D),
                  full(1, D), full(1, D), full(D, LANES), full(D, LANES), full(1, LANES), full(tm, tm)],
        out_specs=[pl.BlockSpec((1, tm, D), row), pl.BlockSpec((1, tm, D // 2), row),
                   pl.BlockSpec((1, tm, LANES), row), pl.BlockSpec((1, tm, LANES), row), full(1, LANES)],
        scratch_shapes=[pltpu.VMEM((1, LANES), F32)],
        compiler_params=_params(2),
        name="merge",
    )(attn, hy, ga, gh, x, mod, w_attn_o.astype(BF16), w_hy_o.astype(BF16), w_out.astype(BF16),
      ln1_g.reshape(1, D), ln1_b.reshape(1, D), rwh, rwl, rb, tri)


TOK_CH = 1024
WAIT_ROWS = 128


def _dispatch_kernel(poffs_ref, psz_ref, dest_ref, h_ref, xb_ref, zero_ref, sem, zsem):
    @pl.when(pl.program_id(0) == 0)
    def _():
        zero_ref[...] = jnp.zeros_like(zero_ref)

        def zcopy(e):
            start = pl.multiple_of(poffs_ref[e] + psz_ref[e] - MOE_BLOCK, MOE_BLOCK)
            return pltpu.make_async_copy(zero_ref, xb_ref.at[pl.ds(start, MOE_BLOCK)], zsem)

        def zissue(e, c):
            @pl.when(psz_ref[e] > 0)
            def _():
                zcopy(e).start()
            return c

        def zdrain(e, c):
            @pl.when(psz_ref[e] > 0)
            def _():
                zcopy(e).wait()
            return c

        lax.fori_loop(0, N_EXPERTS, zissue, 0)
        lax.fori_loop(0, N_EXPERTS, zdrain, 0)

        def tcopy(b):
            return pltpu.make_async_copy(zero_ref, xb_ref.at[pl.ds(pl.multiple_of(b * MOE_BLOCK, MOE_BLOCK), MOE_BLOCK)], zsem)

        first_free = (poffs_ref[N_EXPERTS - 1] + psz_ref[N_EXPERTS - 1]) // MOE_BLOCK
        n_blocks = xb_ref.shape[0] // MOE_BLOCK
        lax.fori_loop(first_free, n_blocks, lambda b, c: (tcopy(b).start(), c)[1], 0)
        lax.fori_loop(first_free, n_blocks, lambda b, c: (tcopy(b).wait(), c)[1], 0)

    def issue(g, c):
        for u in range(SUBLANES):
            for j in range(2):
                d = dest_ref[0, 0, 2 * SUBLANES * g + 2 * u + j]
                pltpu.make_async_copy(h_ref.at[g, pl.ds(u, 1)], xb_ref.at[pl.ds(d, 1)], sem).start(priority=j)
        return c

    lax.fori_loop(0, TOK_CH // SUBLANES, issue, 0)

    def drain(r, c):
        pltpu.make_async_copy(xb_ref.at[pl.ds(0, WAIT_ROWS)], xb_ref.at[pl.ds(0, WAIT_ROWS)], sem).wait()
        return c

    lax.fori_loop(0, 2 * TOK_CH // WAIT_ROWS, drain, 0)


def _dispatch(h2, dest, poffs, psizes, P):
    T, W = h2.shape
    nch = T // TOK_CH
    grid_spec = pltpu.PrefetchScalarGridSpec(
        num_scalar_prefetch=2,
        grid=(nch,),
        in_specs=[pl.BlockSpec((1, 1, 2 * TOK_CH), lambda i, po, ps: (i, 0, 0), memory_space=pltpu.SMEM),
                  pl.BlockSpec((TOK_CH // SUBLANES, SUBLANES, W), lambda i, po, ps: (i, 0, 0))],
        out_specs=pl.BlockSpec(memory_space=pl.ANY),
        scratch_shapes=[pltpu.VMEM((MOE_BLOCK, W), h2.dtype), pltpu.SemaphoreType.DMA(()), pltpu.SemaphoreType.DMA(())],
    )
    return pl.pallas_call(
        _dispatch_kernel,
        out_shape=jax.ShapeDtypeStruct((P, W), h2.dtype),
        grid_spec=grid_spec,
        compiler_params=_params(1),
        name="dispatch",
    )(poffs, psizes, dest, h2.reshape(T // SUBLANES, SUBLANES, W))


def _expert_kernel(be_ref, nu_ref, x_ref, w1_ref, w3_ref, w2_ref, y_ref, c1_ref, c3_ref, c2_ref):
    i = pl.program_id(0)
    used = i < nu_ref[0]
    fresh = (i == 0) | (be_ref[i] != be_ref[jnp.maximum(i - 1, 0)])

    @pl.when(used & fresh)
    def _():
        c1_ref[...] = w1_ref[0].astype(BF16)
        c3_ref[...] = w3_ref[0].astype(BF16)
        c2_ref[...] = w2_ref[0].astype(BF16)

    @pl.when(used)
    def _():
        xa, xb = _unpack2(x_ref[...])
        x = jnp.concatenate([xa, xb], axis=1).astype(BF16)
        a = jnp.dot(x, c1_ref[...], preferred_element_type=F32)
        g = jnp.dot(x, c3_ref[...], preferred_element_type=F32)
        hmid = (a * _sigmoid(a) * g).astype(BF16)
        y = jnp.dot(hmid, c2_ref[...], preferred_element_type=F32)
        half = y.shape[1] // 2
        y_ref[...] = _pack2(y[:, :half], y[:, half:])

    @pl.when(jnp.logical_not(used))
    def _():
        y_ref[...] = jnp.zeros_like(y_ref)


def _experts(xb, blk_e, n_used, w1, w3, w2):
    P, W = xb.shape
    E, D, DE = w1.shape
    nblk = P // MOE_BLOCK
    grid_spec = pltpu.PrefetchScalarGridSpec(
        num_scalar_prefetch=2,
        grid=(nblk,),
        in_specs=[pl.BlockSpec((MOE_BLOCK, W), lambda i, be, nu: (jnp.minimum(i, jnp.maximum(nu[0] - 1, 0)), 0)),
                  pl.BlockSpec((1, D, DE), lambda i, be, nu: (be[i], 0, 0)),
                  pl.BlockSpec((1, D, DE), lambda i, be, nu: (be[i], 0, 0)),
                  pl.BlockSpec((1, DE, D), lambda i, be, nu: (be[i], 0, 0))],
        out_specs=pl.BlockSpec((MOE_BLOCK, W), lambda i, be, nu: (i, 0)),
        scratch_shapes=[pltpu.VMEM((D, DE), BF16), pltpu.VMEM((D, DE), BF16), pltpu.VMEM((DE, D), BF16)],
    )
    return pl.pallas_call(
        _expert_kernel,
        out_shape=jax.ShapeDtypeStruct((P, W), F32),
        grid_spec=grid_spec,
        compiler_params=_params(1),
        name="experts",
    )(blk_e, n_used, xb, w1, w3, w2)


def _combine_kernel(dest_ref, wts_ref, x1_ref, mod_ref, g_ref, b_ref, yb_ref, o_ref, rows_ref, sems):
    i = pl.program_id(0)
    n = pl.num_programs(0) - 1
    slot = lax.rem(i, 2)
    prev = 1 - slot

    @pl.when(i < n)
    def _():
        def issue(g, c):
            for u in range(SUBLANES):
                for j in range(2):
                    d = dest_ref[0, 0, 2 * SUBLANES * g + 2 * u + j]
                    pltpu.make_async_copy(yb_ref.at[pl.ds(d, 1)], rows_ref.at[slot, j, g, pl.ds(u, 1)],
                                          sems.at[slot]).start(priority=j)
            return c

        lax.fori_loop(0, TOK_CH // SUBLANES, issue, 0)

    @pl.when(i > 0)
    def _():
        def drain(r, c):
            unit = rows_ref.at[prev, 0, pl.ds(0, WAIT_ROWS // SUBLANES)]
            pltpu.make_async_copy(unit, unit, sems.at[prev]).wait()
            return c

        lax.fori_loop(0, 2 * TOK_CH // WAIT_ROWS, drain, 0)
        width = yb_ref.shape[1]
        w = wts_ref[...]
        y0 = jnp.concatenate(_unpack2(rows_ref[prev, 0].reshape(TOK_CH, width)), axis=1)
        y1 = jnp.concatenate(_unpack2(rows_ref[prev, 1].reshape(TOK_CH, width)), axis=1)
        y = w[:, 0:1] * y0 + w[:, 1:2] * y1
        gate2 = mod_ref[0, 5:6, :]
        o_ref[...] = _layer_norm(DN_ALPHA * x1_ref[...] + gate2 * y, g_ref[...], b_ref[...])


def _combine(yb, dest, wts, x1, mod, ln2_g, ln2_b, S):
    T, D = x1.shape
    nch = T // TOK_CH
    per_b = S // TOK_CH
    cur = lambda i: jnp.minimum(i, nch - 1)
    done = lambda i: jnp.maximum(i - 1, 0)
    return pl.pallas_call(
        _combine_kernel,
        out_shape=jax.ShapeDtypeStruct((T, D), F32),
        grid=(nch + 1,),
        in_specs=[pl.BlockSpec((1, 1, 2 * TOK_CH), lambda i: (cur(i), 0, 0), memory_space=pltpu.SMEM),
                  pl.BlockSpec((TOK_CH, LANES), lambda i: (done(i), 0)),
                  pl.BlockSpec((TOK_CH, D), lambda i: (done(i), 0)),
                  pl.BlockSpec((1, 6, D), lambda i: (done(i) // per_b, 0, 0)),
                  pl.BlockSpec((1, D), lambda i: (0, 0)),
                  pl.BlockSpec((1, D), lambda i: (0, 0)),
                  pl.BlockSpec(memory_space=pl.ANY)],
        out_specs=pl.BlockSpec((TOK_CH, D), lambda i: (done(i), 0)),
        scratch_shapes=[pltpu.VMEM((2, 2, TOK_CH // SUBLANES, SUBLANES, yb.shape[1]), yb.dtype),
                        pltpu.SemaphoreType.DMA((2,))],
        compiler_params=_params(1),
        name="combine",
    )(dest, wts, x1, mod, ln2_g.reshape(1, D), ln2_b.reshape(1, D), yb)


SC_ROWS = 64


def _sc_gather_rows(table, idx):
    n, width = idx.shape[0], table.shape[1]
    info = plsc.get_sparse_core_info()
    nc, ns = info.num_cores, info.num_subcores
    workers = nc * ns
    per_worker = n // workers
    chunks = per_worker // SC_ROWS
    assert per_worker * workers == n and chunks * SC_ROWS == per_worker and chunks % 2 == 0
    mesh = plsc.VectorSubcoreMesh(core_axis_name="c", subcore_axis_name="s")

    def body(table_hbm, idx_hbm, out_hbm, idx_v, rows_v, sem):
        wid = lax.axis_index("s") * nc + lax.axis_index("c")
        base = wid * per_worker
        pltpu.sync_copy(idx_hbm.at[wid], idx_v)

        def gather(chunk, buf):
            return pltpu.make_async_copy(table_hbm.at[idx_v.at[chunk]], rows_v.at[buf], sem)

        gather(0, 0).start()

        @pl.loop(0, chunks, step=2)
        def _(c):
            for b in range(2):
                chunk = c + b
                gather(chunk, b).wait()

                @pl.when(chunk + 1 < chunks)
                def _():
                    gather(chunk + 1, 1 - b).start()

                pltpu.sync_copy(rows_v.at[b], out_hbm.at[pl.ds(base + chunk * SC_ROWS, SC_ROWS)])

    return pl.kernel(
        body,
        out_type=jax.ShapeDtypeStruct((n, width), table.dtype),
        mesh=mesh,
        scratch_types=[pltpu.VMEM((chunks, SC_ROWS), jnp.int32),
                       pltpu.VMEM((2, SC_ROWS, width), table.dtype),
                       pltpu.SemaphoreType.DMA],
        name="sc_gather",
    )(table, idx.reshape(workers, chunks, SC_ROWS))


def _combine_dense_kernel(r0_ref, r1_ref, wts_ref, x1_ref, mod_ref, g_ref, b_ref, o_ref):
    w = wts_ref[...]
    y0 = jnp.concatenate(_unpack2(r0_ref[0]), axis=1)
    y1 = jnp.concatenate(_unpack2(r1_ref[0]), axis=1)
    y = w[:, 0:1] * y0 + w[:, 1:2] * y1
    gate2 = mod_ref[0, 5:6, :]
    o_ref[...] = _layer_norm(DN_ALPHA * x1_ref[...] + gate2 * y, g_ref[...], b_ref[...])


def _combine_dense(rows, wts, x1, mod, ln2_g, ln2_b, S):
    T, D = x1.shape
    tm = min(512, S)
    per_b = S // tm
    return pl.pallas_call(
        _combine_dense_kernel,
        out_shape=jax.ShapeDtypeStruct((T, D), F32),
        grid=(T // tm,),
        in_specs=[pl.BlockSpec((1, tm, rows.shape[2]), lambda i: (0, i, 0)),
                  pl.BlockSpec((1, tm, rows.shape[2]), lambda i: (1, i, 0)),
                  pl.BlockSpec((tm, LANES), lambda i: (i, 0)),
                  pl.BlockSpec((tm, D), lambda i: (i, 0)),
                  pl.BlockSpec((1, 6, D), lambda i: (i // per_b, 0, 0)),
                  pl.BlockSpec((1, D), lambda i: (0, 0)),
                  pl.BlockSpec((1, D), lambda i: (0, 0))],
        out_specs=pl.BlockSpec((tm, D), lambda i: (i, 0)),
        compiler_params=_params(1),
        name="combine",
    )(rows, rows, wts, x1, mod, ln2_g.reshape(1, D), ln2_b.reshape(1, D))


def _moe(h2, x1, route, wts, counts, mod, w1, w3, w2, ln2_g, ln2_b):
    B, S, D = x1.shape
    T = B * S
    P = 2 * T + N_EXPERTS * MOE_BLOCK
    nblk = P // MOE_BLOCK
    nch = T // TOK_CH
    sizes = counts[0, ROUTE_OFF:ROUTE_OFF + N_EXPERTS].astype(jnp.int32)
    psizes = (sizes + MOE_BLOCK - 1) // MOE_BLOCK * MOE_BLOCK
    pends = jnp.cumsum(psizes)
    poffs = pends - psizes
    starts = jnp.arange(nblk, dtype=jnp.int32) * MOE_BLOCK
    blk_e = jnp.minimum(jnp.sum((pends[None, :] <= starts[:, None]).astype(jnp.int32), axis=1), N_EXPERTS - 1)
    n_used = pends[-1:] // MOE_BLOCK
    r4 = route.reshape(T, LANES)[:, :4]
    sel = r4[:, :2, None] == jnp.arange(N_EXPERTS, dtype=jnp.int32)[None, None, :]
    dest = (r4[:, 2:4] + jnp.sum(jnp.where(sel, poffs[None, None, :], 0), axis=-1)).reshape(nch, 1, 2 * TOK_CH)
    xb = _dispatch(h2.reshape(T, D // 2), dest, poffs, psizes, P)
    yb = _experts(xb, blk_e, n_used, w1, w3, w2)
    slot_major = dest.reshape(T, 2).T.reshape(2 * T)
    rows = _sc_gather_rows(yb, slot_major).reshape(2, T, yb.shape[1])
    out = _combine_dense(rows, wts.reshape(T, LANES), x1.reshape(T, D), mod, ln2_g, ln2_b, S)
    return out.reshape(B, S, D)


def _layer(x, c, w_ada, b_ada, w_in, conv_w, conv_b, fw1, fb1, ff1, fw2, fb2, ff2, fw3, decay, skip,
           w_hy_o, w_attn_o, attn_sink, w_out, ln1_g, ln1_b, rg_w, rg_b, re_w, re_b, ew1, ew3, ew2,
           ln2_g, ln2_b):
    mod = _ada(c, w_ada, b_ada)
    q, kv, hv, hx1, hx2, ga, gh = _in_proj(x, mod, w_in, conv_w, conv_b)
    attn = _attention(q, kv, attn_sink)
    hy = _hyena(hv, hx1, hx2, fw1, fb1, ff1, fw2, fb2, ff2, fw3, decay, skip)
    x1, h2, route, wts, counts = _merge(attn, hy, ga, gh, x, mod, w_attn_o, w_hy_o, w_out,
                                        ln1_g, ln1_b, rg_w, rg_b, re_w, re_b)
    return _moe(h2, x1, route, wts, counts, mod, ew1, ew3, ew2, ln2_g, ln2_b)


def kernel(x, c, w_ada, b_ada, w_in, conv_w, conv_b, filt_w1, filt_b1, filt_freq1, filt_w2, filt_b2, filt_freq2, filt_w3, filt_decay, hy_skip, w_hy_o, w_attn_o, attn_sink, w_out, ln1_g, ln1_b, router_group_w, router_group_b, router_expert_w, router_expert_b, exp_w1, exp_w3, exp_w2, ln2_g, ln2_b):
    for l in range(w_ada.shape[0]):
        x = _layer(x, c, w_ada[l], b_ada[l], w_in[l], conv_w[l], conv_b[l], filt_w1[l], filt_b1[l],
                   filt_freq1[l], filt_w2[l], filt_b2[l], filt_freq2[l], filt_w3[l], filt_decay[l],
                   hy_skip[l], w_hy_o[l], w_attn_o[l], attn_sink[l], w_out[l], ln1_g[l], ln1_b[l],
                   router_group_w[l], router_group_b[l], router_expert_w[l], router_expert_b[l],
                   exp_w1[l], exp_w3[l], exp_w2[l], ln2_g[l], ln2_b[l])
    return x
```

```python
import functools
import math

import numpy as np
import jax
import jax.numpy as jnp
from jax import lax
from jax.experimental import pallas as pl
from jax.experimental.pallas import tpu as pltpu
from jax.experimental.pallas import tpu_sc as plsc

F32 = jnp.float32
BF16 = jnp.bfloat16

N_HEADS = 8
N_KV_HEADS = 2
HEAD_DIM = 64
ATTN_WIDTH = N_HEADS * HEAD_DIM
KV_WIDTH = N_KV_HEADS * HEAD_DIM
WINDOW = 128
BLOCK_Q = 128
HYENA_WIDTH = 512
FILTER_EMB = 33
FILTER_BANDS = (FILTER_EMB - 1) // 2
WINDOW_SHIFT = 0.05
N_GROUPS = 8
EXPERTS_PER_GROUP = 8
N_EXPERTS = N_GROUPS * EXPERTS_PER_GROUP
D_EXPERT = 512
MOE_BLOCK = 512
LN_EPS = 1e-5
DEPTH = 1
DN_ALPHA = (2.0 * DEPTH) ** 0.25
NEG = -1e30

LANES = 128
SUBLANES = 8
ROUTE_OFF = N_GROUPS
VMEM_LIMIT = 56 * 1024 * 1024


def _params(n_axes, vmem=VMEM_LIMIT):
    return pltpu.CompilerParams(dimension_semantics=("arbitrary",) * n_axes, vmem_limit_bytes=vmem)


def _split(a):
    hi = a.astype(BF16)
    lo = (a - hi.astype(F32)).astype(BF16)
    return hi, lo


def _dot3(a, b_hi, b_lo):
    a_hi, a_lo = _split(a)
    acc = jnp.dot(a_hi, b_hi, preferred_element_type=F32)
    acc = acc + jnp.dot(a_hi, b_lo, preferred_element_type=F32)
    acc = acc + jnp.dot(a_lo, b_hi, preferred_element_type=F32)
    return acc


def _pack2(a, b):
    ia = lax.bitcast_convert_type(a.astype(BF16).astype(F32), jnp.int32)
    ib = lax.bitcast_convert_type(b.astype(BF16).astype(F32), jnp.int32)
    return lax.bitcast_convert_type(ia | lax.shift_right_logical(ib, 16), F32)


def _unpack2(p):
    p = lax.bitcast_convert_type(p, jnp.int32)
    a = lax.bitcast_convert_type(p & jnp.int32(-65536), F32)
    b = lax.bitcast_convert_type(lax.shift_left(p, 16), F32)
    return a, b


def _sigmoid(x):
    return 1.0 / (1.0 + jnp.exp(-x))


def _layer_norm(r, g, b):
    mu = jnp.mean(r, axis=-1, keepdims=True)
    d = r - mu
    var = jnp.mean(d * d, axis=-1, keepdims=True)
    return d * lax.rsqrt(var + LN_EPS) * g + b


def _ada_kernel(c_ref, wh_ref, wl_ref, b_ref, o_ref):
    c = c_ref[...]
    s = c * _sigmoid(c)
    o_ref[...] = _dot3(s, wh_ref[...], wl_ref[...]) + b_ref[...]


def _ada(c, w_ada, b_ada):
    B, D = c.shape
    n_out = w_ada.shape[1]
    rows = 8
    cp = jnp.zeros((rows, D), F32).at[:B].set(c)
    wh, wl = _split(w_ada)
    tn = 1024
    out = pl.pallas_call(
        _ada_kernel,
        out_shape=jax.ShapeDtypeStruct((rows, n_out), F32),
        grid=(n_out // tn,),
        in_specs=[pl.BlockSpec((rows, D), lambda j: (0, 0)),
                  pl.BlockSpec((D, tn), lambda j: (0, j)),
                  pl.BlockSpec((D, tn), lambda j: (0, j)),
                  pl.BlockSpec((1, tn), lambda j: (0, j))],
        out_specs=pl.BlockSpec((rows, tn), lambda j: (0, j)),
        compiler_params=_params(1),
        name="ada",
    )(cp, wh, wl, b_ada.reshape(1, n_out))
    return out[:B].reshape(B, 6, D)


def _inproj_kernel(x_ref, xp_ref, xn_ref, mod_ref, w_ref, cw_ref, cb_ref,
                   q_ref, kv_ref, v_ref, x1_ref, x2_ref, ga_ref, gh_ref):
    i = pl.program_id(1)
    n = pl.num_programs(1)
    C = HYENA_WIDTH
    x = x_ref[0]
    tm, D = x.shape
    shift = mod_ref[0, 0:1, :]
    scale = mod_ref[0, 1:2, :]
    h = (x * (1.0 + scale) + shift).astype(BF16)

    def seg(lo, hi):
        return jnp.dot(h, w_ref[:, lo:hi], preferred_element_type=F32)

    o = 0
    q_ref[0] = (seg(o, o + ATTN_WIDTH) * (HEAD_DIM ** -0.5)).astype(BF16)
    o += ATTN_WIDTH
    kv_ref[0] = seg(o, o + 2 * KV_WIDTH).astype(BF16)
    o += 2 * KV_WIDTH

    u = seg(o, o + 3 * C)
    xe = jnp.concatenate([xp_ref[0], xn_ref[0]], axis=0)
    he = (xe * (1.0 + scale) + shift).astype(BF16)
    ue = jnp.dot(he, w_ref[:, o:o + 3 * C], preferred_element_type=F32)
    prow = jnp.where(i > 0, ue[SUBLANES - 1:SUBLANES], 0.0)
    nrow = jnp.where(i < n - 1, ue[SUBLANES:SUBLANES + 1], 0.0)
    rid = lax.broadcasted_iota(jnp.int32, (tm, 1), 0)
    up = jnp.where(rid == 0, prow, pltpu.roll(u, 1, 0))
    dn = jnp.where(rid == tm - 1, nrow, pltpu.roll(u, tm - 1, 0))
    conv = cw_ref[0:1, :] * up + cw_ref[1:2, :] * u + cw_ref[2:3, :] * dn + cb_ref[...]
    v_ref[0] = conv[:, :C]
    x1_ref[0] = conv[:, C:2 * C]
    x2_ref[0] = conv[:, 2 * C:]
    o += 3 * C

    ga_ref[0] = _sigmoid(seg(o, o + D)).astype(BF16)
    o += D
    gh_ref[0] = _sigmoid(seg(o, o + D)).astype(BF16)


def _in_proj(x, mod, w_in, conv_w, conv_b):
    B, S, D = x.shape
    C = HYENA_WIDTH
    tm = min(512, S)
    r8 = tm // SUBLANES
    nb8 = S // SUBLANES
    wb = w_in.astype(BF16)
    nw = wb.shape[1]
    row = lambda b, i: (b, i, 0)
    shapes = [(ATTN_WIDTH, BF16), (2 * KV_WIDTH, BF16), (C, F32), (C, F32), (C, F32), (D, BF16), (D, BF16)]
    return pl.pallas_call(
        _inproj_kernel,
        out_shape=[jax.ShapeDtypeStruct((B, S, w), dt) for w, dt in shapes],
        grid=(B, S // tm),
        in_specs=[pl.BlockSpec((1, tm, D), row),
                  pl.BlockSpec((1, SUBLANES, D), lambda b, i: (b, jnp.maximum(i * r8 - 1, 0), 0)),
                  pl.BlockSpec((1, SUBLANES, D), lambda b, i: (b, jnp.minimum((i + 1) * r8, nb8 - 1), 0)),
                  pl.BlockSpec((1, 6, D), lambda b, i: (b, 0, 0)),
                  pl.BlockSpec((D, nw), lambda b, i: (0, 0)),
                  pl.BlockSpec((3, 3 * C), lambda b, i: (0, 0)),
                  pl.BlockSpec((1, 3 * C), lambda b, i: (0, 0))],
        out_specs=[pl.BlockSpec((1, tm, w), row) for w, _ in shapes],
        compiler_params=_params(2),
        name="in_proj",
    )(x, x, x, mod, wb, conv_w.astype(F32), conv_b.reshape(1, 3 * C).astype(F32))


ATT_TQ = 512


def _attn_kernel(sink_ref, q_ref, kvp_ref, kvc_ref, kvn_ref, bias_ref, o_ref, kv_scr, *, seq_len):
    i = pl.program_id(1)
    Q = BLOCK_Q
    TQ = q_ref.shape[1]
    G = N_HEADS // N_KV_HEADS
    kv_scr[0:Q] = kvp_ref[0]
    kv_scr[Q:Q + TQ] = kvc_ref[0]
    kv_scr[Q + TQ:] = kvn_ref[0]
    col = lax.broadcasted_iota(jnp.int32, (1, 3 * Q), 1)
    rhead = lax.broadcasted_iota(jnp.int32, (G * Q, 1), 0) // Q
    for j in range(TQ // Q):
        kpos = i * TQ + (j - 1) * Q + col
        colbias = jnp.where((kpos >= 0) & (kpos < seq_len), 0.0, NEG)
        for kv in range(N_KV_HEADS):
            kk = kv_scr[j * Q:(j + 3) * Q, kv * HEAD_DIM:(kv + 1) * HEAD_DIM]
            vv = kv_scr[j * Q:(j + 3) * Q, KV_WIDTH + kv * HEAD_DIM:KV_WIDTH + (kv + 1) * HEAD_DIM]
            heads = [kv * G + g for g in range(G)]
            qg = jnp.concatenate([q_ref[0, j * Q:(j + 1) * Q, h * HEAD_DIM:(h + 1) * HEAD_DIM] for h in heads], axis=0)
            s = lax.dot_general(qg, kk, (((1,), (1,)), ((), ())), preferred_element_type=F32)
            s = s + bias_ref[kv] + colbias
            snk = jnp.where(rhead == 0, sink_ref[heads[0]],
                            jnp.where(rhead == 1, sink_ref[heads[1]],
                                      jnp.where(rhead == 2, sink_ref[heads[2]], sink_ref[heads[3]])))
            m = jnp.maximum(jnp.max(s, axis=1, keepdims=True), snk)
            p = jnp.exp(s - m)
            den = jnp.sum(p, axis=1, keepdims=True) + jnp.exp(snk - m)
            o = jnp.dot(p.astype(BF16), vv, preferred_element_type=F32) / den
            for g, h in enumerate(heads):
                o_ref[0, j * Q:(j + 1) * Q, h * HEAD_DIM:(h + 1) * HEAD_DIM] = o[g * Q:(g + 1) * Q].astype(BF16)


def _attention(q, kv, sink):
    B, S, _ = q.shape
    Q = BLOCK_Q
    TQ = min(ATT_TQ, S)
    r = TQ // Q
    nq = S // Q
    G = N_HEADS // N_KV_HEADS
    assert G == 4
    a = jnp.arange(Q)[:, None]
    j = jnp.arange(3 * Q)[None, :]
    rel = jnp.abs(j - Q - a).astype(F32)
    slopes = 2.0 ** (-8.0 * jnp.arange(1, N_HEADS + 1, dtype=F32) / N_HEADS)
    bias = jnp.where(rel[None] <= WINDOW, -slopes[:, None, None] * rel[None], NEG).astype(F32)
    bias = bias.reshape(N_KV_HEADS, G * Q, 3 * Q)
    cur = lambda b, i: (b, i, 0)
    return pl.pallas_call(
        functools.partial(_attn_kernel, seq_len=S),
        out_shape=jax.ShapeDtypeStruct((B, S, ATTN_WIDTH), BF16),
        grid=(B, S // TQ),
        in_specs=[pl.BlockSpec(memory_space=pltpu.SMEM),
                  pl.BlockSpec((1, TQ, ATTN_WIDTH), cur),
                  pl.BlockSpec((1, Q, 2 * KV_WIDTH), lambda b, i: (b, jnp.maximum(i * r - 1, 0), 0)),
                  pl.BlockSpec((1, TQ, 2 * KV_WIDTH), cur),
                  pl.BlockSpec((1, Q, 2 * KV_WIDTH), lambda b, i: (b, jnp.minimum((i + 1) * r, nq - 1), 0)),
                  pl.BlockSpec((N_KV_HEADS, G * Q, 3 * Q), lambda b, i: (0, 0, 0))],
        out_specs=pl.BlockSpec((1, TQ, ATTN_WIDTH), cur),
        scratch_shapes=[pltpu.VMEM((TQ + 2 * Q, 2 * KV_WIDTH), BF16)],
        compiler_params=_params(2),
        name="attn",
    )(sink.astype(F32), q, kv, kv, kv, bias)


def _filter_kernel(z_ref, w1h, w1l, b1_ref, f1_ref, w2h, w2l, b2_ref, f2_ref, w3h, w3l, dec_ref,
                   k_ref, s_ref):
    i = pl.program_id(0)
    z = z_ref[...]
    h1 = jnp.sin(f1_ref[...] * (_dot3(z, w1h[...], w1l[...]) + b1_ref[...]))
    h2 = jnp.sin(f2_ref[...] * (_dot3(h1, w2h[...], w2l[...]) + b2_ref[...]))
    k = _dot3(h2, w3h[...], w3l[...])
    t = z[:, 0:1]
    k = k * (jnp.exp(-t * jnp.abs(dec_ref[...])) + WINDOW_SHIFT)
    k_ref[...] = k

    @pl.when(i == 0)
    def _():
        s_ref[...] = jnp.zeros_like(s_ref)

    s_ref[...] += jnp.sum(jnp.abs(k), axis=0, keepdims=True)


def _filter_embedding(L):
    t = np.linspace(0.0, 1.0, L, dtype=np.float32).astype(np.float64)[:, None]
    w = (2.0 * math.pi * np.arange(L, dtype=np.float32) / np.float32(L)).astype(np.float64)[:, None]
    bands = np.linspace(1e-4, FILTER_BANDS - 1, FILTER_BANDS, dtype=np.float32).astype(np.float64)[None, :]
    bw = (bands.astype(np.float32) * w.astype(np.float32)).astype(np.float64)
    z = np.concatenate([t, np.cos(bw), -np.sin(bw)], axis=-1)
    zp = np.zeros((L, LANES), np.float32)
    zp[:, :FILTER_EMB] = z.astype(np.float32)
    return jnp.asarray(zp)


def _pad2(a, r, c):
    return jnp.zeros((r, c), F32).at[:a.shape[0], :a.shape[1]].set(a.astype(F32))


def _filters(L, fw1, fb1, ff1, fw2, fb2, ff2, fw3, decay):
    H = LANES
    nf = fw3.shape[1]
    z = _filter_embedding(L)
    w1h, w1l = _split(_pad2(fw1, H, H))
    w2h, w2l = _split(_pad2(fw2, H, H))
    w3h, w3l = _split(_pad2(fw3, H, nf))
    b1 = _pad2(fb1[None], 1, H)
    f1 = _pad2(ff1[None], 1, H)
    b2 = _pad2(fb2[None], 1, H)
    f2 = _pad2(ff2[None], 1, H)
    tr = min(512, L)
    full = lambda r, c: pl.BlockSpec((r, c), lambda i: (0, 0))
    return pl.pallas_call(
        _filter_kernel,
        out_shape=[jax.ShapeDtypeStruct((L, nf), F32), jax.ShapeDtypeStruct((1, nf), F32)],
        grid=(L // tr,),
        in_specs=[pl.BlockSpec((tr, H), lambda i: (i, 0)),
                  full(H, H), full(H, H), full(1, H), full(1, H),
                  full(H, H), full(H, H), full(1, H), full(1, H),
                  full(H, nf), full(H, nf), full(1, nf)],
        out_specs=[pl.BlockSpec((tr, nf), lambda i: (i, 0)), full(1, nf)],
        compiler_params=_params(1),
        name="filter",
    )(z, w1h, w1l, b1, f1, w2h, w2l, b2, f2, w3h, w3l, decay.reshape(1, nf).astype(F32))


def _np_bf16(m64):
    return jnp.asarray(m64.astype(np.float32).astype(BF16))


def _dft_constants(L):
    N = 2 * L
    n2 = LANES
    n1 = N // n2
    h1 = n1 // 2
    k1 = np.arange(n1)[:, None]
    s1 = np.arange(h1)[None, :]
    ang = -2.0 * np.pi * ((k1 * s1) % n1) / n1
    wr, wi = np.cos(ang), np.sin(ang)
    w1_filt = np.block([[wr, wr], [wi, wi], [wr, -wr], [wi, -wi]])
    w1_cplx = np.block([[wr, -wi], [wi, wr]])
    vr, vi = wr.T / N, -wi.T / N
    w3 = np.block([[vr, -vi], [vi, vr]])
    k2 = np.arange(n2)[:, None]
    s2 = np.arange(n2)[None, :]
    a2 = -2.0 * np.pi * ((k2 * s2) % n2) / n2
    w2r, w2i = jnp.asarray(np.cos(a2), F32), jnp.asarray(np.sin(a2), F32)
    at = -2.0 * np.pi * ((np.arange(n1)[:, None] * s2) % N) / N
    twr, twi = jnp.asarray(np.cos(at), F32), jnp.asarray(np.sin(at), F32)
    mr = w2r[None] * twr[:, None, :] - w2i[None] * twi[:, None, :]
    mi = w2r[None] * twi[:, None, :] + w2i[None] * twr[:, None, :]
    fwd = jnp.concatenate([jnp.concatenate([mr, -mi], axis=2),
                           jnp.concatenate([mi, mr], axis=2)], axis=1)
    fwd = fwd.astype(BF16)
    return dict(n1=n1, w1_filt=_np_bf16(w1_filt), w1_cplx=_np_bf16(w1_cplx), w3=_np_bf16(w3),
                fwd=fwd, inv=jnp.swapaxes(fwd, 1, 2))


SCH = 8


def _dft1_kernel(x_ref, w_ref, a_ref, *, n1):
    w = w_ref[...]
    for j in range(SCH):
        rhs = jnp.concatenate([x_ref[0, 0, :, j, :], x_ref[0, 1, :, j, :]], axis=0)
        res = jnp.dot(w, rhs.astype(BF16), preferred_element_type=F32)
        a_ref[0, :, j, :] = _pack2(res[:n1], res[n1:])


def _dft1_data(x, consts):
    B, L, C = x.shape
    n1 = consts["n1"]
    h1 = n1 // 2
    xv = x.reshape(B // 2, 2, h1, LANES, C)
    return pl.pallas_call(
        functools.partial(_dft1_kernel, n1=n1),
        out_shape=jax.ShapeDtypeStruct((B // 2, n1, LANES, C), F32),
        grid=(B // 2, LANES // SCH),
        in_specs=[pl.BlockSpec((1, 2, h1, SCH, C), lambda p, j: (p, 0, 0, j, 0)),
                  pl.BlockSpec((2 * n1, n1), lambda p, j: (0, 0))],
        out_specs=pl.BlockSpec((1, n1, SCH, C), lambda p, j: (p, 0, j, 0)),
        compiler_params=_params(2),
        name="dft1",
    )(xv, consts["w1_cplx"])


def _dft1f_kernel(x_ref, w_ref, a_ref, *, n1):
    C = HYENA_WIDTH
    w = w_ref[...]
    for j in range(SCH):
        rhs = jnp.concatenate([x_ref[:, j, :C], x_ref[:, j, C:]], axis=0)
        res = jnp.dot(w, rhs.astype(BF16), preferred_element_type=F32)
        a_ref[0, :, 0, j, :] = _pack2(res[:n1], res[n1:2 * n1])
        a_ref[0, :, 1, j, :] = _pack2(res[2 * n1:3 * n1], res[3 * n1:])


def _dft1_filter(kraw, consts):
    L, nf = kraw.shape
    C = HYENA_WIDTH
    n_ord = nf // (2 * C)
    n1 = consts["n1"]
    h1 = n1 // 2
    kv = kraw.reshape(h1, LANES, nf)
    return pl.pallas_call(
        functools.partial(_dft1f_kernel, n1=n1),
        out_shape=jax.ShapeDtypeStruct((n_ord, n1, 2, LANES, C), F32),
        grid=(n_ord, LANES // SCH),
        in_specs=[pl.BlockSpec((h1, SCH, 2 * C), lambda o, j: (0, j, o)),
                  pl.BlockSpec((4 * n1, n1), lambda o, j: (0, 0))],
        out_specs=pl.BlockSpec((1, n1, 2, SCH, C), lambda o, j: (o, 0, 0, j, 0)),
        compiler_params=_params(2),
        name="dft1f",
    )(kv, consts["w1_filt"])


KCH = 8


def _midf_kernel(a_ref, f_ref, inv_ref, b0_ref, h_ref):
    n2 = LANES
    sc = inv_ref[0]
    for k in range(KCH):
        p = jnp.concatenate(_unpack2(a_ref[0, k, :n2, :]), axis=0).astype(BF16)
        q = jnp.concatenate(_unpack2(a_ref[0, k, n2:, :]), axis=0).astype(BF16)
        h_re = jnp.dot(f_ref[k, :n2, :], p, preferred_element_type=F32)
        h_im = jnp.dot(f_ref[k, n2:, :], q, preferred_element_type=F32)
        h_ref[0, k] = _pack2((h_re - b0_ref[0]) * sc, h_im * sc)


def _filter_spectrum(af, inv_den, bwd0, consts):
    n_ord, n1, _, n2, C = af.shape
    a = af.reshape(n_ord, n1, 2 * n2, C)
    tab = pl.BlockSpec((KCH, 2 * n2, 2 * n2), lambda k, o: (k, 0, 0))
    vec = pl.BlockSpec((1, 1, C), lambda k, o: (o, 0, 0))
    return pl.pallas_call(
        _midf_kernel,
        out_shape=jax.ShapeDtypeStruct((n_ord, n1, n2, C), F32),
        grid=(n1 // KCH, n_ord),
        in_specs=[pl.BlockSpec((1, KCH, 2 * n2, C), lambda k, o: (o, k, 0, 0)), tab, vec, vec],
        out_specs=pl.BlockSpec((1, KCH, n2, C), lambda k, o: (o, k, 0, 0)),
        compiler_params=_params(2),
        name="midf",
    )(a, consts["fwd"], inv_den, bwd0)


def _mid_kernel(a_ref, f_ref, i_ref, h_ref, b_ref):
    n2 = LANES
    for k in range(KCH):
        a = jnp.concatenate(_unpack2(a_ref[0, k]), axis=0).astype(BF16)
        x = jnp.dot(f_ref[k], a, preferred_element_type=F32)
        xr, xi = x[:n2], x[n2:]
        hr, hi = _unpack2(h_ref[0, k])
        y = jnp.concatenate([xr * hr - xi * hi, xr * hi + xi * hr], axis=0)
        b = jnp.dot(i_ref[k], y.astype(BF16), preferred_element_type=F32)
        b_ref[0, k] = _pack2(b[:n2], b[n2:])


def _mid(a, hspec, order, consts):
    P, n1, n2, C = a.shape
    tab = pl.BlockSpec((KCH, 2 * n2, 2 * n2), lambda k, p: (k, 0, 0))
    return pl.pallas_call(
        _mid_kernel,
        out_shape=jax.ShapeDtypeStruct((P, n1, n2, C), F32),
        grid=(n1 // KCH, P),
        in_specs=[pl.BlockSpec((1, KCH, n2, C), lambda k, p: (p, k, 0, 0)),
                  tab, tab,
                  pl.BlockSpec((1, KCH, n2, C), lambda k, p: (order, k, 0, 0))],
        out_specs=pl.BlockSpec((1, KCH, n2, C), lambda k, p: (p, k, 0, 0)),
        compiler_params=_params(2),
        name="mid",
    )(a, consts["fwd"], consts["inv"], hspec)


def _dft3_kernel(b_ref, w_ref, v_ref, g_ref, skip_ref, z_ref, slab_ref, *, h1):
    w = w_ref[...]
    skip = skip_ref[0]
    for j in range(SCH):
        slab_ref[...] = b_ref[0, :, j, :]
        rhs = jnp.concatenate(_unpack2(slab_ref[...]), axis=0)
        y = jnp.dot(w, rhs.astype(BF16), preferred_element_type=F32)
        for r in range(2):
            yr = y[r * h1:(r + 1) * h1]
            z_ref[0, r, :, j, :] = g_ref[0, r, :, j, :] * (yr + v_ref[0, r, :, j, :] * skip)


def _dft3_gate(b5, v, gate, skip, consts):
    P, n1, n2, C = b5.shape
    h1 = n1 // 2
    B, L, _ = v.shape
    five = lambda t: t.reshape(P, 2, h1, n2, C)
    dat = pl.BlockSpec((1, 2, h1, SCH, C), lambda p, j: (p, 0, 0, j, 0))
    out = pl.pallas_call(
        functools.partial(_dft3_kernel, h1=h1),
        out_shape=jax.ShapeDtypeStruct((P, 2, h1, n2, C), F32),
        grid=(P, n2 // SCH),
        in_specs=[pl.BlockSpec((1, n1, SCH, C), lambda p, j: (p, 0, j, 0)),
                  pl.BlockSpec((n1, 2 * n1), lambda p, j: (0, 0)),
                  dat, dat,
                  pl.BlockSpec((1, C), lambda p, j: (0, 0))],
        out_specs=dat,
        scratch_shapes=[pltpu.VMEM((n1, C), F32)],
        compiler_params=_params(2),
        name="dft3",
    )(b5, consts["w3"], five(v), five(gate), skip.reshape(1, C).astype(F32))
    return out.reshape(B, L, C)


def _hyena(v, x1, x2, fw1, fb1, ff1, fw2, fb2, ff2, fw3, decay, skip):
    B, L, C = v.shape
    consts = _dft_constants(L)
    kraw, ksum = _filters(L, fw1, fb1, ff1, fw2, fb2, ff2, fw3, decay)
    ks = ksum.reshape(2, 2, C)
    inv_den = (1.0 / (ks[:, 0] + ks[:, 1])).reshape(2, 1, C)
    bwd0 = kraw[0].reshape(2, 2, C)[:, 1].reshape(2, 1, C)
    hspec = _filter_spectrum(_dft1_filter(kraw, consts), inv_den, bwd0, consts)
    z = v
    for o, gate in enumerate((x1, x2)):
        a5 = _dft1_data(z, consts)
        b5 = _mid(a5, hspec, o, consts)
        z = _dft3_gate(b5, z, gate, skip[o], consts)
    return z


def _merge_kernel(attn_ref, hy_ref, ga_ref, gh_ref, x_ref, mod_ref, wa_ref, wh_ref, wo_ref,
                  g1_ref, b1_ref, rwh_ref, rwl_ref, rb_ref, tri_ref,
                  x1_ref, h2_ref, route_ref, wts_ref, cnt_ref, carry_ref):
    first = (pl.program_id(0) == 0) & (pl.program_id(1) == 0)

    @pl.when(first)
    def _():
        carry_ref[...] = jnp.zeros_like(carry_ref)

    a = jnp.dot(attn_ref[0], wa_ref[...], preferred_element_type=F32)
    hy = jnp.dot(hy_ref[0].astype(BF16), wh_ref[...], preferred_element_type=F32)
    merged = ga_ref[0].astype(F32) * a + gh_ref[0].astype(F32) * hy
    y = jnp.dot(merged.astype(BF16), wo_ref[...], preferred_element_type=F32)
    gate1 = mod_ref[0, 2:3, :]
    shift2 = mod_ref[0, 3:4, :]
    scale2 = mod_ref[0, 4:5, :]
    x1 = _layer_norm(DN_ALPHA * x_ref[0] + gate1 * y, g1_ref[...], b1_ref[...])
    x1_ref[0] = x1
    h2 = x1 * (1.0 + scale2) + shift2
    half = h2.shape[1] // 2
    h2_ref[0] = _pack2(h2[:, :half], h2[:, half:])

    logits = _dot3(h2, rwh_ref[...], rwl_ref[...]) + rb_ref[...]
    tm = logits.shape[0]
    lane = lax.broadcasted_iota(jnp.int32, (tm, LANES), 1)
    lanef = lane.astype(F32)
    big = float(LANES)

    def first_lane(mask):
        return jnp.min(jnp.where(mask, lanef, big), axis=1, keepdims=True).astype(jnp.int32)

    gmask = lane < N_GROUPS
    gl = jnp.where(gmask, logits, NEG)
    gmax = jnp.max(gl, axis=1, keepdims=True)
    gidx = first_lane(gl == gmax)
    pg = 1.0 / jnp.sum(jnp.exp(gl - gmax), axis=1, keepdims=True)
    lo = ROUTE_OFF + gidx * EXPERTS_PER_GROUP
    emask = (lane >= lo) & (lane < lo + EXPERTS_PER_GROUP)
    el = jnp.where(emask, logits, NEG)
    v1 = jnp.max(el, axis=1, keepdims=True)
    i1 = first_lane(el == v1)
    el2 = jnp.where(emask & (lane != i1), logits, NEG)
    v2 = jnp.max(el2, axis=1, keepdims=True)
    i2 = first_lane(el2 == v2)
    e21 = jnp.exp(v2 - v1)
    w1 = pg / (1.0 + e21)
    w2 = pg * e21 / (1.0 + e21)

    sel1 = lane == i1
    sel2 = lane == i2
    onehot = jnp.where(sel1 | sel2, 1.0, 0.0)
    prefix = jnp.dot(tri_ref[...], onehot.astype(BF16), preferred_element_type=F32) + carry_ref[...]
    r1 = jnp.sum(jnp.where(sel1, prefix, 0.0), axis=1, keepdims=True)
    r2 = jnp.sum(jnp.where(sel2, prefix, 0.0), axis=1, keepdims=True)
    carry_ref[...] += jnp.sum(onehot, axis=0, keepdims=True)
    cnt_ref[...] = carry_ref[...]

    ranks = jnp.where(lane == 2, r1, jnp.where(lane == 3, r2, 0.0)).astype(jnp.int32)
    route_ref[0] = jnp.where(lane == 0, i1 - ROUTE_OFF, jnp.where(lane == 1, i2 - ROUTE_OFF, ranks))
    wts_ref[0] = jnp.where(lane == 0, w1, jnp.where(lane == 1, w2, 0.0))


def _merge(attn, hy, ga, gh, x, mod, w_attn_o, w_hy_o, w_out, ln1_g, ln1_b, rg_w, rg_b, re_w, re_b):
    B, S, D = x.shape
    tm = min(512, S)
    rw = jnp.zeros((D, LANES), F32).at[:, :N_GROUPS].set(rg_w).at[:, ROUTE_OFF:ROUTE_OFF + N_EXPERTS].set(re_w)
    rb = jnp.zeros((1, LANES), F32).at[0, :N_GROUPS].set(rg_b).at[0, ROUTE_OFF:ROUTE_OFF + N_EXPERTS].set(re_b)
    rwh, rwl = _split(rw)
    tri = (jnp.arange(tm)[:, None] > jnp.arange(tm)[None, :]).astype(BF16)
    row = lambda b, i: (b, i, 0)
    full = lambda r, c: pl.BlockSpec((r, c), lambda b, i: (0, 0))
    outs = [jax.ShapeDtypeStruct((B, S, D), F32), jax.ShapeDtypeStruct((B, S, D // 2), F32),
            jax.ShapeDtypeStruct((B, S, LANES), jnp.int32), jax.ShapeDtypeStruct((B, S, LANES), F32),
            jax.ShapeDtypeStruct((1, LANES), F32)]
    return pl.pallas_call(
        _merge_kernel,
        out_shape=outs,
        grid=(B, S // tm),
        in_specs=[pl.BlockSpec((1, tm, ATTN_WIDTH), row), pl.BlockSpec((1, tm, HYENA_WIDTH), row),
                  pl.BlockSpec((1, tm, D), row), pl.BlockSpec((1, tm, D), row), pl.BlockSpec((1, tm, D), row),
                  pl.BlockSpec((1, 6, D), lambda b, i: (b, 0, 0)),
                  full(ATTN_WIDTH, D), full(HYENA_WIDTH, D), full(D, D),
                  full(1, D), full(1, D), full(D, LANES), full(D, LANES), full(1, LANES), full(tm, tm)],
        out_specs=[pl.BlockSpec((1, tm, D), row), pl.BlockSpec((1, tm, D // 2), row),
                   pl.BlockSpec((1, tm, LANES), row), pl.BlockSpec((1, tm, LANES), row), full(1, LANES)],
        scratch_shapes=[pltpu.VMEM((1, LANES), F32)],
        compiler_params=_params(2),
        name="merge",
    )(attn, hy, ga, gh, x, mod, w_attn_o.astype(BF16), w_hy_o.astype(BF16), w_out.astype(BF16),
      ln1_g.reshape(1, D), ln1_b.reshape(1, D), rwh, rwl, rb, tri)


SC_ROWS = 64


def _sc_workers():
    info = plsc.get_sparse_core_info()
    return info.num_cores, info.num_cores * info.num_subcores


def _sc_split(n):
    _, workers = _sc_workers()
    per_worker = n // workers
    chunks = per_worker // SC_ROWS
    assert per_worker * workers == n and chunks * SC_ROWS == per_worker and chunks % 2 == 0
    return workers, per_worker, chunks


def _sc_scatter_rows(src, idx0, idx1, n_out):
    n, width = src.shape
    nc, _ = _sc_workers()
    workers, per_worker, chunks = _sc_split(n)
    mesh = plsc.VectorSubcoreMesh(core_axis_name="c", subcore_axis_name="s")

    def body(src_hbm, i0_hbm, i1_hbm, out_hbm, i0_v, i1_v, rows_v, sem):
        wid = lax.axis_index("s") * nc + lax.axis_index("c")
        base = wid * per_worker
        pltpu.sync_copy(i0_hbm.at[wid], i0_v)
        pltpu.sync_copy(i1_hbm.at[wid], i1_v)

        def load(chunk, buf):
            return pltpu.make_async_copy(src_hbm.at[pl.ds(base + chunk * SC_ROWS, SC_ROWS)], rows_v.at[buf], sem)

        load(0, 0).start()

        @pl.loop(0, chunks, step=2)
        def _(c):
            for b in range(2):
                chunk = c + b
                load(chunk, b).wait()

                @pl.when(chunk + 1 < chunks)
                def _():
                    load(chunk + 1, 1 - b).start()

                pltpu.sync_copy(rows_v.at[b], out_hbm.at[i0_v.at[chunk]])
                pltpu.sync_copy(rows_v.at[b], out_hbm.at[i1_v.at[chunk]])

    shaped = lambda i: i.reshape(workers, chunks, SC_ROWS)
    return pl.kernel(
        body,
        out_type=jax.ShapeDtypeStruct((n_out, width), src.dtype),
        mesh=mesh,
        scratch_types=[pltpu.VMEM((chunks, SC_ROWS), jnp.int32),
                       pltpu.VMEM((chunks, SC_ROWS), jnp.int32),
                       pltpu.VMEM((2, SC_ROWS, width), src.dtype),
                       pltpu.SemaphoreType.DMA],
        name="sc_scatter",
    )(src, shaped(idx0), shaped(idx1))


def _expert_kernel(be_ref, nv_ref, x_ref, w1_ref, w3_ref, w2_ref, y_ref, c1_ref, c3_ref, c2_ref):
    i = pl.program_id(0)
    n_valid = nv_ref[i]
    used = n_valid > 0
    fresh = (i == 0) | (be_ref[i] != be_ref[jnp.maximum(i - 1, 0)])

    @pl.when(used & fresh)
    def _():
        c1_ref[...] = w1_ref[0].astype(BF16)
        c3_ref[...] = w3_ref[0].astype(BF16)
        c2_ref[...] = w2_ref[0].astype(BF16)

    @pl.when(used)
    def _():
        rid = lax.broadcasted_iota(jnp.int32, (x_ref.shape[0], 1), 0)
        xa, xb = _unpack2(jnp.where(rid < n_valid, x_ref[...], 0.0))
        x = jnp.concatenate([xa, xb], axis=1).astype(BF16)
        a = jnp.dot(x, c1_ref[...], preferred_element_type=F32)
        g = jnp.dot(x, c3_ref[...], preferred_element_type=F32)
        hmid = (a * _sigmoid(a) * g).astype(BF16)
        y = jnp.dot(hmid, c2_ref[...], preferred_element_type=F32)
        half = y.shape[1] // 2
        y_ref[...] = _pack2(y[:, :half], y[:, half:])

    @pl.when(jnp.logical_not(used))
    def _():
        y_ref[...] = jnp.zeros_like(y_ref)


def _experts(xb, blk_e, n_valid, w1, w3, w2):
    P, W = xb.shape
    E, D, DE = w1.shape
    nblk = P // MOE_BLOCK
    grid_spec = pltpu.PrefetchScalarGridSpec(
        num_scalar_prefetch=2,
        grid=(nblk,),
        in_specs=[pl.BlockSpec((MOE_BLOCK, W), lambda i, be, nv: (i, 0)),
                  pl.BlockSpec((1, D, DE), lambda i, be, nv: (be[i], 0, 0)),
                  pl.BlockSpec((1, D, DE), lambda i, be, nv: (be[i], 0, 0)),
                  pl.BlockSpec((1, DE, D), lambda i, be, nv: (be[i], 0, 0))],
        out_specs=pl.BlockSpec((MOE_BLOCK, W), lambda i, be, nv: (i, 0)),
        scratch_shapes=[pltpu.VMEM((D, DE), BF16), pltpu.VMEM((D, DE), BF16), pltpu.VMEM((DE, D), BF16)],
    )
    return pl.pallas_call(
        _expert_kernel,
        out_shape=jax.ShapeDtypeStruct((P, W), F32),
        grid_spec=grid_spec,
        compiler_params=_params(1),
        name="experts",
    )(blk_e, n_valid, xb, w1, w3, w2)


def _sc_gather_rows(table, idx):
    n, width = idx.shape[0], table.shape[1]
    nc, _ = _sc_workers()
    workers, per_worker, chunks = _sc_split(n)
    mesh = plsc.VectorSubcoreMesh(core_axis_name="c", subcore_axis_name="s")

    def body(table_hbm, idx_hbm, out_hbm, idx_v, rows_v, sem):
        wid = lax.axis_index("s") * nc + lax.axis_index("c")
        base = wid * per_worker
        pltpu.sync_copy(idx_hbm.at[wid], idx_v)

        def gather(chunk, buf):
            return pltpu.make_async_copy(table_hbm.at[idx_v.at[chunk]], rows_v.at[buf], sem)

        gather(0, 0).start()

        @pl.loop(0, chunks, step=2)
        def _(c):
            for b in range(2):
                chunk = c + b
                gather(chunk, b).wait()

                @pl.when(chunk + 1 < chunks)
                def _():
                    gather(chunk + 1, 1 - b).start()

                pltpu.sync_copy(rows_v.at[b], out_hbm.at[pl.ds(base + chunk * SC_ROWS, SC_ROWS)])

    return pl.kernel(
        body,
        out_type=jax.ShapeDtypeStruct((n, width), table.dtype),
        mesh=mesh,
        scratch_types=[pltpu.VMEM((chunks, SC_ROWS), jnp.int32),
                       pltpu.VMEM((2, SC_ROWS, width), table.dtype),
                       pltpu.SemaphoreType.DMA],
        name="sc_gather",
    )(table, idx.reshape(workers, chunks, SC_ROWS))


def _combine_dense_kernel(r0_ref, r1_ref, wts_ref, x1_ref, mod_ref, g_ref, b_ref, o_ref):
    w = wts_ref[...]
    y0 = jnp.concatenate(_unpack2(r0_ref[0]), axis=1)
    y1 = jnp.concatenate(_unpack2(r1_ref[0]), axis=1)
    y = w[:, 0:1] * y0 + w[:, 1:2] * y1
    gate2 = mod_ref[0, 5:6, :]
    o_ref[...] = _layer_norm(DN_ALPHA * x1_ref[...] + gate2 * y, g_ref[...], b_ref[...])


def _combine_dense(rows, wts, x1, mod, ln2_g, ln2_b, S):
    T, D = x1.shape
    tm = min(512, S)
    per_b = S // tm
    return pl.pallas_call(
        _combine_dense_kernel,
        out_shape=jax.ShapeDtypeStruct((T, D), F32),
        grid=(T // tm,),
        in_specs=[pl.BlockSpec((1, tm, rows.shape[2]), lambda i: (0, i, 0)),
                  pl.BlockSpec((1, tm, rows.shape[2]), lambda i: (1, i, 0)),
                  pl.BlockSpec((tm, LANES), lambda i: (i, 0)),
                  pl.BlockSpec((tm, D), lambda i: (i, 0)),
                  pl.BlockSpec((1, 6, D), lambda i: (i // per_b, 0, 0)),
                  pl.BlockSpec((1, D), lambda i: (0, 0)),
                  pl.BlockSpec((1, D), lambda i: (0, 0))],
        out_specs=pl.BlockSpec((tm, D), lambda i: (i, 0)),
        compiler_params=_params(1),
        name="combine",
    )(rows, rows, wts, x1, mod, ln2_g.reshape(1, D), ln2_b.reshape(1, D))


def _moe(h2, x1, route, wts, counts, mod, w1, w3, w2, ln2_g, ln2_b):
    B, S, D = x1.shape
    T = B * S
    P = 2 * T + N_EXPERTS * MOE_BLOCK
    nblk = P // MOE_BLOCK
    sizes = counts[0, ROUTE_OFF:ROUTE_OFF + N_EXPERTS].astype(jnp.int32)
    psizes = (sizes + MOE_BLOCK - 1) // MOE_BLOCK * MOE_BLOCK
    pends = jnp.cumsum(psizes)
    poffs = pends - psizes
    starts = jnp.arange(nblk, dtype=jnp.int32) * MOE_BLOCK
    owns = pends[None, :] <= starts[:, None]
    blk_e = jnp.minimum(jnp.sum(owns.astype(jnp.int32), axis=1), N_EXPERTS - 1)
    in_blk = blk_e[:, None] == jnp.arange(N_EXPERTS, dtype=jnp.int32)[None, :]
    seg_end = jnp.sum(jnp.where(in_blk, (poffs + sizes)[None, :], 0), axis=1)
    n_valid = jnp.clip(seg_end - starts, 0, MOE_BLOCK).astype(jnp.int32)
    r4 = route.reshape(T, LANES)[:, :4]
    sel = r4[:, :2, None] == jnp.arange(N_EXPERTS, dtype=jnp.int32)[None, None, :]
    dest = r4[:, 2:4] + jnp.sum(jnp.where(sel, poffs[None, None, :], 0), axis=-1)
    xb = _sc_scatter_rows(h2.reshape(T, D // 2), dest[:, 0], dest[:, 1], P)
    yb = _experts(xb, blk_e, n_valid, w1, w3, w2)
    slot_major = dest.T.reshape(2 * T)
    rows = _sc_gather_rows(yb, slot_major).reshape(2, T, yb.shape[1])
    out = _combine_dense(rows, wts.reshape(T, LANES), x1.reshape(T, D), mod, ln2_g, ln2_b, S)
    return out.reshape(B, S, D)


def _layer(x, c, w_ada, b_ada, w_in, conv_w, conv_b, fw1, fb1, ff1, fw2, fb2, ff2, fw3, decay, skip,
           w_hy_o, w_attn_o, attn_sink, w_out, ln1_g, ln1_b, rg_w, rg_b, re_w, re_b, ew1, ew3, ew2,
           ln2_g, ln2_b):
    mod = _ada(c, w_ada, b_ada)
    q, kv, hv, hx1, hx2, ga, gh = _in_proj(x, mod, w_in, conv_w, conv_b)
    attn = _attention(q, kv, attn_sink)
    hy = _hyena(hv, hx1, hx2, fw1, fb1, ff1, fw2, fb2, ff2, fw3, decay, skip)
    x1, h2, route, wts, counts = _merge(attn, hy, ga, gh, x, mod, w_attn_o, w_hy_o, w_out,
                                        ln1_g, ln1_b, rg_w, rg_b, re_w, re_b)
    return _moe(h2, x1, route, wts, counts, mod, ew1, ew3, ew2, ln2_g, ln2_b)


def kernel(x, c, w_ada, b_ada, w_in, conv_w, conv_b, filt_w1, filt_b1, filt_freq1, filt_w2, filt_b2, filt_freq2, filt_w3, filt_decay, hy_skip, w_hy_o, w_attn_o, attn_sink, w_out, ln1_g, ln1_b, router_group_w, router_group_b, router_expert_w, router_expert_b, exp_w1, exp_w3, exp_w2, ln2_g, ln2_b):
    for l in range(w_ada.shape[0]):
        x = _layer(x, c, w_ada[l], b_ada[l], w_in[l], conv_w[l], conv_b[l], filt_w1[l], filt_b1[l],
                   filt_freq1[l], filt_w2[l], filt_b2[l], filt_freq2[l], filt_w3[l], filt_decay[l],
                   hy_skip[l], w_hy_o[l], w_attn_o[l], attn_sink[l], w_out[l], ln1_g[l], ln1_b[l],
                   router_group_w[l], router_group_b[l], router_expert_w[l], router_expert_b[l],
                   exp_w1[l], exp_w3[l], exp_w2[l], ln2_g[l], ln2_b[l])
    return x
```

```python
import functools
import math

import numpy as np
import jax
import jax.numpy as jnp
from jax import lax
from jax.experimental import pallas as pl
from jax.experimental.pallas import tpu as pltpu
from jax.experimental.pallas import tpu_sc as plsc

F32 = jnp.float32
BF16 = jnp.bfloat16

N_HEADS = 8
N_KV_HEADS = 2
HEAD_DIM = 64
ATTN_WIDTH = N_HEADS * HEAD_DIM
KV_WIDTH = N_KV_HEADS * HEAD_DIM
WINDOW = 128
BLOCK_Q = 128
HYENA_WIDTH = 512
FILTER_EMB = 33
FILTER_BANDS = (FILTER_EMB - 1) // 2
WINDOW_SHIFT = 0.05
N_GROUPS = 8
EXPERTS_PER_GROUP = 8
N_EXPERTS = N_GROUPS * EXPERTS_PER_GROUP
D_EXPERT = 512
MOE_BLOCK = 512
LN_EPS = 1e-5
DEPTH = 1
DN_ALPHA = (2.0 * DEPTH) ** 0.25
NEG = -1e30

LANES = 128
SUBLANES = 8
ROUTE_OFF = N_GROUPS
VMEM_LIMIT = 56 * 1024 * 1024


def _params(n_axes, vmem=VMEM_LIMIT):
    return pltpu.CompilerParams(dimension_semantics=("arbitrary",) * n_axes, vmem_limit_bytes=vmem)


def _split(a):
    hi = a.astype(BF16)
    lo = (a - hi.astype(F32)).astype(BF16)
    return hi, lo


def _dot3(a, b_hi, b_lo):
    a_hi, a_lo = _split(a)
    acc = jnp.dot(a_hi, b_hi, preferred_element_type=F32)
    acc = acc + jnp.dot(a_hi, b_lo, preferred_element_type=F32)
    acc = acc + jnp.dot(a_lo, b_hi, preferred_element_type=F32)
    return acc


def _pack2(a, b):
    ia = lax.bitcast_convert_type(a.astype(BF16).astype(F32), jnp.int32)
    ib = lax.bitcast_convert_type(b.astype(BF16).astype(F32), jnp.int32)
    return lax.bitcast_convert_type(ia | lax.shift_right_logical(ib, 16), F32)


def _unpack2(p):
    p = lax.bitcast_convert_type(p, jnp.int32)
    a = lax.bitcast_convert_type(p & jnp.int32(-65536), F32)
    b = lax.bitcast_convert_type(lax.shift_left(p, 16), F32)
    return a, b


def _sigmoid(x):
    return 1.0 / (1.0 + jnp.exp(-x))


def _layer_norm(r, g, b):
    mu = jnp.mean(r, axis=-1, keepdims=True)
    d = r - mu
    var = jnp.mean(d * d, axis=-1, keepdims=True)
    return d * lax.rsqrt(var + LN_EPS) * g + b


def _ada_kernel(c_ref, wh_ref, wl_ref, b_ref, o_ref):
    c = c_ref[...]
    s = c * _sigmoid(c)
    o_ref[...] = _dot3(s, wh_ref[...], wl_ref[...]) + b_ref[...]


def _ada(c, w_ada, b_ada):
    B, D = c.shape
    n_out = w_ada.shape[1]
    rows = 8
    cp = jnp.zeros((rows, D), F32).at[:B].set(c)
    wh, wl = _split(w_ada)
    tn = 1024
    out = pl.pallas_call(
        _ada_kernel,
        out_shape=jax.ShapeDtypeStruct((rows, n_out), F32),
        grid=(n_out // tn,),
        in_specs=[pl.BlockSpec((rows, D), lambda j: (0, 0)),
                  pl.BlockSpec((D, tn), lambda j: (0, j)),
                  pl.BlockSpec((D, tn), lambda j: (0, j)),
                  pl.BlockSpec((1, tn), lambda j: (0, j))],
        out_specs=pl.BlockSpec((rows, tn), lambda j: (0, j)),
        compiler_params=_params(1),
        name="ada",
    )(cp, wh, wl, b_ada.reshape(1, n_out))
    return out[:B].reshape(B, 6, D)


def _inproj_kernel(x_ref, xp_ref, xn_ref, mod_ref, w_ref, cw_ref, cb_ref,
                   q_ref, kv_ref, v_ref, x1_ref, x2_ref, ga_ref, gh_ref):
    i = pl.program_id(1)
    n = pl.num_programs(1)
    C = HYENA_WIDTH
    x = x_ref[0]
    tm, D = x.shape
    shift = mod_ref[0, 0:1, :]
    scale = mod_ref[0, 1:2, :]
    h = (x * (1.0 + scale) + shift).astype(BF16)

    def seg(lo, hi):
        return jnp.dot(h, w_ref[:, lo:hi], preferred_element_type=F32)

    o = 0
    q_ref[0] = (seg(o, o + ATTN_WIDTH) * (HEAD_DIM ** -0.5)).astype(BF16)
    o += ATTN_WIDTH
    kv_ref[0] = seg(o, o + 2 * KV_WIDTH).astype(BF16)
    o += 2 * KV_WIDTH

    u = seg(o, o + 3 * C)
    xe = jnp.concatenate([xp_ref[0], xn_ref[0]], axis=0)
    he = (xe * (1.0 + scale) + shift).astype(BF16)
    ue = jnp.dot(he, w_ref[:, o:o + 3 * C], preferred_element_type=F32)
    prow = jnp.where(i > 0, ue[SUBLANES - 1:SUBLANES], 0.0)
    nrow = jnp.where(i < n - 1, ue[SUBLANES:SUBLANES + 1], 0.0)
    rid = lax.broadcasted_iota(jnp.int32, (tm, 1), 0)
    up = jnp.where(rid == 0, prow, pltpu.roll(u, 1, 0))
    dn = jnp.where(rid == tm - 1, nrow, pltpu.roll(u, tm - 1, 0))
    conv = cw_ref[0:1, :] * up + cw_ref[1:2, :] * u + cw_ref[2:3, :] * dn + cb_ref[...]
    v_ref[0] = conv[:, :C]
    x1_ref[0] = conv[:, C:2 * C]
    x2_ref[0] = conv[:, 2 * C:]
    o += 3 * C

    ga_ref[0] = _sigmoid(seg(o, o + D)).astype(BF16)
    o += D
    gh_ref[0] = _sigmoid(seg(o, o + D)).astype(BF16)


def _in_proj(x, mod, w_in, conv_w, conv_b):
    B, S, D = x.shape
    C = HYENA_WIDTH
    tm = min(512, S)
    r8 = tm // SUBLANES
    nb8 = S // SUBLANES
    wb = w_in.astype(BF16)
    nw = wb.shape[1]
    row = lambda b, i: (b, i, 0)
    shapes = [(ATTN_WIDTH, BF16), (2 * KV_WIDTH, BF16), (C, F32), (C, F32), (C, F32), (D, BF16), (D, BF16)]
    return pl.pallas_call(
        _inproj_kernel,
        out_shape=[jax.ShapeDtypeStruct((B, S, w), dt) for w, dt in shapes],
        grid=(B, S // tm),
        in_specs=[pl.BlockSpec((1, tm, D), row),
                  pl.BlockSpec((1, SUBLANES, D), lambda b, i: (b, jnp.maximum(i * r8 - 1, 0), 0)),
                  pl.BlockSpec((1, SUBLANES, D), lambda b, i: (b, jnp.minimum((i + 1) * r8, nb8 - 1), 0)),
                  pl.BlockSpec((1, 6, D), lambda b, i: (b, 0, 0)),
                  pl.BlockSpec((D, nw), lambda b, i: (0, 0)),
                  pl.BlockSpec((3, 3 * C), lambda b, i: (0, 0)),
                  pl.BlockSpec((1, 3 * C), lambda b, i: (0, 0))],
        out_specs=[pl.BlockSpec((1, tm, w), row) for w, _ in shapes],
        compiler_params=_params(2),
        name="in_proj",
    )(x, x, x, mod, wb, conv_w.astype(F32), conv_b.reshape(1, 3 * C).astype(F32))


ATT_TQ = 512


def _attn_kernel(sink_ref, q_ref, kvp_ref, kvc_ref, kvn_ref, bias_ref, o_ref, kv_scr, *, seq_len):
    i = pl.program_id(1)
    Q = BLOCK_Q
    TQ = q_ref.shape[1]
    G = N_HEADS // N_KV_HEADS
    kv_scr[0:Q] = kvp_ref[0]
    kv_scr[Q:Q + TQ] = kvc_ref[0]
    kv_scr[Q + TQ:] = kvn_ref[0]
    col = lax.broadcasted_iota(jnp.int32, (1, 3 * Q), 1)
    rhead = lax.broadcasted_iota(jnp.int32, (G * Q, 1), 0) // Q
    for j in range(TQ // Q):
        kpos = i * TQ + (j - 1) * Q + col
        colbias = jnp.where((kpos >= 0) & (kpos < seq_len), 0.0, NEG)
        for kv in range(N_KV_HEADS):
            kk = kv_scr[j * Q:(j + 3) * Q, kv * HEAD_DIM:(kv + 1) * HEAD_DIM]
            vv = kv_scr[j * Q:(j + 3) * Q, KV_WIDTH + kv * HEAD_DIM:KV_WIDTH + (kv + 1) * HEAD_DIM]
            heads = [kv * G + g for g in range(G)]
            qg = jnp.concatenate([q_ref[0, j * Q:(j + 1) * Q, h * HEAD_DIM:(h + 1) * HEAD_DIM] for h in heads], axis=0)
            s = lax.dot_general(qg, kk, (((1,), (1,)), ((), ())), preferred_element_type=F32)
            s = s + bias_ref[kv] + colbias
            snk = jnp.where(rhead == 0, sink_ref[heads[0]],
                            jnp.where(rhead == 1, sink_ref[heads[1]],
                                      jnp.where(rhead == 2, sink_ref[heads[2]], sink_ref[heads[3]])))
            m = jnp.maximum(jnp.max(s, axis=1, keepdims=True), snk)
            p = jnp.exp(s - m)
            den = jnp.sum(p, axis=1, keepdims=True) + jnp.exp(snk - m)
            o = jnp.dot(p.astype(BF16), vv, preferred_element_type=F32) / den
            for g, h in enumerate(heads):
                o_ref[0, j * Q:(j + 1) * Q, h * HEAD_DIM:(h + 1) * HEAD_DIM] = o[g * Q:(g + 1) * Q].astype(BF16)


def _attention(q, kv, sink):
    B, S, _ = q.shape
    Q = BLOCK_Q
    TQ = min(ATT_TQ, S)
    r = TQ // Q
    nq = S // Q
    G = N_HEADS // N_KV_HEADS
    assert G == 4
    a = jnp.arange(Q)[:, None]
    j = jnp.arange(3 * Q)[None, :]
    rel = jnp.abs(j - Q - a).astype(F32)
    slopes = 2.0 ** (-8.0 * jnp.arange(1, N_HEADS + 1, dtype=F32) / N_HEADS)
    bias = jnp.where(rel[None] <= WINDOW, -slopes[:, None, None] * rel[None], NEG).astype(F32)
    bias = bias.reshape(N_KV_HEADS, G * Q, 3 * Q)
    cur = lambda b, i: (b, i, 0)
    return pl.pallas_call(
        functools.partial(_attn_kernel, seq_len=S),
        out_shape=jax.ShapeDtypeStruct((B, S, ATTN_WIDTH), BF16),
        grid=(B, S // TQ),
        in_specs=[pl.BlockSpec(memory_space=pltpu.SMEM),
                  pl.BlockSpec((1, TQ, ATTN_WIDTH), cur),
                  pl.BlockSpec((1, Q, 2 * KV_WIDTH), lambda b, i: (b, jnp.maximum(i * r - 1, 0), 0)),
                  pl.BlockSpec((1, TQ, 2 * KV_WIDTH), cur),
                  pl.BlockSpec((1, Q, 2 * KV_WIDTH), lambda b, i: (b, jnp.minimum((i + 1) * r, nq - 1), 0)),
                  pl.BlockSpec((N_KV_HEADS, G * Q, 3 * Q), lambda b, i: (0, 0, 0))],
        out_specs=pl.BlockSpec((1, TQ, ATTN_WIDTH), cur),
        scratch_shapes=[pltpu.VMEM((TQ + 2 * Q, 2 * KV_WIDTH), BF16)],
        compiler_params=_params(2),
        name="attn",
    )(sink.astype(F32), q, kv, kv, kv, bias)


def _filter_kernel(z_ref, w1h, w1l, b1_ref, f1_ref, w2h, w2l, b2_ref, f2_ref, w3h, w3l, dec_ref,
                   k_ref, s_ref):
    i = pl.program_id(0)
    z = z_ref[...]
    h1 = jnp.sin(f1_ref[...] * (_dot3(z, w1h[...], w1l[...]) + b1_ref[...]))
    h2 = jnp.sin(f2_ref[...] * (_dot3(h1, w2h[...], w2l[...]) + b2_ref[...]))
    k = _dot3(h2, w3h[...], w3l[...])
    t = z[:, 0:1]
    k = k * (jnp.exp(-t * jnp.abs(dec_ref[...])) + WINDOW_SHIFT)
    k_ref[...] = k

    @pl.when(i == 0)
    def _():
        s_ref[...] = jnp.zeros_like(s_ref)

    s_ref[...] += jnp.sum(jnp.abs(k), axis=0, keepdims=True)


def _filter_embedding(L):
    t = np.linspace(0.0, 1.0, L, dtype=np.float32).astype(np.float64)[:, None]
    w = (2.0 * math.pi * np.arange(L, dtype=np.float32) / np.float32(L)).astype(np.float64)[:, None]
    bands = np.linspace(1e-4, FILTER_BANDS - 1, FILTER_BANDS, dtype=np.float32).astype(np.float64)[None, :]
    bw = (bands.astype(np.float32) * w.astype(np.float32)).astype(np.float64)
    z = np.concatenate([t, np.cos(bw), -np.sin(bw)], axis=-1)
    zp = np.zeros((L, LANES), np.float32)
    zp[:, :FILTER_EMB] = z.astype(np.float32)
    return jnp.asarray(zp)


def _pad2(a, r, c):
    return jnp.zeros((r, c), F32).at[:a.shape[0], :a.shape[1]].set(a.astype(F32))


def _filters(L, fw1, fb1, ff1, fw2, fb2, ff2, fw3, decay):
    H = LANES
    nf = fw3.shape[1]
    z = _filter_embedding(L)
    w1h, w1l = _split(_pad2(fw1, H, H))
    w2h, w2l = _split(_pad2(fw2, H, H))
    w3h, w3l = _split(_pad2(fw3, H, nf))
    b1 = _pad2(fb1[None], 1, H)
    f1 = _pad2(ff1[None], 1, H)
    b2 = _pad2(fb2[None], 1, H)
    f2 = _pad2(ff2[None], 1, H)
    tr = min(512, L)
    full = lambda r, c: pl.BlockSpec((r, c), lambda i: (0, 0))
    return pl.pallas_call(
        _filter_kernel,
        out_shape=[jax.ShapeDtypeStruct((L, nf), F32), jax.ShapeDtypeStruct((1, nf), F32)],
        grid=(L // tr,),
        in_specs=[pl.BlockSpec((tr, H), lambda i: (i, 0)),
                  full(H, H), full(H, H), full(1, H), full(1, H),
                  full(H, H), full(H, H), full(1, H), full(1, H),
                  full(H, nf), full(H, nf), full(1, nf)],
        out_specs=[pl.BlockSpec((tr, nf), lambda i: (i, 0)), full(1, nf)],
        compiler_params=_params(1),
        name="filter",
    )(z, w1h, w1l, b1, f1, w2h, w2l, b2, f2, w3h, w3l, decay.reshape(1, nf).astype(F32))


def _np_bf16(m64):
    return jnp.asarray(m64.astype(np.float32).astype(BF16))


def _dft_constants(L):
    N = 2 * L
    n2 = LANES
    n1 = N // n2
    h1 = n1 // 2
    k1 = np.arange(n1)[:, None]
    s1 = np.arange(h1)[None, :]
    ang = -2.0 * np.pi * ((k1 * s1) % n1) / n1
    wr, wi = np.cos(ang), np.sin(ang)
    w1_filt = np.block([[wr, wr], [wi, wi], [wr, -wr], [wi, -wi]])
    w1_cplx = np.block([[wr, -wi], [wi, wr]])
    vr, vi = wr.T / N, -wi.T / N
    w3 = np.block([[vr, -vi], [vi, vr]])
    k2 = np.arange(n2)[:, None]
    s2 = np.arange(n2)[None, :]
    a2 = -2.0 * np.pi * ((k2 * s2) % n2) / n2
    w2r, w2i = jnp.asarray(np.cos(a2), F32), jnp.asarray(np.sin(a2), F32)
    at = -2.0 * np.pi * ((np.arange(n1)[:, None] * s2) % N) / N
    twr, twi = jnp.asarray(np.cos(at), F32), jnp.asarray(np.sin(at), F32)
    mr = w2r[None] * twr[:, None, :] - w2i[None] * twi[:, None, :]
    mi = w2r[None] * twi[:, None, :] + w2i[None] * twr[:, None, :]
    fwd = jnp.concatenate([jnp.concatenate([mr, -mi], axis=2),
                           jnp.concatenate([mi, mr], axis=2)], axis=1)
    fwd = fwd.astype(BF16)
    return dict(n1=n1, w1_filt=_np_bf16(w1_filt), w1_cplx=_np_bf16(w1_cplx), w3=_np_bf16(w3),
                fwd=fwd, inv=jnp.swapaxes(fwd, 1, 2))


SCH = 8


def _dft1_kernel(x_ref, w_ref, a_ref, *, n1):
    w = w_ref[...]
    for j in range(SCH):
        rhs = jnp.concatenate([x_ref[0, 0, :, j, :], x_ref[0, 1, :, j, :]], axis=0)
        res = jnp.dot(w, rhs.astype(BF16), preferred_element_type=F32)
        a_ref[0, :, j, :] = _pack2(res[:n1], res[n1:])


def _dft1_data(x, consts):
    B, L, C = x.shape
    n1 = consts["n1"]
    h1 = n1 // 2
    xv = x.reshape(B // 2, 2, h1, LANES, C)
    return pl.pallas_call(
        functools.partial(_dft1_kernel, n1=n1),
        out_shape=jax.ShapeDtypeStruct((B // 2, n1, LANES, C), F32),
        grid=(B // 2, LANES // SCH),
        in_specs=[pl.BlockSpec((1, 2, h1, SCH, C), lambda p, j: (p, 0, 0, j, 0)),
                  pl.BlockSpec((2 * n1, n1), lambda p, j: (0, 0))],
        out_specs=pl.BlockSpec((1, n1, SCH, C), lambda p, j: (p, 0, j, 0)),
        compiler_params=_params(2),
        name="dft1",
    )(xv, consts["w1_cplx"])


def _dft1f_kernel(x_ref, w_ref, a_ref, *, n1):
    C = HYENA_WIDTH
    w = w_ref[...]
    for j in range(SCH):
        rhs = jnp.concatenate([x_ref[:, j, :C], x_ref[:, j, C:]], axis=0)
        res = jnp.dot(w, rhs.astype(BF16), preferred_element_type=F32)
        a_ref[0, :, 0, j, :] = _pack2(res[:n1], res[n1:2 * n1])
        a_ref[0, :, 1, j, :] = _pack2(res[2 * n1:3 * n1], res[3 * n1:])


def _dft1_filter(kraw, consts):
    L, nf = kraw.shape
    C = HYENA_WIDTH
    n_ord = nf // (2 * C)
    n1 = consts["n1"]
    h1 = n1 // 2
    kv = kraw.reshape(h1, LANES, nf)
    return pl.pallas_call(
        functools.partial(_dft1f_kernel, n1=n1),
        out_shape=jax.ShapeDtypeStruct((n_ord, n1, 2, LANES, C), F32),
        grid=(n_ord, LANES // SCH),
        in_specs=[pl.BlockSpec((h1, SCH, 2 * C), lambda o, j: (0, j, o)),
                  pl.BlockSpec((4 * n1, n1), lambda o, j: (0, 0))],
        out_specs=pl.BlockSpec((1, n1, 2, SCH, C), lambda o, j: (o, 0, 0, j, 0)),
        compiler_params=_params(2),
        name="dft1f",
    )(kv, consts["w1_filt"])


KCH = 8


def _midf_kernel(a_ref, f_ref, inv_ref, b0_ref, h_ref):
    n2 = LANES
    sc = inv_ref[0]
    for k in range(KCH):
        p = jnp.concatenate(_unpack2(a_ref[0, k, :n2, :]), axis=0).astype(BF16)
        q = jnp.concatenate(_unpack2(a_ref[0, k, n2:, :]), axis=0).astype(BF16)
        h_re = jnp.dot(f_ref[k, :n2, :], p, preferred_element_type=F32)
        h_im = jnp.dot(f_ref[k, n2:, :], q, preferred_element_type=F32)
        h_ref[0, k] = _pack2((h_re - b0_ref[0]) * sc, h_im * sc)


def _filter_spectrum(af, inv_den, bwd0, consts):
    n_ord, n1, _, n2, C = af.shape
    a = af.reshape(n_ord, n1, 2 * n2, C)
    tab = pl.BlockSpec((KCH, 2 * n2, 2 * n2), lambda k, o: (k, 0, 0))
    vec = pl.BlockSpec((1, 1, C), lambda k, o: (o, 0, 0))
    return pl.pallas_call(
        _midf_kernel,
        out_shape=jax.ShapeDtypeStruct((n_ord, n1, n2, C), F32),
        grid=(n1 // KCH, n_ord),
        in_specs=[pl.BlockSpec((1, KCH, 2 * n2, C), lambda k, o: (o, k, 0, 0)), tab, vec, vec],
        out_specs=pl.BlockSpec((1, KCH, n2, C), lambda k, o: (o, k, 0, 0)),
        compiler_params=_params(2),
        name="midf",
    )(a, consts["fwd"], inv_den, bwd0)


def _mid_kernel(a_ref, f_ref, i_ref, h_ref, b_ref):
    n2 = LANES
    for k in range(KCH):
        a = jnp.concatenate(_unpack2(a_ref[0, k]), axis=0).astype(BF16)
        x = jnp.dot(f_ref[k], a, preferred_element_type=F32)
        xr, xi = x[:n2], x[n2:]
        hr, hi = _unpack2(h_ref[0, k])
        y = jnp.concatenate([xr * hr - xi * hi, xr * hi + xi * hr], axis=0)
        b = jnp.dot(i_ref[k], y.astype(BF16), preferred_element_type=F32)
        b_ref[0, k] = _pack2(b[:n2], b[n2:])


def _mid(a, hspec, order, consts):
    P, n1, n2, C = a.shape
    tab = pl.BlockSpec((KCH, 2 * n2, 2 * n2), lambda k, p: (k, 0, 0))
    return pl.pallas_call(
        _mid_kernel,
        out_shape=jax.ShapeDtypeStruct((P, n1, n2, C), F32),
        grid=(n1 // KCH, P),
        in_specs=[pl.BlockSpec((1, KCH, n2, C), lambda k, p: (p, k, 0, 0)),
                  tab, tab,
                  pl.BlockSpec((1, KCH, n2, C), lambda k, p: (order, k, 0, 0))],
        out_specs=pl.BlockSpec((1, KCH, n2, C), lambda k, p: (p, k, 0, 0)),
        compiler_params=_params(2),
        name="mid",
    )(a, consts["fwd"], consts["inv"], hspec)


def _dft3_kernel(b_ref, w_ref, v_ref, g_ref, skip_ref, z_ref, slab_ref, *, h1):
    w = w_ref[...]
    skip = skip_ref[0]
    for j in range(SCH):
        slab_ref[...] = b_ref[0, :, j, :]
        rhs = jnp.concatenate(_unpack2(slab_ref[...]), axis=0)
        y = jnp.dot(w, rhs.astype(BF16), preferred_element_type=F32)
        for r in range(2):
            yr = y[r * h1:(r + 1) * h1]
            z_ref[0, r, :, j, :] = g_ref[0, r, :, j, :] * (yr + v_ref[0, r, :, j, :] * skip)


def _dft3_gate(b5, v, gate, skip, consts):
    P, n1, n2, C = b5.shape
    h1 = n1 // 2
    B, L, _ = v.shape
    five = lambda t: t.reshape(P, 2, h1, n2, C)
    dat = pl.BlockSpec((1, 2, h1, SCH, C), lambda p, j: (p, 0, 0, j, 0))
    out = pl.pallas_call(
        functools.partial(_dft3_kernel, h1=h1),
        out_shape=jax.ShapeDtypeStruct((P, 2, h1, n2, C), F32),
        grid=(P, n2 // SCH),
        in_specs=[pl.BlockSpec((1, n1, SCH, C), lambda p, j: (p, 0, j, 0)),
                  pl.BlockSpec((n1, 2 * n1), lambda p, j: (0, 0)),
                  dat, dat,
                  pl.BlockSpec((1, C), lambda p, j: (0, 0))],
        out_specs=dat,
        scratch_shapes=[pltpu.VMEM((n1, C), F32)],
        compiler_params=_params(2),
        name="dft3",
    )(b5, consts["w3"], five(v), five(gate), skip.reshape(1, C).astype(F32))
    return out.reshape(B, L, C)


def _hyena(v, x1, x2, fw1, fb1, ff1, fw2, fb2, ff2, fw3, decay, skip):
    B, L, C = v.shape
    consts = _dft_constants(L)
    kraw, ksum = _filters(L, fw1, fb1, ff1, fw2, fb2, ff2, fw3, decay)
    ks = ksum.reshape(2, 2, C)
    inv_den = (1.0 / (ks[:, 0] + ks[:, 1])).reshape(2, 1, C)
    bwd0 = kraw[0].reshape(2, 2, C)[:, 1].reshape(2, 1, C)
    hspec = _filter_spectrum(_dft1_filter(kraw, consts), inv_den, bwd0, consts)
    z = v
    for o, gate in enumerate((x1, x2)):
        a5 = _dft1_data(z, consts)
        b5 = _mid(a5, hspec, o, consts)
        z = _dft3_gate(b5, z, gate, skip[o], consts)
    return z


def _merge_kernel(attn_ref, hy_ref, ga_ref, gh_ref, x_ref, mod_ref, wa_ref, wh_ref, wo_ref,
                  g1_ref, b1_ref, rwh_ref, rwl_ref, rb_ref, tri_ref,
                  x1_ref, h2_ref, route_ref, wts_ref, cnt_ref, carry_ref):
    first = (pl.program_id(0) == 0) & (pl.program_id(1) == 0)

    @pl.when(first)
    def _():
        carry_ref[...] = jnp.zeros_like(carry_ref)

    a = jnp.dot(attn_ref[0], wa_ref[...], preferred_element_type=F32)
    hy = jnp.dot(hy_ref[0].astype(BF16), wh_ref[...], preferred_element_type=F32)
    merged = ga_ref[0].astype(F32) * a + gh_ref[0].astype(F32) * hy
    y = jnp.dot(merged.astype(BF16), wo_ref[...], preferred_element_type=F32)
    gate1 = mod_ref[0, 2:3, :]
    shift2 = mod_ref[0, 3:4, :]
    scale2 = mod_ref[0, 4:5, :]
    x1 = _layer_norm(DN_ALPHA * x_ref[0] + gate1 * y, g1_ref[...], b1_ref[...])
    x1_ref[0] = x1
    h2 = x1 * (1.0 + scale2) + shift2
    half = h2.shape[1] // 2
    h2_ref[0] = _pack2(h2[:, :half], h2[:, half:])

    logits = _dot3(h2, rwh_ref[...], rwl_ref[...]) + rb_ref[...]
    tm = logits.shape[0]
    lane = lax.broadcasted_iota(jnp.int32, (tm, LANES), 1)
    lanef = lane.astype(F32)
    big = float(LANES)

    def first_lane(mask):
        return jnp.min(jnp.where(mask, lanef, big), axis=1, keepdims=True).astype(jnp.int32)

    gmask = lane < N_GROUPS
    gl = jnp.where(gmask, logits, NEG)
    gmax = jnp.max(gl, axis=1, keepdims=True)
    gidx = first_lane(gl == gmax)
    pg = 1.0 / jnp.sum(jnp.exp(gl - gmax), axis=1, keepdims=True)
    lo = ROUTE_OFF + gidx * EXPERTS_PER_GROUP
    emask = (lane >= lo) & (lane < lo + EXPERTS_PER_GROUP)
    el = jnp.where(emask, logits, NEG)
    v1 = jnp.max(el, axis=1, keepdims=True)
    i1 = first_lane(el == v1)
    el2 = jnp.where(emask & (lane != i1), logits, NEG)
    v2 = jnp.max(el2, axis=1, keepdims=True)
    i2 = first_lane(el2 == v2)
    e21 = jnp.exp(v2 - v1)
    w1 = pg / (1.0 + e21)
    w2 = pg * e21 / (1.0 + e21)

    sel1 = lane == i1
    sel2 = lane == i2
    onehot = jnp.where(sel1 | sel2, 1.0, 0.0)
    prefix = jnp.dot(tri_ref[...], onehot.astype(BF16), preferred_element_type=F32) + carry_ref[...]
    r1 = jnp.sum(jnp.where(sel1, prefix, 0.0), axis=1, keepdims=True)
    r2 = jnp.sum(jnp.where(sel2, prefix, 0.0), axis=1, keepdims=True)
    carry_ref[...] += jnp.sum(onehot, axis=0, keepdims=True)
    cnt_ref[...] = carry_ref[...]

    ranks = jnp.where(lane == 2, r1, jnp.where(lane == 3, r2, 0.0)).astype(jnp.int32)
    route_ref[0] = jnp.where(lane == 0, i1 - ROUTE_OFF, jnp.where(lane == 1, i2 - ROUTE_OFF, ranks))
    wts_ref[0] = jnp.where(lane == 0, w1, jnp.where(lane == 1, w2, 0.0))


def _merge(attn, hy, ga, gh, x, mod, w_attn_o, w_hy_o, w_out, ln1_g, ln1_b, rg_w, rg_b, re_w, re_b):
    B, S, D = x.shape
    tm = min(512, S)
    rw = jnp.zeros((D, LANES), F32).at[:, :N_GROUPS].set(rg_w).at[:, ROUTE_OFF:ROUTE_OFF + N_EXPERTS].set(re_w)
    rb = jnp.zeros((1, LANES), F32).at[0, :N_GROUPS].set(rg_b).at[0, ROUTE_OFF:ROUTE_OFF + N_EXPERTS].set(re_b)
    rwh, rwl = _split(rw)
    tri = (jnp.arange(tm)[:, None] > jnp.arange(tm)[None, :]).astype(BF16)
    row = lambda b, i: (b, i, 0)
    full = lambda r, c: pl.BlockSpec((r, c), lambda b, i: (0, 0))
    outs = [jax.ShapeDtypeStruct((B, S, D), F32), jax.ShapeDtypeStruct((B, S, D // 2), F32),
            jax.ShapeDtypeStruct((B, S, LANES), jnp.int32), jax.ShapeDtypeStruct((B, S, LANES), F32),
            jax.ShapeDtypeStruct((1, LANES), F32)]
    return pl.pallas_call(
        _merge_kernel,
        out_shape=outs,
        grid=(B, S // tm),
        in_specs=[pl.BlockSpec((1, tm, ATTN_WIDTH), row), pl.BlockSpec((1, tm, HYENA_WIDTH), row),
                  pl.BlockSpec((1, tm, D), row), pl.BlockSpec((1, tm, D), row), pl.BlockSpec((1, tm, D), row),
                  pl.BlockSpec((1, 6, D), lambda b, i: (b, 0, 0)),
                  full(ATTN_WIDTH, D), full(HYENA_WIDTH, D), full(D, D),
                  full(1, D), full(1, D), full(D, LANES), full(D, LANES), full(1, LANES), full(tm, tm)],
        out_specs=[pl.BlockSpec((1, tm, D), row), pl.BlockSpec((1, tm, D // 2), row),
                   pl.BlockSpec((1, tm, LANES), row), pl.BlockSpec((1, tm, LANES), row), full(1, LANES)],
        scratch_shapes=[pltpu.VMEM((1, LANES), F32)],
        compiler_params=_params(2),
        name="merge",
    )(attn, hy, ga, gh, x, mod, w_attn_o.astype(BF16), w_hy_o.astype(BF16), w_out.astype(BF16),
      ln1_g.reshape(1, D), ln1_b.reshape(1, D), rwh, rwl, rb, tri)


SC_ROWS = 64


def _sc_workers():
    info = plsc.get_sparse_core_info()
    return info.num_cores, info.num_cores * info.num_subcores


def _sc_split(n):
    _, workers = _sc_workers()
    per_worker = n // workers
    chunks = per_worker // SC_ROWS
    assert per_worker * workers == n and chunks * SC_ROWS == per_worker and chunks % 2 == 0
    return workers, per_worker, chunks


def _sc_scatter_rows(src, idx0, idx1, n_out):
    n, width = src.shape
    nc, _ = _sc_workers()
    workers, per_worker, chunks = _sc_split(n)
    mesh = plsc.VectorSubcoreMesh(core_axis_name="c", subcore_axis_name="s")

    def body(src_hbm, i0_hbm, i1_hbm, out_hbm, i0_v, i1_v, rows_v, sem):
        wid = lax.axis_index("s") * nc + lax.axis_index("c")
        base = wid * per_worker
        pltpu.sync_copy(i0_hbm.at[wid], i0_v)
        pltpu.sync_copy(i1_hbm.at[wid], i1_v)

        def load(chunk, buf):
            return pltpu.make_async_copy(src_hbm.at[pl.ds(base + chunk * SC_ROWS, SC_ROWS)], rows_v.at[buf], sem)

        load(0, 0).start()

        @pl.loop(0, chunks, step=2)
        def _(c):
            for b in range(2):
                chunk = c + b
                load(chunk, b).wait()

                @pl.when(chunk + 1 < chunks)
                def _():
                    load(chunk + 1, 1 - b).start()

                pltpu.sync_copy(rows_v.at[b], out_hbm.at[i0_v.at[chunk]])
                pltpu.sync_copy(rows_v.at[b], out_hbm.at[i1_v.at[chunk]])

    shaped = lambda i: i.reshape(workers, chunks, SC_ROWS)
    return pl.kernel(
        body,
        out_type=jax.ShapeDtypeStruct((n_out, width), src.dtype),
        mesh=mesh,
        scratch_types=[pltpu.VMEM((chunks, SC_ROWS), jnp.int32),
                       pltpu.VMEM((chunks, SC_ROWS), jnp.int32),
                       pltpu.VMEM((2, SC_ROWS, width), src.dtype),
                       pltpu.SemaphoreType.DMA],
        name="sc_scatter",
    )(src, shaped(idx0), shaped(idx1))


def _expert_kernel(be_ref, nv_ref, x_ref, w1_ref, w3_ref, w2_ref, y_ref, c1_ref, c3_ref, c2_ref):
    i = pl.program_id(0)
    n_valid = nv_ref[i]
    used = n_valid > 0
    fresh = (i == 0) | (be_ref[i] != be_ref[jnp.maximum(i - 1, 0)])

    @pl.when(used & fresh)
    def _():
        c1_ref[...] = w1_ref[0].astype(BF16)
        c3_ref[...] = w3_ref[0].astype(BF16)
        c2_ref[...] = w2_ref[0].astype(BF16)

    @pl.when(used)
    def _():
        rid = lax.broadcasted_iota(jnp.int32, (x_ref.shape[0], 1), 0)
        xa, xb = _unpack2(jnp.where(rid < n_valid, x_ref[...], 0.0))
        x = jnp.concatenate([xa, xb], axis=1).astype(BF16)
        a = jnp.dot(x, c1_ref[...], preferred_element_type=F32)
        g = jnp.dot(x, c3_ref[...], preferred_element_type=F32)
        hmid = (a * _sigmoid(a) * g).astype(BF16)
        y = jnp.dot(hmid, c2_ref[...], preferred_element_type=F32)
        half = y.shape[1] // 2
        y_ref[...] = _pack2(y[:, :half], y[:, half:])

    @pl.when(jnp.logical_not(used))
    def _():
        y_ref[...] = jnp.zeros_like(y_ref)


def _experts(xb, blk_e, n_valid, w1, w3, w2):
    P, W = xb.shape
    E, D, DE = w1.shape
    nblk = P // MOE_BLOCK
    grid_spec = pltpu.PrefetchScalarGridSpec(
        num_scalar_prefetch=2,
        grid=(nblk,),
        in_specs=[pl.BlockSpec((MOE_BLOCK, W), lambda i, be, nv: (i, 0)),
                  pl.BlockSpec((1, D, DE), lambda i, be, nv: (be[i], 0, 0)),
                  pl.BlockSpec((1, D, DE), lambda i, be, nv: (be[i], 0, 0)),
                  pl.BlockSpec((1, DE, D), lambda i, be, nv: (be[i], 0, 0))],
        out_specs=pl.BlockSpec((MOE_BLOCK, W), lambda i, be, nv: (i, 0)),
        scratch_shapes=[pltpu.VMEM((D, DE), BF16), pltpu.VMEM((D, DE), BF16), pltpu.VMEM((DE, D), BF16)],
    )
    return pl.pallas_call(
        _expert_kernel,
        out_shape=jax.ShapeDtypeStruct((P, W), F32),
        grid_spec=grid_spec,
        compiler_params=_params(1),
        name="experts",
    )(blk_e, n_valid, xb, w1, w3, w2)


def _sc_gather_rows(table, idx):
    n, width = idx.shape[0], table.shape[1]
    nc, _ = _sc_workers()
    workers, per_worker, chunks = _sc_split(n)
    mesh = plsc.VectorSubcoreMesh(core_axis_name="c", subcore_axis_name="s")

    def body(table_hbm, idx_hbm, out_hbm, idx_v, rows_v, sem):
        wid = lax.axis_index("s") * nc + lax.axis_index("c")
        base = wid * per_worker
        pltpu.sync_copy(idx_hbm.at[wid], idx_v)

        def gather(chunk, buf):
            return pltpu.make_async_copy(table_hbm.at[idx_v.at[chunk]], rows_v.at[buf], sem)

        gather(0, 0).start()

        @pl.loop(0, chunks, step=2)
        def _(c):
            for b in range(2):
                chunk = c + b
                gather(chunk, b).wait()

                @pl.when(chunk + 1 < chunks)
                def _():
                    gather(chunk + 1, 1 - b).start()

                pltpu.sync_copy(rows_v.at[b], out_hbm.at[pl.ds(base + chunk * SC_ROWS, SC_ROWS)])

    return pl.kernel(
        body,
        out_type=jax.ShapeDtypeStruct((n, width), table.dtype),
        mesh=mesh,
        scratch_types=[pltpu.VMEM((chunks, SC_ROWS), jnp.int32),
                       pltpu.VMEM((2, SC_ROWS, width), table.dtype),
                       pltpu.SemaphoreType.DMA],
        name="sc_gather",
    )(table, idx.reshape(workers, chunks, SC_ROWS))


def _combine_dense_kernel(r0_ref, r1_ref, wts_ref, x1_ref, mod_ref, g_ref, b_ref, *rest):
    o_ref = rest[-1]
    w = wts_ref[...]
    y0 = jnp.concatenate(_unpack2(r0_ref[0]), axis=1)
    y1 = jnp.concatenate(_unpack2(r1_ref[0]), axis=1)
    y = w[:, 0:1] * y0 + w[:, 1:2] * y1
    gate2 = mod_ref[0, 5:6, :]
    o_ref[...] = _layer_norm(DN_ALPHA * x1_ref[...] + gate2 * y, g_ref[...], b_ref[...])


def _combine_dense(rows, wts, x1, mod, ln2_g, ln2_b, S, b, out):
    T, D = x1.shape
    tm = min(512, S)
    per_b = S // tm
    here = lambda i: (b * per_b + i, 0)
    in_specs = [pl.BlockSpec((1, tm, rows.shape[2]), lambda i: (0, i, 0)),
                pl.BlockSpec((1, tm, rows.shape[2]), lambda i: (1, i, 0)),
                pl.BlockSpec((tm, LANES), here),
                pl.BlockSpec((tm, D), here),
                pl.BlockSpec((1, 6, D), lambda i: (b, 0, 0)),
                pl.BlockSpec((1, D), lambda i: (0, 0)),
                pl.BlockSpec((1, D), lambda i: (0, 0))]
    args = [rows, rows, wts, x1, mod, ln2_g.reshape(1, D), ln2_b.reshape(1, D)]
    aliases = {}
    if out is not None:
        in_specs.append(pl.BlockSpec(memory_space=pl.ANY))
        aliases = {len(args): 0}
        args.append(out)
    return pl.pallas_call(
        _combine_dense_kernel,
        out_shape=jax.ShapeDtypeStruct((T, D), F32),
        grid=(per_b,),
        in_specs=in_specs,
        out_specs=pl.BlockSpec((tm, D), here),
        input_output_aliases=aliases,
        compiler_params=_params(1),
        name="combine",
    )(*args)


def _moe(h2, x1, route, wts, counts, mod, w1, w3, w2, ln2_g, ln2_b):
    B, S, D = x1.shape
    T = B * S
    P = 2 * T + N_EXPERTS * MOE_BLOCK
    nblk = P // MOE_BLOCK
    sizes = counts[0, ROUTE_OFF:ROUTE_OFF + N_EXPERTS].astype(jnp.int32)
    psizes = (sizes + MOE_BLOCK - 1) // MOE_BLOCK * MOE_BLOCK
    pends = jnp.cumsum(psizes)
    poffs = pends - psizes
    starts = jnp.arange(nblk, dtype=jnp.int32) * MOE_BLOCK
    owns = pends[None, :] <= starts[:, None]
    blk_e = jnp.minimum(jnp.sum(owns.astype(jnp.int32), axis=1), N_EXPERTS - 1)
    in_blk = blk_e[:, None] == jnp.arange(N_EXPERTS, dtype=jnp.int32)[None, :]
    seg_end = jnp.sum(jnp.where(in_blk, (poffs + sizes)[None, :], 0), axis=1)
    n_valid = jnp.clip(seg_end - starts, 0, MOE_BLOCK).astype(jnp.int32)
    r4 = route.reshape(T, LANES)[:, :4]
    sel = r4[:, :2, None] == jnp.arange(N_EXPERTS, dtype=jnp.int32)[None, None, :]
    dest = r4[:, 2:4] + jnp.sum(jnp.where(sel, poffs[None, None, :], 0), axis=-1)
    xb = _sc_scatter_rows(h2.reshape(T, D // 2), dest[:, 0], dest[:, 1], P)
    yb = _experts(xb, blk_e, n_valid, w1, w3, w2)
    out = None
    for b in range(B):
        slot_major = dest[b * S:(b + 1) * S].T.reshape(2 * S)
        rows = _sc_gather_rows(yb, slot_major).reshape(2, S, yb.shape[1])
        out = _combine_dense(rows, wts.reshape(T, LANES), x1.reshape(T, D), mod, ln2_g, ln2_b, S, b, out)
    return out.reshape(B, S, D)


def _layer(x, c, w_ada, b_ada, w_in, conv_w, conv_b, fw1, fb1, ff1, fw2, fb2, ff2, fw3, decay, skip,
           w_hy_o, w_attn_o, attn_sink, w_out, ln1_g, ln1_b, rg_w, rg_b, re_w, re_b, ew1, ew3, ew2,
           ln2_g, ln2_b):
    mod = _ada(c, w_ada, b_ada)
    q, kv, hv, hx1, hx2, ga, gh = _in_proj(x, mod, w_in, conv_w, conv_b)
    attn = _attention(q, kv, attn_sink)
    hy = _hyena(hv, hx1, hx2, fw1, fb1, ff1, fw2, fb2, ff2, fw3, decay, skip)
    x1, h2, route, wts, counts = _merge(attn, hy, ga, gh, x, mod, w_attn_o, w_hy_o, w_out,
                                        ln1_g, ln1_b, rg_w, rg_b, re_w, re_b)
    return _moe(h2, x1, route, wts, counts, mod, ew1, ew3, ew2, ln2_g, ln2_b)


def kernel(x, c, w_ada, b_ada, w_in, conv_w, conv_b, filt_w1, filt_b1, filt_freq1, filt_w2, filt_b2, filt_freq2, filt_w3, filt_decay, hy_skip, w_hy_o, w_attn_o, attn_sink, w_out, ln1_g, ln1_b, router_group_w, router_group_b, router_expert_w, router_expert_b, exp_w1, exp_w3, exp_w2, ln2_g, ln2_b):
    for l in range(w_ada.shape[0]):
        x = _layer(x, c, w_ada[l], b_ada[l], w_in[l], conv_w[l], conv_b[l], filt_w1[l], filt_b1[l],
                   filt_freq1[l], filt_w2[l], filt_b2[l], filt_freq2[l], filt_w3[l], filt_decay[l],
                   hy_skip[l], w_hy_o[l], w_attn_o[l], attn_sink[l], w_out[l], ln1_g[l], ln1_b[l],
                   router_group_w[l], router_group_b[l], router_expert_w[l], router_expert_b[l],
                   exp_w1[l], exp_w3[l], exp_w2[l], ln2_g[l], ln2_b[l])
    return x
```

```python
import functools
import math

import numpy as np
import jax
import jax.numpy as jnp
from jax import lax
from jax.experimental import pallas as pl
from jax.experimental.pallas import tpu as pltpu
from jax.experimental.pallas import tpu_sc as plsc

F32 = jnp.float32
BF16 = jnp.bfloat16

N_HEADS = 8
N_KV_HEADS = 2
HEAD_DIM = 64
ATTN_WIDTH = N_HEADS * HEAD_DIM
KV_WIDTH = N_KV_HEADS * HEAD_DIM
WINDOW = 128
BLOCK_Q = 128
HYENA_WIDTH = 512
FILTER_EMB = 33
FILTER_BANDS = (FILTER_EMB - 1) // 2
WINDOW_SHIFT = 0.05
N_GROUPS = 8
EXPERTS_PER_GROUP = 8
N_EXPERTS = N_GROUPS * EXPERTS_PER_GROUP
D_EXPERT = 512
MOE_BLOCK = 512
LN_EPS = 1e-5
DEPTH = 1
DN_ALPHA = (2.0 * DEPTH) ** 0.25
NEG = -1e30

LANES = 128
SUBLANES = 8
ROUTE_OFF = N_GROUPS
VMEM_LIMIT = 56 * 1024 * 1024


def _params(n_axes, vmem=VMEM_LIMIT):
    return pltpu.CompilerParams(dimension_semantics=("arbitrary",) * n_axes, vmem_limit_bytes=vmem)


def _split(a):
    hi = a.astype(BF16)
    lo = (a - hi.astype(F32)).astype(BF16)
    return hi, lo


def _dot3(a, b_hi, b_lo):
    a_hi, a_lo = _split(a)
    acc = jnp.dot(a_hi, b_hi, preferred_element_type=F32)
    acc = acc + jnp.dot(a_hi, b_lo, preferred_element_type=F32)
    acc = acc + jnp.dot(a_lo, b_hi, preferred_element_type=F32)
    return acc


def _pack2(a, b):
    ia = lax.bitcast_convert_type(a.astype(BF16).astype(F32), jnp.int32)
    ib = lax.bitcast_convert_type(b.astype(BF16).astype(F32), jnp.int32)
    return lax.bitcast_convert_type(ia | lax.shift_right_logical(ib, 16), F32)


def _unpack2(p):
    p = lax.bitcast_convert_type(p, jnp.int32)
    a = lax.bitcast_convert_type(p & jnp.int32(-65536), F32)
    b = lax.bitcast_convert_type(lax.shift_left(p, 16), F32)
    return a, b


def _sigmoid(x):
    return 0.5 * jnp.tanh(0.5 * x) + 0.5


def _layer_norm(r, g, b):
    mu = jnp.mean(r, axis=-1, keepdims=True)
    d = r - mu
    var = jnp.mean(d * d, axis=-1, keepdims=True)
    return d * lax.rsqrt(var + LN_EPS) * g + b


def _ada_kernel(c_ref, wh_ref, wl_ref, b_ref, o_ref):
    c = c_ref[...]
    s = c * _sigmoid(c)
    o_ref[...] = _dot3(s, wh_ref[...], wl_ref[...]) + b_ref[...]


def _ada(c, w_ada, b_ada):
    B, D = c.shape
    n_out = w_ada.shape[1]
    rows = 8
    cp = jnp.zeros((rows, D), F32).at[:B].set(c)
    wh, wl = _split(w_ada)
    tn = 1024
    out = pl.pallas_call(
        _ada_kernel,
        out_shape=jax.ShapeDtypeStruct((rows, n_out), F32),
        grid=(n_out // tn,),
        in_specs=[pl.BlockSpec((rows, D), lambda j: (0, 0)),
                  pl.BlockSpec((D, tn), lambda j: (0, j)),
                  pl.BlockSpec((D, tn), lambda j: (0, j)),
                  pl.BlockSpec((1, tn), lambda j: (0, j))],
        out_specs=pl.BlockSpec((rows, tn), lambda j: (0, j)),
        compiler_params=_params(1),
        name="ada",
    )(cp, wh, wl, b_ada.reshape(1, n_out))
    return out[:B].reshape(B, 6, D)


def _inproj_kernel(x_ref, xp_ref, xn_ref, mod_ref, w_ref, cw_ref, cb_ref,
                   q_ref, kv_ref, v_ref, x1_ref, x2_ref, ga_ref, gh_ref):
    i = pl.program_id(1)
    n = pl.num_programs(1)
    C = HYENA_WIDTH
    x = x_ref[0]
    tm, D = x.shape
    shift = mod_ref[0, 0:1, :]
    scale = mod_ref[0, 1:2, :]
    h = (x * (1.0 + scale) + shift).astype(BF16)

    def seg(lo, hi):
        return jnp.dot(h, w_ref[:, lo:hi], preferred_element_type=F32)

    o_q = 0
    o_kv = o_q + ATTN_WIDTH
    o_hy = o_kv + 2 * KV_WIDTH
    o_ga = o_hy + 3 * C
    o_gh = o_ga + D
    ga_ref[0] = _sigmoid(seg(o_ga, o_ga + D)).astype(BF16)
    gh_ref[0] = _sigmoid(seg(o_gh, o_gh + D)).astype(BF16)

    u = seg(o_hy, o_hy + 3 * C)
    xe = jnp.concatenate([xp_ref[0], xn_ref[0]], axis=0)
    he = (xe * (1.0 + scale) + shift).astype(BF16)
    ue = jnp.dot(he, w_ref[:, o_hy:o_hy + 3 * C], preferred_element_type=F32)
    prow = jnp.where(i > 0, ue[SUBLANES - 1:SUBLANES], 0.0)
    nrow = jnp.where(i < n - 1, ue[SUBLANES:SUBLANES + 1], 0.0)
    rid = lax.broadcasted_iota(jnp.int32, (tm, 1), 0)
    up = jnp.where(rid == 0, prow, pltpu.roll(u, 1, 0))
    dn = jnp.where(rid == tm - 1, nrow, pltpu.roll(u, tm - 1, 0))
    conv = cw_ref[0:1, :] * up + cw_ref[1:2, :] * u + cw_ref[2:3, :] * dn + cb_ref[...]
    v_ref[0] = conv[:, :C]
    x1_ref[0] = conv[:, C:2 * C]
    x2_ref[0] = conv[:, 2 * C:]

    q_ref[0] = (seg(o_q, o_q + ATTN_WIDTH) * (HEAD_DIM ** -0.5)).astype(BF16)
    kv_ref[0] = seg(o_kv, o_kv + 2 * KV_WIDTH).astype(BF16)


def _in_proj(x, mod, w_in, conv_w, conv_b):
    B, S, D = x.shape
    C = HYENA_WIDTH
    tm = min(512, S)
    r8 = tm // SUBLANES
    nb8 = S // SUBLANES
    wb = w_in.astype(BF16)
    nw = wb.shape[1]
    row = lambda b, i: (b, i, 0)
    shapes = [(ATTN_WIDTH, BF16), (2 * KV_WIDTH, BF16), (C, F32), (C, F32), (C, F32), (D, BF16), (D, BF16)]
    return pl.pallas_call(
        _inproj_kernel,
        out_shape=[jax.ShapeDtypeStruct((B, S, w), dt) for w, dt in shapes],
        grid=(B, S // tm),
        in_specs=[pl.BlockSpec((1, tm, D), row),
                  pl.BlockSpec((1, SUBLANES, D), lambda b, i: (b, jnp.maximum(i * r8 - 1, 0), 0)),
                  pl.BlockSpec((1, SUBLANES, D), lambda b, i: (b, jnp.minimum((i + 1) * r8, nb8 - 1), 0)),
                  pl.BlockSpec((1, 6, D), lambda b, i: (b, 0, 0)),
                  pl.BlockSpec((D, nw), lambda b, i: (0, 0)),
                  pl.BlockSpec((3, 3 * C), lambda b, i: (0, 0)),
                  pl.BlockSpec((1, 3 * C), lambda b, i: (0, 0))],
        out_specs=[pl.BlockSpec((1, tm, w), row) for w, _ in shapes],
        compiler_params=_params(2),
        name="in_proj",
    )(x, x, x, mod, wb, conv_w.astype(F32), conv_b.reshape(1, 3 * C).astype(F32))


ATT_TQ = 512


def _attn_kernel(sink_ref, q_ref, kvp_ref, kvc_ref, kvn_ref, bias_ref, o_ref, kv_scr, *, seq_len):
    i = pl.program_id(1)
    Q = BLOCK_Q
    TQ = q_ref.shape[1]
    G = N_HEADS // N_KV_HEADS
    kv_scr[0:Q] = kvp_ref[0]
    kv_scr[Q:Q + TQ] = kvc_ref[0]
    kv_scr[Q + TQ:] = kvn_ref[0]
    col = lax.broadcasted_iota(jnp.int32, (1, 3 * Q), 1)
    rhead = lax.broadcasted_iota(jnp.int32, (G * Q, 1), 0) // Q
    for j in range(TQ // Q):
        kpos = i * TQ + (j - 1) * Q + col
        colbias = jnp.where((kpos >= 0) & (kpos < seq_len), 0.0, NEG)
        for kv in range(N_KV_HEADS):
            kk = kv_scr[j * Q:(j + 3) * Q, kv * HEAD_DIM:(kv + 1) * HEAD_DIM]
            vv = kv_scr[j * Q:(j + 3) * Q, KV_WIDTH + kv * HEAD_DIM:KV_WIDTH + (kv + 1) * HEAD_DIM]
            heads = [kv * G + g for g in range(G)]
            qg = jnp.concatenate([q_ref[0, j * Q:(j + 1) * Q, h * HEAD_DIM:(h + 1) * HEAD_DIM] for h in heads], axis=0)
            s = lax.dot_general(qg, kk, (((1,), (1,)), ((), ())), preferred_element_type=F32)
            s = s + bias_ref[kv] + colbias
            snk = jnp.where(rhead == 0, sink_ref[heads[0]],
                            jnp.where(rhead == 1, sink_ref[heads[1]],
                                      jnp.where(rhead == 2, sink_ref[heads[2]], sink_ref[heads[3]])))
            m = jnp.maximum(jnp.max(s, axis=1, keepdims=True), snk)
            p = jnp.exp(s - m)
            den = jnp.sum(p, axis=1, keepdims=True) + jnp.exp(snk - m)
            o = jnp.dot(p.astype(BF16), vv, preferred_element_type=F32) / den
            for g, h in enumerate(heads):
                o_ref[0, j * Q:(j + 1) * Q, h * HEAD_DIM:(h + 1) * HEAD_DIM] = o[g * Q:(g + 1) * Q].astype(BF16)


def _attention(q, kv, sink):
    B, S, _ = q.shape
    Q = BLOCK_Q
    TQ = min(ATT_TQ, S)
    r = TQ // Q
    nq = S // Q
    G = N_HEADS // N_KV_HEADS
    assert G == 4
    a = jnp.arange(Q)[:, None]
    j = jnp.arange(3 * Q)[None, :]
    rel = jnp.abs(j - Q - a).astype(F32)
    slopes = 2.0 ** (-8.0 * jnp.arange(1, N_HEADS + 1, dtype=F32) / N_HEADS)
    bias = jnp.where(rel[None] <= WINDOW, -slopes[:, None, None] * rel[None], NEG).astype(F32)
    bias = bias.reshape(N_KV_HEADS, G * Q, 3 * Q)
    cur = lambda b, i: (b, i, 0)
    return pl.pallas_call(
        functools.partial(_attn_kernel, seq_len=S),
        out_shape=jax.ShapeDtypeStruct((B, S, ATTN_WIDTH), BF16),
        grid=(B, S // TQ),
        in_specs=[pl.BlockSpec(memory_space=pltpu.SMEM),
                  pl.BlockSpec((1, TQ, ATTN_WIDTH), cur),
                  pl.BlockSpec((1, Q, 2 * KV_WIDTH), lambda b, i: (b, jnp.maximum(i * r - 1, 0), 0)),
                  pl.BlockSpec((1, TQ, 2 * KV_WIDTH), cur),
                  pl.BlockSpec((1, Q, 2 * KV_WIDTH), lambda b, i: (b, jnp.minimum((i + 1) * r, nq - 1), 0)),
                  pl.BlockSpec((N_KV_HEADS, G * Q, 3 * Q), lambda b, i: (0, 0, 0))],
        out_specs=pl.BlockSpec((1, TQ, ATTN_WIDTH), cur),
        scratch_shapes=[pltpu.VMEM((TQ + 2 * Q, 2 * KV_WIDTH), BF16)],
        compiler_params=_params(2),
        name="attn",
    )(sink.astype(F32), q, kv, kv, kv, bias)


def _filter_kernel(z_ref, w1h, w1l, b1_ref, f1_ref, w2h, w2l, b2_ref, f2_ref, w3h, w3l, dec_ref,
                   k_ref, s_ref):
    i = pl.program_id(0)
    z = z_ref[...]
    h1 = jnp.sin(f1_ref[...] * (_dot3(z, w1h[...], w1l[...]) + b1_ref[...]))
    h2 = jnp.sin(f2_ref[...] * (_dot3(h1, w2h[...], w2l[...]) + b2_ref[...]))
    k = _dot3(h2, w3h[...], w3l[...])
    t = z[:, 0:1]
    k = k * (jnp.exp(-t * jnp.abs(dec_ref[...])) + WINDOW_SHIFT)
    k_ref[...] = k

    @pl.when(i == 0)
    def _():
        s_ref[...] = jnp.zeros_like(s_ref)

    s_ref[...] += jnp.sum(jnp.abs(k), axis=0, keepdims=True)


def _filter_embedding(L):
    t = np.linspace(0.0, 1.0, L, dtype=np.float32).astype(np.float64)[:, None]
    w = (2.0 * math.pi * np.arange(L, dtype=np.float32) / np.float32(L)).astype(np.float64)[:, None]
    bands = np.linspace(1e-4, FILTER_BANDS - 1, FILTER_BANDS, dtype=np.float32).astype(np.float64)[None, :]
    bw = (bands.astype(np.float32) * w.astype(np.float32)).astype(np.float64)
    z = np.concatenate([t, np.cos(bw), -np.sin(bw)], axis=-1)
    zp = np.zeros((L, LANES), np.float32)
    zp[:, :FILTER_EMB] = z.astype(np.float32)
    return jnp.asarray(zp)


def _pad2(a, r, c):
    return jnp.zeros((r, c), F32).at[:a.shape[0], :a.shape[1]].set(a.astype(F32))


def _filters(L, fw1, fb1, ff1, fw2, fb2, ff2, fw3, decay):
    H = LANES
    nf = fw3.shape[1]
    z = _filter_embedding(L)
    w1h, w1l = _split(_pad2(fw1, H, H))
    w2h, w2l = _split(_pad2(fw2, H, H))
    w3h, w3l = _split(_pad2(fw3, H, nf))
    b1 = _pad2(fb1[None], 1, H)
    f1 = _pad2(ff1[None], 1, H)
    b2 = _pad2(fb2[None], 1, H)
    f2 = _pad2(ff2[None], 1, H)
    tr = min(512, L)
    full = lambda r, c: pl.BlockSpec((r, c), lambda i: (0, 0))
    return pl.pallas_call(
        _filter_kernel,
        out_shape=[jax.ShapeDtypeStruct((L, nf), F32), jax.ShapeDtypeStruct((1, nf), F32)],
        grid=(L // tr,),
        in_specs=[pl.BlockSpec((tr, H), lambda i: (i, 0)),
                  full(H, H), full(H, H), full(1, H), full(1, H),
                  full(H, H), full(H, H), full(1, H), full(1, H),
                  full(H, nf), full(H, nf), full(1, nf)],
        out_specs=[pl.BlockSpec((tr, nf), lambda i: (i, 0)), full(1, nf)],
        compiler_params=_params(1),
        name="filter",
    )(z, w1h, w1l, b1, f1, w2h, w2l, b2, f2, w3h, w3l, decay.reshape(1, nf).astype(F32))


def _np_bf16(m64):
    return jnp.asarray(m64.astype(np.float32).astype(BF16))


def _dft_constants(L):
    N = 2 * L
    n2 = LANES
    n1 = N // n2
    h1 = n1 // 2
    k1 = np.arange(n1)[:, None]
    s1 = np.arange(h1)[None, :]
    ang = -2.0 * np.pi * ((k1 * s1) % n1) / n1
    wr, wi = np.cos(ang), np.sin(ang)
    w1_filt = np.block([[wr, wr], [wi, wi], [wr, -wr], [wi, -wi]])
    w1_cplx = np.block([[wr, -wi], [wi, wr]])
    vr, vi = wr.T / N, -wi.T / N
    w3 = np.block([[vr, -vi], [vi, vr]])
    k2 = np.arange(n2)[:, None]
    s2 = np.arange(n2)[None, :]
    a2 = -2.0 * np.pi * ((k2 * s2) % n2) / n2
    w2r, w2i = jnp.asarray(np.cos(a2), F32), jnp.asarray(np.sin(a2), F32)
    at = -2.0 * np.pi * ((np.arange(n1)[:, None] * s2) % N) / N
    twr, twi = jnp.asarray(np.cos(at), F32), jnp.asarray(np.sin(at), F32)
    mr = w2r[None] * twr[:, None, :] - w2i[None] * twi[:, None, :]
    mi = w2r[None] * twi[:, None, :] + w2i[None] * twr[:, None, :]
    fwd = jnp.concatenate([jnp.concatenate([mr, -mi], axis=2),
                           jnp.concatenate([mi, mr], axis=2)], axis=1)
    fwd = fwd.astype(BF16)
    return dict(n1=n1, w1_filt=_np_bf16(w1_filt), w1_cplx=_np_bf16(w1_cplx), w3=_np_bf16(w3),
                fwd=fwd)


SCH = 8


def _dft1_kernel(x_ref, w_ref, a_ref, *, n1):
    w = w_ref[...]
    for j in range(SCH):
        rhs = jnp.concatenate([x_ref[0, 0, :, j, :], x_ref[0, 1, :, j, :]], axis=0)
        res = jnp.dot(w, rhs.astype(BF16), preferred_element_type=F32)
        a_ref[0, :, j, :] = _pack2(res[:n1], res[n1:])


def _dft1_data(x, consts):
    B, L, C = x.shape
    n1 = consts["n1"]
    h1 = n1 // 2
    xv = x.reshape(B // 2, 2, h1, LANES, C)
    return pl.pallas_call(
        functools.partial(_dft1_kernel, n1=n1),
        out_shape=jax.ShapeDtypeStruct((B // 2, n1, LANES, C), F32),
        grid=(B // 2, LANES // SCH),
        in_specs=[pl.BlockSpec((1, 2, h1, SCH, C), lambda p, j: (p, 0, 0, j, 0)),
                  pl.BlockSpec((2 * n1, n1), lambda p, j: (0, 0))],
        out_specs=pl.BlockSpec((1, n1, SCH, C), lambda p, j: (p, 0, j, 0)),
        compiler_params=_params(2),
        name="dft1",
    )(xv, consts["w1_cplx"])


def _dft1f_kernel(x_ref, w_ref, a_ref, *, n1):
    C = HYENA_WIDTH
    w = w_ref[...]
    for j in range(SCH):
        rhs = jnp.concatenate([x_ref[:, j, :C], x_ref[:, j, C:]], axis=0)
        res = jnp.dot(w, rhs.astype(BF16), preferred_element_type=F32)
        a_ref[0, :, 0, j, :] = _pack2(res[:n1], res[n1:2 * n1])
        a_ref[0, :, 1, j, :] = _pack2(res[2 * n1:3 * n1], res[3 * n1:])


def _dft1_filter(kraw, consts):
    L, nf = kraw.shape
    C = HYENA_WIDTH
    n_ord = nf // (2 * C)
    n1 = consts["n1"]
    h1 = n1 // 2
    kv = kraw.reshape(h1, LANES, nf)
    return pl.pallas_call(
        functools.partial(_dft1f_kernel, n1=n1),
        out_shape=jax.ShapeDtypeStruct((n_ord, n1, 2, LANES, C), F32),
        grid=(n_ord, LANES // SCH),
        in_specs=[pl.BlockSpec((h1, SCH, 2 * C), lambda o, j: (0, j, o)),
                  pl.BlockSpec((4 * n1, n1), lambda o, j: (0, 0))],
        out_specs=pl.BlockSpec((1, n1, 2, SCH, C), lambda o, j: (o, 0, 0, j, 0)),
        compiler_params=_params(2),
        name="dft1f",
    )(kv, consts["w1_filt"])


KCH = 8


def _midf_kernel(a_ref, f_ref, inv_ref, b0_ref, h_ref):
    n2 = LANES
    sc = inv_ref[0]
    for k in range(KCH):
        p = jnp.concatenate(_unpack2(a_ref[0, k, :n2, :]), axis=0).astype(BF16)
        q = jnp.concatenate(_unpack2(a_ref[0, k, n2:, :]), axis=0).astype(BF16)
        h_re = jnp.dot(f_ref[k, :n2, :], p, preferred_element_type=F32)
        h_im = jnp.dot(f_ref[k, n2:, :], q, preferred_element_type=F32)
        h_ref[0, k] = _pack2((h_re - b0_ref[0]) * sc, h_im * sc)


def _filter_spectrum(af, inv_den, bwd0, consts):
    n_ord, n1, _, n2, C = af.shape
    a = af.reshape(n_ord, n1, 2 * n2, C)
    tab = pl.BlockSpec((KCH, 2 * n2, 2 * n2), lambda k, o: (k, 0, 0))
    vec = pl.BlockSpec((1, 1, C), lambda k, o: (o, 0, 0))
    return pl.pallas_call(
        _midf_kernel,
        out_shape=jax.ShapeDtypeStruct((n_ord, n1, n2, C), F32),
        grid=(n1 // KCH, n_ord),
        in_specs=[pl.BlockSpec((1, KCH, 2 * n2, C), lambda k, o: (o, k, 0, 0)), tab, vec, vec],
        out_specs=pl.BlockSpec((1, KCH, n2, C), lambda k, o: (o, k, 0, 0)),
        compiler_params=_params(2),
        name="midf",
    )(a, consts["fwd"], inv_den, bwd0)


def _mid_kernel(a_ref, f_ref, h_ref, b_ref):
    n2 = LANES
    for k in range(KCH):
        a = jnp.concatenate(_unpack2(a_ref[0, k]), axis=0).astype(BF16)
        x = jnp.dot(f_ref[k], a, preferred_element_type=F32)
        xr, xi = x[:n2], x[n2:]
        hr, hi = _unpack2(h_ref[0, k])
        y = jnp.concatenate([xr * hr - xi * hi, xr * hi + xi * hr], axis=0)
        b = lax.dot_general(f_ref[k], y.astype(BF16), (((0,), (0,)), ((), ())), preferred_element_type=F32)
        b_ref[0, k] = _pack2(b[:n2], b[n2:])


def _mid(a, hspec, order, consts):
    P, n1, n2, C = a.shape
    tab = pl.BlockSpec((KCH, 2 * n2, 2 * n2), lambda k, p: (k, 0, 0))
    return pl.pallas_call(
        _mid_kernel,
        out_shape=jax.ShapeDtypeStruct((P, n1, n2, C), F32),
        grid=(n1 // KCH, P),
        in_specs=[pl.BlockSpec((1, KCH, n2, C), lambda k, p: (p, k, 0, 0)),
                  tab,
                  pl.BlockSpec((1, KCH, n2, C), lambda k, p: (order, k, 0, 0))],
        out_specs=pl.BlockSpec((1, KCH, n2, C), lambda k, p: (p, k, 0, 0)),
        compiler_params=_params(2),
        name="mid",
    )(a, consts["fwd"], hspec)


def _dft3_kernel(b_ref, w_ref, v_ref, g_ref, skip_ref, z_ref, slab_ref, *, h1):
    w = w_ref[...]
    skip = skip_ref[0]
    for j in range(SCH):
        slab_ref[...] = b_ref[0, :, j, :]
        rhs = jnp.concatenate(_unpack2(slab_ref[...]), axis=0)
        y = jnp.dot(w, rhs.astype(BF16), preferred_element_type=F32)
        for r in range(2):
            yr = y[r * h1:(r + 1) * h1]
            z_ref[0, r, :, j, :] = g_ref[0, r, :, j, :] * (yr + v_ref[0, r, :, j, :] * skip)


def _dft3_gate(b5, v, gate, skip, consts):
    P, n1, n2, C = b5.shape
    h1 = n1 // 2
    B, L, _ = v.shape
    five = lambda t: t.reshape(P, 2, h1, n2, C)
    dat = pl.BlockSpec((1, 2, h1, SCH, C), lambda p, j: (p, 0, 0, j, 0))
    out = pl.pallas_call(
        functools.partial(_dft3_kernel, h1=h1),
        out_shape=jax.ShapeDtypeStruct((P, 2, h1, n2, C), F32),
        grid=(P, n2 // SCH),
        in_specs=[pl.BlockSpec((1, n1, SCH, C), lambda p, j: (p, 0, j, 0)),
                  pl.BlockSpec((n1, 2 * n1), lambda p, j: (0, 0)),
                  dat, dat,
                  pl.BlockSpec((1, C), lambda p, j: (0, 0))],
        out_specs=dat,
        scratch_shapes=[pltpu.VMEM((n1, C), F32)],
        compiler_params=_params(2),
        name="dft3",
    )(b5, consts["w3"], five(v), five(gate), skip.reshape(1, C).astype(F32))
    return out.reshape(B, L, C)


def _hyena(v, x1, x2, fw1, fb1, ff1, fw2, fb2, ff2, fw3, decay, skip):
    B, L, C = v.shape
    consts = _dft_constants(L)
    kraw, ksum = _filters(L, fw1, fb1, ff1, fw2, fb2, ff2, fw3, decay)
    ks = ksum.reshape(2, 2, C)
    inv_den = (1.0 / (ks[:, 0] + ks[:, 1])).reshape(2, 1, C)
    bwd0 = kraw[0].reshape(2, 2, C)[:, 1].reshape(2, 1, C)
    hspec = _filter_spectrum(_dft1_filter(kraw, consts), inv_den, bwd0, consts)
    z = v
    for o, gate in enumerate((x1, x2)):
        a5 = _dft1_data(z, consts)
        b5 = _mid(a5, hspec, o, consts)
        z = _dft3_gate(b5, z, gate, skip[o], consts)
    return z


MERGE_SPLIT = 1

def _merge_kernel(attn_ref, hy_ref, ga_ref, gh_ref, x_ref, mod_ref, wa_ref, wh_ref, wo_ref,
                  g1_ref, b1_ref, rwh_ref, rwl_ref, rb_ref, tri_ref,
                  x1_ref, h2_ref, route_ref, wts_ref, cnt_ref, carry_ref):
    first = (pl.program_id(0) == 0) & (pl.program_id(1) == 0)

    @pl.when(first)
    def _():
        carry_ref[...] = jnp.zeros_like(carry_ref)

    sub = tri_ref.shape[0]
    for part in range(x_ref.shape[1] // sub):
        _merge_rows(slice(part * sub, (part + 1) * sub), attn_ref, hy_ref, ga_ref, gh_ref, x_ref, mod_ref,
                    wa_ref, wh_ref, wo_ref, g1_ref, b1_ref, rwh_ref, rwl_ref, rb_ref, tri_ref,
                    x1_ref, h2_ref, route_ref, wts_ref, carry_ref)
    cnt_ref[...] = carry_ref[...]


def _merge_rows(rows, attn_ref, hy_ref, ga_ref, gh_ref, x_ref, mod_ref, wa_ref, wh_ref, wo_ref,
                g1_ref, b1_ref, rwh_ref, rwl_ref, rb_ref, tri_ref, x1_ref, h2_ref, route_ref, wts_ref, carry_ref):
    a = jnp.dot(attn_ref[0, rows, :], wa_ref[...], preferred_element_type=F32)
    hy = jnp.dot(hy_ref[0, rows, :].astype(BF16), wh_ref[...], preferred_element_type=F32)
    merged = ga_ref[0, rows, :].astype(F32) * a + gh_ref[0, rows, :].astype(F32) * hy
    y = jnp.dot(merged.astype(BF16), wo_ref[...], preferred_element_type=F32)
    gate1 = mod_ref[0, 2:3, :]
    shift2 = mod_ref[0, 3:4, :]
    scale2 = mod_ref[0, 4:5, :]
    x1 = _layer_norm(DN_ALPHA * x_ref[0, rows, :] + gate1 * y, g1_ref[...], b1_ref[...])
    x1_ref[0, rows, :] = x1
    h2 = x1 * (1.0 + scale2) + shift2
    half = h2.shape[1] // 2
    h2_ref[0, rows, :] = _pack2(h2[:, :half], h2[:, half:])

    logits = _dot3(h2, rwh_ref[...], rwl_ref[...]) + rb_ref[...]
    tm = logits.shape[0]
    lane = lax.broadcasted_iota(jnp.int32, (tm, LANES), 1)
    lanef = lane.astype(F32)
    big = float(LANES)

    def first_lane(mask):
        return jnp.min(jnp.where(mask, lanef, big), axis=1, keepdims=True).astype(jnp.int32)

    gmask = lane < N_GROUPS
    gl = jnp.where(gmask, logits, NEG)
    gmax = jnp.max(gl, axis=1, keepdims=True)
    gidx = first_lane(gl == gmax)
    pg = 1.0 / jnp.sum(jnp.exp(gl - gmax), axis=1, keepdims=True)
    lo = ROUTE_OFF + gidx * EXPERTS_PER_GROUP
    emask = (lane >= lo) & (lane < lo + EXPERTS_PER_GROUP)
    el = jnp.where(emask, logits, NEG)
    v1 = jnp.max(el, axis=1, keepdims=True)
    i1 = first_lane(el == v1)
    el2 = jnp.where(emask & (lane != i1), logits, NEG)
    v2 = jnp.max(el2, axis=1, keepdims=True)
    i2 = first_lane(el2 == v2)
    e21 = jnp.exp(v2 - v1)
    w1 = pg / (1.0 + e21)
    w2 = pg * e21 / (1.0 + e21)

    sel1 = lane == i1
    sel2 = lane == i2
    onehot = jnp.where(sel1 | sel2, 1.0, 0.0)
    prefix = jnp.dot(tri_ref[...], onehot.astype(BF16), preferred_element_type=F32) + carry_ref[...]
    r1 = jnp.sum(jnp.where(sel1, prefix, 0.0), axis=1, keepdims=True)
    r2 = jnp.sum(jnp.where(sel2, prefix, 0.0), axis=1, keepdims=True)
    carry_ref[...] += jnp.sum(onehot, axis=0, keepdims=True)

    ranks = jnp.where(lane == 2, r1, jnp.where(lane == 3, r2, 0.0)).astype(jnp.int32)
    route_ref[0, rows, :] = jnp.where(lane == 0, i1 - ROUTE_OFF, jnp.where(lane == 1, i2 - ROUTE_OFF, ranks))
    wts_ref[0, rows, :] = jnp.where(lane == 0, w1, jnp.where(lane == 1, w2, 0.0))


def _merge(attn, hy, ga, gh, x, mod, w_attn_o, w_hy_o, w_out, ln1_g, ln1_b, rg_w, rg_b, re_w, re_b):
    B, S, D = x.shape
    tm = min(512, S)
    rw = jnp.zeros((D, LANES), F32).at[:, :N_GROUPS].set(rg_w).at[:, ROUTE_OFF:ROUTE_OFF + N_EXPERTS].set(re_w)
    rb = jnp.zeros((1, LANES), F32).at[0, :N_GROUPS].set(rg_b).at[0, ROUTE_OFF:ROUTE_OFF + N_EXPERTS].set(re_b)
    rwh, rwl = _split(rw)
    sub = tm // MERGE_SPLIT
    tri = (jnp.arange(sub)[:, None] > jnp.arange(sub)[None, :]).astype(BF16)
    row = lambda b, i: (b, i, 0)
    full = lambda r, c: pl.BlockSpec((r, c), lambda b, i: (0, 0))
    outs = [jax.ShapeDtypeStruct((B, S, D), F32), jax.ShapeDtypeStruct((B, S, D // 2), F32),
            jax.ShapeDtypeStruct((B, S, LANES), jnp.int32), jax.ShapeDtypeStruct((B, S, LANES), F32),
            jax.ShapeDtypeStruct((1, LANES), F32)]
    return pl.pallas_call(
        _merge_kernel,
        out_shape=outs,
        grid=(B, S // tm),
        in_specs=[pl.BlockSpec((1, tm, ATTN_WIDTH), row), pl.BlockSpec((1, tm, HYENA_WIDTH), row),
                  pl.BlockSpec((1, tm, D), row), pl.BlockSpec((1, tm, D), row), pl.BlockSpec((1, tm, D), row),
                  pl.BlockSpec((1, 6, D), lambda b, i: (b, 0, 0)),
                  full(ATTN_WIDTH, D), full(HYENA_WIDTH, D), full(D, D),
                  full(1, D), full(1, D), full(D, LANES), full(D, LANES), full(1, LANES), full(sub, sub)],
        out_specs=[pl.BlockSpec((1, tm, D), row), pl.BlockSpec((1, tm, D // 2), row),
                   pl.BlockSpec((1, tm, LANES), row), pl.BlockSpec((1, tm, LANES), row), full(1, LANES)],
        scratch_shapes=[pltpu.VMEM((1, LANES), F32)],
        compiler_params=_params(2),
        name="merge",
    )(attn, hy, ga, gh, x, mod, w_attn_o.astype(BF16), w_hy_o.astype(BF16), w_out.astype(BF16),
      ln1_g.reshape(1, D), ln1_b.reshape(1, D), rwh, rwl, rb, tri)


SC_ROWS = 64


def _sc_workers():
    info = plsc.get_sparse_core_info()
    return info.num_cores, info.num_cores * info.num_subcores


def _sc_split(n):
    _, workers = _sc_workers()
    per_worker = n // workers
    chunks = per_worker // SC_ROWS
    assert per_worker * workers == n and chunks * SC_ROWS == per_worker and chunks % 2 == 0
    return workers, per_worker, chunks


def _sc_scatter_rows(src, idx0, idx1, n_out):
    n, width = src.shape
    nc, _ = _sc_workers()
    workers, per_worker, chunks = _sc_split(n)
    mesh = plsc.VectorSubcoreMesh(core_axis_name="c", subcore_axis_name="s")

    def body(src_hbm, i0_hbm, i1_hbm, out_hbm, i0_v, i1_v, rows_v, sem):
        wid = lax.axis_index("s") * nc + lax.axis_index("c")
        base = wid * per_worker
        pltpu.sync_copy(i0_hbm.at[wid], i0_v)
        pltpu.sync_copy(i1_hbm.at[wid], i1_v)

        def load(chunk, buf):
            return pltpu.make_async_copy(src_hbm.at[pl.ds(base + chunk * SC_ROWS, SC_ROWS)], rows_v.at[buf], sem)

        load(0, 0).start()

        @pl.loop(0, chunks, step=2)
        def _(c):
            for b in range(2):
                chunk = c + b
                load(chunk, b).wait()

                @pl.when(chunk + 1 < chunks)
                def _():
                    load(chunk + 1, 1 - b).start()

                pltpu.sync_copy(rows_v.at[b], out_hbm.at[i0_v.at[chunk]])
                pltpu.sync_copy(rows_v.at[b], out_hbm.at[i1_v.at[chunk]])

    shaped = lambda i: i.reshape(workers, chunks, SC_ROWS)
    return pl.kernel(
        body,
        out_type=jax.ShapeDtypeStruct((n_out, width), src.dtype),
        mesh=mesh,
        scratch_types=[pltpu.VMEM((chunks, SC_ROWS), jnp.int32),
                       pltpu.VMEM((chunks, SC_ROWS), jnp.int32),
                       pltpu.VMEM((2, SC_ROWS, width), src.dtype),
                       pltpu.SemaphoreType.DMA],
        name="sc_scatter",
    )(src, shaped(idx0), shaped(idx1))


def _expert_kernel(be_ref, nv_ref, x_ref, w1_ref, w3_ref, w2_ref, y_ref, c1_ref, c3_ref, c2_ref):
    i = pl.program_id(0)
    n_valid = nv_ref[i]
    used = n_valid > 0
    fresh = (i == 0) | (be_ref[i] != be_ref[jnp.maximum(i - 1, 0)])

    @pl.when(used & fresh)
    def _():
        c1_ref[...] = w1_ref[0].astype(BF16)
        c3_ref[...] = w3_ref[0].astype(BF16)
        c2_ref[...] = w2_ref[0].astype(BF16)

    @pl.when(used)
    def _():
        rid = lax.broadcasted_iota(jnp.int32, (x_ref.shape[0], 1), 0)
        xa, xb = _unpack2(jnp.where(rid < n_valid, x_ref[...], 0.0))
        x = jnp.concatenate([xa, xb], axis=1).astype(BF16)
        a = jnp.dot(x, c1_ref[...], preferred_element_type=F32)
        g = jnp.dot(x, c3_ref[...], preferred_element_type=F32)
        hmid = (a * _sigmoid(a) * g).astype(BF16)
        y = jnp.dot(hmid, c2_ref[...], preferred_element_type=F32)
        half = y.shape[1] // 2
        y_ref[...] = _pack2(y[:, :half], y[:, half:])

    @pl.when(jnp.logical_not(used))
    def _():
        y_ref[...] = jnp.zeros_like(y_ref)


def _experts(xb, blk_e, n_valid, w1, w3, w2):
    P, W = xb.shape
    E, D, DE = w1.shape
    nblk = P // MOE_BLOCK
    grid_spec = pltpu.PrefetchScalarGridSpec(
        num_scalar_prefetch=2,
        grid=(nblk,),
        in_specs=[pl.BlockSpec((MOE_BLOCK, W), lambda i, be, nv: (i, 0)),
                  pl.BlockSpec((1, D, DE), lambda i, be, nv: (be[i], 0, 0)),
                  pl.BlockSpec((1, D, DE), lambda i, be, nv: (be[i], 0, 0)),
                  pl.BlockSpec((1, DE, D), lambda i, be, nv: (be[i], 0, 0))],
        out_specs=pl.BlockSpec((MOE_BLOCK, W), lambda i, be, nv: (i, 0)),
        scratch_shapes=[pltpu.VMEM((D, DE), BF16), pltpu.VMEM((D, DE), BF16), pltpu.VMEM((DE, D), BF16)],
    )
    return pl.pallas_call(
        _expert_kernel,
        out_shape=jax.ShapeDtypeStruct((P, W), F32),
        grid_spec=grid_spec,
        compiler_params=_params(1),
        name="experts",
    )(blk_e, n_valid, xb, w1, w3, w2)


def _sc_gather_rows(table, idx):
    n, width = idx.shape[0], table.shape[1]
    nc, _ = _sc_workers()
    workers, per_worker, chunks = _sc_split(n)
    mesh = plsc.VectorSubcoreMesh(core_axis_name="c", subcore_axis_name="s")

    def body(table_hbm, idx_hbm, out_hbm, idx_v, rows_v, sem):
        wid = lax.axis_index("s") * nc + lax.axis_index("c")
        base = wid * per_worker
        pltpu.sync_copy(idx_hbm.at[wid], idx_v)

        def gather(chunk, buf):
            return pltpu.make_async_copy(table_hbm.at[idx_v.at[chunk]], rows_v.at[buf], sem)

        gather(0, 0).start()

        @pl.loop(0, chunks, step=2)
        def _(c):
            for b in range(2):
                chunk = c + b
                gather(chunk, b).wait()

                @pl.when(chunk + 1 < chunks)
                def _():
                    gather(chunk + 1, 1 - b).start()

                pltpu.sync_copy(rows_v.at[b], out_hbm.at[pl.ds(base + chunk * SC_ROWS, SC_ROWS)])

    return pl.kernel(
        body,
        out_type=jax.ShapeDtypeStruct((n, width), table.dtype),
        mesh=mesh,
        scratch_types=[pltpu.VMEM((chunks, SC_ROWS), jnp.int32),
                       pltpu.VMEM((2, SC_ROWS, width), table.dtype),
                       pltpu.SemaphoreType.DMA],
        name="sc_gather",
    )(table, idx.reshape(workers, chunks, SC_ROWS))


def _combine_dense_kernel(r0_ref, r1_ref, wts_ref, x1_ref, mod_ref, g_ref, b_ref, *rest):
    o_ref = rest[-1]
    w = wts_ref[...]
    y0 = jnp.concatenate(_unpack2(r0_ref[0]), axis=1)
    y1 = jnp.concatenate(_unpack2(r1_ref[0]), axis=1)
    y = w[:, 0:1] * y0 + w[:, 1:2] * y1
    gate2 = mod_ref[0, 5:6, :]
    o_ref[...] = _layer_norm(DN_ALPHA * x1_ref[...] + gate2 * y, g_ref[...], b_ref[...])


def _combine_dense(rows, wts, x1, mod, ln2_g, ln2_b, S, b, out):
    T, D = x1.shape
    tm = min(512, S)
    per_b = S // tm
    here = lambda i: (b * per_b + i, 0)
    in_specs = [pl.BlockSpec((1, tm, rows.shape[2]), lambda i: (0, i, 0)),
                pl.BlockSpec((1, tm, rows.shape[2]), lambda i: (1, i, 0)),
                pl.BlockSpec((tm, LANES), here),
                pl.BlockSpec((tm, D), here),
                pl.BlockSpec((1, 6, D), lambda i: (b, 0, 0)),
                pl.BlockSpec((1, D), lambda i: (0, 0)),
                pl.BlockSpec((1, D), lambda i: (0, 0))]
    args = [rows, rows, wts, x1, mod, ln2_g.reshape(1, D), ln2_b.reshape(1, D)]
    aliases = {}
    if out is not None:
        in_specs.append(pl.BlockSpec(memory_space=pl.ANY))
        aliases = {len(args): 0}
        args.append(out)
    return pl.pallas_call(
        _combine_dense_kernel,
        out_shape=jax.ShapeDtypeStruct((T, D), F32),
        grid=(per_b,),
        in_specs=in_specs,
        out_specs=pl.BlockSpec((tm, D), here),
        input_output_aliases=aliases,
        compiler_params=_params(1),
        name="combine",
    )(*args)


def _moe(h2, x1, route, wts, counts, mod, w1, w3, w2, ln2_g, ln2_b):
    B, S, D = x1.shape
    T = B * S
    P = 2 * T + N_EXPERTS * MOE_BLOCK
    nblk = P // MOE_BLOCK
    sizes = counts[0, ROUTE_OFF:ROUTE_OFF + N_EXPERTS].astype(jnp.int32)
    psizes = (sizes + MOE_BLOCK - 1) // MOE_BLOCK * MOE_BLOCK
    pends = jnp.cumsum(psizes)
    poffs = pends - psizes
    starts = jnp.arange(nblk, dtype=jnp.int32) * MOE_BLOCK
    owns = pends[None, :] <= starts[:, None]
    blk_e = jnp.minimum(jnp.sum(owns.astype(jnp.int32), axis=1), N_EXPERTS - 1)
    in_blk = blk_e[:, None] == jnp.arange(N_EXPERTS, dtype=jnp.int32)[None, :]
    seg_end = jnp.sum(jnp.where(in_blk, (poffs + sizes)[None, :], 0), axis=1)
    n_valid = jnp.clip(seg_end - starts, 0, MOE_BLOCK).astype(jnp.int32)
    r4 = route.reshape(T, LANES)[:, :4]
    sel = r4[:, :2, None] == jnp.arange(N_EXPERTS, dtype=jnp.int32)[None, None, :]
    dest = r4[:, 2:4] + jnp.sum(jnp.where(sel, poffs[None, None, :], 0), axis=-1)
    xb = _sc_scatter_rows(h2.reshape(T, D // 2), dest[:, 0], dest[:, 1], P)
    yb = _experts(xb, blk_e, n_valid, w1, w3, w2)
    out = None
    for b in range(B):
        slot_major = dest[b * S:(b + 1) * S].T.reshape(2 * S)
        rows = _sc_gather_rows(yb, slot_major).reshape(2, S, yb.shape[1])
        out = _combine_dense(rows, wts.reshape(T, LANES), x1.reshape(T, D), mod, ln2_g, ln2_b, S, b, out)
    return out.reshape(B, S, D)


def _layer(x, c, w_ada, b_ada, w_in, conv_w, conv_b, fw1, fb1, ff1, fw2, fb2, ff2, fw3, decay, skip,
           w_hy_o, w_attn_o, attn_sink, w_out, ln1_g, ln1_b, rg_w, rg_b, re_w, re_b, ew1, ew3, ew2,
           ln2_g, ln2_b):
    mod = _ada(c, w_ada, b_ada)
    q, kv, hv, hx1, hx2, ga, gh = _in_proj(x, mod, w_in, conv_w, conv_b)
    attn = _attention(q, kv, attn_sink)
    hy = _hyena(hv, hx1, hx2, fw1, fb1, ff1, fw2, fb2, ff2, fw3, decay, skip)
    x1, h2, route, wts, counts = _merge(attn, hy, ga, gh, x, mod, w_attn_o, w_hy_o, w_out,
                                        ln1_g, ln1_b, rg_w, rg_b, re_w, re_b)
    return _moe(h2, x1, route, wts, counts, mod, ew1, ew3, ew2, ln2_g, ln2_b)


def kernel(x, c, w_ada, b_ada, w_in, conv_w, conv_b, filt_w1, filt_b1, filt_freq1, filt_w2, filt_b2, filt_freq2, filt_w3, filt_decay, hy_skip, w_hy_o, w_attn_o, attn_sink, w_out, ln1_g, ln1_b, router_group_w, router_group_b, router_expert_w, router_expert_b, exp_w1, exp_w3, exp_w2, ln2_g, ln2_b):
    for l in range(w_ada.shape[0]):
        x = _layer(x, c, w_ada[l], b_ada[l], w_in[l], conv_w[l], conv_b[l], filt_w1[l], filt_b1[l],
                   filt_freq1[l], filt_w2[l], filt_b2[l], filt_freq2[l], filt_w3[l], filt_decay[l],
                   hy_skip[l], w_hy_o[l], w_attn_o[l], attn_sink[l], w_out[l], ln1_g[l], ln1_b[l],
                   router_group_w[l], router_group_b[l], router_expert_w[l], router_expert_b[l],
                   exp_w1[l], exp_w3[l], exp_w2[l], ln2_g[l], ln2_b[l])
    return x
```

```python
import functools
import math

import numpy as np
import jax
import jax.numpy as jnp
from jax import lax
from jax.experimental import pallas as pl
from jax.experimental.pallas import tpu as pltpu
from jax.experimental.pallas import tpu_sc as plsc

F32 = jnp.float32
BF16 = jnp.bfloat16

N_HEADS = 8
N_KV_HEADS = 2
HEAD_DIM = 64
ATTN_WIDTH = N_HEADS * HEAD_DIM
KV_WIDTH = N_KV_HEADS * HEAD_DIM
WINDOW = 128
BLOCK_Q = 128
HYENA_WIDTH = 512
FILTER_EMB = 33
FILTER_BANDS = (FILTER_EMB - 1) // 2
WINDOW_SHIFT = 0.05
N_GROUPS = 8
EXPERTS_PER_GROUP = 8
N_EXPERTS = N_GROUPS * EXPERTS_PER_GROUP
D_EXPERT = 512
MOE_BLOCK = 256
LN_EPS = 1e-5
DEPTH = 1
DN_ALPHA = (2.0 * DEPTH) ** 0.25
NEG = -1e30

LANES = 128
SUBLANES = 8
ROUTE_OFF = N_GROUPS
VMEM_LIMIT = 56 * 1024 * 1024


def _params(n_axes, vmem=VMEM_LIMIT):
    return pltpu.CompilerParams(dimension_semantics=("arbitrary",) * n_axes, vmem_limit_bytes=vmem)


def _split(a):
    hi = a.astype(BF16)
    lo = (a - hi.astype(F32)).astype(BF16)
    return hi, lo


def _dot3(a, b_hi, b_lo):
    a_hi, a_lo = _split(a)
    acc = jnp.dot(a_hi, b_hi, preferred_element_type=F32)
    acc = acc + jnp.dot(a_hi, b_lo, preferred_element_type=F32)
    acc = acc + jnp.dot(a_lo, b_hi, preferred_element_type=F32)
    return acc


def _pack2(a, b):
    ia = lax.bitcast_convert_type(a.astype(BF16).astype(F32), jnp.int32)
    ib = lax.bitcast_convert_type(b.astype(BF16).astype(F32), jnp.int32)
    return lax.bitcast_convert_type(ia | lax.shift_right_logical(ib, 16), F32)


def _unpack2(p):
    p = lax.bitcast_convert_type(p, jnp.int32)
    a = lax.bitcast_convert_type(p & jnp.int32(-65536), F32)
    b = lax.bitcast_convert_type(lax.shift_left(p, 16), F32)
    return a, b


def _sigmoid(x):
    return 0.5 * jnp.tanh(0.5 * x) + 0.5


def _layer_norm(r, g, b):
    mu = jnp.mean(r, axis=-1, keepdims=True)
    d = r - mu
    var = jnp.mean(d * d, axis=-1, keepdims=True)
    return d * lax.rsqrt(var + LN_EPS) * g + b


def _ada_kernel(c_ref, wh_ref, wl_ref, b_ref, o_ref):
    c = c_ref[...]
    s = c * _sigmoid(c)
    o_ref[...] = _dot3(s, wh_ref[...], wl_ref[...]) + b_ref[...]


def _ada(c, w_ada, b_ada):
    B, D = c.shape
    n_out = w_ada.shape[1]
    rows = 8
    cp = jnp.zeros((rows, D), F32).at[:B].set(c)
    wh, wl = _split(w_ada)
    tn = 1024
    out = pl.pallas_call(
        _ada_kernel,
        out_shape=jax.ShapeDtypeStruct((rows, n_out), F32),
        grid=(n_out // tn,),
        in_specs=[pl.BlockSpec((rows, D), lambda j: (0, 0)),
                  pl.BlockSpec((D, tn), lambda j: (0, j)),
                  pl.BlockSpec((D, tn), lambda j: (0, j)),
                  pl.BlockSpec((1, tn), lambda j: (0, j))],
        out_specs=pl.BlockSpec((rows, tn), lambda j: (0, j)),
        compiler_params=_params(1),
        name="ada",
    )(cp, wh, wl, b_ada.reshape(1, n_out))
    return out[:B].reshape(B, 6, D)


def _inproj_kernel(x_ref, xp_ref, xn_ref, mod_ref, w_ref, cw_ref, cb_ref,
                   q_ref, kv_ref, v_ref, x1_ref, x2_ref, ga_ref, gh_ref):
    i = pl.program_id(1)
    n = pl.num_programs(1)
    C = HYENA_WIDTH
    x = x_ref[0]
    tm, D = x.shape
    shift = mod_ref[0, 0:1, :]
    scale = mod_ref[0, 1:2, :]
    h = (x * (1.0 + scale) + shift).astype(BF16)

    def seg(lo, hi):
        return jnp.dot(h, w_ref[:, lo:hi], preferred_element_type=F32)

    o_q = 0
    o_kv = o_q + ATTN_WIDTH
    o_hy = o_kv + 2 * KV_WIDTH
    o_ga = o_hy + 3 * C
    o_gh = o_ga + D
    ga_ref[0] = _sigmoid(seg(o_ga, o_ga + D)).astype(BF16)
    gh_ref[0] = _sigmoid(seg(o_gh, o_gh + D)).astype(BF16)

    u = seg(o_hy, o_hy + 3 * C)
    xe = jnp.concatenate([xp_ref[0], xn_ref[0]], axis=0)
    he = (xe * (1.0 + scale) + shift).astype(BF16)
    ue = jnp.dot(he, w_ref[:, o_hy:o_hy + 3 * C], preferred_element_type=F32)
    prow = jnp.where(i > 0, ue[SUBLANES - 1:SUBLANES], 0.0)
    nrow = jnp.where(i < n - 1, ue[SUBLANES:SUBLANES + 1], 0.0)
    rid = lax.broadcasted_iota(jnp.int32, (tm, 1), 0)
    up = jnp.where(rid == 0, prow, pltpu.roll(u, 1, 0))
    dn = jnp.where(rid == tm - 1, nrow, pltpu.roll(u, tm - 1, 0))
    conv = cw_ref[0:1, :] * up + cw_ref[1:2, :] * u + cw_ref[2:3, :] * dn + cb_ref[...]
    v_ref[0] = conv[:, :C]
    x1_ref[0] = conv[:, C:2 * C]
    x2_ref[0] = conv[:, 2 * C:]

    q_ref[0] = (seg(o_q, o_q + ATTN_WIDTH) * (HEAD_DIM ** -0.5)).astype(BF16)
    kv_ref[0] = seg(o_kv, o_kv + 2 * KV_WIDTH).astype(BF16)


def _in_proj(x, mod, w_in, conv_w, conv_b):
    B, S, D = x.shape
    C = HYENA_WIDTH
    tm = min(512, S)
    r8 = tm // SUBLANES
    nb8 = S // SUBLANES
    wb = w_in.astype(BF16)
    nw = wb.shape[1]
    row = lambda b, i: (b, i, 0)
    shapes = [(ATTN_WIDTH, BF16), (2 * KV_WIDTH, BF16), (C, F32), (C, F32), (C, F32), (D, BF16), (D, BF16)]
    return pl.pallas_call(
        _inproj_kernel,
        out_shape=[jax.ShapeDtypeStruct((B, S, w), dt) for w, dt in shapes],
        grid=(B, S // tm),
        in_specs=[pl.BlockSpec((1, tm, D), row),
                  pl.BlockSpec((1, SUBLANES, D), lambda b, i: (b, jnp.maximum(i * r8 - 1, 0), 0)),
                  pl.BlockSpec((1, SUBLANES, D), lambda b, i: (b, jnp.minimum((i + 1) * r8, nb8 - 1), 0)),
                  pl.BlockSpec((1, 6, D), lambda b, i: (b, 0, 0)),
                  pl.BlockSpec((D, nw), lambda b, i: (0, 0)),
                  pl.BlockSpec((3, 3 * C), lambda b, i: (0, 0)),
                  pl.BlockSpec((1, 3 * C), lambda b, i: (0, 0))],
        out_specs=[pl.BlockSpec((1, tm, w), row) for w, _ in shapes],
        compiler_params=_params(2),
        name="in_proj",
    )(x, x, x, mod, wb, conv_w.astype(F32), conv_b.reshape(1, 3 * C).astype(F32))


ATT_TQ = 512


def _attn_kernel(sink_ref, q_ref, kvp_ref, kvc_ref, kvn_ref, bias_ref, o_ref, kv_scr, *, seq_len):
    i = pl.program_id(1)
    Q = BLOCK_Q
    TQ = q_ref.shape[1]
    G = N_HEADS // N_KV_HEADS
    kv_scr[0:Q] = kvp_ref[0]
    kv_scr[Q:Q + TQ] = kvc_ref[0]
    kv_scr[Q + TQ:] = kvn_ref[0]
    col = lax.broadcasted_iota(jnp.int32, (1, 3 * Q), 1)
    rhead = lax.broadcasted_iota(jnp.int32, (G * Q, 1), 0) // Q
    for j in range(TQ // Q):
        kpos = i * TQ + (j - 1) * Q + col
        colbias = jnp.where((kpos >= 0) & (kpos < seq_len), 0.0, NEG)
        for kv in range(N_KV_HEADS):
            kk = kv_scr[j * Q:(j + 3) * Q, kv * HEAD_DIM:(kv + 1) * HEAD_DIM]
            vv = kv_scr[j * Q:(j + 3) * Q, KV_WIDTH + kv * HEAD_DIM:KV_WIDTH + (kv + 1) * HEAD_DIM]
            heads = [kv * G + g for g in range(G)]
            qg = jnp.concatenate([q_ref[0, j * Q:(j + 1) * Q, h * HEAD_DIM:(h + 1) * HEAD_DIM] for h in heads], axis=0)
            s = lax.dot_general(qg, kk, (((1,), (1,)), ((), ())), preferred_element_type=F32)
            s = s + bias_ref[kv] + colbias
            snk = jnp.where(rhead == 0, sink_ref[heads[0]],
                            jnp.where(rhead == 1, sink_ref[heads[1]],
                                      jnp.where(rhead == 2, sink_ref[heads[2]], sink_ref[heads[3]])))
            m = jnp.maximum(jnp.max(s, axis=1, keepdims=True), snk)
            p = jnp.exp(s - m)
            den = jnp.sum(p, axis=1, keepdims=True) + jnp.exp(snk - m)
            o = jnp.dot(p.astype(BF16), vv, preferred_element_type=F32) / den
            for g, h in enumerate(heads):
                o_ref[0, j * Q:(j + 1) * Q, h * HEAD_DIM:(h + 1) * HEAD_DIM] = o[g * Q:(g + 1) * Q].astype(BF16)


def _attention(q, kv, sink):
    B, S, _ = q.shape
    Q = BLOCK_Q
    TQ = min(ATT_TQ, S)
    r = TQ // Q
    nq = S // Q
    G = N_HEADS // N_KV_HEADS
    assert G == 4
    a = jnp.arange(Q)[:, None]
    j = jnp.arange(3 * Q)[None, :]
    rel = jnp.abs(j - Q - a).astype(F32)
    slopes = 2.0 ** (-8.0 * jnp.arange(1, N_HEADS + 1, dtype=F32) / N_HEADS)
    bias = jnp.where(rel[None] <= WINDOW, -slopes[:, None, None] * rel[None], NEG).astype(F32)
    bias = bias.reshape(N_KV_HEADS, G * Q, 3 * Q)
    cur = lambda b, i: (b, i, 0)
    return pl.pallas_call(
        functools.partial(_attn_kernel, seq_len=S),
        out_shape=jax.ShapeDtypeStruct((B, S, ATTN_WIDTH), BF16),
        grid=(B, S // TQ),
        in_specs=[pl.BlockSpec(memory_space=pltpu.SMEM),
                  pl.BlockSpec((1, TQ, ATTN_WIDTH), cur),
                  pl.BlockSpec((1, Q, 2 * KV_WIDTH), lambda b, i: (b, jnp.maximum(i * r - 1, 0), 0)),
                  pl.BlockSpec((1, TQ, 2 * KV_WIDTH), cur),
                  pl.BlockSpec((1, Q, 2 * KV_WIDTH), lambda b, i: (b, jnp.minimum((i + 1) * r, nq - 1), 0)),
                  pl.BlockSpec((N_KV_HEADS, G * Q, 3 * Q), lambda b, i: (0, 0, 0))],
        out_specs=pl.BlockSpec((1, TQ, ATTN_WIDTH), cur),
        scratch_shapes=[pltpu.VMEM((TQ + 2 * Q, 2 * KV_WIDTH), BF16)],
        compiler_params=_params(2),
        name="attn",
    )(sink.astype(F32), q, kv, kv, kv, bias)


def _filter_kernel(z_ref, w1h, w1l, b1_ref, f1_ref, w2h, w2l, b2_ref, f2_ref, w3h, w3l, dec_ref,
                   k_ref, s_ref):
    i = pl.program_id(0)
    z = z_ref[...]
    h1 = jnp.sin(f1_ref[...] * (_dot3(z, w1h[...], w1l[...]) + b1_ref[...]))
    h2 = jnp.sin(f2_ref[...] * (_dot3(h1, w2h[...], w2l[...]) + b2_ref[...]))
    k = _dot3(h2, w3h[...], w3l[...])
    t = z[:, 0:1]
    k = k * (jnp.exp(-t * jnp.abs(dec_ref[...])) + WINDOW_SHIFT)
    k_ref[...] = k

    @pl.when(i == 0)
    def _():
        s_ref[...] = jnp.zeros_like(s_ref)

    s_ref[...] += jnp.sum(jnp.abs(k), axis=0, keepdims=True)


def _filter_embedding(L):
    t = np.linspace(0.0, 1.0, L, dtype=np.float32).astype(np.float64)[:, None]
    w = (2.0 * math.pi * np.arange(L, dtype=np.float32) / np.float32(L)).astype(np.float64)[:, None]
    bands = np.linspace(1e-4, FILTER_BANDS - 1, FILTER_BANDS, dtype=np.float32).astype(np.float64)[None, :]
    bw = (bands.astype(np.float32) * w.astype(np.float32)).astype(np.float64)
    z = np.concatenate([t, np.cos(bw), -np.sin(bw)], axis=-1)
    zp = np.zeros((L, LANES), np.float32)
    zp[:, :FILTER_EMB] = z.astype(np.float32)
    return jnp.asarray(zp)


def _pad2(a, r, c):
    return jnp.zeros((r, c), F32).at[:a.shape[0], :a.shape[1]].set(a.astype(F32))


def _filters(L, fw1, fb1, ff1, fw2, fb2, ff2, fw3, decay):
    H = LANES
    nf = fw3.shape[1]
    z = _filter_embedding(L)
    w1h, w1l = _split(_pad2(fw1, H, H))
    w2h, w2l = _split(_pad2(fw2, H, H))
    w3h, w3l = _split(_pad2(fw3, H, nf))
    b1 = _pad2(fb1[None], 1, H)
    f1 = _pad2(ff1[None], 1, H)
    b2 = _pad2(fb2[None], 1, H)
    f2 = _pad2(ff2[None], 1, H)
    tr = min(512, L)
    full = lambda r, c: pl.BlockSpec((r, c), lambda i: (0, 0))
    return pl.pallas_call(
        _filter_kernel,
        out_shape=[jax.ShapeDtypeStruct((L, nf), F32), jax.ShapeDtypeStruct((1, nf), F32)],
        grid=(L // tr,),
        in_specs=[pl.BlockSpec((tr, H), lambda i: (i, 0)),
                  full(H, H), full(H, H), full(1, H), full(1, H),
                  full(H, H), full(H, H), full(1, H), full(1, H),
                  full(H, nf), full(H, nf), full(1, nf)],
        out_specs=[pl.BlockSpec((tr, nf), lambda i: (i, 0)), full(1, nf)],
        compiler_params=_params(1),
        name="filter",
    )(z, w1h, w1l, b1, f1, w2h, w2l, b2, f2, w3h, w3l, decay.reshape(1, nf).astype(F32))


def _np_bf16(m64):
    return jnp.asarray(m64.astype(np.float32).astype(BF16))


def _dft_constants(L):
    N = 2 * L
    n2 = LANES
    n1 = N // n2
    h1 = n1 // 2
    k1 = np.arange(n1)[:, None]
    s1 = np.arange(h1)[None, :]
    ang = -2.0 * np.pi * ((k1 * s1) % n1) / n1
    wr, wi = np.cos(ang), np.sin(ang)
    w1_filt = np.block([[wr, wr], [wi, wi], [wr, -wr], [wi, -wi]])
    w1_cplx = np.block([[wr, -wi], [wi, wr]])
    vr, vi = wr.T / N, -wi.T / N
    w3 = np.block([[vr, -vi], [vi, vr]])
    k2 = np.arange(n2)[:, None]
    s2 = np.arange(n2)[None, :]
    a2 = -2.0 * np.pi * ((k2 * s2) % n2) / n2
    w2r, w2i = jnp.asarray(np.cos(a2), F32), jnp.asarray(np.sin(a2), F32)
    at = -2.0 * np.pi * ((np.arange(n1)[:, None] * s2) % N) / N
    twr, twi = jnp.asarray(np.cos(at), F32), jnp.asarray(np.sin(at), F32)
    mr = w2r[None] * twr[:, None, :] - w2i[None] * twi[:, None, :]
    mi = w2r[None] * twi[:, None, :] + w2i[None] * twr[:, None, :]
    fwd = jnp.concatenate([jnp.concatenate([mr, -mi], axis=2),
                           jnp.concatenate([mi, mr], axis=2)], axis=1)
    fwd = fwd.astype(BF16)
    return dict(n1=n1, w1_filt=_np_bf16(w1_filt), w1_cplx=_np_bf16(w1_cplx), w3=_np_bf16(w3),
                fwd=fwd)


SCH = 8


def _dft1_kernel(x_ref, w_ref, a_ref, *, n1):
    w = w_ref[...]
    for j in range(SCH):
        rhs = jnp.concatenate([x_ref[0, 0, :, j, :], x_ref[0, 1, :, j, :]], axis=0)
        res = jnp.dot(w, rhs.astype(BF16), preferred_element_type=F32)
        a_ref[0, :, j, :] = _pack2(res[:n1], res[n1:])


def _dft1_data(x, consts):
    B, L, C = x.shape
    n1 = consts["n1"]
    h1 = n1 // 2
    xv = x.reshape(B // 2, 2, h1, LANES, C)
    return pl.pallas_call(
        functools.partial(_dft1_kernel, n1=n1),
        out_shape=jax.ShapeDtypeStruct((B // 2, n1, LANES, C), F32),
        grid=(B // 2, LANES // SCH),
        in_specs=[pl.BlockSpec((1, 2, h1, SCH, C), lambda p, j: (p, 0, 0, j, 0)),
                  pl.BlockSpec((2 * n1, n1), lambda p, j: (0, 0))],
        out_specs=pl.BlockSpec((1, n1, SCH, C), lambda p, j: (p, 0, j, 0)),
        compiler_params=_params(2),
        name="dft1",
    )(xv, consts["w1_cplx"])


def _dft1f_kernel(x_ref, w_ref, a_ref, *, n1):
    C = HYENA_WIDTH
    w = w_ref[...]
    for j in range(SCH):
        rhs = jnp.concatenate([x_ref[:, j, :C], x_ref[:, j, C:]], axis=0)
        res = jnp.dot(w, rhs.astype(BF16), preferred_element_type=F32)
        a_ref[0, :, 0, j, :] = _pack2(res[:n1], res[n1:2 * n1])
        a_ref[0, :, 1, j, :] = _pack2(res[2 * n1:3 * n1], res[3 * n1:])


def _dft1_filter(kraw, consts):
    L, nf = kraw.shape
    C = HYENA_WIDTH
    n_ord = nf // (2 * C)
    n1 = consts["n1"]
    h1 = n1 // 2
    kv = kraw.reshape(h1, LANES, nf)
    return pl.pallas_call(
        functools.partial(_dft1f_kernel, n1=n1),
        out_shape=jax.ShapeDtypeStruct((n_ord, n1, 2, LANES, C), F32),
        grid=(n_ord, LANES // SCH),
        in_specs=[pl.BlockSpec((h1, SCH, 2 * C), lambda o, j: (0, j, o)),
                  pl.BlockSpec((4 * n1, n1), lambda o, j: (0, 0))],
        out_specs=pl.BlockSpec((1, n1, 2, SCH, C), lambda o, j: (o, 0, 0, j, 0)),
        compiler_params=_params(2),
        name="dft1f",
    )(kv, consts["w1_filt"])


KCH = 8


def _midf_kernel(a_ref, f_ref, inv_ref, b0_ref, h_ref):
    n2 = LANES
    sc = inv_ref[0]
    for k in range(KCH):
        p = jnp.concatenate(_unpack2(a_ref[0, k, :n2, :]), axis=0).astype(BF16)
        q = jnp.concatenate(_unpack2(a_ref[0, k, n2:, :]), axis=0).astype(BF16)
        h_re = jnp.dot(f_ref[k, :n2, :], p, preferred_element_type=F32)
        h_im = jnp.dot(f_ref[k, n2:, :], q, preferred_element_type=F32)
        h_ref[0, k] = _pack2((h_re - b0_ref[0]) * sc, h_im * sc)


def _filter_spectrum(af, inv_den, bwd0, consts):
    n_ord, n1, _, n2, C = af.shape
    a = af.reshape(n_ord, n1, 2 * n2, C)
    tab = pl.BlockSpec((KCH, 2 * n2, 2 * n2), lambda k, o: (k, 0, 0))
    vec = pl.BlockSpec((1, 1, C), lambda k, o: (o, 0, 0))
    return pl.pallas_call(
        _midf_kernel,
        out_shape=jax.ShapeDtypeStruct((n_ord, n1, n2, C), F32),
        grid=(n1 // KCH, n_ord),
        in_specs=[pl.BlockSpec((1, KCH, 2 * n2, C), lambda k, o: (o, k, 0, 0)), tab, vec, vec],
        out_specs=pl.BlockSpec((1, KCH, n2, C), lambda k, o: (o, k, 0, 0)),
        compiler_params=_params(2),
        name="midf",
    )(a, consts["fwd"], inv_den, bwd0)


def _mid_kernel(a_ref, f_ref, h_ref, b_ref):
    n2 = LANES
    for k in range(KCH):
        a = jnp.concatenate(_unpack2(a_ref[0, k]), axis=0).astype(BF16)
        x = jnp.dot(f_ref[k], a, preferred_element_type=F32)
        xr, xi = x[:n2], x[n2:]
        hr, hi = _unpack2(h_ref[0, k])
        y = jnp.concatenate([xr * hr - xi * hi, xr * hi + xi * hr], axis=0)
        b = lax.dot_general(f_ref[k], y.astype(BF16), (((0,), (0,)), ((), ())), preferred_element_type=F32)
        b_ref[0, k] = _pack2(b[:n2], b[n2:])


def _mid(a, hspec, order, consts):
    P, n1, n2, C = a.shape
    tab = pl.BlockSpec((KCH, 2 * n2, 2 * n2), lambda k, p: (k, 0, 0))
    return pl.pallas_call(
        _mid_kernel,
        out_shape=jax.ShapeDtypeStruct((P, n1, n2, C), F32),
        grid=(n1 // KCH, P),
        in_specs=[pl.BlockSpec((1, KCH, n2, C), lambda k, p: (p, k, 0, 0)),
                  tab,
                  pl.BlockSpec((1, KCH, n2, C), lambda k, p: (order, k, 0, 0))],
        out_specs=pl.BlockSpec((1, KCH, n2, C), lambda k, p: (p, k, 0, 0)),
        compiler_params=_params(2),
        name="mid",
    )(a, consts["fwd"], hspec)


def _dft3_kernel(b_ref, w_ref, v_ref, g_ref, skip_ref, *rest, h1, chain):
    if chain:
        w1_ref, z_ref, a_ref, slab_ref = rest
        w1 = w1_ref[...]
    else:
        z_ref, slab_ref = rest
    w = w_ref[...]
    skip = skip_ref[0]
    n1 = 2 * h1
    for j in range(SCH):
        slab_ref[...] = b_ref[0, :, j, :]
        rhs = jnp.concatenate(_unpack2(slab_ref[...]), axis=0)
        y = jnp.dot(w, rhs.astype(BF16), preferred_element_type=F32)
        z = [g_ref[0, r, :, j, :] * (y[r * h1:(r + 1) * h1] + v_ref[0, r, :, j, :] * skip) for r in range(2)]
        for r in range(2):
            z_ref[0, r, :, j, :] = z[r]
        if chain:
            res = jnp.dot(w1, jnp.concatenate(z, axis=0).astype(BF16), preferred_element_type=F32)
            a_ref[0, :, j, :] = _pack2(res[:n1], res[n1:])


def _dft3_gate(b5, v, gate, skip, consts, chain):
    P, n1, n2, C = b5.shape
    h1 = n1 // 2
    B, L, _ = v.shape
    five = lambda t: t.reshape(P, 2, h1, n2, C)
    dat = pl.BlockSpec((1, 2, h1, SCH, C), lambda p, j: (p, 0, 0, j, 0))
    packed = pl.BlockSpec((1, n1, SCH, C), lambda p, j: (p, 0, j, 0))
    in_specs = [packed, pl.BlockSpec((n1, 2 * n1), lambda p, j: (0, 0)), dat, dat,
                pl.BlockSpec((1, C), lambda p, j: (0, 0))]
    args = [b5, consts["w3"], five(v), five(gate), skip.reshape(1, C).astype(F32)]
    out_shape = [jax.ShapeDtypeStruct((P, 2, h1, n2, C), F32)]
    out_specs = [dat]
    if chain:
        in_specs.append(pl.BlockSpec((2 * n1, n1), lambda p, j: (0, 0)))
        args.append(consts["w1_cplx"])
        out_shape.append(jax.ShapeDtypeStruct((P, n1, n2, C), F32))
        out_specs.append(packed)
    outs = pl.pallas_call(
        functools.partial(_dft3_kernel, h1=h1, chain=chain),
        out_shape=out_shape,
        grid=(P, n2 // SCH),
        in_specs=in_specs,
        out_specs=out_specs,
        scratch_shapes=[pltpu.VMEM((n1, C), F32)],
        compiler_params=_params(2),
        name="dft3",
    )(*args)
    z = outs[0].reshape(B, L, C)
    return (z, outs[1]) if chain else (z, None)


def _hyena(v, x1, x2, fw1, fb1, ff1, fw2, fb2, ff2, fw3, decay, skip):
    B, L, C = v.shape
    consts = _dft_constants(L)
    kraw, ksum = _filters(L, fw1, fb1, ff1, fw2, fb2, ff2, fw3, decay)
    ks = ksum.reshape(2, 2, C)
    inv_den = (1.0 / (ks[:, 0] + ks[:, 1])).reshape(2, 1, C)
    bwd0 = kraw[0].reshape(2, 2, C)[:, 1].reshape(2, 1, C)
    hspec = _filter_spectrum(_dft1_filter(kraw, consts), inv_den, bwd0, consts)
    gates = (x1, x2)
    z, a5 = v, _dft1_data(v, consts)
    for o, gate in enumerate(gates):
        b5 = _mid(a5, hspec, o, consts)
        z, a5 = _dft3_gate(b5, z, gate, skip[o], consts, chain=o + 1 < len(gates))
    return z


MERGE_SPLIT = 1

def _merge_kernel(attn_ref, hy_ref, ga_ref, gh_ref, x_ref, mod_ref, wa_ref, wh_ref, wo_ref,
                  g1_ref, b1_ref, rwh_ref, rwl_ref, rb_ref, tri_ref,
                  x1_ref, h2_ref, route_ref, wts_ref, cnt_ref, carry_ref):
    first = (pl.program_id(0) == 0) & (pl.program_id(1) == 0)

    @pl.when(first)
    def _():
        carry_ref[...] = jnp.zeros_like(carry_ref)

    sub = tri_ref.shape[0]
    for part in range(x_ref.shape[1] // sub):
        _merge_rows(slice(part * sub, (part + 1) * sub), attn_ref, hy_ref, ga_ref, gh_ref, x_ref, mod_ref,
                    wa_ref, wh_ref, wo_ref, g1_ref, b1_ref, rwh_ref, rwl_ref, rb_ref, tri_ref,
                    x1_ref, h2_ref, route_ref, wts_ref, carry_ref)
    cnt_ref[...] = carry_ref[...]


def _merge_rows(rows, attn_ref, hy_ref, ga_ref, gh_ref, x_ref, mod_ref, wa_ref, wh_ref, wo_ref,
                g1_ref, b1_ref, rwh_ref, rwl_ref, rb_ref, tri_ref, x1_ref, h2_ref, route_ref, wts_ref, carry_ref):
    a = jnp.dot(attn_ref[0, rows, :], wa_ref[...], preferred_element_type=F32)
    hy = jnp.dot(hy_ref[0, rows, :].astype(BF16), wh_ref[...], preferred_element_type=F32)
    merged = ga_ref[0, rows, :].astype(F32) * a + gh_ref[0, rows, :].astype(F32) * hy
    y = jnp.dot(merged.astype(BF16), wo_ref[...], preferred_element_type=F32)
    gate1 = mod_ref[0, 2:3, :]
    shift2 = mod_ref[0, 3:4, :]
    scale2 = mod_ref[0, 4:5, :]
    x1 = _layer_norm(DN_ALPHA * x_ref[0, rows, :] + gate1 * y, g1_ref[...], b1_ref[...])
    x1_ref[0, rows, :] = x1
    h2 = x1 * (1.0 + scale2) + shift2
    half = h2.shape[1] // 2
    h2_ref[0, rows, :] = _pack2(h2[:, :half], h2[:, half:])

    logits = _dot3(h2, rwh_ref[...], rwl_ref[...]) + rb_ref[...]
    tm = logits.shape[0]
    lane = lax.broadcasted_iota(jnp.int32, (tm, LANES), 1)
    lanef = lane.astype(F32)
    big = float(LANES)

    def first_lane(mask):
        return jnp.min(jnp.where(mask, lanef, big), axis=1, keepdims=True).astype(jnp.int32)

    gmask = lane < N_GROUPS
    gl = jnp.where(gmask, logits, NEG)
    gmax = jnp.max(gl, axis=1, keepdims=True)
    gidx = first_lane(gl == gmax)
    pg = 1.0 / jnp.sum(jnp.exp(gl - gmax), axis=1, keepdims=True)
    lo = ROUTE_OFF + gidx * EXPERTS_PER_GROUP
    emask = (lane >= lo) & (lane < lo + EXPERTS_PER_GROUP)
    el = jnp.where(emask, logits, NEG)
    v1 = jnp.max(el, axis=1, keepdims=True)
    i1 = first_lane(el == v1)
    el2 = jnp.where(emask & (lane != i1), logits, NEG)
    v2 = jnp.max(el2, axis=1, keepdims=True)
    i2 = first_lane(el2 == v2)
    e21 = jnp.exp(v2 - v1)
    w1 = pg / (1.0 + e21)
    w2 = pg * e21 / (1.0 + e21)

    sel1 = lane == i1
    sel2 = lane == i2
    onehot = jnp.where(sel1 | sel2, 1.0, 0.0)
    prefix = jnp.dot(tri_ref[...], onehot.astype(BF16), preferred_element_type=F32) + carry_ref[...]
    r1 = jnp.sum(jnp.where(sel1, prefix, 0.0), axis=1, keepdims=True)
    r2 = jnp.sum(jnp.where(sel2, prefix, 0.0), axis=1, keepdims=True)
    carry_ref[...] += jnp.sum(onehot, axis=0, keepdims=True)

    ranks = jnp.where(lane == 2, r1, jnp.where(lane == 3, r2, 0.0)).astype(jnp.int32)
    route_ref[0, rows, :] = jnp.where(lane == 0, i1 - ROUTE_OFF, jnp.where(lane == 1, i2 - ROUTE_OFF, ranks))
    wts_ref[0, rows, :] = jnp.where(lane == 0, w1, jnp.where(lane == 1, w2, 0.0))


def _merge(attn, hy, ga, gh, x, mod, w_attn_o, w_hy_o, w_out, ln1_g, ln1_b, rg_w, rg_b, re_w, re_b):
    B, S, D = x.shape
    tm = min(512, S)
    rw = jnp.zeros((D, LANES), F32).at[:, :N_GROUPS].set(rg_w).at[:, ROUTE_OFF:ROUTE_OFF + N_EXPERTS].set(re_w)
    rb = jnp.zeros((1, LANES), F32).at[0, :N_GROUPS].set(rg_b).at[0, ROUTE_OFF:ROUTE_OFF + N_EXPERTS].set(re_b)
    rwh, rwl = _split(rw)
    sub = tm // MERGE_SPLIT
    tri = (jnp.arange(sub)[:, None] > jnp.arange(sub)[None, :]).astype(BF16)
    row = lambda b, i: (b, i, 0)
    full = lambda r, c: pl.BlockSpec((r, c), lambda b, i: (0, 0))
    outs = [jax.ShapeDtypeStruct((B, S, D), F32), jax.ShapeDtypeStruct((B, S, D // 2), F32),
            jax.ShapeDtypeStruct((B, S, LANES), jnp.int32), jax.ShapeDtypeStruct((B, S, LANES), F32),
            jax.ShapeDtypeStruct((1, LANES), F32)]
    return pl.pallas_call(
        _merge_kernel,
        out_shape=outs,
        grid=(B, S // tm),
        in_specs=[pl.BlockSpec((1, tm, ATTN_WIDTH), row), pl.BlockSpec((1, tm, HYENA_WIDTH), row),
                  pl.BlockSpec((1, tm, D), row), pl.BlockSpec((1, tm, D), row), pl.BlockSpec((1, tm, D), row),
                  pl.BlockSpec((1, 6, D), lambda b, i: (b, 0, 0)),
                  full(ATTN_WIDTH, D), full(HYENA_WIDTH, D), full(D, D),
                  full(1, D), full(1, D), full(D, LANES), full(D, LANES), full(1, LANES), full(sub, sub)],
        out_specs=[pl.BlockSpec((1, tm, D), row), pl.BlockSpec((1, tm, D // 2), row),
                   pl.BlockSpec((1, tm, LANES), row), pl.BlockSpec((1, tm, LANES), row), full(1, LANES)],
        scratch_shapes=[pltpu.VMEM((1, LANES), F32)],
        compiler_params=_params(2),
        name="merge",
    )(attn, hy, ga, gh, x, mod, w_attn_o.astype(BF16), w_hy_o.astype(BF16), w_out.astype(BF16),
      ln1_g.reshape(1, D), ln1_b.reshape(1, D), rwh, rwl, rb, tri)


SC_ROWS = 64


def _sc_workers():
    info = plsc.get_sparse_core_info()
    return info.num_cores, info.num_cores * info.num_subcores


def _sc_split(n):
    _, workers = _sc_workers()
    per_worker = n // workers
    chunks = per_worker // SC_ROWS
    assert per_worker * workers == n and chunks * SC_ROWS == per_worker and chunks % 2 == 0
    return workers, per_worker, chunks


def _sc_scatter_rows(src, idx0, idx1, n_out):
    n, width = src.shape
    nc, _ = _sc_workers()
    workers, per_worker, chunks = _sc_split(n)
    mesh = plsc.VectorSubcoreMesh(core_axis_name="c", subcore_axis_name="s")

    def body(src_hbm, i0_hbm, i1_hbm, out_hbm, i0_v, i1_v, rows_v, sem):
        wid = lax.axis_index("s") * nc + lax.axis_index("c")
        base = wid * per_worker
        pltpu.sync_copy(i0_hbm.at[wid], i0_v)
        pltpu.sync_copy(i1_hbm.at[wid], i1_v)

        def load(chunk, buf):
            return pltpu.make_async_copy(src_hbm.at[pl.ds(base + chunk * SC_ROWS, SC_ROWS)], rows_v.at[buf], sem)

        load(0, 0).start()

        @pl.loop(0, chunks, step=2)
        def _(c):
            for b in range(2):
                chunk = c + b
                load(chunk, b).wait()

                @pl.when(chunk + 1 < chunks)
                def _():
                    load(chunk + 1, 1 - b).start()

                pltpu.sync_copy(rows_v.at[b], out_hbm.at[i0_v.at[chunk]])
                pltpu.sync_copy(rows_v.at[b], out_hbm.at[i1_v.at[chunk]])

    shaped = lambda i: i.reshape(workers, chunks, SC_ROWS)
    return pl.kernel(
        body,
        out_type=jax.ShapeDtypeStruct((n_out, width), src.dtype),
        mesh=mesh,
        scratch_types=[pltpu.VMEM((chunks, SC_ROWS), jnp.int32),
                       pltpu.VMEM((chunks, SC_ROWS), jnp.int32),
                       pltpu.VMEM((2, SC_ROWS, width), src.dtype),
                       pltpu.SemaphoreType.DMA],
        name="sc_scatter",
    )(src, shaped(idx0), shaped(idx1))


def _expert_kernel(first_ref, nb_ref, sz_ref, tot_ref, w1_ref, w3_ref, w2_ref, xb_ref, yb_ref,
                   xbuf, ybuf, c1_ref, c3_ref, c2_ref, lsem, ssem):
    e = pl.program_id(0)
    nb = nb_ref[e]
    first = first_ref[e]
    total = tot_ref[0]
    rows = xbuf.shape[1]

    def load(g, slot):
        src = xb_ref.at[pl.ds(pl.multiple_of(g * rows, rows), rows)]
        return pltpu.make_async_copy(src, xbuf.at[slot], lsem.at[slot])

    def store(g, slot):
        dst = yb_ref.at[pl.ds(pl.multiple_of(g * rows, rows), rows)]
        return pltpu.make_async_copy(ybuf.at[slot], dst, ssem.at[slot])

    @pl.when((e == 0) & (total > 0))
    def _():
        load(0, 0).start()

    @pl.when(nb > 0)
    def _():
        c1_ref[...] = w1_ref[0].astype(BF16)
        c3_ref[...] = w3_ref[0].astype(BF16)
        c2_ref[...] = w2_ref[0].astype(BF16)

        def block(j, carry):
            g = first + j
            slot = lax.rem(g, 2)
            load(g, slot).wait()

            @pl.when(g + 1 < total)
            def _():
                load(g + 1, 1 - slot).start()

            @pl.when(g >= 2)
            def _():
                store(g - 2, slot).wait()

            n_valid = sz_ref[e] - j * rows
            rid = lax.broadcasted_iota(jnp.int32, (rows, 1), 0)
            xa, xb = _unpack2(jnp.where(rid < n_valid, xbuf[slot], 0.0))
            x = jnp.concatenate([xa, xb], axis=1).astype(BF16)
            a = jnp.dot(x, c1_ref[...], preferred_element_type=F32)
            gate = jnp.dot(x, c3_ref[...], preferred_element_type=F32)
            hmid = (a * _sigmoid(a) * gate).astype(BF16)
            y = jnp.dot(hmid, c2_ref[...], preferred_element_type=F32)
            half = y.shape[1] // 2
            ybuf[slot] = _pack2(y[:, :half], y[:, half:])
            store(g, slot).start()
            return carry

        lax.fori_loop(0, nb, block, 0)

    @pl.when(e == pl.num_programs(0) - 1)
    def _():
        for back in (2, 1):
            @pl.when(total >= back)
            def _():
                g = total - back
                store(g, lax.rem(g, 2)).wait()


def _experts(xb, first_blk, n_blk, sizes, w1, w3, w2):
    P, W = xb.shape
    E, D, DE = w1.shape
    total = jnp.sum(n_blk, keepdims=True)
    wspec = lambda r, c: pl.BlockSpec((1, r, c), lambda e, *_: (e, 0, 0))
    grid_spec = pltpu.PrefetchScalarGridSpec(
        num_scalar_prefetch=4,
        grid=(E,),
        in_specs=[wspec(D, DE), wspec(D, DE), wspec(DE, D), pl.BlockSpec(memory_space=pl.ANY)],
        out_specs=pl.BlockSpec(memory_space=pl.ANY),
        scratch_shapes=[pltpu.VMEM((2, MOE_BLOCK, W), F32), pltpu.VMEM((2, MOE_BLOCK, W), F32),
                        pltpu.VMEM((D, DE), BF16), pltpu.VMEM((D, DE), BF16), pltpu.VMEM((DE, D), BF16),
                        pltpu.SemaphoreType.DMA((2,)), pltpu.SemaphoreType.DMA((2,))],
    )
    return pl.pallas_call(
        _expert_kernel,
        out_shape=jax.ShapeDtypeStruct((P, W), F32),
        grid_spec=grid_spec,
        compiler_params=_params(1),
        name="experts",
    )(first_blk, n_blk, sizes, total, w1, w3, w2, xb)


def _sc_gather_rows(table, idx):
    n, width = idx.shape[0], table.shape[1]
    nc, _ = _sc_workers()
    workers, per_worker, chunks = _sc_split(n)
    mesh = plsc.VectorSubcoreMesh(core_axis_name="c", subcore_axis_name="s")

    def body(table_hbm, idx_hbm, out_hbm, idx_v, rows_v, sem):
        wid = lax.axis_index("s") * nc + lax.axis_index("c")
        base = wid * per_worker
        pltpu.sync_copy(idx_hbm.at[wid], idx_v)

        def gather(chunk, buf):
            return pltpu.make_async_copy(table_hbm.at[idx_v.at[chunk]], rows_v.at[buf], sem)

        gather(0, 0).start()

        @pl.loop(0, chunks, step=2)
        def _(c):
            for b in range(2):
                chunk = c + b
                gather(chunk, b).wait()

                @pl.when(chunk + 1 < chunks)
                def _():
                    gather(chunk + 1, 1 - b).start()

                pltpu.sync_copy(rows_v.at[b], out_hbm.at[pl.ds(base + chunk * SC_ROWS, SC_ROWS)])

    return pl.kernel(
        body,
        out_type=jax.ShapeDtypeStruct((n, width), table.dtype),
        mesh=mesh,
        scratch_types=[pltpu.VMEM((chunks, SC_ROWS), jnp.int32),
                       pltpu.VMEM((2, SC_ROWS, width), table.dtype),
                       pltpu.SemaphoreType.DMA],
        name="sc_gather",
    )(table, idx.reshape(workers, chunks, SC_ROWS))


def _combine_dense_kernel(r0_ref, r1_ref, wts_ref, x1_ref, mod_ref, g_ref, b_ref, *rest):
    o_ref = rest[-1]
    w = wts_ref[...]
    y0 = jnp.concatenate(_unpack2(r0_ref[0]), axis=1)
    y1 = jnp.concatenate(_unpack2(r1_ref[0]), axis=1)
    y = w[:, 0:1] * y0 + w[:, 1:2] * y1
    gate2 = mod_ref[0, 5:6, :]
    o_ref[...] = _layer_norm(DN_ALPHA * x1_ref[...] + gate2 * y, g_ref[...], b_ref[...])


def _combine_dense(rows, wts, x1, mod, ln2_g, ln2_b, S, b, out):
    T, D = x1.shape
    tm = min(512, S)
    per_b = S // tm
    here = lambda i: (b * per_b + i, 0)
    in_specs = [pl.BlockSpec((1, tm, rows.shape[2]), lambda i: (0, i, 0)),
                pl.BlockSpec((1, tm, rows.shape[2]), lambda i: (1, i, 0)),
                pl.BlockSpec((tm, LANES), here),
                pl.BlockSpec((tm, D), here),
                pl.BlockSpec((1, 6, D), lambda i: (b, 0, 0)),
                pl.BlockSpec((1, D), lambda i: (0, 0)),
                pl.BlockSpec((1, D), lambda i: (0, 0))]
    args = [rows, rows, wts, x1, mod, ln2_g.reshape(1, D), ln2_b.reshape(1, D)]
    aliases = {}
    if out is not None:
        in_specs.append(pl.BlockSpec(memory_space=pl.ANY))
        aliases = {len(args): 0}
        args.append(out)
    return pl.pallas_call(
        _combine_dense_kernel,
        out_shape=jax.ShapeDtypeStruct((T, D), F32),
        grid=(per_b,),
        in_specs=in_specs,
        out_specs=pl.BlockSpec((tm, D), here),
        input_output_aliases=aliases,
        compiler_params=_params(1),
        name="combine",
    )(*args)


def _moe(h2, x1, route, wts, counts, mod, w1, w3, w2, ln2_g, ln2_b):
    B, S, D = x1.shape
    T = B * S
    P = 2 * T + N_EXPERTS * MOE_BLOCK
    sizes = counts[0, ROUTE_OFF:ROUTE_OFF + N_EXPERTS].astype(jnp.int32)
    n_blk = (sizes + MOE_BLOCK - 1) // MOE_BLOCK
    psizes = n_blk * MOE_BLOCK
    poffs = jnp.cumsum(psizes) - psizes
    r4 = route.reshape(T, LANES)[:, :4]
    sel = r4[:, :2, None] == jnp.arange(N_EXPERTS, dtype=jnp.int32)[None, None, :]
    dest = r4[:, 2:4] + jnp.sum(jnp.where(sel, poffs[None, None, :], 0), axis=-1)
    xb = _sc_scatter_rows(h2.reshape(T, D // 2), dest[:, 0], dest[:, 1], P)
    yb = _experts(xb, poffs // MOE_BLOCK, n_blk, sizes, w1, w3, w2)
    out = None
    for b in range(B):
        slot_major = dest[b * S:(b + 1) * S].T.reshape(2 * S)
        rows = _sc_gather_rows(yb, slot_major).reshape(2, S, yb.shape[1])
        out = _combine_dense(rows, wts.reshape(T, LANES), x1.reshape(T, D), mod, ln2_g, ln2_b, S, b, out)
    return out.reshape(B, S, D)


def _layer(x, c, w_ada, b_ada, w_in, conv_w, conv_b, fw1, fb1, ff1, fw2, fb2, ff2, fw3, decay, skip,
           w_hy_o, w_attn_o, attn_sink, w_out, ln1_g, ln1_b, rg_w, rg_b, re_w, re_b, ew1, ew3, ew2,
           ln2_g, ln2_b):
    mod = _ada(c, w_ada, b_ada)
    q, kv, hv, hx1, hx2, ga, gh = _in_proj(x, mod, w_in, conv_w, conv_b)
    attn = _attention(q, kv, attn_sink)
    hy = _hyena(hv, hx1, hx2, fw1, fb1, ff1, fw2, fb2, ff2, fw3, decay, skip)
    x1, h2, route, wts, counts = _merge(attn, hy, ga, gh, x, mod, w_attn_o, w_hy_o, w_out,
                                        ln1_g, ln1_b, rg_w, rg_b, re_w, re_b)
    return _moe(h2, x1, route, wts, counts, mod, ew1, ew3, ew2, ln2_g, ln2_b)


def kernel(x, c, w_ada, b_ada, w_in, conv_w, conv_b, filt_w1, filt_b1, filt_freq1, filt_w2, filt_b2, filt_freq2, filt_w3, filt_decay, hy_skip, w_hy_o, w_attn_o, attn_sink, w_out, ln1_g, ln1_b, router_group_w, router_group_b, router_expert_w, router_expert_b, exp_w1, exp_w3, exp_w2, ln2_g, ln2_b):
    for l in range(w_ada.shape[0]):
        x = _layer(x, c, w_ada[l], b_ada[l], w_in[l], conv_w[l], conv_b[l], filt_w1[l], filt_b1[l],
                   filt_freq1[l], filt_w2[l], filt_b2[l], filt_freq2[l], filt_w3[l], filt_decay[l],
                   hy_skip[l], w_hy_o[l], w_attn_o[l], attn_sink[l], w_out[l], ln1_g[l], ln1_b[l],
                   router_group_w[l], router_group_b[l], router_expert_w[l], router_expert_b[l],
                   exp_w1[l], exp_w3[l], exp_w2[l], ln2_g[l], ln2_b[l])
    return x
```

```python
import functools
import math

import numpy as np
import jax
import jax.numpy as jnp
from jax import lax
from jax.experimental import pallas as pl
from jax.experimental.pallas import tpu as pltpu
from jax.experimental.pallas import tpu_sc as plsc

F32 = jnp.float32
BF16 = jnp.bfloat16

N_HEADS = 8
N_KV_HEADS = 2
HEAD_DIM = 64
ATTN_WIDTH = N_HEADS * HEAD_DIM
KV_WIDTH = N_KV_HEADS * HEAD_DIM
WINDOW = 128
BLOCK_Q = 128
HYENA_WIDTH = 512
FILTER_EMB = 33
FILTER_BANDS = (FILTER_EMB - 1) // 2
WINDOW_SHIFT = 0.05
N_GROUPS = 8
EXPERTS_PER_GROUP = 8
N_EXPERTS = N_GROUPS * EXPERTS_PER_GROUP
D_EXPERT = 512
MOE_BLOCK = 512
LN_EPS = 1e-5
DEPTH = 1
DN_ALPHA = (2.0 * DEPTH) ** 0.25
NEG = -1e30

LANES = 128
SUBLANES = 8
ROUTE_OFF = N_GROUPS
VMEM_LIMIT = 56 * 1024 * 1024


def _params(n_axes, vmem=VMEM_LIMIT):
    return pltpu.CompilerParams(dimension_semantics=("arbitrary",) * n_axes, vmem_limit_bytes=vmem)


def _split(a):
    hi = a.astype(BF16)
    lo = (a - hi.astype(F32)).astype(BF16)
    return hi, lo


def _dot3(a, b_hi, b_lo):
    a_hi, a_lo = _split(a)
    acc = jnp.dot(a_hi, b_hi, preferred_element_type=F32)
    acc = acc + jnp.dot(a_hi, b_lo, preferred_element_type=F32)
    acc = acc + jnp.dot(a_lo, b_hi, preferred_element_type=F32)
    return acc


def _pack2(a, b):
    ia = lax.bitcast_convert_type(a.astype(BF16).astype(F32), jnp.int32)
    ib = lax.bitcast_convert_type(b.astype(BF16).astype(F32), jnp.int32)
    return lax.bitcast_convert_type(ia | lax.shift_right_logical(ib, 16), F32)


def _unpack2(p):
    p = lax.bitcast_convert_type(p, jnp.int32)
    a = lax.bitcast_convert_type(p & jnp.int32(-65536), F32)
    b = lax.bitcast_convert_type(lax.shift_left(p, 16), F32)
    return a, b


def _sigmoid(x):
    return 0.5 * jnp.tanh(0.5 * x) + 0.5


def _layer_norm(r, g, b):
    mu = jnp.mean(r, axis=-1, keepdims=True)
    d = r - mu
    var = jnp.mean(d * d, axis=-1, keepdims=True)
    return d * lax.rsqrt(var + LN_EPS) * g + b


def _ada_kernel(c_ref, wh_ref, wl_ref, b_ref, o_ref):
    c = c_ref[...]
    s = c * _sigmoid(c)
    o_ref[...] = _dot3(s, wh_ref[...], wl_ref[...]) + b_ref[...]


def _ada(c, w_ada, b_ada):
    B, D = c.shape
    n_out = w_ada.shape[1]
    rows = 8
    cp = jnp.zeros((rows, D), F32).at[:B].set(c)
    wh, wl = _split(w_ada)
    tn = 1024
    out = pl.pallas_call(
        _ada_kernel,
        out_shape=jax.ShapeDtypeStruct((rows, n_out), F32),
        grid=(n_out // tn,),
        in_specs=[pl.BlockSpec((rows, D), lambda j: (0, 0)),
                  pl.BlockSpec((D, tn), lambda j: (0, j)),
                  pl.BlockSpec((D, tn), lambda j: (0, j)),
                  pl.BlockSpec((1, tn), lambda j: (0, j))],
        out_specs=pl.BlockSpec((rows, tn), lambda j: (0, j)),
        compiler_params=_params(1),
        name="ada",
    )(cp, wh, wl, b_ada.reshape(1, n_out))
    return out[:B].reshape(B, 6, D)


def _inproj_kernel(x_ref, xp_ref, xn_ref, mod_ref, w_ref, cw_ref, cb_ref,
                   q_ref, kv_ref, v_ref, x1_ref, x2_ref, ga_ref, gh_ref):
    i = pl.program_id(1)
    n = pl.num_programs(1)
    C = HYENA_WIDTH
    x = x_ref[0]
    tm, D = x.shape
    shift = mod_ref[0, 0:1, :]
    scale = mod_ref[0, 1:2, :]
    h = (x * (1.0 + scale) + shift).astype(BF16)

    def seg(lo, hi):
        return jnp.dot(h, w_ref[:, lo:hi], preferred_element_type=F32)

    o_q = 0
    o_kv = o_q + ATTN_WIDTH
    o_hy = o_kv + 2 * KV_WIDTH
    o_ga = o_hy + 3 * C
    o_gh = o_ga + D
    ga_ref[0] = _sigmoid(seg(o_ga, o_ga + D)).astype(BF16)
    gh_ref[0] = _sigmoid(seg(o_gh, o_gh + D)).astype(BF16)

    u = seg(o_hy, o_hy + 3 * C)
    xe = jnp.concatenate([xp_ref[0], xn_ref[0]], axis=0)
    he = (xe * (1.0 + scale) + shift).astype(BF16)
    ue = jnp.dot(he, w_ref[:, o_hy:o_hy + 3 * C], preferred_element_type=F32)
    prow = jnp.where(i > 0, ue[SUBLANES - 1:SUBLANES], 0.0)
    nrow = jnp.where(i < n - 1, ue[SUBLANES:SUBLANES + 1], 0.0)
    rid = lax.broadcasted_iota(jnp.int32, (tm, 1), 0)
    up = jnp.where(rid == 0, prow, pltpu.roll(u, 1, 0))
    dn = jnp.where(rid == tm - 1, nrow, pltpu.roll(u, tm - 1, 0))
    conv = cw_ref[0:1, :] * up + cw_ref[1:2, :] * u + cw_ref[2:3, :] * dn + cb_ref[...]
    v_ref[0] = conv[:, :C]
    x1_ref[0] = conv[:, C:2 * C]
    x2_ref[0] = conv[:, 2 * C:]

    q_ref[0] = (seg(o_q, o_q + ATTN_WIDTH) * (HEAD_DIM ** -0.5)).astype(BF16)
    kv_ref[0] = seg(o_kv, o_kv + 2 * KV_WIDTH).astype(BF16)


def _in_proj(x, mod, w_in, conv_w, conv_b):
    B, S, D = x.shape
    C = HYENA_WIDTH
    tm = min(512, S)
    r8 = tm // SUBLANES
    nb8 = S // SUBLANES
    wb = w_in.astype(BF16)
    nw = wb.shape[1]
    row = lambda b, i: (b, i, 0)
    shapes = [(ATTN_WIDTH, BF16), (2 * KV_WIDTH, BF16), (C, F32), (C, F32), (C, F32), (D, BF16), (D, BF16)]
    return pl.pallas_call(
        _inproj_kernel,
        out_shape=[jax.ShapeDtypeStruct((B, S, w), dt) for w, dt in shapes],
        grid=(B, S // tm),
        in_specs=[pl.BlockSpec((1, tm, D), row),
                  pl.BlockSpec((1, SUBLANES, D), lambda b, i: (b, jnp.maximum(i * r8 - 1, 0), 0)),
                  pl.BlockSpec((1, SUBLANES, D), lambda b, i: (b, jnp.minimum((i + 1) * r8, nb8 - 1), 0)),
                  pl.BlockSpec((1, 6, D), lambda b, i: (b, 0, 0)),
                  pl.BlockSpec((D, nw), lambda b, i: (0, 0)),
                  pl.BlockSpec((3, 3 * C), lambda b, i: (0, 0)),
                  pl.BlockSpec((1, 3 * C), lambda b, i: (0, 0))],
        out_specs=[pl.BlockSpec((1, tm, w), row) for w, _ in shapes],
        compiler_params=_params(2),
        name="in_proj",
    )(x, x, x, mod, wb, conv_w.astype(F32), conv_b.reshape(1, 3 * C).astype(F32))


ATT_TQ = 512


def _attn_kernel(sink_ref, q_ref, kvp_ref, kvc_ref, kvn_ref, bias_ref, o_ref, kv_scr, vx_scr, *, seq_len):
    i = pl.program_id(1)
    Q = BLOCK_Q
    TQ = q_ref.shape[1]
    G = N_HEADS // N_KV_HEADS
    kv_scr[0:Q] = kvp_ref[0]
    kv_scr[Q:Q + TQ] = kvc_ref[0]
    kv_scr[Q + TQ:] = kvn_ref[0]
    for kv in range(N_KV_HEADS):
        vx_scr[:, kv * LANES:kv * LANES + HEAD_DIM] = kv_scr[:, KV_WIDTH + kv * HEAD_DIM:KV_WIDTH + (kv + 1) * HEAD_DIM]
        vx_scr[:, kv * LANES + HEAD_DIM:(kv + 1) * LANES] = jnp.ones((TQ + 2 * Q, LANES - HEAD_DIM), BF16)
    col = lax.broadcasted_iota(jnp.int32, (1, 3 * Q), 1)
    rhead = lax.broadcasted_iota(jnp.int32, (G * Q, 1), 0) // Q
    for j in range(TQ // Q):
        kpos = i * TQ + (j - 1) * Q + col
        colbias = jnp.where((kpos >= 0) & (kpos < seq_len), 0.0, NEG)
        for kv in range(N_KV_HEADS):
            kk = kv_scr[j * Q:(j + 3) * Q, kv * HEAD_DIM:(kv + 1) * HEAD_DIM]
            vx = vx_scr[j * Q:(j + 3) * Q, kv * LANES:(kv + 1) * LANES]
            heads = [kv * G + g for g in range(G)]
            qg = jnp.concatenate([q_ref[0, j * Q:(j + 1) * Q, h * HEAD_DIM:(h + 1) * HEAD_DIM] for h in heads], axis=0)
            s = lax.dot_general(qg, kk, (((1,), (1,)), ((), ())), preferred_element_type=F32)
            s = s + bias_ref[kv]
            if j == 0 or j == TQ // Q - 1:
                s = s + colbias
            snk = jnp.where(rhead == 0, sink_ref[heads[0]],
                            jnp.where(rhead == 1, sink_ref[heads[1]],
                                      jnp.where(rhead == 2, sink_ref[heads[2]], sink_ref[heads[3]])))
            m = jnp.maximum(jnp.max(s, axis=1, keepdims=True), snk)
            p = jnp.exp(s - m).astype(BF16)
            ox = jnp.dot(p, vx, preferred_element_type=F32)
            den = ox[:, HEAD_DIM:HEAD_DIM + 1] + jnp.exp(snk - m)
            o = ox[:, :HEAD_DIM] / den
            for g, h in enumerate(heads):
                o_ref[0, j * Q:(j + 1) * Q, h * HEAD_DIM:(h + 1) * HEAD_DIM] = o[g * Q:(g + 1) * Q].astype(BF16)


def _attention(q, kv, sink):
    B, S, _ = q.shape
    Q = BLOCK_Q
    TQ = min(ATT_TQ, S)
    r = TQ // Q
    nq = S // Q
    G = N_HEADS // N_KV_HEADS
    assert G == 4
    a = jnp.arange(Q)[:, None]
    j = jnp.arange(3 * Q)[None, :]
    rel = jnp.abs(j - Q - a).astype(F32)
    slopes = 2.0 ** (-8.0 * jnp.arange(1, N_HEADS + 1, dtype=F32) / N_HEADS)
    bias = jnp.where(rel[None] <= WINDOW, -slopes[:, None, None] * rel[None], NEG).astype(F32)
    bias = bias.reshape(N_KV_HEADS, G * Q, 3 * Q)
    cur = lambda b, i: (b, i, 0)
    return pl.pallas_call(
        functools.partial(_attn_kernel, seq_len=S),
        out_shape=jax.ShapeDtypeStruct((B, S, ATTN_WIDTH), BF16),
        grid=(B, S // TQ),
        in_specs=[pl.BlockSpec(memory_space=pltpu.SMEM),
                  pl.BlockSpec((1, TQ, ATTN_WIDTH), cur),
                  pl.BlockSpec((1, Q, 2 * KV_WIDTH), lambda b, i: (b, jnp.maximum(i * r - 1, 0), 0)),
                  pl.BlockSpec((1, TQ, 2 * KV_WIDTH), cur),
                  pl.BlockSpec((1, Q, 2 * KV_WIDTH), lambda b, i: (b, jnp.minimum((i + 1) * r, nq - 1), 0)),
                  pl.BlockSpec((N_KV_HEADS, G * Q, 3 * Q), lambda b, i: (0, 0, 0))],
        out_specs=pl.BlockSpec((1, TQ, ATTN_WIDTH), cur),
        scratch_shapes=[pltpu.VMEM((TQ + 2 * Q, 2 * KV_WIDTH), BF16),
                        pltpu.VMEM((TQ + 2 * Q, N_KV_HEADS * LANES), BF16)],
        compiler_params=_params(2),
        name="attn",
    )(sink.astype(F32), q, kv, kv, kv, bias)


def _filter_kernel(z_ref, w1h, w1l, b1_ref, f1_ref, w2h, w2l, b2_ref, f2_ref, w3h, w3l, dec_ref,
                   k_ref, s_ref):
    i = pl.program_id(0)
    z = z_ref[...]
    h1 = jnp.sin(f1_ref[...] * (_dot3(z, w1h[...], w1l[...]) + b1_ref[...]))
    h2 = jnp.sin(f2_ref[...] * (_dot3(h1, w2h[...], w2l[...]) + b2_ref[...]))
    k = _dot3(h2, w3h[...], w3l[...])
    t = z[:, 0:1]
    k = k * (jnp.exp(-t * jnp.abs(dec_ref[...])) + WINDOW_SHIFT)
    k_ref[...] = k

    @pl.when(i == 0)
    def _():
        s_ref[...] = jnp.zeros_like(s_ref)

    s_ref[...] += jnp.sum(jnp.abs(k), axis=0, keepdims=True)


def _filter_embedding(L):
    t = np.linspace(0.0, 1.0, L, dtype=np.float32).astype(np.float64)[:, None]
    w = (2.0 * math.pi * np.arange(L, dtype=np.float32) / np.float32(L)).astype(np.float64)[:, None]
    bands = np.linspace(1e-4, FILTER_BANDS - 1, FILTER_BANDS, dtype=np.float32).astype(np.float64)[None, :]
    bw = (bands.astype(np.float32) * w.astype(np.float32)).astype(np.float64)
    z = np.concatenate([t, np.cos(bw), -np.sin(bw)], axis=-1)
    zp = np.zeros((L, LANES), np.float32)
    zp[:, :FILTER_EMB] = z.astype(np.float32)
    return jnp.asarray(zp)


def _pad2(a, r, c):
    return jnp.zeros((r, c), F32).at[:a.shape[0], :a.shape[1]].set(a.astype(F32))


def _filters(L, fw1, fb1, ff1, fw2, fb2, ff2, fw3, decay):
    H = LANES
    nf = fw3.shape[1]
    z = _filter_embedding(L)
    w1h, w1l = _split(_pad2(fw1, H, H))
    w2h, w2l = _split(_pad2(fw2, H, H))
    w3h, w3l = _split(_pad2(fw3, H, nf))
    b1 = _pad2(fb1[None], 1, H)
    f1 = _pad2(ff1[None], 1, H)
    b2 = _pad2(fb2[None], 1, H)
    f2 = _pad2(ff2[None], 1, H)
    tr = min(512, L)
    full = lambda r, c: pl.BlockSpec((r, c), lambda i: (0, 0))
    return pl.pallas_call(
        _filter_kernel,
        out_shape=[jax.ShapeDtypeStruct((L, nf), F32), jax.ShapeDtypeStruct((1, nf), F32)],
        grid=(L // tr,),
        in_specs=[pl.BlockSpec((tr, H), lambda i: (i, 0)),
                  full(H, H), full(H, H), full(1, H), full(1, H),
                  full(H, H), full(H, H), full(1, H), full(1, H),
                  full(H, nf), full(H, nf), full(1, nf)],
        out_specs=[pl.BlockSpec((tr, nf), lambda i: (i, 0)), full(1, nf)],
        compiler_params=_params(1),
        name="filter",
    )(z, w1h, w1l, b1, f1, w2h, w2l, b2, f2, w3h, w3l, decay.reshape(1, nf).astype(F32))


def _np_bf16(m64):
    return jnp.asarray(m64.astype(np.float32).astype(BF16))


def _dft_constants(L):
    N = 2 * L
    n2 = LANES
    n1 = N // n2
    h1 = n1 // 2
    k1 = np.arange(n1)[:, None]
    s1 = np.arange(h1)[None, :]
    ang = -2.0 * np.pi * ((k1 * s1) % n1) / n1
    wr, wi = np.cos(ang), np.sin(ang)
    w1_filt = np.block([[wr, wr], [wi, wi], [wr, -wr], [wi, -wi]])
    w1_cplx = np.block([[wr, -wi], [wi, wr]])
    vr, vi = wr.T / N, -wi.T / N
    w3 = np.block([[vr, -vi], [vi, vr]])
    k2 = np.arange(n2)[:, None]
    s2 = np.arange(n2)[None, :]
    a2 = -2.0 * np.pi * ((k2 * s2) % n2) / n2
    w2r, w2i = jnp.asarray(np.cos(a2), F32), jnp.asarray(np.sin(a2), F32)
    at = -2.0 * np.pi * ((np.arange(n1)[:, None] * s2) % N) / N
    twr, twi = jnp.asarray(np.cos(at), F32), jnp.asarray(np.sin(at), F32)
    mr = w2r[None] * twr[:, None, :] - w2i[None] * twi[:, None, :]
    mi = w2r[None] * twi[:, None, :] + w2i[None] * twr[:, None, :]
    fwd = jnp.concatenate([jnp.concatenate([mr, -mi], axis=2),
                           jnp.concatenate([mi, mr], axis=2)], axis=1)
    fwd = fwd.astype(BF16)
    return dict(n1=n1, w1_filt=_np_bf16(w1_filt), w1_cplx=_np_bf16(w1_cplx), w3=_np_bf16(w3),
                fwd=fwd)


SCH = 8


def _dft1_kernel(x_ref, w_ref, a_ref, *, n1):
    w = w_ref[...]
    for j in range(SCH):
        rhs = jnp.concatenate([x_ref[0, 0, :, j, :], x_ref[0, 1, :, j, :]], axis=0)
        res = jnp.dot(w, rhs.astype(BF16), preferred_element_type=F32)
        a_ref[0, :, j, :] = _pack2(res[:n1], res[n1:])


def _dft1_data(x, consts):
    B, L, C = x.shape
    n1 = consts["n1"]
    h1 = n1 // 2
    xv = x.reshape(B // 2, 2, h1, LANES, C)
    return pl.pallas_call(
        functools.partial(_dft1_kernel, n1=n1),
        out_shape=jax.ShapeDtypeStruct((B // 2, n1, LANES, C), F32),
        grid=(B // 2, LANES // SCH),
        in_specs=[pl.BlockSpec((1, 2, h1, SCH, C), lambda p, j: (p, 0, 0, j, 0)),
                  pl.BlockSpec((2 * n1, n1), lambda p, j: (0, 0))],
        out_specs=pl.BlockSpec((1, n1, SCH, C), lambda p, j: (p, 0, j, 0)),
        compiler_params=_params(2),
        name="dft1",
    )(xv, consts["w1_cplx"])


def _dft1f_kernel(x_ref, w_ref, a_ref, *, n1):
    C = HYENA_WIDTH
    w = w_ref[...]
    for j in range(SCH):
        rhs = jnp.concatenate([x_ref[:, j, :C], x_ref[:, j, C:]], axis=0)
        res = jnp.dot(w, rhs.astype(BF16), preferred_element_type=F32)
        a_ref[0, :, 0, j, :] = _pack2(res[:n1], res[n1:2 * n1])
        a_ref[0, :, 1, j, :] = _pack2(res[2 * n1:3 * n1], res[3 * n1:])


def _dft1_filter(kraw, consts):
    L, nf = kraw.shape
    C = HYENA_WIDTH
    n_ord = nf // (2 * C)
    n1 = consts["n1"]
    h1 = n1 // 2
    kv = kraw.reshape(h1, LANES, nf)
    return pl.pallas_call(
        functools.partial(_dft1f_kernel, n1=n1),
        out_shape=jax.ShapeDtypeStruct((n_ord, n1, 2, LANES, C), F32),
        grid=(n_ord, LANES // SCH),
        in_specs=[pl.BlockSpec((h1, SCH, 2 * C), lambda o, j: (0, j, o)),
                  pl.BlockSpec((4 * n1, n1), lambda o, j: (0, 0))],
        out_specs=pl.BlockSpec((1, n1, 2, SCH, C), lambda o, j: (o, 0, 0, j, 0)),
        compiler_params=_params(2),
        name="dft1f",
    )(kv, consts["w1_filt"])


KCH = 8


def _midf_kernel(a_ref, f_ref, inv_ref, b0_ref, h_ref):
    n2 = LANES
    sc = inv_ref[0]
    for k in range(KCH):
        p = jnp.concatenate(_unpack2(a_ref[0, k, :n2, :]), axis=0).astype(BF16)
        q = jnp.concatenate(_unpack2(a_ref[0, k, n2:, :]), axis=0).astype(BF16)
        h_re = jnp.dot(f_ref[k, :n2, :], p, preferred_element_type=F32)
        h_im = jnp.dot(f_ref[k, n2:, :], q, preferred_element_type=F32)
        h_ref[0, k] = _pack2((h_re - b0_ref[0]) * sc, h_im * sc)


def _filter_spectrum(af, inv_den, bwd0, consts):
    n_ord, n1, _, n2, C = af.shape
    a = af.reshape(n_ord, n1, 2 * n2, C)
    tab = pl.BlockSpec((KCH, 2 * n2, 2 * n2), lambda k, o: (k, 0, 0))
    vec = pl.BlockSpec((1, 1, C), lambda k, o: (o, 0, 0))
    return pl.pallas_call(
        _midf_kernel,
        out_shape=jax.ShapeDtypeStruct((n_ord, n1, n2, C), F32),
        grid=(n1 // KCH, n_ord),
        in_specs=[pl.BlockSpec((1, KCH, 2 * n2, C), lambda k, o: (o, k, 0, 0)), tab, vec, vec],
        out_specs=pl.BlockSpec((1, KCH, n2, C), lambda k, o: (o, k, 0, 0)),
        compiler_params=_params(2),
        name="midf",
    )(a, consts["fwd"], inv_den, bwd0)


def _mid_kernel(a_ref, f_ref, h_ref, b_ref):
    n2 = LANES
    for k in range(KCH):
        a = jnp.concatenate(_unpack2(a_ref[0, k]), axis=0).astype(BF16)
        x = jnp.dot(f_ref[k], a, preferred_element_type=F32)
        xr, xi = x[:n2], x[n2:]
        hr, hi = _unpack2(h_ref[0, k])
        y = jnp.concatenate([xr * hr - xi * hi, xr * hi + xi * hr], axis=0)
        b = lax.dot_general(f_ref[k], y.astype(BF16), (((0,), (0,)), ((), ())), preferred_element_type=F32)
        b_ref[0, k] = _pack2(b[:n2], b[n2:])


def _mid(a, hspec, order, consts):
    P, n1, n2, C = a.shape
    tab = pl.BlockSpec((KCH, 2 * n2, 2 * n2), lambda k, p: (k, 0, 0))
    return pl.pallas_call(
        _mid_kernel,
        out_shape=jax.ShapeDtypeStruct((P, n1, n2, C), F32),
        grid=(n1 // KCH, P),
        in_specs=[pl.BlockSpec((1, KCH, n2, C), lambda k, p: (p, k, 0, 0)),
                  tab,
                  pl.BlockSpec((1, KCH, n2, C), lambda k, p: (order, k, 0, 0))],
        out_specs=pl.BlockSpec((1, KCH, n2, C), lambda k, p: (p, k, 0, 0)),
        compiler_params=_params(2),
        name="mid",
    )(a, consts["fwd"], hspec)


def _dft3_kernel(b_ref, w_ref, v_ref, g_ref, skip_ref, *rest, h1, chain):
    if chain:
        w1_ref, z_ref, a_ref, slab_ref = rest
        w1 = w1_ref[...]
    else:
        z_ref, slab_ref = rest
    w = w_ref[...]
    skip = skip_ref[0]
    n1 = 2 * h1
    for j in range(SCH):
        slab_ref[...] = b_ref[0, :, j, :]
        rhs = jnp.concatenate(_unpack2(slab_ref[...]), axis=0)
        y = jnp.dot(w, rhs.astype(BF16), preferred_element_type=F32)
        z = [g_ref[0, r, :, j, :] * (y[r * h1:(r + 1) * h1] + v_ref[0, r, :, j, :] * skip) for r in range(2)]
        for r in range(2):
            z_ref[0, r, :, j, :] = z[r]
        if chain:
            res = jnp.dot(w1, jnp.concatenate(z, axis=0).astype(BF16), preferred_element_type=F32)
            a_ref[0, :, j, :] = _pack2(res[:n1], res[n1:])


def _dft3_gate(b5, v, gate, skip, consts, chain):
    P, n1, n2, C = b5.shape
    h1 = n1 // 2
    B, L, _ = v.shape
    five = lambda t: t.reshape(P, 2, h1, n2, C)
    dat = pl.BlockSpec((1, 2, h1, SCH, C), lambda p, j: (p, 0, 0, j, 0))
    packed = pl.BlockSpec((1, n1, SCH, C), lambda p, j: (p, 0, j, 0))
    in_specs = [packed, pl.BlockSpec((n1, 2 * n1), lambda p, j: (0, 0)), dat, dat,
                pl.BlockSpec((1, C), lambda p, j: (0, 0))]
    args = [b5, consts["w3"], five(v), five(gate), skip.reshape(1, C).astype(F32)]
    out_shape = [jax.ShapeDtypeStruct((P, 2, h1, n2, C), F32)]
    out_specs = [dat]
    if chain:
        in_specs.append(pl.BlockSpec((2 * n1, n1), lambda p, j: (0, 0)))
        args.append(consts["w1_cplx"])
        out_shape.append(jax.ShapeDtypeStruct((P, n1, n2, C), F32))
        out_specs.append(packed)
    outs = pl.pallas_call(
        functools.partial(_dft3_kernel, h1=h1, chain=chain),
        out_shape=out_shape,
        grid=(P, n2 // SCH),
        in_specs=in_specs,
        out_specs=out_specs,
        scratch_shapes=[pltpu.VMEM((n1, C), F32)],
        compiler_params=_params(2),
        name="dft3",
    )(*args)
    z = outs[0].reshape(B, L, C)
    return (z, outs[1]) if chain else (z, None)


def _hyena(v, x1, x2, fw1, fb1, ff1, fw2, fb2, ff2, fw3, decay, skip):
    B, L, C = v.shape
    consts = _dft_constants(L)
    kraw, ksum = _filters(L, fw1, fb1, ff1, fw2, fb2, ff2, fw3, decay)
    ks = ksum.reshape(2, 2, C)
    inv_den = (1.0 / (ks[:, 0] + ks[:, 1])).reshape(2, 1, C)
    bwd0 = kraw[0].reshape(2, 2, C)[:, 1].reshape(2, 1, C)
    hspec = _filter_spectrum(_dft1_filter(kraw, consts), inv_den, bwd0, consts)
    gates = (x1, x2)
    z, a5 = v, _dft1_data(v, consts)
    for o, gate in enumerate(gates):
        b5 = _mid(a5, hspec, o, consts)
        z, a5 = _dft3_gate(b5, z, gate, skip[o], consts, chain=o + 1 < len(gates))
    return z


def _merge_kernel(attn_ref, hy_ref, ga_ref, gh_ref, x_ref, mod_ref, wa_ref, wh_ref, wo_ref,
                  g1_ref, b1_ref, rwh_ref, rwl_ref, rb_ref, tri_ref,
                  x1_ref, h2_ref, route_ref, wts_ref, cnt_ref, carry_ref):
    @pl.when((pl.program_id(0) == 0) & (pl.program_id(1) == 0))
    def _():
        carry_ref[...] = jnp.zeros_like(carry_ref)

    logits = _merge_dense(attn_ref, hy_ref, ga_ref, gh_ref, x_ref, mod_ref, wa_ref, wh_ref, wo_ref,
                          g1_ref, b1_ref, rwh_ref, rwl_ref, rb_ref, x1_ref, h2_ref)
    _route_rows(logits, tri_ref, route_ref, wts_ref, carry_ref)
    cnt_ref[...] = carry_ref[...]


def _merge_dense(attn_ref, hy_ref, ga_ref, gh_ref, x_ref, mod_ref, wa_ref, wh_ref, wo_ref,
                 g1_ref, b1_ref, rwh_ref, rwl_ref, rb_ref, x1_ref, h2_ref):
    a = jnp.dot(attn_ref[0], wa_ref[...], preferred_element_type=F32)
    hy = jnp.dot(hy_ref[0].astype(BF16), wh_ref[...], preferred_element_type=F32)
    merged = ga_ref[0].astype(F32) * a + gh_ref[0].astype(F32) * hy
    y = jnp.dot(merged.astype(BF16), wo_ref[...], preferred_element_type=F32)
    gate1 = mod_ref[0, 2:3, :]
    shift2 = mod_ref[0, 3:4, :]
    scale2 = mod_ref[0, 4:5, :]
    x1 = _layer_norm(DN_ALPHA * x_ref[0] + gate1 * y, g1_ref[...], b1_ref[...])
    x1_ref[0] = x1
    h2 = x1 * (1.0 + scale2) + shift2
    half = h2.shape[1] // 2
    h2_ref[0] = _pack2(h2[:, :half], h2[:, half:])
    return _dot3(h2, rwh_ref[...], rwl_ref[...]) + rb_ref[...]


def _route_rows(logits, tri_ref, route_ref, wts_ref, carry_ref):
    tm = logits.shape[0]
    lane = lax.broadcasted_iota(jnp.int32, (tm, LANES), 1)
    lanef = lane.astype(F32)
    big = float(LANES)

    def first_lane(mask):
        return jnp.min(jnp.where(mask, lanef, big), axis=1, keepdims=True).astype(jnp.int32)

    gmask = lane < N_GROUPS
    gl = jnp.where(gmask, logits, NEG)
    gmax = jnp.max(gl, axis=1, keepdims=True)
    gidx = first_lane(gl == gmax)
    pg = 1.0 / jnp.sum(jnp.exp(gl - gmax), axis=1, keepdims=True)
    lo = ROUTE_OFF + gidx * EXPERTS_PER_GROUP
    emask = (lane >= lo) & (lane < lo + EXPERTS_PER_GROUP)
    el = jnp.where(emask, logits, NEG)
    v1 = jnp.max(el, axis=1, keepdims=True)
    i1 = first_lane(el == v1)
    el2 = jnp.where(emask & (lane != i1), logits, NEG)
    v2 = jnp.max(el2, axis=1, keepdims=True)
    i2 = first_lane(el2 == v2)
    e21 = jnp.exp(v2 - v1)
    w1 = pg / (1.0 + e21)
    w2 = pg * e21 / (1.0 + e21)

    sel1 = lane == i1
    sel2 = lane == i2
    onehot = jnp.where(sel1 | sel2, 1.0, 0.0)
    prefix = jnp.dot(tri_ref[...], onehot.astype(BF16), preferred_element_type=F32) + carry_ref[...]
    r1 = jnp.sum(jnp.where(sel1, prefix, 0.0), axis=1, keepdims=True)
    r2 = jnp.sum(jnp.where(sel2, prefix, 0.0), axis=1, keepdims=True)
    carry_ref[...] += jnp.sum(onehot, axis=0, keepdims=True)

    ranks = jnp.where(lane == 2, r1, jnp.where(lane == 3, r2, 0.0)).astype(jnp.int32)
    route_ref[0] = jnp.where(lane == 0, i1 - ROUTE_OFF, jnp.where(lane == 1, i2 - ROUTE_OFF, ranks))
    wts_ref[0] = jnp.where(lane == 0, w1, jnp.where(lane == 1, w2, 0.0))


def _merge(attn, hy, ga, gh, x, mod, w_attn_o, w_hy_o, w_out, ln1_g, ln1_b, rg_w, rg_b, re_w, re_b):
    B, S, D = x.shape
    tm = min(512, S)
    rw = jnp.zeros((D, LANES), F32).at[:, :N_GROUPS].set(rg_w).at[:, ROUTE_OFF:ROUTE_OFF + N_EXPERTS].set(re_w)
    rb = jnp.zeros((1, LANES), F32).at[0, :N_GROUPS].set(rg_b).at[0, ROUTE_OFF:ROUTE_OFF + N_EXPERTS].set(re_b)
    rwh, rwl = _split(rw)
    tri = (jnp.arange(tm)[:, None] > jnp.arange(tm)[None, :]).astype(BF16)
    row = lambda b, i: (b, i, 0)
    full = lambda r, c: pl.BlockSpec((r, c), lambda b, i: (0, 0))
    outs = [jax.ShapeDtypeStruct((B, S, D), F32), jax.ShapeDtypeStruct((B, S, D // 2), F32),
            jax.ShapeDtypeStruct((B, S, LANES), jnp.int32), jax.ShapeDtypeStruct((B, S, LANES), F32),
            jax.ShapeDtypeStruct((1, LANES), F32)]
    return pl.pallas_call(
        _merge_kernel,
        out_shape=outs,
        grid=(B, S // tm),
        in_specs=[pl.BlockSpec((1, tm, ATTN_WIDTH), row), pl.BlockSpec((1, tm, HYENA_WIDTH), row),
                  pl.BlockSpec((1, tm, D), row), pl.BlockSpec((1, tm, D), row), pl.BlockSpec((1, tm, D), row),
                  pl.BlockSpec((1, 6, D), lambda b, i: (b, 0, 0)),
                  full(ATTN_WIDTH, D), full(HYENA_WIDTH, D), full(D, D),
                  full(1, D), full(1, D), full(D, LANES), full(D, LANES), full(1, LANES), full(tm, tm)],
        out_specs=[pl.BlockSpec((1, tm, D), row), pl.BlockSpec((1, tm, D // 2), row),
                   pl.BlockSpec((1, tm, LANES), row), pl.BlockSpec((1, tm, LANES), row), full(1, LANES)],
        scratch_shapes=[pltpu.VMEM((1, LANES), F32)],
        compiler_params=_params(2),
        name="merge",
    )(attn, hy, ga, gh, x, mod, w_attn_o.astype(BF16), w_hy_o.astype(BF16), w_out.astype(BF16),
      ln1_g.reshape(1, D), ln1_b.reshape(1, D), rwh, rwl, rb, tri)


SC_ROWS = 64


def _sc_workers():
    info = plsc.get_sparse_core_info()
    return info.num_cores, info.num_cores * info.num_subcores


def _sc_split(n):
    _, workers = _sc_workers()
    per_worker = n // workers
    chunks = per_worker // SC_ROWS
    assert per_worker * workers == n and chunks * SC_ROWS == per_worker and chunks % 2 == 0
    return workers, per_worker, chunks


def _sc_scatter_rows(src, idx0, idx1, n_out):
    n, width = src.shape
    nc, _ = _sc_workers()
    workers, per_worker, chunks = _sc_split(n)
    mesh = plsc.VectorSubcoreMesh(core_axis_name="c", subcore_axis_name="s")

    def body(src_hbm, i0_hbm, i1_hbm, out_hbm, i0_v, i1_v, rows_v, sem, ssem):
        wid = lax.axis_index("s") * nc + lax.axis_index("c")
        base = wid * per_worker
        pltpu.sync_copy(i0_hbm.at[wid], i0_v)
        pltpu.sync_copy(i1_hbm.at[wid], i1_v)

        def load(chunk, buf):
            return pltpu.make_async_copy(src_hbm.at[pl.ds(base + chunk * SC_ROWS, SC_ROWS)], rows_v.at[buf], sem)

        load(0, 0).start()

        @pl.loop(0, chunks, step=2)
        def _(c):
            for b in range(2):
                chunk = c + b
                load(chunk, b).wait()

                @pl.when(chunk + 1 < chunks)
                def _():
                    load(chunk + 1, 1 - b).start()

                first = pltpu.make_async_copy(rows_v.at[b], out_hbm.at[i0_v.at[chunk]], ssem)
                second = pltpu.make_async_copy(rows_v.at[b], out_hbm.at[i1_v.at[chunk]], ssem)
                first.start()
                second.start()
                first.wait()
                second.wait()

    shaped = lambda i: i.reshape(workers, chunks, SC_ROWS)
    return pl.kernel(
        body,
        out_type=jax.ShapeDtypeStruct((n_out, width), src.dtype),
        mesh=mesh,
        scratch_types=[pltpu.VMEM((chunks, SC_ROWS), jnp.int32),
                       pltpu.VMEM((chunks, SC_ROWS), jnp.int32),
                       pltpu.VMEM((2, SC_ROWS, width), src.dtype),
                       pltpu.SemaphoreType.DMA, pltpu.SemaphoreType.DMA],
        name="sc_scatter",
    )(src, shaped(idx0), shaped(idx1))


def _expert_kernel(first_ref, nb_ref, sz_ref, tot_ref, w1_ref, w3_ref, w2_ref, xb_ref, yb_ref,
                   xbuf, ybuf, c1_ref, c3_ref, c2_ref, lsem, ssem):
    e = pl.program_id(0)
    nb = nb_ref[e]
    first = first_ref[e]
    total = tot_ref[0]
    rows = xbuf.shape[1]

    def load(g, slot):
        src = xb_ref.at[pl.ds(pl.multiple_of(g * rows, rows), rows)]
        return pltpu.make_async_copy(src, xbuf.at[slot], lsem.at[slot])

    def store(g, slot):
        dst = yb_ref.at[pl.ds(pl.multiple_of(g * rows, rows), rows)]
        return pltpu.make_async_copy(ybuf.at[slot], dst, ssem.at[slot])

    @pl.when((e == 0) & (total > 0))
    def _():
        load(0, 0).start()

    @pl.when(nb > 0)
    def _():
        c1_ref[...] = w1_ref[0].astype(BF16)
        c3_ref[...] = w3_ref[0].astype(BF16)
        c2_ref[...] = w2_ref[0].astype(BF16)

        def block(j, carry):
            g = first + j
            slot = lax.rem(g, 2)
            load(g, slot).wait()

            @pl.when(g + 1 < total)
            def _():
                load(g + 1, 1 - slot).start()

            @pl.when(g >= 2)
            def _():
                store(g - 2, slot).wait()

            n_valid = sz_ref[e] - j * rows
            rid = lax.broadcasted_iota(jnp.int32, (rows, 1), 0)
            xa, xb = _unpack2(jnp.where(rid < n_valid, xbuf[slot], 0.0))
            x = jnp.concatenate([xa, xb], axis=1).astype(BF16)
            a = jnp.dot(x, c1_ref[...], preferred_element_type=F32)
            gate = jnp.dot(x, c3_ref[...], preferred_element_type=F32)
            hmid = (a * _sigmoid(a) * gate).astype(BF16)
            y = jnp.dot(hmid, c2_ref[...], preferred_element_type=F32)
            half = y.shape[1] // 2
            ybuf[slot] = _pack2(y[:, :half], y[:, half:])
            store(g, slot).start()
            return carry

        lax.fori_loop(0, nb, block, 0)

    @pl.when(e == pl.num_programs(0) - 1)
    def _():
        for back in (2, 1):
            @pl.when(total >= back)
            def _():
                g = total - back
                store(g, lax.rem(g, 2)).wait()


def _experts(xb, first_blk, n_blk, sizes, w1, w3, w2):
    P, W = xb.shape
    E, D, DE = w1.shape
    total = jnp.sum(n_blk, keepdims=True)
    wspec = lambda r, c: pl.BlockSpec((1, r, c), lambda e, *_: (e, 0, 0))
    grid_spec = pltpu.PrefetchScalarGridSpec(
        num_scalar_prefetch=4,
        grid=(E,),
        in_specs=[wspec(D, DE), wspec(D, DE), wspec(DE, D), pl.BlockSpec(memory_space=pl.ANY)],
        out_specs=pl.BlockSpec(memory_space=pl.ANY),
        scratch_shapes=[pltpu.VMEM((2, MOE_BLOCK, W), F32), pltpu.VMEM((2, MOE_BLOCK, W), F32),
                        pltpu.VMEM((D, DE), BF16), pltpu.VMEM((D, DE), BF16), pltpu.VMEM((DE, D), BF16),
                        pltpu.SemaphoreType.DMA((2,)), pltpu.SemaphoreType.DMA((2,))],
    )
    return pl.pallas_call(
        _expert_kernel,
        out_shape=jax.ShapeDtypeStruct((P, W), F32),
        grid_spec=grid_spec,
        compiler_params=_params(1),
        name="experts",
    )(first_blk, n_blk, sizes, total, w1, w3, w2, xb)


def _sc_gather_rows(table, idx):
    n, width = idx.shape[0], table.shape[1]
    nc, _ = _sc_workers()
    workers, per_worker, chunks = _sc_split(n)
    mesh = plsc.VectorSubcoreMesh(core_axis_name="c", subcore_axis_name="s")

    def body(table_hbm, idx_hbm, out_hbm, idx_v, rows_v, sem):
        wid = lax.axis_index("s") * nc + lax.axis_index("c")
        base = wid * per_worker
        pltpu.sync_copy(idx_hbm.at[wid], idx_v)

        def gather(chunk, buf):
            return pltpu.make_async_copy(table_hbm.at[idx_v.at[chunk]], rows_v.at[buf], sem)

        gather(0, 0).start()

        @pl.loop(0, chunks, step=2)
        def _(c):
            for b in range(2):
                chunk = c + b
                gather(chunk, b).wait()

                @pl.when(chunk + 1 < chunks)
                def _():
                    gather(chunk + 1, 1 - b).start()

                pltpu.sync_copy(rows_v.at[b], out_hbm.at[pl.ds(base + chunk * SC_ROWS, SC_ROWS)])

    return pl.kernel(
        body,
        out_type=jax.ShapeDtypeStruct((n, width), table.dtype),
        mesh=mesh,
        scratch_types=[pltpu.VMEM((chunks, SC_ROWS), jnp.int32),
                       pltpu.VMEM((2, SC_ROWS, width), table.dtype),
                       pltpu.SemaphoreType.DMA],
        name="sc_gather",
    )(table, idx.reshape(workers, chunks, SC_ROWS))


def _combine_dense_kernel(r0_ref, r1_ref, wts_ref, x1_ref, mod_ref, g_ref, b_ref, *rest):
    o_ref = rest[-1]
    w = wts_ref[...]
    y0 = jnp.concatenate(_unpack2(r0_ref[0]), axis=1)
    y1 = jnp.concatenate(_unpack2(r1_ref[0]), axis=1)
    y = w[:, 0:1] * y0 + w[:, 1:2] * y1
    gate2 = mod_ref[0, 5:6, :]
    o_ref[...] = _layer_norm(DN_ALPHA * x1_ref[...] + gate2 * y, g_ref[...], b_ref[...])


def _combine_dense(rows, wts, x1, mod, ln2_g, ln2_b, S, b, out):
    T, D = x1.shape
    tm = min(512, S)
    per_b = S // tm
    here = lambda i: (b * per_b + i, 0)
    in_specs = [pl.BlockSpec((1, tm, rows.shape[2]), lambda i: (0, i, 0)),
                pl.BlockSpec((1, tm, rows.shape[2]), lambda i: (1, i, 0)),
                pl.BlockSpec((tm, LANES), here),
                pl.BlockSpec((tm, D), here),
                pl.BlockSpec((1, 6, D), lambda i: (b, 0, 0)),
                pl.BlockSpec((1, D), lambda i: (0, 0)),
                pl.BlockSpec((1, D), lambda i: (0, 0))]
    args = [rows, rows, wts, x1, mod, ln2_g.reshape(1, D), ln2_b.reshape(1, D)]
    aliases = {}
    if out is not None:
        in_specs.append(pl.BlockSpec(memory_space=pl.ANY))
        aliases = {len(args): 0}
        args.append(out)
    return pl.pallas_call(
        _combine_dense_kernel,
        out_shape=jax.ShapeDtypeStruct((T, D), F32),
        grid=(per_b,),
        in_specs=in_specs,
        out_specs=pl.BlockSpec((tm, D), here),
        input_output_aliases=aliases,
        compiler_params=_params(1),
        name="combine",
    )(*args)


def _moe(h2, x1, route, wts, counts, mod, w1, w3, w2, ln2_g, ln2_b):
    B, S, D = x1.shape
    T = B * S
    P = 2 * T + N_EXPERTS * MOE_BLOCK
    sizes = counts[0, ROUTE_OFF:ROUTE_OFF + N_EXPERTS].astype(jnp.int32)
    n_blk = (sizes + MOE_BLOCK - 1) // MOE_BLOCK
    psizes = n_blk * MOE_BLOCK
    poffs = jnp.cumsum(psizes) - psizes
    r4 = route.reshape(T, LANES)[:, :4]
    sel = r4[:, :2, None] == jnp.arange(N_EXPERTS, dtype=jnp.int32)[None, None, :]
    dest = r4[:, 2:4] + jnp.sum(jnp.where(sel, poffs[None, None, :], 0), axis=-1)
    xb = _sc_scatter_rows(h2.reshape(T, D // 2), dest[:, 0], dest[:, 1], P)
    yb = _experts(xb, poffs // MOE_BLOCK, n_blk, sizes, w1, w3, w2)
    out = None
    for b in range(B):
        slot_major = dest[b * S:(b + 1) * S].T.reshape(2 * S)
        rows = _sc_gather_rows(yb, slot_major).reshape(2, S, yb.shape[1])
        out = _combine_dense(rows, wts.reshape(T, LANES), x1.reshape(T, D), mod, ln2_g, ln2_b, S, b, out)
    return out.reshape(B, S, D)


def _layer(x, c, w_ada, b_ada, w_in, conv_w, conv_b, fw1, fb1, ff1, fw2, fb2, ff2, fw3, decay, skip,
           w_hy_o, w_attn_o, attn_sink, w_out, ln1_g, ln1_b, rg_w, rg_b, re_w, re_b, ew1, ew3, ew2,
           ln2_g, ln2_b):
    mod = _ada(c, w_ada, b_ada)
    q, kv, hv, hx1, hx2, ga, gh = _in_proj(x, mod, w_in, conv_w, conv_b)
    attn = _attention(q, kv, attn_sink)
    hy = _hyena(hv, hx1, hx2, fw1, fb1, ff1, fw2, fb2, ff2, fw3, decay, skip)
    x1, h2, route, wts, counts = _merge(attn, hy, ga, gh, x, mod, w_attn_o, w_hy_o, w_out,
                                        ln1_g, ln1_b, rg_w, rg_b, re_w, re_b)
    return _moe(h2, x1, route, wts, counts, mod, ew1, ew3, ew2, ln2_g, ln2_b)


def kernel(x, c, w_ada, b_ada, w_in, conv_w, conv_b, filt_w1, filt_b1, filt_freq1, filt_w2, filt_b2, filt_freq2, filt_w3, filt_decay, hy_skip, w_hy_o, w_attn_o, attn_sink, w_out, ln1_g, ln1_b, router_group_w, router_group_b, router_expert_w, router_expert_b, exp_w1, exp_w3, exp_w2, ln2_g, ln2_b):
    for l in range(w_ada.shape[0]):
        x = _layer(x, c, w_ada[l], b_ada[l], w_in[l], conv_w[l], conv_b[l], filt_w1[l], filt_b1[l],
                   filt_freq1[l], filt_w2[l], filt_b2[l], filt_freq2[l], filt_w3[l], filt_decay[l],
                   hy_skip[l], w_hy_o[l], w_attn_o[l], attn_sink[l], w_out[l], ln1_g[l], ln1_b[l],
                   router_group_w[l], router_group_b[l], router_expert_w[l], router_expert_b[l],
                   exp_w1[l], exp_w3[l], exp_w2[l], ln2_g[l], ln2_b[l])
    return x
```

```python
import functools
import math

import numpy as np
import jax
import jax.numpy as jnp
from jax import lax
from jax.experimental import pallas as pl
from jax.experimental.pallas import tpu as pltpu
from jax.experimental.pallas import tpu_sc as plsc

F32 = jnp.float32
BF16 = jnp.bfloat16

N_HEADS = 8
N_KV_HEADS = 2
HEAD_DIM = 64
ATTN_WIDTH = N_HEADS * HEAD_DIM
KV_WIDTH = N_KV_HEADS * HEAD_DIM
WINDOW = 128
BLOCK_Q = 128
HYENA_WIDTH = 512
FILTER_EMB = 33
FILTER_BANDS = (FILTER_EMB - 1) // 2
WINDOW_SHIFT = 0.05
N_GROUPS = 8
EXPERTS_PER_GROUP = 8
N_EXPERTS = N_GROUPS * EXPERTS_PER_GROUP
D_EXPERT = 512
MOE_BLOCK = 512
LN_EPS = 1e-5
DEPTH = 1
DN_ALPHA = (2.0 * DEPTH) ** 0.25
NEG = -1e30

LANES = 128
SUBLANES = 8
ROUTE_OFF = N_GROUPS
VMEM_LIMIT = 56 * 1024 * 1024


def _params(n_axes, vmem=VMEM_LIMIT):
    return pltpu.CompilerParams(dimension_semantics=("arbitrary",) * n_axes, vmem_limit_bytes=vmem)


def _split(a):
    hi = a.astype(BF16)
    lo = (a - hi.astype(F32)).astype(BF16)
    return hi, lo


def _dot3(a, b_hi, b_lo):
    a_hi, a_lo = _split(a)
    acc = jnp.dot(a_hi, b_hi, preferred_element_type=F32)
    acc = acc + jnp.dot(a_hi, b_lo, preferred_element_type=F32)
    acc = acc + jnp.dot(a_lo, b_hi, preferred_element_type=F32)
    return acc


def _pack2(a, b):
    ia = lax.bitcast_convert_type(a.astype(BF16).astype(F32), jnp.int32)
    ib = lax.bitcast_convert_type(b.astype(BF16).astype(F32), jnp.int32)
    return lax.bitcast_convert_type(ia | lax.shift_right_logical(ib, 16), F32)


def _unpack2(p):
    p = lax.bitcast_convert_type(p, jnp.int32)
    a = lax.bitcast_convert_type(p & jnp.int32(-65536), F32)
    b = lax.bitcast_convert_type(lax.shift_left(p, 16), F32)
    return a, b


def _sigmoid(x):
    return 0.5 * jnp.tanh(0.5 * x) + 0.5


def _layer_norm(r, g, b):
    mu = jnp.mean(r, axis=-1, keepdims=True)
    d = r - mu
    var = jnp.mean(d * d, axis=-1, keepdims=True)
    return d * lax.rsqrt(var + LN_EPS) * g + b


def _ada_kernel(c_ref, wh_ref, wl_ref, b_ref, o_ref):
    c = c_ref[...]
    s = c * _sigmoid(c)
    o_ref[...] = _dot3(s, wh_ref[...], wl_ref[...]) + b_ref[...]


def _ada(c, w_ada, b_ada):
    B, D = c.shape
    n_out = w_ada.shape[1]
    rows = 8
    cp = jnp.zeros((rows, D), F32).at[:B].set(c)
    wh, wl = _split(w_ada)
    tn = 1024
    out = pl.pallas_call(
        _ada_kernel,
        out_shape=jax.ShapeDtypeStruct((rows, n_out), F32),
        grid=(n_out // tn,),
        in_specs=[pl.BlockSpec((rows, D), lambda j: (0, 0)),
                  pl.BlockSpec((D, tn), lambda j: (0, j)),
                  pl.BlockSpec((D, tn), lambda j: (0, j)),
                  pl.BlockSpec((1, tn), lambda j: (0, j))],
        out_specs=pl.BlockSpec((rows, tn), lambda j: (0, j)),
        compiler_params=_params(1),
        name="ada",
    )(cp, wh, wl, b_ada.reshape(1, n_out))
    return out[:B].reshape(B, 6, D)


def _inproj_kernel(x_ref, xp_ref, xn_ref, mod_ref, w_ref, cw_ref, cb_ref,
                   q_ref, kv_ref, v_ref, x1_ref, x2_ref, ga_ref, gh_ref):
    i = pl.program_id(1)
    n = pl.num_programs(1)
    C = HYENA_WIDTH
    x = x_ref[0]
    tm, D = x.shape
    shift = mod_ref[0, 0:1, :]
    scale = mod_ref[0, 1:2, :]
    h = (x * (1.0 + scale) + shift).astype(BF16)

    def seg(lo, hi):
        return jnp.dot(h, w_ref[:, lo:hi], preferred_element_type=F32)

    o_q = 0
    o_kv = o_q + ATTN_WIDTH
    o_hy = o_kv + 2 * KV_WIDTH
    o_ga = o_hy + 3 * C
    o_gh = o_ga + D
    ga_ref[0] = _sigmoid(seg(o_ga, o_ga + D)).astype(BF16)
    gh_ref[0] = _sigmoid(seg(o_gh, o_gh + D)).astype(BF16)

    u = seg(o_hy, o_hy + 3 * C)
    xe = jnp.concatenate([xp_ref[0], xn_ref[0]], axis=0)
    he = (xe * (1.0 + scale) + shift).astype(BF16)
    ue = jnp.dot(he, w_ref[:, o_hy:o_hy + 3 * C], preferred_element_type=F32)
    prow = jnp.where(i > 0, ue[SUBLANES - 1:SUBLANES], 0.0)
    nrow = jnp.where(i < n - 1, ue[SUBLANES:SUBLANES + 1], 0.0)
    rid = lax.broadcasted_iota(jnp.int32, (tm, 1), 0)
    up = jnp.where(rid == 0, prow, pltpu.roll(u, 1, 0))
    dn = jnp.where(rid == tm - 1, nrow, pltpu.roll(u, tm - 1, 0))
    conv = cw_ref[0:1, :] * up + cw_ref[1:2, :] * u + cw_ref[2:3, :] * dn + cb_ref[...]
    v_ref[0] = conv[:, :C]
    x1_ref[0] = conv[:, C:2 * C]
    x2_ref[0] = conv[:, 2 * C:]

    q_ref[0] = (seg(o_q, o_q + ATTN_WIDTH) * (HEAD_DIM ** -0.5)).astype(BF16)
    kv_ref[0] = seg(o_kv, o_kv + 2 * KV_WIDTH).astype(BF16)


def _in_proj(x, mod, w_in, conv_w, conv_b):
    B, S, D = x.shape
    C = HYENA_WIDTH
    tm = min(512, S)
    r8 = tm // SUBLANES
    nb8 = S // SUBLANES
    wb = w_in.astype(BF16)
    nw = wb.shape[1]
    row = lambda b, i: (b, i, 0)
    shapes = [(ATTN_WIDTH, BF16), (2 * KV_WIDTH, BF16), (C, F32), (C, F32), (C, F32), (D, BF16), (D, BF16)]
    return pl.pallas_call(
        _inproj_kernel,
        out_shape=[jax.ShapeDtypeStruct((B, S, w), dt) for w, dt in shapes],
        grid=(B, S // tm),
        in_specs=[pl.BlockSpec((1, tm, D), row),
                  pl.BlockSpec((1, SUBLANES, D), lambda b, i: (b, jnp.maximum(i * r8 - 1, 0), 0)),
                  pl.BlockSpec((1, SUBLANES, D), lambda b, i: (b, jnp.minimum((i + 1) * r8, nb8 - 1), 0)),
                  pl.BlockSpec((1, 6, D), lambda b, i: (b, 0, 0)),
                  pl.BlockSpec((D, nw), lambda b, i: (0, 0)),
                  pl.BlockSpec((3, 3 * C), lambda b, i: (0, 0)),
                  pl.BlockSpec((1, 3 * C), lambda b, i: (0, 0))],
        out_specs=[pl.BlockSpec((1, tm, w), row) for w, _ in shapes],
        compiler_params=_params(2),
        name="in_proj",
    )(x, x, x, mod, wb, conv_w.astype(F32), conv_b.reshape(1, 3 * C).astype(F32))


ATT_TQ = 512


def _attn_kernel(sink_ref, q_ref, kvp_ref, kvc_ref, kvn_ref, bias_ref, o_ref, kv_scr, vx_scr, *, seq_len):
    i = pl.program_id(1)
    Q = BLOCK_Q
    TQ = q_ref.shape[1]
    G = N_HEADS // N_KV_HEADS
    kv_scr[0:Q] = kvp_ref[0]
    kv_scr[Q:Q + TQ] = kvc_ref[0]
    kv_scr[Q + TQ:] = kvn_ref[0]
    for kv in range(N_KV_HEADS):
        vx_scr[:, kv * LANES:kv * LANES + HEAD_DIM] = kv_scr[:, KV_WIDTH + kv * HEAD_DIM:KV_WIDTH + (kv + 1) * HEAD_DIM]
        vx_scr[:, kv * LANES + HEAD_DIM:(kv + 1) * LANES] = jnp.ones((TQ + 2 * Q, LANES - HEAD_DIM), BF16)
    col = lax.broadcasted_iota(jnp.int32, (1, 3 * Q), 1)
    rhead = lax.broadcasted_iota(jnp.int32, (G * Q, 1), 0) // Q
    for j in range(TQ // Q):
        kpos = i * TQ + (j - 1) * Q + col
        colbias = jnp.where((kpos >= 0) & (kpos < seq_len), 0.0, NEG)
        for kv in range(N_KV_HEADS):
            kk = kv_scr[j * Q:(j + 3) * Q, kv * HEAD_DIM:(kv + 1) * HEAD_DIM]
            vx = vx_scr[j * Q:(j + 3) * Q, kv * LANES:(kv + 1) * LANES]
            heads = [kv * G + g for g in range(G)]
            qg = jnp.concatenate([q_ref[0, j * Q:(j + 1) * Q, h * HEAD_DIM:(h + 1) * HEAD_DIM] for h in heads], axis=0)
            s = lax.dot_general(qg, kk, (((1,), (1,)), ((), ())), preferred_element_type=F32)
            s = s + bias_ref[kv] + colbias
            snk = jnp.where(rhead == 0, sink_ref[heads[0]],
                            jnp.where(rhead == 1, sink_ref[heads[1]],
                                      jnp.where(rhead == 2, sink_ref[heads[2]], sink_ref[heads[3]])))
            m = jnp.maximum(jnp.max(s, axis=1, keepdims=True), snk)
            p = jnp.exp(s - m).astype(BF16)
            ox = jnp.dot(p, vx, preferred_element_type=F32)
            den = ox[:, HEAD_DIM:HEAD_DIM + 1] + jnp.exp(snk - m)
            o = ox[:, :HEAD_DIM] / den
            for g, h in enumerate(heads):
                o_ref[0, j * Q:(j + 1) * Q, h * HEAD_DIM:(h + 1) * HEAD_DIM] = o[g * Q:(g + 1) * Q].astype(BF16)


def _attention(q, kv, sink):
    B, S, _ = q.shape
    Q = BLOCK_Q
    TQ = min(ATT_TQ, S)
    r = TQ // Q
    nq = S // Q
    G = N_HEADS // N_KV_HEADS
    assert G == 4
    a = jnp.arange(Q)[:, None]
    j = jnp.arange(3 * Q)[None, :]
    rel = jnp.abs(j - Q - a).astype(F32)
    slopes = 2.0 ** (-8.0 * jnp.arange(1, N_HEADS + 1, dtype=F32) / N_HEADS)
    bias = jnp.where(rel[None] <= WINDOW, -slopes[:, None, None] * rel[None], NEG).astype(F32)
    bias = bias.reshape(N_KV_HEADS, G * Q, 3 * Q)
    cur = lambda b, i: (b, i, 0)
    return pl.pallas_call(
        functools.partial(_attn_kernel, seq_len=S),
        out_shape=jax.ShapeDtypeStruct((B, S, ATTN_WIDTH), BF16),
        grid=(B, S // TQ),
        in_specs=[pl.BlockSpec(memory_space=pltpu.SMEM),
                  pl.BlockSpec((1, TQ, ATTN_WIDTH), cur),
                  pl.BlockSpec((1, Q, 2 * KV_WIDTH), lambda b, i: (b, jnp.maximum(i * r - 1, 0), 0)),
                  pl.BlockSpec((1, TQ, 2 * KV_WIDTH), cur),
                  pl.BlockSpec((1, Q, 2 * KV_WIDTH), lambda b, i: (b, jnp.minimum((i + 1) * r, nq - 1), 0)),
                  pl.BlockSpec((N_KV_HEADS, G * Q, 3 * Q), lambda b, i: (0, 0, 0))],
        out_specs=pl.BlockSpec((1, TQ, ATTN_WIDTH), cur),
        scratch_shapes=[pltpu.VMEM((TQ + 2 * Q, 2 * KV_WIDTH), BF16),
                        pltpu.VMEM((TQ + 2 * Q, N_KV_HEADS * LANES), BF16)],
        compiler_params=_params(2),
        name="attn",
    )(sink.astype(F32), q, kv, kv, kv, bias)


def _filter_kernel(z_ref, w1h, w1l, b1_ref, f1_ref, w2h, w2l, b2_ref, f2_ref, w3h, w3l, dec_ref,
                   k_ref, s_ref):
    i = pl.program_id(0)
    z = z_ref[...]
    h1 = jnp.sin(f1_ref[...] * (_dot3(z, w1h[...], w1l[...]) + b1_ref[...]))
    h2 = jnp.sin(f2_ref[...] * (_dot3(h1, w2h[...], w2l[...]) + b2_ref[...]))
    k = _dot3(h2, w3h[...], w3l[...])
    t = z[:, 0:1]
    k = k * (jnp.exp(-t * jnp.abs(dec_ref[...])) + WINDOW_SHIFT)
    k_ref[...] = k

    @pl.when(i == 0)
    def _():
        s_ref[...] = jnp.zeros_like(s_ref)

    s_ref[...] += jnp.sum(jnp.abs(k), axis=0, keepdims=True)


def _filter_embedding(L):
    t = np.linspace(0.0, 1.0, L, dtype=np.float32).astype(np.float64)[:, None]
    w = (2.0 * math.pi * np.arange(L, dtype=np.float32) / np.float32(L)).astype(np.float64)[:, None]
    bands = np.linspace(1e-4, FILTER_BANDS - 1, FILTER_BANDS, dtype=np.float32).astype(np.float64)[None, :]
    bw = (bands.astype(np.float32) * w.astype(np.float32)).astype(np.float64)
    z = np.concatenate([t, np.cos(bw), -np.sin(bw)], axis=-1)
    zp = np.zeros((L, LANES), np.float32)
    zp[:, :FILTER_EMB] = z.astype(np.float32)
    return jnp.asarray(zp)


def _pad2(a, r, c):
    return jnp.zeros((r, c), F32).at[:a.shape[0], :a.shape[1]].set(a.astype(F32))


def _filters(L, fw1, fb1, ff1, fw2, fb2, ff2, fw3, decay):
    H = LANES
    nf = fw3.shape[1]
    z = _filter_embedding(L)
    w1h, w1l = _split(_pad2(fw1, H, H))
    w2h, w2l = _split(_pad2(fw2, H, H))
    w3h, w3l = _split(_pad2(fw3, H, nf))
    b1 = _pad2(fb1[None], 1, H)
    f1 = _pad2(ff1[None], 1, H)
    b2 = _pad2(fb2[None], 1, H)
    f2 = _pad2(ff2[None], 1, H)
    tr = min(512, L)
    full = lambda r, c: pl.BlockSpec((r, c), lambda i: (0, 0))
    return pl.pallas_call(
        _filter_kernel,
        out_shape=[jax.ShapeDtypeStruct((L, nf), F32), jax.ShapeDtypeStruct((1, nf), F32)],
        grid=(L // tr,),
        in_specs=[pl.BlockSpec((tr, H), lambda i: (i, 0)),
                  full(H, H), full(H, H), full(1, H), full(1, H),
                  full(H, H), full(H, H), full(1, H), full(1, H),
                  full(H, nf), full(H, nf), full(1, nf)],
        out_specs=[pl.BlockSpec((tr, nf), lambda i: (i, 0)), full(1, nf)],
        compiler_params=_params(1),
        name="filter",
    )(z, w1h, w1l, b1, f1, w2h, w2l, b2, f2, w3h, w3l, decay.reshape(1, nf).astype(F32))


def _np_bf16(m64):
    return jnp.asarray(m64.astype(np.float32).astype(BF16))


def _dft_constants(L):
    N = 2 * L
    n2 = LANES
    n1 = N // n2
    h1 = n1 // 2
    k1 = np.arange(n1)[:, None]
    s1 = np.arange(h1)[None, :]
    ang = -2.0 * np.pi * ((k1 * s1) % n1) / n1
    wr, wi = np.cos(ang), np.sin(ang)
    w1_filt = np.block([[wr, wr], [wi, wi], [wr, -wr], [wi, -wi]])
    w1_cplx = np.block([[wr, -wi], [wi, wr]])
    vr, vi = wr.T / N, -wi.T / N
    w3 = np.block([[vr, -vi], [vi, vr]])
    k2 = np.arange(n2)[:, None]
    s2 = np.arange(n2)[None, :]
    a2 = -2.0 * np.pi * ((k2 * s2) % n2) / n2
    w2r, w2i = jnp.asarray(np.cos(a2), F32), jnp.asarray(np.sin(a2), F32)
    at = -2.0 * np.pi * ((np.arange(n1)[:, None] * s2) % N) / N
    twr, twi = jnp.asarray(np.cos(at), F32), jnp.asarray(np.sin(at), F32)
    mr = w2r[None] * twr[:, None, :] - w2i[None] * twi[:, None, :]
    mi = w2r[None] * twi[:, None, :] + w2i[None] * twr[:, None, :]
    fwd = jnp.concatenate([jnp.concatenate([mr, -mi], axis=2),
                           jnp.concatenate([mi, mr], axis=2)], axis=1)
    fwd = fwd.astype(BF16)
    return dict(n1=n1, w1_filt=_np_bf16(w1_filt), w1_cplx=_np_bf16(w1_cplx), w3=_np_bf16(w3),
                fwd=fwd)


SCH = 8


def _dft1_kernel(x_ref, w_ref, a_ref, *, n1):
    w = w_ref[...]
    for j in range(SCH):
        rhs = jnp.concatenate([x_ref[0, 0, :, j, :], x_ref[0, 1, :, j, :]], axis=0)
        res = jnp.dot(w, rhs.astype(BF16), preferred_element_type=F32)
        a_ref[0, :, j, :] = _pack2(res[:n1], res[n1:])


def _dft1_data(x, consts):
    B, L, C = x.shape
    n1 = consts["n1"]
    h1 = n1 // 2
    xv = x.reshape(B // 2, 2, h1, LANES, C)
    return pl.pallas_call(
        functools.partial(_dft1_kernel, n1=n1),
        out_shape=jax.ShapeDtypeStruct((B // 2, n1, LANES, C), F32),
        grid=(B // 2, LANES // SCH),
        in_specs=[pl.BlockSpec((1, 2, h1, SCH, C), lambda p, j: (p, 0, 0, j, 0)),
                  pl.BlockSpec((2 * n1, n1), lambda p, j: (0, 0))],
        out_specs=pl.BlockSpec((1, n1, SCH, C), lambda p, j: (p, 0, j, 0)),
        compiler_params=_params(2),
        name="dft1",
    )(xv, consts["w1_cplx"])


def _dft1f_kernel(x_ref, w_ref, a_ref, *, n1):
    C = HYENA_WIDTH
    w = w_ref[...]
    for j in range(SCH):
        rhs = jnp.concatenate([x_ref[:, j, :C], x_ref[:, j, C:]], axis=0)
        res = jnp.dot(w, rhs.astype(BF16), preferred_element_type=F32)
        a_ref[0, :, 0, j, :] = _pack2(res[:n1], res[n1:2 * n1])
        a_ref[0, :, 1, j, :] = _pack2(res[2 * n1:3 * n1], res[3 * n1:])


def _dft1_filter(kraw, consts):
    L, nf = kraw.shape
    C = HYENA_WIDTH
    n_ord = nf // (2 * C)
    n1 = consts["n1"]
    h1 = n1 // 2
    kv = kraw.reshape(h1, LANES, nf)
    return pl.pallas_call(
        functools.partial(_dft1f_kernel, n1=n1),
        out_shape=jax.ShapeDtypeStruct((n_ord, n1, 2, LANES, C), F32),
        grid=(n_ord, LANES // SCH),
        in_specs=[pl.BlockSpec((h1, SCH, 2 * C), lambda o, j: (0, j, o)),
                  pl.BlockSpec((4 * n1, n1), lambda o, j: (0, 0))],
        out_specs=pl.BlockSpec((1, n1, 2, SCH, C), lambda o, j: (o, 0, 0, j, 0)),
        compiler_params=_params(2),
        name="dft1f",
    )(kv, consts["w1_filt"])


KCH = 8


def _midf_kernel(a_ref, f_ref, inv_ref, b0_ref, h_ref):
    n2 = LANES
    sc = inv_ref[0]
    for k in range(KCH):
        p = jnp.concatenate(_unpack2(a_ref[0, k, :n2, :]), axis=0).astype(BF16)
        q = jnp.concatenate(_unpack2(a_ref[0, k, n2:, :]), axis=0).astype(BF16)
        h_re = jnp.dot(f_ref[k, :n2, :], p, preferred_element_type=F32)
        h_im = jnp.dot(f_ref[k, n2:, :], q, preferred_element_type=F32)
        h_ref[0, k] = _pack2((h_re - b0_ref[0]) * sc, h_im * sc)


def _filter_spectrum(af, inv_den, bwd0, consts):
    n_ord, n1, _, n2, C = af.shape
    a = af.reshape(n_ord, n1, 2 * n2, C)
    tab = pl.BlockSpec((KCH, 2 * n2, 2 * n2), lambda k, o: (k, 0, 0))
    vec = pl.BlockSpec((1, 1, C), lambda k, o: (o, 0, 0))
    return pl.pallas_call(
        _midf_kernel,
        out_shape=jax.ShapeDtypeStruct((n_ord, n1, n2, C), F32),
        grid=(n1 // KCH, n_ord),
        in_specs=[pl.BlockSpec((1, KCH, 2 * n2, C), lambda k, o: (o, k, 0, 0)), tab, vec, vec],
        out_specs=pl.BlockSpec((1, KCH, n2, C), lambda k, o: (o, k, 0, 0)),
        compiler_params=_params(2),
        name="midf",
    )(a, consts["fwd"], inv_den, bwd0)


def _mid_kernel(a_ref, f_ref, h_ref, b_ref):
    n2 = LANES
    for k in range(KCH):
        a = jnp.concatenate(_unpack2(a_ref[0, k]), axis=0).astype(BF16)
        x = jnp.dot(f_ref[k], a, preferred_element_type=F32)
        xr, xi = x[:n2], x[n2:]
        hr, hi = _unpack2(h_ref[0, k])
        y = jnp.concatenate([xr * hr - xi * hi, xr * hi + xi * hr], axis=0)
        b = lax.dot_general(f_ref[k], y.astype(BF16), (((0,), (0,)), ((), ())), preferred_element_type=F32)
        b_ref[0, k] = _pack2(b[:n2], b[n2:])


def _mid(a, hspec, order, consts):
    P, n1, n2, C = a.shape
    tab = pl.BlockSpec((KCH, 2 * n2, 2 * n2), lambda k, p: (k, 0, 0))
    return pl.pallas_call(
        _mid_kernel,
        out_shape=jax.ShapeDtypeStruct((P, n1, n2, C), F32),
        grid=(n1 // KCH, P),
        in_specs=[pl.BlockSpec((1, KCH, n2, C), lambda k, p: (p, k, 0, 0)),
                  tab,
                  pl.BlockSpec((1, KCH, n2, C), lambda k, p: (order, k, 0, 0))],
        out_specs=pl.BlockSpec((1, KCH, n2, C), lambda k, p: (p, k, 0, 0)),
        compiler_params=_params(2),
        name="mid",
    )(a, consts["fwd"], hspec)


def _dft3_kernel(b_ref, w_ref, v_ref, g_ref, skip_ref, *rest, h1, chain):
    if chain:
        w1_ref, z_ref, a_ref, slab_ref = rest
        w1 = w1_ref[...]
    else:
        z_ref, slab_ref = rest
    w = w_ref[...]
    skip = skip_ref[0]
    n1 = 2 * h1
    for j in range(SCH):
        slab_ref[...] = b_ref[0, :, j, :]
        rhs = jnp.concatenate(_unpack2(slab_ref[...]), axis=0)
        y = jnp.dot(w, rhs.astype(BF16), preferred_element_type=F32)
        z = [g_ref[0, r, :, j, :] * (y[r * h1:(r + 1) * h1] + v_ref[0, r, :, j, :] * skip) for r in range(2)]
        for r in range(2):
            z_ref[0, r, :, j, :] = z[r]
        if chain:
            res = jnp.dot(w1, jnp.concatenate(z, axis=0).astype(BF16), preferred_element_type=F32)
            a_ref[0, :, j, :] = _pack2(res[:n1], res[n1:])


def _dft3_gate(b5, v, gate, skip, consts, chain):
    P, n1, n2, C = b5.shape
    h1 = n1 // 2
    B, L, _ = v.shape
    five = lambda t: t.reshape(P, 2, h1, n2, C)
    dat = pl.BlockSpec((1, 2, h1, SCH, C), lambda p, j: (p, 0, 0, j, 0))
    packed = pl.BlockSpec((1, n1, SCH, C), lambda p, j: (p, 0, j, 0))
    in_specs = [packed, pl.BlockSpec((n1, 2 * n1), lambda p, j: (0, 0)), dat, dat,
                pl.BlockSpec((1, C), lambda p, j: (0, 0))]
    args = [b5, consts["w3"], five(v), five(gate), skip.reshape(1, C).astype(F32)]
    out_shape = [jax.ShapeDtypeStruct((P, 2, h1, n2, C), F32)]
    out_specs = [dat]
    if chain:
        in_specs.append(pl.BlockSpec((2 * n1, n1), lambda p, j: (0, 0)))
        args.append(consts["w1_cplx"])
        out_shape.append(jax.ShapeDtypeStruct((P, n1, n2, C), F32))
        out_specs.append(packed)
    outs = pl.pallas_call(
        functools.partial(_dft3_kernel, h1=h1, chain=chain),
        out_shape=out_shape,
        grid=(P, n2 // SCH),
        in_specs=in_specs,
        out_specs=out_specs,
        scratch_shapes=[pltpu.VMEM((n1, C), F32)],
        compiler_params=_params(2),
        name="dft3",
    )(*args)
    z = outs[0].reshape(B, L, C)
    return (z, outs[1]) if chain else (z, None)


def _hyena(v, x1, x2, fw1, fb1, ff1, fw2, fb2, ff2, fw3, decay, skip):
    B, L, C = v.shape
    consts = _dft_constants(L)
    kraw, ksum = _filters(L, fw1, fb1, ff1, fw2, fb2, ff2, fw3, decay)
    ks = ksum.reshape(2, 2, C)
    inv_den = (1.0 / (ks[:, 0] + ks[:, 1])).reshape(2, 1, C)
    bwd0 = kraw[0].reshape(2, 2, C)[:, 1].reshape(2, 1, C)
    hspec = _filter_spectrum(_dft1_filter(kraw, consts), inv_den, bwd0, consts)
    gates = (x1, x2)
    z, a5 = v, _dft1_data(v, consts)
    for o, gate in enumerate(gates):
        b5 = _mid(a5, hspec, o, consts)
        z, a5 = _dft3_gate(b5, z, gate, skip[o], consts, chain=o + 1 < len(gates))
    return z


def _merge_kernel(attn_ref, hy_ref, ga_ref, gh_ref, x_ref, mod_ref, wa_ref, wh_ref, wo_ref,
                  g1_ref, b1_ref, rwh_ref, rwl_ref, rb_ref, tri_ref,
                  x1_ref, h2_ref, route_ref, wts_ref, cnt_ref, carry_ref):
    @pl.when((pl.program_id(0) == 0) & (pl.program_id(1) == 0))
    def _():
        carry_ref[...] = jnp.zeros_like(carry_ref)

    logits = _merge_dense(attn_ref, hy_ref, ga_ref, gh_ref, x_ref, mod_ref, wa_ref, wh_ref, wo_ref,
                          g1_ref, b1_ref, rwh_ref, rwl_ref, rb_ref, x1_ref, h2_ref)
    _route_rows(logits, tri_ref, route_ref, wts_ref, carry_ref)
    cnt_ref[...] = carry_ref[...]


def _merge_dense(attn_ref, hy_ref, ga_ref, gh_ref, x_ref, mod_ref, wa_ref, wh_ref, wo_ref,
                 g1_ref, b1_ref, rwh_ref, rwl_ref, rb_ref, x1_ref, h2_ref):
    a = jnp.dot(attn_ref[0], wa_ref[...], preferred_element_type=F32)
    hy = jnp.dot(hy_ref[0].astype(BF16), wh_ref[...], preferred_element_type=F32)
    merged = ga_ref[0].astype(F32) * a + gh_ref[0].astype(F32) * hy
    y = jnp.dot(merged.astype(BF16), wo_ref[...], preferred_element_type=F32)
    gate1 = mod_ref[0, 2:3, :]
    shift2 = mod_ref[0, 3:4, :]
    scale2 = mod_ref[0, 4:5, :]
    x1 = _layer_norm(DN_ALPHA * x_ref[0] + gate1 * y, g1_ref[...], b1_ref[...])
    x1_ref[0] = x1
    h2 = x1 * (1.0 + scale2) + shift2
    half = h2.shape[1] // 2
    h2_ref[0] = _pack2(h2[:, :half], h2[:, half:])
    return _dot3(h2, rwh_ref[...], rwl_ref[...]) + rb_ref[...]


def _route_rows(logits, tri_ref, route_ref, wts_ref, carry_ref):
    tm = logits.shape[0]
    lane = lax.broadcasted_iota(jnp.int32, (tm, LANES), 1)
    lanef = lane.astype(F32)
    big = float(LANES)

    def first_lane(mask):
        return jnp.min(jnp.where(mask, lanef, big), axis=1, keepdims=True).astype(jnp.int32)

    gmask = lane < N_GROUPS
    gl = jnp.where(gmask, logits, NEG)
    gmax = jnp.max(gl, axis=1, keepdims=True)
    gidx = first_lane(gl == gmax)
    pg = 1.0 / jnp.sum(jnp.exp(gl - gmax), axis=1, keepdims=True)
    lo = ROUTE_OFF + gidx * EXPERTS_PER_GROUP
    emask = (lane >= lo) & (lane < lo + EXPERTS_PER_GROUP)
    el = jnp.where(emask, logits, NEG)
    v1 = jnp.max(el, axis=1, keepdims=True)
    i1 = first_lane(el == v1)
    el2 = jnp.where(emask & (lane != i1), logits, NEG)
    v2 = jnp.max(el2, axis=1, keepdims=True)
    i2 = first_lane(el2 == v2)
    e21 = jnp.exp(v2 - v1)
    w1 = pg / (1.0 + e21)
    w2 = pg * e21 / (1.0 + e21)

    sel1 = lane == i1
    sel2 = lane == i2
    onehot = jnp.where(sel1 | sel2, 1.0, 0.0)
    prefix = jnp.dot(tri_ref[...], onehot.astype(BF16), preferred_element_type=F32) + carry_ref[...]
    r1 = jnp.sum(jnp.where(sel1, prefix, 0.0), axis=1, keepdims=True)
    r2 = jnp.sum(jnp.where(sel2, prefix, 0.0), axis=1, keepdims=True)
    carry_ref[...] += jnp.sum(onehot, axis=0, keepdims=True)

    ranks = jnp.where(lane == 2, r1, jnp.where(lane == 3, r2, 0.0)).astype(jnp.int32)
    route_ref[0] = jnp.where(lane == 0, i1 - ROUTE_OFF, jnp.where(lane == 1, i2 - ROUTE_OFF, ranks))
    wts_ref[0] = jnp.where(lane == 0, w1, jnp.where(lane == 1, w2, 0.0))


def _merge(attn, hy, ga, gh, x, mod, w_attn_o, w_hy_o, w_out, ln1_g, ln1_b, rg_w, rg_b, re_w, re_b):
    B, S, D = x.shape
    tm = min(512, S)
    rw = jnp.zeros((D, LANES), F32).at[:, :N_GROUPS].set(rg_w).at[:, ROUTE_OFF:ROUTE_OFF + N_EXPERTS].set(re_w)
    rb = jnp.zeros((1, LANES), F32).at[0, :N_GROUPS].set(rg_b).at[0, ROUTE_OFF:ROUTE_OFF + N_EXPERTS].set(re_b)
    rwh, rwl = _split(rw)
    tri = (jnp.arange(tm)[:, None] > jnp.arange(tm)[None, :]).astype(BF16)
    row = lambda b, i: (b, i, 0)
    full = lambda r, c: pl.BlockSpec((r, c), lambda b, i: (0, 0))
    outs = [jax.ShapeDtypeStruct((B, S, D), F32), jax.ShapeDtypeStruct((B, S, D // 2), F32),
            jax.ShapeDtypeStruct((B, S, LANES), jnp.int32), jax.ShapeDtypeStruct((B, S, LANES), F32),
            jax.ShapeDtypeStruct((1, LANES), F32)]
    return pl.pallas_call(
        _merge_kernel,
        out_shape=outs,
        grid=(B, S // tm),
        in_specs=[pl.BlockSpec((1, tm, ATTN_WIDTH), row), pl.BlockSpec((1, tm, HYENA_WIDTH), row),
                  pl.BlockSpec((1, tm, D), row), pl.BlockSpec((1, tm, D), row), pl.BlockSpec((1, tm, D), row),
                  pl.BlockSpec((1, 6, D), lambda b, i: (b, 0, 0)),
                  full(ATTN_WIDTH, D), full(HYENA_WIDTH, D), full(D, D),
                  full(1, D), full(1, D), full(D, LANES), full(D, LANES), full(1, LANES), full(tm, tm)],
        out_specs=[pl.BlockSpec((1, tm, D), row), pl.BlockSpec((1, tm, D // 2), row),
                   pl.BlockSpec((1, tm, LANES), row), pl.BlockSpec((1, tm, LANES), row), full(1, LANES)],
        scratch_shapes=[pltpu.VMEM((1, LANES), F32)],
        compiler_params=_params(2),
        name="merge",
    )(attn, hy, ga, gh, x, mod, w_attn_o.astype(BF16), w_hy_o.astype(BF16), w_out.astype(BF16),
      ln1_g.reshape(1, D), ln1_b.reshape(1, D), rwh, rwl, rb, tri)


SC_ROWS = 64


def _sc_workers():
    info = plsc.get_sparse_core_info()
    return info.num_cores, info.num_cores * info.num_subcores


def _sc_split(n):
    _, workers = _sc_workers()
    per_worker = n // workers
    chunks = per_worker // SC_ROWS
    assert per_worker * workers == n and chunks * SC_ROWS == per_worker and chunks % 2 == 0
    return workers, per_worker, chunks


def _sc_scatter_rows(src, idx0, idx1, n_out):
    n, width = src.shape
    nc, _ = _sc_workers()
    workers, per_worker, chunks = _sc_split(n)
    mesh = plsc.VectorSubcoreMesh(core_axis_name="c", subcore_axis_name="s")

    def body(src_hbm, i0_hbm, i1_hbm, out_hbm, i0_v, i1_v, rows_v, sem, ssem):
        wid = lax.axis_index("s") * nc + lax.axis_index("c")
        base = wid * per_worker
        pltpu.sync_copy(i0_hbm.at[wid], i0_v)
        pltpu.sync_copy(i1_hbm.at[wid], i1_v)

        def load(chunk, buf):
            return pltpu.make_async_copy(src_hbm.at[pl.ds(base + chunk * SC_ROWS, SC_ROWS)], rows_v.at[buf], sem)

        load(0, 0).start()

        @pl.loop(0, chunks, step=2)
        def _(c):
            for b in range(2):
                chunk = c + b
                load(chunk, b).wait()

                @pl.when(chunk + 1 < chunks)
                def _():
                    load(chunk + 1, 1 - b).start()

                first = pltpu.make_async_copy(rows_v.at[b], out_hbm.at[i0_v.at[chunk]], ssem)
                second = pltpu.make_async_copy(rows_v.at[b], out_hbm.at[i1_v.at[chunk]], ssem)
                first.start()
                second.start()
                first.wait()
                second.wait()

    shaped = lambda i: i.reshape(workers, chunks, SC_ROWS)
    return pl.kernel(
        body,
        out_type=jax.ShapeDtypeStruct((n_out, width), src.dtype),
        mesh=mesh,
        scratch_types=[pltpu.VMEM((chunks, SC_ROWS), jnp.int32),
                       pltpu.VMEM((chunks, SC_ROWS), jnp.int32),
                       pltpu.VMEM((2, SC_ROWS, width), src.dtype),
                       pltpu.SemaphoreType.DMA, pltpu.SemaphoreType.DMA],
        name="sc_scatter",
    )(src, shaped(idx0), shaped(idx1))


def _expert_kernel(first_ref, nb_ref, sz_ref, tot_ref, w1_ref, w3_ref, w2_ref, xb_ref, yb_ref,
                   xbuf, ybuf, c1_ref, c3_ref, c2_ref, lsem, ssem):
    e = pl.program_id(0)
    nb = nb_ref[e]
    first = first_ref[e]
    total = tot_ref[0]
    rows = xbuf.shape[1]

    def load(g, slot):
        src = xb_ref.at[pl.ds(pl.multiple_of(g * rows, rows), rows)]
        return pltpu.make_async_copy(src, xbuf.at[slot], lsem.at[slot])

    def store(g, slot):
        dst = yb_ref.at[pl.ds(pl.multiple_of(g * rows, rows), rows)]
        return pltpu.make_async_copy(ybuf.at[slot], dst, ssem.at[slot])

    @pl.when((e == 0) & (total > 0))
    def _():
        load(0, 0).start()

    @pl.when(nb > 0)
    def _():
        c1_ref[...] = w1_ref[0].astype(BF16)
        c3_ref[...] = w3_ref[0].astype(BF16)
        c2_ref[...] = w2_ref[0].astype(BF16)

        def block(j, carry):
            g = first + j
            slot = lax.rem(g, 2)
            load(g, slot).wait()

            @pl.when(g + 1 < total)
            def _():
                load(g + 1, 1 - slot).start()

            @pl.when(g >= 2)
            def _():
                store(g - 2, slot).wait()

            n_valid = sz_ref[e] - j * rows
            rid = lax.broadcasted_iota(jnp.int32, (rows, 1), 0)
            xa, xb = _unpack2(jnp.where(rid < n_valid, xbuf[slot], 0.0))
            x = jnp.concatenate([xa, xb], axis=1).astype(BF16)
            a = jnp.dot(x, c1_ref[...], preferred_element_type=F32)
            gate = jnp.dot(x, c3_ref[...], preferred_element_type=F32)
            hmid = (a * _sigmoid(a) * gate).astype(BF16)
            y = jnp.dot(hmid, c2_ref[...], preferred_element_type=F32)
            half = y.shape[1] // 2
            ybuf[slot] = _pack2(y[:, :half], y[:, half:])
            store(g, slot).start()
            return carry

        lax.fori_loop(0, nb, block, 0)

    @pl.when(e == pl.num_programs(0) - 1)
    def _():
        for back in (2, 1):
            @pl.when(total >= back)
            def _():
                g = total - back
                store(g, lax.rem(g, 2)).wait()


def _experts(xb, first_blk, n_blk, sizes, w1, w3, w2):
    P, W = xb.shape
    E, D, DE = w1.shape
    total = jnp.sum(n_blk, keepdims=True)
    wspec = lambda r, c: pl.BlockSpec((1, r, c), lambda e, *_: (e, 0, 0))
    grid_spec = pltpu.PrefetchScalarGridSpec(
        num_scalar_prefetch=4,
        grid=(E,),
        in_specs=[wspec(D, DE), wspec(D, DE), wspec(DE, D), pl.BlockSpec(memory_space=pl.ANY)],
        out_specs=pl.BlockSpec(memory_space=pl.ANY),
        scratch_shapes=[pltpu.VMEM((2, MOE_BLOCK, W), F32), pltpu.VMEM((2, MOE_BLOCK, W), F32),
                        pltpu.VMEM((D, DE), BF16), pltpu.VMEM((D, DE), BF16), pltpu.VMEM((DE, D), BF16),
                        pltpu.SemaphoreType.DMA((2,)), pltpu.SemaphoreType.DMA((2,))],
    )
    return pl.pallas_call(
        _expert_kernel,
        out_shape=jax.ShapeDtypeStruct((P, W), F32),
        grid_spec=grid_spec,
        compiler_params=_params(1),
        name="experts",
    )(first_blk, n_blk, sizes, total, w1, w3, w2, xb)


def _sc_gather_rows(table, idx):
    n, width = idx.shape[0], table.shape[1]
    nc, _ = _sc_workers()
    workers, per_worker, chunks = _sc_split(n)
    mesh = plsc.VectorSubcoreMesh(core_axis_name="c", subcore_axis_name="s")

    def body(table_hbm, idx_hbm, out_hbm, idx_v, rows_v, sem):
        wid = lax.axis_index("s") * nc + lax.axis_index("c")
        base = wid * per_worker
        pltpu.sync_copy(idx_hbm.at[wid], idx_v)

        def gather(chunk, buf):
            return pltpu.make_async_copy(table_hbm.at[idx_v.at[chunk]], rows_v.at[buf], sem)

        gather(0, 0).start()

        @pl.loop(0, chunks, step=2)
        def _(c):
            for b in range(2):
                chunk = c + b
                gather(chunk, b).wait()

                @pl.when(chunk + 1 < chunks)
                def _():
                    gather(chunk + 1, 1 - b).start()

                pltpu.sync_copy(rows_v.at[b], out_hbm.at[pl.ds(base + chunk * SC_ROWS, SC_ROWS)])

    return pl.kernel(
        body,
        out_type=jax.ShapeDtypeStruct((n, width), table.dtype),
        mesh=mesh,
        scratch_types=[pltpu.VMEM((chunks, SC_ROWS), jnp.int32),
                       pltpu.VMEM((2, SC_ROWS, width), table.dtype),
                       pltpu.SemaphoreType.DMA],
        name="sc_gather",
    )(table, idx.reshape(workers, chunks, SC_ROWS))


def _combine_dense_kernel(r0_ref, r1_ref, wts_ref, x1_ref, mod_ref, g_ref, b_ref, *rest):
    o_ref = rest[-1]
    w = wts_ref[...]
    y0 = jnp.concatenate(_unpack2(r0_ref[0]), axis=1)
    y1 = jnp.concatenate(_unpack2(r1_ref[0]), axis=1)
    y = w[:, 0:1] * y0 + w[:, 1:2] * y1
    gate2 = mod_ref[0, 5:6, :]
    o_ref[...] = _layer_norm(DN_ALPHA * x1_ref[...] + gate2 * y, g_ref[...], b_ref[...])


def _combine_dense(rows, wts, x1, mod, ln2_g, ln2_b, S, b, out):
    T, D = x1.shape
    tm = min(512, S)
    per_b = S // tm
    here = lambda i: (b * per_b + i, 0)
    in_specs = [pl.BlockSpec((1, tm, rows.shape[2]), lambda i: (0, i, 0)),
                pl.BlockSpec((1, tm, rows.shape[2]), lambda i: (1, i, 0)),
                pl.BlockSpec((tm, LANES), here),
                pl.BlockSpec((tm, D), here),
                pl.BlockSpec((1, 6, D), lambda i: (b, 0, 0)),
                pl.BlockSpec((1, D), lambda i: (0, 0)),
                pl.BlockSpec((1, D), lambda i: (0, 0))]
    args = [rows, rows, wts, x1, mod, ln2_g.reshape(1, D), ln2_b.reshape(1, D)]
    aliases = {}
    if out is not None:
        in_specs.append(pl.BlockSpec(memory_space=pl.ANY))
        aliases = {len(args): 0}
        args.append(out)
    return pl.pallas_call(
        _combine_dense_kernel,
        out_shape=jax.ShapeDtypeStruct((T, D), F32),
        grid=(per_b,),
        in_specs=in_specs,
        out_specs=pl.BlockSpec((tm, D), here),
        input_output_aliases=aliases,
        compiler_params=_params(1),
        name="combine",
    )(*args)


def _moe(h2, x1, route, wts, counts, mod, w1, w3, w2, ln2_g, ln2_b):
    B, S, D = x1.shape
    T = B * S
    P = 2 * T + N_EXPERTS * MOE_BLOCK
    sizes = counts[0, ROUTE_OFF:ROUTE_OFF + N_EXPERTS].astype(jnp.int32)
    n_blk = (sizes + MOE_BLOCK - 1) // MOE_BLOCK
    psizes = n_blk * MOE_BLOCK
    poffs = jnp.cumsum(psizes) - psizes
    r4 = route.reshape(T, LANES)[:, :4]
    sel = r4[:, :2, None] == jnp.arange(N_EXPERTS, dtype=jnp.int32)[None, None, :]
    dest = r4[:, 2:4] + jnp.sum(jnp.where(sel, poffs[None, None, :], 0), axis=-1)
    xb = _sc_scatter_rows(h2.reshape(T, D // 2), dest[:, 0], dest[:, 1], P)
    yb = _experts(xb, poffs // MOE_BLOCK, n_blk, sizes, w1, w3, w2)
    out = None
    for b in range(B):
        slot_major = dest[b * S:(b + 1) * S].T.reshape(2 * S)
        rows = _sc_gather_rows(yb, slot_major).reshape(2, S, yb.shape[1])
        out = _combine_dense(rows, wts.reshape(T, LANES), x1.reshape(T, D), mod, ln2_g, ln2_b, S, b, out)
    return out.reshape(B, S, D)


def _layer(x, c, w_ada, b_ada, w_in, conv_w, conv_b, fw1, fb1, ff1, fw2, fb2, ff2, fw3, decay, skip,
           w_hy_o, w_attn_o, attn_sink, w_out, ln1_g, ln1_b, rg_w, rg_b, re_w, re_b, ew1, ew3, ew2,
           ln2_g, ln2_b):
    mod = _ada(c, w_ada, b_ada)
    q, kv, hv, hx1, hx2, ga, gh = _in_proj(x, mod, w_in, conv_w, conv_b)
    attn = _attention(q, kv, attn_sink)
    hy = _hyena(hv, hx1, hx2, fw1, fb1, ff1, fw2, fb2, ff2, fw3, decay, skip)
    x1, h2, route, wts, counts = _merge(attn, hy, ga, gh, x, mod, w_attn_o, w_hy_o, w_out,
                                        ln1_g, ln1_b, rg_w, rg_b, re_w, re_b)
    return _moe(h2, x1, route, wts, counts, mod, ew1, ew3, ew2, ln2_g, ln2_b)


def kernel(x, c, w_ada, b_ada, w_in, conv_w, conv_b, filt_w1, filt_b1, filt_freq1, filt_w2, filt_b2, filt_freq2, filt_w3, filt_decay, hy_skip, w_hy_o, w_attn_o, attn_sink, w_out, ln1_g, ln1_b, router_group_w, router_group_b, router_expert_w, router_expert_b, exp_w1, exp_w3, exp_w2, ln2_g, ln2_b):
    for l in range(w_ada.shape[0]):
        x = _layer(x, c, w_ada[l], b_ada[l], w_in[l], conv_w[l], conv_b[l], filt_w1[l], filt_b1[l],
                   filt_freq1[l], filt_w2[l], filt_b2[l], filt_freq2[l], filt_w3[l], filt_decay[l],
                   hy_skip[l], w_hy_o[l], w_attn_o[l], attn_sink[l], w_out[l], ln1_g[l], ln1_b[l],
                   router_group_w[l], router_group_b[l], router_expert_w[l], router_expert_b[l],
                   exp_w1[l], exp_w3[l], exp_w2[l], ln2_g[l], ln2_b[l])
    return x
```

```python
import functools
import math

import numpy as np
import jax
import jax.numpy as jnp
from jax import lax
from jax.experimental import pallas as pl
from jax.experimental.pallas import tpu as pltpu
from jax.experimental.pallas import tpu_sc as plsc

F32 = jnp.float32
BF16 = jnp.bfloat16

N_HEADS = 8
N_KV_HEADS = 2
HEAD_DIM = 64
ATTN_WIDTH = N_HEADS * HEAD_DIM
KV_WIDTH = N_KV_HEADS * HEAD_DIM
WINDOW = 128
HYENA_WIDTH = 512
FILTER_EMB = 33
FILTER_BANDS = (FILTER_EMB - 1) // 2
WINDOW_SHIFT = 0.05
N_GROUPS = 8
EXPERTS_PER_GROUP = 8
N_EXPERTS = N_GROUPS * EXPERTS_PER_GROUP
D_EXPERT = 512
MOE_BLOCK = 512
LN_EPS = 1e-5
DEPTH = 1
DN_ALPHA = (2.0 * DEPTH) ** 0.25
NEG = -1e30

LANES = 128
SUBLANES = 8
ROUTE_OFF = N_GROUPS
VMEM_LIMIT = 56 * 1024 * 1024


def _params(n_axes, vmem=VMEM_LIMIT):
    return pltpu.CompilerParams(dimension_semantics=("arbitrary",) * n_axes, vmem_limit_bytes=vmem)


def _split(a):
    hi = a.astype(BF16)
    lo = (a - hi.astype(F32)).astype(BF16)
    return hi, lo


def _dot3(a, b_hi, b_lo):
    a_hi, a_lo = _split(a)
    acc = jnp.dot(a_hi, b_hi, preferred_element_type=F32)
    acc = acc + jnp.dot(a_hi, b_lo, preferred_element_type=F32)
    acc = acc + jnp.dot(a_lo, b_hi, preferred_element_type=F32)
    return acc


def _pack2(a, b):
    ia = lax.bitcast_convert_type(a.astype(BF16).astype(F32), jnp.int32)
    ib = lax.bitcast_convert_type(b.astype(BF16).astype(F32), jnp.int32)
    return lax.bitcast_convert_type(ia | lax.shift_right_logical(ib, 16), F32)


def _unpack2(p):
    p = lax.bitcast_convert_type(p, jnp.int32)
    a = lax.bitcast_convert_type(p & jnp.int32(-65536), F32)
    b = lax.bitcast_convert_type(lax.shift_left(p, 16), F32)
    return a, b


def _sigmoid(x):
    return 0.5 * jnp.tanh(0.5 * x) + 0.5


def _layer_norm(r, g, b):
    mu = jnp.mean(r, axis=-1, keepdims=True)
    d = r - mu
    var = jnp.mean(d * d, axis=-1, keepdims=True)
    return d * lax.rsqrt(var + LN_EPS) * g + b


def _ada_kernel(c_ref, w_ref, b_ref, o_ref):
    c = c_ref[...]
    s = c * _sigmoid(c)
    wh, wl = _split(w_ref[...])
    o_ref[...] = _dot3(s, wh, wl) + b_ref[...]


def _ada(c, w_ada, b_ada):
    B, D = c.shape
    n_out = w_ada.shape[1]
    rows = SUBLANES
    cp = jnp.pad(c, ((0, rows - B), (0, 0)))
    tn = 1024
    out = pl.pallas_call(
        _ada_kernel,
        out_shape=jax.ShapeDtypeStruct((rows, n_out), F32),
        grid=(n_out // tn,),
        in_specs=[pl.BlockSpec((rows, D), lambda j: (0, 0)),
                  pl.BlockSpec((D, tn), lambda j: (0, j)),
                  pl.BlockSpec((1, tn), lambda j: (0, j))],
        out_specs=pl.BlockSpec((rows, tn), lambda j: (0, j)),
        compiler_params=_params(1),
        name="ada",
    )(cp, w_ada, b_ada.reshape(1, n_out))
    return out[:B].reshape(B, 6, D)


def _inproj_kernel(x_ref, xp_ref, xn_ref, mod_ref, w_ref, cw_ref, cb_ref,
                   q_ref, kv_ref, v_ref, x1_ref, x2_ref, ga_ref, gh_ref):
    i = pl.program_id(1)
    n = pl.num_programs(1)
    C = HYENA_WIDTH
    x = x_ref[0]
    tm, D = x.shape
    shift = mod_ref[0, 0:1, :]
    scale = mod_ref[0, 1:2, :]
    h = (x * (1.0 + scale) + shift).astype(BF16)

    def seg(lo, hi):
        return jnp.dot(h, w_ref[:, lo:hi], preferred_element_type=F32)

    o_q = 0
    o_kv = o_q + ATTN_WIDTH
    o_hy = o_kv + 2 * KV_WIDTH
    o_ga = o_hy + 3 * C
    o_gh = o_ga + D
    ga_ref[0] = _sigmoid(seg(o_ga, o_ga + D)).astype(BF16)
    gh_ref[0] = _sigmoid(seg(o_gh, o_gh + D)).astype(BF16)

    u = seg(o_hy, o_hy + 3 * C)
    xe = jnp.concatenate([xp_ref[0], xn_ref[0]], axis=0)
    he = (xe * (1.0 + scale) + shift).astype(BF16)
    ue = jnp.dot(he, w_ref[:, o_hy:o_hy + 3 * C], preferred_element_type=F32)
    prow = jnp.where(i > 0, ue[SUBLANES - 1:SUBLANES], 0.0)
    nrow = jnp.where(i < n - 1, ue[SUBLANES:SUBLANES + 1], 0.0)
    rid = lax.broadcasted_iota(jnp.int32, (tm, 1), 0)
    up = jnp.where(rid == 0, prow, pltpu.roll(u, 1, 0))
    dn = jnp.where(rid == tm - 1, nrow, pltpu.roll(u, tm - 1, 0))
    conv = cw_ref[0:1, :] * up + cw_ref[1:2, :] * u + cw_ref[2:3, :] * dn + cb_ref[...]
    v_ref[0] = conv[:, :C]
    x1_ref[0] = conv[:, C:2 * C]
    x2_ref[0] = conv[:, 2 * C:]

    q_ref[0] = (seg(o_q, o_q + ATTN_WIDTH) * (HEAD_DIM ** -0.5)).astype(BF16)
    kv_ref[0] = seg(o_kv, o_kv + 2 * KV_WIDTH).astype(BF16)


def _in_proj(x, mod, w_in, conv_w, conv_b):
    B, S, D = x.shape
    C = HYENA_WIDTH
    tm = min(512, S)
    r8 = tm // SUBLANES
    nb8 = S // SUBLANES
    wb = w_in.astype(BF16)
    nw = wb.shape[1]
    row = lambda b, i: (b, i, 0)
    shapes = [(ATTN_WIDTH, BF16), (2 * KV_WIDTH, BF16), (C, F32), (C, F32), (C, F32), (D, BF16), (D, BF16)]
    return pl.pallas_call(
        _inproj_kernel,
        out_shape=[jax.ShapeDtypeStruct((B, S, w), dt) for w, dt in shapes],
        grid=(B, S // tm),
        in_specs=[pl.BlockSpec((1, tm, D), row),
                  pl.BlockSpec((1, SUBLANES, D), lambda b, i: (b, jnp.maximum(i * r8 - 1, 0), 0)),
                  pl.BlockSpec((1, SUBLANES, D), lambda b, i: (b, jnp.minimum((i + 1) * r8, nb8 - 1), 0)),
                  pl.BlockSpec((1, 6, D), lambda b, i: (b, 0, 0)),
                  pl.BlockSpec((D, nw), lambda b, i: (0, 0)),
                  pl.BlockSpec((3, 3 * C), lambda b, i: (0, 0)),
                  pl.BlockSpec((1, 3 * C), lambda b, i: (0, 0))],
        out_specs=[pl.BlockSpec((1, tm, w), row) for w, _ in shapes],
        compiler_params=_params(2),
        name="in_proj",
    )(x, x, x, mod, wb, conv_w.astype(F32), conv_b.reshape(1, 3 * C).astype(F32))


ATT_TQ = 512
ATT_QB = 128
ATT_STACK = 4


def _attn_kernel(sink_ref, q_ref, kvp_ref, kvc_ref, kvn_ref, bias_ref, o_ref, kv_scr, vx_scr, *, seq_len):
    i = pl.program_id(1)
    H = WINDOW
    TQ = q_ref.shape[1]
    Q = min(ATT_QB, TQ)
    band = Q + 2 * H
    G = N_HEADS // N_KV_HEADS
    kv_scr[0:H] = kvp_ref[0]
    kv_scr[H:H + TQ] = kvc_ref[0]
    kv_scr[H + TQ:] = kvn_ref[0]
    for kv in range(N_KV_HEADS):
        vx_scr[:, kv * LANES:kv * LANES + HEAD_DIM] = kv_scr[:, KV_WIDTH + kv * HEAD_DIM:KV_WIDTH + (kv + 1) * HEAD_DIM]
        vx_scr[:, kv * LANES + HEAD_DIM:(kv + 1) * LANES] = jnp.ones((TQ + 2 * H, LANES - HEAD_DIM), BF16)
    col = lax.broadcasted_iota(jnp.int32, (1, band), 1)
    rhead = lax.broadcasted_iota(jnp.int32, (ATT_STACK * Q, 1), 0) // Q
    for j in range(TQ // Q):
        kpos = i * TQ + j * Q - H + col
        colbias = jnp.where((kpos >= 0) & (kpos < seq_len), 0.0, NEG)
        for kv in range(N_KV_HEADS):
            kk = kv_scr[j * Q:j * Q + band, kv * HEAD_DIM:(kv + 1) * HEAD_DIM]
            vx = vx_scr[j * Q:j * Q + band, kv * LANES:(kv + 1) * LANES]
            for sub in range(G // ATT_STACK):
                first = sub * ATT_STACK
                heads = [kv * G + first + g for g in range(ATT_STACK)]
                qg = jnp.concatenate([q_ref[0, j * Q:(j + 1) * Q, h * HEAD_DIM:(h + 1) * HEAD_DIM] for h in heads],
                                     axis=0)
                s = lax.dot_general(qg, kk, (((1,), (1,)), ((), ())), preferred_element_type=F32)
                s = s + bias_ref[kv, first * Q:(first + ATT_STACK) * Q, :] + colbias
                snk = sink_ref[heads[-1]]
                for g in range(ATT_STACK - 2, -1, -1):
                    snk = jnp.where(rhead == g, sink_ref[heads[g]], snk)
                m = jnp.maximum(jnp.max(s, axis=1, keepdims=True), snk)
                p = jnp.exp(s - m).astype(BF16)
                ox = jnp.dot(p, vx, preferred_element_type=F32)
                den = ox[:, HEAD_DIM:HEAD_DIM + 1] + jnp.exp(snk - m)
                o = ox[:, :HEAD_DIM] / den
                for g, h in enumerate(heads):
                    o_ref[0, j * Q:(j + 1) * Q, h * HEAD_DIM:(h + 1) * HEAD_DIM] = o[g * Q:(g + 1) * Q].astype(BF16)


def _attention(q, kv, sink):
    B, S, _ = q.shape
    H = WINDOW
    TQ = min(ATT_TQ, S)
    Q = min(ATT_QB, TQ)
    r = TQ // H
    nq = S // H
    G = N_HEADS // N_KV_HEADS
    assert G % ATT_STACK == 0
    a = jnp.arange(Q)[:, None]
    j = jnp.arange(Q + 2 * H)[None, :]
    rel = jnp.abs(j - H - a).astype(F32)
    slopes = 2.0 ** (-8.0 * jnp.arange(1, N_HEADS + 1, dtype=F32) / N_HEADS)
    bias = jnp.where(rel[None] <= WINDOW, -slopes[:, None, None] * rel[None], NEG).astype(F32)
    bias = bias.reshape(N_KV_HEADS, G * Q, Q + 2 * H)
    cur = lambda b, i: (b, i, 0)
    return pl.pallas_call(
        functools.partial(_attn_kernel, seq_len=S),
        out_shape=jax.ShapeDtypeStruct((B, S, ATTN_WIDTH), BF16),
        grid=(B, S // TQ),
        in_specs=[pl.BlockSpec(memory_space=pltpu.SMEM),
                  pl.BlockSpec((1, TQ, ATTN_WIDTH), cur),
                  pl.BlockSpec((1, H, 2 * KV_WIDTH), lambda b, i: (b, jnp.maximum(i * r - 1, 0), 0)),
                  pl.BlockSpec((1, TQ, 2 * KV_WIDTH), cur),
                  pl.BlockSpec((1, H, 2 * KV_WIDTH), lambda b, i: (b, jnp.minimum((i + 1) * r, nq - 1), 0)),
                  pl.BlockSpec((N_KV_HEADS, G * Q, Q + 2 * H), lambda b, i: (0, 0, 0))],
        out_specs=pl.BlockSpec((1, TQ, ATTN_WIDTH), cur),
        scratch_shapes=[pltpu.VMEM((TQ + 2 * H, 2 * KV_WIDTH), BF16),
                        pltpu.VMEM((TQ + 2 * H, N_KV_HEADS * LANES), BF16)],
        compiler_params=_params(2),
        name="attn",
    )(sink.astype(F32), q, kv, kv, kv, bias)


def _filter_kernel(z_ref, w1h, w1l, b1_ref, f1_ref, w2h, w2l, b2_ref, f2_ref, w3h, w3l, dec_ref,
                   k_ref, s_ref):
    i = pl.program_id(0)
    z = z_ref[...]
    h1 = jnp.sin(f1_ref[...] * (_dot3(z, w1h[...], w1l[...]) + b1_ref[...]))
    h2 = jnp.sin(f2_ref[...] * (_dot3(h1, w2h[...], w2l[...]) + b2_ref[...]))
    k = _dot3(h2, w3h[...], w3l[...])
    t = z[:, 0:1]
    k = k * (jnp.exp(-t * jnp.abs(dec_ref[...])) + WINDOW_SHIFT)
    k_ref[...] = k

    @pl.when(i == 0)
    def _():
        s_ref[...] = jnp.zeros_like(s_ref)

    s_ref[...] += jnp.sum(jnp.abs(k), axis=0, keepdims=True)


def _filter_embedding(L):
    t = np.linspace(0.0, 1.0, L, dtype=np.float32).astype(np.float64)[:, None]
    w = (2.0 * math.pi * np.arange(L, dtype=np.float32) / np.float32(L)).astype(np.float64)[:, None]
    bands = np.linspace(1e-4, FILTER_BANDS - 1, FILTER_BANDS, dtype=np.float32).astype(np.float64)[None, :]
    bw = (bands.astype(np.float32) * w.astype(np.float32)).astype(np.float64)
    z = np.concatenate([t, np.cos(bw), -np.sin(bw)], axis=-1)
    zp = np.zeros((L, LANES), np.float32)
    zp[:, :FILTER_EMB] = z.astype(np.float32)
    return jnp.asarray(zp)


def _pad2(a, r, c):
    return jnp.zeros((r, c), F32).at[:a.shape[0], :a.shape[1]].set(a.astype(F32))


def _filters(L, fw1, fb1, ff1, fw2, fb2, ff2, fw3, decay):
    H = LANES
    nf = fw3.shape[1]
    z = _filter_embedding(L)
    w1h, w1l = _split(_pad2(fw1, H, H))
    w2h, w2l = _split(_pad2(fw2, H, H))
    w3h, w3l = _split(_pad2(fw3, H, nf))
    b1 = _pad2(fb1[None], 1, H)
    f1 = _pad2(ff1[None], 1, H)
    b2 = _pad2(fb2[None], 1, H)
    f2 = _pad2(ff2[None], 1, H)
    tr = min(512, L)
    full = lambda r, c: pl.BlockSpec((r, c), lambda i: (0, 0))
    return pl.pallas_call(
        _filter_kernel,
        out_shape=[jax.ShapeDtypeStruct((L, nf), F32), jax.ShapeDtypeStruct((1, nf), F32)],
        grid=(L // tr,),
        in_specs=[pl.BlockSpec((tr, H), lambda i: (i, 0)),
                  full(H, H), full(H, H), full(1, H), full(1, H),
                  full(H, H), full(H, H), full(1, H), full(1, H),
                  full(H, nf), full(H, nf), full(1, nf)],
        out_specs=[pl.BlockSpec((tr, nf), lambda i: (i, 0)), full(1, nf)],
        compiler_params=_params(1),
        name="filter",
    )(z, w1h, w1l, b1, f1, w2h, w2l, b2, f2, w3h, w3l, decay.reshape(1, nf).astype(F32))


def _np_bf16(m64):
    return jnp.asarray(m64.astype(np.float32).astype(BF16))


def _dft_constants(L):
    N = 2 * L
    n2 = LANES
    n1 = N // n2
    h1 = n1 // 2
    k1 = np.arange(n1)[:, None]
    s1 = np.arange(h1)[None, :]
    ang = -2.0 * np.pi * ((k1 * s1) % n1) / n1
    wr, wi = np.cos(ang), np.sin(ang)
    w1_filt = np.block([[wr, wr], [wi, wi], [wr, -wr], [wi, -wi]])
    w1_cplx = np.block([[wr, -wi], [wi, wr]])
    vr, vi = wr.T / N, -wi.T / N
    w3 = np.block([[vr, -vi], [vi, vr]])
    k2 = np.arange(n2)[:, None]
    s2 = np.arange(n2)[None, :]
    a2 = -2.0 * np.pi * ((k2 * s2) % n2) / n2
    w2r, w2i = jnp.asarray(np.cos(a2), F32), jnp.asarray(np.sin(a2), F32)
    at = -2.0 * np.pi * ((np.arange(n1)[:, None] * s2) % N) / N
    twr, twi = jnp.asarray(np.cos(at), F32), jnp.asarray(np.sin(at), F32)
    mr = w2r[None] * twr[:, None, :] - w2i[None] * twi[:, None, :]
    mi = w2r[None] * twi[:, None, :] + w2i[None] * twr[:, None, :]
    fwd = jnp.concatenate([jnp.concatenate([mr, -mi], axis=2),
                           jnp.concatenate([mi, mr], axis=2)], axis=1)
    fwd = fwd.astype(BF16)
    return dict(n1=n1, w1_filt=_np_bf16(w1_filt), w1_cplx=_np_bf16(w1_cplx), w3=_np_bf16(w3),
                fwd=fwd)


SCH = 8


def _dft1_kernel(x_ref, w_ref, a_ref, *, n1):
    w = w_ref[...]
    for j in range(SCH):
        rhs = jnp.concatenate([x_ref[0, 0, :, j, :], x_ref[0, 1, :, j, :]], axis=0)
        res = jnp.dot(w, rhs.astype(BF16), preferred_element_type=F32)
        a_ref[0, :, j, :] = _pack2(res[:n1], res[n1:])


def _dft1_data(x, consts):
    B, L, C = x.shape
    n1 = consts["n1"]
    h1 = n1 // 2
    xv = x.reshape(B // 2, 2, h1, LANES, C)
    return pl.pallas_call(
        functools.partial(_dft1_kernel, n1=n1),
        out_shape=jax.ShapeDtypeStruct((B // 2, n1, LANES, C), F32),
        grid=(B // 2, LANES // SCH),
        in_specs=[pl.BlockSpec((1, 2, h1, SCH, C), lambda p, j: (p, 0, 0, j, 0)),
                  pl.BlockSpec((2 * n1, n1), lambda p, j: (0, 0))],
        out_specs=pl.BlockSpec((1, n1, SCH, C), lambda p, j: (p, 0, j, 0)),
        compiler_params=_params(2),
        name="dft1",
    )(xv, consts["w1_cplx"])


def _dft1f_kernel(x_ref, w_ref, a_ref, *, n1):
    C = HYENA_WIDTH
    w = w_ref[...]
    for j in range(SCH):
        rhs = jnp.concatenate([x_ref[:, j, :C], x_ref[:, j, C:]], axis=0)
        res = jnp.dot(w, rhs.astype(BF16), preferred_element_type=F32)
        a_ref[0, :, 0, j, :] = _pack2(res[:n1], res[n1:2 * n1])
        a_ref[0, :, 1, j, :] = _pack2(res[2 * n1:3 * n1], res[3 * n1:])


def _dft1_filter(kraw, consts):
    L, nf = kraw.shape
    C = HYENA_WIDTH
    n_ord = nf // (2 * C)
    n1 = consts["n1"]
    h1 = n1 // 2
    kv = kraw.reshape(h1, LANES, nf)
    return pl.pallas_call(
        functools.partial(_dft1f_kernel, n1=n1),
        out_shape=jax.ShapeDtypeStruct((n_ord, n1, 2, LANES, C), F32),
        grid=(n_ord, LANES // SCH),
        in_specs=[pl.BlockSpec((h1, SCH, 2 * C), lambda o, j: (0, j, o)),
                  pl.BlockSpec((4 * n1, n1), lambda o, j: (0, 0))],
        out_specs=pl.BlockSpec((1, n1, 2, SCH, C), lambda o, j: (o, 0, 0, j, 0)),
        compiler_params=_params(2),
        name="dft1f",
    )(kv, consts["w1_filt"])


KCH = 8


def _midf_kernel(a_ref, f_ref, inv_ref, b0_ref, h_ref):
    n2 = LANES
    sc = inv_ref[0]
    for k in range(KCH):
        p = jnp.concatenate(_unpack2(a_ref[0, k, :n2, :]), axis=0).astype(BF16)
        q = jnp.concatenate(_unpack2(a_ref[0, k, n2:, :]), axis=0).astype(BF16)
        h_re = jnp.dot(f_ref[k, :n2, :], p, preferred_element_type=F32)
        h_im = jnp.dot(f_ref[k, n2:, :], q, preferred_element_type=F32)
        h_ref[0, k] = _pack2((h_re - b0_ref[0]) * sc, h_im * sc)


def _filter_spectrum(af, inv_den, bwd0, consts):
    n_ord, n1, _, n2, C = af.shape
    a = af.reshape(n_ord, n1, 2 * n2, C)
    tab = pl.BlockSpec((KCH, 2 * n2, 2 * n2), lambda k, o: (k, 0, 0))
    vec = pl.BlockSpec((1, 1, C), lambda k, o: (o, 0, 0))
    return pl.pallas_call(
        _midf_kernel,
        out_shape=jax.ShapeDtypeStruct((n_ord, n1, n2, C), F32),
        grid=(n1 // KCH, n_ord),
        in_specs=[pl.BlockSpec((1, KCH, 2 * n2, C), lambda k, o: (o, k, 0, 0)), tab, vec, vec],
        out_specs=pl.BlockSpec((1, KCH, n2, C), lambda k, o: (o, k, 0, 0)),
        compiler_params=_params(2),
        name="midf",
    )(a, consts["fwd"], inv_den, bwd0)


def _mid_kernel(a_ref, f_ref, h_ref, b_ref):
    n2 = LANES
    for k in range(KCH):
        a = jnp.concatenate(_unpack2(a_ref[0, k]), axis=0).astype(BF16)
        x = jnp.dot(f_ref[k], a, preferred_element_type=F32)
        xr, xi = x[:n2], x[n2:]
        hr, hi = _unpack2(h_ref[0, k])
        y = jnp.concatenate([xr * hr - xi * hi, xr * hi + xi * hr], axis=0)
        b = lax.dot_general(f_ref[k], y.astype(BF16), (((0,), (0,)), ((), ())), preferred_element_type=F32)
        b_ref[0, k] = _pack2(b[:n2], b[n2:])


def _mid(a, hspec, order, consts):
    P, n1, n2, C = a.shape
    tab = pl.BlockSpec((KCH, 2 * n2, 2 * n2), lambda k, p: (k, 0, 0))
    return pl.pallas_call(
        _mid_kernel,
        out_shape=jax.ShapeDtypeStruct((P, n1, n2, C), F32),
        grid=(n1 // KCH, P),
        in_specs=[pl.BlockSpec((1, KCH, n2, C), lambda k, p: (p, k, 0, 0)),
                  tab,
                  pl.BlockSpec((1, KCH, n2, C), lambda k, p: (order, k, 0, 0))],
        out_specs=pl.BlockSpec((1, KCH, n2, C), lambda k, p: (p, k, 0, 0)),
        compiler_params=_params(2),
        name="mid",
    )(a, consts["fwd"], hspec)


def _dft3_kernel(b_ref, w_ref, v_ref, g_ref, skip_ref, *rest, h1, chain):
    if chain:
        w1_ref, z_ref, a_ref, slab_ref = rest
        w1 = w1_ref[...]
    else:
        z_ref, slab_ref = rest
    w = w_ref[...]
    skip = skip_ref[0]
    n1 = 2 * h1
    for j in range(SCH):
        slab_ref[...] = b_ref[0, :, j, :]
        rhs = jnp.concatenate(_unpack2(slab_ref[...]), axis=0)
        y = jnp.dot(w, rhs.astype(BF16), preferred_element_type=F32)
        z = [g_ref[0, r, :, j, :] * (y[r * h1:(r + 1) * h1] + v_ref[0, r, :, j, :] * skip) for r in range(2)]
        for r in range(2):
            z_ref[0, r, :, j, :] = z[r]
        if chain:
            res = jnp.dot(w1, jnp.concatenate(z, axis=0).astype(BF16), preferred_element_type=F32)
            a_ref[0, :, j, :] = _pack2(res[:n1], res[n1:])


def _dft3_gate(b5, v, gate, skip, consts, chain):
    P, n1, n2, C = b5.shape
    h1 = n1 // 2
    B, L, _ = v.shape
    five = lambda t: t.reshape(P, 2, h1, n2, C)
    dat = pl.BlockSpec((1, 2, h1, SCH, C), lambda p, j: (p, 0, 0, j, 0))
    packed = pl.BlockSpec((1, n1, SCH, C), lambda p, j: (p, 0, j, 0))
    in_specs = [packed, pl.BlockSpec((n1, 2 * n1), lambda p, j: (0, 0)), dat, dat,
                pl.BlockSpec((1, C), lambda p, j: (0, 0))]
    args = [b5, consts["w3"], five(v), five(gate), skip.reshape(1, C).astype(F32)]
    out_shape = [jax.ShapeDtypeStruct((P, 2, h1, n2, C), F32)]
    out_specs = [dat]
    if chain:
        in_specs.append(pl.BlockSpec((2 * n1, n1), lambda p, j: (0, 0)))
        args.append(consts["w1_cplx"])
        out_shape.append(jax.ShapeDtypeStruct((P, n1, n2, C), F32))
        out_specs.append(packed)
    outs = pl.pallas_call(
        functools.partial(_dft3_kernel, h1=h1, chain=chain),
        out_shape=out_shape,
        grid=(P, n2 // SCH),
        in_specs=in_specs,
        out_specs=out_specs,
        scratch_shapes=[pltpu.VMEM((n1, C), F32)],
        compiler_params=_params(2),
        name="dft3",
    )(*args)
    z = outs[0].reshape(B, L, C)
    return (z, outs[1]) if chain else (z, None)


def _hyena(v, x1, x2, fw1, fb1, ff1, fw2, fb2, ff2, fw3, decay, skip):
    B, L, C = v.shape
    consts = _dft_constants(L)
    kraw, ksum = _filters(L, fw1, fb1, ff1, fw2, fb2, ff2, fw3, decay)
    ks = ksum.reshape(2, 2, C)
    inv_den = (1.0 / (ks[:, 0] + ks[:, 1])).reshape(2, 1, C)
    bwd0 = kraw[0].reshape(2, 2, C)[:, 1].reshape(2, 1, C)
    hspec = _filter_spectrum(_dft1_filter(kraw, consts), inv_den, bwd0, consts)
    gates = (x1, x2)
    z, a5 = v, _dft1_data(v, consts)
    for o, gate in enumerate(gates):
        b5 = _mid(a5, hspec, o, consts)
        z, a5 = _dft3_gate(b5, z, gate, skip[o], consts, chain=o + 1 < len(gates))
    return z


def _merge_kernel(attn_ref, hy_ref, ga_ref, gh_ref, x_ref, mod_ref, wa_ref, wh_ref, wo_ref,
                  g1_ref, b1_ref, rwh_ref, rwl_ref, rb_ref, tri_ref,
                  x1_ref, h2_ref, route_ref, wts_ref, cnt_ref, carry_ref):
    @pl.when((pl.program_id(0) == 0) & (pl.program_id(1) == 0))
    def _():
        carry_ref[...] = jnp.zeros_like(carry_ref)

    logits = _merge_dense(attn_ref, hy_ref, ga_ref, gh_ref, x_ref, mod_ref, wa_ref, wh_ref, wo_ref,
                          g1_ref, b1_ref, rwh_ref, rwl_ref, rb_ref, x1_ref, h2_ref)
    _route_rows(logits, tri_ref, route_ref, wts_ref, carry_ref)
    cnt_ref[...] = carry_ref[...]


def _merge_dense(attn_ref, hy_ref, ga_ref, gh_ref, x_ref, mod_ref, wa_ref, wh_ref, wo_ref,
                 g1_ref, b1_ref, rwh_ref, rwl_ref, rb_ref, x1_ref, h2_ref):
    a = jnp.dot(attn_ref[0], wa_ref[...], preferred_element_type=F32)
    hy = jnp.dot(hy_ref[0].astype(BF16), wh_ref[...], preferred_element_type=F32)
    merged = ga_ref[0].astype(F32) * a + gh_ref[0].astype(F32) * hy
    y = jnp.dot(merged.astype(BF16), wo_ref[...], preferred_element_type=F32)
    gate1 = mod_ref[0, 2:3, :]
    shift2 = mod_ref[0, 3:4, :]
    scale2 = mod_ref[0, 4:5, :]
    x1 = _layer_norm(DN_ALPHA * x_ref[0] + gate1 * y, g1_ref[...], b1_ref[...])
    x1_ref[0] = x1
    h2 = x1 * (1.0 + scale2) + shift2
    half = h2.shape[1] // 2
    h2_ref[0] = _pack2(h2[:, :half], h2[:, half:])
    return _dot3(h2, rwh_ref[...], rwl_ref[...]) + rb_ref[...]


def _route_rows(logits, tri_ref, route_ref, wts_ref, carry_ref):
    tm = logits.shape[0]
    lane = lax.broadcasted_iota(jnp.int32, (tm, LANES), 1)
    lanef = lane.astype(F32)
    big = float(LANES)

    def first_lane(mask):
        return jnp.min(jnp.where(mask, lanef, big), axis=1, keepdims=True).astype(jnp.int32)

    gmask = lane < N_GROUPS
    gl = jnp.where(gmask, logits, NEG)
    gmax = jnp.max(gl, axis=1, keepdims=True)
    gidx = first_lane(gl == gmax)
    pg = 1.0 / jnp.sum(jnp.exp(gl - gmax), axis=1, keepdims=True)
    lo = ROUTE_OFF + gidx * EXPERTS_PER_GROUP
    emask = (lane >= lo) & (lane < lo + EXPERTS_PER_GROUP)
    el = jnp.where(emask, logits, NEG)
    v1 = jnp.max(el, axis=1, keepdims=True)
    i1 = first_lane(el == v1)
    el2 = jnp.where(emask & (lane != i1), logits, NEG)
    v2 = jnp.max(el2, axis=1, keepdims=True)
    i2 = first_lane(el2 == v2)
    e21 = jnp.exp(v2 - v1)
    w1 = pg / (1.0 + e21)
    w2 = pg * e21 / (1.0 + e21)

    sel1 = lane == i1
    sel2 = lane == i2
    onehot = jnp.where(sel1 | sel2, 1.0, 0.0)
    prefix = jnp.dot(tri_ref[...], onehot.astype(BF16), preferred_element_type=F32) + carry_ref[...]
    r1 = jnp.sum(jnp.where(sel1, prefix, 0.0), axis=1, keepdims=True)
    r2 = jnp.sum(jnp.where(sel2, prefix, 0.0), axis=1, keepdims=True)
    carry_ref[...] += jnp.sum(onehot, axis=0, keepdims=True)

    ranks = jnp.where(lane == 2, r1, jnp.where(lane == 3, r2, 0.0)).astype(jnp.int32)
    route_ref[0] = jnp.where(lane == 0, i1 - ROUTE_OFF, jnp.where(lane == 1, i2 - ROUTE_OFF, ranks))
    wts_ref[0] = jnp.where(lane == 0, w1, jnp.where(lane == 1, w2, 0.0))


def _merge(attn, hy, ga, gh, x, mod, w_attn_o, w_hy_o, w_out, ln1_g, ln1_b, rg_w, rg_b, re_w, re_b):
    B, S, D = x.shape
    tm = min(512, S)
    spare = LANES - N_GROUPS - N_EXPERTS
    rw = jnp.concatenate([rg_w, re_w, jnp.zeros((D, spare), F32)], axis=1)
    rb = jnp.concatenate([rg_b, re_b, jnp.zeros((spare,), F32)]).reshape(1, LANES)
    rwh, rwl = _split(rw)
    tri = (jnp.arange(tm)[:, None] > jnp.arange(tm)[None, :]).astype(BF16)
    row = lambda b, i: (b, i, 0)
    full = lambda r, c: pl.BlockSpec((r, c), lambda b, i: (0, 0))
    outs = [jax.ShapeDtypeStruct((B, S, D), F32), jax.ShapeDtypeStruct((B, S, D // 2), F32),
            jax.ShapeDtypeStruct((B, S, LANES), jnp.int32), jax.ShapeDtypeStruct((B, S, LANES), F32),
            jax.ShapeDtypeStruct((1, LANES), F32)]
    return pl.pallas_call(
        _merge_kernel,
        out_shape=outs,
        grid=(B, S // tm),
        in_specs=[pl.BlockSpec((1, tm, ATTN_WIDTH), row), pl.BlockSpec((1, tm, HYENA_WIDTH), row),
                  pl.BlockSpec((1, tm, D), row), pl.BlockSpec((1, tm, D), row), pl.BlockSpec((1, tm, D), row),
                  pl.BlockSpec((1, 6, D), lambda b, i: (b, 0, 0)),
                  full(ATTN_WIDTH, D), full(HYENA_WIDTH, D), full(D, D),
                  full(1, D), full(1, D), full(D, LANES), full(D, LANES), full(1, LANES), full(tm, tm)],
        out_specs=[pl.BlockSpec((1, tm, D), row), pl.BlockSpec((1, tm, D // 2), row),
                   pl.BlockSpec((1, tm, LANES), row), pl.BlockSpec((1, tm, LANES), row), full(1, LANES)],
        scratch_shapes=[pltpu.VMEM((1, LANES), F32)],
        compiler_params=_params(2),
        name="merge",
    )(attn, hy, ga, gh, x, mod, w_attn_o.astype(BF16), w_hy_o.astype(BF16), w_out.astype(BF16),
      ln1_g.reshape(1, D), ln1_b.reshape(1, D), rwh, rwl, rb, tri)


SC_ROWS = 64


def _sc_workers():
    info = plsc.get_sparse_core_info()
    return info.num_cores, info.num_cores * info.num_subcores


def _sc_split(n):
    _, workers = _sc_workers()
    per_worker = n // workers
    chunks = per_worker // SC_ROWS
    assert per_worker * workers == n and chunks * SC_ROWS == per_worker and chunks % 2 == 0
    return workers, per_worker, chunks


def _sc_scatter_rows(src, idx0, idx1, n_out):
    n, width = src.shape
    nc, _ = _sc_workers()
    workers, per_worker, chunks = _sc_split(n)
    mesh = plsc.VectorSubcoreMesh(core_axis_name="c", subcore_axis_name="s")

    def body(src_hbm, i0_hbm, i1_hbm, out_hbm, i0_v, i1_v, rows_v, sem, ssem):
        wid = lax.axis_index("s") * nc + lax.axis_index("c")
        base = wid * per_worker
        pltpu.sync_copy(i0_hbm.at[wid], i0_v)
        pltpu.sync_copy(i1_hbm.at[wid], i1_v)

        def load(chunk, buf):
            return pltpu.make_async_copy(src_hbm.at[pl.ds(base + chunk * SC_ROWS, SC_ROWS)], rows_v.at[buf], sem)

        load(0, 0).start()

        @pl.loop(0, chunks, step=2)
        def _(c):
            for b in range(2):
                chunk = c + b
                load(chunk, b).wait()

                @pl.when(chunk + 1 < chunks)
                def _():
                    load(chunk + 1, 1 - b).start()

                first = pltpu.make_async_copy(rows_v.at[b], out_hbm.at[i0_v.at[chunk]], ssem)
                second = pltpu.make_async_copy(rows_v.at[b], out_hbm.at[i1_v.at[chunk]], ssem)
                first.start()
                second.start()
                first.wait()
                second.wait()

    shaped = lambda i: i.reshape(workers, chunks, SC_ROWS)
    return pl.kernel(
        body,
        out_type=jax.ShapeDtypeStruct((n_out, width), src.dtype),
        mesh=mesh,
        scratch_types=[pltpu.VMEM((chunks, SC_ROWS), jnp.int32),
                       pltpu.VMEM((chunks, SC_ROWS), jnp.int32),
                       pltpu.VMEM((2, SC_ROWS, width), src.dtype),
                       pltpu.SemaphoreType.DMA, pltpu.SemaphoreType.DMA],
        name="sc_scatter",
    )(src, shaped(idx0), shaped(idx1))


def _expert_kernel(first_ref, nb_ref, sz_ref, tot_ref, w1_ref, w3_ref, w2_ref, xb_ref, yb_ref,
                   xbuf, ybuf, c1_ref, c3_ref, c2_ref, lsem, ssem):
    e = pl.program_id(0)
    nb = nb_ref[e]
    first = first_ref[e]
    total = tot_ref[0]
    rows = xbuf.shape[1]

    def load(g, slot):
        src = xb_ref.at[pl.ds(pl.multiple_of(g * rows, rows), rows)]
        return pltpu.make_async_copy(src, xbuf.at[slot], lsem.at[slot])

    def store(g, slot):
        dst = yb_ref.at[pl.ds(pl.multiple_of(g * rows, rows), rows)]
        return pltpu.make_async_copy(ybuf.at[slot], dst, ssem.at[slot])

    @pl.when((e == 0) & (total > 0))
    def _():
        load(0, 0).start()

    @pl.when(nb > 0)
    def _():
        c1_ref[...] = w1_ref[0].astype(BF16)
        c3_ref[...] = w3_ref[0].astype(BF16)
        c2_ref[...] = w2_ref[0].astype(BF16)

        def block(j, carry):
            g = first + j
            slot = lax.rem(g, 2)
            load(g, slot).wait()

            @pl.when(g + 1 < total)
            def _():
                load(g + 1, 1 - slot).start()

            @pl.when(g >= 2)
            def _():
                store(g - 2, slot).wait()

            n_valid = sz_ref[e] - j * rows
            rid = lax.broadcasted_iota(jnp.int32, (rows, 1), 0)
            xa, xb = _unpack2(jnp.where(rid < n_valid, xbuf[slot], 0.0))
            x = jnp.concatenate([xa, xb], axis=1).astype(BF16)
            a = jnp.dot(x, c1_ref[...], preferred_element_type=F32)
            gate = jnp.dot(x, c3_ref[...], preferred_element_type=F32)
            hmid = (a * _sigmoid(a) * gate).astype(BF16)
            y = jnp.dot(hmid, c2_ref[...], preferred_element_type=F32)
            half = y.shape[1] // 2
            ybuf[slot] = _pack2(y[:, :half], y[:, half:])
            store(g, slot).start()
            return carry

        lax.fori_loop(0, nb, block, 0)

    @pl.when(e == pl.num_programs(0) - 1)
    def _():
        for back in (2, 1):
            @pl.when(total >= back)
            def _():
                g = total - back
                store(g, lax.rem(g, 2)).wait()


def _experts(xb, first_blk, n_blk, sizes, w1, w3, w2):
    P, W = xb.shape
    E, D, DE = w1.shape
    total = jnp.sum(n_blk, keepdims=True)
    wspec = lambda r, c: pl.BlockSpec((1, r, c), lambda e, *_: (e, 0, 0))
    grid_spec = pltpu.PrefetchScalarGridSpec(
        num_scalar_prefetch=4,
        grid=(E,),
        in_specs=[wspec(D, DE), wspec(D, DE), wspec(DE, D), pl.BlockSpec(memory_space=pl.ANY)],
        out_specs=pl.BlockSpec(memory_space=pl.ANY),
        scratch_shapes=[pltpu.VMEM((2, MOE_BLOCK, W), F32), pltpu.VMEM((2, MOE_BLOCK, W), F32),
                        pltpu.VMEM((D, DE), BF16), pltpu.VMEM((D, DE), BF16), pltpu.VMEM((DE, D), BF16),
                        pltpu.SemaphoreType.DMA((2,)), pltpu.SemaphoreType.DMA((2,))],
    )
    return pl.pallas_call(
        _expert_kernel,
        out_shape=jax.ShapeDtypeStruct((P, W), F32),
        grid_spec=grid_spec,
        compiler_params=_params(1),
        name="experts",
    )(first_blk, n_blk, sizes, total, w1, w3, w2, xb)


def _sc_gather_rows(table, idx):
    n, width = idx.shape[0], table.shape[1]
    nc, _ = _sc_workers()
    workers, per_worker, chunks = _sc_split(n)
    mesh = plsc.VectorSubcoreMesh(core_axis_name="c", subcore_axis_name="s")

    def body(table_hbm, idx_hbm, out_hbm, idx_v, rows_v, sem):
        wid = lax.axis_index("s") * nc + lax.axis_index("c")
        base = wid * per_worker
        pltpu.sync_copy(idx_hbm.at[wid], idx_v)

        def gather(chunk, buf):
            return pltpu.make_async_copy(table_hbm.at[idx_v.at[chunk]], rows_v.at[buf], sem)

        gather(0, 0).start()

        @pl.loop(0, chunks, step=2)
        def _(c):
            for b in range(2):
                chunk = c + b
                gather(chunk, b).wait()

                @pl.when(chunk + 1 < chunks)
                def _():
                    gather(chunk + 1, 1 - b).start()

                pltpu.sync_copy(rows_v.at[b], out_hbm.at[pl.ds(base + chunk * SC_ROWS, SC_ROWS)])

    return pl.kernel(
        body,
        out_type=jax.ShapeDtypeStruct((n, width), table.dtype),
        mesh=mesh,
        scratch_types=[pltpu.VMEM((chunks, SC_ROWS), jnp.int32),
                       pltpu.VMEM((2, SC_ROWS, width), table.dtype),
                       pltpu.SemaphoreType.DMA],
        name="sc_gather",
    )(table, idx.reshape(workers, chunks, SC_ROWS))


def _combine_dense_kernel(r0_ref, r1_ref, wts_ref, x1_ref, mod_ref, g_ref, b_ref, *rest):
    o_ref = rest[-1]
    w = wts_ref[...]
    y0 = jnp.concatenate(_unpack2(r0_ref[0]), axis=1)
    y1 = jnp.concatenate(_unpack2(r1_ref[0]), axis=1)
    y = w[:, 0:1] * y0 + w[:, 1:2] * y1
    gate2 = mod_ref[0, 5:6, :]
    o_ref[...] = _layer_norm(DN_ALPHA * x1_ref[...] + gate2 * y, g_ref[...], b_ref[...])


def _combine_dense(rows, wts, x1, mod, ln2_g, ln2_b, S, b, out):
    T, D = x1.shape
    tm = min(512, S)
    per_b = S // tm
    here = lambda i: (b * per_b + i, 0)
    in_specs = [pl.BlockSpec((1, tm, rows.shape[2]), lambda i: (0, i, 0)),
                pl.BlockSpec((1, tm, rows.shape[2]), lambda i: (1, i, 0)),
                pl.BlockSpec((tm, LANES), here),
                pl.BlockSpec((tm, D), here),
                pl.BlockSpec((1, 6, D), lambda i: (b, 0, 0)),
                pl.BlockSpec((1, D), lambda i: (0, 0)),
                pl.BlockSpec((1, D), lambda i: (0, 0))]
    args = [rows, rows, wts, x1, mod, ln2_g.reshape(1, D), ln2_b.reshape(1, D)]
    aliases = {}
    if out is not None:
        in_specs.append(pl.BlockSpec(memory_space=pl.ANY))
        aliases = {len(args): 0}
        args.append(out)
    return pl.pallas_call(
        _combine_dense_kernel,
        out_shape=jax.ShapeDtypeStruct((T, D), F32),
        grid=(per_b,),
        in_specs=in_specs,
        out_specs=pl.BlockSpec((tm, D), here),
        input_output_aliases=aliases,
        compiler_params=_params(1),
        name="combine",
    )(*args)


def _moe(h2, x1, route, wts, counts, mod, w1, w3, w2, ln2_g, ln2_b):
    B, S, D = x1.shape
    T = B * S
    P = 2 * T + N_EXPERTS * MOE_BLOCK
    sizes = counts[0, ROUTE_OFF:ROUTE_OFF + N_EXPERTS].astype(jnp.int32)
    n_blk = (sizes + MOE_BLOCK - 1) // MOE_BLOCK
    psizes = n_blk * MOE_BLOCK
    poffs = jnp.cumsum(psizes) - psizes
    r4 = route.reshape(T, LANES)[:, :4]
    sel = r4[:, :2, None] == jnp.arange(N_EXPERTS, dtype=jnp.int32)[None, None, :]
    dest = r4[:, 2:4] + jnp.sum(jnp.where(sel, poffs[None, None, :], 0), axis=-1)
    xb = _sc_scatter_rows(h2.reshape(T, D // 2), dest[:, 0], dest[:, 1], P)
    yb = _experts(xb, poffs // MOE_BLOCK, n_blk, sizes, w1, w3, w2)
    out = None
    for b in range(B):
        slot_major = dest[b * S:(b + 1) * S].T.reshape(2 * S)
        rows = _sc_gather_rows(yb, slot_major).reshape(2, S, yb.shape[1])
        out = _combine_dense(rows, wts.reshape(T, LANES), x1.reshape(T, D), mod, ln2_g, ln2_b, S, b, out)
    return out.reshape(B, S, D)


def _layer(x, c, w_ada, b_ada, w_in, conv_w, conv_b, fw1, fb1, ff1, fw2, fb2, ff2, fw3, decay, skip,
           w_hy_o, w_attn_o, attn_sink, w_out, ln1_g, ln1_b, rg_w, rg_b, re_w, re_b, ew1, ew3, ew2,
           ln2_g, ln2_b):
    mod = _ada(c, w_ada, b_ada)
    q, kv, hv, hx1, hx2, ga, gh = _in_proj(x, mod, w_in, conv_w, conv_b)
    attn = _attention(q, kv, attn_sink)
    hy = _hyena(hv, hx1, hx2, fw1, fb1, ff1, fw2, fb2, ff2, fw3, decay, skip)
    x1, h2, route, wts, counts = _merge(attn, hy, ga, gh, x, mod, w_attn_o, w_hy_o, w_out,
                                        ln1_g, ln1_b, rg_w, rg_b, re_w, re_b)
    return _moe(h2, x1, route, wts, counts, mod, ew1, ew3, ew2, ln2_g, ln2_b)


def kernel(x, c, w_ada, b_ada, w_in, conv_w, conv_b, filt_w1, filt_b1, filt_freq1, filt_w2, filt_b2, filt_freq2, filt_w3, filt_decay, hy_skip, w_hy_o, w_attn_o, attn_sink, w_out, ln1_g, ln1_b, router_group_w, router_group_b, router_expert_w, router_expert_b, exp_w1, exp_w3, exp_w2, ln2_g, ln2_b):
    for l in range(w_ada.shape[0]):
        x = _layer(x, c, w_ada[l], b_ada[l], w_in[l], conv_w[l], conv_b[l], filt_w1[l], filt_b1[l],
                   filt_freq1[l], filt_w2[l], filt_b2[l], filt_freq2[l], filt_w3[l], filt_decay[l],
                   hy_skip[l], w_hy_o[l], w_attn_o[l], attn_sink[l], w_out[l], ln1_g[l], ln1_b[l],
                   router_group_w[l], router_group_b[l], router_expert_w[l], router_expert_b[l],
                   exp_w1[l], exp_w3[l], exp_w2[l], ln2_g[l], ln2_b[l])
    return x
```

```python
import functools
import math

import numpy as np
import jax
import jax.numpy as jnp
from jax import lax
from jax.experimental import pallas as pl
from jax.experimental.pallas import tpu as pltpu
from jax.experimental.pallas import tpu_sc as plsc

F32 = jnp.float32
BF16 = jnp.bfloat16

N_HEADS = 8
N_KV_HEADS = 2
HEAD_DIM = 64
ATTN_WIDTH = N_HEADS * HEAD_DIM
KV_WIDTH = N_KV_HEADS * HEAD_DIM
WINDOW = 128
HYENA_WIDTH = 512
FILTER_EMB = 33
FILTER_BANDS = (FILTER_EMB - 1) // 2
WINDOW_SHIFT = 0.05
N_GROUPS = 8
EXPERTS_PER_GROUP = 8
N_EXPERTS = N_GROUPS * EXPERTS_PER_GROUP
D_EXPERT = 512
MOE_BLOCK = 512
LN_EPS = 1e-5
DEPTH = 1
DN_ALPHA = (2.0 * DEPTH) ** 0.25
NEG = -1e30

LANES = 128
SUBLANES = 8
ROUTE_OFF = N_GROUPS
VMEM_LIMIT = 56 * 1024 * 1024


def _params(n_axes, vmem=VMEM_LIMIT):
    return pltpu.CompilerParams(dimension_semantics=("arbitrary",) * n_axes, vmem_limit_bytes=vmem)


def _split(a):
    hi = a.astype(BF16)
    lo = (a - hi.astype(F32)).astype(BF16)
    return hi, lo


def _dot3(a, b_hi, b_lo):
    a_hi, a_lo = _split(a)
    acc = jnp.dot(a_hi, b_hi, preferred_element_type=F32)
    acc = acc + jnp.dot(a_hi, b_lo, preferred_element_type=F32)
    acc = acc + jnp.dot(a_lo, b_hi, preferred_element_type=F32)
    return acc


def _pack2(a, b):
    ia = lax.bitcast_convert_type(a.astype(BF16).astype(F32), jnp.int32)
    ib = lax.bitcast_convert_type(b.astype(BF16).astype(F32), jnp.int32)
    return lax.bitcast_convert_type(ia | lax.shift_right_logical(ib, 16), F32)


def _unpack2(p):
    p = lax.bitcast_convert_type(p, jnp.int32)
    a = lax.bitcast_convert_type(p & jnp.int32(-65536), F32)
    b = lax.bitcast_convert_type(lax.shift_left(p, 16), F32)
    return a, b


def _sigmoid(x):
    return 0.5 * jnp.tanh(0.5 * x) + 0.5


def _layer_norm(r, g, b):
    mu = jnp.mean(r, axis=-1, keepdims=True)
    d = r - mu
    var = jnp.mean(d * d, axis=-1, keepdims=True)
    return d * lax.rsqrt(var + LN_EPS) * g + b


def _ada_kernel(c_ref, w_ref, b_ref, o_ref):
    c = c_ref[...]
    s = c * _sigmoid(c)
    wh, wl = _split(w_ref[...])
    o_ref[...] = _dot3(s, wh, wl) + b_ref[...]


def _ada(c, w_ada, b_ada):
    B, D = c.shape
    n_out = w_ada.shape[1]
    rows = SUBLANES
    cp = jnp.pad(c, ((0, rows - B), (0, 0)))
    tn = 1024
    out = pl.pallas_call(
        _ada_kernel,
        out_shape=jax.ShapeDtypeStruct((rows, n_out), F32),
        grid=(n_out // tn,),
        in_specs=[pl.BlockSpec((rows, D), lambda j: (0, 0)),
                  pl.BlockSpec((D, tn), lambda j: (0, j)),
                  pl.BlockSpec((1, tn), lambda j: (0, j))],
        out_specs=pl.BlockSpec((rows, tn), lambda j: (0, j)),
        compiler_params=_params(1),
        name="ada",
    )(cp, w_ada, b_ada.reshape(1, n_out))
    return out[:B].reshape(B, 6, D)


def _inproj_kernel(x_ref, xp_ref, xn_ref, mod_ref, w_ref, cw_ref, cb_ref,
                   q_ref, kv_ref, v_ref, x1_ref, x2_ref, ga_ref, gh_ref):
    i = pl.program_id(1)
    n = pl.num_programs(1)
    C = HYENA_WIDTH
    x = x_ref[0]
    tm, D = x.shape
    shift = mod_ref[0, 0:1, :]
    scale = mod_ref[0, 1:2, :]
    h = (x * (1.0 + scale) + shift).astype(BF16)

    def seg(lo, hi):
        return jnp.dot(h, w_ref[:, lo:hi], preferred_element_type=F32)

    o_q = 0
    o_kv = o_q + ATTN_WIDTH
    o_hy = o_kv + 2 * KV_WIDTH
    o_ga = o_hy + 3 * C
    o_gh = o_ga + D
    ga_ref[0] = _sigmoid(seg(o_ga, o_ga + D)).astype(BF16)
    gh_ref[0] = _sigmoid(seg(o_gh, o_gh + D)).astype(BF16)

    u = seg(o_hy, o_hy + 3 * C)
    xe = jnp.concatenate([xp_ref[0], xn_ref[0]], axis=0)
    he = (xe * (1.0 + scale) + shift).astype(BF16)
    ue = jnp.dot(he, w_ref[:, o_hy:o_hy + 3 * C], preferred_element_type=F32)
    prow = jnp.where(i > 0, ue[SUBLANES - 1:SUBLANES], 0.0)
    nrow = jnp.where(i < n - 1, ue[SUBLANES:SUBLANES + 1], 0.0)
    rid = lax.broadcasted_iota(jnp.int32, (tm, 1), 0)
    up = jnp.where(rid == 0, prow, pltpu.roll(u, 1, 0))
    dn = jnp.where(rid == tm - 1, nrow, pltpu.roll(u, tm - 1, 0))
    conv = cw_ref[0:1, :] * up + cw_ref[1:2, :] * u + cw_ref[2:3, :] * dn + cb_ref[...]
    v_ref[0] = conv[:, :C]
    x1_ref[0] = conv[:, C:2 * C]
    x2_ref[0] = conv[:, 2 * C:]

    q_ref[0] = (seg(o_q, o_q + ATTN_WIDTH) * (HEAD_DIM ** -0.5)).astype(BF16)
    kv_ref[0] = seg(o_kv, o_kv + 2 * KV_WIDTH).astype(BF16)


def _in_proj(x, mod, w_in, conv_w, conv_b):
    B, S, D = x.shape
    C = HYENA_WIDTH
    tm = min(512, S)
    r8 = tm // SUBLANES
    nb8 = S // SUBLANES
    wb = w_in.astype(BF16)
    nw = wb.shape[1]
    row = lambda b, i: (b, i, 0)
    shapes = [(ATTN_WIDTH, BF16), (2 * KV_WIDTH, BF16), (C, F32), (C, F32), (C, F32), (D, BF16), (D, BF16)]
    return pl.pallas_call(
        _inproj_kernel,
        out_shape=[jax.ShapeDtypeStruct((B, S, w), dt) for w, dt in shapes],
        grid=(B, S // tm),
        in_specs=[pl.BlockSpec((1, tm, D), row),
                  pl.BlockSpec((1, SUBLANES, D), lambda b, i: (b, jnp.maximum(i * r8 - 1, 0), 0)),
                  pl.BlockSpec((1, SUBLANES, D), lambda b, i: (b, jnp.minimum((i + 1) * r8, nb8 - 1), 0)),
                  pl.BlockSpec((1, 6, D), lambda b, i: (b, 0, 0)),
                  pl.BlockSpec((D, nw), lambda b, i: (0, 0)),
                  pl.BlockSpec((3, 3 * C), lambda b, i: (0, 0)),
                  pl.BlockSpec((1, 3 * C), lambda b, i: (0, 0))],
        out_specs=[pl.BlockSpec((1, tm, w), row) for w, _ in shapes],
        compiler_params=_params(2),
        name="in_proj",
    )(x, x, x, mod, wb, conv_w.astype(F32), conv_b.reshape(1, 3 * C).astype(F32))


ATT_TQ = 512
ATT_QB = 128
ATT_STACK = 4


def _attn_kernel(sink_ref, q_ref, kvp_ref, kvc_ref, kvn_ref, bias_ref, o_ref, kv_scr, vx_scr, *, seq_len):
    i = pl.program_id(1)
    H = WINDOW
    TQ = q_ref.shape[1]
    Q = min(ATT_QB, TQ)
    band = Q + 2 * H
    G = N_HEADS // N_KV_HEADS
    kv_scr[0:H] = kvp_ref[0]
    kv_scr[H:H + TQ] = kvc_ref[0]
    kv_scr[H + TQ:] = kvn_ref[0]
    for kv in range(N_KV_HEADS):
        vx_scr[:, kv * LANES:kv * LANES + HEAD_DIM] = kv_scr[:, KV_WIDTH + kv * HEAD_DIM:KV_WIDTH + (kv + 1) * HEAD_DIM]
        vx_scr[:, kv * LANES + HEAD_DIM:(kv + 1) * LANES] = jnp.ones((TQ + 2 * H, LANES - HEAD_DIM), BF16)
    col = lax.broadcasted_iota(jnp.int32, (1, band), 1)
    rhead = lax.broadcasted_iota(jnp.int32, (ATT_STACK * Q, 1), 0) // Q
    for j in range(TQ // Q):
        kpos = i * TQ + j * Q - H + col
        colbias = jnp.where((kpos >= 0) & (kpos < seq_len), 0.0, NEG)
        for kv in range(N_KV_HEADS):
            kk = kv_scr[j * Q:j * Q + band, kv * HEAD_DIM:(kv + 1) * HEAD_DIM]
            vx = vx_scr[j * Q:j * Q + band, kv * LANES:(kv + 1) * LANES]
            for sub in range(G // ATT_STACK):
                first = sub * ATT_STACK
                heads = [kv * G + first + g for g in range(ATT_STACK)]
                qg = jnp.concatenate([q_ref[0, j * Q:(j + 1) * Q, h * HEAD_DIM:(h + 1) * HEAD_DIM] for h in heads],
                                     axis=0)
                s = lax.dot_general(qg, kk, (((1,), (1,)), ((), ())), preferred_element_type=F32)
                s = s + bias_ref[kv, first * Q:(first + ATT_STACK) * Q, :] + colbias
                snk = sink_ref[heads[-1]]
                for g in range(ATT_STACK - 2, -1, -1):
                    snk = jnp.where(rhead == g, sink_ref[heads[g]], snk)
                m = jnp.maximum(jnp.max(s, axis=1, keepdims=True), snk)
                p = jnp.exp(s - m).astype(BF16)
                ox = jnp.dot(p, vx, preferred_element_type=F32)
                den = ox[:, HEAD_DIM:HEAD_DIM + 1] + jnp.exp(snk - m)
                o = ox[:, :HEAD_DIM] / den
                for g, h in enumerate(heads):
                    o_ref[0, j * Q:(j + 1) * Q, h * HEAD_DIM:(h + 1) * HEAD_DIM] = o[g * Q:(g + 1) * Q].astype(BF16)


def _attention(q, kv, sink):
    B, S, _ = q.shape
    H = WINDOW
    TQ = min(ATT_TQ, S)
    Q = min(ATT_QB, TQ)
    r = TQ // H
    nq = S // H
    G = N_HEADS // N_KV_HEADS
    assert G % ATT_STACK == 0
    a = jnp.arange(Q)[:, None]
    j = jnp.arange(Q + 2 * H)[None, :]
    rel = jnp.abs(j - H - a).astype(F32)
    slopes = 2.0 ** (-8.0 * jnp.arange(1, N_HEADS + 1, dtype=F32) / N_HEADS)
    bias = jnp.where(rel[None] <= WINDOW, -slopes[:, None, None] * rel[None], NEG).astype(F32)
    bias = bias.reshape(N_KV_HEADS, G * Q, Q + 2 * H)
    cur = lambda b, i: (b, i, 0)
    return pl.pallas_call(
        functools.partial(_attn_kernel, seq_len=S),
        out_shape=jax.ShapeDtypeStruct((B, S, ATTN_WIDTH), BF16),
        grid=(B, S // TQ),
        in_specs=[pl.BlockSpec(memory_space=pltpu.SMEM),
                  pl.BlockSpec((1, TQ, ATTN_WIDTH), cur),
                  pl.BlockSpec((1, H, 2 * KV_WIDTH), lambda b, i: (b, jnp.maximum(i * r - 1, 0), 0)),
                  pl.BlockSpec((1, TQ, 2 * KV_WIDTH), cur),
                  pl.BlockSpec((1, H, 2 * KV_WIDTH), lambda b, i: (b, jnp.minimum((i + 1) * r, nq - 1), 0)),
                  pl.BlockSpec((N_KV_HEADS, G * Q, Q + 2 * H), lambda b, i: (0, 0, 0))],
        out_specs=pl.BlockSpec((1, TQ, ATTN_WIDTH), cur),
        scratch_shapes=[pltpu.VMEM((TQ + 2 * H, 2 * KV_WIDTH), BF16),
                        pltpu.VMEM((TQ + 2 * H, N_KV_HEADS * LANES), BF16)],
        compiler_params=_params(2),
        name="attn",
    )(sink.astype(F32), q, kv, kv, kv, bias)


def _filter_kernel(z_ref, w1h, w1l, b1_ref, f1_ref, w2h, w2l, b2_ref, f2_ref, w3h, w3l, dec_ref,
                   k_ref, s_ref):
    i = pl.program_id(0)
    z = z_ref[...]
    h1 = jnp.sin(f1_ref[...] * (_dot3(z, w1h[...], w1l[...]) + b1_ref[...]))
    h2 = jnp.sin(f2_ref[...] * (_dot3(h1, w2h[...], w2l[...]) + b2_ref[...]))
    k = _dot3(h2, w3h[...], w3l[...])
    t = z[:, 0:1]
    k = k * (jnp.exp(-t * jnp.abs(dec_ref[...])) + WINDOW_SHIFT)
    k_ref[...] = k

    @pl.when(i == 0)
    def _():
        s_ref[...] = jnp.zeros_like(s_ref)

    s_ref[...] += jnp.sum(jnp.abs(k), axis=0, keepdims=True)


def _filter_embedding(L):
    t = np.linspace(0.0, 1.0, L, dtype=np.float32).astype(np.float64)[:, None]
    w = (2.0 * math.pi * np.arange(L, dtype=np.float32) / np.float32(L)).astype(np.float64)[:, None]
    bands = np.linspace(1e-4, FILTER_BANDS - 1, FILTER_BANDS, dtype=np.float32).astype(np.float64)[None, :]
    bw = (bands.astype(np.float32) * w.astype(np.float32)).astype(np.float64)
    z = np.concatenate([t, np.cos(bw), -np.sin(bw)], axis=-1)
    zp = np.zeros((L, LANES), np.float32)
    zp[:, :FILTER_EMB] = z.astype(np.float32)
    return jnp.asarray(zp)


def _pad2(a, r, c):
    return jnp.zeros((r, c), F32).at[:a.shape[0], :a.shape[1]].set(a.astype(F32))


def _filters(L, fw1, fb1, ff1, fw2, fb2, ff2, fw3, decay):
    H = LANES
    nf = fw3.shape[1]
    z = _filter_embedding(L)
    w1h, w1l = _split(_pad2(fw1, H, H))
    w2h, w2l = _split(_pad2(fw2, H, H))
    w3h, w3l = _split(_pad2(fw3, H, nf))
    b1 = _pad2(fb1[None], 1, H)
    f1 = _pad2(ff1[None], 1, H)
    b2 = _pad2(fb2[None], 1, H)
    f2 = _pad2(ff2[None], 1, H)
    tr = min(512, L)
    full = lambda r, c: pl.BlockSpec((r, c), lambda i: (0, 0))
    return pl.pallas_call(
        _filter_kernel,
        out_shape=[jax.ShapeDtypeStruct((L, nf), F32), jax.ShapeDtypeStruct((1, nf), F32)],
        grid=(L // tr,),
        in_specs=[pl.BlockSpec((tr, H), lambda i: (i, 0)),
                  full(H, H), full(H, H), full(1, H), full(1, H),
                  full(H, H), full(H, H), full(1, H), full(1, H),
                  full(H, nf), full(H, nf), full(1, nf)],
        out_specs=[pl.BlockSpec((tr, nf), lambda i: (i, 0)), full(1, nf)],
        compiler_params=_params(1),
        name="filter",
    )(z, w1h, w1l, b1, f1, w2h, w2l, b2, f2, w3h, w3l, decay.reshape(1, nf).astype(F32))


def _np_bf16(m64):
    return jnp.asarray(m64.astype(np.float32).astype(BF16))


def _dft_constants(L):
    N = 2 * L
    n2 = LANES
    n1 = N // n2
    h1 = n1 // 2
    k1 = np.arange(n1)[:, None]
    s1 = np.arange(h1)[None, :]
    ang = -2.0 * np.pi * ((k1 * s1) % n1) / n1
    wr, wi = np.cos(ang), np.sin(ang)
    w1_filt = np.block([[wr, wr], [wi, wi], [wr, -wr], [wi, -wi]])
    w1_cplx = np.block([[wr, -wi], [wi, wr]])
    vr, vi = wr.T / N, -wi.T / N
    w3 = np.block([[vr, -vi], [vi, vr]])
    k2 = np.arange(n2)[:, None]
    s2 = np.arange(n2)[None, :]
    a2 = -2.0 * np.pi * ((k2 * s2) % n2) / n2
    w2r, w2i = jnp.asarray(np.cos(a2), F32), jnp.asarray(np.sin(a2), F32)
    at = -2.0 * np.pi * ((np.arange(n1)[:, None] * s2) % N) / N
    twr, twi = jnp.asarray(np.cos(at), F32), jnp.asarray(np.sin(at), F32)
    mr = w2r[None] * twr[:, None, :] - w2i[None] * twi[:, None, :]
    mi = w2r[None] * twi[:, None, :] + w2i[None] * twr[:, None, :]
    fwd = jnp.concatenate([jnp.concatenate([mr, -mi], axis=2),
                           jnp.concatenate([mi, mr], axis=2)], axis=1)
    fwd = fwd.astype(BF16)
    return dict(n1=n1, w1_filt=_np_bf16(w1_filt), w1_cplx=_np_bf16(w1_cplx), w3=_np_bf16(w3),
                fwd=fwd)


SCH = 8


def _dft1_kernel(x_ref, w_ref, a_ref, *, n1):
    w = w_ref[...]
    for j in range(SCH):
        rhs = jnp.concatenate([x_ref[0, 0, :, j, :], x_ref[0, 1, :, j, :]], axis=0)
        res = jnp.dot(w, rhs.astype(BF16), preferred_element_type=F32)
        a_ref[0, :, j, :] = _pack2(res[:n1], res[n1:])


def _dft1_data(x, consts):
    B, L, C = x.shape
    n1 = consts["n1"]
    h1 = n1 // 2
    xv = x.reshape(B // 2, 2, h1, LANES, C)
    return pl.pallas_call(
        functools.partial(_dft1_kernel, n1=n1),
        out_shape=jax.ShapeDtypeStruct((B // 2, n1, LANES, C), F32),
        grid=(B // 2, LANES // SCH),
        in_specs=[pl.BlockSpec((1, 2, h1, SCH, C), lambda p, j: (p, 0, 0, j, 0)),
                  pl.BlockSpec((2 * n1, n1), lambda p, j: (0, 0))],
        out_specs=pl.BlockSpec((1, n1, SCH, C), lambda p, j: (p, 0, j, 0)),
        compiler_params=_params(2),
        name="dft1",
    )(xv, consts["w1_cplx"])


def _dft1f_kernel(x_ref, w_ref, a_ref, *, n1):
    C = HYENA_WIDTH
    w = w_ref[...]
    for j in range(SCH):
        rhs = jnp.concatenate([x_ref[:, j, :C], x_ref[:, j, C:]], axis=0)
        res = jnp.dot(w, rhs.astype(BF16), preferred_element_type=F32)
        a_ref[0, :, 0, j, :] = _pack2(res[:n1], res[n1:2 * n1])
        a_ref[0, :, 1, j, :] = _pack2(res[2 * n1:3 * n1], res[3 * n1:])


def _dft1_filter(kraw, consts):
    L, nf = kraw.shape
    C = HYENA_WIDTH
    n_ord = nf // (2 * C)
    n1 = consts["n1"]
    h1 = n1 // 2
    kv = kraw.reshape(h1, LANES, nf)
    return pl.pallas_call(
        functools.partial(_dft1f_kernel, n1=n1),
        out_shape=jax.ShapeDtypeStruct((n_ord, n1, 2, LANES, C), F32),
        grid=(n_ord, LANES // SCH),
        in_specs=[pl.BlockSpec((h1, SCH, 2 * C), lambda o, j: (0, j, o)),
                  pl.BlockSpec((4 * n1, n1), lambda o, j: (0, 0))],
        out_specs=pl.BlockSpec((1, n1, 2, SCH, C), lambda o, j: (o, 0, 0, j, 0)),
        compiler_params=_params(2),
        name="dft1f",
    )(kv, consts["w1_filt"])


KCH = 8


def _midf_kernel(a_ref, f_ref, inv_ref, b0_ref, h_ref):
    n2 = LANES
    sc = inv_ref[0]
    for k in range(KCH):
        p = jnp.concatenate(_unpack2(a_ref[0, k, :n2, :]), axis=0).astype(BF16)
        q = jnp.concatenate(_unpack2(a_ref[0, k, n2:, :]), axis=0).astype(BF16)
        h_re = jnp.dot(f_ref[k, :n2, :], p, preferred_element_type=F32)
        h_im = jnp.dot(f_ref[k, n2:, :], q, preferred_element_type=F32)
        h_ref[0, k] = _pack2((h_re - b0_ref[0]) * sc, h_im * sc)


def _filter_spectrum(af, inv_den, bwd0, consts):
    n_ord, n1, _, n2, C = af.shape
    a = af.reshape(n_ord, n1, 2 * n2, C)
    tab = pl.BlockSpec((KCH, 2 * n2, 2 * n2), lambda k, o: (k, 0, 0))
    vec = pl.BlockSpec((1, 1, C), lambda k, o: (o, 0, 0))
    return pl.pallas_call(
        _midf_kernel,
        out_shape=jax.ShapeDtypeStruct((n_ord, n1, n2, C), F32),
        grid=(n1 // KCH, n_ord),
        in_specs=[pl.BlockSpec((1, KCH, 2 * n2, C), lambda k, o: (o, k, 0, 0)), tab, vec, vec],
        out_specs=pl.BlockSpec((1, KCH, n2, C), lambda k, o: (o, k, 0, 0)),
        compiler_params=_params(2),
        name="midf",
    )(a, consts["fwd"], inv_den, bwd0)


def _mid_kernel(a_ref, f_ref, h_ref, b_ref):
    n2 = LANES
    for k in range(KCH):
        a = jnp.concatenate(_unpack2(a_ref[0, k]), axis=0).astype(BF16)
        x = jnp.dot(f_ref[k], a, preferred_element_type=F32)
        xr, xi = x[:n2], x[n2:]
        hr, hi = _unpack2(h_ref[0, k])
        y = jnp.concatenate([xr * hr - xi * hi, xr * hi + xi * hr], axis=0)
        b = lax.dot_general(f_ref[k], y.astype(BF16), (((0,), (0,)), ((), ())), preferred_element_type=F32)
        b_ref[0, k] = _pack2(b[:n2], b[n2:])


def _mid(a, hspec, order, consts):
    P, n1, n2, C = a.shape
    tab = pl.BlockSpec((KCH, 2 * n2, 2 * n2), lambda k, p: (k, 0, 0))
    return pl.pallas_call(
        _mid_kernel,
        out_shape=jax.ShapeDtypeStruct((P, n1, n2, C), F32),
        grid=(n1 // KCH, P),
        in_specs=[pl.BlockSpec((1, KCH, n2, C), lambda k, p: (p, k, 0, 0)),
                  tab,
                  pl.BlockSpec((1, KCH, n2, C), lambda k, p: (order, k, 0, 0))],
        out_specs=pl.BlockSpec((1, KCH, n2, C), lambda k, p: (p, k, 0, 0)),
        compiler_params=_params(2),
        name="mid",
    )(a, consts["fwd"], hspec)


def _dft3_kernel(b_ref, w_ref, v_ref, g_ref, skip_ref, *rest, h1, chain):
    if chain:
        w1_ref, z_ref, a_ref, slab_ref = rest
        w1 = w1_ref[...]
    else:
        z_ref, slab_ref = rest
    w = w_ref[...]
    skip = skip_ref[0]
    n1 = 2 * h1
    for j in range(SCH):
        slab_ref[...] = b_ref[0, :, j, :]
        rhs = jnp.concatenate(_unpack2(slab_ref[...]), axis=0)
        y = jnp.dot(w, rhs.astype(BF16), preferred_element_type=F32)
        z = [g_ref[0, r, :, j, :] * (y[r * h1:(r + 1) * h1] + v_ref[0, r, :, j, :] * skip) for r in range(2)]
        for r in range(2):
            z_ref[0, r, :, j, :] = z[r]
        if chain:
            res = jnp.dot(w1, jnp.concatenate(z, axis=0).astype(BF16), preferred_element_type=F32)
            a_ref[0, :, j, :] = _pack2(res[:n1], res[n1:])


def _dft3_gate(b5, v, gate, skip, consts, chain):
    P, n1, n2, C = b5.shape
    h1 = n1 // 2
    B, L, _ = v.shape
    five = lambda t: t.reshape(P, 2, h1, n2, C)
    dat = pl.BlockSpec((1, 2, h1, SCH, C), lambda p, j: (p, 0, 0, j, 0))
    packed = pl.BlockSpec((1, n1, SCH, C), lambda p, j: (p, 0, j, 0))
    in_specs = [packed, pl.BlockSpec((n1, 2 * n1), lambda p, j: (0, 0)), dat, dat,
                pl.BlockSpec((1, C), lambda p, j: (0, 0))]
    args = [b5, consts["w3"], five(v), five(gate), skip.reshape(1, C).astype(F32)]
    out_shape = [jax.ShapeDtypeStruct((P, 2, h1, n2, C), F32)]
    out_specs = [dat]
    if chain:
        in_specs.append(pl.BlockSpec((2 * n1, n1), lambda p, j: (0, 0)))
        args.append(consts["w1_cplx"])
        out_shape.append(jax.ShapeDtypeStruct((P, n1, n2, C), F32))
        out_specs.append(packed)
    outs = pl.pallas_call(
        functools.partial(_dft3_kernel, h1=h1, chain=chain),
        out_shape=out_shape,
        grid=(P, n2 // SCH),
        in_specs=in_specs,
        out_specs=out_specs,
        scratch_shapes=[pltpu.VMEM((n1, C), F32)],
        compiler_params=_params(2),
        name="dft3",
    )(*args)
    z = outs[0].reshape(B, L, C)
    return (z, outs[1]) if chain else (z, None)


def _hyena(v, x1, x2, fw1, fb1, ff1, fw2, fb2, ff2, fw3, decay, skip):
    B, L, C = v.shape
    consts = _dft_constants(L)
    kraw, ksum = _filters(L, fw1, fb1, ff1, fw2, fb2, ff2, fw3, decay)
    ks = ksum.reshape(2, 2, C)
    inv_den = (1.0 / (ks[:, 0] + ks[:, 1])).reshape(2, 1, C)
    bwd0 = kraw[0].reshape(2, 2, C)[:, 1].reshape(2, 1, C)
    hspec = _filter_spectrum(_dft1_filter(kraw, consts), inv_den, bwd0, consts)
    gates = (x1, x2)
    z, a5 = v, _dft1_data(v, consts)
    for o, gate in enumerate(gates):
        b5 = _mid(a5, hspec, o, consts)
        z, a5 = _dft3_gate(b5, z, gate, skip[o], consts, chain=o + 1 < len(gates))
    return z


def _merge_kernel(attn_ref, hy_ref, ga_ref, gh_ref, x_ref, mod_ref, wa_ref, wh_ref, wo_ref,
                  g1_ref, b1_ref, rwh_ref, rwl_ref, rb_ref, tri_ref,
                  x1_ref, h2_ref, route_ref, wts_ref, cnt_ref, carry_ref):
    @pl.when((pl.program_id(0) == 0) & (pl.program_id(1) == 0))
    def _():
        carry_ref[...] = jnp.zeros_like(carry_ref)

    logits = _merge_dense(attn_ref, hy_ref, ga_ref, gh_ref, x_ref, mod_ref, wa_ref, wh_ref, wo_ref,
                          g1_ref, b1_ref, rwh_ref, rwl_ref, rb_ref, x1_ref, h2_ref)
    _route_rows(logits, tri_ref, route_ref, wts_ref, carry_ref)
    cnt_ref[...] = carry_ref[...]


def _merge_dense(attn_ref, hy_ref, ga_ref, gh_ref, x_ref, mod_ref, wa_ref, wh_ref, wo_ref,
                 g1_ref, b1_ref, rwh_ref, rwl_ref, rb_ref, x1_ref, h2_ref):
    a = jnp.dot(attn_ref[0], wa_ref[...], preferred_element_type=F32)
    hy = jnp.dot(hy_ref[0].astype(BF16), wh_ref[...], preferred_element_type=F32)
    merged = ga_ref[0].astype(F32) * a + gh_ref[0].astype(F32) * hy
    y = jnp.dot(merged.astype(BF16), wo_ref[...], preferred_element_type=F32)
    gate1 = mod_ref[0, 2:3, :]
    shift2 = mod_ref[0, 3:4, :]
    scale2 = mod_ref[0, 4:5, :]
    x1 = _layer_norm(DN_ALPHA * x_ref[0] + gate1 * y, g1_ref[...], b1_ref[...])
    x1_ref[0] = x1
    h2 = x1 * (1.0 + scale2) + shift2
    half = h2.shape[1] // 2
    h2_ref[0] = _pack2(h2[:, :half], h2[:, half:])
    return _dot3(h2, rwh_ref[...], rwl_ref[...]) + rb_ref[...]


def _route_rows(logits, tri_ref, route_ref, wts_ref, carry_ref):
    tm = logits.shape[0]
    lane = lax.broadcasted_iota(jnp.int32, (tm, LANES), 1)
    lanef = lane.astype(F32)
    big = float(LANES)

    def first_lane(mask):
        return jnp.min(jnp.where(mask, lanef, big), axis=1, keepdims=True).astype(jnp.int32)

    gmask = lane < N_GROUPS
    gl = jnp.where(gmask, logits, NEG)
    gmax = jnp.max(gl, axis=1, keepdims=True)
    gidx = first_lane(gl == gmax)
    pg = 1.0 / jnp.sum(jnp.exp(gl - gmax), axis=1, keepdims=True)
    lo = ROUTE_OFF + gidx * EXPERTS_PER_GROUP
    emask = (lane >= lo) & (lane < lo + EXPERTS_PER_GROUP)
    el = jnp.where(emask, logits, NEG)
    v1 = jnp.max(el, axis=1, keepdims=True)
    i1 = first_lane(el == v1)
    el2 = jnp.where(emask & (lane != i1), logits, NEG)
    v2 = jnp.max(el2, axis=1, keepdims=True)
    i2 = first_lane(el2 == v2)
    e21 = jnp.exp(v2 - v1)
    w1 = pg / (1.0 + e21)
    w2 = pg * e21 / (1.0 + e21)

    sel1 = lane == i1
    sel2 = lane == i2
    onehot = jnp.where(sel1 | sel2, 1.0, 0.0)
    prefix = jnp.dot(tri_ref[...], onehot.astype(BF16), preferred_element_type=F32) + carry_ref[...]
    r1 = jnp.sum(jnp.where(sel1, prefix, 0.0), axis=1, keepdims=True)
    r2 = jnp.sum(jnp.where(sel2, prefix, 0.0), axis=1, keepdims=True)
    carry_ref[...] += jnp.sum(onehot, axis=0, keepdims=True)

    e1 = (i1 - ROUTE_OFF).astype(F32)
    e2 = (i2 - ROUTE_OFF).astype(F32)
    table = jnp.where(lane == 0, e1, jnp.where(lane == 1, e2, jnp.where(lane == 2, r1, jnp.where(lane == 3, r2, 0.0))))
    route_ref[...] = table.T[:SUBLANES].astype(jnp.int32)
    wts_ref[0] = jnp.where(lane == 0, w1, jnp.where(lane == 1, w2, 0.0))


def _merge(attn, hy, ga, gh, x, mod, w_attn_o, w_hy_o, w_out, ln1_g, ln1_b, rg_w, rg_b, re_w, re_b):
    B, S, D = x.shape
    tm = min(512, S)
    spare = LANES - N_GROUPS - N_EXPERTS
    rw = jnp.concatenate([rg_w, re_w, jnp.zeros((D, spare), F32)], axis=1)
    rb = jnp.concatenate([rg_b, re_b, jnp.zeros((spare,), F32)]).reshape(1, LANES)
    rwh, rwl = _split(rw)
    tri = (jnp.arange(tm)[:, None] > jnp.arange(tm)[None, :]).astype(BF16)
    row = lambda b, i: (b, i, 0)
    full = lambda r, c: pl.BlockSpec((r, c), lambda b, i: (0, 0))
    per_b = S // tm
    outs = [jax.ShapeDtypeStruct((B, S, D), F32), jax.ShapeDtypeStruct((B, S, D // 2), F32),
            jax.ShapeDtypeStruct((SUBLANES, B * S), jnp.int32), jax.ShapeDtypeStruct((B, S, LANES), F32),
            jax.ShapeDtypeStruct((1, LANES), F32)]
    return pl.pallas_call(
        _merge_kernel,
        out_shape=outs,
        grid=(B, per_b),
        in_specs=[pl.BlockSpec((1, tm, ATTN_WIDTH), row), pl.BlockSpec((1, tm, HYENA_WIDTH), row),
                  pl.BlockSpec((1, tm, D), row), pl.BlockSpec((1, tm, D), row), pl.BlockSpec((1, tm, D), row),
                  pl.BlockSpec((1, 6, D), lambda b, i: (b, 0, 0)),
                  full(ATTN_WIDTH, D), full(HYENA_WIDTH, D), full(D, D),
                  full(1, D), full(1, D), full(D, LANES), full(D, LANES), full(1, LANES), full(tm, tm)],
        out_specs=[pl.BlockSpec((1, tm, D), row), pl.BlockSpec((1, tm, D // 2), row),
                   pl.BlockSpec((SUBLANES, tm), lambda b, i: (0, b * per_b + i)),
                   pl.BlockSpec((1, tm, LANES), row), full(1, LANES)],
        scratch_shapes=[pltpu.VMEM((1, LANES), F32)],
        compiler_params=_params(2),
        name="merge",
    )(attn, hy, ga, gh, x, mod, w_attn_o.astype(BF16), w_hy_o.astype(BF16), w_out.astype(BF16),
      ln1_g.reshape(1, D), ln1_b.reshape(1, D), rwh, rwl, rb, tri)


SC_ROWS = 64


def _sc_workers():
    info = plsc.get_sparse_core_info()
    return info.num_cores, info.num_cores * info.num_subcores


def _sc_split(n):
    _, workers = _sc_workers()
    per_worker = n // workers
    chunks = per_worker // SC_ROWS
    assert per_worker * workers == n and chunks * SC_ROWS == per_worker and chunks % 2 == 0
    return workers, per_worker, chunks


def _sc_scatter_rows(src, idx0, idx1, n_out):
    n, width = src.shape
    nc, _ = _sc_workers()
    workers, per_worker, chunks = _sc_split(n)
    mesh = plsc.VectorSubcoreMesh(core_axis_name="c", subcore_axis_name="s")

    def body(src_hbm, i0_hbm, i1_hbm, out_hbm, i0_v, i1_v, rows_v, sem, ssem):
        wid = lax.axis_index("s") * nc + lax.axis_index("c")
        base = wid * per_worker
        pltpu.sync_copy(i0_hbm.at[wid], i0_v)
        pltpu.sync_copy(i1_hbm.at[wid], i1_v)

        def load(chunk, buf):
            return pltpu.make_async_copy(src_hbm.at[pl.ds(base + chunk * SC_ROWS, SC_ROWS)], rows_v.at[buf], sem)

        load(0, 0).start()

        @pl.loop(0, chunks, step=2)
        def _(c):
            for b in range(2):
                chunk = c + b
                load(chunk, b).wait()

                @pl.when(chunk + 1 < chunks)
                def _():
                    load(chunk + 1, 1 - b).start()

                first = pltpu.make_async_copy(rows_v.at[b], out_hbm.at[i0_v.at[chunk]], ssem)
                second = pltpu.make_async_copy(rows_v.at[b], out_hbm.at[i1_v.at[chunk]], ssem)
                first.start()
                second.start()
                first.wait()
                second.wait()

    shaped = lambda i: i.reshape(workers, chunks, SC_ROWS)
    return pl.kernel(
        body,
        out_type=jax.ShapeDtypeStruct((n_out, width), src.dtype),
        mesh=mesh,
        scratch_types=[pltpu.VMEM((chunks, SC_ROWS), jnp.int32),
                       pltpu.VMEM((chunks, SC_ROWS), jnp.int32),
                       pltpu.VMEM((2, SC_ROWS, width), src.dtype),
                       pltpu.SemaphoreType.DMA, pltpu.SemaphoreType.DMA],
        name="sc_scatter",
    )(src, shaped(idx0), shaped(idx1))


def _expert_kernel(first_ref, nb_ref, sz_ref, tot_ref, w1_ref, w3_ref, w2_ref, xb_ref, yb_ref,
                   xbuf, ybuf, c1_ref, c3_ref, c2_ref, lsem, ssem):
    e = pl.program_id(0)
    nb = nb_ref[e]
    first = first_ref[e]
    total = tot_ref[0]
    rows = xbuf.shape[1]

    def load(g, slot):
        src = xb_ref.at[pl.ds(pl.multiple_of(g * rows, rows), rows)]
        return pltpu.make_async_copy(src, xbuf.at[slot], lsem.at[slot])

    def store(g, slot):
        dst = yb_ref.at[pl.ds(pl.multiple_of(g * rows, rows), rows)]
        return pltpu.make_async_copy(ybuf.at[slot], dst, ssem.at[slot])

    @pl.when((e == 0) & (total > 0))
    def _():
        load(0, 0).start()

    @pl.when(nb > 0)
    def _():
        c1_ref[...] = w1_ref[0].astype(BF16)
        c3_ref[...] = w3_ref[0].astype(BF16)
        c2_ref[...] = w2_ref[0].astype(BF16)

        def block(j, carry):
            g = first + j
            slot = lax.rem(g, 2)
            load(g, slot).wait()

            @pl.when(g + 1 < total)
            def _():
                load(g + 1, 1 - slot).start()

            @pl.when(g >= 2)
            def _():
                store(g - 2, slot).wait()

            n_valid = sz_ref[e] - j * rows
            rid = lax.broadcasted_iota(jnp.int32, (rows, 1), 0)
            xa, xb = _unpack2(jnp.where(rid < n_valid, xbuf[slot], 0.0))
            x = jnp.concatenate([xa, xb], axis=1).astype(BF16)
            a = jnp.dot(x, c1_ref[...], preferred_element_type=F32)
            gate = jnp.dot(x, c3_ref[...], preferred_element_type=F32)
            hmid = (a * _sigmoid(a) * gate).astype(BF16)
            y = jnp.dot(hmid, c2_ref[...], preferred_element_type=F32)
            half = y.shape[1] // 2
            ybuf[slot] = _pack2(y[:, :half], y[:, half:])
            store(g, slot).start()
            return carry

        lax.fori_loop(0, nb, block, 0)

    @pl.when(e == pl.num_programs(0) - 1)
    def _():
        for back in (2, 1):
            @pl.when(total >= back)
            def _():
                g = total - back
                store(g, lax.rem(g, 2)).wait()


def _experts(xb, first_blk, n_blk, sizes, w1, w3, w2):
    P, W = xb.shape
    E, D, DE = w1.shape
    total = jnp.sum(n_blk, keepdims=True)
    wspec = lambda r, c: pl.BlockSpec((1, r, c), lambda e, *_: (e, 0, 0))
    grid_spec = pltpu.PrefetchScalarGridSpec(
        num_scalar_prefetch=4,
        grid=(E,),
        in_specs=[wspec(D, DE), wspec(D, DE), wspec(DE, D), pl.BlockSpec(memory_space=pl.ANY)],
        out_specs=pl.BlockSpec(memory_space=pl.ANY),
        scratch_shapes=[pltpu.VMEM((2, MOE_BLOCK, W), F32), pltpu.VMEM((2, MOE_BLOCK, W), F32),
                        pltpu.VMEM((D, DE), BF16), pltpu.VMEM((D, DE), BF16), pltpu.VMEM((DE, D), BF16),
                        pltpu.SemaphoreType.DMA((2,)), pltpu.SemaphoreType.DMA((2,))],
    )
    return pl.pallas_call(
        _expert_kernel,
        out_shape=jax.ShapeDtypeStruct((P, W), F32),
        grid_spec=grid_spec,
        compiler_params=_params(1),
        name="experts",
    )(first_blk, n_blk, sizes, total, w1, w3, w2, xb)


def _sc_gather_rows(table, idx):
    n, width = idx.shape[0], table.shape[1]
    nc, _ = _sc_workers()
    workers, per_worker, chunks = _sc_split(n)
    mesh = plsc.VectorSubcoreMesh(core_axis_name="c", subcore_axis_name="s")

    def body(table_hbm, idx_hbm, out_hbm, idx_v, rows_v, sem):
        wid = lax.axis_index("s") * nc + lax.axis_index("c")
        base = wid * per_worker
        pltpu.sync_copy(idx_hbm.at[wid], idx_v)

        def gather(chunk, buf):
            return pltpu.make_async_copy(table_hbm.at[idx_v.at[chunk]], rows_v.at[buf], sem)

        gather(0, 0).start()

        @pl.loop(0, chunks, step=2)
        def _(c):
            for b in range(2):
                chunk = c + b
                gather(chunk, b).wait()

                @pl.when(chunk + 1 < chunks)
                def _():
                    gather(chunk + 1, 1 - b).start()

                pltpu.sync_copy(rows_v.at[b], out_hbm.at[pl.ds(base + chunk * SC_ROWS, SC_ROWS)])

    return pl.kernel(
        body,
        out_type=jax.ShapeDtypeStruct((n, width), table.dtype),
        mesh=mesh,
        scratch_types=[pltpu.VMEM((chunks, SC_ROWS), jnp.int32),
                       pltpu.VMEM((2, SC_ROWS, width), table.dtype),
                       pltpu.SemaphoreType.DMA],
        name="sc_gather",
    )(table, idx.reshape(workers, chunks, SC_ROWS))


def _combine_dense_kernel(r0_ref, r1_ref, wts_ref, x1_ref, mod_ref, g_ref, b_ref, *rest):
    o_ref = rest[-1]
    w = wts_ref[...]
    y0 = jnp.concatenate(_unpack2(r0_ref[0]), axis=1)
    y1 = jnp.concatenate(_unpack2(r1_ref[0]), axis=1)
    y = w[:, 0:1] * y0 + w[:, 1:2] * y1
    gate2 = mod_ref[0, 5:6, :]
    o_ref[...] = _layer_norm(DN_ALPHA * x1_ref[...] + gate2 * y, g_ref[...], b_ref[...])


def _combine_dense(rows, wts, x1, mod, ln2_g, ln2_b, S, b, out):
    T, D = x1.shape
    tm = min(512, S)
    per_b = S // tm
    here = lambda i: (b * per_b + i, 0)
    in_specs = [pl.BlockSpec((1, tm, rows.shape[2]), lambda i: (0, i, 0)),
                pl.BlockSpec((1, tm, rows.shape[2]), lambda i: (1, i, 0)),
                pl.BlockSpec((tm, LANES), here),
                pl.BlockSpec((tm, D), here),
                pl.BlockSpec((1, 6, D), lambda i: (b, 0, 0)),
                pl.BlockSpec((1, D), lambda i: (0, 0)),
                pl.BlockSpec((1, D), lambda i: (0, 0))]
    args = [rows, rows, wts, x1, mod, ln2_g.reshape(1, D), ln2_b.reshape(1, D)]
    aliases = {}
    if out is not None:
        in_specs.append(pl.BlockSpec(memory_space=pl.ANY))
        aliases = {len(args): 0}
        args.append(out)
    return pl.pallas_call(
        _combine_dense_kernel,
        out_shape=jax.ShapeDtypeStruct((T, D), F32),
        grid=(per_b,),
        in_specs=in_specs,
        out_specs=pl.BlockSpec((tm, D), here),
        input_output_aliases=aliases,
        compiler_params=_params(1),
        name="combine",
    )(*args)


def _moe(h2, x1, route, wts, counts, mod, w1, w3, w2, ln2_g, ln2_b):
    B, S, D = x1.shape
    T = B * S
    P = 2 * T + N_EXPERTS * MOE_BLOCK
    sizes = counts[0, ROUTE_OFF:ROUTE_OFF + N_EXPERTS].astype(jnp.int32)
    n_blk = (sizes + MOE_BLOCK - 1) // MOE_BLOCK
    psizes = n_blk * MOE_BLOCK
    poffs = jnp.cumsum(psizes) - psizes
    sel = route[0:2, None, :] == jnp.arange(N_EXPERTS, dtype=jnp.int32)[None, :, None]
    dest = route[2:4] + jnp.sum(jnp.where(sel, poffs[None, :, None], 0), axis=1)
    xb = _sc_scatter_rows(h2.reshape(T, D // 2), dest[0], dest[1], P)
    yb = _experts(xb, poffs // MOE_BLOCK, n_blk, sizes, w1, w3, w2)
    out = None
    for b in range(B):
        slot_major = dest[:, b * S:(b + 1) * S].reshape(2 * S)
        rows = _sc_gather_rows(yb, slot_major).reshape(2, S, yb.shape[1])
        out = _combine_dense(rows, wts.reshape(T, LANES), x1.reshape(T, D), mod, ln2_g, ln2_b, S, b, out)
    return out.reshape(B, S, D)


def _layer(x, c, w_ada, b_ada, w_in, conv_w, conv_b, fw1, fb1, ff1, fw2, fb2, ff2, fw3, decay, skip,
           w_hy_o, w_attn_o, attn_sink, w_out, ln1_g, ln1_b, rg_w, rg_b, re_w, re_b, ew1, ew3, ew2,
           ln2_g, ln2_b):
    mod = _ada(c, w_ada, b_ada)
    q, kv, hv, hx1, hx2, ga, gh = _in_proj(x, mod, w_in, conv_w, conv_b)
    attn = _attention(q, kv, attn_sink)
    hy = _hyena(hv, hx1, hx2, fw1, fb1, ff1, fw2, fb2, ff2, fw3, decay, skip)
    x1, h2, route, wts, counts = _merge(attn, hy, ga, gh, x, mod, w_attn_o, w_hy_o, w_out,
                                        ln1_g, ln1_b, rg_w, rg_b, re_w, re_b)
    return _moe(h2, x1, route, wts, counts, mod, ew1, ew3, ew2, ln2_g, ln2_b)


def kernel(x, c, w_ada, b_ada, w_in, conv_w, conv_b, filt_w1, filt_b1, filt_freq1, filt_w2, filt_b2, filt_freq2, filt_w3, filt_decay, hy_skip, w_hy_o, w_attn_o, attn_sink, w_out, ln1_g, ln1_b, router_group_w, router_group_b, router_expert_w, router_expert_b, exp_w1, exp_w3, exp_w2, ln2_g, ln2_b):
    for l in range(w_ada.shape[0]):
        x = _layer(x, c, w_ada[l], b_ada[l], w_in[l], conv_w[l], conv_b[l], filt_w1[l], filt_b1[l],
                   filt_freq1[l], filt_w2[l], filt_b2[l], filt_freq2[l], filt_w3[l], filt_decay[l],
                   hy_skip[l], w_hy_o[l], w_attn_o[l], attn_sink[l], w_out[l], ln1_g[l], ln1_b[l],
                   router_group_w[l], router_group_b[l], router_expert_w[l], router_expert_b[l],
                   exp_w1[l], exp_w3[l], exp_w2[l], ln2_g[l], ln2_b[l])
    return x
```

```python
import functools
import math

import numpy as np
import jax
import jax.numpy as jnp
from jax import lax
from jax.experimental import pallas as pl
from jax.experimental.pallas import tpu as pltpu
from jax.experimental.pallas import tpu_sc as plsc

F32 = jnp.float32
BF16 = jnp.bfloat16

N_HEADS = 8
N_KV_HEADS = 2
HEAD_DIM = 64
ATTN_WIDTH = N_HEADS * HEAD_DIM
KV_WIDTH = N_KV_HEADS * HEAD_DIM
WINDOW = 128
HYENA_WIDTH = 512
FILTER_EMB = 33
FILTER_BANDS = (FILTER_EMB - 1) // 2
WINDOW_SHIFT = 0.05
N_GROUPS = 8
EXPERTS_PER_GROUP = 8
N_EXPERTS = N_GROUPS * EXPERTS_PER_GROUP
MOE_BLOCK = 512
EXPERT_ROWS = 128
LN_EPS = 1e-5
DEPTH = 1
DN_ALPHA = (2.0 * DEPTH) ** 0.25
NEG = -1e30

LANES = 128
SUBLANES = 8
ROUTE_OFF = N_GROUPS
VMEM_LIMIT = 56 * 1024 * 1024


def _params(n_axes, vmem=VMEM_LIMIT):
    return pltpu.CompilerParams(dimension_semantics=("arbitrary",) * n_axes, vmem_limit_bytes=vmem)


def _split(a):
    hi = a.astype(BF16)
    lo = (a - hi.astype(F32)).astype(BF16)
    return hi, lo


def _dot3(a, b_hi, b_lo):
    a_hi, a_lo = _split(a)
    acc = jnp.dot(a_hi, b_hi, preferred_element_type=F32)
    acc = acc + jnp.dot(a_hi, b_lo, preferred_element_type=F32)
    acc = acc + jnp.dot(a_lo, b_hi, preferred_element_type=F32)
    return acc


def _pack2(a, b):
    ia = lax.bitcast_convert_type(a.astype(BF16).astype(F32), jnp.int32)
    ib = lax.bitcast_convert_type(b.astype(BF16).astype(F32), jnp.int32)
    return lax.bitcast_convert_type(ia | lax.shift_right_logical(ib, 16), F32)


def _unpack2(p):
    p = lax.bitcast_convert_type(p, jnp.int32)
    a = lax.bitcast_convert_type(p & jnp.int32(-65536), F32)
    b = lax.bitcast_convert_type(lax.shift_left(p, 16), F32)
    return a, b


def _sigmoid(x):
    return 0.5 * jnp.tanh(0.5 * x) + 0.5


def _layer_norm(r, g, b):
    mu = jnp.mean(r, axis=-1, keepdims=True)
    d = r - mu
    var = jnp.mean(d * d, axis=-1, keepdims=True)
    return d * lax.rsqrt(var + LN_EPS) * g + b


def _ada_kernel(c_ref, w_ref, b_ref, o_ref):
    c = c_ref[...]
    s = c * _sigmoid(c)
    wh, wl = _split(w_ref[...])
    o_ref[...] = _dot3(s, wh, wl) + b_ref[...]


def _ada(c, w_ada, b_ada):
    B, D = c.shape
    n_out = w_ada.shape[1]
    rows = SUBLANES
    cp = jnp.pad(c, ((0, rows - B), (0, 0)))
    tn = 1024
    out = pl.pallas_call(
        _ada_kernel,
        out_shape=jax.ShapeDtypeStruct((rows, n_out), F32),
        grid=(n_out // tn,),
        in_specs=[pl.BlockSpec((rows, D), lambda j: (0, 0)),
                  pl.BlockSpec((D, tn), lambda j: (0, j)),
                  pl.BlockSpec((1, tn), lambda j: (0, j))],
        out_specs=pl.BlockSpec((rows, tn), lambda j: (0, j)),
        compiler_params=_params(1),
        name="ada",
    )(cp, w_ada, b_ada.reshape(1, n_out))
    return out[:B].reshape(B, 6, D)


def _inproj_kernel(x_ref, xp_ref, xn_ref, mod_ref, w_ref, cw_ref, cb_ref,
                   q_ref, kv_ref, v_ref, x1_ref, x2_ref, ga_ref, gh_ref):
    i = pl.program_id(1)
    n = pl.num_programs(1)
    C = HYENA_WIDTH
    x = x_ref[0]
    tm, D = x.shape
    shift = mod_ref[0, 0:1, :]
    scale = mod_ref[0, 1:2, :]
    h = (x * (1.0 + scale) + shift).astype(BF16)

    def seg(lo, hi):
        return jnp.dot(h, w_ref[:, lo:hi], preferred_element_type=F32)

    o_q = 0
    o_kv = o_q + ATTN_WIDTH
    o_hy = o_kv + 2 * KV_WIDTH
    o_ga = o_hy + 3 * C
    o_gh = o_ga + D
    ga_ref[0] = _sigmoid(seg(o_ga, o_ga + D)).astype(BF16)
    gh_ref[0] = _sigmoid(seg(o_gh, o_gh + D)).astype(BF16)

    u = seg(o_hy, o_hy + 3 * C)
    xe = jnp.concatenate([xp_ref[0], xn_ref[0]], axis=0)
    he = (xe * (1.0 + scale) + shift).astype(BF16)
    ue = jnp.dot(he, w_ref[:, o_hy:o_hy + 3 * C], preferred_element_type=F32)
    prow = jnp.where(i > 0, ue[SUBLANES - 1:SUBLANES], 0.0)
    nrow = jnp.where(i < n - 1, ue[SUBLANES:SUBLANES + 1], 0.0)
    rid = lax.broadcasted_iota(jnp.int32, (tm, 1), 0)
    up = jnp.where(rid == 0, prow, pltpu.roll(u, 1, 0))
    dn = jnp.where(rid == tm - 1, nrow, pltpu.roll(u, tm - 1, 0))
    conv = cw_ref[0:1, :] * up + cw_ref[1:2, :] * u + cw_ref[2:3, :] * dn + cb_ref[...]
    v_ref[0] = conv[:, :C]
    x1_ref[0] = conv[:, C:2 * C]
    x2_ref[0] = conv[:, 2 * C:]

    q_ref[0] = (seg(o_q, o_q + ATTN_WIDTH) * (HEAD_DIM ** -0.5)).astype(BF16)
    kv_ref[0] = seg(o_kv, o_kv + 2 * KV_WIDTH).astype(BF16)


def _in_proj(x, mod, w_in, conv_w, conv_b):
    B, S, D = x.shape
    C = HYENA_WIDTH
    tm = min(512, S)
    r8 = tm // SUBLANES
    nb8 = S // SUBLANES
    wb = w_in.astype(BF16)
    nw = wb.shape[1]
    row = lambda b, i: (b, i, 0)
    shapes = [(ATTN_WIDTH, BF16), (2 * KV_WIDTH, BF16), (C, F32), (C, F32), (C, F32), (D, BF16), (D, BF16)]
    return pl.pallas_call(
        _inproj_kernel,
        out_shape=[jax.ShapeDtypeStruct((B, S, w), dt) for w, dt in shapes],
        grid=(B, S // tm),
        in_specs=[pl.BlockSpec((1, tm, D), row),
                  pl.BlockSpec((1, SUBLANES, D), lambda b, i: (b, jnp.maximum(i * r8 - 1, 0), 0)),
                  pl.BlockSpec((1, SUBLANES, D), lambda b, i: (b, jnp.minimum((i + 1) * r8, nb8 - 1), 0)),
                  pl.BlockSpec((1, 6, D), lambda b, i: (b, 0, 0)),
                  pl.BlockSpec((D, nw), lambda b, i: (0, 0)),
                  pl.BlockSpec((3, 3 * C), lambda b, i: (0, 0)),
                  pl.BlockSpec((1, 3 * C), lambda b, i: (0, 0))],
        out_specs=[pl.BlockSpec((1, tm, w), row) for w, _ in shapes],
        compiler_params=_params(2),
        name="in_proj",
    )(x, x, x, mod, wb, conv_w.astype(F32), conv_b.reshape(1, 3 * C).astype(F32))


ATT_TQ = 512
ATT_QB = 128
ATT_STACK = 4


def _attn_kernel(sink_ref, q_ref, kvp_ref, kvc_ref, kvn_ref, bias_ref, o_ref, kv_scr, vx_scr, *, seq_len):
    i = pl.program_id(1)
    H = WINDOW
    TQ = q_ref.shape[1]
    Q = min(ATT_QB, TQ)
    band = Q + 2 * H
    G = N_HEADS // N_KV_HEADS
    kv_scr[0:H] = kvp_ref[0]
    kv_scr[H:H + TQ] = kvc_ref[0]
    kv_scr[H + TQ:] = kvn_ref[0]
    for kv in range(N_KV_HEADS):
        vx_scr[:, kv * LANES:kv * LANES + HEAD_DIM] = kv_scr[:, KV_WIDTH + kv * HEAD_DIM:KV_WIDTH + (kv + 1) * HEAD_DIM]
        vx_scr[:, kv * LANES + HEAD_DIM:(kv + 1) * LANES] = jnp.ones((TQ + 2 * H, LANES - HEAD_DIM), BF16)
    col = lax.broadcasted_iota(jnp.int32, (1, band), 1)
    rhead = lax.broadcasted_iota(jnp.int32, (ATT_STACK * Q, 1), 0) // Q
    for j in range(TQ // Q):
        kpos = i * TQ + j * Q - H + col
        colbias = jnp.where((kpos >= 0) & (kpos < seq_len), 0.0, NEG)
        for kv in range(N_KV_HEADS):
            kk = kv_scr[j * Q:j * Q + band, kv * HEAD_DIM:(kv + 1) * HEAD_DIM]
            vx = vx_scr[j * Q:j * Q + band, kv * LANES:(kv + 1) * LANES]
            for sub in range(G // ATT_STACK):
                first = sub * ATT_STACK
                heads = [kv * G + first + g for g in range(ATT_STACK)]
                qg = jnp.concatenate([q_ref[0, j * Q:(j + 1) * Q, h * HEAD_DIM:(h + 1) * HEAD_DIM] for h in heads],
                                     axis=0)
                s = lax.dot_general(qg, kk, (((1,), (1,)), ((), ())), preferred_element_type=F32)
                s = s + bias_ref[kv, first * Q:(first + ATT_STACK) * Q, :] + colbias
                snk = sink_ref[heads[-1]]
                for g in range(ATT_STACK - 2, -1, -1):
                    snk = jnp.where(rhead == g, sink_ref[heads[g]], snk)
                m = jnp.maximum(jnp.max(s, axis=1, keepdims=True), snk)
                p = jnp.exp(s - m).astype(BF16)
                ox = jnp.dot(p, vx, preferred_element_type=F32)
                den = ox[:, HEAD_DIM:HEAD_DIM + 1] + jnp.exp(snk - m)
                o = ox[:, :HEAD_DIM] / den
                for g, h in enumerate(heads):
                    o_ref[0, j * Q:(j + 1) * Q, h * HEAD_DIM:(h + 1) * HEAD_DIM] = o[g * Q:(g + 1) * Q].astype(BF16)


def _attention(q, kv, sink):
    B, S, _ = q.shape
    H = WINDOW
    TQ = min(ATT_TQ, S)
    Q = min(ATT_QB, TQ)
    r = TQ // H
    nq = S // H
    G = N_HEADS // N_KV_HEADS
    assert G % ATT_STACK == 0
    a = jnp.arange(Q)[:, None]
    j = jnp.arange(Q + 2 * H)[None, :]
    rel = jnp.abs(j - H - a).astype(F32)
    slopes = 2.0 ** (-8.0 * jnp.arange(1, N_HEADS + 1, dtype=F32) / N_HEADS)
    bias = jnp.where(rel[None] <= WINDOW, -slopes[:, None, None] * rel[None], NEG).astype(F32)
    bias = bias.reshape(N_KV_HEADS, G * Q, Q + 2 * H)
    cur = lambda b, i: (b, i, 0)
    return pl.pallas_call(
        functools.partial(_attn_kernel, seq_len=S),
        out_shape=jax.ShapeDtypeStruct((B, S, ATTN_WIDTH), BF16),
        grid=(B, S // TQ),
        in_specs=[pl.BlockSpec(memory_space=pltpu.SMEM),
                  pl.BlockSpec((1, TQ, ATTN_WIDTH), cur),
                  pl.BlockSpec((1, H, 2 * KV_WIDTH), lambda b, i: (b, jnp.maximum(i * r - 1, 0), 0)),
                  pl.BlockSpec((1, TQ, 2 * KV_WIDTH), cur),
                  pl.BlockSpec((1, H, 2 * KV_WIDTH), lambda b, i: (b, jnp.minimum((i + 1) * r, nq - 1), 0)),
                  pl.BlockSpec((N_KV_HEADS, G * Q, Q + 2 * H), lambda b, i: (0, 0, 0))],
        out_specs=pl.BlockSpec((1, TQ, ATTN_WIDTH), cur),
        scratch_shapes=[pltpu.VMEM((TQ + 2 * H, 2 * KV_WIDTH), BF16),
                        pltpu.VMEM((TQ + 2 * H, N_KV_HEADS * LANES), BF16)],
        compiler_params=_params(2),
        name="attn",
    )(sink.astype(F32), q, kv, kv, kv, bias)


def _filter_kernel(z_ref, w1h, w1l, b1_ref, f1_ref, w2h, w2l, b2_ref, f2_ref, w3h, w3l, dec_ref,
                   k_ref, s_ref):
    i = pl.program_id(0)
    z = z_ref[...]
    h1 = jnp.sin(f1_ref[...] * (_dot3(z, w1h[...], w1l[...]) + b1_ref[...]))
    h2 = jnp.sin(f2_ref[...] * (_dot3(h1, w2h[...], w2l[...]) + b2_ref[...]))
    k = _dot3(h2, w3h[...], w3l[...])
    t = z[:, 0:1]
    k = k * (jnp.exp(-t * jnp.abs(dec_ref[...])) + WINDOW_SHIFT)
    k_ref[...] = k

    @pl.when(i == 0)
    def _():
        s_ref[...] = jnp.zeros_like(s_ref)

    s_ref[...] += jnp.sum(jnp.abs(k), axis=0, keepdims=True)


def _filter_embedding(L):
    t = np.linspace(0.0, 1.0, L, dtype=np.float32).astype(np.float64)[:, None]
    w = (2.0 * math.pi * np.arange(L, dtype=np.float32) / np.float32(L)).astype(np.float64)[:, None]
    bands = np.linspace(1e-4, FILTER_BANDS - 1, FILTER_BANDS, dtype=np.float32).astype(np.float64)[None, :]
    bw = (bands.astype(np.float32) * w.astype(np.float32)).astype(np.float64)
    z = np.concatenate([t, np.cos(bw), -np.sin(bw)], axis=-1)
    zp = np.zeros((L, LANES), np.float32)
    zp[:, :FILTER_EMB] = z.astype(np.float32)
    return jnp.asarray(zp)


def _pad2(a, r, c):
    return jnp.zeros((r, c), F32).at[:a.shape[0], :a.shape[1]].set(a.astype(F32))


def _filters(L, fw1, fb1, ff1, fw2, fb2, ff2, fw3, decay):
    H = LANES
    nf = fw3.shape[1]
    z = _filter_embedding(L)
    w1h, w1l = _split(_pad2(fw1, H, H))
    w2h, w2l = _split(_pad2(fw2, H, H))
    w3h, w3l = _split(_pad2(fw3, H, nf))
    b1 = _pad2(fb1[None], 1, H)
    f1 = _pad2(ff1[None], 1, H)
    b2 = _pad2(fb2[None], 1, H)
    f2 = _pad2(ff2[None], 1, H)
    tr = min(512, L)
    full = lambda r, c: pl.BlockSpec((r, c), lambda i: (0, 0))
    return pl.pallas_call(
        _filter_kernel,
        out_shape=[jax.ShapeDtypeStruct((L, nf), F32), jax.ShapeDtypeStruct((1, nf), F32)],
        grid=(L // tr,),
        in_specs=[pl.BlockSpec((tr, H), lambda i: (i, 0)),
                  full(H, H), full(H, H), full(1, H), full(1, H),
                  full(H, H), full(H, H), full(1, H), full(1, H),
                  full(H, nf), full(H, nf), full(1, nf)],
        out_specs=[pl.BlockSpec((tr, nf), lambda i: (i, 0)), full(1, nf)],
        compiler_params=_params(1),
        name="filter",
    )(z, w1h, w1l, b1, f1, w2h, w2l, b2, f2, w3h, w3l, decay.reshape(1, nf).astype(F32))


def _np_bf16(m64):
    return jnp.asarray(m64.astype(np.float32).astype(BF16))


def _dft_constants(L):
    N = 2 * L
    n2 = LANES
    n1 = N // n2
    h1 = n1 // 2
    k1 = np.arange(n1)[:, None]
    s1 = np.arange(h1)[None, :]
    ang = -2.0 * np.pi * ((k1 * s1) % n1) / n1
    wr, wi = np.cos(ang), np.sin(ang)
    w1_filt = np.block([[wr, wr], [wi, wi], [wr, -wr], [wi, -wi]])
    w1_cplx = np.block([[wr, -wi], [wi, wr]])
    vr, vi = wr.T / N, -wi.T / N
    w3 = np.block([[vr, -vi], [vi, vr]])
    k2 = np.arange(n2)[:, None]
    s2 = np.arange(n2)[None, :]
    a2 = -2.0 * np.pi * ((k2 * s2) % n2) / n2
    w2r, w2i = jnp.asarray(np.cos(a2), F32), jnp.asarray(np.sin(a2), F32)
    at = -2.0 * np.pi * ((np.arange(n1)[:, None] * s2) % N) / N
    twr, twi = jnp.asarray(np.cos(at), F32), jnp.asarray(np.sin(at), F32)
    mr = w2r[None] * twr[:, None, :] - w2i[None] * twi[:, None, :]
    mi = w2r[None] * twi[:, None, :] + w2i[None] * twr[:, None, :]
    fwd = jnp.concatenate([jnp.concatenate([mr, -mi], axis=2),
                           jnp.concatenate([mi, mr], axis=2)], axis=1)
    fwd = fwd.astype(BF16)
    return dict(n1=n1, w1_filt=_np_bf16(w1_filt), w1_cplx=_np_bf16(w1_cplx), w3=_np_bf16(w3),
                fwd=fwd)


SCH = 8


def _dft1_kernel(x_ref, w_ref, a_ref, *, n1):
    w = w_ref[...]
    for j in range(SCH):
        rhs = jnp.concatenate([x_ref[0, 0, :, j, :], x_ref[0, 1, :, j, :]], axis=0)
        res = jnp.dot(w, rhs.astype(BF16), preferred_element_type=F32)
        a_ref[0, :, j, :] = _pack2(res[:n1], res[n1:])


def _dft1_data(x, consts):
    B, L, C = x.shape
    n1 = consts["n1"]
    h1 = n1 // 2
    xv = x.reshape(B // 2, 2, h1, LANES, C)
    return pl.pallas_call(
        functools.partial(_dft1_kernel, n1=n1),
        out_shape=jax.ShapeDtypeStruct((B // 2, n1, LANES, C), F32),
        grid=(B // 2, LANES // SCH),
        in_specs=[pl.BlockSpec((1, 2, h1, SCH, C), lambda p, j: (p, 0, 0, j, 0)),
                  pl.BlockSpec((2 * n1, n1), lambda p, j: (0, 0))],
        out_specs=pl.BlockSpec((1, n1, SCH, C), lambda p, j: (p, 0, j, 0)),
        compiler_params=_params(2),
        name="dft1",
    )(xv, consts["w1_cplx"])


def _dft1f_kernel(x_ref, w_ref, a_ref, *, n1):
    C = HYENA_WIDTH
    w = w_ref[...]
    for j in range(SCH):
        rhs = jnp.concatenate([x_ref[:, j, :C], x_ref[:, j, C:]], axis=0)
        res = jnp.dot(w, rhs.astype(BF16), preferred_element_type=F32)
        a_ref[0, :, 0, j, :] = _pack2(res[:n1], res[n1:2 * n1])
        a_ref[0, :, 1, j, :] = _pack2(res[2 * n1:3 * n1], res[3 * n1:])


def _dft1_filter(kraw, consts):
    L, nf = kraw.shape
    C = HYENA_WIDTH
    n_ord = nf // (2 * C)
    n1 = consts["n1"]
    h1 = n1 // 2
    kv = kraw.reshape(h1, LANES, nf)
    return pl.pallas_call(
        functools.partial(_dft1f_kernel, n1=n1),
        out_shape=jax.ShapeDtypeStruct((n_ord, n1, 2, LANES, C), F32),
        grid=(n_ord, LANES // SCH),
        in_specs=[pl.BlockSpec((h1, SCH, 2 * C), lambda o, j: (0, j, o)),
                  pl.BlockSpec((4 * n1, n1), lambda o, j: (0, 0))],
        out_specs=pl.BlockSpec((1, n1, 2, SCH, C), lambda o, j: (o, 0, 0, j, 0)),
        compiler_params=_params(2),
        name="dft1f",
    )(kv, consts["w1_filt"])


KCH = 8


def _midf_kernel(a_ref, f_ref, inv_ref, b0_ref, h_ref):
    n2 = LANES
    sc = inv_ref[0]
    for k in range(KCH):
        p = jnp.concatenate(_unpack2(a_ref[0, k, :n2, :]), axis=0).astype(BF16)
        q = jnp.concatenate(_unpack2(a_ref[0, k, n2:, :]), axis=0).astype(BF16)
        h_re = jnp.dot(f_ref[k, :n2, :], p, preferred_element_type=F32)
        h_im = jnp.dot(f_ref[k, n2:, :], q, preferred_element_type=F32)
        h_ref[0, k] = _pack2((h_re - b0_ref[0]) * sc, h_im * sc)


def _filter_spectrum(af, inv_den, bwd0, consts):
    n_ord, n1, _, n2, C = af.shape
    a = af.reshape(n_ord, n1, 2 * n2, C)
    tab = pl.BlockSpec((KCH, 2 * n2, 2 * n2), lambda k, o: (k, 0, 0))
    vec = pl.BlockSpec((1, 1, C), lambda k, o: (o, 0, 0))
    return pl.pallas_call(
        _midf_kernel,
        out_shape=jax.ShapeDtypeStruct((n_ord, n1, n2, C), F32),
        grid=(n1 // KCH, n_ord),
        in_specs=[pl.BlockSpec((1, KCH, 2 * n2, C), lambda k, o: (o, k, 0, 0)), tab, vec, vec],
        out_specs=pl.BlockSpec((1, KCH, n2, C), lambda k, o: (o, k, 0, 0)),
        compiler_params=_params(2),
        name="midf",
    )(a, consts["fwd"], inv_den, bwd0)


def _mid_kernel(a_ref, f_ref, h_ref, b_ref):
    n2 = LANES
    for k in range(KCH):
        a = jnp.concatenate(_unpack2(a_ref[0, k]), axis=0).astype(BF16)
        x = jnp.dot(f_ref[k], a, preferred_element_type=F32)
        xr, xi = x[:n2], x[n2:]
        hr, hi = _unpack2(h_ref[0, k])
        y = jnp.concatenate([xr * hr - xi * hi, xr * hi + xi * hr], axis=0)
        b = lax.dot_general(f_ref[k], y.astype(BF16), (((0,), (0,)), ((), ())), preferred_element_type=F32)
        b_ref[0, k] = _pack2(b[:n2], b[n2:])


def _mid(a, hspec, order, consts):
    P, n1, n2, C = a.shape
    tab = pl.BlockSpec((KCH, 2 * n2, 2 * n2), lambda k, p: (k, 0, 0))
    return pl.pallas_call(
        _mid_kernel,
        out_shape=jax.ShapeDtypeStruct((P, n1, n2, C), F32),
        grid=(n1 // KCH, P),
        in_specs=[pl.BlockSpec((1, KCH, n2, C), lambda k, p: (p, k, 0, 0)),
                  tab,
                  pl.BlockSpec((1, KCH, n2, C), lambda k, p: (order, k, 0, 0))],
        out_specs=pl.BlockSpec((1, KCH, n2, C), lambda k, p: (p, k, 0, 0)),
        compiler_params=_params(2),
        name="mid",
    )(a, consts["fwd"], hspec)


def _dft3_kernel(b_ref, w_ref, v_ref, g_ref, skip_ref, *rest, h1, chain):
    if chain:
        w1_ref, z_ref, a_ref, slab_ref = rest
        w1 = w1_ref[...]
    else:
        z_ref, slab_ref = rest
    w = w_ref[...]
    skip = skip_ref[0]
    n1 = 2 * h1
    for j in range(SCH):
        slab_ref[...] = b_ref[0, :, j, :]
        rhs = jnp.concatenate(_unpack2(slab_ref[...]), axis=0)
        y = jnp.dot(w, rhs.astype(BF16), preferred_element_type=F32)
        z = [g_ref[0, r, :, j, :] * (y[r * h1:(r + 1) * h1] + v_ref[0, r, :, j, :] * skip) for r in range(2)]
        for r in range(2):
            z_ref[0, r, :, j, :] = z[r]
        if chain:
            res = jnp.dot(w1, jnp.concatenate(z, axis=0).astype(BF16), preferred_element_type=F32)
            a_ref[0, :, j, :] = _pack2(res[:n1], res[n1:])


def _dft3_gate(b5, v, gate, skip, consts, chain):
    P, n1, n2, C = b5.shape
    h1 = n1 // 2
    B, L, _ = v.shape
    five = lambda t: t.reshape(P, 2, h1, n2, C)
    dat = pl.BlockSpec((1, 2, h1, SCH, C), lambda p, j: (p, 0, 0, j, 0))
    packed = pl.BlockSpec((1, n1, SCH, C), lambda p, j: (p, 0, j, 0))
    in_specs = [packed, pl.BlockSpec((n1, 2 * n1), lambda p, j: (0, 0)), dat, dat,
                pl.BlockSpec((1, C), lambda p, j: (0, 0))]
    args = [b5, consts["w3"], five(v), five(gate), skip.reshape(1, C).astype(F32)]
    out_shape = [jax.ShapeDtypeStruct((P, 2, h1, n2, C), F32)]
    out_specs = [dat]
    if chain:
        in_specs.append(pl.BlockSpec((2 * n1, n1), lambda p, j: (0, 0)))
        args.append(consts["w1_cplx"])
        out_shape.append(jax.ShapeDtypeStruct((P, n1, n2, C), F32))
        out_specs.append(packed)
    outs = pl.pallas_call(
        functools.partial(_dft3_kernel, h1=h1, chain=chain),
        out_shape=out_shape,
        grid=(P, n2 // SCH),
        in_specs=in_specs,
        out_specs=out_specs,
        scratch_shapes=[pltpu.VMEM((n1, C), F32)],
        compiler_params=_params(2),
        name="dft3",
    )(*args)
    z = outs[0].reshape(B, L, C)
    return (z, outs[1]) if chain else (z, None)


def _hyena(v, x1, x2, fw1, fb1, ff1, fw2, fb2, ff2, fw3, decay, skip):
    B, L, C = v.shape
    consts = _dft_constants(L)
    kraw, ksum = _filters(L, fw1, fb1, ff1, fw2, fb2, ff2, fw3, decay)
    ks = ksum.reshape(2, 2, C)
    inv_den = (1.0 / (ks[:, 0] + ks[:, 1])).reshape(2, 1, C)
    bwd0 = kraw[0].reshape(2, 2, C)[:, 1].reshape(2, 1, C)
    hspec = _filter_spectrum(_dft1_filter(kraw, consts), inv_den, bwd0, consts)
    gates = (x1, x2)
    z, a5 = v, _dft1_data(v, consts)
    for o, gate in enumerate(gates):
        b5 = _mid(a5, hspec, o, consts)
        z, a5 = _dft3_gate(b5, z, gate, skip[o], consts, chain=o + 1 < len(gates))
    return z


def _merge_kernel(attn_ref, hy_ref, ga_ref, gh_ref, x_ref, mod_ref, wa_ref, wh_ref, wo_ref,
                  g1_ref, b1_ref, rwh_ref, rwl_ref, rb_ref, tri_ref,
                  x1_ref, h2_ref, route_ref, wts_ref, cnt_ref, carry_ref):
    @pl.when((pl.program_id(0) == 0) & (pl.program_id(1) == 0))
    def _():
        carry_ref[...] = jnp.zeros_like(carry_ref)

    logits = _merge_dense(attn_ref, hy_ref, ga_ref, gh_ref, x_ref, mod_ref, wa_ref, wh_ref, wo_ref,
                          g1_ref, b1_ref, rwh_ref, rwl_ref, rb_ref, x1_ref, h2_ref)
    _route_rows(logits, tri_ref, route_ref, wts_ref, carry_ref)
    cnt_ref[...] = carry_ref[...]


def _merge_dense(attn_ref, hy_ref, ga_ref, gh_ref, x_ref, mod_ref, wa_ref, wh_ref, wo_ref,
                 g1_ref, b1_ref, rwh_ref, rwl_ref, rb_ref, x1_ref, h2_ref):
    a = jnp.dot(attn_ref[0], wa_ref[...], preferred_element_type=F32)
    hy = jnp.dot(hy_ref[0].astype(BF16), wh_ref[...], preferred_element_type=F32)
    merged = ga_ref[0].astype(F32) * a + gh_ref[0].astype(F32) * hy
    y = jnp.dot(merged.astype(BF16), wo_ref[...], preferred_element_type=F32)
    gate1 = mod_ref[0, 2:3, :]
    shift2 = mod_ref[0, 3:4, :]
    scale2 = mod_ref[0, 4:5, :]
    x1 = _layer_norm(DN_ALPHA * x_ref[0] + gate1 * y, g1_ref[...], b1_ref[...])
    x1_ref[0] = x1
    h2 = x1 * (1.0 + scale2) + shift2
    half = h2.shape[1] // 2
    h2_ref[0] = _pack2(h2[:, :half], h2[:, half:])
    return _dot3(h2, rwh_ref[...], rwl_ref[...]) + rb_ref[...]


def _route_rows(logits, tri_ref, route_ref, wts_ref, carry_ref):
    tm = logits.shape[0]
    lane = lax.broadcasted_iota(jnp.int32, (tm, LANES), 1)
    lanef = lane.astype(F32)
    big = float(LANES)

    def first_lane(mask):
        return jnp.min(jnp.where(mask, lanef, big), axis=1, keepdims=True).astype(jnp.int32)

    gmask = lane < N_GROUPS
    gl = jnp.where(gmask, logits, NEG)
    gmax = jnp.max(gl, axis=1, keepdims=True)
    gidx = first_lane(gl == gmax)
    pg = 1.0 / jnp.sum(jnp.exp(gl - gmax), axis=1, keepdims=True)
    lo = ROUTE_OFF + gidx * EXPERTS_PER_GROUP
    emask = (lane >= lo) & (lane < lo + EXPERTS_PER_GROUP)
    el = jnp.where(emask, logits, NEG)
    v1 = jnp.max(el, axis=1, keepdims=True)
    i1 = first_lane(el == v1)
    el2 = jnp.where(emask & (lane != i1), logits, NEG)
    v2 = jnp.max(el2, axis=1, keepdims=True)
    i2 = first_lane(el2 == v2)
    e21 = jnp.exp(v2 - v1)
    w1 = pg / (1.0 + e21)
    w2 = pg * e21 / (1.0 + e21)

    sel1 = lane == i1
    sel2 = lane == i2
    onehot = jnp.where(sel1 | sel2, 1.0, 0.0)
    prefix = jnp.dot(tri_ref[...], onehot.astype(BF16), preferred_element_type=F32) + carry_ref[...]
    r1 = jnp.sum(jnp.where(sel1, prefix, 0.0), axis=1, keepdims=True)
    r2 = jnp.sum(jnp.where(sel2, prefix, 0.0), axis=1, keepdims=True)
    carry_ref[...] += jnp.sum(onehot, axis=0, keepdims=True)

    e1 = (i1 - ROUTE_OFF).astype(F32)
    e2 = (i2 - ROUTE_OFF).astype(F32)
    table = jnp.where(lane == 0, e1, jnp.where(lane == 1, e2, jnp.where(lane == 2, r1, jnp.where(lane == 3, r2, 0.0))))
    route_ref[...] = table.T[:SUBLANES].astype(jnp.int32)
    wts_ref[0] = jnp.where(lane == 0, w1, jnp.where(lane == 1, w2, 0.0))


def _merge(attn, hy, ga, gh, x, mod, w_attn_o, w_hy_o, w_out, ln1_g, ln1_b, rg_w, rg_b, re_w, re_b):
    B, S, D = x.shape
    tm = min(512, S)
    spare = LANES - N_GROUPS - N_EXPERTS
    rw = jnp.concatenate([rg_w, re_w, jnp.zeros((D, spare), F32)], axis=1)
    rb = jnp.concatenate([rg_b, re_b, jnp.zeros((spare,), F32)]).reshape(1, LANES)
    rwh, rwl = _split(rw)
    tri = (jnp.arange(tm)[:, None] > jnp.arange(tm)[None, :]).astype(BF16)
    row = lambda b, i: (b, i, 0)
    full = lambda r, c: pl.BlockSpec((r, c), lambda b, i: (0, 0))
    per_b = S // tm
    outs = [jax.ShapeDtypeStruct((B, S, D), F32), jax.ShapeDtypeStruct((B, S, D // 2), F32),
            jax.ShapeDtypeStruct((SUBLANES, B * S), jnp.int32), jax.ShapeDtypeStruct((B, S, LANES), F32),
            jax.ShapeDtypeStruct((1, LANES), F32)]
    return pl.pallas_call(
        _merge_kernel,
        out_shape=outs,
        grid=(B, per_b),
        in_specs=[pl.BlockSpec((1, tm, ATTN_WIDTH), row), pl.BlockSpec((1, tm, HYENA_WIDTH), row),
                  pl.BlockSpec((1, tm, D), row), pl.BlockSpec((1, tm, D), row), pl.BlockSpec((1, tm, D), row),
                  pl.BlockSpec((1, 6, D), lambda b, i: (b, 0, 0)),
                  full(ATTN_WIDTH, D), full(HYENA_WIDTH, D), full(D, D),
                  full(1, D), full(1, D), full(D, LANES), full(D, LANES), full(1, LANES), full(tm, tm)],
        out_specs=[pl.BlockSpec((1, tm, D), row), pl.BlockSpec((1, tm, D // 2), row),
                   pl.BlockSpec((SUBLANES, tm), lambda b, i: (0, b * per_b + i)),
                   pl.BlockSpec((1, tm, LANES), row), full(1, LANES)],
        scratch_shapes=[pltpu.VMEM((1, LANES), F32)],
        compiler_params=_params(2),
        name="merge",
    )(attn, hy, ga, gh, x, mod, w_attn_o.astype(BF16), w_hy_o.astype(BF16), w_out.astype(BF16),
      ln1_g.reshape(1, D), ln1_b.reshape(1, D), rwh, rwl, rb, tri)


SC_ROWS = 64


def _sc_workers():
    info = plsc.get_sparse_core_info()
    return info.num_cores, info.num_cores * info.num_subcores


def _sc_split(n):
    _, workers = _sc_workers()
    per_worker = n // workers
    chunks = per_worker // SC_ROWS
    assert per_worker * workers == n and chunks * SC_ROWS == per_worker and chunks % 2 == 0
    return workers, per_worker, chunks


def _sc_scatter_rows(src, idx0, idx1, n_out):
    n, width = src.shape
    nc, _ = _sc_workers()
    workers, per_worker, chunks = _sc_split(n)
    mesh = plsc.VectorSubcoreMesh(core_axis_name="c", subcore_axis_name="s")

    def body(src_hbm, i0_hbm, i1_hbm, out_hbm, i0_v, i1_v, rows_v, sem, ssem):
        wid = lax.axis_index("s") * nc + lax.axis_index("c")
        base = wid * per_worker
        pltpu.sync_copy(i0_hbm.at[wid], i0_v)
        pltpu.sync_copy(i1_hbm.at[wid], i1_v)

        def load(chunk, buf):
            return pltpu.make_async_copy(src_hbm.at[pl.ds(base + chunk * SC_ROWS, SC_ROWS)], rows_v.at[buf], sem)

        load(0, 0).start()

        @pl.loop(0, chunks, step=2)
        def _(c):
            for b in range(2):
                chunk = c + b
                load(chunk, b).wait()

                @pl.when(chunk + 1 < chunks)
                def _():
                    load(chunk + 1, 1 - b).start()

                first = pltpu.make_async_copy(rows_v.at[b], out_hbm.at[i0_v.at[chunk]], ssem)
                second = pltpu.make_async_copy(rows_v.at[b], out_hbm.at[i1_v.at[chunk]], ssem)
                first.start()
                second.start()
                first.wait()
                second.wait()

    shaped = lambda i: i.reshape(workers, chunks, SC_ROWS)
    return pl.kernel(
        body,
        out_type=jax.ShapeDtypeStruct((n_out, width), src.dtype),
        mesh=mesh,
        scratch_types=[pltpu.VMEM((chunks, SC_ROWS), jnp.int32),
                       pltpu.VMEM((chunks, SC_ROWS), jnp.int32),
                       pltpu.VMEM((2, SC_ROWS, width), src.dtype),
                       pltpu.SemaphoreType.DMA, pltpu.SemaphoreType.DMA],
        name="sc_scatter",
    )(src, shaped(idx0), shaped(idx1))


def _expert_kernel(first_ref, nb_ref, sz_ref, tot_ref, w1_ref, w3_ref, w2_ref, xb_ref, yb_ref,
                   xbuf, ybuf, c1_ref, c3_ref, c2_ref, lsem, ssem):
    e = pl.program_id(0)
    nb = nb_ref[e]
    first = first_ref[e]
    total = tot_ref[0]
    rows = xbuf.shape[1]

    def load(g, slot):
        src = xb_ref.at[pl.ds(pl.multiple_of(g * rows, rows), rows)]
        return pltpu.make_async_copy(src, xbuf.at[slot], lsem.at[slot])

    def store(g, slot):
        dst = yb_ref.at[pl.ds(pl.multiple_of(g * rows, rows), rows)]
        return pltpu.make_async_copy(ybuf.at[slot], dst, ssem.at[slot])

    @pl.when((e == 0) & (total > 0))
    def _():
        load(0, 0).start()

    @pl.when(nb > 0)
    def _():
        c1_ref[...] = w1_ref[0].astype(BF16)
        c3_ref[...] = w3_ref[0].astype(BF16)
        c2_ref[...] = w2_ref[0].astype(BF16)

        def block(j, carry):
            g = first + j
            slot = lax.rem(g, 2)
            load(g, slot).wait()

            @pl.when(g + 1 < total)
            def _():
                load(g + 1, 1 - slot).start()

            @pl.when(g >= 2)
            def _():
                store(g - 2, slot).wait()

            n_valid = sz_ref[e] - j * rows
            pieces = jnp.minimum((n_valid + EXPERT_ROWS - 1) // EXPERT_ROWS, rows // EXPERT_ROWS)

            def swiglu(n):
                rid = lax.broadcasted_iota(jnp.int32, (n, 1), 0)
                xa, xb = _unpack2(jnp.where(rid < n_valid, xbuf[slot, :n, :], 0.0))
                x = jnp.concatenate([xa, xb], axis=1).astype(BF16)
                a = jnp.dot(x, c1_ref[...], preferred_element_type=F32)
                gate = jnp.dot(x, c3_ref[...], preferred_element_type=F32)
                hmid = (a * _sigmoid(a) * gate).astype(BF16)
                y = jnp.dot(hmid, c2_ref[...], preferred_element_type=F32)
                half = y.shape[1] // 2
                ybuf[slot, :n, :] = _pack2(y[:, :half], y[:, half:])

            for q in range(1, rows // EXPERT_ROWS + 1):
                pl.when(pieces == q)(functools.partial(swiglu, q * EXPERT_ROWS))
            store(g, slot).start()
            return carry

        lax.fori_loop(0, nb, block, 0)

    @pl.when(e == pl.num_programs(0) - 1)
    def _():
        for back in (2, 1):
            @pl.when(total >= back)
            def _():
                g = total - back
                store(g, lax.rem(g, 2)).wait()


def _experts(xb, first_blk, n_blk, sizes, w1, w3, w2):
    P, W = xb.shape
    E, D, DE = w1.shape
    total = jnp.sum(n_blk, keepdims=True)
    wspec = lambda r, c: pl.BlockSpec((1, r, c), lambda e, *_: (e, 0, 0))
    grid_spec = pltpu.PrefetchScalarGridSpec(
        num_scalar_prefetch=4,
        grid=(E,),
        in_specs=[wspec(D, DE), wspec(D, DE), wspec(DE, D), pl.BlockSpec(memory_space=pl.ANY)],
        out_specs=pl.BlockSpec(memory_space=pl.ANY),
        scratch_shapes=[pltpu.VMEM((2, MOE_BLOCK, W), F32), pltpu.VMEM((2, MOE_BLOCK, W), F32),
                        pltpu.VMEM((D, DE), BF16), pltpu.VMEM((D, DE), BF16), pltpu.VMEM((DE, D), BF16),
                        pltpu.SemaphoreType.DMA((2,)), pltpu.SemaphoreType.DMA((2,))],
    )
    return pl.pallas_call(
        _expert_kernel,
        out_shape=jax.ShapeDtypeStruct((P, W), F32),
        grid_spec=grid_spec,
        compiler_params=_params(1),
        name="experts",
    )(first_blk, n_blk, sizes, total, w1, w3, w2, xb)


def _sc_gather_rows(table, idx):
    n, width = idx.shape[0], table.shape[1]
    nc, _ = _sc_workers()
    workers, per_worker, chunks = _sc_split(n)
    mesh = plsc.VectorSubcoreMesh(core_axis_name="c", subcore_axis_name="s")

    def body(table_hbm, idx_hbm, out_hbm, idx_v, rows_v, sem):
        wid = lax.axis_index("s") * nc + lax.axis_index("c")
        base = wid * per_worker
        pltpu.sync_copy(idx_hbm.at[wid], idx_v)

        def gather(chunk, buf):
            return pltpu.make_async_copy(table_hbm.at[idx_v.at[chunk]], rows_v.at[buf], sem)

        gather(0, 0).start()

        @pl.loop(0, chunks, step=2)
        def _(c):
            for b in range(2):
                chunk = c + b
                gather(chunk, b).wait()

                @pl.when(chunk + 1 < chunks)
                def _():
                    gather(chunk + 1, 1 - b).start()

                pltpu.sync_copy(rows_v.at[b], out_hbm.at[pl.ds(base + chunk * SC_ROWS, SC_ROWS)])

    return pl.kernel(
        body,
        out_type=jax.ShapeDtypeStruct((n, width), table.dtype),
        mesh=mesh,
        scratch_types=[pltpu.VMEM((chunks, SC_ROWS), jnp.int32),
                       pltpu.VMEM((2, SC_ROWS, width), table.dtype),
                       pltpu.SemaphoreType.DMA],
        name="sc_gather",
    )(table, idx.reshape(workers, chunks, SC_ROWS))


def _combine_dense_kernel(r0_ref, r1_ref, wts_ref, x1_ref, mod_ref, g_ref, b_ref, *rest):
    o_ref = rest[-1]
    w = wts_ref[...]
    y0 = jnp.concatenate(_unpack2(r0_ref[0]), axis=1)
    y1 = jnp.concatenate(_unpack2(r1_ref[0]), axis=1)
    y = w[:, 0:1] * y0 + w[:, 1:2] * y1
    gate2 = mod_ref[0, 5:6, :]
    o_ref[...] = _layer_norm(DN_ALPHA * x1_ref[...] + gate2 * y, g_ref[...], b_ref[...])


def _combine_dense(rows, wts, x1, mod, ln2_g, ln2_b, S, b, out):
    T, D = x1.shape
    tm = min(512, S)
    per_b = S // tm
    here = lambda i: (b * per_b + i, 0)
    in_specs = [pl.BlockSpec((1, tm, rows.shape[2]), lambda i: (0, i, 0)),
                pl.BlockSpec((1, tm, rows.shape[2]), lambda i: (1, i, 0)),
                pl.BlockSpec((tm, LANES), here),
                pl.BlockSpec((tm, D), here),
                pl.BlockSpec((1, 6, D), lambda i: (b, 0, 0)),
                pl.BlockSpec((1, D), lambda i: (0, 0)),
                pl.BlockSpec((1, D), lambda i: (0, 0))]
    args = [rows, rows, wts, x1, mod, ln2_g.reshape(1, D), ln2_b.reshape(1, D)]
    aliases = {}
    if out is not None:
        in_specs.append(pl.BlockSpec(memory_space=pl.ANY))
        aliases = {len(args): 0}
        args.append(out)
    return pl.pallas_call(
        _combine_dense_kernel,
        out_shape=jax.ShapeDtypeStruct((T, D), F32),
        grid=(per_b,),
        in_specs=in_specs,
        out_specs=pl.BlockSpec((tm, D), here),
        input_output_aliases=aliases,
        compiler_params=_params(1),
        name="combine",
    )(*args)


def _moe(h2, x1, route, wts, counts, mod, w1, w3, w2, ln2_g, ln2_b):
    B, S, D = x1.shape
    T = B * S
    P = 2 * T + N_EXPERTS * MOE_BLOCK
    sizes = counts[0, ROUTE_OFF:ROUTE_OFF + N_EXPERTS].astype(jnp.int32)
    n_blk = (sizes + MOE_BLOCK - 1) // MOE_BLOCK
    psizes = n_blk * MOE_BLOCK
    poffs = jnp.cumsum(psizes) - psizes
    sel = route[0:2, None, :] == jnp.arange(N_EXPERTS, dtype=jnp.int32)[None, :, None]
    dest = route[2:4] + jnp.sum(jnp.where(sel, poffs[None, :, None], 0), axis=1)
    xb = _sc_scatter_rows(h2.reshape(T, D // 2), dest[0], dest[1], P)
    yb = _experts(xb, poffs // MOE_BLOCK, n_blk, sizes, w1, w3, w2)
    out = None
    for b in range(B):
        slot_major = dest[:, b * S:(b + 1) * S].reshape(2 * S)
        rows = _sc_gather_rows(yb, slot_major).reshape(2, S, yb.shape[1])
        out = _combine_dense(rows, wts.reshape(T, LANES), x1.reshape(T, D), mod, ln2_g, ln2_b, S, b, out)
    return out.reshape(B, S, D)


def _layer(x, c, w_ada, b_ada, w_in, conv_w, conv_b, fw1, fb1, ff1, fw2, fb2, ff2, fw3, decay, skip,
           w_hy_o, w_attn_o, attn_sink, w_out, ln1_g, ln1_b, rg_w, rg_b, re_w, re_b, ew1, ew3, ew2,
           ln2_g, ln2_b):
    mod = _ada(c, w_ada, b_ada)
    q, kv, hv, hx1, hx2, ga, gh = _in_proj(x, mod, w_in, conv_w, conv_b)
    attn = _attention(q, kv, attn_sink)
    hy = _hyena(hv, hx1, hx2, fw1, fb1, ff1, fw2, fb2, ff2, fw3, decay, skip)
    x1, h2, route, wts, counts = _merge(attn, hy, ga, gh, x, mod, w_attn_o, w_hy_o, w_out,
                                        ln1_g, ln1_b, rg_w, rg_b, re_w, re_b)
    return _moe(h2, x1, route, wts, counts, mod, ew1, ew3, ew2, ln2_g, ln2_b)


def kernel(x, c, w_ada, b_ada, w_in, conv_w, conv_b, filt_w1, filt_b1, filt_freq1, filt_w2, filt_b2, filt_freq2, filt_w3, filt_decay, hy_skip, w_hy_o, w_attn_o, attn_sink, w_out, ln1_g, ln1_b, router_group_w, router_group_b, router_expert_w, router_expert_b, exp_w1, exp_w3, exp_w2, ln2_g, ln2_b):
    for l in range(w_ada.shape[0]):
        x = _layer(x, c, w_ada[l], b_ada[l], w_in[l], conv_w[l], conv_b[l], filt_w1[l], filt_b1[l],
                   filt_freq1[l], filt_w2[l], filt_b2[l], filt_freq2[l], filt_w3[l], filt_decay[l],
                   hy_skip[l], w_hy_o[l], w_attn_o[l], attn_sink[l], w_out[l], ln1_g[l], ln1_b[l],
                   router_group_w[l], router_group_b[l], router_expert_w[l], router_expert_b[l],
                   exp_w1[l], exp_w3[l], exp_w2[l], ln2_g[l], ln2_b[l])
    return x
```

```python
import functools
import math

import numpy as np
import jax
import jax.numpy as jnp
from jax import lax
from jax.experimental import pallas as pl
from jax.experimental.pallas import tpu as pltpu
from jax.experimental.pallas import tpu_sc as plsc

F32 = jnp.float32
BF16 = jnp.bfloat16

N_HEADS = 8
N_KV_HEADS = 2
HEAD_DIM = 64
ATTN_WIDTH = N_HEADS * HEAD_DIM
KV_WIDTH = N_KV_HEADS * HEAD_DIM
WINDOW = 128
HYENA_WIDTH = 512
FILTER_EMB = 33
FILTER_BANDS = (FILTER_EMB - 1) // 2
WINDOW_SHIFT = 0.05
N_GROUPS = 8
EXPERTS_PER_GROUP = 8
N_EXPERTS = N_GROUPS * EXPERTS_PER_GROUP
MOE_BLOCK = 512
EXPERT_ROWS = 128
LN_EPS = 1e-5
DEPTH = 1
DN_ALPHA = (2.0 * DEPTH) ** 0.25
NEG = -1e30

LANES = 128
SUBLANES = 8
ROUTE_OFF = N_GROUPS
VMEM_LIMIT = 56 * 1024 * 1024


def _params(n_axes, vmem=VMEM_LIMIT):
    return pltpu.CompilerParams(dimension_semantics=("arbitrary",) * n_axes, vmem_limit_bytes=vmem)


def _split(a):
    hi = a.astype(BF16)
    lo = (a - hi.astype(F32)).astype(BF16)
    return hi, lo


def _dot3(a, b_hi, b_lo):
    a_hi, a_lo = _split(a)
    acc = jnp.dot(a_hi, b_hi, preferred_element_type=F32)
    acc = acc + jnp.dot(a_hi, b_lo, preferred_element_type=F32)
    acc = acc + jnp.dot(a_lo, b_hi, preferred_element_type=F32)
    return acc


def _pack2(a, b):
    ia = lax.bitcast_convert_type(a.astype(BF16).astype(F32), jnp.int32)
    ib = lax.bitcast_convert_type(b.astype(BF16).astype(F32), jnp.int32)
    return lax.bitcast_convert_type(ia | lax.shift_right_logical(ib, 16), F32)


def _unpack2(p):
    p = lax.bitcast_convert_type(p, jnp.int32)
    a = lax.bitcast_convert_type(p & jnp.int32(-65536), F32)
    b = lax.bitcast_convert_type(lax.shift_left(p, 16), F32)
    return a, b


def _sigmoid(x):
    return 0.5 * jnp.tanh(0.5 * x) + 0.5


def _layer_norm(r, g, b):
    mu = jnp.mean(r, axis=-1, keepdims=True)
    d = r - mu
    var = jnp.mean(d * d, axis=-1, keepdims=True)
    return d * lax.rsqrt(var + LN_EPS) * g + b


def _ada_kernel(c_ref, w_ref, b_ref, o_ref):
    c = c_ref[...]
    s = c * _sigmoid(c)
    wh, wl = _split(w_ref[...])
    o_ref[...] = _dot3(s, wh, wl) + b_ref[...]


def _ada(c, w_ada, b_ada):
    B, D = c.shape
    n_out = w_ada.shape[1]
    rows = SUBLANES
    cp = jnp.pad(c, ((0, rows - B), (0, 0)))
    tn = 1024
    out = pl.pallas_call(
        _ada_kernel,
        out_shape=jax.ShapeDtypeStruct((rows, n_out), F32),
        grid=(n_out // tn,),
        in_specs=[pl.BlockSpec((rows, D), lambda j: (0, 0)),
                  pl.BlockSpec((D, tn), lambda j: (0, j)),
                  pl.BlockSpec((1, tn), lambda j: (0, j))],
        out_specs=pl.BlockSpec((rows, tn), lambda j: (0, j)),
        compiler_params=_params(1),
        name="ada",
    )(cp, w_ada, b_ada.reshape(1, n_out))
    return out[:B].reshape(B, 6, D)


def _inproj_kernel(x_ref, xp_ref, xn_ref, mod_ref, w_ref, cw_ref, cb_ref,
                   q_ref, kv_ref, v_ref, x1_ref, x2_ref, ga_ref, gh_ref):
    i = pl.program_id(1)
    n = pl.num_programs(1)
    C = HYENA_WIDTH
    x = x_ref[0]
    tm, D = x.shape
    shift = mod_ref[0, 0:1, :]
    scale = mod_ref[0, 1:2, :]
    h = (x * (1.0 + scale) + shift).astype(BF16)

    def seg(lo, hi):
        return jnp.dot(h, w_ref[:, lo:hi], preferred_element_type=F32)

    o_q = 0
    o_kv = o_q + ATTN_WIDTH
    o_hy = o_kv + 2 * KV_WIDTH
    o_ga = o_hy + 3 * C
    o_gh = o_ga + D
    ga_ref[0] = _sigmoid(seg(o_ga, o_ga + D)).astype(BF16)
    gh_ref[0] = _sigmoid(seg(o_gh, o_gh + D)).astype(BF16)

    u = seg(o_hy, o_hy + 3 * C)
    xe = jnp.concatenate([xp_ref[0], xn_ref[0]], axis=0)
    he = (xe * (1.0 + scale) + shift).astype(BF16)
    ue = jnp.dot(he, w_ref[:, o_hy:o_hy + 3 * C], preferred_element_type=F32)
    prow = jnp.where(i > 0, ue[SUBLANES - 1:SUBLANES], 0.0)
    nrow = jnp.where(i < n - 1, ue[SUBLANES:SUBLANES + 1], 0.0)
    rid = lax.broadcasted_iota(jnp.int32, (tm, 1), 0)
    up = jnp.where(rid == 0, prow, pltpu.roll(u, 1, 0))
    dn = jnp.where(rid == tm - 1, nrow, pltpu.roll(u, tm - 1, 0))
    conv = cw_ref[0:1, :] * up + cw_ref[1:2, :] * u + cw_ref[2:3, :] * dn + cb_ref[...]
    v_ref[0] = conv[:, :C]
    x1_ref[0] = conv[:, C:2 * C]
    x2_ref[0] = conv[:, 2 * C:]

    q_ref[0] = (seg(o_q, o_q + ATTN_WIDTH) * (HEAD_DIM ** -0.5)).astype(BF16)
    kv_ref[0] = seg(o_kv, o_kv + 2 * KV_WIDTH).astype(BF16)


def _in_proj(x, mod, w_in, conv_w, conv_b):
    B, S, D = x.shape
    C = HYENA_WIDTH
    tm = min(1024, S)
    r8 = tm // SUBLANES
    nb8 = S // SUBLANES
    wb = w_in.astype(BF16)
    nw = wb.shape[1]
    row = lambda b, i: (b, i, 0)
    shapes = [(ATTN_WIDTH, BF16), (2 * KV_WIDTH, BF16), (C, F32), (C, F32), (C, F32), (D, BF16), (D, BF16)]
    return pl.pallas_call(
        _inproj_kernel,
        out_shape=[jax.ShapeDtypeStruct((B, S, w), dt) for w, dt in shapes],
        grid=(B, S // tm),
        in_specs=[pl.BlockSpec((1, tm, D), row),
                  pl.BlockSpec((1, SUBLANES, D), lambda b, i: (b, jnp.maximum(i * r8 - 1, 0), 0)),
                  pl.BlockSpec((1, SUBLANES, D), lambda b, i: (b, jnp.minimum((i + 1) * r8, nb8 - 1), 0)),
                  pl.BlockSpec((1, 6, D), lambda b, i: (b, 0, 0)),
                  pl.BlockSpec((D, nw), lambda b, i: (0, 0)),
                  pl.BlockSpec((3, 3 * C), lambda b, i: (0, 0)),
                  pl.BlockSpec((1, 3 * C), lambda b, i: (0, 0))],
        out_specs=[pl.BlockSpec((1, tm, w), row) for w, _ in shapes],
        compiler_params=_params(2),
        name="in_proj",
    )(x, x, x, mod, wb, conv_w.astype(F32), conv_b.reshape(1, 3 * C).astype(F32))


ATT_TQ = 512
ATT_QB = 128
ATT_STACK = 4


def _attn_kernel(sink_ref, q_ref, kvp_ref, kvc_ref, kvn_ref, bias_ref, o_ref, kv_scr, vx_scr, *, seq_len):
    i = pl.program_id(1)
    H = WINDOW
    TQ = q_ref.shape[1]
    Q = min(ATT_QB, TQ)
    band = Q + 2 * H
    G = N_HEADS // N_KV_HEADS
    kv_scr[0:H] = kvp_ref[0]
    kv_scr[H:H + TQ] = kvc_ref[0]
    kv_scr[H + TQ:] = kvn_ref[0]
    for kv in range(N_KV_HEADS):
        vx_scr[:, kv * LANES:kv * LANES + HEAD_DIM] = kv_scr[:, KV_WIDTH + kv * HEAD_DIM:KV_WIDTH + (kv + 1) * HEAD_DIM]
        vx_scr[:, kv * LANES + HEAD_DIM:(kv + 1) * LANES] = jnp.ones((TQ + 2 * H, LANES - HEAD_DIM), BF16)
    col = lax.broadcasted_iota(jnp.int32, (1, band), 1)
    rhead = lax.broadcasted_iota(jnp.int32, (ATT_STACK * Q, 1), 0) // Q
    for j in range(TQ // Q):
        kpos = i * TQ + j * Q - H + col
        colbias = jnp.where((kpos >= 0) & (kpos < seq_len), 0.0, NEG)
        for kv in range(N_KV_HEADS):
            kk = kv_scr[j * Q:j * Q + band, kv * HEAD_DIM:(kv + 1) * HEAD_DIM]
            vx = vx_scr[j * Q:j * Q + band, kv * LANES:(kv + 1) * LANES]
            for sub in range(G // ATT_STACK):
                first = sub * ATT_STACK
                heads = [kv * G + first + g for g in range(ATT_STACK)]
                qg = jnp.concatenate([q_ref[0, j * Q:(j + 1) * Q, h * HEAD_DIM:(h + 1) * HEAD_DIM] for h in heads],
                                     axis=0)
                s = lax.dot_general(qg, kk, (((1,), (1,)), ((), ())), preferred_element_type=F32)
                s = s + bias_ref[kv, first * Q:(first + ATT_STACK) * Q, :] + colbias
                snk = sink_ref[heads[-1]]
                for g in range(ATT_STACK - 2, -1, -1):
                    snk = jnp.where(rhead == g, sink_ref[heads[g]], snk)
                m = jnp.maximum(jnp.max(s, axis=1, keepdims=True), snk)
                p = jnp.exp(s - m).astype(BF16)
                ox = jnp.dot(p, vx, preferred_element_type=F32)
                den = ox[:, HEAD_DIM:HEAD_DIM + 1] + jnp.exp(snk - m)
                o = ox[:, :HEAD_DIM] / den
                for g, h in enumerate(heads):
                    o_ref[0, j * Q:(j + 1) * Q, h * HEAD_DIM:(h + 1) * HEAD_DIM] = o[g * Q:(g + 1) * Q].astype(BF16)


def _attention(q, kv, sink):
    B, S, _ = q.shape
    H = WINDOW
    TQ = min(ATT_TQ, S)
    Q = min(ATT_QB, TQ)
    r = TQ // H
    nq = S // H
    G = N_HEADS // N_KV_HEADS
    assert G % ATT_STACK == 0
    a = jnp.arange(Q)[:, None]
    j = jnp.arange(Q + 2 * H)[None, :]
    rel = jnp.abs(j - H - a).astype(F32)
    slopes = 2.0 ** (-8.0 * jnp.arange(1, N_HEADS + 1, dtype=F32) / N_HEADS)
    bias = jnp.where(rel[None] <= WINDOW, -slopes[:, None, None] * rel[None], NEG).astype(F32)
    bias = bias.reshape(N_KV_HEADS, G * Q, Q + 2 * H)
    cur = lambda b, i: (b, i, 0)
    return pl.pallas_call(
        functools.partial(_attn_kernel, seq_len=S),
        out_shape=jax.ShapeDtypeStruct((B, S, ATTN_WIDTH), BF16),
        grid=(B, S // TQ),
        in_specs=[pl.BlockSpec(memory_space=pltpu.SMEM),
                  pl.BlockSpec((1, TQ, ATTN_WIDTH), cur),
                  pl.BlockSpec((1, H, 2 * KV_WIDTH), lambda b, i: (b, jnp.maximum(i * r - 1, 0), 0)),
                  pl.BlockSpec((1, TQ, 2 * KV_WIDTH), cur),
                  pl.BlockSpec((1, H, 2 * KV_WIDTH), lambda b, i: (b, jnp.minimum((i + 1) * r, nq - 1), 0)),
                  pl.BlockSpec((N_KV_HEADS, G * Q, Q + 2 * H), lambda b, i: (0, 0, 0))],
        out_specs=pl.BlockSpec((1, TQ, ATTN_WIDTH), cur),
        scratch_shapes=[pltpu.VMEM((TQ + 2 * H, 2 * KV_WIDTH), BF16),
                        pltpu.VMEM((TQ + 2 * H, N_KV_HEADS * LANES), BF16)],
        compiler_params=_params(2),
        name="attn",
    )(sink.astype(F32), q, kv, kv, kv, bias)


def _filter_kernel(z_ref, w1h, w1l, b1_ref, f1_ref, w2h, w2l, b2_ref, f2_ref, w3h, w3l, dec_ref,
                   k_ref, s_ref):
    i = pl.program_id(0)
    z = z_ref[...]
    h1 = jnp.sin(f1_ref[...] * (_dot3(z, w1h[...], w1l[...]) + b1_ref[...]))
    h2 = jnp.sin(f2_ref[...] * (_dot3(h1, w2h[...], w2l[...]) + b2_ref[...]))
    k = _dot3(h2, w3h[...], w3l[...])
    t = z[:, 0:1]
    k = k * (jnp.exp(-t * jnp.abs(dec_ref[...])) + WINDOW_SHIFT)
    k_ref[...] = k

    @pl.when(i == 0)
    def _():
        s_ref[...] = jnp.zeros_like(s_ref)

    s_ref[...] += jnp.sum(jnp.abs(k), axis=0, keepdims=True)


def _filter_embedding(L):
    t = np.linspace(0.0, 1.0, L, dtype=np.float32).astype(np.float64)[:, None]
    w = (2.0 * math.pi * np.arange(L, dtype=np.float32) / np.float32(L)).astype(np.float64)[:, None]
    bands = np.linspace(1e-4, FILTER_BANDS - 1, FILTER_BANDS, dtype=np.float32).astype(np.float64)[None, :]
    bw = (bands.astype(np.float32) * w.astype(np.float32)).astype(np.float64)
    z = np.concatenate([t, np.cos(bw), -np.sin(bw)], axis=-1)
    zp = np.zeros((L, LANES), np.float32)
    zp[:, :FILTER_EMB] = z.astype(np.float32)
    return jnp.asarray(zp)


def _pad2(a, r, c):
    return jnp.zeros((r, c), F32).at[:a.shape[0], :a.shape[1]].set(a.astype(F32))


def _filters(L, fw1, fb1, ff1, fw2, fb2, ff2, fw3, decay):
    H = LANES
    nf = fw3.shape[1]
    z = _filter_embedding(L)
    w1h, w1l = _split(_pad2(fw1, H, H))
    w2h, w2l = _split(_pad2(fw2, H, H))
    w3h, w3l = _split(_pad2(fw3, H, nf))
    b1 = _pad2(fb1[None], 1, H)
    f1 = _pad2(ff1[None], 1, H)
    b2 = _pad2(fb2[None], 1, H)
    f2 = _pad2(ff2[None], 1, H)
    tr = min(512, L)
    full = lambda r, c: pl.BlockSpec((r, c), lambda i: (0, 0))
    return pl.pallas_call(
        _filter_kernel,
        out_shape=[jax.ShapeDtypeStruct((L, nf), F32), jax.ShapeDtypeStruct((1, nf), F32)],
        grid=(L // tr,),
        in_specs=[pl.BlockSpec((tr, H), lambda i: (i, 0)),
                  full(H, H), full(H, H), full(1, H), full(1, H),
                  full(H, H), full(H, H), full(1, H), full(1, H),
                  full(H, nf), full(H, nf), full(1, nf)],
        out_specs=[pl.BlockSpec((tr, nf), lambda i: (i, 0)), full(1, nf)],
        compiler_params=_params(1),
        name="filter",
    )(z, w1h, w1l, b1, f1, w2h, w2l, b2, f2, w3h, w3l, decay.reshape(1, nf).astype(F32))


def _np_bf16(m64):
    return jnp.asarray(m64.astype(np.float32).astype(BF16))


def _dft_constants(L):
    N = 2 * L
    n2 = LANES
    n1 = N // n2
    h1 = n1 // 2
    k1 = np.arange(n1)[:, None]
    s1 = np.arange(h1)[None, :]
    ang = -2.0 * np.pi * ((k1 * s1) % n1) / n1
    wr, wi = np.cos(ang), np.sin(ang)
    w1_filt = np.block([[wr, wr], [wi, wi], [wr, -wr], [wi, -wi]])
    w1_cplx = np.block([[wr, -wi], [wi, wr]])
    vr, vi = wr.T / N, -wi.T / N
    w3 = np.block([[vr, -vi], [vi, vr]])
    k2 = np.arange(n2)[:, None]
    s2 = np.arange(n2)[None, :]
    a2 = -2.0 * np.pi * ((k2 * s2) % n2) / n2
    w2r, w2i = jnp.asarray(np.cos(a2), F32), jnp.asarray(np.sin(a2), F32)
    at = -2.0 * np.pi * ((np.arange(n1)[:, None] * s2) % N) / N
    twr, twi = jnp.asarray(np.cos(at), F32), jnp.asarray(np.sin(at), F32)
    mr = w2r[None] * twr[:, None, :] - w2i[None] * twi[:, None, :]
    mi = w2r[None] * twi[:, None, :] + w2i[None] * twr[:, None, :]
    fwd = jnp.concatenate([jnp.concatenate([mr, -mi], axis=2),
                           jnp.concatenate([mi, mr], axis=2)], axis=1)
    fwd = fwd.astype(BF16)
    return dict(n1=n1, w1_filt=_np_bf16(w1_filt), w1_cplx=_np_bf16(w1_cplx), w3=_np_bf16(w3),
                fwd=fwd)


SCH = 8


def _dft1_kernel(x_ref, w_ref, a_ref, *, n1):
    w = w_ref[...]
    for j in range(SCH):
        rhs = jnp.concatenate([x_ref[0, 0, :, j, :], x_ref[0, 1, :, j, :]], axis=0)
        res = jnp.dot(w, rhs.astype(BF16), preferred_element_type=F32)
        a_ref[0, :, j, :] = _pack2(res[:n1], res[n1:])


def _dft1_data(x, consts):
    B, L, C = x.shape
    n1 = consts["n1"]
    h1 = n1 // 2
    xv = x.reshape(B // 2, 2, h1, LANES, C)
    return pl.pallas_call(
        functools.partial(_dft1_kernel, n1=n1),
        out_shape=jax.ShapeDtypeStruct((B // 2, n1, LANES, C), F32),
        grid=(B // 2, LANES // SCH),
        in_specs=[pl.BlockSpec((1, 2, h1, SCH, C), lambda p, j: (p, 0, 0, j, 0)),
                  pl.BlockSpec((2 * n1, n1), lambda p, j: (0, 0))],
        out_specs=pl.BlockSpec((1, n1, SCH, C), lambda p, j: (p, 0, j, 0)),
        compiler_params=_params(2),
        name="dft1",
    )(xv, consts["w1_cplx"])


def _dft1f_kernel(x_ref, w_ref, a_ref, *, n1):
    C = HYENA_WIDTH
    w = w_ref[...]
    for j in range(SCH):
        rhs = jnp.concatenate([x_ref[:, j, :C], x_ref[:, j, C:]], axis=0)
        res = jnp.dot(w, rhs.astype(BF16), preferred_element_type=F32)
        a_ref[0, :, 0, j, :] = _pack2(res[:n1], res[n1:2 * n1])
        a_ref[0, :, 1, j, :] = _pack2(res[2 * n1:3 * n1], res[3 * n1:])


def _dft1_filter(kraw, consts):
    L, nf = kraw.shape
    C = HYENA_WIDTH
    n_ord = nf // (2 * C)
    n1 = consts["n1"]
    h1 = n1 // 2
    kv = kraw.reshape(h1, LANES, nf)
    return pl.pallas_call(
        functools.partial(_dft1f_kernel, n1=n1),
        out_shape=jax.ShapeDtypeStruct((n_ord, n1, 2, LANES, C), F32),
        grid=(n_ord, LANES // SCH),
        in_specs=[pl.BlockSpec((h1, SCH, 2 * C), lambda o, j: (0, j, o)),
                  pl.BlockSpec((4 * n1, n1), lambda o, j: (0, 0))],
        out_specs=pl.BlockSpec((1, n1, 2, SCH, C), lambda o, j: (o, 0, 0, j, 0)),
        compiler_params=_params(2),
        name="dft1f",
    )(kv, consts["w1_filt"])


KCH = 8


def _midf_kernel(a_ref, f_ref, inv_ref, b0_ref, h_ref):
    n2 = LANES
    sc = inv_ref[0]
    for k in range(KCH):
        p = jnp.concatenate(_unpack2(a_ref[0, k, :n2, :]), axis=0).astype(BF16)
        q = jnp.concatenate(_unpack2(a_ref[0, k, n2:, :]), axis=0).astype(BF16)
        h_re = jnp.dot(f_ref[k, :n2, :], p, preferred_element_type=F32)
        h_im = jnp.dot(f_ref[k, n2:, :], q, preferred_element_type=F32)
        h_ref[0, k] = _pack2((h_re - b0_ref[0]) * sc, h_im * sc)


def _filter_spectrum(af, inv_den, bwd0, consts):
    n_ord, n1, _, n2, C = af.shape
    a = af.reshape(n_ord, n1, 2 * n2, C)
    tab = pl.BlockSpec((KCH, 2 * n2, 2 * n2), lambda k, o: (k, 0, 0))
    vec = pl.BlockSpec((1, 1, C), lambda k, o: (o, 0, 0))
    return pl.pallas_call(
        _midf_kernel,
        out_shape=jax.ShapeDtypeStruct((n_ord, n1, n2, C), F32),
        grid=(n1 // KCH, n_ord),
        in_specs=[pl.BlockSpec((1, KCH, 2 * n2, C), lambda k, o: (o, k, 0, 0)), tab, vec, vec],
        out_specs=pl.BlockSpec((1, KCH, n2, C), lambda k, o: (o, k, 0, 0)),
        compiler_params=_params(2),
        name="midf",
    )(a, consts["fwd"], inv_den, bwd0)


def _mid_kernel(a_ref, f_ref, h_ref, b_ref):
    n2 = LANES
    for k in range(KCH):
        a = jnp.concatenate(_unpack2(a_ref[0, k]), axis=0).astype(BF16)
        x = jnp.dot(f_ref[k], a, preferred_element_type=F32)
        xr, xi = x[:n2], x[n2:]
        hr, hi = _unpack2(h_ref[0, k])
        y = jnp.concatenate([xr * hr - xi * hi, xr * hi + xi * hr], axis=0)
        b = lax.dot_general(f_ref[k], y.astype(BF16), (((0,), (0,)), ((), ())), preferred_element_type=F32)
        b_ref[0, k] = _pack2(b[:n2], b[n2:])


def _mid(a, hspec, order, consts):
    P, n1, n2, C = a.shape
    tab = pl.BlockSpec((KCH, 2 * n2, 2 * n2), lambda k, p: (k, 0, 0))
    return pl.pallas_call(
        _mid_kernel,
        out_shape=jax.ShapeDtypeStruct((P, n1, n2, C), F32),
        grid=(n1 // KCH, P),
        in_specs=[pl.BlockSpec((1, KCH, n2, C), lambda k, p: (p, k, 0, 0)),
                  tab,
                  pl.BlockSpec((1, KCH, n2, C), lambda k, p: (order, k, 0, 0))],
        out_specs=pl.BlockSpec((1, KCH, n2, C), lambda k, p: (p, k, 0, 0)),
        compiler_params=_params(2),
        name="mid",
    )(a, consts["fwd"], hspec)


def _dft3_kernel(b_ref, w_ref, v_ref, g_ref, skip_ref, *rest, h1, chain):
    if chain:
        w1_ref, z_ref, a_ref, slab_ref = rest
        w1 = w1_ref[...]
    else:
        z_ref, slab_ref = rest
    w = w_ref[...]
    skip = skip_ref[0]
    n1 = 2 * h1
    for j in range(SCH):
        slab_ref[...] = b_ref[0, :, j, :]
        rhs = jnp.concatenate(_unpack2(slab_ref[...]), axis=0)
        y = jnp.dot(w, rhs.astype(BF16), preferred_element_type=F32)
        z = [g_ref[0, r, :, j, :] * (y[r * h1:(r + 1) * h1] + v_ref[0, r, :, j, :] * skip) for r in range(2)]
        for r in range(2):
            z_ref[0, r, :, j, :] = z[r]
        if chain:
            res = jnp.dot(w1, jnp.concatenate(z, axis=0).astype(BF16), preferred_element_type=F32)
            a_ref[0, :, j, :] = _pack2(res[:n1], res[n1:])


def _dft3_gate(b5, v, gate, skip, consts, chain):
    P, n1, n2, C = b5.shape
    h1 = n1 // 2
    B, L, _ = v.shape
    five = lambda t: t.reshape(P, 2, h1, n2, C)
    dat = pl.BlockSpec((1, 2, h1, SCH, C), lambda p, j: (p, 0, 0, j, 0))
    packed = pl.BlockSpec((1, n1, SCH, C), lambda p, j: (p, 0, j, 0))
    in_specs = [packed, pl.BlockSpec((n1, 2 * n1), lambda p, j: (0, 0)), dat, dat,
                pl.BlockSpec((1, C), lambda p, j: (0, 0))]
    args = [b5, consts["w3"], five(v), five(gate), skip.reshape(1, C).astype(F32)]
    out_shape = [jax.ShapeDtypeStruct((P, 2, h1, n2, C), F32)]
    out_specs = [dat]
    if chain:
        in_specs.append(pl.BlockSpec((2 * n1, n1), lambda p, j: (0, 0)))
        args.append(consts["w1_cplx"])
        out_shape.append(jax.ShapeDtypeStruct((P, n1, n2, C), F32))
        out_specs.append(packed)
    outs = pl.pallas_call(
        functools.partial(_dft3_kernel, h1=h1, chain=chain),
        out_shape=out_shape,
        grid=(P, n2 // SCH),
        in_specs=in_specs,
        out_specs=out_specs,
        scratch_shapes=[pltpu.VMEM((n1, C), F32)],
        compiler_params=_params(2),
        name="dft3",
    )(*args)
    z = outs[0].reshape(B, L, C)
    return (z, outs[1]) if chain else (z, None)


def _hyena(v, x1, x2, fw1, fb1, ff1, fw2, fb2, ff2, fw3, decay, skip):
    B, L, C = v.shape
    consts = _dft_constants(L)
    kraw, ksum = _filters(L, fw1, fb1, ff1, fw2, fb2, ff2, fw3, decay)
    ks = ksum.reshape(2, 2, C)
    inv_den = (1.0 / (ks[:, 0] + ks[:, 1])).reshape(2, 1, C)
    bwd0 = kraw[0].reshape(2, 2, C)[:, 1].reshape(2, 1, C)
    hspec = _filter_spectrum(_dft1_filter(kraw, consts), inv_den, bwd0, consts)
    gates = (x1, x2)
    z, a5 = v, _dft1_data(v, consts)
    for o, gate in enumerate(gates):
        b5 = _mid(a5, hspec, o, consts)
        z, a5 = _dft3_gate(b5, z, gate, skip[o], consts, chain=o + 1 < len(gates))
    return z


def _merge_kernel(attn_ref, hy_ref, ga_ref, gh_ref, x_ref, mod_ref, wa_ref, wh_ref, wo_ref,
                  g1_ref, b1_ref, rwh_ref, rwl_ref, rb_ref, tri_ref,
                  x1_ref, h2_ref, route_ref, wts_ref, cnt_ref, carry_ref):
    @pl.when((pl.program_id(0) == 0) & (pl.program_id(1) == 0))
    def _():
        carry_ref[...] = jnp.zeros_like(carry_ref)

    logits = _merge_dense(attn_ref, hy_ref, ga_ref, gh_ref, x_ref, mod_ref, wa_ref, wh_ref, wo_ref,
                          g1_ref, b1_ref, rwh_ref, rwl_ref, rb_ref, x1_ref, h2_ref)
    _route_rows(logits, tri_ref, route_ref, wts_ref, carry_ref)
    cnt_ref[...] = carry_ref[...]


def _merge_dense(attn_ref, hy_ref, ga_ref, gh_ref, x_ref, mod_ref, wa_ref, wh_ref, wo_ref,
                 g1_ref, b1_ref, rwh_ref, rwl_ref, rb_ref, x1_ref, h2_ref):
    a = jnp.dot(attn_ref[0], wa_ref[...], preferred_element_type=F32)
    hy = jnp.dot(hy_ref[0].astype(BF16), wh_ref[...], preferred_element_type=F32)
    merged = ga_ref[0].astype(F32) * a + gh_ref[0].astype(F32) * hy
    y = jnp.dot(merged.astype(BF16), wo_ref[...], preferred_element_type=F32)
    gate1 = mod_ref[0, 2:3, :]
    shift2 = mod_ref[0, 3:4, :]
    scale2 = mod_ref[0, 4:5, :]
    x1 = _layer_norm(DN_ALPHA * x_ref[0] + gate1 * y, g1_ref[...], b1_ref[...])
    x1_ref[0] = x1
    h2 = x1 * (1.0 + scale2) + shift2
    half = h2.shape[1] // 2
    h2_ref[0] = _pack2(h2[:, :half], h2[:, half:])
    return _dot3(h2, rwh_ref[...], rwl_ref[...]) + rb_ref[...]


def _route_rows(logits, tri_ref, route_ref, wts_ref, carry_ref):
    tm = logits.shape[0]
    lane = lax.broadcasted_iota(jnp.int32, (tm, LANES), 1)
    lanef = lane.astype(F32)
    big = float(LANES)

    def first_lane(mask):
        return jnp.min(jnp.where(mask, lanef, big), axis=1, keepdims=True).astype(jnp.int32)

    gmask = lane < N_GROUPS
    gl = jnp.where(gmask, logits, NEG)
    gmax = jnp.max(gl, axis=1, keepdims=True)
    gidx = first_lane(gl == gmax)
    pg = 1.0 / jnp.sum(jnp.exp(gl - gmax), axis=1, keepdims=True)
    lo = ROUTE_OFF + gidx * EXPERTS_PER_GROUP
    emask = (lane >= lo) & (lane < lo + EXPERTS_PER_GROUP)
    el = jnp.where(emask, logits, NEG)
    v1 = jnp.max(el, axis=1, keepdims=True)
    i1 = first_lane(el == v1)
    el2 = jnp.where(emask & (lane != i1), logits, NEG)
    v2 = jnp.max(el2, axis=1, keepdims=True)
    i2 = first_lane(el2 == v2)
    e21 = jnp.exp(v2 - v1)
    w1 = pg / (1.0 + e21)
    w2 = pg * e21 / (1.0 + e21)

    sel1 = lane == i1
    sel2 = lane == i2
    onehot = jnp.where(sel1 | sel2, 1.0, 0.0)
    prefix = jnp.dot(tri_ref[...], onehot.astype(BF16), preferred_element_type=F32) + carry_ref[...]
    r1 = jnp.sum(jnp.where(sel1, prefix, 0.0), axis=1, keepdims=True)
    r2 = jnp.sum(jnp.where(sel2, prefix, 0.0), axis=1, keepdims=True)
    carry_ref[...] += jnp.sum(onehot, axis=0, keepdims=True)

    e1 = (i1 - ROUTE_OFF).astype(F32)
    e2 = (i2 - ROUTE_OFF).astype(F32)
    table = jnp.where(lane == 0, e1, jnp.where(lane == 1, e2, jnp.where(lane == 2, r1, jnp.where(lane == 3, r2, 0.0))))
    route_ref[...] = table.T[:SUBLANES].astype(jnp.int32)
    wts_ref[0] = jnp.where(lane == 0, w1, jnp.where(lane == 1, w2, 0.0))


def _merge(attn, hy, ga, gh, x, mod, w_attn_o, w_hy_o, w_out, ln1_g, ln1_b, rg_w, rg_b, re_w, re_b):
    B, S, D = x.shape
    tm = min(512, S)
    spare = LANES - N_GROUPS - N_EXPERTS
    rw = jnp.concatenate([rg_w, re_w, jnp.zeros((D, spare), F32)], axis=1)
    rb = jnp.concatenate([rg_b, re_b, jnp.zeros((spare,), F32)]).reshape(1, LANES)
    rwh, rwl = _split(rw)
    tri = (jnp.arange(tm)[:, None] > jnp.arange(tm)[None, :]).astype(BF16)
    row = lambda b, i: (b, i, 0)
    full = lambda r, c: pl.BlockSpec((r, c), lambda b, i: (0, 0))
    per_b = S // tm
    outs = [jax.ShapeDtypeStruct((B, S, D), F32), jax.ShapeDtypeStruct((B, S, D // 2), F32),
            jax.ShapeDtypeStruct((SUBLANES, B * S), jnp.int32), jax.ShapeDtypeStruct((B, S, LANES), F32),
            jax.ShapeDtypeStruct((1, LANES), F32)]
    return pl.pallas_call(
        _merge_kernel,
        out_shape=outs,
        grid=(B, per_b),
        in_specs=[pl.BlockSpec((1, tm, ATTN_WIDTH), row), pl.BlockSpec((1, tm, HYENA_WIDTH), row),
                  pl.BlockSpec((1, tm, D), row), pl.BlockSpec((1, tm, D), row), pl.BlockSpec((1, tm, D), row),
                  pl.BlockSpec((1, 6, D), lambda b, i: (b, 0, 0)),
                  full(ATTN_WIDTH, D), full(HYENA_WIDTH, D), full(D, D),
                  full(1, D), full(1, D), full(D, LANES), full(D, LANES), full(1, LANES), full(tm, tm)],
        out_specs=[pl.BlockSpec((1, tm, D), row), pl.BlockSpec((1, tm, D // 2), row),
                   pl.BlockSpec((SUBLANES, tm), lambda b, i: (0, b * per_b + i)),
                   pl.BlockSpec((1, tm, LANES), row), full(1, LANES)],
        scratch_shapes=[pltpu.VMEM((1, LANES), F32)],
        compiler_params=_params(2),
        name="merge",
    )(attn, hy, ga, gh, x, mod, w_attn_o.astype(BF16), w_hy_o.astype(BF16), w_out.astype(BF16),
      ln1_g.reshape(1, D), ln1_b.reshape(1, D), rwh, rwl, rb, tri)


SC_ROWS = 64


def _sc_workers():
    info = plsc.get_sparse_core_info()
    return info.num_cores, info.num_cores * info.num_subcores


def _sc_split(n):
    _, workers = _sc_workers()
    per_worker = n // workers
    chunks = per_worker // SC_ROWS
    assert per_worker * workers == n and chunks * SC_ROWS == per_worker and chunks % 2 == 0
    return workers, per_worker, chunks


def _sc_scatter_rows(src, idx0, idx1, n_out):
    n, width = src.shape
    nc, _ = _sc_workers()
    workers, per_worker, chunks = _sc_split(n)
    mesh = plsc.VectorSubcoreMesh(core_axis_name="c", subcore_axis_name="s")

    def body(src_hbm, i0_hbm, i1_hbm, out_hbm, i0_v, i1_v, rows_v, sem, ssem):
        wid = lax.axis_index("s") * nc + lax.axis_index("c")
        base = wid * per_worker
        pltpu.sync_copy(i0_hbm.at[wid], i0_v)
        pltpu.sync_copy(i1_hbm.at[wid], i1_v)

        def load(chunk, buf):
            return pltpu.make_async_copy(src_hbm.at[pl.ds(base + chunk * SC_ROWS, SC_ROWS)], rows_v.at[buf], sem)

        load(0, 0).start()

        @pl.loop(0, chunks, step=2)
        def _(c):
            for b in range(2):
                chunk = c + b
                load(chunk, b).wait()

                @pl.when(chunk + 1 < chunks)
                def _():
                    load(chunk + 1, 1 - b).start()

                first = pltpu.make_async_copy(rows_v.at[b], out_hbm.at[i0_v.at[chunk]], ssem)
                second = pltpu.make_async_copy(rows_v.at[b], out_hbm.at[i1_v.at[chunk]], ssem)
                first.start()
                second.start()
                first.wait()
                second.wait()

    shaped = lambda i: i.reshape(workers, chunks, SC_ROWS)
    return pl.kernel(
        body,
        out_type=jax.ShapeDtypeStruct((n_out, width), src.dtype),
        mesh=mesh,
        scratch_types=[pltpu.VMEM((chunks, SC_ROWS), jnp.int32),
                       pltpu.VMEM((chunks, SC_ROWS), jnp.int32),
                       pltpu.VMEM((2, SC_ROWS, width), src.dtype),
                       pltpu.SemaphoreType.DMA, pltpu.SemaphoreType.DMA],
        name="sc_scatter",
    )(src, shaped(idx0), shaped(idx1))


def _expert_kernel(first_ref, nb_ref, sz_ref, tot_ref, w1_ref, w3_ref, w2_ref, xb_ref, yb_ref,
                   xbuf, ybuf, c1_ref, c3_ref, c2_ref, lsem, ssem):
    e = pl.program_id(0)
    nb = nb_ref[e]
    first = first_ref[e]
    total = tot_ref[0]
    rows = xbuf.shape[1]

    def load(g, slot):
        src = xb_ref.at[pl.ds(pl.multiple_of(g * rows, rows), rows)]
        return pltpu.make_async_copy(src, xbuf.at[slot], lsem.at[slot])

    def store(g, slot):
        dst = yb_ref.at[pl.ds(pl.multiple_of(g * rows, rows), rows)]
        return pltpu.make_async_copy(ybuf.at[slot], dst, ssem.at[slot])

    @pl.when((e == 0) & (total > 0))
    def _():
        load(0, 0).start()

    @pl.when(nb > 0)
    def _():
        c1_ref[...] = w1_ref[0].astype(BF16)
        c3_ref[...] = w3_ref[0].astype(BF16)
        c2_ref[...] = w2_ref[0].astype(BF16)

        def block(j, carry):
            g = first + j
            slot = lax.rem(g, 2)
            load(g, slot).wait()

            @pl.when(g + 1 < total)
            def _():
                load(g + 1, 1 - slot).start()

            @pl.when(g >= 2)
            def _():
                store(g - 2, slot).wait()

            n_valid = sz_ref[e] - j * rows
            pieces = jnp.minimum((n_valid + EXPERT_ROWS - 1) // EXPERT_ROWS, rows // EXPERT_ROWS)

            def swiglu(n):
                rid = lax.broadcasted_iota(jnp.int32, (n, 1), 0)
                xa, xb = _unpack2(jnp.where(rid < n_valid, xbuf[slot, :n, :], 0.0))
                x = jnp.concatenate([xa, xb], axis=1).astype(BF16)
                a = jnp.dot(x, c1_ref[...], preferred_element_type=F32)
                gate = jnp.dot(x, c3_ref[...], preferred_element_type=F32)
                hmid = (a * _sigmoid(a) * gate).astype(BF16)
                y = jnp.dot(hmid, c2_ref[...], preferred_element_type=F32)
                half = y.shape[1] // 2
                ybuf[slot, :n, :] = _pack2(y[:, :half], y[:, half:])

            for q in range(1, rows // EXPERT_ROWS + 1):
                pl.when(pieces == q)(functools.partial(swiglu, q * EXPERT_ROWS))
            store(g, slot).start()
            return carry

        lax.fori_loop(0, nb, block, 0)

    @pl.when(e == pl.num_programs(0) - 1)
    def _():
        for back in (2, 1):
            @pl.when(total >= back)
            def _():
                g = total - back
                store(g, lax.rem(g, 2)).wait()


def _experts(xb, first_blk, n_blk, sizes, w1, w3, w2):
    P, W = xb.shape
    E, D, DE = w1.shape
    total = jnp.sum(n_blk, keepdims=True)
    wspec = lambda r, c: pl.BlockSpec((1, r, c), lambda e, *_: (e, 0, 0))
    grid_spec = pltpu.PrefetchScalarGridSpec(
        num_scalar_prefetch=4,
        grid=(E,),
        in_specs=[wspec(D, DE), wspec(D, DE), wspec(DE, D), pl.BlockSpec(memory_space=pl.ANY)],
        out_specs=pl.BlockSpec(memory_space=pl.ANY),
        scratch_shapes=[pltpu.VMEM((2, MOE_BLOCK, W), F32), pltpu.VMEM((2, MOE_BLOCK, W), F32),
                        pltpu.VMEM((D, DE), BF16), pltpu.VMEM((D, DE), BF16), pltpu.VMEM((DE, D), BF16),
                        pltpu.SemaphoreType.DMA((2,)), pltpu.SemaphoreType.DMA((2,))],
    )
    return pl.pallas_call(
        _expert_kernel,
        out_shape=jax.ShapeDtypeStruct((P, W), F32),
        grid_spec=grid_spec,
        compiler_params=_params(1),
        name="experts",
    )(first_blk, n_blk, sizes, total, w1, w3, w2, xb)


def _sc_gather_rows(table, idx):
    n, width = idx.shape[0], table.shape[1]
    nc, _ = _sc_workers()
    workers, per_worker, chunks = _sc_split(n)
    mesh = plsc.VectorSubcoreMesh(core_axis_name="c", subcore_axis_name="s")

    def body(table_hbm, idx_hbm, out_hbm, idx_v, rows_v, sem):
        wid = lax.axis_index("s") * nc + lax.axis_index("c")
        base = wid * per_worker
        pltpu.sync_copy(idx_hbm.at[wid], idx_v)

        def gather(chunk, buf):
            return pltpu.make_async_copy(table_hbm.at[idx_v.at[chunk]], rows_v.at[buf], sem)

        gather(0, 0).start()

        @pl.loop(0, chunks, step=2)
        def _(c):
            for b in range(2):
                chunk = c + b
                gather(chunk, b).wait()

                @pl.when(chunk + 1 < chunks)
                def _():
                    gather(chunk + 1, 1 - b).start()

                pltpu.sync_copy(rows_v.at[b], out_hbm.at[pl.ds(base + chunk * SC_ROWS, SC_ROWS)])

    return pl.kernel(
        body,
        out_type=jax.ShapeDtypeStruct((n, width), table.dtype),
        mesh=mesh,
        scratch_types=[pltpu.VMEM((chunks, SC_ROWS), jnp.int32),
                       pltpu.VMEM((2, SC_ROWS, width), table.dtype),
                       pltpu.SemaphoreType.DMA],
        name="sc_gather",
    )(table, idx.reshape(workers, chunks, SC_ROWS))


def _combine_dense_kernel(r0_ref, r1_ref, wts_ref, x1_ref, mod_ref, g_ref, b_ref, *rest):
    o_ref = rest[-1]
    w = wts_ref[...]
    y0 = jnp.concatenate(_unpack2(r0_ref[0]), axis=1)
    y1 = jnp.concatenate(_unpack2(r1_ref[0]), axis=1)
    y = w[:, 0:1] * y0 + w[:, 1:2] * y1
    gate2 = mod_ref[0, 5:6, :]
    o_ref[...] = _layer_norm(DN_ALPHA * x1_ref[...] + gate2 * y, g_ref[...], b_ref[...])


def _combine_dense(rows, wts, x1, mod, ln2_g, ln2_b, S, b, out):
    T, D = x1.shape
    tm = min(512, S)
    per_b = S // tm
    here = lambda i: (b * per_b + i, 0)
    in_specs = [pl.BlockSpec((1, tm, rows.shape[2]), lambda i: (0, i, 0)),
                pl.BlockSpec((1, tm, rows.shape[2]), lambda i: (1, i, 0)),
                pl.BlockSpec((tm, LANES), here),
                pl.BlockSpec((tm, D), here),
                pl.BlockSpec((1, 6, D), lambda i: (b, 0, 0)),
                pl.BlockSpec((1, D), lambda i: (0, 0)),
                pl.BlockSpec((1, D), lambda i: (0, 0))]
    args = [rows, rows, wts, x1, mod, ln2_g.reshape(1, D), ln2_b.reshape(1, D)]
    aliases = {}
    if out is not None:
        in_specs.append(pl.BlockSpec(memory_space=pl.ANY))
        aliases = {len(args): 0}
        args.append(out)
    return pl.pallas_call(
        _combine_dense_kernel,
        out_shape=jax.ShapeDtypeStruct((T, D), F32),
        grid=(per_b,),
        in_specs=in_specs,
        out_specs=pl.BlockSpec((tm, D), here),
        input_output_aliases=aliases,
        compiler_params=_params(1),
        name="combine",
    )(*args)


def _moe(h2, x1, route, wts, counts, mod, w1, w3, w2, ln2_g, ln2_b):
    B, S, D = x1.shape
    T = B * S
    P = 2 * T + N_EXPERTS * MOE_BLOCK
    sizes = counts[0, ROUTE_OFF:ROUTE_OFF + N_EXPERTS].astype(jnp.int32)
    n_blk = (sizes + MOE_BLOCK - 1) // MOE_BLOCK
    psizes = n_blk * MOE_BLOCK
    poffs = jnp.cumsum(psizes) - psizes
    sel = route[0:2, None, :] == jnp.arange(N_EXPERTS, dtype=jnp.int32)[None, :, None]
    dest = route[2:4] + jnp.sum(jnp.where(sel, poffs[None, :, None], 0), axis=1)
    xb = _sc_scatter_rows(h2.reshape(T, D // 2), dest[0], dest[1], P)
    yb = _experts(xb, poffs // MOE_BLOCK, n_blk, sizes, w1, w3, w2)
    out = None
    for b in range(B):
        slot_major = dest[:, b * S:(b + 1) * S].reshape(2 * S)
        rows = _sc_gather_rows(yb, slot_major).reshape(2, S, yb.shape[1])
        out = _combine_dense(rows, wts.reshape(T, LANES), x1.reshape(T, D), mod, ln2_g, ln2_b, S, b, out)
    return out.reshape(B, S, D)


def _layer(x, c, w_ada, b_ada, w_in, conv_w, conv_b, fw1, fb1, ff1, fw2, fb2, ff2, fw3, decay, skip,
           w_hy_o, w_attn_o, attn_sink, w_out, ln1_g, ln1_b, rg_w, rg_b, re_w, re_b, ew1, ew3, ew2,
           ln2_g, ln2_b):
    mod = _ada(c, w_ada, b_ada)
    q, kv, hv, hx1, hx2, ga, gh = _in_proj(x, mod, w_in, conv_w, conv_b)
    attn = _attention(q, kv, attn_sink)
    hy = _hyena(hv, hx1, hx2, fw1, fb1, ff1, fw2, fb2, ff2, fw3, decay, skip)
    x1, h2, route, wts, counts = _merge(attn, hy, ga, gh, x, mod, w_attn_o, w_hy_o, w_out,
                                        ln1_g, ln1_b, rg_w, rg_b, re_w, re_b)
    return _moe(h2, x1, route, wts, counts, mod, ew1, ew3, ew2, ln2_g, ln2_b)


def kernel(x, c, w_ada, b_ada, w_in, conv_w, conv_b, filt_w1, filt_b1, filt_freq1, filt_w2, filt_b2, filt_freq2, filt_w3, filt_decay, hy_skip, w_hy_o, w_attn_o, attn_sink, w_out, ln1_g, ln1_b, router_group_w, router_group_b, router_expert_w, router_expert_b, exp_w1, exp_w3, exp_w2, ln2_g, ln2_b):
    for l in range(w_ada.shape[0]):
        x = _layer(x, c, w_ada[l], b_ada[l], w_in[l], conv_w[l], conv_b[l], filt_w1[l], filt_b1[l],
                   filt_freq1[l], filt_w2[l], filt_b2[l], filt_freq2[l], filt_w3[l], filt_decay[l],
                   hy_skip[l], w_hy_o[l], w_attn_o[l], attn_sink[l], w_out[l], ln1_g[l], ln1_b[l],
                   router_group_w[l], router_group_b[l], router_expert_w[l], router_expert_b[l],
                   exp_w1[l], exp_w3[l], exp_w2[l], ln2_g[l], ln2_b[l])
    return x
```

```python
import functools
import math

import numpy as np
import jax
import jax.numpy as jnp
from jax import lax
from jax.experimental import pallas as pl
from jax.experimental.pallas import tpu as pltpu
from jax.experimental.pallas import tpu_sc as plsc

F32 = jnp.float32
BF16 = jnp.bfloat16

N_HEADS = 8
N_KV_HEADS = 2
HEAD_DIM = 64
ATTN_WIDTH = N_HEADS * HEAD_DIM
KV_WIDTH = N_KV_HEADS * HEAD_DIM
WINDOW = 128
HYENA_WIDTH = 512
FILTER_EMB = 33
FILTER_BANDS = (FILTER_EMB - 1) // 2
WINDOW_SHIFT = 0.05
N_GROUPS = 8
EXPERTS_PER_GROUP = 8
N_EXPERTS = N_GROUPS * EXPERTS_PER_GROUP
MOE_BLOCK = 512
EXPERT_ROWS = 128
LN_EPS = 1e-5
DEPTH = 1
DN_ALPHA = (2.0 * DEPTH) ** 0.25
NEG = -1e30

LANES = 128
SUBLANES = 8
ROUTE_OFF = N_GROUPS
VMEM_LIMIT = 56 * 1024 * 1024


def _params(n_axes, vmem=VMEM_LIMIT):
    return pltpu.CompilerParams(dimension_semantics=("arbitrary",) * n_axes, vmem_limit_bytes=vmem)


def _split(a):
    hi = a.astype(BF16)
    lo = (a - hi.astype(F32)).astype(BF16)
    return hi, lo


def _dot3(a, b_hi, b_lo):
    a_hi, a_lo = _split(a)
    acc = jnp.dot(a_hi, b_hi, preferred_element_type=F32)
    acc = acc + jnp.dot(a_hi, b_lo, preferred_element_type=F32)
    acc = acc + jnp.dot(a_lo, b_hi, preferred_element_type=F32)
    return acc


def _pack2(a, b):
    ia = lax.bitcast_convert_type(a.astype(BF16).astype(F32), jnp.int32)
    ib = lax.bitcast_convert_type(b.astype(BF16).astype(F32), jnp.int32)
    return lax.bitcast_convert_type(ia | lax.shift_right_logical(ib, 16), F32)


def _unpack2(p):
    p = lax.bitcast_convert_type(p, jnp.int32)
    a = lax.bitcast_convert_type(p & jnp.int32(-65536), F32)
    b = lax.bitcast_convert_type(lax.shift_left(p, 16), F32)
    return a, b


def _sigmoid(x):
    return 0.5 * jnp.tanh(0.5 * x) + 0.5


def _layer_norm(r, g, b):
    mu = jnp.mean(r, axis=-1, keepdims=True)
    d = r - mu
    var = jnp.mean(d * d, axis=-1, keepdims=True)
    return d * lax.rsqrt(var + LN_EPS) * g + b


def _ada_kernel(c_ref, w_ref, b_ref, o_ref):
    c = c_ref[...]
    s = c * _sigmoid(c)
    wh, wl = _split(w_ref[...])
    o_ref[...] = _dot3(s, wh, wl) + b_ref[...]


def _ada(c, w_ada, b_ada):
    B, D = c.shape
    n_out = w_ada.shape[1]
    rows = SUBLANES
    cp = jnp.pad(c, ((0, rows - B), (0, 0)))
    tn = 1024
    out = pl.pallas_call(
        _ada_kernel,
        out_shape=jax.ShapeDtypeStruct((rows, n_out), F32),
        grid=(n_out // tn,),
        in_specs=[pl.BlockSpec((rows, D), lambda j: (0, 0)),
                  pl.BlockSpec((D, tn), lambda j: (0, j)),
                  pl.BlockSpec((1, tn), lambda j: (0, j))],
        out_specs=pl.BlockSpec((rows, tn), lambda j: (0, j)),
        compiler_params=_params(1),
        name="ada",
    )(cp, w_ada, b_ada.reshape(1, n_out))
    return out[:B].reshape(B, 6, D)


def _inproj_kernel(x_ref, xp_ref, xn_ref, mod_ref, w_ref, cw_ref, cb_ref,
                   q_ref, kv_ref, v_ref, x1_ref, x2_ref, ga_ref, gh_ref):
    i = pl.program_id(1)
    n = pl.num_programs(1)
    C = HYENA_WIDTH
    x = x_ref[0]
    tm, D = x.shape
    shift = mod_ref[0, 0:1, :]
    scale = mod_ref[0, 1:2, :]
    h = (x * (1.0 + scale) + shift).astype(BF16)

    def seg(lo, hi):
        return jnp.dot(h, w_ref[:, lo:hi], preferred_element_type=F32)

    o_q = 0
    o_kv = o_q + ATTN_WIDTH
    o_hy = o_kv + 2 * KV_WIDTH
    o_ga = o_hy + 3 * C
    o_gh = o_ga + D
    ga_ref[0] = _sigmoid(seg(o_ga, o_ga + D)).astype(BF16)
    gh_ref[0] = _sigmoid(seg(o_gh, o_gh + D)).astype(BF16)

    u = seg(o_hy, o_hy + 3 * C)
    xe = jnp.concatenate([xp_ref[0], xn_ref[0]], axis=0)
    he = (xe * (1.0 + scale) + shift).astype(BF16)
    ue = jnp.dot(he, w_ref[:, o_hy:o_hy + 3 * C], preferred_element_type=F32)
    prow = jnp.where(i > 0, ue[SUBLANES - 1:SUBLANES], 0.0)
    nrow = jnp.where(i < n - 1, ue[SUBLANES:SUBLANES + 1], 0.0)
    rid = lax.broadcasted_iota(jnp.int32, (tm, 1), 0)
    up = jnp.where(rid == 0, prow, pltpu.roll(u, 1, 0))
    dn = jnp.where(rid == tm - 1, nrow, pltpu.roll(u, tm - 1, 0))
    conv = cw_ref[0:1, :] * up + cw_ref[1:2, :] * u + cw_ref[2:3, :] * dn + cb_ref[...]
    v_ref[0] = conv[:, :C]
    x1_ref[0] = conv[:, C:2 * C]
    x2_ref[0] = conv[:, 2 * C:]

    q_ref[0] = (seg(o_q, o_q + ATTN_WIDTH) * (HEAD_DIM ** -0.5)).astype(BF16)
    kv_ref[0] = seg(o_kv, o_kv + 2 * KV_WIDTH).astype(BF16)


def _in_proj(x, mod, w_in, conv_w, conv_b):
    B, S, D = x.shape
    C = HYENA_WIDTH
    tm = min(1024, S)
    r8 = tm // SUBLANES
    nb8 = S // SUBLANES
    wb = w_in.astype(BF16)
    nw = wb.shape[1]
    row = lambda b, i: (b, i, 0)
    shapes = [(ATTN_WIDTH, BF16), (2 * KV_WIDTH, BF16), (C, F32), (C, F32), (C, F32), (D, BF16), (D, BF16)]
    return pl.pallas_call(
        _inproj_kernel,
        out_shape=[jax.ShapeDtypeStruct((B, S, w), dt) for w, dt in shapes],
        grid=(B, S // tm),
        in_specs=[pl.BlockSpec((1, tm, D), row),
                  pl.BlockSpec((1, SUBLANES, D), lambda b, i: (b, jnp.maximum(i * r8 - 1, 0), 0)),
                  pl.BlockSpec((1, SUBLANES, D), lambda b, i: (b, jnp.minimum((i + 1) * r8, nb8 - 1), 0)),
                  pl.BlockSpec((1, 6, D), lambda b, i: (b, 0, 0)),
                  pl.BlockSpec((D, nw), lambda b, i: (0, 0)),
                  pl.BlockSpec((3, 3 * C), lambda b, i: (0, 0)),
                  pl.BlockSpec((1, 3 * C), lambda b, i: (0, 0))],
        out_specs=[pl.BlockSpec((1, tm, w), row) for w, _ in shapes],
        compiler_params=_params(2),
        name="in_proj",
    )(x, x, x, mod, wb, conv_w.astype(F32), conv_b.reshape(1, 3 * C).astype(F32))


ATT_TQ = 512
ATT_QB = 128
ATT_STACK = 4


def _attn_kernel(sink_ref, q_ref, kvp_ref, kvc_ref, kvn_ref, bias_ref, o_ref, kv_scr, vx_scr, *, seq_len):
    i = pl.program_id(1)
    H = WINDOW
    TQ = q_ref.shape[1]
    Q = min(ATT_QB, TQ)
    band = Q + 2 * H
    G = N_HEADS // N_KV_HEADS
    kv_scr[0:H] = kvp_ref[0]
    kv_scr[H:H + TQ] = kvc_ref[0]
    kv_scr[H + TQ:] = kvn_ref[0]
    for kv in range(N_KV_HEADS):
        vx_scr[:, kv * LANES:kv * LANES + HEAD_DIM] = kv_scr[:, KV_WIDTH + kv * HEAD_DIM:KV_WIDTH + (kv + 1) * HEAD_DIM]
        vx_scr[:, kv * LANES + HEAD_DIM:(kv + 1) * LANES] = jnp.ones((TQ + 2 * H, LANES - HEAD_DIM), BF16)
    col = lax.broadcasted_iota(jnp.int32, (1, band), 1)
    rhead = lax.broadcasted_iota(jnp.int32, (ATT_STACK * Q, 1), 0) // Q
    for j in range(TQ // Q):
        kpos = i * TQ + j * Q - H + col
        colbias = jnp.where((kpos >= 0) & (kpos < seq_len), 0.0, NEG)
        for kv in range(N_KV_HEADS):
            kk = kv_scr[j * Q:j * Q + band, kv * HEAD_DIM:(kv + 1) * HEAD_DIM]
            vx = vx_scr[j * Q:j * Q + band, kv * LANES:(kv + 1) * LANES]
            for sub in range(G // ATT_STACK):
                first = sub * ATT_STACK
                heads = [kv * G + first + g for g in range(ATT_STACK)]
                qg = jnp.concatenate([q_ref[0, j * Q:(j + 1) * Q, h * HEAD_DIM:(h + 1) * HEAD_DIM] for h in heads],
                                     axis=0)
                s = lax.dot_general(qg, kk, (((1,), (1,)), ((), ())), preferred_element_type=F32)
                s = s + bias_ref[kv, first * Q:(first + ATT_STACK) * Q, :] + colbias
                snk = sink_ref[heads[-1]]
                for g in range(ATT_STACK - 2, -1, -1):
                    snk = jnp.where(rhead == g, sink_ref[heads[g]], snk)
                m = jnp.maximum(jnp.max(s, axis=1, keepdims=True), snk)
                p = jnp.exp(s - m).astype(BF16)
                ox = jnp.dot(p, vx, preferred_element_type=F32)
                den = ox[:, HEAD_DIM:HEAD_DIM + 1] + jnp.exp(snk - m)
                o = ox[:, :HEAD_DIM] / den
                for g, h in enumerate(heads):
                    o_ref[0, j * Q:(j + 1) * Q, h * HEAD_DIM:(h + 1) * HEAD_DIM] = o[g * Q:(g + 1) * Q].astype(BF16)


def _attention(q, kv, sink):
    B, S, _ = q.shape
    H = WINDOW
    TQ = min(ATT_TQ, S)
    Q = min(ATT_QB, TQ)
    r = TQ // H
    nq = S // H
    G = N_HEADS // N_KV_HEADS
    assert G % ATT_STACK == 0
    a = jnp.arange(Q)[:, None]
    j = jnp.arange(Q + 2 * H)[None, :]
    rel = jnp.abs(j - H - a).astype(F32)
    slopes = 2.0 ** (-8.0 * jnp.arange(1, N_HEADS + 1, dtype=F32) / N_HEADS)
    bias = jnp.where(rel[None] <= WINDOW, -slopes[:, None, None] * rel[None], NEG).astype(F32)
    bias = bias.reshape(N_KV_HEADS, G * Q, Q + 2 * H)
    cur = lambda b, i: (b, i, 0)
    return pl.pallas_call(
        functools.partial(_attn_kernel, seq_len=S),
        out_shape=jax.ShapeDtypeStruct((B, S, ATTN_WIDTH), BF16),
        grid=(B, S // TQ),
        in_specs=[pl.BlockSpec(memory_space=pltpu.SMEM),
                  pl.BlockSpec((1, TQ, ATTN_WIDTH), cur),
                  pl.BlockSpec((1, H, 2 * KV_WIDTH), lambda b, i: (b, jnp.maximum(i * r - 1, 0), 0)),
                  pl.BlockSpec((1, TQ, 2 * KV_WIDTH), cur),
                  pl.BlockSpec((1, H, 2 * KV_WIDTH), lambda b, i: (b, jnp.minimum((i + 1) * r, nq - 1), 0)),
                  pl.BlockSpec((N_KV_HEADS, G * Q, Q + 2 * H), lambda b, i: (0, 0, 0))],
        out_specs=pl.BlockSpec((1, TQ, ATTN_WIDTH), cur),
        scratch_shapes=[pltpu.VMEM((TQ + 2 * H, 2 * KV_WIDTH), BF16),
                        pltpu.VMEM((TQ + 2 * H, N_KV_HEADS * LANES), BF16)],
        compiler_params=_params(2),
        name="attn",
    )(sink.astype(F32), q, kv, kv, kv, bias)


def _filter_kernel(z_ref, w1h, w1l, b1_ref, f1_ref, w2h, w2l, b2_ref, f2_ref, w3h, w3l, dec_ref,
                   k_ref, s_ref):
    i = pl.program_id(0)
    z = z_ref[...]
    h1 = jnp.sin(f1_ref[...] * (_dot3(z, w1h[...], w1l[...]) + b1_ref[...]))
    h2 = jnp.sin(f2_ref[...] * (_dot3(h1, w2h[...], w2l[...]) + b2_ref[...]))
    k = _dot3(h2, w3h[...], w3l[...])
    t = z[:, 0:1]
    k = k * (jnp.exp(-t * jnp.abs(dec_ref[...])) + WINDOW_SHIFT)
    k_ref[...] = k

    @pl.when(i == 0)
    def _():
        s_ref[...] = jnp.zeros_like(s_ref)

    s_ref[...] += jnp.sum(jnp.abs(k), axis=0, keepdims=True)


def _filter_embedding(L):
    t = np.linspace(0.0, 1.0, L, dtype=np.float32).astype(np.float64)[:, None]
    w = (2.0 * math.pi * np.arange(L, dtype=np.float32) / np.float32(L)).astype(np.float64)[:, None]
    bands = np.linspace(1e-4, FILTER_BANDS - 1, FILTER_BANDS, dtype=np.float32).astype(np.float64)[None, :]
    bw = (bands.astype(np.float32) * w.astype(np.float32)).astype(np.float64)
    z = np.concatenate([t, np.cos(bw), -np.sin(bw)], axis=-1)
    zp = np.zeros((L, LANES), np.float32)
    zp[:, :FILTER_EMB] = z.astype(np.float32)
    return jnp.asarray(zp)


def _pad2(a, r, c):
    return jnp.zeros((r, c), F32).at[:a.shape[0], :a.shape[1]].set(a.astype(F32))


def _filters(L, fw1, fb1, ff1, fw2, fb2, ff2, fw3, decay):
    H = LANES
    nf = fw3.shape[1]
    z = _filter_embedding(L)
    w1h, w1l = _split(_pad2(fw1, H, H))
    w2h, w2l = _split(_pad2(fw2, H, H))
    w3h, w3l = _split(_pad2(fw3, H, nf))
    b1 = _pad2(fb1[None], 1, H)
    f1 = _pad2(ff1[None], 1, H)
    b2 = _pad2(fb2[None], 1, H)
    f2 = _pad2(ff2[None], 1, H)
    tr = min(512, L)
    full = lambda r, c: pl.BlockSpec((r, c), lambda i: (0, 0))
    return pl.pallas_call(
        _filter_kernel,
        out_shape=[jax.ShapeDtypeStruct((L, nf), F32), jax.ShapeDtypeStruct((1, nf), F32)],
        grid=(L // tr,),
        in_specs=[pl.BlockSpec((tr, H), lambda i: (i, 0)),
                  full(H, H), full(H, H), full(1, H), full(1, H),
                  full(H, H), full(H, H), full(1, H), full(1, H),
                  full(H, nf), full(H, nf), full(1, nf)],
        out_specs=[pl.BlockSpec((tr, nf), lambda i: (i, 0)), full(1, nf)],
        compiler_params=_params(1),
        name="filter",
    )(z, w1h, w1l, b1, f1, w2h, w2l, b2, f2, w3h, w3l, decay.reshape(1, nf).astype(F32))


def _np_bf16(m64):
    return jnp.asarray(m64.astype(np.float32).astype(BF16))


def _dft_constants(L):
    N = 2 * L
    n2 = LANES
    n1 = N // n2
    h1 = n1 // 2
    k1 = np.arange(n1)[:, None]
    s1 = np.arange(h1)[None, :]
    ang = -2.0 * np.pi * ((k1 * s1) % n1) / n1
    wr, wi = np.cos(ang), np.sin(ang)
    w1_filt = np.block([[wr, wr], [wi, wi], [wr, -wr], [wi, -wi]])
    w1_cplx = np.block([[wr, -wi], [wi, wr]])
    vr, vi = wr.T / N, -wi.T / N
    w3 = np.block([[vr, -vi], [vi, vr]])
    k2 = np.arange(n2)[:, None]
    s2 = np.arange(n2)[None, :]
    a2 = -2.0 * np.pi * ((k2 * s2) % n2) / n2
    w2r, w2i = jnp.asarray(np.cos(a2), F32), jnp.asarray(np.sin(a2), F32)
    at = -2.0 * np.pi * ((np.arange(n1)[:, None] * s2) % N) / N
    twr, twi = jnp.asarray(np.cos(at), F32), jnp.asarray(np.sin(at), F32)
    mr = w2r[None] * twr[:, None, :] - w2i[None] * twi[:, None, :]
    mi = w2r[None] * twi[:, None, :] + w2i[None] * twr[:, None, :]
    fwd = jnp.concatenate([jnp.concatenate([mr, -mi], axis=2),
                           jnp.concatenate([mi, mr], axis=2)], axis=1)
    fwd = fwd.astype(BF16)
    return dict(n1=n1, w1_filt=_np_bf16(w1_filt), w1_cplx=_np_bf16(w1_cplx), w3=_np_bf16(w3),
                fwd=fwd)


SCH = 16


def _dft1_kernel(x_ref, w_ref, a_ref, *, n1):
    w = w_ref[...]
    for j in range(SCH):
        rhs = jnp.concatenate([x_ref[0, 0, :, j, :], x_ref[0, 1, :, j, :]], axis=0)
        res = jnp.dot(w, rhs.astype(BF16), preferred_element_type=F32)
        a_ref[0, :, j, :] = _pack2(res[:n1], res[n1:])


def _dft1_data(x, consts):
    B, L, C = x.shape
    n1 = consts["n1"]
    h1 = n1 // 2
    xv = x.reshape(B // 2, 2, h1, LANES, C)
    return pl.pallas_call(
        functools.partial(_dft1_kernel, n1=n1),
        out_shape=jax.ShapeDtypeStruct((B // 2, n1, LANES, C), F32),
        grid=(B // 2, LANES // SCH),
        in_specs=[pl.BlockSpec((1, 2, h1, SCH, C), lambda p, j: (p, 0, 0, j, 0)),
                  pl.BlockSpec((2 * n1, n1), lambda p, j: (0, 0))],
        out_specs=pl.BlockSpec((1, n1, SCH, C), lambda p, j: (p, 0, j, 0)),
        compiler_params=_params(2),
        name="dft1",
    )(xv, consts["w1_cplx"])


def _dft1f_kernel(x_ref, w_ref, a_ref, *, n1):
    C = HYENA_WIDTH
    w = w_ref[...]
    for j in range(SCH):
        rhs = jnp.concatenate([x_ref[:, j, :C], x_ref[:, j, C:]], axis=0)
        res = jnp.dot(w, rhs.astype(BF16), preferred_element_type=F32)
        a_ref[0, :, 0, j, :] = _pack2(res[:n1], res[n1:2 * n1])
        a_ref[0, :, 1, j, :] = _pack2(res[2 * n1:3 * n1], res[3 * n1:])


def _dft1_filter(kraw, consts):
    L, nf = kraw.shape
    C = HYENA_WIDTH
    n_ord = nf // (2 * C)
    n1 = consts["n1"]
    h1 = n1 // 2
    kv = kraw.reshape(h1, LANES, nf)
    return pl.pallas_call(
        functools.partial(_dft1f_kernel, n1=n1),
        out_shape=jax.ShapeDtypeStruct((n_ord, n1, 2, LANES, C), F32),
        grid=(n_ord, LANES // SCH),
        in_specs=[pl.BlockSpec((h1, SCH, 2 * C), lambda o, j: (0, j, o)),
                  pl.BlockSpec((4 * n1, n1), lambda o, j: (0, 0))],
        out_specs=pl.BlockSpec((1, n1, 2, SCH, C), lambda o, j: (o, 0, 0, j, 0)),
        compiler_params=_params(2),
        name="dft1f",
    )(kv, consts["w1_filt"])


KCH = 16


def _midf_kernel(a_ref, f_ref, inv_ref, b0_ref, h_ref):
    n2 = LANES
    sc = inv_ref[0]
    for k in range(KCH):
        p = jnp.concatenate(_unpack2(a_ref[0, k, :n2, :]), axis=0).astype(BF16)
        q = jnp.concatenate(_unpack2(a_ref[0, k, n2:, :]), axis=0).astype(BF16)
        h_re = jnp.dot(f_ref[k, :n2, :], p, preferred_element_type=F32)
        h_im = jnp.dot(f_ref[k, n2:, :], q, preferred_element_type=F32)
        h_ref[0, k] = _pack2((h_re - b0_ref[0]) * sc, h_im * sc)


def _filter_spectrum(af, inv_den, bwd0, consts):
    n_ord, n1, _, n2, C = af.shape
    a = af.reshape(n_ord, n1, 2 * n2, C)
    tab = pl.BlockSpec((KCH, 2 * n2, 2 * n2), lambda k, o: (k, 0, 0))
    vec = pl.BlockSpec((1, 1, C), lambda k, o: (o, 0, 0))
    return pl.pallas_call(
        _midf_kernel,
        out_shape=jax.ShapeDtypeStruct((n_ord, n1, n2, C), F32),
        grid=(n1 // KCH, n_ord),
        in_specs=[pl.BlockSpec((1, KCH, 2 * n2, C), lambda k, o: (o, k, 0, 0)), tab, vec, vec],
        out_specs=pl.BlockSpec((1, KCH, n2, C), lambda k, o: (o, k, 0, 0)),
        compiler_params=_params(2),
        name="midf",
    )(a, consts["fwd"], inv_den, bwd0)


def _mid_kernel(a_ref, f_ref, h_ref, b_ref):
    n2 = LANES
    for k in range(KCH):
        a = jnp.concatenate(_unpack2(a_ref[0, k]), axis=0).astype(BF16)
        x = jnp.dot(f_ref[k], a, preferred_element_type=F32)
        xr, xi = x[:n2], x[n2:]
        hr, hi = _unpack2(h_ref[0, k])
        y = jnp.concatenate([xr * hr - xi * hi, xr * hi + xi * hr], axis=0)
        b = lax.dot_general(f_ref[k], y.astype(BF16), (((0,), (0,)), ((), ())), preferred_element_type=F32)
        b_ref[0, k] = _pack2(b[:n2], b[n2:])


def _mid(a, hspec, order, consts):
    P, n1, n2, C = a.shape
    tab = pl.BlockSpec((KCH, 2 * n2, 2 * n2), lambda k, p: (k, 0, 0))
    return pl.pallas_call(
        _mid_kernel,
        out_shape=jax.ShapeDtypeStruct((P, n1, n2, C), F32),
        grid=(n1 // KCH, P),
        in_specs=[pl.BlockSpec((1, KCH, n2, C), lambda k, p: (p, k, 0, 0)),
                  tab,
                  pl.BlockSpec((1, KCH, n2, C), lambda k, p: (order, k, 0, 0))],
        out_specs=pl.BlockSpec((1, KCH, n2, C), lambda k, p: (p, k, 0, 0)),
        compiler_params=_params(2),
        name="mid",
    )(a, consts["fwd"], hspec)


def _dft3_kernel(b_ref, w_ref, v_ref, g_ref, skip_ref, *rest, h1, chain):
    if chain:
        w1_ref, z_ref, a_ref, slab_ref = rest
        w1 = w1_ref[...]
    else:
        z_ref, slab_ref = rest
    w = w_ref[...]
    skip = skip_ref[0]
    n1 = 2 * h1
    for j in range(SCH):
        slab_ref[...] = b_ref[0, :, j, :]
        rhs = jnp.concatenate(_unpack2(slab_ref[...]), axis=0)
        y = jnp.dot(w, rhs.astype(BF16), preferred_element_type=F32)
        z = [g_ref[0, r, :, j, :] * (y[r * h1:(r + 1) * h1] + v_ref[0, r, :, j, :] * skip) for r in range(2)]
        for r in range(2):
            z_ref[0, r, :, j, :] = z[r]
        if chain:
            res = jnp.dot(w1, jnp.concatenate(z, axis=0).astype(BF16), preferred_element_type=F32)
            a_ref[0, :, j, :] = _pack2(res[:n1], res[n1:])


def _dft3_gate(b5, v, gate, skip, consts, chain):
    P, n1, n2, C = b5.shape
    h1 = n1 // 2
    B, L, _ = v.shape
    five = lambda t: t.reshape(P, 2, h1, n2, C)
    dat = pl.BlockSpec((1, 2, h1, SCH, C), lambda p, j: (p, 0, 0, j, 0))
    packed = pl.BlockSpec((1, n1, SCH, C), lambda p, j: (p, 0, j, 0))
    in_specs = [packed, pl.BlockSpec((n1, 2 * n1), lambda p, j: (0, 0)), dat, dat,
                pl.BlockSpec((1, C), lambda p, j: (0, 0))]
    args = [b5, consts["w3"], five(v), five(gate), skip.reshape(1, C).astype(F32)]
    out_shape = [jax.ShapeDtypeStruct((P, 2, h1, n2, C), F32)]
    out_specs = [dat]
    if chain:
        in_specs.append(pl.BlockSpec((2 * n1, n1), lambda p, j: (0, 0)))
        args.append(consts["w1_cplx"])
        out_shape.append(jax.ShapeDtypeStruct((P, n1, n2, C), F32))
        out_specs.append(packed)
    outs = pl.pallas_call(
        functools.partial(_dft3_kernel, h1=h1, chain=chain),
        out_shape=out_shape,
        grid=(P, n2 // SCH),
        in_specs=in_specs,
        out_specs=out_specs,
        scratch_shapes=[pltpu.VMEM((n1, C), F32)],
        compiler_params=_params(2),
        name="dft3",
    )(*args)
    z = outs[0].reshape(B, L, C)
    return (z, outs[1]) if chain else (z, None)


def _hyena(v, x1, x2, fw1, fb1, ff1, fw2, fb2, ff2, fw3, decay, skip):
    B, L, C = v.shape
    consts = _dft_constants(L)
    kraw, ksum = _filters(L, fw1, fb1, ff1, fw2, fb2, ff2, fw3, decay)
    ks = ksum.reshape(2, 2, C)
    inv_den = (1.0 / (ks[:, 0] + ks[:, 1])).reshape(2, 1, C)
    bwd0 = kraw[0].reshape(2, 2, C)[:, 1].reshape(2, 1, C)
    hspec = _filter_spectrum(_dft1_filter(kraw, consts), inv_den, bwd0, consts)
    gates = (x1, x2)
    z, a5 = v, _dft1_data(v, consts)
    for o, gate in enumerate(gates):
        b5 = _mid(a5, hspec, o, consts)
        z, a5 = _dft3_gate(b5, z, gate, skip[o], consts, chain=o + 1 < len(gates))
    return z


def _merge_kernel(attn_ref, hy_ref, ga_ref, gh_ref, x_ref, mod_ref, wa_ref, wh_ref, wo_ref,
                  g1_ref, b1_ref, rwh_ref, rwl_ref, rb_ref, tri_ref,
                  x1_ref, h2_ref, route_ref, wts_ref, cnt_ref, carry_ref):
    @pl.when((pl.program_id(0) == 0) & (pl.program_id(1) == 0))
    def _():
        carry_ref[...] = jnp.zeros_like(carry_ref)

    logits = _merge_dense(attn_ref, hy_ref, ga_ref, gh_ref, x_ref, mod_ref, wa_ref, wh_ref, wo_ref,
                          g1_ref, b1_ref, rwh_ref, rwl_ref, rb_ref, x1_ref, h2_ref)
    _route_rows(logits, tri_ref, route_ref, wts_ref, carry_ref)
    cnt_ref[...] = carry_ref[...]


def _merge_dense(attn_ref, hy_ref, ga_ref, gh_ref, x_ref, mod_ref, wa_ref, wh_ref, wo_ref,
                 g1_ref, b1_ref, rwh_ref, rwl_ref, rb_ref, x1_ref, h2_ref):
    a = jnp.dot(attn_ref[0], wa_ref[...], preferred_element_type=F32)
    hy = jnp.dot(hy_ref[0].astype(BF16), wh_ref[...], preferred_element_type=F32)
    merged = ga_ref[0].astype(F32) * a + gh_ref[0].astype(F32) * hy
    y = jnp.dot(merged.astype(BF16), wo_ref[...], preferred_element_type=F32)
    gate1 = mod_ref[0, 2:3, :]
    shift2 = mod_ref[0, 3:4, :]
    scale2 = mod_ref[0, 4:5, :]
    x1 = _layer_norm(DN_ALPHA * x_ref[0] + gate1 * y, g1_ref[...], b1_ref[...])
    x1_ref[0] = x1
    h2 = x1 * (1.0 + scale2) + shift2
    half = h2.shape[1] // 2
    h2_ref[0] = _pack2(h2[:, :half], h2[:, half:])
    return _dot3(h2, rwh_ref[...], rwl_ref[...]) + rb_ref[...]


def _route_rows(logits, tri_ref, route_ref, wts_ref, carry_ref):
    tm = logits.shape[0]
    lane = lax.broadcasted_iota(jnp.int32, (tm, LANES), 1)
    lanef = lane.astype(F32)
    big = float(LANES)

    def first_lane(mask):
        return jnp.min(jnp.where(mask, lanef, big), axis=1, keepdims=True).astype(jnp.int32)

    gmask = lane < N_GROUPS
    gl = jnp.where(gmask, logits, NEG)
    gmax = jnp.max(gl, axis=1, keepdims=True)
    gidx = first_lane(gl == gmax)
    pg = 1.0 / jnp.sum(jnp.exp(gl - gmax), axis=1, keepdims=True)
    lo = ROUTE_OFF + gidx * EXPERTS_PER_GROUP
    emask = (lane >= lo) & (lane < lo + EXPERTS_PER_GROUP)
    el = jnp.where(emask, logits, NEG)
    v1 = jnp.max(el, axis=1, keepdims=True)
    i1 = first_lane(el == v1)
    el2 = jnp.where(emask & (lane != i1), logits, NEG)
    v2 = jnp.max(el2, axis=1, keepdims=True)
    i2 = first_lane(el2 == v2)
    e21 = jnp.exp(v2 - v1)
    w1 = pg / (1.0 + e21)
    w2 = pg * e21 / (1.0 + e21)

    sel1 = lane == i1
    sel2 = lane == i2
    onehot = jnp.where(sel1 | sel2, 1.0, 0.0)
    prefix = jnp.dot(tri_ref[...], onehot.astype(BF16), preferred_element_type=F32) + carry_ref[...]
    r1 = jnp.sum(jnp.where(sel1, prefix, 0.0), axis=1, keepdims=True)
    r2 = jnp.sum(jnp.where(sel2, prefix, 0.0), axis=1, keepdims=True)
    carry_ref[...] += jnp.sum(onehot, axis=0, keepdims=True)

    e1 = (i1 - ROUTE_OFF).astype(F32)
    e2 = (i2 - ROUTE_OFF).astype(F32)
    table = jnp.where(lane == 0, e1, jnp.where(lane == 1, e2, jnp.where(lane == 2, r1, jnp.where(lane == 3, r2, 0.0))))
    route_ref[...] = table.T[:SUBLANES].astype(jnp.int32)
    wts_ref[0] = jnp.where(lane == 0, w1, jnp.where(lane == 1, w2, 0.0))


def _merge(attn, hy, ga, gh, x, mod, w_attn_o, w_hy_o, w_out, ln1_g, ln1_b, rg_w, rg_b, re_w, re_b):
    B, S, D = x.shape
    tm = min(512, S)
    spare = LANES - N_GROUPS - N_EXPERTS
    rw = jnp.concatenate([rg_w, re_w, jnp.zeros((D, spare), F32)], axis=1)
    rb = jnp.concatenate([rg_b, re_b, jnp.zeros((spare,), F32)]).reshape(1, LANES)
    rwh, rwl = _split(rw)
    tri = (jnp.arange(tm)[:, None] > jnp.arange(tm)[None, :]).astype(BF16)
    row = lambda b, i: (b, i, 0)
    full = lambda r, c: pl.BlockSpec((r, c), lambda b, i: (0, 0))
    per_b = S // tm
    outs = [jax.ShapeDtypeStruct((B, S, D), F32), jax.ShapeDtypeStruct((B, S, D // 2), F32),
            jax.ShapeDtypeStruct((SUBLANES, B * S), jnp.int32), jax.ShapeDtypeStruct((B, S, LANES), F32),
            jax.ShapeDtypeStruct((1, LANES), F32)]
    return pl.pallas_call(
        _merge_kernel,
        out_shape=outs,
        grid=(B, per_b),
        in_specs=[pl.BlockSpec((1, tm, ATTN_WIDTH), row), pl.BlockSpec((1, tm, HYENA_WIDTH), row),
                  pl.BlockSpec((1, tm, D), row), pl.BlockSpec((1, tm, D), row), pl.BlockSpec((1, tm, D), row),
                  pl.BlockSpec((1, 6, D), lambda b, i: (b, 0, 0)),
                  full(ATTN_WIDTH, D), full(HYENA_WIDTH, D), full(D, D),
                  full(1, D), full(1, D), full(D, LANES), full(D, LANES), full(1, LANES), full(tm, tm)],
        out_specs=[pl.BlockSpec((1, tm, D), row), pl.BlockSpec((1, tm, D // 2), row),
                   pl.BlockSpec((SUBLANES, tm), lambda b, i: (0, b * per_b + i)),
                   pl.BlockSpec((1, tm, LANES), row), full(1, LANES)],
        scratch_shapes=[pltpu.VMEM((1, LANES), F32)],
        compiler_params=_params(2),
        name="merge",
    )(attn, hy, ga, gh, x, mod, w_attn_o.astype(BF16), w_hy_o.astype(BF16), w_out.astype(BF16),
      ln1_g.reshape(1, D), ln1_b.reshape(1, D), rwh, rwl, rb, tri)


SC_ROWS = 64


def _sc_workers():
    info = plsc.get_sparse_core_info()
    return info.num_cores, info.num_cores * info.num_subcores


def _sc_split(n):
    _, workers = _sc_workers()
    per_worker = n // workers
    chunks = per_worker // SC_ROWS
    assert per_worker * workers == n and chunks * SC_ROWS == per_worker and chunks % 2 == 0
    return workers, per_worker, chunks


def _sc_scatter_rows(src, idx0, idx1, n_out):
    n, width = src.shape
    nc, _ = _sc_workers()
    workers, per_worker, chunks = _sc_split(n)
    mesh = plsc.VectorSubcoreMesh(core_axis_name="c", subcore_axis_name="s")

    def body(src_hbm, i0_hbm, i1_hbm, out_hbm, i0_v, i1_v, rows_v, sem, ssem):
        wid = lax.axis_index("s") * nc + lax.axis_index("c")
        base = wid * per_worker
        pltpu.sync_copy(i0_hbm.at[wid], i0_v)
        pltpu.sync_copy(i1_hbm.at[wid], i1_v)

        def load(chunk, buf):
            return pltpu.make_async_copy(src_hbm.at[pl.ds(base + chunk * SC_ROWS, SC_ROWS)], rows_v.at[buf], sem)

        load(0, 0).start()

        @pl.loop(0, chunks, step=2)
        def _(c):
            for b in range(2):
                chunk = c + b
                load(chunk, b).wait()

                @pl.when(chunk + 1 < chunks)
                def _():
                    load(chunk + 1, 1 - b).start()

                first = pltpu.make_async_copy(rows_v.at[b], out_hbm.at[i0_v.at[chunk]], ssem)
                second = pltpu.make_async_copy(rows_v.at[b], out_hbm.at[i1_v.at[chunk]], ssem)
                first.start()
                second.start()
                first.wait()
                second.wait()

    shaped = lambda i: i.reshape(workers, chunks, SC_ROWS)
    return pl.kernel(
        body,
        out_type=jax.ShapeDtypeStruct((n_out, width), src.dtype),
        mesh=mesh,
        scratch_types=[pltpu.VMEM((chunks, SC_ROWS), jnp.int32),
                       pltpu.VMEM((chunks, SC_ROWS), jnp.int32),
                       pltpu.VMEM((2, SC_ROWS, width), src.dtype),
                       pltpu.SemaphoreType.DMA, pltpu.SemaphoreType.DMA],
        name="sc_scatter",
    )(src, shaped(idx0), shaped(idx1))


def _expert_kernel(first_ref, nb_ref, sz_ref, tot_ref, w1_ref, w3_ref, w2_ref, xb_ref, yb_ref,
                   xbuf, ybuf, c1_ref, c3_ref, c2_ref, lsem, ssem):
    e = pl.program_id(0)
    nb = nb_ref[e]
    first = first_ref[e]
    total = tot_ref[0]
    rows = xbuf.shape[1]

    def load(g, slot):
        src = xb_ref.at[pl.ds(pl.multiple_of(g * rows, rows), rows)]
        return pltpu.make_async_copy(src, xbuf.at[slot], lsem.at[slot])

    def store(g, slot):
        dst = yb_ref.at[pl.ds(pl.multiple_of(g * rows, rows), rows)]
        return pltpu.make_async_copy(ybuf.at[slot], dst, ssem.at[slot])

    @pl.when((e == 0) & (total > 0))
    def _():
        load(0, 0).start()

    @pl.when(nb > 0)
    def _():
        c1_ref[...] = w1_ref[0].astype(BF16)
        c3_ref[...] = w3_ref[0].astype(BF16)
        c2_ref[...] = w2_ref[0].astype(BF16)

        def block(j, carry):
            g = first + j
            slot = lax.rem(g, 2)
            load(g, slot).wait()

            @pl.when(g + 1 < total)
            def _():
                load(g + 1, 1 - slot).start()

            @pl.when(g >= 2)
            def _():
                store(g - 2, slot).wait()

            n_valid = sz_ref[e] - j * rows
            pieces = jnp.minimum((n_valid + EXPERT_ROWS - 1) // EXPERT_ROWS, rows // EXPERT_ROWS)

            def swiglu(n):
                rid = lax.broadcasted_iota(jnp.int32, (n, 1), 0)
                xa, xb = _unpack2(jnp.where(rid < n_valid, xbuf[slot, :n, :], 0.0))
                x = jnp.concatenate([xa, xb], axis=1).astype(BF16)
                a = jnp.dot(x, c1_ref[...], preferred_element_type=F32)
                gate = jnp.dot(x, c3_ref[...], preferred_element_type=F32)
                hmid = (a * _sigmoid(a) * gate).astype(BF16)
                y = jnp.dot(hmid, c2_ref[...], preferred_element_type=F32)
                half = y.shape[1] // 2
                ybuf[slot, :n, :] = _pack2(y[:, :half], y[:, half:])

            for q in range(1, rows // EXPERT_ROWS + 1):
                pl.when(pieces == q)(functools.partial(swiglu, q * EXPERT_ROWS))
            store(g, slot).start()
            return carry

        lax.fori_loop(0, nb, block, 0)

    @pl.when(e == pl.num_programs(0) - 1)
    def _():
        for back in (2, 1):
            @pl.when(total >= back)
            def _():
                g = total - back
                store(g, lax.rem(g, 2)).wait()


def _experts(xb, first_blk, n_blk, sizes, w1, w3, w2):
    P, W = xb.shape
    E, D, DE = w1.shape
    total = jnp.sum(n_blk, keepdims=True)
    wspec = lambda r, c: pl.BlockSpec((1, r, c), lambda e, *_: (e, 0, 0))
    grid_spec = pltpu.PrefetchScalarGridSpec(
        num_scalar_prefetch=4,
        grid=(E,),
        in_specs=[wspec(D, DE), wspec(D, DE), wspec(DE, D), pl.BlockSpec(memory_space=pl.ANY)],
        out_specs=pl.BlockSpec(memory_space=pl.ANY),
        scratch_shapes=[pltpu.VMEM((2, MOE_BLOCK, W), F32), pltpu.VMEM((2, MOE_BLOCK, W), F32),
                        pltpu.VMEM((D, DE), BF16), pltpu.VMEM((D, DE), BF16), pltpu.VMEM((DE, D), BF16),
                        pltpu.SemaphoreType.DMA((2,)), pltpu.SemaphoreType.DMA((2,))],
    )
    return pl.pallas_call(
        _expert_kernel,
        out_shape=jax.ShapeDtypeStruct((P, W), F32),
        grid_spec=grid_spec,
        compiler_params=_params(1),
        name="experts",
    )(first_blk, n_blk, sizes, total, w1, w3, w2, xb)


def _sc_gather_rows(table, idx):
    n, width = idx.shape[0], table.shape[1]
    nc, _ = _sc_workers()
    workers, per_worker, chunks = _sc_split(n)
    mesh = plsc.VectorSubcoreMesh(core_axis_name="c", subcore_axis_name="s")

    def body(table_hbm, idx_hbm, out_hbm, idx_v, rows_v, sem):
        wid = lax.axis_index("s") * nc + lax.axis_index("c")
        base = wid * per_worker
        pltpu.sync_copy(idx_hbm.at[wid], idx_v)

        def gather(chunk, buf):
            return pltpu.make_async_copy(table_hbm.at[idx_v.at[chunk]], rows_v.at[buf], sem)

        gather(0, 0).start()

        @pl.loop(0, chunks, step=2)
        def _(c):
            for b in range(2):
                chunk = c + b
                gather(chunk, b).wait()

                @pl.when(chunk + 1 < chunks)
                def _():
                    gather(chunk + 1, 1 - b).start()

                pltpu.sync_copy(rows_v.at[b], out_hbm.at[pl.ds(base + chunk * SC_ROWS, SC_ROWS)])

    return pl.kernel(
        body,
        out_type=jax.ShapeDtypeStruct((n, width), table.dtype),
        mesh=mesh,
        scratch_types=[pltpu.VMEM((chunks, SC_ROWS), jnp.int32),
                       pltpu.VMEM((2, SC_ROWS, width), table.dtype),
                       pltpu.SemaphoreType.DMA],
        name="sc_gather",
    )(table, idx.reshape(workers, chunks, SC_ROWS))


def _combine_dense_kernel(r0_ref, r1_ref, wts_ref, x1_ref, mod_ref, g_ref, b_ref, *rest):
    o_ref = rest[-1]
    w = wts_ref[...]
    y0 = jnp.concatenate(_unpack2(r0_ref[0]), axis=1)
    y1 = jnp.concatenate(_unpack2(r1_ref[0]), axis=1)
    y = w[:, 0:1] * y0 + w[:, 1:2] * y1
    gate2 = mod_ref[0, 5:6, :]
    o_ref[...] = _layer_norm(DN_ALPHA * x1_ref[...] + gate2 * y, g_ref[...], b_ref[...])


def _combine_dense(rows, wts, x1, mod, ln2_g, ln2_b, S, b, out):
    T, D = x1.shape
    tm = min(512, S)
    per_b = S // tm
    here = lambda i: (b * per_b + i, 0)
    in_specs = [pl.BlockSpec((1, tm, rows.shape[2]), lambda i: (0, i, 0)),
                pl.BlockSpec((1, tm, rows.shape[2]), lambda i: (1, i, 0)),
                pl.BlockSpec((tm, LANES), here),
                pl.BlockSpec((tm, D), here),
                pl.BlockSpec((1, 6, D), lambda i: (b, 0, 0)),
                pl.BlockSpec((1, D), lambda i: (0, 0)),
                pl.BlockSpec((1, D), lambda i: (0, 0))]
    args = [rows, rows, wts, x1, mod, ln2_g.reshape(1, D), ln2_b.reshape(1, D)]
    aliases = {}
    if out is not None:
        in_specs.append(pl.BlockSpec(memory_space=pl.ANY))
        aliases = {len(args): 0}
        args.append(out)
    return pl.pallas_call(
        _combine_dense_kernel,
        out_shape=jax.ShapeDtypeStruct((T, D), F32),
        grid=(per_b,),
        in_specs=in_specs,
        out_specs=pl.BlockSpec((tm, D), here),
        input_output_aliases=aliases,
        compiler_params=_params(1),
        name="combine",
    )(*args)


def _moe(h2, x1, route, wts, counts, mod, w1, w3, w2, ln2_g, ln2_b):
    B, S, D = x1.shape
    T = B * S
    P = 2 * T + N_EXPERTS * MOE_BLOCK
    sizes = counts[0, ROUTE_OFF:ROUTE_OFF + N_EXPERTS].astype(jnp.int32)
    n_blk = (sizes + MOE_BLOCK - 1) // MOE_BLOCK
    psizes = n_blk * MOE_BLOCK
    poffs = jnp.cumsum(psizes) - psizes
    sel = route[0:2, None, :] == jnp.arange(N_EXPERTS, dtype=jnp.int32)[None, :, None]
    dest = route[2:4] + jnp.sum(jnp.where(sel, poffs[None, :, None], 0), axis=1)
    xb = _sc_scatter_rows(h2.reshape(T, D // 2), dest[0], dest[1], P)
    yb = _experts(xb, poffs // MOE_BLOCK, n_blk, sizes, w1, w3, w2)
    out = None
    for b in range(B):
        slot_major = dest[:, b * S:(b + 1) * S].reshape(2 * S)
        rows = _sc_gather_rows(yb, slot_major).reshape(2, S, yb.shape[1])
        out = _combine_dense(rows, wts.reshape(T, LANES), x1.reshape(T, D), mod, ln2_g, ln2_b, S, b, out)
    return out.reshape(B, S, D)


def _layer(x, c, w_ada, b_ada, w_in, conv_w, conv_b, fw1, fb1, ff1, fw2, fb2, ff2, fw3, decay, skip,
           w_hy_o, w_attn_o, attn_sink, w_out, ln1_g, ln1_b, rg_w, rg_b, re_w, re_b, ew1, ew3, ew2,
           ln2_g, ln2_b):
    mod = _ada(c, w_ada, b_ada)
    q, kv, hv, hx1, hx2, ga, gh = _in_proj(x, mod, w_in, conv_w, conv_b)
    attn = _attention(q, kv, attn_sink)
    hy = _hyena(hv, hx1, hx2, fw1, fb1, ff1, fw2, fb2, ff2, fw3, decay, skip)
    x1, h2, route, wts, counts = _merge(attn, hy, ga, gh, x, mod, w_attn_o, w_hy_o, w_out,
                                        ln1_g, ln1_b, rg_w, rg_b, re_w, re_b)
    return _moe(h2, x1, route, wts, counts, mod, ew1, ew3, ew2, ln2_g, ln2_b)


def kernel(x, c, w_ada, b_ada, w_in, conv_w, conv_b, filt_w1, filt_b1, filt_freq1, filt_w2, filt_b2, filt_freq2, filt_w3, filt_decay, hy_skip, w_hy_o, w_attn_o, attn_sink, w_out, ln1_g, ln1_b, router_group_w, router_group_b, router_expert_w, router_expert_b, exp_w1, exp_w3, exp_w2, ln2_g, ln2_b):
    for l in range(w_ada.shape[0]):
        x = _layer(x, c, w_ada[l], b_ada[l], w_in[l], conv_w[l], conv_b[l], filt_w1[l], filt_b1[l],
                   filt_freq1[l], filt_w2[l], filt_b2[l], filt_freq2[l], filt_w3[l], filt_decay[l],
                   hy_skip[l], w_hy_o[l], w_attn_o[l], attn_sink[l], w_out[l], ln1_g[l], ln1_b[l],
                   router_group_w[l], router_group_b[l], router_expert_w[l], router_expert_b[l],
                   exp_w1[l], exp_w3[l], exp_w2[l], ln2_g[l], ln2_b[l])
    return x
```

```python
import functools
import math

import numpy as np
import jax
import jax.numpy as jnp
from jax import lax
from jax.experimental import pallas as pl
from jax.experimental.pallas import tpu as pltpu
from jax.experimental.pallas import tpu_sc as plsc

F32 = jnp.float32
BF16 = jnp.bfloat16

N_HEADS = 8
N_KV_HEADS = 2
HEAD_DIM = 64
ATTN_WIDTH = N_HEADS * HEAD_DIM
KV_WIDTH = N_KV_HEADS * HEAD_DIM
WINDOW = 128
HYENA_WIDTH = 512
FILTER_EMB = 33
FILTER_BANDS = (FILTER_EMB - 1) // 2
WINDOW_SHIFT = 0.05
N_GROUPS = 8
EXPERTS_PER_GROUP = 8
N_EXPERTS = N_GROUPS * EXPERTS_PER_GROUP
MOE_BLOCK = 512
EXPERT_ROWS = 128
LN_EPS = 1e-5
DEPTH = 1
DN_ALPHA = (2.0 * DEPTH) ** 0.25
NEG = -1e30

LANES = 128
SUBLANES = 8
ROUTE_OFF = N_GROUPS
VMEM_LIMIT = 56 * 1024 * 1024


def _params(n_axes, vmem=VMEM_LIMIT):
    return pltpu.CompilerParams(dimension_semantics=("arbitrary",) * n_axes, vmem_limit_bytes=vmem)


def _split(a):
    hi = a.astype(BF16)
    lo = (a - hi.astype(F32)).astype(BF16)
    return hi, lo


def _dot3(a, b_hi, b_lo):
    a_hi, a_lo = _split(a)
    acc = jnp.dot(a_hi, b_hi, preferred_element_type=F32)
    acc = acc + jnp.dot(a_hi, b_lo, preferred_element_type=F32)
    acc = acc + jnp.dot(a_lo, b_hi, preferred_element_type=F32)
    return acc


def _pack2(a, b):
    ia = lax.bitcast_convert_type(a.astype(BF16).astype(F32), jnp.int32)
    ib = lax.bitcast_convert_type(b.astype(BF16).astype(F32), jnp.int32)
    return lax.bitcast_convert_type(ia | lax.shift_right_logical(ib, 16), F32)


def _unpack2(p):
    p = lax.bitcast_convert_type(p, jnp.int32)
    a = lax.bitcast_convert_type(p & jnp.int32(-65536), F32)
    b = lax.bitcast_convert_type(lax.shift_left(p, 16), F32)
    return a, b


def _sigmoid(x):
    return 0.5 * jnp.tanh(0.5 * x) + 0.5


def _layer_norm(r, g, b):
    mu = jnp.mean(r, axis=-1, keepdims=True)
    d = r - mu
    var = jnp.mean(d * d, axis=-1, keepdims=True)
    return d * lax.rsqrt(var + LN_EPS) * g + b


def _ada_kernel(c_ref, w_ref, b_ref, o_ref):
    c = c_ref[...]
    s = c * _sigmoid(c)
    wh, wl = _split(w_ref[...])
    o_ref[...] = _dot3(s, wh, wl) + b_ref[...]


def _ada(c, w_ada, b_ada):
    B, D = c.shape
    n_out = w_ada.shape[1]
    rows = SUBLANES
    cp = jnp.pad(c, ((0, rows - B), (0, 0)))
    tn = 1024
    out = pl.pallas_call(
        _ada_kernel,
        out_shape=jax.ShapeDtypeStruct((rows, n_out), F32),
        grid=(n_out // tn,),
        in_specs=[pl.BlockSpec((rows, D), lambda j: (0, 0)),
                  pl.BlockSpec((D, tn), lambda j: (0, j)),
                  pl.BlockSpec((1, tn), lambda j: (0, j))],
        out_specs=pl.BlockSpec((rows, tn), lambda j: (0, j)),
        compiler_params=_params(1),
        name="ada",
    )(cp, w_ada, b_ada.reshape(1, n_out))
    return out[:B].reshape(B, 6, D)


def _inproj_kernel(x_ref, xp_ref, xn_ref, mod_ref, w_ref, cw_ref, cb_ref,
                   q_ref, kv_ref, v_ref, x1_ref, x2_ref, ga_ref, gh_ref):
    i = pl.program_id(1)
    n = pl.num_programs(1)
    C = HYENA_WIDTH
    x = x_ref[0]
    tm, D = x.shape
    shift = mod_ref[0, 0:1, :]
    scale = mod_ref[0, 1:2, :]
    h = (x * (1.0 + scale) + shift).astype(BF16)

    def seg(lo, hi):
        return jnp.dot(h, w_ref[:, lo:hi], preferred_element_type=F32)

    o_q = 0
    o_kv = o_q + ATTN_WIDTH
    o_hy = o_kv + 2 * KV_WIDTH
    o_ga = o_hy + 3 * C
    o_gh = o_ga + D
    ga_ref[0] = _sigmoid(seg(o_ga, o_ga + D)).astype(BF16)
    gh_ref[0] = _sigmoid(seg(o_gh, o_gh + D)).astype(BF16)

    u = seg(o_hy, o_hy + 3 * C)
    xe = jnp.concatenate([xp_ref[0], xn_ref[0]], axis=0)
    he = (xe * (1.0 + scale) + shift).astype(BF16)
    ue = jnp.dot(he, w_ref[:, o_hy:o_hy + 3 * C], preferred_element_type=F32)
    prow = jnp.where(i > 0, ue[SUBLANES - 1:SUBLANES], 0.0)
    nrow = jnp.where(i < n - 1, ue[SUBLANES:SUBLANES + 1], 0.0)
    rid = lax.broadcasted_iota(jnp.int32, (tm, 1), 0)
    up = jnp.where(rid == 0, prow, pltpu.roll(u, 1, 0))
    dn = jnp.where(rid == tm - 1, nrow, pltpu.roll(u, tm - 1, 0))
    conv = cw_ref[0:1, :] * up + cw_ref[1:2, :] * u + cw_ref[2:3, :] * dn + cb_ref[...]
    v_ref[0] = conv[:, :C]
    x1_ref[0] = conv[:, C:2 * C]
    x2_ref[0] = conv[:, 2 * C:]

    q_ref[0] = (seg(o_q, o_q + ATTN_WIDTH) * (HEAD_DIM ** -0.5)).astype(BF16)
    kv_ref[0] = seg(o_kv, o_kv + 2 * KV_WIDTH).astype(BF16)


def _in_proj(x, mod, w_in, conv_w, conv_b):
    B, S, D = x.shape
    C = HYENA_WIDTH
    tm = min(1024, S)
    r8 = tm // SUBLANES
    nb8 = S // SUBLANES
    wb = w_in.astype(BF16)
    nw = wb.shape[1]
    row = lambda b, i: (b, i, 0)
    shapes = [(ATTN_WIDTH, BF16), (2 * KV_WIDTH, BF16), (C, F32), (C, F32), (C, F32), (D, BF16), (D, BF16)]
    return pl.pallas_call(
        _inproj_kernel,
        out_shape=[jax.ShapeDtypeStruct((B, S, w), dt) for w, dt in shapes],
        grid=(B, S // tm),
        in_specs=[pl.BlockSpec((1, tm, D), row),
                  pl.BlockSpec((1, SUBLANES, D), lambda b, i: (b, jnp.maximum(i * r8 - 1, 0), 0)),
                  pl.BlockSpec((1, SUBLANES, D), lambda b, i: (b, jnp.minimum((i + 1) * r8, nb8 - 1), 0)),
                  pl.BlockSpec((1, 6, D), lambda b, i: (b, 0, 0)),
                  pl.BlockSpec((D, nw), lambda b, i: (0, 0)),
                  pl.BlockSpec((3, 3 * C), lambda b, i: (0, 0)),
                  pl.BlockSpec((1, 3 * C), lambda b, i: (0, 0))],
        out_specs=[pl.BlockSpec((1, tm, w), row) for w, _ in shapes],
        compiler_params=_params(2),
        name="in_proj",
    )(x, x, x, mod, wb, conv_w.astype(F32), conv_b.reshape(1, 3 * C).astype(F32))


ATT_TQ = 512
ATT_QB = 128
ATT_STACK = 4


def _attn_kernel(sink_ref, q_ref, kvp_ref, kvc_ref, kvn_ref, bias_ref, o_ref, kv_scr, vx_scr, *, seq_len):
    i = pl.program_id(1)
    H = WINDOW
    TQ = q_ref.shape[1]
    Q = min(ATT_QB, TQ)
    band = Q + 2 * H
    G = N_HEADS // N_KV_HEADS
    kv_scr[0:H] = kvp_ref[0]
    kv_scr[H:H + TQ] = kvc_ref[0]
    kv_scr[H + TQ:] = kvn_ref[0]
    for kv in range(N_KV_HEADS):
        vx_scr[:, kv * LANES:kv * LANES + HEAD_DIM] = kv_scr[:, KV_WIDTH + kv * HEAD_DIM:KV_WIDTH + (kv + 1) * HEAD_DIM]
        vx_scr[:, kv * LANES + HEAD_DIM:(kv + 1) * LANES] = jnp.ones((TQ + 2 * H, LANES - HEAD_DIM), BF16)
    col = lax.broadcasted_iota(jnp.int32, (1, band), 1)
    rhead = lax.broadcasted_iota(jnp.int32, (ATT_STACK * Q, 1), 0) // Q
    for j in range(TQ // Q):
        kpos = i * TQ + j * Q - H + col
        colbias = jnp.where((kpos >= 0) & (kpos < seq_len), 0.0, NEG)
        for kv in range(N_KV_HEADS):
            kk = kv_scr[j * Q:j * Q + band, kv * HEAD_DIM:(kv + 1) * HEAD_DIM]
            vx = vx_scr[j * Q:j * Q + band, kv * LANES:(kv + 1) * LANES]
            for sub in range(G // ATT_STACK):
                first = sub * ATT_STACK
                heads = [kv * G + first + g for g in range(ATT_STACK)]
                qg = jnp.concatenate([q_ref[0, j * Q:(j + 1) * Q, h * HEAD_DIM:(h + 1) * HEAD_DIM] for h in heads],
                                     axis=0)
                s = lax.dot_general(qg, kk, (((1,), (1,)), ((), ())), preferred_element_type=F32)
                s = s + bias_ref[kv, first * Q:(first + ATT_STACK) * Q, :] + colbias
                snk = sink_ref[heads[-1]]
                for g in range(ATT_STACK - 2, -1, -1):
                    snk = jnp.where(rhead == g, sink_ref[heads[g]], snk)
                m = jnp.maximum(jnp.max(s, axis=1, keepdims=True), snk)
                p = jnp.exp(s - m).astype(BF16)
                ox = jnp.dot(p, vx, preferred_element_type=F32)
                den = ox[:, HEAD_DIM:HEAD_DIM + 1] + jnp.exp(snk - m)
                o = ox[:, :HEAD_DIM] / den
                for g, h in enumerate(heads):
                    o_ref[0, j * Q:(j + 1) * Q, h * HEAD_DIM:(h + 1) * HEAD_DIM] = o[g * Q:(g + 1) * Q].astype(BF16)


def _attention(q, kv, sink):
    B, S, _ = q.shape
    H = WINDOW
    TQ = min(ATT_TQ, S)
    Q = min(ATT_QB, TQ)
    r = TQ // H
    nq = S // H
    G = N_HEADS // N_KV_HEADS
    assert G % ATT_STACK == 0
    a = jnp.arange(Q)[:, None]
    j = jnp.arange(Q + 2 * H)[None, :]
    rel = jnp.abs(j - H - a).astype(F32)
    slopes = 2.0 ** (-8.0 * jnp.arange(1, N_HEADS + 1, dtype=F32) / N_HEADS)
    bias = jnp.where(rel[None] <= WINDOW, -slopes[:, None, None] * rel[None], NEG).astype(F32)
    bias = bias.reshape(N_KV_HEADS, G * Q, Q + 2 * H)
    cur = lambda b, i: (b, i, 0)
    return pl.pallas_call(
        functools.partial(_attn_kernel, seq_len=S),
        out_shape=jax.ShapeDtypeStruct((B, S, ATTN_WIDTH), BF16),
        grid=(B, S // TQ),
        in_specs=[pl.BlockSpec(memory_space=pltpu.SMEM),
                  pl.BlockSpec((1, TQ, ATTN_WIDTH), cur),
                  pl.BlockSpec((1, H, 2 * KV_WIDTH), lambda b, i: (b, jnp.maximum(i * r - 1, 0), 0)),
                  pl.BlockSpec((1, TQ, 2 * KV_WIDTH), cur),
                  pl.BlockSpec((1, H, 2 * KV_WIDTH), lambda b, i: (b, jnp.minimum((i + 1) * r, nq - 1), 0)),
                  pl.BlockSpec((N_KV_HEADS, G * Q, Q + 2 * H), lambda b, i: (0, 0, 0))],
        out_specs=pl.BlockSpec((1, TQ, ATTN_WIDTH), cur),
        scratch_shapes=[pltpu.VMEM((TQ + 2 * H, 2 * KV_WIDTH), BF16),
                        pltpu.VMEM((TQ + 2 * H, N_KV_HEADS * LANES), BF16)],
        compiler_params=_params(2),
        name="attn",
    )(sink.astype(F32), q, kv, kv, kv, bias)


def _filter_kernel(z_ref, w1h, w1l, b1_ref, f1_ref, w2h, w2l, b2_ref, f2_ref, w3h, w3l, dec_ref,
                   k_ref, s_ref):
    i = pl.program_id(0)
    z = z_ref[...]
    h1 = jnp.sin(f1_ref[...] * (_dot3(z, w1h[...], w1l[...]) + b1_ref[...]))
    h2 = jnp.sin(f2_ref[...] * (_dot3(h1, w2h[...], w2l[...]) + b2_ref[...]))
    k = _dot3(h2, w3h[...], w3l[...])
    t = z[:, 0:1]
    k = k * (jnp.exp(-t * jnp.abs(dec_ref[...])) + WINDOW_SHIFT)
    k_ref[...] = k

    @pl.when(i == 0)
    def _():
        s_ref[...] = jnp.zeros_like(s_ref)

    s_ref[...] += jnp.sum(jnp.abs(k), axis=0, keepdims=True)


def _filter_embedding(L):
    t = np.linspace(0.0, 1.0, L, dtype=np.float32).astype(np.float64)[:, None]
    w = (2.0 * math.pi * np.arange(L, dtype=np.float32) / np.float32(L)).astype(np.float64)[:, None]
    bands = np.linspace(1e-4, FILTER_BANDS - 1, FILTER_BANDS, dtype=np.float32).astype(np.float64)[None, :]
    bw = (bands.astype(np.float32) * w.astype(np.float32)).astype(np.float64)
    z = np.concatenate([t, np.cos(bw), -np.sin(bw)], axis=-1)
    zp = np.zeros((L, LANES), np.float32)
    zp[:, :FILTER_EMB] = z.astype(np.float32)
    return jnp.asarray(zp)


def _pad2(a, r, c):
    return jnp.zeros((r, c), F32).at[:a.shape[0], :a.shape[1]].set(a.astype(F32))


def _filters(L, fw1, fb1, ff1, fw2, fb2, ff2, fw3, decay):
    H = LANES
    nf = fw3.shape[1]
    z = _filter_embedding(L)
    w1h, w1l = _split(_pad2(fw1, H, H))
    w2h, w2l = _split(_pad2(fw2, H, H))
    w3h, w3l = _split(_pad2(fw3, H, nf))
    b1 = _pad2(fb1[None], 1, H)
    f1 = _pad2(ff1[None], 1, H)
    b2 = _pad2(fb2[None], 1, H)
    f2 = _pad2(ff2[None], 1, H)
    tr = min(1024, L)
    full = lambda r, c: pl.BlockSpec((r, c), lambda i: (0, 0))
    return pl.pallas_call(
        _filter_kernel,
        out_shape=[jax.ShapeDtypeStruct((L, nf), F32), jax.ShapeDtypeStruct((1, nf), F32)],
        grid=(L // tr,),
        in_specs=[pl.BlockSpec((tr, H), lambda i: (i, 0)),
                  full(H, H), full(H, H), full(1, H), full(1, H),
                  full(H, H), full(H, H), full(1, H), full(1, H),
                  full(H, nf), full(H, nf), full(1, nf)],
        out_specs=[pl.BlockSpec((tr, nf), lambda i: (i, 0)), full(1, nf)],
        compiler_params=_params(1),
        name="filter",
    )(z, w1h, w1l, b1, f1, w2h, w2l, b2, f2, w3h, w3l, decay.reshape(1, nf).astype(F32))


def _np_bf16(m64):
    return jnp.asarray(m64.astype(np.float32).astype(BF16))


def _dft_constants(L):
    N = 2 * L
    n2 = LANES
    n1 = N // n2
    h1 = n1 // 2
    k1 = np.arange(n1)[:, None]
    s1 = np.arange(h1)[None, :]
    ang = -2.0 * np.pi * ((k1 * s1) % n1) / n1
    wr, wi = np.cos(ang), np.sin(ang)
    w1_filt = np.block([[wr, wr], [wi, wi], [wr, -wr], [wi, -wi]])
    w1_cplx = np.block([[wr, -wi], [wi, wr]])
    vr, vi = wr.T / N, -wi.T / N
    w3 = np.block([[vr, -vi], [vi, vr]])
    k2 = np.arange(n2)[:, None]
    s2 = np.arange(n2)[None, :]
    a2 = -2.0 * np.pi * ((k2 * s2) % n2) / n2
    w2r, w2i = jnp.asarray(np.cos(a2), F32), jnp.asarray(np.sin(a2), F32)
    at = -2.0 * np.pi * ((np.arange(n1)[:, None] * s2) % N) / N
    twr, twi = jnp.asarray(np.cos(at), F32), jnp.asarray(np.sin(at), F32)
    mr = w2r[None] * twr[:, None, :] - w2i[None] * twi[:, None, :]
    mi = w2r[None] * twi[:, None, :] + w2i[None] * twr[:, None, :]
    fwd = jnp.concatenate([jnp.concatenate([mr, -mi], axis=2),
                           jnp.concatenate([mi, mr], axis=2)], axis=1)
    fwd = fwd.astype(BF16)
    return dict(n1=n1, w1_filt=_np_bf16(w1_filt), w1_cplx=_np_bf16(w1_cplx), w3=_np_bf16(w3),
                fwd=fwd)


SCH = 16


def _dft1_kernel(x_ref, w_ref, a_ref, *, n1):
    w = w_ref[...]
    for j in range(SCH):
        rhs = jnp.concatenate([x_ref[0, 0, :, j, :], x_ref[0, 1, :, j, :]], axis=0)
        res = jnp.dot(w, rhs.astype(BF16), preferred_element_type=F32)
        a_ref[0, :, j, :] = _pack2(res[:n1], res[n1:])


def _dft1_data(x, consts):
    B, L, C = x.shape
    n1 = consts["n1"]
    h1 = n1 // 2
    xv = x.reshape(B // 2, 2, h1, LANES, C)
    return pl.pallas_call(
        functools.partial(_dft1_kernel, n1=n1),
        out_shape=jax.ShapeDtypeStruct((B // 2, n1, LANES, C), F32),
        grid=(B // 2, LANES // SCH),
        in_specs=[pl.BlockSpec((1, 2, h1, SCH, C), lambda p, j: (p, 0, 0, j, 0)),
                  pl.BlockSpec((2 * n1, n1), lambda p, j: (0, 0))],
        out_specs=pl.BlockSpec((1, n1, SCH, C), lambda p, j: (p, 0, j, 0)),
        compiler_params=_params(2),
        name="dft1",
    )(xv, consts["w1_cplx"])


def _dft1f_kernel(x_ref, w_ref, a_ref, *, n1):
    C = HYENA_WIDTH
    w = w_ref[...]
    for j in range(SCH):
        rhs = jnp.concatenate([x_ref[:, j, :C], x_ref[:, j, C:]], axis=0)
        res = jnp.dot(w, rhs.astype(BF16), preferred_element_type=F32)
        a_ref[0, :, 0, j, :] = _pack2(res[:n1], res[n1:2 * n1])
        a_ref[0, :, 1, j, :] = _pack2(res[2 * n1:3 * n1], res[3 * n1:])


def _dft1_filter(kraw, consts):
    L, nf = kraw.shape
    C = HYENA_WIDTH
    n_ord = nf // (2 * C)
    n1 = consts["n1"]
    h1 = n1 // 2
    kv = kraw.reshape(h1, LANES, nf)
    return pl.pallas_call(
        functools.partial(_dft1f_kernel, n1=n1),
        out_shape=jax.ShapeDtypeStruct((n_ord, n1, 2, LANES, C), F32),
        grid=(n_ord, LANES // SCH),
        in_specs=[pl.BlockSpec((h1, SCH, 2 * C), lambda o, j: (0, j, o)),
                  pl.BlockSpec((4 * n1, n1), lambda o, j: (0, 0))],
        out_specs=pl.BlockSpec((1, n1, 2, SCH, C), lambda o, j: (o, 0, 0, j, 0)),
        compiler_params=_params(2),
        name="dft1f",
    )(kv, consts["w1_filt"])


KCH = 16


def _midf_kernel(a_ref, f_ref, inv_ref, b0_ref, h_ref):
    n2 = LANES
    sc = inv_ref[0]
    for k in range(KCH):
        p = jnp.concatenate(_unpack2(a_ref[0, k, :n2, :]), axis=0).astype(BF16)
        q = jnp.concatenate(_unpack2(a_ref[0, k, n2:, :]), axis=0).astype(BF16)
        h_re = jnp.dot(f_ref[k, :n2, :], p, preferred_element_type=F32)
        h_im = jnp.dot(f_ref[k, n2:, :], q, preferred_element_type=F32)
        h_ref[0, k] = _pack2((h_re - b0_ref[0]) * sc, h_im * sc)


def _filter_spectrum(af, inv_den, bwd0, consts):
    n_ord, n1, _, n2, C = af.shape
    a = af.reshape(n_ord, n1, 2 * n2, C)
    tab = pl.BlockSpec((KCH, 2 * n2, 2 * n2), lambda k, o: (k, 0, 0))
    vec = pl.BlockSpec((1, 1, C), lambda k, o: (o, 0, 0))
    return pl.pallas_call(
        _midf_kernel,
        out_shape=jax.ShapeDtypeStruct((n_ord, n1, n2, C), F32),
        grid=(n1 // KCH, n_ord),
        in_specs=[pl.BlockSpec((1, KCH, 2 * n2, C), lambda k, o: (o, k, 0, 0)), tab, vec, vec],
        out_specs=pl.BlockSpec((1, KCH, n2, C), lambda k, o: (o, k, 0, 0)),
        compiler_params=_params(2),
        name="midf",
    )(a, consts["fwd"], inv_den, bwd0)


def _mid_kernel(a_ref, f_ref, h_ref, b_ref):
    n2 = LANES
    for k in range(KCH):
        a = jnp.concatenate(_unpack2(a_ref[0, k]), axis=0).astype(BF16)
        x = jnp.dot(f_ref[k], a, preferred_element_type=F32)
        xr, xi = x[:n2], x[n2:]
        hr, hi = _unpack2(h_ref[0, k])
        y = jnp.concatenate([xr * hr - xi * hi, xr * hi + xi * hr], axis=0)
        b = lax.dot_general(f_ref[k], y.astype(BF16), (((0,), (0,)), ((), ())), preferred_element_type=F32)
        b_ref[0, k] = _pack2(b[:n2], b[n2:])


def _mid(a, hspec, order, consts):
    P, n1, n2, C = a.shape
    tab = pl.BlockSpec((KCH, 2 * n2, 2 * n2), lambda k, p: (k, 0, 0))
    return pl.pallas_call(
        _mid_kernel,
        out_shape=jax.ShapeDtypeStruct((P, n1, n2, C), F32),
        grid=(n1 // KCH, P),
        in_specs=[pl.BlockSpec((1, KCH, n2, C), lambda k, p: (p, k, 0, 0)),
                  tab,
                  pl.BlockSpec((1, KCH, n2, C), lambda k, p: (order, k, 0, 0))],
        out_specs=pl.BlockSpec((1, KCH, n2, C), lambda k, p: (p, k, 0, 0)),
        compiler_params=_params(2),
        name="mid",
    )(a, consts["fwd"], hspec)


def _dft3_kernel(b_ref, w_ref, v_ref, g_ref, skip_ref, *rest, h1, chain):
    if chain:
        w1_ref, z_ref, a_ref, slab_ref = rest
        w1 = w1_ref[...]
    else:
        z_ref, slab_ref = rest
    w = w_ref[...]
    skip = skip_ref[0]
    n1 = 2 * h1
    for j in range(SCH):
        slab_ref[...] = b_ref[0, :, j, :]
        rhs = jnp.concatenate(_unpack2(slab_ref[...]), axis=0)
        y = jnp.dot(w, rhs.astype(BF16), preferred_element_type=F32)
        z = [g_ref[0, r, :, j, :] * (y[r * h1:(r + 1) * h1] + v_ref[0, r, :, j, :] * skip) for r in range(2)]
        for r in range(2):
            z_ref[0, r, :, j, :] = z[r]
        if chain:
            res = jnp.dot(w1, jnp.concatenate(z, axis=0).astype(BF16), preferred_element_type=F32)
            a_ref[0, :, j, :] = _pack2(res[:n1], res[n1:])


def _dft3_gate(b5, v, gate, skip, consts, chain):
    P, n1, n2, C = b5.shape
    h1 = n1 // 2
    B, L, _ = v.shape
    five = lambda t: t.reshape(P, 2, h1, n2, C)
    dat = pl.BlockSpec((1, 2, h1, SCH, C), lambda p, j: (p, 0, 0, j, 0))
    packed = pl.BlockSpec((1, n1, SCH, C), lambda p, j: (p, 0, j, 0))
    in_specs = [packed, pl.BlockSpec((n1, 2 * n1), lambda p, j: (0, 0)), dat, dat,
                pl.BlockSpec((1, C), lambda p, j: (0, 0))]
    args = [b5, consts["w3"], five(v), five(gate), skip.reshape(1, C).astype(F32)]
    out_shape = [jax.ShapeDtypeStruct((P, 2, h1, n2, C), F32)]
    out_specs = [dat]
    if chain:
        in_specs.append(pl.BlockSpec((2 * n1, n1), lambda p, j: (0, 0)))
        args.append(consts["w1_cplx"])
        out_shape.append(jax.ShapeDtypeStruct((P, n1, n2, C), F32))
        out_specs.append(packed)
    outs = pl.pallas_call(
        functools.partial(_dft3_kernel, h1=h1, chain=chain),
        out_shape=out_shape,
        grid=(P, n2 // SCH),
        in_specs=in_specs,
        out_specs=out_specs,
        scratch_shapes=[pltpu.VMEM((n1, C), F32)],
        compiler_params=_params(2),
        name="dft3",
    )(*args)
    z = outs[0].reshape(B, L, C)
    return (z, outs[1]) if chain else (z, None)


def _hyena(v, x1, x2, fw1, fb1, ff1, fw2, fb2, ff2, fw3, decay, skip):
    B, L, C = v.shape
    consts = _dft_constants(L)
    kraw, ksum = _filters(L, fw1, fb1, ff1, fw2, fb2, ff2, fw3, decay)
    ks = ksum.reshape(2, 2, C)
    inv_den = (1.0 / (ks[:, 0] + ks[:, 1])).reshape(2, 1, C)
    bwd0 = kraw[0].reshape(2, 2, C)[:, 1].reshape(2, 1, C)
    hspec = _filter_spectrum(_dft1_filter(kraw, consts), inv_den, bwd0, consts)
    gates = (x1, x2)
    z, a5 = v, _dft1_data(v, consts)
    for o, gate in enumerate(gates):
        b5 = _mid(a5, hspec, o, consts)
        z, a5 = _dft3_gate(b5, z, gate, skip[o], consts, chain=o + 1 < len(gates))
    return z


def _merge_kernel(attn_ref, hy_ref, ga_ref, gh_ref, x_ref, mod_ref, wa_ref, wh_ref, wo_ref,
                  g1_ref, b1_ref, rwh_ref, rwl_ref, rb_ref, tri_ref,
                  x1_ref, h2_ref, route_ref, wts_ref, cnt_ref, carry_ref):
    @pl.when((pl.program_id(0) == 0) & (pl.program_id(1) == 0))
    def _():
        carry_ref[...] = jnp.zeros_like(carry_ref)

    logits = _merge_dense(attn_ref, hy_ref, ga_ref, gh_ref, x_ref, mod_ref, wa_ref, wh_ref, wo_ref,
                          g1_ref, b1_ref, rwh_ref, rwl_ref, rb_ref, x1_ref, h2_ref)
    _route_rows(logits, tri_ref, route_ref, wts_ref, carry_ref)
    cnt_ref[...] = carry_ref[...]


def _merge_dense(attn_ref, hy_ref, ga_ref, gh_ref, x_ref, mod_ref, wa_ref, wh_ref, wo_ref,
                 g1_ref, b1_ref, rwh_ref, rwl_ref, rb_ref, x1_ref, h2_ref):
    a = jnp.dot(attn_ref[0], wa_ref[...], preferred_element_type=F32)
    hy = jnp.dot(hy_ref[0].astype(BF16), wh_ref[...], preferred_element_type=F32)
    merged = ga_ref[0].astype(F32) * a + gh_ref[0].astype(F32) * hy
    y = jnp.dot(merged.astype(BF16), wo_ref[...], preferred_element_type=F32)
    gate1 = mod_ref[0, 2:3, :]
    shift2 = mod_ref[0, 3:4, :]
    scale2 = mod_ref[0, 4:5, :]
    x1 = _layer_norm(DN_ALPHA * x_ref[0] + gate1 * y, g1_ref[...], b1_ref[...])
    x1_ref[0] = x1
    h2 = x1 * (1.0 + scale2) + shift2
    half = h2.shape[1] // 2
    h2_ref[0] = _pack2(h2[:, :half], h2[:, half:])
    return _dot3(h2, rwh_ref[...], rwl_ref[...]) + rb_ref[...]


def _route_rows(logits, tri_ref, route_ref, wts_ref, carry_ref):
    tm = logits.shape[0]
    lane = lax.broadcasted_iota(jnp.int32, (tm, LANES), 1)
    lanef = lane.astype(F32)
    big = float(LANES)

    def first_lane(mask):
        return jnp.min(jnp.where(mask, lanef, big), axis=1, keepdims=True).astype(jnp.int32)

    gmask = lane < N_GROUPS
    gl = jnp.where(gmask, logits, NEG)
    gmax = jnp.max(gl, axis=1, keepdims=True)
    gidx = first_lane(gl == gmax)
    pg = 1.0 / jnp.sum(jnp.exp(gl - gmax), axis=1, keepdims=True)
    lo = ROUTE_OFF + gidx * EXPERTS_PER_GROUP
    emask = (lane >= lo) & (lane < lo + EXPERTS_PER_GROUP)
    el = jnp.where(emask, logits, NEG)
    v1 = jnp.max(el, axis=1, keepdims=True)
    i1 = first_lane(el == v1)
    el2 = jnp.where(emask & (lane != i1), logits, NEG)
    v2 = jnp.max(el2, axis=1, keepdims=True)
    i2 = first_lane(el2 == v2)
    e21 = jnp.exp(v2 - v1)
    w1 = pg / (1.0 + e21)
    w2 = pg * e21 / (1.0 + e21)

    sel1 = lane == i1
    sel2 = lane == i2
    onehot = jnp.where(sel1 | sel2, 1.0, 0.0)
    prefix = jnp.dot(tri_ref[...], onehot.astype(BF16), preferred_element_type=F32) + carry_ref[...]
    r1 = jnp.sum(jnp.where(sel1, prefix, 0.0), axis=1, keepdims=True)
    r2 = jnp.sum(jnp.where(sel2, prefix, 0.0), axis=1, keepdims=True)
    carry_ref[...] += jnp.sum(onehot, axis=0, keepdims=True)

    e1 = (i1 - ROUTE_OFF).astype(F32)
    e2 = (i2 - ROUTE_OFF).astype(F32)
    table = jnp.where(lane == 0, e1, jnp.where(lane == 1, e2, jnp.where(lane == 2, r1, jnp.where(lane == 3, r2, 0.0))))
    route_ref[...] = table.T[:SUBLANES].astype(jnp.int32)
    wts_ref[0] = jnp.where(lane == 0, w1, jnp.where(lane == 1, w2, 0.0))


def _merge(attn, hy, ga, gh, x, mod, w_attn_o, w_hy_o, w_out, ln1_g, ln1_b, rg_w, rg_b, re_w, re_b):
    B, S, D = x.shape
    tm = min(512, S)
    spare = LANES - N_GROUPS - N_EXPERTS
    rw = jnp.concatenate([rg_w, re_w, jnp.zeros((D, spare), F32)], axis=1)
    rb = jnp.concatenate([rg_b, re_b, jnp.zeros((spare,), F32)]).reshape(1, LANES)
    rwh, rwl = _split(rw)
    tri = (jnp.arange(tm)[:, None] > jnp.arange(tm)[None, :]).astype(BF16)
    row = lambda b, i: (b, i, 0)
    full = lambda r, c: pl.BlockSpec((r, c), lambda b, i: (0, 0))
    per_b = S // tm
    outs = [jax.ShapeDtypeStruct((B, S, D), F32), jax.ShapeDtypeStruct((B, S, D // 2), F32),
            jax.ShapeDtypeStruct((SUBLANES, B * S), jnp.int32), jax.ShapeDtypeStruct((B, S, LANES), F32),
            jax.ShapeDtypeStruct((1, LANES), F32)]
    return pl.pallas_call(
        _merge_kernel,
        out_shape=outs,
        grid=(B, per_b),
        in_specs=[pl.BlockSpec((1, tm, ATTN_WIDTH), row), pl.BlockSpec((1, tm, HYENA_WIDTH), row),
                  pl.BlockSpec((1, tm, D), row), pl.BlockSpec((1, tm, D), row), pl.BlockSpec((1, tm, D), row),
                  pl.BlockSpec((1, 6, D), lambda b, i: (b, 0, 0)),
                  full(ATTN_WIDTH, D), full(HYENA_WIDTH, D), full(D, D),
                  full(1, D), full(1, D), full(D, LANES), full(D, LANES), full(1, LANES), full(tm, tm)],
        out_specs=[pl.BlockSpec((1, tm, D), row), pl.BlockSpec((1, tm, D // 2), row),
                   pl.BlockSpec((SUBLANES, tm), lambda b, i: (0, b * per_b + i)),
                   pl.BlockSpec((1, tm, LANES), row), full(1, LANES)],
        scratch_shapes=[pltpu.VMEM((1, LANES), F32)],
        compiler_params=_params(2),
        name="merge",
    )(attn, hy, ga, gh, x, mod, w_attn_o.astype(BF16), w_hy_o.astype(BF16), w_out.astype(BF16),
      ln1_g.reshape(1, D), ln1_b.reshape(1, D), rwh, rwl, rb, tri)


SC_ROWS = 64


def _sc_workers():
    info = plsc.get_sparse_core_info()
    return info.num_cores, info.num_cores * info.num_subcores


def _sc_split(n):
    _, workers = _sc_workers()
    per_worker = n // workers
    chunks = per_worker // SC_ROWS
    assert per_worker * workers == n and chunks * SC_ROWS == per_worker and chunks % 2 == 0
    return workers, per_worker, chunks


def _sc_scatter_rows(src, idx0, idx1, n_out):
    n, width = src.shape
    nc, _ = _sc_workers()
    workers, per_worker, chunks = _sc_split(n)
    mesh = plsc.VectorSubcoreMesh(core_axis_name="c", subcore_axis_name="s")

    def body(src_hbm, i0_hbm, i1_hbm, out_hbm, i0_v, i1_v, rows_v, sem, ssem):
        wid = lax.axis_index("s") * nc + lax.axis_index("c")
        base = wid * per_worker
        pltpu.sync_copy(i0_hbm.at[wid], i0_v)
        pltpu.sync_copy(i1_hbm.at[wid], i1_v)

        def load(chunk, buf):
            return pltpu.make_async_copy(src_hbm.at[pl.ds(base + chunk * SC_ROWS, SC_ROWS)], rows_v.at[buf], sem)

        load(0, 0).start()

        @pl.loop(0, chunks, step=2)
        def _(c):
            for b in range(2):
                chunk = c + b
                load(chunk, b).wait()

                @pl.when(chunk + 1 < chunks)
                def _():
                    load(chunk + 1, 1 - b).start()

                first = pltpu.make_async_copy(rows_v.at[b], out_hbm.at[i0_v.at[chunk]], ssem)
                second = pltpu.make_async_copy(rows_v.at[b], out_hbm.at[i1_v.at[chunk]], ssem)
                first.start()
                second.start()
                first.wait()
                second.wait()

    shaped = lambda i: i.reshape(workers, chunks, SC_ROWS)
    return pl.kernel(
        body,
        out_type=jax.ShapeDtypeStruct((n_out, width), src.dtype),
        mesh=mesh,
        scratch_types=[pltpu.VMEM((chunks, SC_ROWS), jnp.int32),
                       pltpu.VMEM((chunks, SC_ROWS), jnp.int32),
                       pltpu.VMEM((2, SC_ROWS, width), src.dtype),
                       pltpu.SemaphoreType.DMA, pltpu.SemaphoreType.DMA],
        name="sc_scatter",
    )(src, shaped(idx0), shaped(idx1))


def _expert_kernel(first_ref, nb_ref, sz_ref, tot_ref, w1_ref, w3_ref, w2_ref, xb_ref, yb_ref,
                   xbuf, ybuf, c1_ref, c3_ref, c2_ref, lsem, ssem):
    e = pl.program_id(0)
    nb = nb_ref[e]
    first = first_ref[e]
    total = tot_ref[0]
    rows = xbuf.shape[1]

    def load(g, slot):
        src = xb_ref.at[pl.ds(pl.multiple_of(g * rows, rows), rows)]
        return pltpu.make_async_copy(src, xbuf.at[slot], lsem.at[slot])

    def store(g, slot):
        dst = yb_ref.at[pl.ds(pl.multiple_of(g * rows, rows), rows)]
        return pltpu.make_async_copy(ybuf.at[slot], dst, ssem.at[slot])

    @pl.when((e == 0) & (total > 0))
    def _():
        load(0, 0).start()

    @pl.when(nb > 0)
    def _():
        c1_ref[...] = w1_ref[0].astype(BF16)
        c3_ref[...] = w3_ref[0].astype(BF16)
        c2_ref[...] = w2_ref[0].astype(BF16)

        def block(j, carry):
            g = first + j
            slot = lax.rem(g, 2)
            load(g, slot).wait()

            @pl.when(g + 1 < total)
            def _():
                load(g + 1, 1 - slot).start()

            @pl.when(g >= 2)
            def _():
                store(g - 2, slot).wait()

            n_valid = sz_ref[e] - j * rows
            pieces = jnp.minimum((n_valid + EXPERT_ROWS - 1) // EXPERT_ROWS, rows // EXPERT_ROWS)

            def swiglu(n):
                rid = lax.broadcasted_iota(jnp.int32, (n, 1), 0)
                xa, xb = _unpack2(jnp.where(rid < n_valid, xbuf[slot, :n, :], 0.0))
                x = jnp.concatenate([xa, xb], axis=1).astype(BF16)
                a = jnp.dot(x, c1_ref[...], preferred_element_type=F32)
                gate = jnp.dot(x, c3_ref[...], preferred_element_type=F32)
                hmid = (a * _sigmoid(a) * gate).astype(BF16)
                y = jnp.dot(hmid, c2_ref[...], preferred_element_type=F32)
                half = y.shape[1] // 2
                ybuf[slot, :n, :] = _pack2(y[:, :half], y[:, half:])

            for q in range(1, rows // EXPERT_ROWS + 1):
                pl.when(pieces == q)(functools.partial(swiglu, q * EXPERT_ROWS))
            store(g, slot).start()
            return carry

        lax.fori_loop(0, nb, block, 0)

    @pl.when(e == pl.num_programs(0) - 1)
    def _():
        for back in (2, 1):
            @pl.when(total >= back)
            def _():
                g = total - back
                store(g, lax.rem(g, 2)).wait()


def _experts(xb, first_blk, n_blk, sizes, w1, w3, w2):
    P, W = xb.shape
    E, D, DE = w1.shape
    total = jnp.sum(n_blk, keepdims=True)
    wspec = lambda r, c: pl.BlockSpec((1, r, c), lambda e, *_: (e, 0, 0))
    grid_spec = pltpu.PrefetchScalarGridSpec(
        num_scalar_prefetch=4,
        grid=(E,),
        in_specs=[wspec(D, DE), wspec(D, DE), wspec(DE, D), pl.BlockSpec(memory_space=pl.ANY)],
        out_specs=pl.BlockSpec(memory_space=pl.ANY),
        scratch_shapes=[pltpu.VMEM((2, MOE_BLOCK, W), F32), pltpu.VMEM((2, MOE_BLOCK, W), F32),
                        pltpu.VMEM((D, DE), BF16), pltpu.VMEM((D, DE), BF16), pltpu.VMEM((DE, D), BF16),
                        pltpu.SemaphoreType.DMA((2,)), pltpu.SemaphoreType.DMA((2,))],
    )
    return pl.pallas_call(
        _expert_kernel,
        out_shape=jax.ShapeDtypeStruct((P, W), F32),
        grid_spec=grid_spec,
        compiler_params=_params(1),
        name="experts",
    )(first_blk, n_blk, sizes, total, w1, w3, w2, xb)


def _sc_gather_rows(table, idx):
    n, width = idx.shape[0], table.shape[1]
    nc, _ = _sc_workers()
    workers, per_worker, chunks = _sc_split(n)
    mesh = plsc.VectorSubcoreMesh(core_axis_name="c", subcore_axis_name="s")

    def body(table_hbm, idx_hbm, out_hbm, idx_v, rows_v, sem):
        wid = lax.axis_index("s") * nc + lax.axis_index("c")
        base = wid * per_worker
        pltpu.sync_copy(idx_hbm.at[wid], idx_v)

        def gather(chunk, buf):
            return pltpu.make_async_copy(table_hbm.at[idx_v.at[chunk]], rows_v.at[buf], sem)

        gather(0, 0).start()

        @pl.loop(0, chunks, step=2)
        def _(c):
            for b in range(2):
                chunk = c + b
                gather(chunk, b).wait()

                @pl.when(chunk + 1 < chunks)
                def _():
                    gather(chunk + 1, 1 - b).start()

                pltpu.sync_copy(rows_v.at[b], out_hbm.at[pl.ds(base + chunk * SC_ROWS, SC_ROWS)])

    return pl.kernel(
        body,
        out_type=jax.ShapeDtypeStruct((n, width), table.dtype),
        mesh=mesh,
        scratch_types=[pltpu.VMEM((chunks, SC_ROWS), jnp.int32),
                       pltpu.VMEM((2, SC_ROWS, width), table.dtype),
                       pltpu.SemaphoreType.DMA],
        name="sc_gather",
    )(table, idx.reshape(workers, chunks, SC_ROWS))


def _combine_dense_kernel(r0_ref, r1_ref, wts_ref, x1_ref, mod_ref, g_ref, b_ref, *rest):
    o_ref = rest[-1]
    w = wts_ref[...]
    y0 = jnp.concatenate(_unpack2(r0_ref[0]), axis=1)
    y1 = jnp.concatenate(_unpack2(r1_ref[0]), axis=1)
    y = w[:, 0:1] * y0 + w[:, 1:2] * y1
    gate2 = mod_ref[0, 5:6, :]
    o_ref[...] = _layer_norm(DN_ALPHA * x1_ref[...] + gate2 * y, g_ref[...], b_ref[...])


def _combine_dense(rows, wts, x1, mod, ln2_g, ln2_b, S, b, out):
    T, D = x1.shape
    tm = min(512, S)
    per_b = S // tm
    here = lambda i: (b * per_b + i, 0)
    in_specs = [pl.BlockSpec((1, tm, rows.shape[2]), lambda i: (0, i, 0)),
                pl.BlockSpec((1, tm, rows.shape[2]), lambda i: (1, i, 0)),
                pl.BlockSpec((tm, LANES), here),
                pl.BlockSpec((tm, D), here),
                pl.BlockSpec((1, 6, D), lambda i: (b, 0, 0)),
                pl.BlockSpec((1, D), lambda i: (0, 0)),
                pl.BlockSpec((1, D), lambda i: (0, 0))]
    args = [rows, rows, wts, x1, mod, ln2_g.reshape(1, D), ln2_b.reshape(1, D)]
    aliases = {}
    if out is not None:
        in_specs.append(pl.BlockSpec(memory_space=pl.ANY))
        aliases = {len(args): 0}
        args.append(out)
    return pl.pallas_call(
        _combine_dense_kernel,
        out_shape=jax.ShapeDtypeStruct((T, D), F32),
        grid=(per_b,),
        in_specs=in_specs,
        out_specs=pl.BlockSpec((tm, D), here),
        input_output_aliases=aliases,
        compiler_params=_params(1),
        name="combine",
    )(*args)


def _moe(h2, x1, route, wts, counts, mod, w1, w3, w2, ln2_g, ln2_b):
    B, S, D = x1.shape
    T = B * S
    P = 2 * T + N_EXPERTS * MOE_BLOCK
    sizes = counts[0, ROUTE_OFF:ROUTE_OFF + N_EXPERTS].astype(jnp.int32)
    n_blk = (sizes + MOE_BLOCK - 1) // MOE_BLOCK
    psizes = n_blk * MOE_BLOCK
    poffs = jnp.cumsum(psizes) - psizes
    sel = route[0:2, None, :] == jnp.arange(N_EXPERTS, dtype=jnp.int32)[None, :, None]
    dest = route[2:4] + jnp.sum(jnp.where(sel, poffs[None, :, None], 0), axis=1)
    xb = _sc_scatter_rows(h2.reshape(T, D // 2), dest[0], dest[1], P)
    yb = _experts(xb, poffs // MOE_BLOCK, n_blk, sizes, w1, w3, w2)
    out = None
    for b in range(B):
        slot_major = dest[:, b * S:(b + 1) * S].reshape(2 * S)
        rows = _sc_gather_rows(yb, slot_major).reshape(2, S, yb.shape[1])
        out = _combine_dense(rows, wts.reshape(T, LANES), x1.reshape(T, D), mod, ln2_g, ln2_b, S, b, out)
    return out.reshape(B, S, D)


def _layer(x, c, w_ada, b_ada, w_in, conv_w, conv_b, fw1, fb1, ff1, fw2, fb2, ff2, fw3, decay, skip,
           w_hy_o, w_attn_o, attn_sink, w_out, ln1_g, ln1_b, rg_w, rg_b, re_w, re_b, ew1, ew3, ew2,
           ln2_g, ln2_b):
    mod = _ada(c, w_ada, b_ada)
    q, kv, hv, hx1, hx2, ga, gh = _in_proj(x, mod, w_in, conv_w, conv_b)
    attn = _attention(q, kv, attn_sink)
    hy = _hyena(hv, hx1, hx2, fw1, fb1, ff1, fw2, fb2, ff2, fw3, decay, skip)
    x1, h2, route, wts, counts = _merge(attn, hy, ga, gh, x, mod, w_attn_o, w_hy_o, w_out,
                                        ln1_g, ln1_b, rg_w, rg_b, re_w, re_b)
    return _moe(h2, x1, route, wts, counts, mod, ew1, ew3, ew2, ln2_g, ln2_b)


def kernel(x, c, w_ada, b_ada, w_in, conv_w, conv_b, filt_w1, filt_b1, filt_freq1, filt_w2, filt_b2, filt_freq2, filt_w3, filt_decay, hy_skip, w_hy_o, w_attn_o, attn_sink, w_out, ln1_g, ln1_b, router_group_w, router_group_b, router_expert_w, router_expert_b, exp_w1, exp_w3, exp_w2, ln2_g, ln2_b):
    for l in range(w_ada.shape[0]):
        x = _layer(x, c, w_ada[l], b_ada[l], w_in[l], conv_w[l], conv_b[l], filt_w1[l], filt_b1[l],
                   filt_freq1[l], filt_w2[l], filt_b2[l], filt_freq2[l], filt_w3[l], filt_decay[l],
                   hy_skip[l], w_hy_o[l], w_attn_o[l], attn_sink[l], w_out[l], ln1_g[l], ln1_b[l],
                   router_group_w[l], router_group_b[l], router_expert_w[l], router_expert_b[l],
                   exp_w1[l], exp_w3[l], exp_w2[l], ln2_g[l], ln2_b[l])
    return x
```

```python
import functools
import math

import numpy as np
import jax
import jax.numpy as jnp
from jax import lax
from jax.experimental import pallas as pl
from jax.experimental.pallas import tpu as pltpu
from jax.experimental.pallas import tpu_sc as plsc

F32 = jnp.float32
BF16 = jnp.bfloat16

N_HEADS = 8
N_KV_HEADS = 2
HEAD_DIM = 64
ATTN_WIDTH = N_HEADS * HEAD_DIM
KV_WIDTH = N_KV_HEADS * HEAD_DIM
WINDOW = 128
HYENA_WIDTH = 512
FILTER_EMB = 33
FILTER_BANDS = (FILTER_EMB - 1) // 2
WINDOW_SHIFT = 0.05
N_GROUPS = 8
EXPERTS_PER_GROUP = 8
N_EXPERTS = N_GROUPS * EXPERTS_PER_GROUP
MOE_BLOCK = 1024
EXPERT_ROWS = 128
LN_EPS = 1e-5
DEPTH = 1
DN_ALPHA = (2.0 * DEPTH) ** 0.25
NEG = -1e30

LANES = 128
SUBLANES = 8
ROUTE_OFF = N_GROUPS
VMEM_LIMIT = 56 * 1024 * 1024


def _params(n_axes, vmem=VMEM_LIMIT):
    return pltpu.CompilerParams(dimension_semantics=("arbitrary",) * n_axes, vmem_limit_bytes=vmem)


def _split(a):
    hi = a.astype(BF16)
    lo = (a - hi.astype(F32)).astype(BF16)
    return hi, lo


def _dot3(a, b_hi, b_lo):
    a_hi, a_lo = _split(a)
    acc = jnp.dot(a_hi, b_hi, preferred_element_type=F32)
    acc = acc + jnp.dot(a_hi, b_lo, preferred_element_type=F32)
    acc = acc + jnp.dot(a_lo, b_hi, preferred_element_type=F32)
    return acc


def _pack2(a, b):
    ia = lax.bitcast_convert_type(a.astype(BF16).astype(F32), jnp.int32)
    ib = lax.bitcast_convert_type(b.astype(BF16).astype(F32), jnp.int32)
    return lax.bitcast_convert_type(ia | lax.shift_right_logical(ib, 16), F32)


def _unpack2(p):
    p = lax.bitcast_convert_type(p, jnp.int32)
    a = lax.bitcast_convert_type(p & jnp.int32(-65536), F32)
    b = lax.bitcast_convert_type(lax.shift_left(p, 16), F32)
    return a, b


def _sigmoid(x):
    return 0.5 * jnp.tanh(0.5 * x) + 0.5


def _layer_norm(r, g, b):
    mu = jnp.mean(r, axis=-1, keepdims=True)
    d = r - mu
    var = jnp.mean(d * d, axis=-1, keepdims=True)
    return d * lax.rsqrt(var + LN_EPS) * g + b


def _ada_kernel(c_ref, w_ref, b_ref, o_ref):
    c = c_ref[...]
    s = c * _sigmoid(c)
    wh, wl = _split(w_ref[...])
    o_ref[...] = _dot3(s, wh, wl) + b_ref[...]


def _ada(c, w_ada, b_ada):
    B, D = c.shape
    n_out = w_ada.shape[1]
    rows = SUBLANES
    cp = jnp.pad(c, ((0, rows - B), (0, 0)))
    tn = 1024
    out = pl.pallas_call(
        _ada_kernel,
        out_shape=jax.ShapeDtypeStruct((rows, n_out), F32),
        grid=(n_out // tn,),
        in_specs=[pl.BlockSpec((rows, D), lambda j: (0, 0)),
                  pl.BlockSpec((D, tn), lambda j: (0, j)),
                  pl.BlockSpec((1, tn), lambda j: (0, j))],
        out_specs=pl.BlockSpec((rows, tn), lambda j: (0, j)),
        compiler_params=_params(1),
        name="ada",
    )(cp, w_ada, b_ada.reshape(1, n_out))
    return out[:B].reshape(B, 6, D)


def _inproj_kernel(x_ref, xp_ref, xn_ref, mod_ref, w_ref, cw_ref, cb_ref,
                   q_ref, kv_ref, v_ref, x1_ref, x2_ref, ga_ref, gh_ref):
    i = pl.program_id(1)
    n = pl.num_programs(1)
    C = HYENA_WIDTH
    x = x_ref[0]
    tm, D = x.shape
    shift = mod_ref[0, 0:1, :]
    scale = mod_ref[0, 1:2, :]
    h = (x * (1.0 + scale) + shift).astype(BF16)

    def seg(lo, hi):
        return jnp.dot(h, w_ref[:, lo:hi], preferred_element_type=F32)

    o_q = 0
    o_kv = o_q + ATTN_WIDTH
    o_hy = o_kv + 2 * KV_WIDTH
    o_ga = o_hy + 3 * C
    o_gh = o_ga + D
    ga_ref[0] = _sigmoid(seg(o_ga, o_ga + D)).astype(BF16)
    gh_ref[0] = _sigmoid(seg(o_gh, o_gh + D)).astype(BF16)

    u = seg(o_hy, o_hy + 3 * C)
    xe = jnp.concatenate([xp_ref[0], xn_ref[0]], axis=0)
    he = (xe * (1.0 + scale) + shift).astype(BF16)
    ue = jnp.dot(he, w_ref[:, o_hy:o_hy + 3 * C], preferred_element_type=F32)
    prow = jnp.where(i > 0, ue[SUBLANES - 1:SUBLANES], 0.0)
    nrow = jnp.where(i < n - 1, ue[SUBLANES:SUBLANES + 1], 0.0)
    rid = lax.broadcasted_iota(jnp.int32, (tm, 1), 0)
    up = jnp.where(rid == 0, prow, pltpu.roll(u, 1, 0))
    dn = jnp.where(rid == tm - 1, nrow, pltpu.roll(u, tm - 1, 0))
    conv = cw_ref[0:1, :] * up + cw_ref[1:2, :] * u + cw_ref[2:3, :] * dn + cb_ref[...]
    v_ref[0] = conv[:, :C]
    x1_ref[0] = conv[:, C:2 * C]
    x2_ref[0] = conv[:, 2 * C:]

    q_ref[0] = (seg(o_q, o_q + ATTN_WIDTH) * (HEAD_DIM ** -0.5)).astype(BF16)
    kv_ref[0] = seg(o_kv, o_kv + 2 * KV_WIDTH).astype(BF16)


def _in_proj(x, mod, w_in, conv_w, conv_b):
    B, S, D = x.shape
    C = HYENA_WIDTH
    tm = min(1024, S)
    r8 = tm // SUBLANES
    nb8 = S // SUBLANES
    wb = w_in.astype(BF16)
    nw = wb.shape[1]
    row = lambda b, i: (b, i, 0)
    shapes = [(ATTN_WIDTH, BF16), (2 * KV_WIDTH, BF16), (C, F32), (C, F32), (C, F32), (D, BF16), (D, BF16)]
    return pl.pallas_call(
        _inproj_kernel,
        out_shape=[jax.ShapeDtypeStruct((B, S, w), dt) for w, dt in shapes],
        grid=(B, S // tm),
        in_specs=[pl.BlockSpec((1, tm, D), row),
                  pl.BlockSpec((1, SUBLANES, D), lambda b, i: (b, jnp.maximum(i * r8 - 1, 0), 0)),
                  pl.BlockSpec((1, SUBLANES, D), lambda b, i: (b, jnp.minimum((i + 1) * r8, nb8 - 1), 0)),
                  pl.BlockSpec((1, 6, D), lambda b, i: (b, 0, 0)),
                  pl.BlockSpec((D, nw), lambda b, i: (0, 0)),
                  pl.BlockSpec((3, 3 * C), lambda b, i: (0, 0)),
                  pl.BlockSpec((1, 3 * C), lambda b, i: (0, 0))],
        out_specs=[pl.BlockSpec((1, tm, w), row) for w, _ in shapes],
        compiler_params=_params(2),
        name="in_proj",
    )(x, x, x, mod, wb, conv_w.astype(F32), conv_b.reshape(1, 3 * C).astype(F32))


ATT_TQ = 512
ATT_QB = 128
ATT_STACK = 4


def _attn_kernel(sink_ref, q_ref, kvp_ref, kvc_ref, kvn_ref, bias_ref, o_ref, kv_scr, vx_scr, *, seq_len):
    i = pl.program_id(1)
    H = WINDOW
    TQ = q_ref.shape[1]
    Q = min(ATT_QB, TQ)
    band = Q + 2 * H
    G = N_HEADS // N_KV_HEADS
    kv_scr[0:H] = kvp_ref[0]
    kv_scr[H:H + TQ] = kvc_ref[0]
    kv_scr[H + TQ:] = kvn_ref[0]
    for kv in range(N_KV_HEADS):
        vx_scr[:, kv * LANES:kv * LANES + HEAD_DIM] = kv_scr[:, KV_WIDTH + kv * HEAD_DIM:KV_WIDTH + (kv + 1) * HEAD_DIM]
        vx_scr[:, kv * LANES + HEAD_DIM:(kv + 1) * LANES] = jnp.ones((TQ + 2 * H, LANES - HEAD_DIM), BF16)
    col = lax.broadcasted_iota(jnp.int32, (1, band), 1)
    rhead = lax.broadcasted_iota(jnp.int32, (ATT_STACK * Q, 1), 0) // Q
    for j in range(TQ // Q):
        kpos = i * TQ + j * Q - H + col
        colbias = jnp.where((kpos >= 0) & (kpos < seq_len), 0.0, NEG)
        for kv in range(N_KV_HEADS):
            kk = kv_scr[j * Q:j * Q + band, kv * HEAD_DIM:(kv + 1) * HEAD_DIM]
            vx = vx_scr[j * Q:j * Q + band, kv * LANES:(kv + 1) * LANES]
            for sub in range(G // ATT_STACK):
                first = sub * ATT_STACK
                heads = [kv * G + first + g for g in range(ATT_STACK)]
                qg = jnp.concatenate([q_ref[0, j * Q:(j + 1) * Q, h * HEAD_DIM:(h + 1) * HEAD_DIM] for h in heads],
                                     axis=0)
                s = lax.dot_general(qg, kk, (((1,), (1,)), ((), ())), preferred_element_type=F32)
                s = s + bias_ref[kv, first * Q:(first + ATT_STACK) * Q, :] + colbias
                snk = sink_ref[heads[-1]]
                for g in range(ATT_STACK - 2, -1, -1):
                    snk = jnp.where(rhead == g, sink_ref[heads[g]], snk)
                m = jnp.maximum(jnp.max(s, axis=1, keepdims=True), snk)
                p = jnp.exp(s - m).astype(BF16)
                ox = jnp.dot(p, vx, preferred_element_type=F32)
                den = ox[:, HEAD_DIM:HEAD_DIM + 1] + jnp.exp(snk - m)
                o = ox[:, :HEAD_DIM] / den
                for g, h in enumerate(heads):
                    o_ref[0, j * Q:(j + 1) * Q, h * HEAD_DIM:(h + 1) * HEAD_DIM] = o[g * Q:(g + 1) * Q].astype(BF16)


def _attention(q, kv, sink):
    B, S, _ = q.shape
    H = WINDOW
    TQ = min(ATT_TQ, S)
    Q = min(ATT_QB, TQ)
    r = TQ // H
    nq = S // H
    G = N_HEADS // N_KV_HEADS
    assert G % ATT_STACK == 0
    a = jnp.arange(Q)[:, None]
    j = jnp.arange(Q + 2 * H)[None, :]
    rel = jnp.abs(j - H - a).astype(F32)
    slopes = 2.0 ** (-8.0 * jnp.arange(1, N_HEADS + 1, dtype=F32) / N_HEADS)
    bias = jnp.where(rel[None] <= WINDOW, -slopes[:, None, None] * rel[None], NEG).astype(F32)
    bias = bias.reshape(N_KV_HEADS, G * Q, Q + 2 * H)
    cur = lambda b, i: (b, i, 0)
    return pl.pallas_call(
        functools.partial(_attn_kernel, seq_len=S),
        out_shape=jax.ShapeDtypeStruct((B, S, ATTN_WIDTH), BF16),
        grid=(B, S // TQ),
        in_specs=[pl.BlockSpec(memory_space=pltpu.SMEM),
                  pl.BlockSpec((1, TQ, ATTN_WIDTH), cur),
                  pl.BlockSpec((1, H, 2 * KV_WIDTH), lambda b, i: (b, jnp.maximum(i * r - 1, 0), 0)),
                  pl.BlockSpec((1, TQ, 2 * KV_WIDTH), cur),
                  pl.BlockSpec((1, H, 2 * KV_WIDTH), lambda b, i: (b, jnp.minimum((i + 1) * r, nq - 1), 0)),
                  pl.BlockSpec((N_KV_HEADS, G * Q, Q + 2 * H), lambda b, i: (0, 0, 0))],
        out_specs=pl.BlockSpec((1, TQ, ATTN_WIDTH), cur),
        scratch_shapes=[pltpu.VMEM((TQ + 2 * H, 2 * KV_WIDTH), BF16),
                        pltpu.VMEM((TQ + 2 * H, N_KV_HEADS * LANES), BF16)],
        compiler_params=_params(2),
        name="attn",
    )(sink.astype(F32), q, kv, kv, kv, bias)


def _filter_kernel(z_ref, w1h, w1l, b1_ref, f1_ref, w2h, w2l, b2_ref, f2_ref, w3h, w3l, dec_ref,
                   k_ref, s_ref):
    i = pl.program_id(0)
    z = z_ref[...]
    h1 = jnp.sin(f1_ref[...] * (_dot3(z, w1h[...], w1l[...]) + b1_ref[...]))
    h2 = jnp.sin(f2_ref[...] * (_dot3(h1, w2h[...], w2l[...]) + b2_ref[...]))
    k = _dot3(h2, w3h[...], w3l[...])
    t = z[:, 0:1]
    k = k * (jnp.exp(-t * jnp.abs(dec_ref[...])) + WINDOW_SHIFT)
    k_ref[...] = k

    @pl.when(i == 0)
    def _():
        s_ref[...] = jnp.zeros_like(s_ref)

    s_ref[...] += jnp.sum(jnp.abs(k), axis=0, keepdims=True)


def _filter_embedding(L):
    t = np.linspace(0.0, 1.0, L, dtype=np.float32).astype(np.float64)[:, None]
    w = (2.0 * math.pi * np.arange(L, dtype=np.float32) / np.float32(L)).astype(np.float64)[:, None]
    bands = np.linspace(1e-4, FILTER_BANDS - 1, FILTER_BANDS, dtype=np.float32).astype(np.float64)[None, :]
    bw = (bands.astype(np.float32) * w.astype(np.float32)).astype(np.float64)
    z = np.concatenate([t, np.cos(bw), -np.sin(bw)], axis=-1)
    zp = np.zeros((L, LANES), np.float32)
    zp[:, :FILTER_EMB] = z.astype(np.float32)
    return jnp.asarray(zp)


def _pad2(a, r, c):
    return jnp.zeros((r, c), F32).at[:a.shape[0], :a.shape[1]].set(a.astype(F32))


def _filters(L, fw1, fb1, ff1, fw2, fb2, ff2, fw3, decay):
    H = LANES
    nf = fw3.shape[1]
    z = _filter_embedding(L)
    w1h, w1l = _split(_pad2(fw1, H, H))
    w2h, w2l = _split(_pad2(fw2, H, H))
    w3h, w3l = _split(_pad2(fw3, H, nf))
    b1 = _pad2(fb1[None], 1, H)
    f1 = _pad2(ff1[None], 1, H)
    b2 = _pad2(fb2[None], 1, H)
    f2 = _pad2(ff2[None], 1, H)
    tr = min(512, L)
    full = lambda r, c: pl.BlockSpec((r, c), lambda i: (0, 0))
    return pl.pallas_call(
        _filter_kernel,
        out_shape=[jax.ShapeDtypeStruct((L, nf), F32), jax.ShapeDtypeStruct((1, nf), F32)],
        grid=(L // tr,),
        in_specs=[pl.BlockSpec((tr, H), lambda i: (i, 0)),
                  full(H, H), full(H, H), full(1, H), full(1, H),
                  full(H, H), full(H, H), full(1, H), full(1, H),
                  full(H, nf), full(H, nf), full(1, nf)],
        out_specs=[pl.BlockSpec((tr, nf), lambda i: (i, 0)), full(1, nf)],
        compiler_params=_params(1),
        name="filter",
    )(z, w1h, w1l, b1, f1, w2h, w2l, b2, f2, w3h, w3l, decay.reshape(1, nf).astype(F32))


def _np_bf16(m64):
    return jnp.asarray(m64.astype(np.float32).astype(BF16))


def _dft_constants(L):
    N = 2 * L
    n2 = LANES
    n1 = N // n2
    h1 = n1 // 2
    k1 = np.arange(n1)[:, None]
    s1 = np.arange(h1)[None, :]
    ang = -2.0 * np.pi * ((k1 * s1) % n1) / n1
    wr, wi = np.cos(ang), np.sin(ang)
    w1_filt = np.block([[wr, wr], [wi, wi], [wr, -wr], [wi, -wi]])
    w1_cplx = np.block([[wr, -wi], [wi, wr]])
    vr, vi = wr.T / N, -wi.T / N
    w3 = np.block([[vr, -vi], [vi, vr]])
    k2 = np.arange(n2)[:, None]
    s2 = np.arange(n2)[None, :]
    a2 = -2.0 * np.pi * ((k2 * s2) % n2) / n2
    w2r, w2i = jnp.asarray(np.cos(a2), F32), jnp.asarray(np.sin(a2), F32)
    at = -2.0 * np.pi * ((np.arange(n1)[:, None] * s2) % N) / N
    twr, twi = jnp.asarray(np.cos(at), F32), jnp.asarray(np.sin(at), F32)
    mr = w2r[None] * twr[:, None, :] - w2i[None] * twi[:, None, :]
    mi = w2r[None] * twi[:, None, :] + w2i[None] * twr[:, None, :]
    fwd = jnp.concatenate([jnp.concatenate([mr, -mi], axis=2),
                           jnp.concatenate([mi, mr], axis=2)], axis=1)
    fwd = fwd.astype(BF16)
    return dict(n1=n1, w1_filt=_np_bf16(w1_filt), w1_cplx=_np_bf16(w1_cplx), w3=_np_bf16(w3),
                fwd=fwd)


SCH = 16


def _dft1_kernel(x_ref, w_ref, a_ref, *, n1):
    w = w_ref[...]
    for j in range(SCH):
        rhs = jnp.concatenate([x_ref[0, 0, :, j, :], x_ref[0, 1, :, j, :]], axis=0)
        res = jnp.dot(w, rhs.astype(BF16), preferred_element_type=F32)
        a_ref[0, :, j, :] = _pack2(res[:n1], res[n1:])


def _dft1_data(x, consts):
    B, L, C = x.shape
    n1 = consts["n1"]
    h1 = n1 // 2
    xv = x.reshape(B // 2, 2, h1, LANES, C)
    return pl.pallas_call(
        functools.partial(_dft1_kernel, n1=n1),
        out_shape=jax.ShapeDtypeStruct((B // 2, n1, LANES, C), F32),
        grid=(B // 2, LANES // SCH),
        in_specs=[pl.BlockSpec((1, 2, h1, SCH, C), lambda p, j: (p, 0, 0, j, 0)),
                  pl.BlockSpec((2 * n1, n1), lambda p, j: (0, 0))],
        out_specs=pl.BlockSpec((1, n1, SCH, C), lambda p, j: (p, 0, j, 0)),
        compiler_params=_params(2),
        name="dft1",
    )(xv, consts["w1_cplx"])


def _dft1f_kernel(x_ref, w_ref, a_ref, *, n1):
    C = HYENA_WIDTH
    w = w_ref[...]
    for j in range(SCH):
        rhs = jnp.concatenate([x_ref[:, j, :C], x_ref[:, j, C:]], axis=0)
        res = jnp.dot(w, rhs.astype(BF16), preferred_element_type=F32)
        a_ref[0, :, 0, j, :] = _pack2(res[:n1], res[n1:2 * n1])
        a_ref[0, :, 1, j, :] = _pack2(res[2 * n1:3 * n1], res[3 * n1:])


def _dft1_filter(kraw, consts):
    L, nf = kraw.shape
    C = HYENA_WIDTH
    n_ord = nf // (2 * C)
    n1 = consts["n1"]
    h1 = n1 // 2
    kv = kraw.reshape(h1, LANES, nf)
    return pl.pallas_call(
        functools.partial(_dft1f_kernel, n1=n1),
        out_shape=jax.ShapeDtypeStruct((n_ord, n1, 2, LANES, C), F32),
        grid=(n_ord, LANES // SCH),
        in_specs=[pl.BlockSpec((h1, SCH, 2 * C), lambda o, j: (0, j, o)),
                  pl.BlockSpec((4 * n1, n1), lambda o, j: (0, 0))],
        out_specs=pl.BlockSpec((1, n1, 2, SCH, C), lambda o, j: (o, 0, 0, j, 0)),
        compiler_params=_params(2),
        name="dft1f",
    )(kv, consts["w1_filt"])


KCH = 16


def _midf_kernel(a_ref, f_ref, inv_ref, b0_ref, h_ref):
    n2 = LANES
    sc = inv_ref[0]
    for k in range(KCH):
        p = jnp.concatenate(_unpack2(a_ref[0, k, :n2, :]), axis=0).astype(BF16)
        q = jnp.concatenate(_unpack2(a_ref[0, k, n2:, :]), axis=0).astype(BF16)
        h_re = jnp.dot(f_ref[k, :n2, :], p, preferred_element_type=F32)
        h_im = jnp.dot(f_ref[k, n2:, :], q, preferred_element_type=F32)
        h_ref[0, k] = _pack2((h_re - b0_ref[0]) * sc, h_im * sc)


def _filter_spectrum(af, inv_den, bwd0, consts):
    n_ord, n1, _, n2, C = af.shape
    a = af.reshape(n_ord, n1, 2 * n2, C)
    tab = pl.BlockSpec((KCH, 2 * n2, 2 * n2), lambda k, o: (k, 0, 0))
    vec = pl.BlockSpec((1, 1, C), lambda k, o: (o, 0, 0))
    return pl.pallas_call(
        _midf_kernel,
        out_shape=jax.ShapeDtypeStruct((n_ord, n1, n2, C), F32),
        grid=(n1 // KCH, n_ord),
        in_specs=[pl.BlockSpec((1, KCH, 2 * n2, C), lambda k, o: (o, k, 0, 0)), tab, vec, vec],
        out_specs=pl.BlockSpec((1, KCH, n2, C), lambda k, o: (o, k, 0, 0)),
        compiler_params=_params(2),
        name="midf",
    )(a, consts["fwd"], inv_den, bwd0)


def _mid_kernel(a_ref, f_ref, h_ref, b_ref):
    n2 = LANES
    for k in range(KCH):
        a = jnp.concatenate(_unpack2(a_ref[0, k]), axis=0).astype(BF16)
        x = jnp.dot(f_ref[k], a, preferred_element_type=F32)
        xr, xi = x[:n2], x[n2:]
        hr, hi = _unpack2(h_ref[0, k])
        y = jnp.concatenate([xr * hr - xi * hi, xr * hi + xi * hr], axis=0)
        b = lax.dot_general(f_ref[k], y.astype(BF16), (((0,), (0,)), ((), ())), preferred_element_type=F32)
        b_ref[0, k] = _pack2(b[:n2], b[n2:])


def _mid(a, hspec, order, consts):
    P, n1, n2, C = a.shape
    tab = pl.BlockSpec((KCH, 2 * n2, 2 * n2), lambda k, p: (k, 0, 0))
    return pl.pallas_call(
        _mid_kernel,
        out_shape=jax.ShapeDtypeStruct((P, n1, n2, C), F32),
        grid=(n1 // KCH, P),
        in_specs=[pl.BlockSpec((1, KCH, n2, C), lambda k, p: (p, k, 0, 0)),
                  tab,
                  pl.BlockSpec((1, KCH, n2, C), lambda k, p: (order, k, 0, 0))],
        out_specs=pl.BlockSpec((1, KCH, n2, C), lambda k, p: (p, k, 0, 0)),
        compiler_params=_params(2),
        name="mid",
    )(a, consts["fwd"], hspec)


def _dft3_kernel(b_ref, w_ref, v_ref, g_ref, skip_ref, *rest, h1, chain):
    if chain:
        w1_ref, z_ref, a_ref, slab_ref = rest
        w1 = w1_ref[...]
    else:
        z_ref, slab_ref = rest
    w = w_ref[...]
    skip = skip_ref[0]
    n1 = 2 * h1
    for j in range(SCH):
        slab_ref[...] = b_ref[0, :, j, :]
        rhs = jnp.concatenate(_unpack2(slab_ref[...]), axis=0)
        y = jnp.dot(w, rhs.astype(BF16), preferred_element_type=F32)
        z = [g_ref[0, r, :, j, :] * (y[r * h1:(r + 1) * h1] + v_ref[0, r, :, j, :] * skip) for r in range(2)]
        for r in range(2):
            z_ref[0, r, :, j, :] = z[r]
        if chain:
            res = jnp.dot(w1, jnp.concatenate(z, axis=0).astype(BF16), preferred_element_type=F32)
            a_ref[0, :, j, :] = _pack2(res[:n1], res[n1:])


def _dft3_gate(b5, v, gate, skip, consts, chain):
    P, n1, n2, C = b5.shape
    h1 = n1 // 2
    B, L, _ = v.shape
    five = lambda t: t.reshape(P, 2, h1, n2, C)
    dat = pl.BlockSpec((1, 2, h1, SCH, C), lambda p, j: (p, 0, 0, j, 0))
    packed = pl.BlockSpec((1, n1, SCH, C), lambda p, j: (p, 0, j, 0))
    in_specs = [packed, pl.BlockSpec((n1, 2 * n1), lambda p, j: (0, 0)), dat, dat,
                pl.BlockSpec((1, C), lambda p, j: (0, 0))]
    args = [b5, consts["w3"], five(v), five(gate), skip.reshape(1, C).astype(F32)]
    out_shape = [jax.ShapeDtypeStruct((P, 2, h1, n2, C), F32)]
    out_specs = [dat]
    if chain:
        in_specs.append(pl.BlockSpec((2 * n1, n1), lambda p, j: (0, 0)))
        args.append(consts["w1_cplx"])
        out_shape.append(jax.ShapeDtypeStruct((P, n1, n2, C), F32))
        out_specs.append(packed)
    outs = pl.pallas_call(
        functools.partial(_dft3_kernel, h1=h1, chain=chain),
        out_shape=out_shape,
        grid=(P, n2 // SCH),
        in_specs=in_specs,
        out_specs=out_specs,
        scratch_shapes=[pltpu.VMEM((n1, C), F32)],
        compiler_params=_params(2),
        name="dft3",
    )(*args)
    z = outs[0].reshape(B, L, C)
    return (z, outs[1]) if chain else (z, None)


def _hyena(v, x1, x2, fw1, fb1, ff1, fw2, fb2, ff2, fw3, decay, skip):
    B, L, C = v.shape
    consts = _dft_constants(L)
    kraw, ksum = _filters(L, fw1, fb1, ff1, fw2, fb2, ff2, fw3, decay)
    ks = ksum.reshape(2, 2, C)
    inv_den = (1.0 / (ks[:, 0] + ks[:, 1])).reshape(2, 1, C)
    bwd0 = kraw[0].reshape(2, 2, C)[:, 1].reshape(2, 1, C)
    hspec = _filter_spectrum(_dft1_filter(kraw, consts), inv_den, bwd0, consts)
    gates = (x1, x2)
    z, a5 = v, _dft1_data(v, consts)
    for o, gate in enumerate(gates):
        b5 = _mid(a5, hspec, o, consts)
        z, a5 = _dft3_gate(b5, z, gate, skip[o], consts, chain=o + 1 < len(gates))
    return z


def _merge_kernel(attn_ref, hy_ref, ga_ref, gh_ref, x_ref, mod_ref, wa_ref, wh_ref, wo_ref,
                  g1_ref, b1_ref, rwh_ref, rwl_ref, rb_ref, tri_ref,
                  x1_ref, h2_ref, route_ref, wts_ref, cnt_ref, carry_ref):
    @pl.when((pl.program_id(0) == 0) & (pl.program_id(1) == 0))
    def _():
        carry_ref[...] = jnp.zeros_like(carry_ref)

    logits = _merge_dense(attn_ref, hy_ref, ga_ref, gh_ref, x_ref, mod_ref, wa_ref, wh_ref, wo_ref,
                          g1_ref, b1_ref, rwh_ref, rwl_ref, rb_ref, x1_ref, h2_ref)
    _route_rows(logits, tri_ref, route_ref, wts_ref, carry_ref)
    cnt_ref[...] = carry_ref[...]


def _merge_dense(attn_ref, hy_ref, ga_ref, gh_ref, x_ref, mod_ref, wa_ref, wh_ref, wo_ref,
                 g1_ref, b1_ref, rwh_ref, rwl_ref, rb_ref, x1_ref, h2_ref):
    a = jnp.dot(attn_ref[0], wa_ref[...], preferred_element_type=F32)
    hy = jnp.dot(hy_ref[0].astype(BF16), wh_ref[...], preferred_element_type=F32)
    merged = ga_ref[0].astype(F32) * a + gh_ref[0].astype(F32) * hy
    y = jnp.dot(merged.astype(BF16), wo_ref[...], preferred_element_type=F32)
    gate1 = mod_ref[0, 2:3, :]
    shift2 = mod_ref[0, 3:4, :]
    scale2 = mod_ref[0, 4:5, :]
    x1 = _layer_norm(DN_ALPHA * x_ref[0] + gate1 * y, g1_ref[...], b1_ref[...])
    x1_ref[0] = x1
    h2 = x1 * (1.0 + scale2) + shift2
    half = h2.shape[1] // 2
    h2_ref[0] = _pack2(h2[:, :half], h2[:, half:])
    return _dot3(h2, rwh_ref[...], rwl_ref[...]) + rb_ref[...]


def _route_rows(logits, tri_ref, route_ref, wts_ref, carry_ref):
    tm = logits.shape[0]
    lane = lax.broadcasted_iota(jnp.int32, (tm, LANES), 1)
    lanef = lane.astype(F32)
    big = float(LANES)

    def first_lane(mask):
        return jnp.min(jnp.where(mask, lanef, big), axis=1, keepdims=True).astype(jnp.int32)

    gmask = lane < N_GROUPS
    gl = jnp.where(gmask, logits, NEG)
    gmax = jnp.max(gl, axis=1, keepdims=True)
    gidx = first_lane(gl == gmax)
    pg = 1.0 / jnp.sum(jnp.exp(gl - gmax), axis=1, keepdims=True)
    lo = ROUTE_OFF + gidx * EXPERTS_PER_GROUP
    emask = (lane >= lo) & (lane < lo + EXPERTS_PER_GROUP)
    el = jnp.where(emask, logits, NEG)
    v1 = jnp.max(el, axis=1, keepdims=True)
    i1 = first_lane(el == v1)
    el2 = jnp.where(emask & (lane != i1), logits, NEG)
    v2 = jnp.max(el2, axis=1, keepdims=True)
    i2 = first_lane(el2 == v2)
    e21 = jnp.exp(v2 - v1)
    w1 = pg / (1.0 + e21)
    w2 = pg * e21 / (1.0 + e21)

    sel1 = lane == i1
    sel2 = lane == i2
    onehot = jnp.where(sel1 | sel2, 1.0, 0.0)
    prefix = jnp.dot(tri_ref[...], onehot.astype(BF16), preferred_element_type=F32) + carry_ref[...]
    r1 = jnp.sum(jnp.where(sel1, prefix, 0.0), axis=1, keepdims=True)
    r2 = jnp.sum(jnp.where(sel2, prefix, 0.0), axis=1, keepdims=True)
    carry_ref[...] += jnp.sum(onehot, axis=0, keepdims=True)

    e1 = (i1 - ROUTE_OFF).astype(F32)
    e2 = (i2 - ROUTE_OFF).astype(F32)
    table = jnp.where(lane == 0, e1, jnp.where(lane == 1, e2, jnp.where(lane == 2, r1, jnp.where(lane == 3, r2, 0.0))))
    route_ref[...] = table.T[:SUBLANES].astype(jnp.int32)
    wts_ref[0] = jnp.where(lane == 0, w1, jnp.where(lane == 1, w2, 0.0))


def _merge(attn, hy, ga, gh, x, mod, w_attn_o, w_hy_o, w_out, ln1_g, ln1_b, rg_w, rg_b, re_w, re_b):
    B, S, D = x.shape
    tm = min(512, S)
    spare = LANES - N_GROUPS - N_EXPERTS
    rw = jnp.concatenate([rg_w, re_w, jnp.zeros((D, spare), F32)], axis=1)
    rb = jnp.concatenate([rg_b, re_b, jnp.zeros((spare,), F32)]).reshape(1, LANES)
    rwh, rwl = _split(rw)
    tri = (jnp.arange(tm)[:, None] > jnp.arange(tm)[None, :]).astype(BF16)
    row = lambda b, i: (b, i, 0)
    full = lambda r, c: pl.BlockSpec((r, c), lambda b, i: (0, 0))
    per_b = S // tm
    outs = [jax.ShapeDtypeStruct((B, S, D), F32), jax.ShapeDtypeStruct((B, S, D // 2), F32),
            jax.ShapeDtypeStruct((SUBLANES, B * S), jnp.int32), jax.ShapeDtypeStruct((B, S, LANES), F32),
            jax.ShapeDtypeStruct((1, LANES), F32)]
    return pl.pallas_call(
        _merge_kernel,
        out_shape=outs,
        grid=(B, per_b),
        in_specs=[pl.BlockSpec((1, tm, ATTN_WIDTH), row), pl.BlockSpec((1, tm, HYENA_WIDTH), row),
                  pl.BlockSpec((1, tm, D), row), pl.BlockSpec((1, tm, D), row), pl.BlockSpec((1, tm, D), row),
                  pl.BlockSpec((1, 6, D), lambda b, i: (b, 0, 0)),
                  full(ATTN_WIDTH, D), full(HYENA_WIDTH, D), full(D, D),
                  full(1, D), full(1, D), full(D, LANES), full(D, LANES), full(1, LANES), full(tm, tm)],
        out_specs=[pl.BlockSpec((1, tm, D), row), pl.BlockSpec((1, tm, D // 2), row),
                   pl.BlockSpec((SUBLANES, tm), lambda b, i: (0, b * per_b + i)),
                   pl.BlockSpec((1, tm, LANES), row), full(1, LANES)],
        scratch_shapes=[pltpu.VMEM((1, LANES), F32)],
        compiler_params=_params(2),
        name="merge",
    )(attn, hy, ga, gh, x, mod, w_attn_o.astype(BF16), w_hy_o.astype(BF16), w_out.astype(BF16),
      ln1_g.reshape(1, D), ln1_b.reshape(1, D), rwh, rwl, rb, tri)


SC_ROWS = 64


def _sc_workers():
    info = plsc.get_sparse_core_info()
    return info.num_cores, info.num_cores * info.num_subcores


def _sc_split(n):
    _, workers = _sc_workers()
    per_worker = n // workers
    chunks = per_worker // SC_ROWS
    assert per_worker * workers == n and chunks * SC_ROWS == per_worker and chunks % 2 == 0
    return workers, per_worker, chunks


def _sc_scatter_rows(src, idx0, idx1, n_out):
    n, width = src.shape
    nc, _ = _sc_workers()
    workers, per_worker, chunks = _sc_split(n)
    mesh = plsc.VectorSubcoreMesh(core_axis_name="c", subcore_axis_name="s")

    def body(src_hbm, i0_hbm, i1_hbm, out_hbm, i0_v, i1_v, rows_v, sem, ssem):
        wid = lax.axis_index("s") * nc + lax.axis_index("c")
        base = wid * per_worker
        pltpu.sync_copy(i0_hbm.at[wid], i0_v)
        pltpu.sync_copy(i1_hbm.at[wid], i1_v)

        def load(chunk, buf):
            return pltpu.make_async_copy(src_hbm.at[pl.ds(base + chunk * SC_ROWS, SC_ROWS)], rows_v.at[buf], sem)

        load(0, 0).start()

        @pl.loop(0, chunks, step=2)
        def _(c):
            for b in range(2):
                chunk = c + b
                load(chunk, b).wait()

                @pl.when(chunk + 1 < chunks)
                def _():
                    load(chunk + 1, 1 - b).start()

                first = pltpu.make_async_copy(rows_v.at[b], out_hbm.at[i0_v.at[chunk]], ssem)
                second = pltpu.make_async_copy(rows_v.at[b], out_hbm.at[i1_v.at[chunk]], ssem)
                first.start()
                second.start()
                first.wait()
                second.wait()

    shaped = lambda i: i.reshape(workers, chunks, SC_ROWS)
    return pl.kernel(
        body,
        out_type=jax.ShapeDtypeStruct((n_out, width), src.dtype),
        mesh=mesh,
        scratch_types=[pltpu.VMEM((chunks, SC_ROWS), jnp.int32),
                       pltpu.VMEM((chunks, SC_ROWS), jnp.int32),
                       pltpu.VMEM((2, SC_ROWS, width), src.dtype),
                       pltpu.SemaphoreType.DMA, pltpu.SemaphoreType.DMA],
        name="sc_scatter",
    )(src, shaped(idx0), shaped(idx1))


def _expert_kernel(first_ref, nb_ref, sz_ref, tot_ref, w1_ref, w3_ref, w2_ref, xb_ref, yb_ref,
                   xbuf, ybuf, c1_ref, c3_ref, c2_ref, lsem, ssem):
    e = pl.program_id(0)
    nb = nb_ref[e]
    first = first_ref[e]
    total = tot_ref[0]
    rows = xbuf.shape[1]

    def load(g, slot):
        src = xb_ref.at[pl.ds(pl.multiple_of(g * rows, rows), rows)]
        return pltpu.make_async_copy(src, xbuf.at[slot], lsem.at[slot])

    def store(g, slot):
        dst = yb_ref.at[pl.ds(pl.multiple_of(g * rows, rows), rows)]
        return pltpu.make_async_copy(ybuf.at[slot], dst, ssem.at[slot])

    @pl.when((e == 0) & (total > 0))
    def _():
        load(0, 0).start()

    @pl.when(nb > 0)
    def _():
        c1_ref[...] = w1_ref[0].astype(BF16)
        c3_ref[...] = w3_ref[0].astype(BF16)
        c2_ref[...] = w2_ref[0].astype(BF16)

        def block(j, carry):
            g = first + j
            slot = lax.rem(g, 2)
            load(g, slot).wait()

            @pl.when(g + 1 < total)
            def _():
                load(g + 1, 1 - slot).start()

            @pl.when(g >= 2)
            def _():
                store(g - 2, slot).wait()

            n_valid = sz_ref[e] - j * rows
            pieces = jnp.minimum((n_valid + EXPERT_ROWS - 1) // EXPERT_ROWS, rows // EXPERT_ROWS)

            def swiglu(n):
                rid = lax.broadcasted_iota(jnp.int32, (n, 1), 0)
                xa, xb = _unpack2(jnp.where(rid < n_valid, xbuf[slot, :n, :], 0.0))
                x = jnp.concatenate([xa, xb], axis=1).astype(BF16)
                a = jnp.dot(x, c1_ref[...], preferred_element_type=F32)
                gate = jnp.dot(x, c3_ref[...], preferred_element_type=F32)
                hmid = (a * _sigmoid(a) * gate).astype(BF16)
                y = jnp.dot(hmid, c2_ref[...], preferred_element_type=F32)
                half = y.shape[1] // 2
                ybuf[slot, :n, :] = _pack2(y[:, :half], y[:, half:])

            for q in range(1, rows // EXPERT_ROWS + 1):
                pl.when(pieces == q)(functools.partial(swiglu, q * EXPERT_ROWS))
            store(g, slot).start()
            return carry

        lax.fori_loop(0, nb, block, 0)

    @pl.when(e == pl.num_programs(0) - 1)
    def _():
        for back in (2, 1):
            @pl.when(total >= back)
            def _():
                g = total - back
                store(g, lax.rem(g, 2)).wait()


def _experts(xb, first_blk, n_blk, sizes, w1, w3, w2):
    P, W = xb.shape
    E, D, DE = w1.shape
    total = jnp.sum(n_blk, keepdims=True)
    wspec = lambda r, c: pl.BlockSpec((1, r, c), lambda e, *_: (e, 0, 0))
    grid_spec = pltpu.PrefetchScalarGridSpec(
        num_scalar_prefetch=4,
        grid=(E,),
        in_specs=[wspec(D, DE), wspec(D, DE), wspec(DE, D), pl.BlockSpec(memory_space=pl.ANY)],
        out_specs=pl.BlockSpec(memory_space=pl.ANY),
        scratch_shapes=[pltpu.VMEM((2, MOE_BLOCK, W), F32), pltpu.VMEM((2, MOE_BLOCK, W), F32),
                        pltpu.VMEM((D, DE), BF16), pltpu.VMEM((D, DE), BF16), pltpu.VMEM((DE, D), BF16),
                        pltpu.SemaphoreType.DMA((2,)), pltpu.SemaphoreType.DMA((2,))],
    )
    return pl.pallas_call(
        _expert_kernel,
        out_shape=jax.ShapeDtypeStruct((P, W), F32),
        grid_spec=grid_spec,
        compiler_params=_params(1),
        name="experts",
    )(first_blk, n_blk, sizes, total, w1, w3, w2, xb)


def _sc_gather_rows(table, idx):
    n, width = idx.shape[0], table.shape[1]
    nc, _ = _sc_workers()
    workers, per_worker, chunks = _sc_split(n)
    mesh = plsc.VectorSubcoreMesh(core_axis_name="c", subcore_axis_name="s")

    def body(table_hbm, idx_hbm, out_hbm, idx_v, rows_v, sem):
        wid = lax.axis_index("s") * nc + lax.axis_index("c")
        base = wid * per_worker
        pltpu.sync_copy(idx_hbm.at[wid], idx_v)

        def gather(chunk, buf):
            return pltpu.make_async_copy(table_hbm.at[idx_v.at[chunk]], rows_v.at[buf], sem)

        gather(0, 0).start()

        @pl.loop(0, chunks, step=2)
        def _(c):
            for b in range(2):
                chunk = c + b
                gather(chunk, b).wait()

                @pl.when(chunk + 1 < chunks)
                def _():
                    gather(chunk + 1, 1 - b).start()

                pltpu.sync_copy(rows_v.at[b], out_hbm.at[pl.ds(base + chunk * SC_ROWS, SC_ROWS)])

    return pl.kernel(
        body,
        out_type=jax.ShapeDtypeStruct((n, width), table.dtype),
        mesh=mesh,
        scratch_types=[pltpu.VMEM((chunks, SC_ROWS), jnp.int32),
                       pltpu.VMEM((2, SC_ROWS, width), table.dtype),
                       pltpu.SemaphoreType.DMA],
        name="sc_gather",
    )(table, idx.reshape(workers, chunks, SC_ROWS))


def _combine_dense_kernel(r0_ref, r1_ref, wts_ref, x1_ref, mod_ref, g_ref, b_ref, *rest):
    o_ref = rest[-1]
    w = wts_ref[...]
    y0 = jnp.concatenate(_unpack2(r0_ref[0]), axis=1)
    y1 = jnp.concatenate(_unpack2(r1_ref[0]), axis=1)
    y = w[:, 0:1] * y0 + w[:, 1:2] * y1
    gate2 = mod_ref[0, 5:6, :]
    o_ref[...] = _layer_norm(DN_ALPHA * x1_ref[...] + gate2 * y, g_ref[...], b_ref[...])


def _combine_dense(rows, wts, x1, mod, ln2_g, ln2_b, S, b, out):
    T, D = x1.shape
    tm = min(512, S)
    per_b = S // tm
    here = lambda i: (b * per_b + i, 0)
    in_specs = [pl.BlockSpec((1, tm, rows.shape[2]), lambda i: (0, i, 0)),
                pl.BlockSpec((1, tm, rows.shape[2]), lambda i: (1, i, 0)),
                pl.BlockSpec((tm, LANES), here),
                pl.BlockSpec((tm, D), here),
                pl.BlockSpec((1, 6, D), lambda i: (b, 0, 0)),
                pl.BlockSpec((1, D), lambda i: (0, 0)),
                pl.BlockSpec((1, D), lambda i: (0, 0))]
    args = [rows, rows, wts, x1, mod, ln2_g.reshape(1, D), ln2_b.reshape(1, D)]
    aliases = {}
    if out is not None:
        in_specs.append(pl.BlockSpec(memory_space=pl.ANY))
        aliases = {len(args): 0}
        args.append(out)
    return pl.pallas_call(
        _combine_dense_kernel,
        out_shape=jax.ShapeDtypeStruct((T, D), F32),
        grid=(per_b,),
        in_specs=in_specs,
        out_specs=pl.BlockSpec((tm, D), here),
        input_output_aliases=aliases,
        compiler_params=_params(1),
        name="combine",
    )(*args)


def _moe(h2, x1, route, wts, counts, mod, w1, w3, w2, ln2_g, ln2_b):
    B, S, D = x1.shape
    T = B * S
    P = 2 * T + N_EXPERTS * MOE_BLOCK
    sizes = counts[0, ROUTE_OFF:ROUTE_OFF + N_EXPERTS].astype(jnp.int32)
    n_blk = (sizes + MOE_BLOCK - 1) // MOE_BLOCK
    psizes = n_blk * MOE_BLOCK
    poffs = jnp.cumsum(psizes) - psizes
    sel = route[0:2, None, :] == jnp.arange(N_EXPERTS, dtype=jnp.int32)[None, :, None]
    dest = route[2:4] + jnp.sum(jnp.where(sel, poffs[None, :, None], 0), axis=1)
    xb = _sc_scatter_rows(h2.reshape(T, D // 2), dest[0], dest[1], P)
    yb = _experts(xb, poffs // MOE_BLOCK, n_blk, sizes, w1, w3, w2)
    out = None
    for b in range(B):
        slot_major = dest[:, b * S:(b + 1) * S].reshape(2 * S)
        rows = _sc_gather_rows(yb, slot_major).reshape(2, S, yb.shape[1])
        out = _combine_dense(rows, wts.reshape(T, LANES), x1.reshape(T, D), mod, ln2_g, ln2_b, S, b, out)
    return out.reshape(B, S, D)


def _layer(x, c, w_ada, b_ada, w_in, conv_w, conv_b, fw1, fb1, ff1, fw2, fb2, ff2, fw3, decay, skip,
           w_hy_o, w_attn_o, attn_sink, w_out, ln1_g, ln1_b, rg_w, rg_b, re_w, re_b, ew1, ew3, ew2,
           ln2_g, ln2_b):
    mod = _ada(c, w_ada, b_ada)
    q, kv, hv, hx1, hx2, ga, gh = _in_proj(x, mod, w_in, conv_w, conv_b)
    attn = _attention(q, kv, attn_sink)
    hy = _hyena(hv, hx1, hx2, fw1, fb1, ff1, fw2, fb2, ff2, fw3, decay, skip)
    x1, h2, route, wts, counts = _merge(attn, hy, ga, gh, x, mod, w_attn_o, w_hy_o, w_out,
                                        ln1_g, ln1_b, rg_w, rg_b, re_w, re_b)
    return _moe(h2, x1, route, wts, counts, mod, ew1, ew3, ew2, ln2_g, ln2_b)


def kernel(x, c, w_ada, b_ada, w_in, conv_w, conv_b, filt_w1, filt_b1, filt_freq1, filt_w2, filt_b2, filt_freq2, filt_w3, filt_decay, hy_skip, w_hy_o, w_attn_o, attn_sink, w_out, ln1_g, ln1_b, router_group_w, router_group_b, router_expert_w, router_expert_b, exp_w1, exp_w3, exp_w2, ln2_g, ln2_b):
    for l in range(w_ada.shape[0]):
        x = _layer(x, c, w_ada[l], b_ada[l], w_in[l], conv_w[l], conv_b[l], filt_w1[l], filt_b1[l],
                   filt_freq1[l], filt_w2[l], filt_b2[l], filt_freq2[l], filt_w3[l], filt_decay[l],
                   hy_skip[l], w_hy_o[l], w_attn_o[l], attn_sink[l], w_out[l], ln1_g[l], ln1_b[l],
                   router_group_w[l], router_group_b[l], router_expert_w[l], router_expert_b[l],
                   exp_w1[l], exp_w3[l], exp_w2[l], ln2_g[l], ln2_b[l])
    return x
```

```python
import functools
import math

import numpy as np
import jax
import jax.numpy as jnp
from jax import lax
from jax.experimental import pallas as pl
from jax.experimental.pallas import tpu as pltpu
from jax.experimental.pallas import tpu_sc as plsc

F32 = jnp.float32
BF16 = jnp.bfloat16

N_HEADS = 8
N_KV_HEADS = 2
HEAD_DIM = 64
ATTN_WIDTH = N_HEADS * HEAD_DIM
KV_WIDTH = N_KV_HEADS * HEAD_DIM
WINDOW = 128
HYENA_WIDTH = 512
FILTER_EMB = 33
FILTER_BANDS = (FILTER_EMB - 1) // 2
WINDOW_SHIFT = 0.05
N_GROUPS = 8
EXPERTS_PER_GROUP = 8
N_EXPERTS = N_GROUPS * EXPERTS_PER_GROUP
MOE_BLOCK = 2048
EXPERT_ROWS = 256
LN_EPS = 1e-5
DEPTH = 1
DN_ALPHA = (2.0 * DEPTH) ** 0.25
NEG = -1e30

LANES = 128
SUBLANES = 8
ROUTE_OFF = N_GROUPS
VMEM_LIMIT = 56 * 1024 * 1024


def _params(n_axes, vmem=VMEM_LIMIT):
    return pltpu.CompilerParams(dimension_semantics=("arbitrary",) * n_axes, vmem_limit_bytes=vmem)


def _split(a):
    hi = a.astype(BF16)
    lo = (a - hi.astype(F32)).astype(BF16)
    return hi, lo


def _dot3(a, b_hi, b_lo):
    a_hi, a_lo = _split(a)
    acc = jnp.dot(a_hi, b_hi, preferred_element_type=F32)
    acc = acc + jnp.dot(a_hi, b_lo, preferred_element_type=F32)
    acc = acc + jnp.dot(a_lo, b_hi, preferred_element_type=F32)
    return acc


def _pack2(a, b):
    ia = lax.bitcast_convert_type(a.astype(BF16).astype(F32), jnp.int32)
    ib = lax.bitcast_convert_type(b.astype(BF16).astype(F32), jnp.int32)
    return lax.bitcast_convert_type(ia | lax.shift_right_logical(ib, 16), F32)


def _unpack2(p):
    p = lax.bitcast_convert_type(p, jnp.int32)
    a = lax.bitcast_convert_type(p & jnp.int32(-65536), F32)
    b = lax.bitcast_convert_type(lax.shift_left(p, 16), F32)
    return a, b


def _sigmoid(x):
    return 0.5 * jnp.tanh(0.5 * x) + 0.5


def _layer_norm(r, g, b):
    mu = jnp.mean(r, axis=-1, keepdims=True)
    d = r - mu
    var = jnp.mean(d * d, axis=-1, keepdims=True)
    return d * lax.rsqrt(var + LN_EPS) * g + b


def _ada_kernel(c_ref, w_ref, b_ref, o_ref):
    c = c_ref[...]
    s = c * _sigmoid(c)
    wh, wl = _split(w_ref[...])
    o_ref[...] = _dot3(s, wh, wl) + b_ref[...]


def _ada(c, w_ada, b_ada):
    B, D = c.shape
    n_out = w_ada.shape[1]
    rows = SUBLANES
    cp = jnp.pad(c, ((0, rows - B), (0, 0)))
    tn = 1024
    out = pl.pallas_call(
        _ada_kernel,
        out_shape=jax.ShapeDtypeStruct((rows, n_out), F32),
        grid=(n_out // tn,),
        in_specs=[pl.BlockSpec((rows, D), lambda j: (0, 0)),
                  pl.BlockSpec((D, tn), lambda j: (0, j)),
                  pl.BlockSpec((1, tn), lambda j: (0, j))],
        out_specs=pl.BlockSpec((rows, tn), lambda j: (0, j)),
        compiler_params=_params(1),
        name="ada",
    )(cp, w_ada, b_ada.reshape(1, n_out))
    return out[:B].reshape(B, 6, D)


def _inproj_kernel(x_ref, xp_ref, xn_ref, mod_ref, w_ref, cw_ref, cb_ref,
                   q_ref, kv_ref, v_ref, x1_ref, x2_ref, ga_ref, gh_ref):
    i = pl.program_id(1)
    n = pl.num_programs(1)
    C = HYENA_WIDTH
    x = x_ref[0]
    tm, D = x.shape
    shift = mod_ref[0, 0:1, :]
    scale = mod_ref[0, 1:2, :]
    h = (x * (1.0 + scale) + shift).astype(BF16)

    def seg(lo, hi):
        return jnp.dot(h, w_ref[:, lo:hi], preferred_element_type=F32)

    o_q = 0
    o_kv = o_q + ATTN_WIDTH
    o_hy = o_kv + 2 * KV_WIDTH
    o_ga = o_hy + 3 * C
    o_gh = o_ga + D
    ga_ref[0] = _sigmoid(seg(o_ga, o_ga + D)).astype(BF16)
    gh_ref[0] = _sigmoid(seg(o_gh, o_gh + D)).astype(BF16)

    u = seg(o_hy, o_hy + 3 * C)
    xe = jnp.concatenate([xp_ref[0], xn_ref[0]], axis=0)
    he = (xe * (1.0 + scale) + shift).astype(BF16)
    ue = jnp.dot(he, w_ref[:, o_hy:o_hy + 3 * C], preferred_element_type=F32)
    prow = jnp.where(i > 0, ue[SUBLANES - 1:SUBLANES], 0.0)
    nrow = jnp.where(i < n - 1, ue[SUBLANES:SUBLANES + 1], 0.0)
    rid = lax.broadcasted_iota(jnp.int32, (tm, 1), 0)
    up = jnp.where(rid == 0, prow, pltpu.roll(u, 1, 0))
    dn = jnp.where(rid == tm - 1, nrow, pltpu.roll(u, tm - 1, 0))
    conv = cw_ref[0:1, :] * up + cw_ref[1:2, :] * u + cw_ref[2:3, :] * dn + cb_ref[...]
    v_ref[0] = conv[:, :C]
    x1_ref[0] = conv[:, C:2 * C]
    x2_ref[0] = conv[:, 2 * C:]

    q_ref[0] = (seg(o_q, o_q + ATTN_WIDTH) * (HEAD_DIM ** -0.5)).astype(BF16)
    kv_ref[0] = seg(o_kv, o_kv + 2 * KV_WIDTH).astype(BF16)


def _in_proj(x, mod, w_in, conv_w, conv_b):
    B, S, D = x.shape
    C = HYENA_WIDTH
    tm = min(1024, S)
    r8 = tm // SUBLANES
    nb8 = S // SUBLANES
    wb = w_in.astype(BF16)
    nw = wb.shape[1]
    row = lambda b, i: (b, i, 0)
    shapes = [(ATTN_WIDTH, BF16), (2 * KV_WIDTH, BF16), (C, F32), (C, F32), (C, F32), (D, BF16), (D, BF16)]
    return pl.pallas_call(
        _inproj_kernel,
        out_shape=[jax.ShapeDtypeStruct((B, S, w), dt) for w, dt in shapes],
        grid=(B, S // tm),
        in_specs=[pl.BlockSpec((1, tm, D), row),
                  pl.BlockSpec((1, SUBLANES, D), lambda b, i: (b, jnp.maximum(i * r8 - 1, 0), 0)),
                  pl.BlockSpec((1, SUBLANES, D), lambda b, i: (b, jnp.minimum((i + 1) * r8, nb8 - 1), 0)),
                  pl.BlockSpec((1, 6, D), lambda b, i: (b, 0, 0)),
                  pl.BlockSpec((D, nw), lambda b, i: (0, 0)),
                  pl.BlockSpec((3, 3 * C), lambda b, i: (0, 0)),
                  pl.BlockSpec((1, 3 * C), lambda b, i: (0, 0))],
        out_specs=[pl.BlockSpec((1, tm, w), row) for w, _ in shapes],
        compiler_params=_params(2),
        name="in_proj",
    )(x, x, x, mod, wb, conv_w.astype(F32), conv_b.reshape(1, 3 * C).astype(F32))


ATT_TQ = 512
ATT_QB = 128
ATT_STACK = 4


def _attn_kernel(sink_ref, q_ref, kvp_ref, kvc_ref, kvn_ref, bias_ref, o_ref, kv_scr, vx_scr, *, seq_len):
    i = pl.program_id(1)
    H = WINDOW
    TQ = q_ref.shape[1]
    Q = min(ATT_QB, TQ)
    band = Q + 2 * H
    G = N_HEADS // N_KV_HEADS
    kv_scr[0:H] = kvp_ref[0]
    kv_scr[H:H + TQ] = kvc_ref[0]
    kv_scr[H + TQ:] = kvn_ref[0]
    for kv in range(N_KV_HEADS):
        vx_scr[:, kv * LANES:kv * LANES + HEAD_DIM] = kv_scr[:, KV_WIDTH + kv * HEAD_DIM:KV_WIDTH + (kv + 1) * HEAD_DIM]
        vx_scr[:, kv * LANES + HEAD_DIM:(kv + 1) * LANES] = jnp.ones((TQ + 2 * H, LANES - HEAD_DIM), BF16)
    col = lax.broadcasted_iota(jnp.int32, (1, band), 1)
    rhead = lax.broadcasted_iota(jnp.int32, (ATT_STACK * Q, 1), 0) // Q
    for j in range(TQ // Q):
        kpos = i * TQ + j * Q - H + col
        colbias = jnp.where((kpos >= 0) & (kpos < seq_len), 0.0, NEG)
        for kv in range(N_KV_HEADS):
            kk = kv_scr[j * Q:j * Q + band, kv * HEAD_DIM:(kv + 1) * HEAD_DIM]
            vx = vx_scr[j * Q:j * Q + band, kv * LANES:(kv + 1) * LANES]
            for sub in range(G // ATT_STACK):
                first = sub * ATT_STACK
                heads = [kv * G + first + g for g in range(ATT_STACK)]
                qg = jnp.concatenate([q_ref[0, j * Q:(j + 1) * Q, h * HEAD_DIM:(h + 1) * HEAD_DIM] for h in heads],
                                     axis=0)
                s = lax.dot_general(qg, kk, (((1,), (1,)), ((), ())), preferred_element_type=F32)
                s = s + bias_ref[kv, first * Q:(first + ATT_STACK) * Q, :] + colbias
                snk = sink_ref[heads[-1]]
                for g in range(ATT_STACK - 2, -1, -1):
                    snk = jnp.where(rhead == g, sink_ref[heads[g]], snk)
                m = jnp.maximum(jnp.max(s, axis=1, keepdims=True), snk)
                p = jnp.exp(s - m).astype(BF16)
                ox = jnp.dot(p, vx, preferred_element_type=F32)
                den = ox[:, HEAD_DIM:HEAD_DIM + 1] + jnp.exp(snk - m)
                o = ox[:, :HEAD_DIM] / den
                for g, h in enumerate(heads):
                    o_ref[0, j * Q:(j + 1) * Q, h * HEAD_DIM:(h + 1) * HEAD_DIM] = o[g * Q:(g + 1) * Q].astype(BF16)


def _attention(q, kv, sink):
    B, S, _ = q.shape
    H = WINDOW
    TQ = min(ATT_TQ, S)
    Q = min(ATT_QB, TQ)
    r = TQ // H
    nq = S // H
    G = N_HEADS // N_KV_HEADS
    assert G % ATT_STACK == 0
    a = jnp.arange(Q)[:, None]
    j = jnp.arange(Q + 2 * H)[None, :]
    rel = jnp.abs(j - H - a).astype(F32)
    slopes = 2.0 ** (-8.0 * jnp.arange(1, N_HEADS + 1, dtype=F32) / N_HEADS)
    bias = jnp.where(rel[None] <= WINDOW, -slopes[:, None, None] * rel[None], NEG).astype(F32)
    bias = bias.reshape(N_KV_HEADS, G * Q, Q + 2 * H)
    cur = lambda b, i: (b, i, 0)
    return pl.pallas_call(
        functools.partial(_attn_kernel, seq_len=S),
        out_shape=jax.ShapeDtypeStruct((B, S, ATTN_WIDTH), BF16),
        grid=(B, S // TQ),
        in_specs=[pl.BlockSpec(memory_space=pltpu.SMEM),
                  pl.BlockSpec((1, TQ, ATTN_WIDTH), cur),
                  pl.BlockSpec((1, H, 2 * KV_WIDTH), lambda b, i: (b, jnp.maximum(i * r - 1, 0), 0)),
                  pl.BlockSpec((1, TQ, 2 * KV_WIDTH), cur),
                  pl.BlockSpec((1, H, 2 * KV_WIDTH), lambda b, i: (b, jnp.minimum((i + 1) * r, nq - 1), 0)),
                  pl.BlockSpec((N_KV_HEADS, G * Q, Q + 2 * H), lambda b, i: (0, 0, 0))],
        out_specs=pl.BlockSpec((1, TQ, ATTN_WIDTH), cur),
        scratch_shapes=[pltpu.VMEM((TQ + 2 * H, 2 * KV_WIDTH), BF16),
                        pltpu.VMEM((TQ + 2 * H, N_KV_HEADS * LANES), BF16)],
        compiler_params=_params(2),
        name="attn",
    )(sink.astype(F32), q, kv, kv, kv, bias)


def _filter_kernel(z_ref, w1h, w1l, b1_ref, f1_ref, w2h, w2l, b2_ref, f2_ref, w3h, w3l, dec_ref,
                   k_ref, s_ref):
    i = pl.program_id(0)
    z = z_ref[...]
    h1 = jnp.sin(f1_ref[...] * (_dot3(z, w1h[...], w1l[...]) + b1_ref[...]))
    h2 = jnp.sin(f2_ref[...] * (_dot3(h1, w2h[...], w2l[...]) + b2_ref[...]))
    k = _dot3(h2, w3h[...], w3l[...])
    t = z[:, 0:1]
    k = k * (jnp.exp(-t * jnp.abs(dec_ref[...])) + WINDOW_SHIFT)
    k_ref[...] = k

    @pl.when(i == 0)
    def _():
        s_ref[...] = jnp.zeros_like(s_ref)

    s_ref[...] += jnp.sum(jnp.abs(k), axis=0, keepdims=True)


def _filter_embedding(L):
    t = np.linspace(0.0, 1.0, L, dtype=np.float32).astype(np.float64)[:, None]
    w = (2.0 * math.pi * np.arange(L, dtype=np.float32) / np.float32(L)).astype(np.float64)[:, None]
    bands = np.linspace(1e-4, FILTER_BANDS - 1, FILTER_BANDS, dtype=np.float32).astype(np.float64)[None, :]
    bw = (bands.astype(np.float32) * w.astype(np.float32)).astype(np.float64)
    z = np.concatenate([t, np.cos(bw), -np.sin(bw)], axis=-1)
    zp = np.zeros((L, LANES), np.float32)
    zp[:, :FILTER_EMB] = z.astype(np.float32)
    return jnp.asarray(zp)


def _pad2(a, r, c):
    return jnp.zeros((r, c), F32).at[:a.shape[0], :a.shape[1]].set(a.astype(F32))


def _filters(L, fw1, fb1, ff1, fw2, fb2, ff2, fw3, decay):
    H = LANES
    nf = fw3.shape[1]
    z = _filter_embedding(L)
    w1h, w1l = _split(_pad2(fw1, H, H))
    w2h, w2l = _split(_pad2(fw2, H, H))
    w3h, w3l = _split(_pad2(fw3, H, nf))
    b1 = _pad2(fb1[None], 1, H)
    f1 = _pad2(ff1[None], 1, H)
    b2 = _pad2(fb2[None], 1, H)
    f2 = _pad2(ff2[None], 1, H)
    tr = min(512, L)
    full = lambda r, c: pl.BlockSpec((r, c), lambda i: (0, 0))
    return pl.pallas_call(
        _filter_kernel,
        out_shape=[jax.ShapeDtypeStruct((L, nf), F32), jax.ShapeDtypeStruct((1, nf), F32)],
        grid=(L // tr,),
        in_specs=[pl.BlockSpec((tr, H), lambda i: (i, 0)),
                  full(H, H), full(H, H), full(1, H), full(1, H),
                  full(H, H), full(H, H), full(1, H), full(1, H),
                  full(H, nf), full(H, nf), full(1, nf)],
        out_specs=[pl.BlockSpec((tr, nf), lambda i: (i, 0)), full(1, nf)],
        compiler_params=_params(1),
        name="filter",
    )(z, w1h, w1l, b1, f1, w2h, w2l, b2, f2, w3h, w3l, decay.reshape(1, nf).astype(F32))


def _np_bf16(m64):
    return jnp.asarray(m64.astype(np.float32).astype(BF16))


def _dft_constants(L):
    N = 2 * L
    n2 = LANES
    n1 = N // n2
    h1 = n1 // 2
    k1 = np.arange(n1)[:, None]
    s1 = np.arange(h1)[None, :]
    ang = -2.0 * np.pi * ((k1 * s1) % n1) / n1
    wr, wi = np.cos(ang), np.sin(ang)
    w1_filt = np.block([[wr, wr], [wi, wi], [wr, -wr], [wi, -wi]])
    w1_cplx = np.block([[wr, -wi], [wi, wr]])
    vr, vi = wr.T / N, -wi.T / N
    w3 = np.block([[vr, -vi], [vi, vr]])
    k2 = np.arange(n2)[:, None]
    s2 = np.arange(n2)[None, :]
    a2 = -2.0 * np.pi * ((k2 * s2) % n2) / n2
    w2r, w2i = jnp.asarray(np.cos(a2), F32), jnp.asarray(np.sin(a2), F32)
    at = -2.0 * np.pi * ((np.arange(n1)[:, None] * s2) % N) / N
    twr, twi = jnp.asarray(np.cos(at), F32), jnp.asarray(np.sin(at), F32)
    mr = w2r[None] * twr[:, None, :] - w2i[None] * twi[:, None, :]
    mi = w2r[None] * twi[:, None, :] + w2i[None] * twr[:, None, :]
    fwd = jnp.concatenate([jnp.concatenate([mr, -mi], axis=2),
                           jnp.concatenate([mi, mr], axis=2)], axis=1)
    fwd = fwd.astype(BF16)
    return dict(n1=n1, w1_filt=_np_bf16(w1_filt), w1_cplx=_np_bf16(w1_cplx), w3=_np_bf16(w3),
                fwd=fwd)


SCH = 16


def _dft1_kernel(x_ref, w_ref, a_ref, *, n1):
    w = w_ref[...]
    for j in range(SCH):
        rhs = jnp.concatenate([x_ref[0, 0, :, j, :], x_ref[0, 1, :, j, :]], axis=0)
        res = jnp.dot(w, rhs.astype(BF16), preferred_element_type=F32)
        a_ref[0, :, j, :] = _pack2(res[:n1], res[n1:])


def _dft1_data(x, consts):
    B, L, C = x.shape
    n1 = consts["n1"]
    h1 = n1 // 2
    xv = x.reshape(B // 2, 2, h1, LANES, C)
    return pl.pallas_call(
        functools.partial(_dft1_kernel, n1=n1),
        out_shape=jax.ShapeDtypeStruct((B // 2, n1, LANES, C), F32),
        grid=(B // 2, LANES // SCH),
        in_specs=[pl.BlockSpec((1, 2, h1, SCH, C), lambda p, j: (p, 0, 0, j, 0)),
                  pl.BlockSpec((2 * n1, n1), lambda p, j: (0, 0))],
        out_specs=pl.BlockSpec((1, n1, SCH, C), lambda p, j: (p, 0, j, 0)),
        compiler_params=_params(2),
        name="dft1",
    )(xv, consts["w1_cplx"])


def _dft1f_kernel(x_ref, w_ref, a_ref, *, n1):
    C = HYENA_WIDTH
    w = w_ref[...]
    for j in range(SCH):
        rhs = jnp.concatenate([x_ref[:, j, :C], x_ref[:, j, C:]], axis=0)
        res = jnp.dot(w, rhs.astype(BF16), preferred_element_type=F32)
        a_ref[0, :, 0, j, :] = _pack2(res[:n1], res[n1:2 * n1])
        a_ref[0, :, 1, j, :] = _pack2(res[2 * n1:3 * n1], res[3 * n1:])


def _dft1_filter(kraw, consts):
    L, nf = kraw.shape
    C = HYENA_WIDTH
    n_ord = nf // (2 * C)
    n1 = consts["n1"]
    h1 = n1 // 2
    kv = kraw.reshape(h1, LANES, nf)
    return pl.pallas_call(
        functools.partial(_dft1f_kernel, n1=n1),
        out_shape=jax.ShapeDtypeStruct((n_ord, n1, 2, LANES, C), F32),
        grid=(n_ord, LANES // SCH),
        in_specs=[pl.BlockSpec((h1, SCH, 2 * C), lambda o, j: (0, j, o)),
                  pl.BlockSpec((4 * n1, n1), lambda o, j: (0, 0))],
        out_specs=pl.BlockSpec((1, n1, 2, SCH, C), lambda o, j: (o, 0, 0, j, 0)),
        compiler_params=_params(2),
        name="dft1f",
    )(kv, consts["w1_filt"])


KCH = 16


def _midf_kernel(a_ref, f_ref, inv_ref, b0_ref, h_ref):
    n2 = LANES
    sc = inv_ref[0]
    for k in range(KCH):
        p = jnp.concatenate(_unpack2(a_ref[0, k, :n2, :]), axis=0).astype(BF16)
        q = jnp.concatenate(_unpack2(a_ref[0, k, n2:, :]), axis=0).astype(BF16)
        h_re = jnp.dot(f_ref[k, :n2, :], p, preferred_element_type=F32)
        h_im = jnp.dot(f_ref[k, n2:, :], q, preferred_element_type=F32)
        h_ref[0, k] = _pack2((h_re - b0_ref[0]) * sc, h_im * sc)


def _filter_spectrum(af, inv_den, bwd0, consts):
    n_ord, n1, _, n2, C = af.shape
    a = af.reshape(n_ord, n1, 2 * n2, C)
    tab = pl.BlockSpec((KCH, 2 * n2, 2 * n2), lambda k, o: (k, 0, 0))
    vec = pl.BlockSpec((1, 1, C), lambda k, o: (o, 0, 0))
    return pl.pallas_call(
        _midf_kernel,
        out_shape=jax.ShapeDtypeStruct((n_ord, n1, n2, C), F32),
        grid=(n1 // KCH, n_ord),
        in_specs=[pl.BlockSpec((1, KCH, 2 * n2, C), lambda k, o: (o, k, 0, 0)), tab, vec, vec],
        out_specs=pl.BlockSpec((1, KCH, n2, C), lambda k, o: (o, k, 0, 0)),
        compiler_params=_params(2),
        name="midf",
    )(a, consts["fwd"], inv_den, bwd0)


def _mid_kernel(a_ref, f_ref, h_ref, b_ref):
    n2 = LANES
    for k in range(KCH):
        a = jnp.concatenate(_unpack2(a_ref[0, k]), axis=0).astype(BF16)
        x = jnp.dot(f_ref[k], a, preferred_element_type=F32)
        xr, xi = x[:n2], x[n2:]
        hr, hi = _unpack2(h_ref[0, k])
        y = jnp.concatenate([xr * hr - xi * hi, xr * hi + xi * hr], axis=0)
        b = lax.dot_general(f_ref[k], y.astype(BF16), (((0,), (0,)), ((), ())), preferred_element_type=F32)
        b_ref[0, k] = _pack2(b[:n2], b[n2:])


def _mid(a, hspec, order, consts):
    P, n1, n2, C = a.shape
    tab = pl.BlockSpec((KCH, 2 * n2, 2 * n2), lambda k, p: (k, 0, 0))
    return pl.pallas_call(
        _mid_kernel,
        out_shape=jax.ShapeDtypeStruct((P, n1, n2, C), F32),
        grid=(n1 // KCH, P),
        in_specs=[pl.BlockSpec((1, KCH, n2, C), lambda k, p: (p, k, 0, 0)),
                  tab,
                  pl.BlockSpec((1, KCH, n2, C), lambda k, p: (order, k, 0, 0))],
        out_specs=pl.BlockSpec((1, KCH, n2, C), lambda k, p: (p, k, 0, 0)),
        compiler_params=_params(2),
        name="mid",
    )(a, consts["fwd"], hspec)


def _dft3_kernel(b_ref, w_ref, v_ref, g_ref, skip_ref, *rest, h1, chain):
    if chain:
        w1_ref, z_ref, a_ref, slab_ref = rest
        w1 = w1_ref[...]
    else:
        z_ref, slab_ref = rest
    w = w_ref[...]
    skip = skip_ref[0]
    n1 = 2 * h1
    for j in range(SCH):
        slab_ref[...] = b_ref[0, :, j, :]
        rhs = jnp.concatenate(_unpack2(slab_ref[...]), axis=0)
        y = jnp.dot(w, rhs.astype(BF16), preferred_element_type=F32)
        z = [g_ref[0, r, :, j, :] * (y[r * h1:(r + 1) * h1] + v_ref[0, r, :, j, :] * skip) for r in range(2)]
        for r in range(2):
            z_ref[0, r, :, j, :] = z[r]
        if chain:
            res = jnp.dot(w1, jnp.concatenate(z, axis=0).astype(BF16), preferred_element_type=F32)
            a_ref[0, :, j, :] = _pack2(res[:n1], res[n1:])


def _dft3_gate(b5, v, gate, skip, consts, chain):
    P, n1, n2, C = b5.shape
    h1 = n1 // 2
    B, L, _ = v.shape
    five = lambda t: t.reshape(P, 2, h1, n2, C)
    dat = pl.BlockSpec((1, 2, h1, SCH, C), lambda p, j: (p, 0, 0, j, 0))
    packed = pl.BlockSpec((1, n1, SCH, C), lambda p, j: (p, 0, j, 0))
    in_specs = [packed, pl.BlockSpec((n1, 2 * n1), lambda p, j: (0, 0)), dat, dat,
                pl.BlockSpec((1, C), lambda p, j: (0, 0))]
    args = [b5, consts["w3"], five(v), five(gate), skip.reshape(1, C).astype(F32)]
    out_shape = [jax.ShapeDtypeStruct((P, 2, h1, n2, C), F32)]
    out_specs = [dat]
    if chain:
        in_specs.append(pl.BlockSpec((2 * n1, n1), lambda p, j: (0, 0)))
        args.append(consts["w1_cplx"])
        out_shape.append(jax.ShapeDtypeStruct((P, n1, n2, C), F32))
        out_specs.append(packed)
    outs = pl.pallas_call(
        functools.partial(_dft3_kernel, h1=h1, chain=chain),
        out_shape=out_shape,
        grid=(P, n2 // SCH),
        in_specs=in_specs,
        out_specs=out_specs,
        scratch_shapes=[pltpu.VMEM((n1, C), F32)],
        compiler_params=_params(2),
        name="dft3",
    )(*args)
    z = outs[0].reshape(B, L, C)
    return (z, outs[1]) if chain else (z, None)


def _hyena(v, x1, x2, fw1, fb1, ff1, fw2, fb2, ff2, fw3, decay, skip):
    B, L, C = v.shape
    consts = _dft_constants(L)
    kraw, ksum = _filters(L, fw1, fb1, ff1, fw2, fb2, ff2, fw3, decay)
    ks = ksum.reshape(2, 2, C)
    inv_den = (1.0 / (ks[:, 0] + ks[:, 1])).reshape(2, 1, C)
    bwd0 = kraw[0].reshape(2, 2, C)[:, 1].reshape(2, 1, C)
    hspec = _filter_spectrum(_dft1_filter(kraw, consts), inv_den, bwd0, consts)
    gates = (x1, x2)
    z, a5 = v, _dft1_data(v, consts)
    for o, gate in enumerate(gates):
        b5 = _mid(a5, hspec, o, consts)
        z, a5 = _dft3_gate(b5, z, gate, skip[o], consts, chain=o + 1 < len(gates))
    return z


def _merge_kernel(attn_ref, hy_ref, ga_ref, gh_ref, x_ref, mod_ref, wa_ref, wh_ref, wo_ref,
                  g1_ref, b1_ref, rwh_ref, rwl_ref, rb_ref, tri_ref,
                  x1_ref, h2_ref, route_ref, wts_ref, cnt_ref, carry_ref):
    @pl.when((pl.program_id(0) == 0) & (pl.program_id(1) == 0))
    def _():
        carry_ref[...] = jnp.zeros_like(carry_ref)

    logits = _merge_dense(attn_ref, hy_ref, ga_ref, gh_ref, x_ref, mod_ref, wa_ref, wh_ref, wo_ref,
                          g1_ref, b1_ref, rwh_ref, rwl_ref, rb_ref, x1_ref, h2_ref)
    _route_rows(logits, tri_ref, route_ref, wts_ref, carry_ref)
    cnt_ref[...] = carry_ref[...]


def _merge_dense(attn_ref, hy_ref, ga_ref, gh_ref, x_ref, mod_ref, wa_ref, wh_ref, wo_ref,
                 g1_ref, b1_ref, rwh_ref, rwl_ref, rb_ref, x1_ref, h2_ref):
    a = jnp.dot(attn_ref[0], wa_ref[...], preferred_element_type=F32)
    hy = jnp.dot(hy_ref[0].astype(BF16), wh_ref[...], preferred_element_type=F32)
    merged = ga_ref[0].astype(F32) * a + gh_ref[0].astype(F32) * hy
    y = jnp.dot(merged.astype(BF16), wo_ref[...], preferred_element_type=F32)
    gate1 = mod_ref[0, 2:3, :]
    shift2 = mod_ref[0, 3:4, :]
    scale2 = mod_ref[0, 4:5, :]
    x1 = _layer_norm(DN_ALPHA * x_ref[0] + gate1 * y, g1_ref[...], b1_ref[...])
    x1_ref[0] = x1
    h2 = x1 * (1.0 + scale2) + shift2
    half = h2.shape[1] // 2
    h2_ref[0] = _pack2(h2[:, :half], h2[:, half:])
    return _dot3(h2, rwh_ref[...], rwl_ref[...]) + rb_ref[...]


def _route_rows(logits, tri_ref, route_ref, wts_ref, carry_ref):
    tm = logits.shape[0]
    lane = lax.broadcasted_iota(jnp.int32, (tm, LANES), 1)
    lanef = lane.astype(F32)
    big = float(LANES)

    def first_lane(mask):
        return jnp.min(jnp.where(mask, lanef, big), axis=1, keepdims=True).astype(jnp.int32)

    gmask = lane < N_GROUPS
    gl = jnp.where(gmask, logits, NEG)
    gmax = jnp.max(gl, axis=1, keepdims=True)
    gidx = first_lane(gl == gmax)
    pg = 1.0 / jnp.sum(jnp.exp(gl - gmax), axis=1, keepdims=True)
    lo = ROUTE_OFF + gidx * EXPERTS_PER_GROUP
    emask = (lane >= lo) & (lane < lo + EXPERTS_PER_GROUP)
    el = jnp.where(emask, logits, NEG)
    v1 = jnp.max(el, axis=1, keepdims=True)
    i1 = first_lane(el == v1)
    el2 = jnp.where(emask & (lane != i1), logits, NEG)
    v2 = jnp.max(el2, axis=1, keepdims=True)
    i2 = first_lane(el2 == v2)
    e21 = jnp.exp(v2 - v1)
    w1 = pg / (1.0 + e21)
    w2 = pg * e21 / (1.0 + e21)

    sel1 = lane == i1
    sel2 = lane == i2
    onehot = jnp.where(sel1 | sel2, 1.0, 0.0)
    prefix = jnp.dot(tri_ref[...], onehot.astype(BF16), preferred_element_type=F32) + carry_ref[...]
    r1 = jnp.sum(jnp.where(sel1, prefix, 0.0), axis=1, keepdims=True)
    r2 = jnp.sum(jnp.where(sel2, prefix, 0.0), axis=1, keepdims=True)
    carry_ref[...] += jnp.sum(onehot, axis=0, keepdims=True)

    e1 = (i1 - ROUTE_OFF).astype(F32)
    e2 = (i2 - ROUTE_OFF).astype(F32)
    table = jnp.where(lane == 0, e1, jnp.where(lane == 1, e2, jnp.where(lane == 2, r1, jnp.where(lane == 3, r2, 0.0))))
    route_ref[...] = table.T[:SUBLANES].astype(jnp.int32)
    wts_ref[0] = jnp.where(lane == 0, w1, jnp.where(lane == 1, w2, 0.0))


def _merge(attn, hy, ga, gh, x, mod, w_attn_o, w_hy_o, w_out, ln1_g, ln1_b, rg_w, rg_b, re_w, re_b):
    B, S, D = x.shape
    tm = min(512, S)
    spare = LANES - N_GROUPS - N_EXPERTS
    rw = jnp.concatenate([rg_w, re_w, jnp.zeros((D, spare), F32)], axis=1)
    rb = jnp.concatenate([rg_b, re_b, jnp.zeros((spare,), F32)]).reshape(1, LANES)
    rwh, rwl = _split(rw)
    tri = (jnp.arange(tm)[:, None] > jnp.arange(tm)[None, :]).astype(BF16)
    row = lambda b, i: (b, i, 0)
    full = lambda r, c: pl.BlockSpec((r, c), lambda b, i: (0, 0))
    per_b = S // tm
    outs = [jax.ShapeDtypeStruct((B, S, D), F32), jax.ShapeDtypeStruct((B, S, D // 2), F32),
            jax.ShapeDtypeStruct((SUBLANES, B * S), jnp.int32), jax.ShapeDtypeStruct((B, S, LANES), F32),
            jax.ShapeDtypeStruct((1, LANES), F32)]
    return pl.pallas_call(
        _merge_kernel,
        out_shape=outs,
        grid=(B, per_b),
        in_specs=[pl.BlockSpec((1, tm, ATTN_WIDTH), row), pl.BlockSpec((1, tm, HYENA_WIDTH), row),
                  pl.BlockSpec((1, tm, D), row), pl.BlockSpec((1, tm, D), row), pl.BlockSpec((1, tm, D), row),
                  pl.BlockSpec((1, 6, D), lambda b, i: (b, 0, 0)),
                  full(ATTN_WIDTH, D), full(HYENA_WIDTH, D), full(D, D),
                  full(1, D), full(1, D), full(D, LANES), full(D, LANES), full(1, LANES), full(tm, tm)],
        out_specs=[pl.BlockSpec((1, tm, D), row), pl.BlockSpec((1, tm, D // 2), row),
                   pl.BlockSpec((SUBLANES, tm), lambda b, i: (0, b * per_b + i)),
                   pl.BlockSpec((1, tm, LANES), row), full(1, LANES)],
        scratch_shapes=[pltpu.VMEM((1, LANES), F32)],
        compiler_params=_params(2),
        name="merge",
    )(attn, hy, ga, gh, x, mod, w_attn_o.astype(BF16), w_hy_o.astype(BF16), w_out.astype(BF16),
      ln1_g.reshape(1, D), ln1_b.reshape(1, D), rwh, rwl, rb, tri)


SC_ROWS = 64


def _sc_workers():
    info = plsc.get_sparse_core_info()
    return info.num_cores, info.num_cores * info.num_subcores


def _sc_split(n):
    _, workers = _sc_workers()
    per_worker = n // workers
    chunks = per_worker // SC_ROWS
    assert per_worker * workers == n and chunks * SC_ROWS == per_worker and chunks % 2 == 0
    return workers, per_worker, chunks


def _sc_scatter_rows(src, idx0, idx1, n_out):
    n, width = src.shape
    nc, _ = _sc_workers()
    workers, per_worker, chunks = _sc_split(n)
    mesh = plsc.VectorSubcoreMesh(core_axis_name="c", subcore_axis_name="s")

    def body(src_hbm, i0_hbm, i1_hbm, out_hbm, i0_v, i1_v, rows_v, sem, ssem):
        wid = lax.axis_index("s") * nc + lax.axis_index("c")
        base = wid * per_worker
        pltpu.sync_copy(i0_hbm.at[wid], i0_v)
        pltpu.sync_copy(i1_hbm.at[wid], i1_v)

        def load(chunk, buf):
            return pltpu.make_async_copy(src_hbm.at[pl.ds(base + chunk * SC_ROWS, SC_ROWS)], rows_v.at[buf], sem)

        load(0, 0).start()

        @pl.loop(0, chunks, step=2)
        def _(c):
            for b in range(2):
                chunk = c + b
                load(chunk, b).wait()

                @pl.when(chunk + 1 < chunks)
                def _():
                    load(chunk + 1, 1 - b).start()

                first = pltpu.make_async_copy(rows_v.at[b], out_hbm.at[i0_v.at[chunk]], ssem)
                second = pltpu.make_async_copy(rows_v.at[b], out_hbm.at[i1_v.at[chunk]], ssem)
                first.start()
                second.start()
                first.wait()
                second.wait()

    shaped = lambda i: i.reshape(workers, chunks, SC_ROWS)
    return pl.kernel(
        body,
        out_type=jax.ShapeDtypeStruct((n_out, width), src.dtype),
        mesh=mesh,
        scratch_types=[pltpu.VMEM((chunks, SC_ROWS), jnp.int32),
                       pltpu.VMEM((chunks, SC_ROWS), jnp.int32),
                       pltpu.VMEM((2, SC_ROWS, width), src.dtype),
                       pltpu.SemaphoreType.DMA, pltpu.SemaphoreType.DMA],
        name="sc_scatter",
    )(src, shaped(idx0), shaped(idx1))


def _expert_kernel(first_ref, nb_ref, sz_ref, tot_ref, w1_ref, w3_ref, w2_ref, xb_ref, yb_ref,
                   xbuf, ybuf, c1_ref, c3_ref, c2_ref, lsem, ssem):
    e = pl.program_id(0)
    nb = nb_ref[e]
    first = first_ref[e]
    total = tot_ref[0]
    rows = xbuf.shape[1]

    def load(g, slot):
        src = xb_ref.at[pl.ds(pl.multiple_of(g * rows, rows), rows)]
        return pltpu.make_async_copy(src, xbuf.at[slot], lsem.at[slot])

    def store(g, slot):
        dst = yb_ref.at[pl.ds(pl.multiple_of(g * rows, rows), rows)]
        return pltpu.make_async_copy(ybuf.at[slot], dst, ssem.at[slot])

    @pl.when((e == 0) & (total > 0))
    def _():
        load(0, 0).start()

    @pl.when(nb > 0)
    def _():
        c1_ref[...] = w1_ref[0].astype(BF16)
        c3_ref[...] = w3_ref[0].astype(BF16)
        c2_ref[...] = w2_ref[0].astype(BF16)

        def block(j, carry):
            g = first + j
            slot = lax.rem(g, 2)
            load(g, slot).wait()

            @pl.when(g + 1 < total)
            def _():
                load(g + 1, 1 - slot).start()

            @pl.when(g >= 2)
            def _():
                store(g - 2, slot).wait()

            n_valid = sz_ref[e] - j * rows
            pieces = jnp.minimum((n_valid + EXPERT_ROWS - 1) // EXPERT_ROWS, rows // EXPERT_ROWS)

            def swiglu(n):
                rid = lax.broadcasted_iota(jnp.int32, (n, 1), 0)
                xa, xb = _unpack2(jnp.where(rid < n_valid, xbuf[slot, :n, :], 0.0))
                x = jnp.concatenate([xa, xb], axis=1).astype(BF16)
                a = jnp.dot(x, c1_ref[...], preferred_element_type=F32)
                gate = jnp.dot(x, c3_ref[...], preferred_element_type=F32)
                hmid = (a * _sigmoid(a) * gate).astype(BF16)
                y = jnp.dot(hmid, c2_ref[...], preferred_element_type=F32)
                half = y.shape[1] // 2
                ybuf[slot, :n, :] = _pack2(y[:, :half], y[:, half:])

            for q in range(1, rows // EXPERT_ROWS + 1):
                pl.when(pieces == q)(functools.partial(swiglu, q * EXPERT_ROWS))
            store(g, slot).start()
            return carry

        lax.fori_loop(0, nb, block, 0)

    @pl.when(e == pl.num_programs(0) - 1)
    def _():
        for back in (2, 1):
            @pl.when(total >= back)
            def _():
                g = total - back
                store(g, lax.rem(g, 2)).wait()


def _experts(xb, first_blk, n_blk, sizes, w1, w3, w2):
    P, W = xb.shape
    E, D, DE = w1.shape
    total = jnp.sum(n_blk, keepdims=True)
    wspec = lambda r, c: pl.BlockSpec((1, r, c), lambda e, *_: (e, 0, 0))
    grid_spec = pltpu.PrefetchScalarGridSpec(
        num_scalar_prefetch=4,
        grid=(E,),
        in_specs=[wspec(D, DE), wspec(D, DE), wspec(DE, D), pl.BlockSpec(memory_space=pl.ANY)],
        out_specs=pl.BlockSpec(memory_space=pl.ANY),
        scratch_shapes=[pltpu.VMEM((2, MOE_BLOCK, W), F32), pltpu.VMEM((2, MOE_BLOCK, W), F32),
                        pltpu.VMEM((D, DE), BF16), pltpu.VMEM((D, DE), BF16), pltpu.VMEM((DE, D), BF16),
                        pltpu.SemaphoreType.DMA((2,)), pltpu.SemaphoreType.DMA((2,))],
    )
    return pl.pallas_call(
        _expert_kernel,
        out_shape=jax.ShapeDtypeStruct((P, W), F32),
        grid_spec=grid_spec,
        compiler_params=_params(1),
        name="experts",
    )(first_blk, n_blk, sizes, total, w1, w3, w2, xb)


def _sc_gather_rows(table, idx):
    n, width = idx.shape[0], table.shape[1]
    nc, _ = _sc_workers()
    workers, per_worker, chunks = _sc_split(n)
    mesh = plsc.VectorSubcoreMesh(core_axis_name="c", subcore_axis_name="s")

    def body(table_hbm, idx_hbm, out_hbm, idx_v, rows_v, sem):
        wid = lax.axis_index("s") * nc + lax.axis_index("c")
        base = wid * per_worker
        pltpu.sync_copy(idx_hbm.at[wid], idx_v)

        def gather(chunk, buf):
            return pltpu.make_async_copy(table_hbm.at[idx_v.at[chunk]], rows_v.at[buf], sem)

        gather(0, 0).start()

        @pl.loop(0, chunks, step=2)
        def _(c):
            for b in range(2):
                chunk = c + b
                gather(chunk, b).wait()

                @pl.when(chunk + 1 < chunks)
                def _():
                    gather(chunk + 1, 1 - b).start()

                pltpu.sync_copy(rows_v.at[b], out_hbm.at[pl.ds(base + chunk * SC_ROWS, SC_ROWS)])

    return pl.kernel(
        body,
        out_type=jax.ShapeDtypeStruct((n, width), table.dtype),
        mesh=mesh,
        scratch_types=[pltpu.VMEM((chunks, SC_ROWS), jnp.int32),
                       pltpu.VMEM((2, SC_ROWS, width), table.dtype),
                       pltpu.SemaphoreType.DMA],
        name="sc_gather",
    )(table, idx.reshape(workers, chunks, SC_ROWS))


def _combine_dense_kernel(r0_ref, r1_ref, wts_ref, x1_ref, mod_ref, g_ref, b_ref, *rest):
    o_ref = rest[-1]
    w = wts_ref[...]
    y0 = jnp.concatenate(_unpack2(r0_ref[0]), axis=1)
    y1 = jnp.concatenate(_unpack2(r1_ref[0]), axis=1)
    y = w[:, 0:1] * y0 + w[:, 1:2] * y1
    gate2 = mod_ref[0, 5:6, :]
    o_ref[...] = _layer_norm(DN_ALPHA * x1_ref[...] + gate2 * y, g_ref[...], b_ref[...])


def _combine_dense(rows, wts, x1, mod, ln2_g, ln2_b, S, b, out):
    T, D = x1.shape
    tm = min(512, S)
    per_b = S // tm
    here = lambda i: (b * per_b + i, 0)
    in_specs = [pl.BlockSpec((1, tm, rows.shape[2]), lambda i: (0, i, 0)),
                pl.BlockSpec((1, tm, rows.shape[2]), lambda i: (1, i, 0)),
                pl.BlockSpec((tm, LANES), here),
                pl.BlockSpec((tm, D), here),
                pl.BlockSpec((1, 6, D), lambda i: (b, 0, 0)),
                pl.BlockSpec((1, D), lambda i: (0, 0)),
                pl.BlockSpec((1, D), lambda i: (0, 0))]
    args = [rows, rows, wts, x1, mod, ln2_g.reshape(1, D), ln2_b.reshape(1, D)]
    aliases = {}
    if out is not None:
        in_specs.append(pl.BlockSpec(memory_space=pl.ANY))
        aliases = {len(args): 0}
        args.append(out)
    return pl.pallas_call(
        _combine_dense_kernel,
        out_shape=jax.ShapeDtypeStruct((T, D), F32),
        grid=(per_b,),
        in_specs=in_specs,
        out_specs=pl.BlockSpec((tm, D), here),
        input_output_aliases=aliases,
        compiler_params=_params(1),
        name="combine",
    )(*args)


def _moe(h2, x1, route, wts, counts, mod, w1, w3, w2, ln2_g, ln2_b):
    B, S, D = x1.shape
    T = B * S
    P = 2 * T + N_EXPERTS * MOE_BLOCK
    sizes = counts[0, ROUTE_OFF:ROUTE_OFF + N_EXPERTS].astype(jnp.int32)
    n_blk = (sizes + MOE_BLOCK - 1) // MOE_BLOCK
    psizes = n_blk * MOE_BLOCK
    poffs = jnp.cumsum(psizes) - psizes
    sel = route[0:2, None, :] == jnp.arange(N_EXPERTS, dtype=jnp.int32)[None, :, None]
    dest = route[2:4] + jnp.sum(jnp.where(sel, poffs[None, :, None], 0), axis=1)
    xb = _sc_scatter_rows(h2.reshape(T, D // 2), dest[0], dest[1], P)
    yb = _experts(xb, poffs // MOE_BLOCK, n_blk, sizes, w1, w3, w2)
    out = None
    for b in range(B):
        slot_major = dest[:, b * S:(b + 1) * S].reshape(2 * S)
        rows = _sc_gather_rows(yb, slot_major).reshape(2, S, yb.shape[1])
        out = _combine_dense(rows, wts.reshape(T, LANES), x1.reshape(T, D), mod, ln2_g, ln2_b, S, b, out)
    return out.reshape(B, S, D)


def _layer(x, c, w_ada, b_ada, w_in, conv_w, conv_b, fw1, fb1, ff1, fw2, fb2, ff2, fw3, decay, skip,
           w_hy_o, w_attn_o, attn_sink, w_out, ln1_g, ln1_b, rg_w, rg_b, re_w, re_b, ew1, ew3, ew2,
           ln2_g, ln2_b):
    mod = _ada(c, w_ada, b_ada)
    q, kv, hv, hx1, hx2, ga, gh = _in_proj(x, mod, w_in, conv_w, conv_b)
    attn = _attention(q, kv, attn_sink)
    hy = _hyena(hv, hx1, hx2, fw1, fb1, ff1, fw2, fb2, ff2, fw3, decay, skip)
    x1, h2, route, wts, counts = _merge(attn, hy, ga, gh, x, mod, w_attn_o, w_hy_o, w_out,
                                        ln1_g, ln1_b, rg_w, rg_b, re_w, re_b)
    return _moe(h2, x1, route, wts, counts, mod, ew1, ew3, ew2, ln2_g, ln2_b)


def kernel(x, c, w_ada, b_ada, w_in, conv_w, conv_b, filt_w1, filt_b1, filt_freq1, filt_w2, filt_b2, filt_freq2, filt_w3, filt_decay, hy_skip, w_hy_o, w_attn_o, attn_sink, w_out, ln1_g, ln1_b, router_group_w, router_group_b, router_expert_w, router_expert_b, exp_w1, exp_w3, exp_w2, ln2_g, ln2_b):
    for l in range(w_ada.shape[0]):
        x = _layer(x, c, w_ada[l], b_ada[l], w_in[l], conv_w[l], conv_b[l], filt_w1[l], filt_b1[l],
                   filt_freq1[l], filt_w2[l], filt_b2[l], filt_freq2[l], filt_w3[l], filt_decay[l],
                   hy_skip[l], w_hy_o[l], w_attn_o[l], attn_sink[l], w_out[l], ln1_g[l], ln1_b[l],
                   router_group_w[l], router_group_b[l], router_expert_w[l], router_expert_b[l],
                   exp_w1[l], exp_w3[l], exp_w2[l], ln2_g[l], ln2_b[l])
    return x
```

```python
import functools
import math

import numpy as np
import jax
import jax.numpy as jnp
from jax import lax
from jax.experimental import pallas as pl
from jax.experimental.pallas import tpu as pltpu
from jax.experimental.pallas import tpu_sc as plsc

F32 = jnp.float32
BF16 = jnp.bfloat16

N_HEADS = 8
N_KV_HEADS = 2
HEAD_DIM = 64
ATTN_WIDTH = N_HEADS * HEAD_DIM
KV_WIDTH = N_KV_HEADS * HEAD_DIM
WINDOW = 128
HYENA_WIDTH = 512
FILTER_EMB = 33
FILTER_BANDS = (FILTER_EMB - 1) // 2
WINDOW_SHIFT = 0.05
N_GROUPS = 8
EXPERTS_PER_GROUP = 8
N_EXPERTS = N_GROUPS * EXPERTS_PER_GROUP
MOE_BLOCK = 1024
EXPERT_ROWS = 128
LN_EPS = 1e-5
DEPTH = 1
DN_ALPHA = (2.0 * DEPTH) ** 0.25
NEG = -1e30

LANES = 128
SUBLANES = 8
ROUTE_OFF = N_GROUPS
VMEM_LIMIT = 56 * 1024 * 1024


def _params(n_axes, vmem=VMEM_LIMIT):
    return pltpu.CompilerParams(dimension_semantics=("arbitrary",) * n_axes, vmem_limit_bytes=vmem)


def _split(a):
    hi = a.astype(BF16)
    lo = (a - hi.astype(F32)).astype(BF16)
    return hi, lo


def _dot3(a, b_hi, b_lo):
    a_hi, a_lo = _split(a)
    acc = jnp.dot(a_hi, b_hi, preferred_element_type=F32)
    acc = acc + jnp.dot(a_hi, b_lo, preferred_element_type=F32)
    acc = acc + jnp.dot(a_lo, b_hi, preferred_element_type=F32)
    return acc


def _pack2(a, b):
    ia = lax.bitcast_convert_type(a.astype(BF16).astype(F32), jnp.int32)
    ib = lax.bitcast_convert_type(b.astype(BF16).astype(F32), jnp.int32)
    return lax.bitcast_convert_type(ia | lax.shift_right_logical(ib, 16), F32)


def _unpack2(p):
    p = lax.bitcast_convert_type(p, jnp.int32)
    a = lax.bitcast_convert_type(p & jnp.int32(-65536), F32)
    b = lax.bitcast_convert_type(lax.shift_left(p, 16), F32)
    return a, b


def _sigmoid(x):
    return 0.5 * jnp.tanh(0.5 * x) + 0.5


def _layer_norm(r, g, b):
    mu = jnp.mean(r, axis=-1, keepdims=True)
    d = r - mu
    var = jnp.mean(d * d, axis=-1, keepdims=True)
    return d * lax.rsqrt(var + LN_EPS) * g + b


def _ada_kernel(c_ref, w_ref, b_ref, o_ref):
    c = c_ref[...]
    s = c * _sigmoid(c)
    wh, wl = _split(w_ref[...])
    o_ref[...] = _dot3(s, wh, wl) + b_ref[...]


def _ada(c, w_ada, b_ada):
    B, D = c.shape
    n_out = w_ada.shape[1]
    rows = SUBLANES
    cp = jnp.pad(c, ((0, rows - B), (0, 0)))
    tn = 1024
    out = pl.pallas_call(
        _ada_kernel,
        out_shape=jax.ShapeDtypeStruct((rows, n_out), F32),
        grid=(n_out // tn,),
        in_specs=[pl.BlockSpec((rows, D), lambda j: (0, 0)),
                  pl.BlockSpec((D, tn), lambda j: (0, j)),
                  pl.BlockSpec((1, tn), lambda j: (0, j))],
        out_specs=pl.BlockSpec((rows, tn), lambda j: (0, j)),
        compiler_params=_params(1),
        name="ada",
    )(cp, w_ada, b_ada.reshape(1, n_out))
    return out[:B].reshape(B, 6, D)


def _inproj_kernel(x_ref, xp_ref, xn_ref, mod_ref, w_ref, cw_ref, cb_ref,
                   q_ref, kv_ref, v_ref, x1_ref, x2_ref, ga_ref, gh_ref):
    i = pl.program_id(1)
    n = pl.num_programs(1)
    C = HYENA_WIDTH
    x = x_ref[0]
    tm, D = x.shape
    shift = mod_ref[0, 0:1, :]
    scale = mod_ref[0, 1:2, :]
    h = (x * (1.0 + scale) + shift).astype(BF16)

    def seg(lo, hi):
        return jnp.dot(h, w_ref[:, lo:hi], preferred_element_type=F32)

    o_q = 0
    o_kv = o_q + ATTN_WIDTH
    o_hy = o_kv + 2 * KV_WIDTH
    o_ga = o_hy + 3 * C
    o_gh = o_ga + D
    ga_ref[0] = _sigmoid(seg(o_ga, o_ga + D)).astype(BF16)
    gh_ref[0] = _sigmoid(seg(o_gh, o_gh + D)).astype(BF16)

    u = seg(o_hy, o_hy + 3 * C)
    xe = jnp.concatenate([xp_ref[0], xn_ref[0]], axis=0)
    he = (xe * (1.0 + scale) + shift).astype(BF16)
    ue = jnp.dot(he, w_ref[:, o_hy:o_hy + 3 * C], preferred_element_type=F32)
    prow = jnp.where(i > 0, ue[SUBLANES - 1:SUBLANES], 0.0)
    nrow = jnp.where(i < n - 1, ue[SUBLANES:SUBLANES + 1], 0.0)
    rid = lax.broadcasted_iota(jnp.int32, (tm, 1), 0)
    up = jnp.where(rid == 0, prow, pltpu.roll(u, 1, 0))
    dn = jnp.where(rid == tm - 1, nrow, pltpu.roll(u, tm - 1, 0))
    conv = cw_ref[0:1, :] * up + cw_ref[1:2, :] * u + cw_ref[2:3, :] * dn + cb_ref[...]
    v_ref[0] = conv[:, :C]
    x1_ref[0] = conv[:, C:2 * C]
    x2_ref[0] = conv[:, 2 * C:]

    q_ref[0] = (seg(o_q, o_q + ATTN_WIDTH) * (HEAD_DIM ** -0.5)).astype(BF16)
    kv_ref[0] = seg(o_kv, o_kv + 2 * KV_WIDTH).astype(BF16)


def _in_proj(x, mod, w_in, conv_w, conv_b):
    B, S, D = x.shape
    C = HYENA_WIDTH
    tm = min(1024, S)
    r8 = tm // SUBLANES
    nb8 = S // SUBLANES
    wb = w_in.astype(BF16)
    nw = wb.shape[1]
    row = lambda b, i: (b, i, 0)
    shapes = [(ATTN_WIDTH, BF16), (2 * KV_WIDTH, BF16), (C, F32), (C, F32), (C, F32), (D, BF16), (D, BF16)]
    return pl.pallas_call(
        _inproj_kernel,
        out_shape=[jax.ShapeDtypeStruct((B, S, w), dt) for w, dt in shapes],
        grid=(B, S // tm),
        in_specs=[pl.BlockSpec((1, tm, D), row),
                  pl.BlockSpec((1, SUBLANES, D), lambda b, i: (b, jnp.maximum(i * r8 - 1, 0), 0)),
                  pl.BlockSpec((1, SUBLANES, D), lambda b, i: (b, jnp.minimum((i + 1) * r8, nb8 - 1), 0)),
                  pl.BlockSpec((1, 6, D), lambda b, i: (b, 0, 0)),
                  pl.BlockSpec((D, nw), lambda b, i: (0, 0)),
                  pl.BlockSpec((3, 3 * C), lambda b, i: (0, 0)),
                  pl.BlockSpec((1, 3 * C), lambda b, i: (0, 0))],
        out_specs=[pl.BlockSpec((1, tm, w), row) for w, _ in shapes],
        compiler_params=_params(2),
        name="in_proj",
    )(x, x, x, mod, wb, conv_w.astype(F32), conv_b.reshape(1, 3 * C).astype(F32))


ATT_TQ = 512
ATT_QB = 128
ATT_STACK = 4


def _attn_kernel(sink_ref, q_ref, kvp_ref, kvc_ref, kvn_ref, bias_ref, o_ref, kv_scr, vx_scr, *, seq_len):
    i = pl.program_id(1)
    H = WINDOW
    TQ = q_ref.shape[1]
    Q = min(ATT_QB, TQ)
    band = Q + 2 * H
    G = N_HEADS // N_KV_HEADS
    kv_scr[0:H] = kvp_ref[0]
    kv_scr[H:H + TQ] = kvc_ref[0]
    kv_scr[H + TQ:] = kvn_ref[0]
    for kv in range(N_KV_HEADS):
        vx_scr[:, kv * LANES:kv * LANES + HEAD_DIM] = kv_scr[:, KV_WIDTH + kv * HEAD_DIM:KV_WIDTH + (kv + 1) * HEAD_DIM]
        vx_scr[:, kv * LANES + HEAD_DIM:(kv + 1) * LANES] = jnp.ones((TQ + 2 * H, LANES - HEAD_DIM), BF16)
    col = lax.broadcasted_iota(jnp.int32, (1, band), 1)
    rhead = lax.broadcasted_iota(jnp.int32, (ATT_STACK * Q, 1), 0) // Q
    for j in range(TQ // Q):
        kpos = i * TQ + j * Q - H + col
        colbias = jnp.where((kpos >= 0) & (kpos < seq_len), 0.0, NEG)
        for kv in range(N_KV_HEADS):
            kk = kv_scr[j * Q:j * Q + band, kv * HEAD_DIM:(kv + 1) * HEAD_DIM]
            vx = vx_scr[j * Q:j * Q + band, kv * LANES:(kv + 1) * LANES]
            for sub in range(G // ATT_STACK):
                first = sub * ATT_STACK
                heads = [kv * G + first + g for g in range(ATT_STACK)]
                qg = jnp.concatenate([q_ref[0, j * Q:(j + 1) * Q, h * HEAD_DIM:(h + 1) * HEAD_DIM] for h in heads],
                                     axis=0)
                s = lax.dot_general(qg, kk, (((1,), (1,)), ((), ())), preferred_element_type=F32)
                s = s + bias_ref[kv, first * Q:(first + ATT_STACK) * Q, :] + colbias
                snk = sink_ref[heads[-1]]
                for g in range(ATT_STACK - 2, -1, -1):
                    snk = jnp.where(rhead == g, sink_ref[heads[g]], snk)
                m = jnp.maximum(jnp.max(s, axis=1, keepdims=True), snk)
                p = jnp.exp(s - m).astype(BF16)
                ox = jnp.dot(p, vx, preferred_element_type=F32)
                den = ox[:, HEAD_DIM:HEAD_DIM + 1] + jnp.exp(snk - m)
                o = ox[:, :HEAD_DIM] / den
                for g, h in enumerate(heads):
                    o_ref[0, j * Q:(j + 1) * Q, h * HEAD_DIM:(h + 1) * HEAD_DIM] = o[g * Q:(g + 1) * Q].astype(BF16)


def _attention(q, kv, sink):
    B, S, _ = q.shape
    H = WINDOW
    TQ = min(ATT_TQ, S)
    Q = min(ATT_QB, TQ)
    r = TQ // H
    nq = S // H
    G = N_HEADS // N_KV_HEADS
    assert G % ATT_STACK == 0
    a = jnp.arange(Q)[:, None]
    j = jnp.arange(Q + 2 * H)[None, :]
    rel = jnp.abs(j - H - a).astype(F32)
    slopes = 2.0 ** (-8.0 * jnp.arange(1, N_HEADS + 1, dtype=F32) / N_HEADS)
    bias = jnp.where(rel[None] <= WINDOW, -slopes[:, None, None] * rel[None], NEG).astype(F32)
    bias = bias.reshape(N_KV_HEADS, G * Q, Q + 2 * H)
    cur = lambda b, i: (b, i, 0)
    return pl.pallas_call(
        functools.partial(_attn_kernel, seq_len=S),
        out_shape=jax.ShapeDtypeStruct((B, S, ATTN_WIDTH), BF16),
        grid=(B, S // TQ),
        in_specs=[pl.BlockSpec(memory_space=pltpu.SMEM),
                  pl.BlockSpec((1, TQ, ATTN_WIDTH), cur),
                  pl.BlockSpec((1, H, 2 * KV_WIDTH), lambda b, i: (b, jnp.maximum(i * r - 1, 0), 0)),
                  pl.BlockSpec((1, TQ, 2 * KV_WIDTH), cur),
                  pl.BlockSpec((1, H, 2 * KV_WIDTH), lambda b, i: (b, jnp.minimum((i + 1) * r, nq - 1), 0)),
                  pl.BlockSpec((N_KV_HEADS, G * Q, Q + 2 * H), lambda b, i: (0, 0, 0))],
        out_specs=pl.BlockSpec((1, TQ, ATTN_WIDTH), cur),
        scratch_shapes=[pltpu.VMEM((TQ + 2 * H, 2 * KV_WIDTH), BF16),
                        pltpu.VMEM((TQ + 2 * H, N_KV_HEADS * LANES), BF16)],
        compiler_params=_params(2),
        name="attn",
    )(sink.astype(F32), q, kv, kv, kv, bias)


def _filter_kernel(z_ref, w1h, w1l, b1_ref, f1_ref, w2h, w2l, b2_ref, f2_ref, w3h, w3l, dec_ref,
                   k_ref, s_ref):
    i = pl.program_id(0)
    z = z_ref[...]
    h1 = jnp.sin(f1_ref[...] * (_dot3(z, w1h[...], w1l[...]) + b1_ref[...]))
    h2 = jnp.sin(f2_ref[...] * (_dot3(h1, w2h[...], w2l[...]) + b2_ref[...]))
    k = _dot3(h2, w3h[...], w3l[...])
    t = z[:, 0:1]
    k = k * (jnp.exp(-t * jnp.abs(dec_ref[...])) + WINDOW_SHIFT)
    k_ref[...] = k

    @pl.when(i == 0)
    def _():
        s_ref[...] = jnp.zeros_like(s_ref)

    s_ref[...] += jnp.sum(jnp.abs(k), axis=0, keepdims=True)


def _filter_embedding(L):
    t = np.linspace(0.0, 1.0, L, dtype=np.float32).astype(np.float64)[:, None]
    w = (2.0 * math.pi * np.arange(L, dtype=np.float32) / np.float32(L)).astype(np.float64)[:, None]
    bands = np.linspace(1e-4, FILTER_BANDS - 1, FILTER_BANDS, dtype=np.float32).astype(np.float64)[None, :]
    bw = (bands.astype(np.float32) * w.astype(np.float32)).astype(np.float64)
    z = np.concatenate([t, np.cos(bw), -np.sin(bw)], axis=-1)
    zp = np.zeros((L, LANES), np.float32)
    zp[:, :FILTER_EMB] = z.astype(np.float32)
    return jnp.asarray(zp)


def _pad2(a, r, c):
    return jnp.zeros((r, c), F32).at[:a.shape[0], :a.shape[1]].set(a.astype(F32))


def _filters(L, fw1, fb1, ff1, fw2, fb2, ff2, fw3, decay):
    H = LANES
    nf = fw3.shape[1]
    z = _filter_embedding(L)
    w1h, w1l = _split(_pad2(fw1, H, H))
    w2h, w2l = _split(_pad2(fw2, H, H))
    w3h, w3l = _split(_pad2(fw3, H, nf))
    b1 = _pad2(fb1[None], 1, H)
    f1 = _pad2(ff1[None], 1, H)
    b2 = _pad2(fb2[None], 1, H)
    f2 = _pad2(ff2[None], 1, H)
    tr = min(512, L)
    full = lambda r, c: pl.BlockSpec((r, c), lambda i: (0, 0))
    return pl.pallas_call(
        _filter_kernel,
        out_shape=[jax.ShapeDtypeStruct((L, nf), F32), jax.ShapeDtypeStruct((1, nf), F32)],
        grid=(L // tr,),
        in_specs=[pl.BlockSpec((tr, H), lambda i: (i, 0)),
                  full(H, H), full(H, H), full(1, H), full(1, H),
                  full(H, H), full(H, H), full(1, H), full(1, H),
                  full(H, nf), full(H, nf), full(1, nf)],
        out_specs=[pl.BlockSpec((tr, nf), lambda i: (i, 0)), full(1, nf)],
        compiler_params=_params(1),
        name="filter",
    )(z, w1h, w1l, b1, f1, w2h, w2l, b2, f2, w3h, w3l, decay.reshape(1, nf).astype(F32))


def _np_bf16(m64):
    return jnp.asarray(m64.astype(np.float32).astype(BF16))


def _dft_constants(L):
    N = 2 * L
    n2 = LANES
    n1 = N // n2
    h1 = n1 // 2
    k1 = np.arange(n1)[:, None]
    s1 = np.arange(h1)[None, :]
    ang = -2.0 * np.pi * ((k1 * s1) % n1) / n1
    wr, wi = np.cos(ang), np.sin(ang)
    w1_filt = np.block([[wr, wr], [wi, wi], [wr, -wr], [wi, -wi]])
    w1_cplx = np.block([[wr, -wi], [wi, wr]])
    vr, vi = wr.T / N, -wi.T / N
    w3 = np.block([[vr, -vi], [vi, vr]])
    k2 = np.arange(n2)[:, None]
    s2 = np.arange(n2)[None, :]
    a2 = -2.0 * np.pi * ((k2 * s2) % n2) / n2
    w2r, w2i = jnp.asarray(np.cos(a2), F32), jnp.asarray(np.sin(a2), F32)
    at = -2.0 * np.pi * ((np.arange(n1)[:, None] * s2) % N) / N
    twr, twi = jnp.asarray(np.cos(at), F32), jnp.asarray(np.sin(at), F32)
    mr = w2r[None] * twr[:, None, :] - w2i[None] * twi[:, None, :]
    mi = w2r[None] * twi[:, None, :] + w2i[None] * twr[:, None, :]
    fwd = jnp.concatenate([jnp.concatenate([mr, -mi], axis=2),
                           jnp.concatenate([mi, mr], axis=2)], axis=1)
    fwd = fwd.astype(BF16)
    return dict(n1=n1, w1_filt=_np_bf16(w1_filt), w1_cplx=_np_bf16(w1_cplx), w3=_np_bf16(w3),
                fwd=fwd)


SCH = 16


def _dft1_kernel(x_ref, w_ref, a_ref, *, n1):
    w = w_ref[...]
    for j in range(SCH):
        rhs = jnp.concatenate([x_ref[0, 0, :, j, :], x_ref[0, 1, :, j, :]], axis=0)
        res = jnp.dot(w, rhs.astype(BF16), preferred_element_type=F32)
        a_ref[0, :, j, :] = _pack2(res[:n1], res[n1:])


def _dft1_data(x, consts):
    B, L, C = x.shape
    n1 = consts["n1"]
    h1 = n1 // 2
    xv = x.reshape(B // 2, 2, h1, LANES, C)
    return pl.pallas_call(
        functools.partial(_dft1_kernel, n1=n1),
        out_shape=jax.ShapeDtypeStruct((B // 2, n1, LANES, C), F32),
        grid=(B // 2, LANES // SCH),
        in_specs=[pl.BlockSpec((1, 2, h1, SCH, C), lambda p, j: (p, 0, 0, j, 0)),
                  pl.BlockSpec((2 * n1, n1), lambda p, j: (0, 0))],
        out_specs=pl.BlockSpec((1, n1, SCH, C), lambda p, j: (p, 0, j, 0)),
        compiler_params=_params(2),
        name="dft1",
    )(xv, consts["w1_cplx"])


def _dft1f_kernel(x_ref, w_ref, a_ref, *, n1):
    C = HYENA_WIDTH
    w = w_ref[...]
    for j in range(SCH):
        rhs = jnp.concatenate([x_ref[:, j, :C], x_ref[:, j, C:]], axis=0)
        res = jnp.dot(w, rhs.astype(BF16), preferred_element_type=F32)
        a_ref[0, :, 0, j, :] = _pack2(res[:n1], res[n1:2 * n1])
        a_ref[0, :, 1, j, :] = _pack2(res[2 * n1:3 * n1], res[3 * n1:])


def _dft1_filter(kraw, consts):
    L, nf = kraw.shape
    C = HYENA_WIDTH
    n_ord = nf // (2 * C)
    n1 = consts["n1"]
    h1 = n1 // 2
    kv = kraw.reshape(h1, LANES, nf)
    return pl.pallas_call(
        functools.partial(_dft1f_kernel, n1=n1),
        out_shape=jax.ShapeDtypeStruct((n_ord, n1, 2, LANES, C), F32),
        grid=(n_ord, LANES // SCH),
        in_specs=[pl.BlockSpec((h1, SCH, 2 * C), lambda o, j: (0, j, o)),
                  pl.BlockSpec((4 * n1, n1), lambda o, j: (0, 0))],
        out_specs=pl.BlockSpec((1, n1, 2, SCH, C), lambda o, j: (o, 0, 0, j, 0)),
        compiler_params=_params(2),
        name="dft1f",
    )(kv, consts["w1_filt"])


KCH = 16


def _midf_kernel(a_ref, f_ref, inv_ref, b0_ref, h_ref):
    n2 = LANES
    sc = inv_ref[0]
    for k in range(KCH):
        p = jnp.concatenate(_unpack2(a_ref[0, k, :n2, :]), axis=0).astype(BF16)
        q = jnp.concatenate(_unpack2(a_ref[0, k, n2:, :]), axis=0).astype(BF16)
        h_re = jnp.dot(f_ref[k, :n2, :], p, preferred_element_type=F32)
        h_im = jnp.dot(f_ref[k, n2:, :], q, preferred_element_type=F32)
        h_ref[0, k] = _pack2((h_re - b0_ref[0]) * sc, h_im * sc)


def _filter_spectrum(af, inv_den, bwd0, consts):
    n_ord, n1, _, n2, C = af.shape
    a = af.reshape(n_ord, n1, 2 * n2, C)
    tab = pl.BlockSpec((KCH, 2 * n2, 2 * n2), lambda k, o: (k, 0, 0))
    vec = pl.BlockSpec((1, 1, C), lambda k, o: (o, 0, 0))
    return pl.pallas_call(
        _midf_kernel,
        out_shape=jax.ShapeDtypeStruct((n_ord, n1, n2, C), F32),
        grid=(n1 // KCH, n_ord),
        in_specs=[pl.BlockSpec((1, KCH, 2 * n2, C), lambda k, o: (o, k, 0, 0)), tab, vec, vec],
        out_specs=pl.BlockSpec((1, KCH, n2, C), lambda k, o: (o, k, 0, 0)),
        compiler_params=_params(2),
        name="midf",
    )(a, consts["fwd"], inv_den, bwd0)


def _mid_kernel(a_ref, f_ref, h_ref, b_ref):
    n2 = LANES
    for k in range(KCH):
        a = jnp.concatenate(_unpack2(a_ref[0, k]), axis=0).astype(BF16)
        x = jnp.dot(f_ref[k], a, preferred_element_type=F32)
        xr, xi = x[:n2], x[n2:]
        hr, hi = _unpack2(h_ref[0, k])
        y = jnp.concatenate([xr * hr - xi * hi, xr * hi + xi * hr], axis=0)
        b = lax.dot_general(f_ref[k], y.astype(BF16), (((0,), (0,)), ((), ())), preferred_element_type=F32)
        b_ref[0, k] = _pack2(b[:n2], b[n2:])


def _mid(a, hspec, order, consts):
    P, n1, n2, C = a.shape
    tab = pl.BlockSpec((KCH, 2 * n2, 2 * n2), lambda k, p: (k, 0, 0))
    return pl.pallas_call(
        _mid_kernel,
        out_shape=jax.ShapeDtypeStruct((P, n1, n2, C), F32),
        grid=(n1 // KCH, P),
        in_specs=[pl.BlockSpec((1, KCH, n2, C), lambda k, p: (p, k, 0, 0)),
                  tab,
                  pl.BlockSpec((1, KCH, n2, C), lambda k, p: (order, k, 0, 0))],
        out_specs=pl.BlockSpec((1, KCH, n2, C), lambda k, p: (p, k, 0, 0)),
        compiler_params=_params(2),
        name="mid",
    )(a, consts["fwd"], hspec)


def _dft3_kernel(b_ref, w_ref, v_ref, g_ref, skip_ref, *rest, h1, chain):
    if chain:
        w1_ref, z_ref, a_ref, slab_ref = rest
        w1 = w1_ref[...]
    else:
        z_ref, slab_ref = rest
    w = w_ref[...]
    skip = skip_ref[0]
    n1 = 2 * h1
    for j in range(SCH):
        slab_ref[...] = b_ref[0, :, j, :]
        rhs = jnp.concatenate(_unpack2(slab_ref[...]), axis=0)
        y = jnp.dot(w, rhs.astype(BF16), preferred_element_type=F32)
        z = [g_ref[0, r, :, j, :] * (y[r * h1:(r + 1) * h1] + v_ref[0, r, :, j, :] * skip) for r in range(2)]
        for r in range(2):
            z_ref[0, r, :, j, :] = z[r]
        if chain:
            res = jnp.dot(w1, jnp.concatenate(z, axis=0).astype(BF16), preferred_element_type=F32)
            a_ref[0, :, j, :] = _pack2(res[:n1], res[n1:])


def _dft3_gate(b5, v, gate, skip, consts, chain):
    P, n1, n2, C = b5.shape
    h1 = n1 // 2
    B, L, _ = v.shape
    five = lambda t: t.reshape(P, 2, h1, n2, C)
    dat = pl.BlockSpec((1, 2, h1, SCH, C), lambda p, j: (p, 0, 0, j, 0))
    packed = pl.BlockSpec((1, n1, SCH, C), lambda p, j: (p, 0, j, 0))
    in_specs = [packed, pl.BlockSpec((n1, 2 * n1), lambda p, j: (0, 0)), dat, dat,
                pl.BlockSpec((1, C), lambda p, j: (0, 0))]
    args = [b5, consts["w3"], five(v), five(gate), skip.reshape(1, C).astype(F32)]
    out_shape = [jax.ShapeDtypeStruct((P, 2, h1, n2, C), F32)]
    out_specs = [dat]
    if chain:
        in_specs.append(pl.BlockSpec((2 * n1, n1), lambda p, j: (0, 0)))
        args.append(consts["w1_cplx"])
        out_shape.append(jax.ShapeDtypeStruct((P, n1, n2, C), F32))
        out_specs.append(packed)
    outs = pl.pallas_call(
        functools.partial(_dft3_kernel, h1=h1, chain=chain),
        out_shape=out_shape,
        grid=(P, n2 // SCH),
        in_specs=in_specs,
        out_specs=out_specs,
        scratch_shapes=[pltpu.VMEM((n1, C), F32)],
        compiler_params=_params(2),
        name="dft3",
    )(*args)
    z = outs[0].reshape(B, L, C)
    return (z, outs[1]) if chain else (z, None)


def _hyena(v, x1, x2, fw1, fb1, ff1, fw2, fb2, ff2, fw3, decay, skip):
    B, L, C = v.shape
    consts = _dft_constants(L)
    kraw, ksum = _filters(L, fw1, fb1, ff1, fw2, fb2, ff2, fw3, decay)
    ks = ksum.reshape(2, 2, C)
    inv_den = (1.0 / (ks[:, 0] + ks[:, 1])).reshape(2, 1, C)
    bwd0 = kraw[0].reshape(2, 2, C)[:, 1].reshape(2, 1, C)
    hspec = _filter_spectrum(_dft1_filter(kraw, consts), inv_den, bwd0, consts)
    gates = (x1, x2)
    z, a5 = v, _dft1_data(v, consts)
    for o, gate in enumerate(gates):
        b5 = _mid(a5, hspec, o, consts)
        z, a5 = _dft3_gate(b5, z, gate, skip[o], consts, chain=o + 1 < len(gates))
    return z


def _merge_kernel(attn_ref, hy_ref, ga_ref, gh_ref, x_ref, mod_ref, wa_ref, wh_ref, wo_ref,
                  g1_ref, b1_ref, rwh_ref, rwl_ref, rb_ref, tri_ref,
                  x1_ref, h2_ref, route_ref, wts_ref, cnt_ref, carry_ref):
    @pl.when((pl.program_id(0) == 0) & (pl.program_id(1) == 0))
    def _():
        carry_ref[...] = jnp.zeros_like(carry_ref)

    logits = _merge_dense(attn_ref, hy_ref, ga_ref, gh_ref, x_ref, mod_ref, wa_ref, wh_ref, wo_ref,
                          g1_ref, b1_ref, rwh_ref, rwl_ref, rb_ref, x1_ref, h2_ref)
    _route_rows(logits, tri_ref, route_ref, wts_ref, carry_ref)
    cnt_ref[...] = carry_ref[...]


def _merge_dense(attn_ref, hy_ref, ga_ref, gh_ref, x_ref, mod_ref, wa_ref, wh_ref, wo_ref,
                 g1_ref, b1_ref, rwh_ref, rwl_ref, rb_ref, x1_ref, h2_ref):
    a = jnp.dot(attn_ref[0], wa_ref[...], preferred_element_type=F32)
    hy = jnp.dot(hy_ref[0].astype(BF16), wh_ref[...], preferred_element_type=F32)
    merged = ga_ref[0].astype(F32) * a + gh_ref[0].astype(F32) * hy
    y = jnp.dot(merged.astype(BF16), wo_ref[...], preferred_element_type=F32)
    gate1 = mod_ref[0, 2:3, :]
    shift2 = mod_ref[0, 3:4, :]
    scale2 = mod_ref[0, 4:5, :]
    x1 = _layer_norm(DN_ALPHA * x_ref[0] + gate1 * y, g1_ref[...], b1_ref[...])
    x1_ref[0] = x1
    h2 = x1 * (1.0 + scale2) + shift2
    half = h2.shape[1] // 2
    h2_ref[0] = _pack2(h2[:, :half], h2[:, half:])
    return _dot3(h2, rwh_ref[...], rwl_ref[...]) + rb_ref[...]


def _route_rows(logits, tri_ref, route_ref, wts_ref, carry_ref):
    tm = logits.shape[0]
    lane = lax.broadcasted_iota(jnp.int32, (tm, LANES), 1)
    lanef = lane.astype(F32)
    big = float(LANES)

    def first_lane(mask):
        return jnp.min(jnp.where(mask, lanef, big), axis=1, keepdims=True).astype(jnp.int32)

    gmask = lane < N_GROUPS
    gl = jnp.where(gmask, logits, NEG)
    gmax = jnp.max(gl, axis=1, keepdims=True)
    gidx = first_lane(gl == gmax)
    pg = 1.0 / jnp.sum(jnp.exp(gl - gmax), axis=1, keepdims=True)
    lo = ROUTE_OFF + gidx * EXPERTS_PER_GROUP
    emask = (lane >= lo) & (lane < lo + EXPERTS_PER_GROUP)
    el = jnp.where(emask, logits, NEG)
    v1 = jnp.max(el, axis=1, keepdims=True)
    i1 = first_lane(el == v1)
    el2 = jnp.where(emask & (lane != i1), logits, NEG)
    v2 = jnp.max(el2, axis=1, keepdims=True)
    i2 = first_lane(el2 == v2)
    e21 = jnp.exp(v2 - v1)
    w1 = pg / (1.0 + e21)
    w2 = pg * e21 / (1.0 + e21)

    sel1 = lane == i1
    sel2 = lane == i2
    onehot = jnp.where(sel1 | sel2, 1.0, 0.0)
    prefix = jnp.dot(tri_ref[...], onehot.astype(BF16), preferred_element_type=F32) + carry_ref[...]
    r1 = jnp.sum(jnp.where(sel1, prefix, 0.0), axis=1, keepdims=True)
    r2 = jnp.sum(jnp.where(sel2, prefix, 0.0), axis=1, keepdims=True)
    carry_ref[...] += jnp.sum(onehot, axis=0, keepdims=True)

    e1 = (i1 - ROUTE_OFF).astype(F32)
    e2 = (i2 - ROUTE_OFF).astype(F32)
    table = jnp.where(lane == 0, e1, jnp.where(lane == 1, e2, jnp.where(lane == 2, r1, jnp.where(lane == 3, r2, 0.0))))
    route_ref[...] = table.T[:SUBLANES].astype(jnp.int32)
    wts_ref[0] = jnp.where(lane == 0, w1, jnp.where(lane == 1, w2, 0.0))


def _merge(attn, hy, ga, gh, x, mod, w_attn_o, w_hy_o, w_out, ln1_g, ln1_b, rg_w, rg_b, re_w, re_b):
    B, S, D = x.shape
    tm = min(1024, S)
    spare = LANES - N_GROUPS - N_EXPERTS
    rw = jnp.concatenate([rg_w, re_w, jnp.zeros((D, spare), F32)], axis=1)
    rb = jnp.concatenate([rg_b, re_b, jnp.zeros((spare,), F32)]).reshape(1, LANES)
    rwh, rwl = _split(rw)
    tri = (jnp.arange(tm)[:, None] > jnp.arange(tm)[None, :]).astype(BF16)
    row = lambda b, i: (b, i, 0)
    full = lambda r, c: pl.BlockSpec((r, c), lambda b, i: (0, 0))
    per_b = S // tm
    outs = [jax.ShapeDtypeStruct((B, S, D), F32), jax.ShapeDtypeStruct((B, S, D // 2), F32),
            jax.ShapeDtypeStruct((SUBLANES, B * S), jnp.int32), jax.ShapeDtypeStruct((B, S, LANES), F32),
            jax.ShapeDtypeStruct((1, LANES), F32)]
    return pl.pallas_call(
        _merge_kernel,
        out_shape=outs,
        grid=(B, per_b),
        in_specs=[pl.BlockSpec((1, tm, ATTN_WIDTH), row), pl.BlockSpec((1, tm, HYENA_WIDTH), row),
                  pl.BlockSpec((1, tm, D), row), pl.BlockSpec((1, tm, D), row), pl.BlockSpec((1, tm, D), row),
                  pl.BlockSpec((1, 6, D), lambda b, i: (b, 0, 0)),
                  full(ATTN_WIDTH, D), full(HYENA_WIDTH, D), full(D, D),
                  full(1, D), full(1, D), full(D, LANES), full(D, LANES), full(1, LANES), full(tm, tm)],
        out_specs=[pl.BlockSpec((1, tm, D), row), pl.BlockSpec((1, tm, D // 2), row),
                   pl.BlockSpec((SUBLANES, tm), lambda b, i: (0, b * per_b + i)),
                   pl.BlockSpec((1, tm, LANES), row), full(1, LANES)],
        scratch_shapes=[pltpu.VMEM((1, LANES), F32)],
        compiler_params=_params(2),
        name="merge",
    )(attn, hy, ga, gh, x, mod, w_attn_o.astype(BF16), w_hy_o.astype(BF16), w_out.astype(BF16),
      ln1_g.reshape(1, D), ln1_b.reshape(1, D), rwh, rwl, rb, tri)


SC_ROWS = 64


def _sc_workers():
    info = plsc.get_sparse_core_info()
    return info.num_cores, info.num_cores * info.num_subcores


def _sc_split(n):
    _, workers = _sc_workers()
    per_worker = n // workers
    chunks = per_worker // SC_ROWS
    assert per_worker * workers == n and chunks * SC_ROWS == per_worker and chunks % 2 == 0
    return workers, per_worker, chunks


def _sc_scatter_rows(src, idx0, idx1, n_out):
    n, width = src.shape
    nc, _ = _sc_workers()
    workers, per_worker, chunks = _sc_split(n)
    mesh = plsc.VectorSubcoreMesh(core_axis_name="c", subcore_axis_name="s")

    def body(src_hbm, i0_hbm, i1_hbm, out_hbm, i0_v, i1_v, rows_v, sem, ssem):
        wid = lax.axis_index("s") * nc + lax.axis_index("c")
        base = wid * per_worker
        pltpu.sync_copy(i0_hbm.at[wid], i0_v)
        pltpu.sync_copy(i1_hbm.at[wid], i1_v)

        def load(chunk, buf):
            return pltpu.make_async_copy(src_hbm.at[pl.ds(base + chunk * SC_ROWS, SC_ROWS)], rows_v.at[buf], sem)

        load(0, 0).start()

        @pl.loop(0, chunks, step=2)
        def _(c):
            for b in range(2):
                chunk = c + b
                load(chunk, b).wait()

                @pl.when(chunk + 1 < chunks)
                def _():
                    load(chunk + 1, 1 - b).start()

                first = pltpu.make_async_copy(rows_v.at[b], out_hbm.at[i0_v.at[chunk]], ssem)
                second = pltpu.make_async_copy(rows_v.at[b], out_hbm.at[i1_v.at[chunk]], ssem)
                first.start()
                second.start()
                first.wait()
                second.wait()

    shaped = lambda i: i.reshape(workers, chunks, SC_ROWS)
    return pl.kernel(
        body,
        out_type=jax.ShapeDtypeStruct((n_out, width), src.dtype),
        mesh=mesh,
        scratch_types=[pltpu.VMEM((chunks, SC_ROWS), jnp.int32),
                       pltpu.VMEM((chunks, SC_ROWS), jnp.int32),
                       pltpu.VMEM((2, SC_ROWS, width), src.dtype),
                       pltpu.SemaphoreType.DMA, pltpu.SemaphoreType.DMA],
        name="sc_scatter",
    )(src, shaped(idx0), shaped(idx1))


def _expert_kernel(first_ref, nb_ref, sz_ref, tot_ref, w1_ref, w3_ref, w2_ref, xb_ref, yb_ref,
                   xbuf, ybuf, c1_ref, c3_ref, c2_ref, lsem, ssem):
    e = pl.program_id(0)
    nb = nb_ref[e]
    first = first_ref[e]
    total = tot_ref[0]
    rows = xbuf.shape[1]

    def load(g, slot):
        src = xb_ref.at[pl.ds(pl.multiple_of(g * rows, rows), rows)]
        return pltpu.make_async_copy(src, xbuf.at[slot], lsem.at[slot])

    def store(g, slot):
        dst = yb_ref.at[pl.ds(pl.multiple_of(g * rows, rows), rows)]
        return pltpu.make_async_copy(ybuf.at[slot], dst, ssem.at[slot])

    @pl.when((e == 0) & (total > 0))
    def _():
        load(0, 0).start()

    @pl.when(nb > 0)
    def _():
        c1_ref[...] = w1_ref[0].astype(BF16)
        c3_ref[...] = w3_ref[0].astype(BF16)
        c2_ref[...] = w2_ref[0].astype(BF16)

        def block(j, carry):
            g = first + j
            slot = lax.rem(g, 2)
            load(g, slot).wait()

            @pl.when(g + 1 < total)
            def _():
                load(g + 1, 1 - slot).start()

            @pl.when(g >= 2)
            def _():
                store(g - 2, slot).wait()

            n_valid = sz_ref[e] - j * rows
            pieces = jnp.minimum((n_valid + EXPERT_ROWS - 1) // EXPERT_ROWS, rows // EXPERT_ROWS)

            def swiglu(n):
                rid = lax.broadcasted_iota(jnp.int32, (n, 1), 0)
                xa, xb = _unpack2(jnp.where(rid < n_valid, xbuf[slot, :n, :], 0.0))
                x = jnp.concatenate([xa, xb], axis=1).astype(BF16)
                a = jnp.dot(x, c1_ref[...], preferred_element_type=F32)
                gate = jnp.dot(x, c3_ref[...], preferred_element_type=F32)
                hmid = (a * _sigmoid(a) * gate).astype(BF16)
                y = jnp.dot(hmid, c2_ref[...], preferred_element_type=F32)
                half = y.shape[1] // 2
                ybuf[slot, :n, :] = _pack2(y[:, :half], y[:, half:])

            for q in range(1, rows // EXPERT_ROWS + 1):
                pl.when(pieces == q)(functools.partial(swiglu, q * EXPERT_ROWS))
            store(g, slot).start()
            return carry

        lax.fori_loop(0, nb, block, 0)

    @pl.when(e == pl.num_programs(0) - 1)
    def _():
        for back in (2, 1):
            @pl.when(total >= back)
            def _():
                g = total - back
                store(g, lax.rem(g, 2)).wait()


def _experts(xb, first_blk, n_blk, sizes, w1, w3, w2):
    P, W = xb.shape
    E, D, DE = w1.shape
    total = jnp.sum(n_blk, keepdims=True)
    wspec = lambda r, c: pl.BlockSpec((1, r, c), lambda e, *_: (e, 0, 0))
    grid_spec = pltpu.PrefetchScalarGridSpec(
        num_scalar_prefetch=4,
        grid=(E,),
        in_specs=[wspec(D, DE), wspec(D, DE), wspec(DE, D), pl.BlockSpec(memory_space=pl.ANY)],
        out_specs=pl.BlockSpec(memory_space=pl.ANY),
        scratch_shapes=[pltpu.VMEM((2, MOE_BLOCK, W), F32), pltpu.VMEM((2, MOE_BLOCK, W), F32),
                        pltpu.VMEM((D, DE), BF16), pltpu.VMEM((D, DE), BF16), pltpu.VMEM((DE, D), BF16),
                        pltpu.SemaphoreType.DMA((2,)), pltpu.SemaphoreType.DMA((2,))],
    )
    return pl.pallas_call(
        _expert_kernel,
        out_shape=jax.ShapeDtypeStruct((P, W), F32),
        grid_spec=grid_spec,
        compiler_params=_params(1),
        name="experts",
    )(first_blk, n_blk, sizes, total, w1, w3, w2, xb)


def _sc_gather_rows(table, idx):
    n, width = idx.shape[0], table.shape[1]
    nc, _ = _sc_workers()
    workers, per_worker, chunks = _sc_split(n)
    mesh = plsc.VectorSubcoreMesh(core_axis_name="c", subcore_axis_name="s")

    def body(table_hbm, idx_hbm, out_hbm, idx_v, rows_v, sem):
        wid = lax.axis_index("s") * nc + lax.axis_index("c")
        base = wid * per_worker
        pltpu.sync_copy(idx_hbm.at[wid], idx_v)

        def gather(chunk, buf):
            return pltpu.make_async_copy(table_hbm.at[idx_v.at[chunk]], rows_v.at[buf], sem)

        gather(0, 0).start()

        @pl.loop(0, chunks, step=2)
        def _(c):
            for b in range(2):
                chunk = c + b
                gather(chunk, b).wait()

                @pl.when(chunk + 1 < chunks)
                def _():
                    gather(chunk + 1, 1 - b).start()

                pltpu.sync_copy(rows_v.at[b], out_hbm.at[pl.ds(base + chunk * SC_ROWS, SC_ROWS)])

    return pl.kernel(
        body,
        out_type=jax.ShapeDtypeStruct((n, width), table.dtype),
        mesh=mesh,
        scratch_types=[pltpu.VMEM((chunks, SC_ROWS), jnp.int32),
                       pltpu.VMEM((2, SC_ROWS, width), table.dtype),
                       pltpu.SemaphoreType.DMA],
        name="sc_gather",
    )(table, idx.reshape(workers, chunks, SC_ROWS))


def _combine_dense_kernel(r0_ref, r1_ref, wts_ref, x1_ref, mod_ref, g_ref, b_ref, *rest):
    o_ref = rest[-1]
    w = wts_ref[...]
    y0 = jnp.concatenate(_unpack2(r0_ref[0]), axis=1)
    y1 = jnp.concatenate(_unpack2(r1_ref[0]), axis=1)
    y = w[:, 0:1] * y0 + w[:, 1:2] * y1
    gate2 = mod_ref[0, 5:6, :]
    o_ref[...] = _layer_norm(DN_ALPHA * x1_ref[...] + gate2 * y, g_ref[...], b_ref[...])


def _combine_dense(rows, wts, x1, mod, ln2_g, ln2_b, S, b, out):
    T, D = x1.shape
    tm = min(1024, S)
    per_b = S // tm
    here = lambda i: (b * per_b + i, 0)
    in_specs = [pl.BlockSpec((1, tm, rows.shape[2]), lambda i: (0, i, 0)),
                pl.BlockSpec((1, tm, rows.shape[2]), lambda i: (1, i, 0)),
                pl.BlockSpec((tm, LANES), here),
                pl.BlockSpec((tm, D), here),
                pl.BlockSpec((1, 6, D), lambda i: (b, 0, 0)),
                pl.BlockSpec((1, D), lambda i: (0, 0)),
                pl.BlockSpec((1, D), lambda i: (0, 0))]
    args = [rows, rows, wts, x1, mod, ln2_g.reshape(1, D), ln2_b.reshape(1, D)]
    aliases = {}
    if out is not None:
        in_specs.append(pl.BlockSpec(memory_space=pl.ANY))
        aliases = {len(args): 0}
        args.append(out)
    return pl.pallas_call(
        _combine_dense_kernel,
        out_shape=jax.ShapeDtypeStruct((T, D), F32),
        grid=(per_b,),
        in_specs=in_specs,
        out_specs=pl.BlockSpec((tm, D), here),
        input_output_aliases=aliases,
        compiler_params=_params(1),
        name="combine",
    )(*args)


def _moe(h2, x1, route, wts, counts, mod, w1, w3, w2, ln2_g, ln2_b):
    B, S, D = x1.shape
    T = B * S
    P = 2 * T + N_EXPERTS * MOE_BLOCK
    sizes = counts[0, ROUTE_OFF:ROUTE_OFF + N_EXPERTS].astype(jnp.int32)
    n_blk = (sizes + MOE_BLOCK - 1) // MOE_BLOCK
    psizes = n_blk * MOE_BLOCK
    poffs = jnp.cumsum(psizes) - psizes
    sel = route[0:2, None, :] == jnp.arange(N_EXPERTS, dtype=jnp.int32)[None, :, None]
    dest = route[2:4] + jnp.sum(jnp.where(sel, poffs[None, :, None], 0), axis=1)
    xb = _sc_scatter_rows(h2.reshape(T, D // 2), dest[0], dest[1], P)
    yb = _experts(xb, poffs // MOE_BLOCK, n_blk, sizes, w1, w3, w2)
    out = None
    for b in range(B):
        slot_major = dest[:, b * S:(b + 1) * S].reshape(2 * S)
        rows = _sc_gather_rows(yb, slot_major).reshape(2, S, yb.shape[1])
        out = _combine_dense(rows, wts.reshape(T, LANES), x1.reshape(T, D), mod, ln2_g, ln2_b, S, b, out)
    return out.reshape(B, S, D)


def _layer(x, c, w_ada, b_ada, w_in, conv_w, conv_b, fw1, fb1, ff1, fw2, fb2, ff2, fw3, decay, skip,
           w_hy_o, w_attn_o, attn_sink, w_out, ln1_g, ln1_b, rg_w, rg_b, re_w, re_b, ew1, ew3, ew2,
           ln2_g, ln2_b):
    mod = _ada(c, w_ada, b_ada)
    q, kv, hv, hx1, hx2, ga, gh = _in_proj(x, mod, w_in, conv_w, conv_b)
    attn = _attention(q, kv, attn_sink)
    hy = _hyena(hv, hx1, hx2, fw1, fb1, ff1, fw2, fb2, ff2, fw3, decay, skip)
    x1, h2, route, wts, counts = _merge(attn, hy, ga, gh, x, mod, w_attn_o, w_hy_o, w_out,
                                        ln1_g, ln1_b, rg_w, rg_b, re_w, re_b)
    return _moe(h2, x1, route, wts, counts, mod, ew1, ew3, ew2, ln2_g, ln2_b)


def kernel(x, c, w_ada, b_ada, w_in, conv_w, conv_b, filt_w1, filt_b1, filt_freq1, filt_w2, filt_b2, filt_freq2, filt_w3, filt_decay, hy_skip, w_hy_o, w_attn_o, attn_sink, w_out, ln1_g, ln1_b, router_group_w, router_group_b, router_expert_w, router_expert_b, exp_w1, exp_w3, exp_w2, ln2_g, ln2_b):
    for l in range(w_ada.shape[0]):
        x = _layer(x, c, w_ada[l], b_ada[l], w_in[l], conv_w[l], conv_b[l], filt_w1[l], filt_b1[l],
                   filt_freq1[l], filt_w2[l], filt_b2[l], filt_freq2[l], filt_w3[l], filt_decay[l],
                   hy_skip[l], w_hy_o[l], w_attn_o[l], attn_sink[l], w_out[l], ln1_g[l], ln1_b[l],
                   router_group_w[l], router_group_b[l], router_expert_w[l], router_expert_b[l],
                   exp_w1[l], exp_w3[l], exp_w2[l], ln2_g[l], ln2_b[l])
    return x
```

```python
import functools
import math

import numpy as np
import jax
import jax.numpy as jnp
from jax import lax
from jax.experimental import pallas as pl
from jax.experimental.pallas import tpu as pltpu
from jax.experimental.pallas import tpu_sc as plsc

F32 = jnp.float32
BF16 = jnp.bfloat16

N_HEADS = 8
N_KV_HEADS = 2
HEAD_DIM = 64
ATTN_WIDTH = N_HEADS * HEAD_DIM
KV_WIDTH = N_KV_HEADS * HEAD_DIM
WINDOW = 128
HYENA_WIDTH = 512
FILTER_EMB = 33
FILTER_BANDS = (FILTER_EMB - 1) // 2
WINDOW_SHIFT = 0.05
N_GROUPS = 8
EXPERTS_PER_GROUP = 8
N_EXPERTS = N_GROUPS * EXPERTS_PER_GROUP
MOE_BLOCK = 1024
EXPERT_ROWS = 128
LN_EPS = 1e-5
DEPTH = 1
DN_ALPHA = (2.0 * DEPTH) ** 0.25
NEG = -1e30

LANES = 128
SUBLANES = 8
ROUTE_OFF = N_GROUPS
VMEM_LIMIT = 56 * 1024 * 1024


def _params(n_axes, vmem=VMEM_LIMIT):
    return pltpu.CompilerParams(dimension_semantics=("arbitrary",) * n_axes, vmem_limit_bytes=vmem)


def _split(a):
    hi = a.astype(BF16)
    lo = (a - hi.astype(F32)).astype(BF16)
    return hi, lo


def _dot3(a, b_hi, b_lo):
    a_hi, a_lo = _split(a)
    acc = jnp.dot(a_hi, b_hi, preferred_element_type=F32)
    acc = acc + jnp.dot(a_hi, b_lo, preferred_element_type=F32)
    acc = acc + jnp.dot(a_lo, b_hi, preferred_element_type=F32)
    return acc


def _pack2(a, b):
    ia = lax.bitcast_convert_type(a.astype(BF16).astype(F32), jnp.int32)
    ib = lax.bitcast_convert_type(b.astype(BF16).astype(F32), jnp.int32)
    return lax.bitcast_convert_type(ia | lax.shift_right_logical(ib, 16), F32)


def _unpack2(p):
    p = lax.bitcast_convert_type(p, jnp.int32)
    a = lax.bitcast_convert_type(p & jnp.int32(-65536), F32)
    b = lax.bitcast_convert_type(lax.shift_left(p, 16), F32)
    return a, b


def _sigmoid(x):
    return 0.5 * jnp.tanh(0.5 * x) + 0.5


def _layer_norm(r, g, b):
    mu = jnp.mean(r, axis=-1, keepdims=True)
    d = r - mu
    var = jnp.mean(d * d, axis=-1, keepdims=True)
    return d * lax.rsqrt(var + LN_EPS) * g + b


def _ada_kernel(c_ref, w_ref, b_ref, o_ref):
    c = c_ref[...]
    s = c * _sigmoid(c)
    wh, wl = _split(w_ref[...])
    o_ref[...] = _dot3(s, wh, wl) + b_ref[...]


def _ada(c, w_ada, b_ada):
    B, D = c.shape
    n_out = w_ada.shape[1]
    rows = SUBLANES
    cp = jnp.pad(c, ((0, rows - B), (0, 0)))
    tn = 1024
    out = pl.pallas_call(
        _ada_kernel,
        out_shape=jax.ShapeDtypeStruct((rows, n_out), F32),
        grid=(n_out // tn,),
        in_specs=[pl.BlockSpec((rows, D), lambda j: (0, 0)),
                  pl.BlockSpec((D, tn), lambda j: (0, j)),
                  pl.BlockSpec((1, tn), lambda j: (0, j))],
        out_specs=pl.BlockSpec((rows, tn), lambda j: (0, j)),
        compiler_params=_params(1),
        name="ada",
    )(cp, w_ada, b_ada.reshape(1, n_out))
    return out[:B].reshape(B, 6, D)


def _inproj_kernel(x_ref, xp_ref, xn_ref, mod_ref, w_ref, cw_ref, cb_ref,
                   q_ref, kv_ref, v_ref, x1_ref, x2_ref, ga_ref, gh_ref):
    i = pl.program_id(1)
    n = pl.num_programs(1)
    C = HYENA_WIDTH
    x = x_ref[0]
    tm, D = x.shape
    shift = mod_ref[0, 0:1, :]
    scale = mod_ref[0, 1:2, :]
    h = (x * (1.0 + scale) + shift).astype(BF16)

    def seg(lo, hi):
        return jnp.dot(h, w_ref[:, lo:hi], preferred_element_type=F32)

    o_q = 0
    o_kv = o_q + ATTN_WIDTH
    o_hy = o_kv + 2 * KV_WIDTH
    o_ga = o_hy + 3 * C
    o_gh = o_ga + D
    ga_ref[0] = _sigmoid(seg(o_ga, o_ga + D)).astype(BF16)
    gh_ref[0] = _sigmoid(seg(o_gh, o_gh + D)).astype(BF16)

    u = seg(o_hy, o_hy + 3 * C)
    xe = jnp.concatenate([xp_ref[0], xn_ref[0]], axis=0)
    he = (xe * (1.0 + scale) + shift).astype(BF16)
    ue = jnp.dot(he, w_ref[:, o_hy:o_hy + 3 * C], preferred_element_type=F32)
    prow = jnp.where(i > 0, ue[SUBLANES - 1:SUBLANES], 0.0)
    nrow = jnp.where(i < n - 1, ue[SUBLANES:SUBLANES + 1], 0.0)
    rid = lax.broadcasted_iota(jnp.int32, (tm, 1), 0)
    up = jnp.where(rid == 0, prow, pltpu.roll(u, 1, 0))
    dn = jnp.where(rid == tm - 1, nrow, pltpu.roll(u, tm - 1, 0))
    conv = cw_ref[0:1, :] * up + cw_ref[1:2, :] * u + cw_ref[2:3, :] * dn + cb_ref[...]
    v_ref[0] = conv[:, :C]
    x1_ref[0] = conv[:, C:2 * C]
    x2_ref[0] = conv[:, 2 * C:]

    q_ref[0] = (seg(o_q, o_q + ATTN_WIDTH) * (HEAD_DIM ** -0.5)).astype(BF16)
    kv_ref[0] = seg(o_kv, o_kv + 2 * KV_WIDTH).astype(BF16)


def _in_proj(x, mod, w_in, conv_w, conv_b):
    B, S, D = x.shape
    C = HYENA_WIDTH
    tm = min(1024, S)
    r8 = tm // SUBLANES
    nb8 = S // SUBLANES
    wb = w_in.astype(BF16)
    nw = wb.shape[1]
    row = lambda b, i: (b, i, 0)
    shapes = [(ATTN_WIDTH, BF16), (2 * KV_WIDTH, BF16), (C, F32), (C, F32), (C, F32), (D, BF16), (D, BF16)]
    return pl.pallas_call(
        _inproj_kernel,
        out_shape=[jax.ShapeDtypeStruct((B, S, w), dt) for w, dt in shapes],
        grid=(B, S // tm),
        in_specs=[pl.BlockSpec((1, tm, D), row),
                  pl.BlockSpec((1, SUBLANES, D), lambda b, i: (b, jnp.maximum(i * r8 - 1, 0), 0)),
                  pl.BlockSpec((1, SUBLANES, D), lambda b, i: (b, jnp.minimum((i + 1) * r8, nb8 - 1), 0)),
                  pl.BlockSpec((1, 6, D), lambda b, i: (b, 0, 0)),
                  pl.BlockSpec((D, nw), lambda b, i: (0, 0)),
                  pl.BlockSpec((3, 3 * C), lambda b, i: (0, 0)),
                  pl.BlockSpec((1, 3 * C), lambda b, i: (0, 0))],
        out_specs=[pl.BlockSpec((1, tm, w), row) for w, _ in shapes],
        compiler_params=_params(2),
        name="in_proj",
    )(x, x, x, mod, wb, conv_w.astype(F32), conv_b.reshape(1, 3 * C).astype(F32))


ATT_TQ = 512
ATT_QB = 128
ATT_STACK = 4


def _attn_kernel(sink_ref, q_ref, kvp_ref, kvc_ref, kvn_ref, bias_ref, o_ref, kv_scr, vx_scr, *, seq_len):
    i = pl.program_id(1)
    H = WINDOW
    TQ = q_ref.shape[1]
    Q = min(ATT_QB, TQ)
    band = Q + 2 * H
    G = N_HEADS // N_KV_HEADS
    kv_scr[0:H] = kvp_ref[0]
    kv_scr[H:H + TQ] = kvc_ref[0]
    kv_scr[H + TQ:] = kvn_ref[0]
    for kv in range(N_KV_HEADS):
        vx_scr[:, kv * LANES:kv * LANES + HEAD_DIM] = kv_scr[:, KV_WIDTH + kv * HEAD_DIM:KV_WIDTH + (kv + 1) * HEAD_DIM]
        vx_scr[:, kv * LANES + HEAD_DIM:(kv + 1) * LANES] = jnp.ones((TQ + 2 * H, LANES - HEAD_DIM), BF16)
    col = lax.broadcasted_iota(jnp.int32, (1, band), 1)
    rhead = lax.broadcasted_iota(jnp.int32, (ATT_STACK * Q, 1), 0) // Q
    for j in range(TQ // Q):
        kpos = i * TQ + j * Q - H + col
        colbias = jnp.where((kpos >= 0) & (kpos < seq_len), 0.0, NEG)
        for kv in range(N_KV_HEADS):
            kk = kv_scr[j * Q:j * Q + band, kv * HEAD_DIM:(kv + 1) * HEAD_DIM]
            vx = vx_scr[j * Q:j * Q + band, kv * LANES:(kv + 1) * LANES]
            for sub in range(G // ATT_STACK):
                first = sub * ATT_STACK
                heads = [kv * G + first + g for g in range(ATT_STACK)]
                qg = jnp.concatenate([q_ref[0, j * Q:(j + 1) * Q, h * HEAD_DIM:(h + 1) * HEAD_DIM] for h in heads],
                                     axis=0)
                s = lax.dot_general(qg, kk, (((1,), (1,)), ((), ())), preferred_element_type=F32)
                s = s + bias_ref[kv, first * Q:(first + ATT_STACK) * Q, :] + colbias
                snk = sink_ref[heads[-1]]
                for g in range(ATT_STACK - 2, -1, -1):
                    snk = jnp.where(rhead == g, sink_ref[heads[g]], snk)
                m = jnp.maximum(jnp.max(s, axis=1, keepdims=True), snk)
                p = jnp.exp(s - m).astype(BF16)
                ox = jnp.dot(p, vx, preferred_element_type=F32)
                den = ox[:, HEAD_DIM:HEAD_DIM + 1] + jnp.exp(snk - m)
                o = ox[:, :HEAD_DIM] / den
                for g, h in enumerate(heads):
                    o_ref[0, j * Q:(j + 1) * Q, h * HEAD_DIM:(h + 1) * HEAD_DIM] = o[g * Q:(g + 1) * Q].astype(BF16)


def _attention(q, kv, sink):
    B, S, _ = q.shape
    H = WINDOW
    TQ = min(ATT_TQ, S)
    Q = min(ATT_QB, TQ)
    r = TQ // H
    nq = S // H
    G = N_HEADS // N_KV_HEADS
    assert G % ATT_STACK == 0
    a = jnp.arange(Q)[:, None]
    j = jnp.arange(Q + 2 * H)[None, :]
    rel = jnp.abs(j - H - a).astype(F32)
    slopes = 2.0 ** (-8.0 * jnp.arange(1, N_HEADS + 1, dtype=F32) / N_HEADS)
    bias = jnp.where(rel[None] <= WINDOW, -slopes[:, None, None] * rel[None], NEG).astype(F32)
    bias = bias.reshape(N_KV_HEADS, G * Q, Q + 2 * H)
    cur = lambda b, i: (b, i, 0)
    return pl.pallas_call(
        functools.partial(_attn_kernel, seq_len=S),
        out_shape=jax.ShapeDtypeStruct((B, S, ATTN_WIDTH), BF16),
        grid=(B, S // TQ),
        in_specs=[pl.BlockSpec(memory_space=pltpu.SMEM),
                  pl.BlockSpec((1, TQ, ATTN_WIDTH), cur),
                  pl.BlockSpec((1, H, 2 * KV_WIDTH), lambda b, i: (b, jnp.maximum(i * r - 1, 0), 0)),
                  pl.BlockSpec((1, TQ, 2 * KV_WIDTH), cur),
                  pl.BlockSpec((1, H, 2 * KV_WIDTH), lambda b, i: (b, jnp.minimum((i + 1) * r, nq - 1), 0)),
                  pl.BlockSpec((N_KV_HEADS, G * Q, Q + 2 * H), lambda b, i: (0, 0, 0))],
        out_specs=pl.BlockSpec((1, TQ, ATTN_WIDTH), cur),
        scratch_shapes=[pltpu.VMEM((TQ + 2 * H, 2 * KV_WIDTH), BF16),
                        pltpu.VMEM((TQ + 2 * H, N_KV_HEADS * LANES), BF16)],
        compiler_params=_params(2),
        name="attn",
    )(sink.astype(F32), q, kv, kv, kv, bias)


def _filter_kernel(z_ref, w1h, w1l, b1_ref, f1_ref, w2h, w2l, b2_ref, f2_ref, w3h, w3l, dec_ref,
                   k_ref, s_ref):
    i = pl.program_id(0)
    z = z_ref[...]
    h1 = jnp.sin(f1_ref[...] * (_dot3(z, w1h[...], w1l[...]) + b1_ref[...]))
    h2 = jnp.sin(f2_ref[...] * (_dot3(h1, w2h[...], w2l[...]) + b2_ref[...]))
    k = _dot3(h2, w3h[...], w3l[...])
    t = z[:, 0:1]
    k = k * (jnp.exp(-t * jnp.abs(dec_ref[...])) + WINDOW_SHIFT)
    k_ref[...] = k

    @pl.when(i == 0)
    def _():
        s_ref[...] = jnp.zeros_like(s_ref)

    s_ref[...] += jnp.sum(jnp.abs(k), axis=0, keepdims=True)


def _filter_embedding(L):
    t = np.linspace(0.0, 1.0, L, dtype=np.float32).astype(np.float64)[:, None]
    w = (2.0 * math.pi * np.arange(L, dtype=np.float32) / np.float32(L)).astype(np.float64)[:, None]
    bands = np.linspace(1e-4, FILTER_BANDS - 1, FILTER_BANDS, dtype=np.float32).astype(np.float64)[None, :]
    bw = (bands.astype(np.float32) * w.astype(np.float32)).astype(np.float64)
    z = np.concatenate([t, np.cos(bw), -np.sin(bw)], axis=-1)
    zp = np.zeros((L, LANES), np.float32)
    zp[:, :FILTER_EMB] = z.astype(np.float32)
    return jnp.asarray(zp)


def _pad2(a, r, c):
    return jnp.zeros((r, c), F32).at[:a.shape[0], :a.shape[1]].set(a.astype(F32))


def _filters(L, fw1, fb1, ff1, fw2, fb2, ff2, fw3, decay):
    H = LANES
    nf = fw3.shape[1]
    z = _filter_embedding(L)
    w1h, w1l = _split(_pad2(fw1, H, H))
    w2h, w2l = _split(_pad2(fw2, H, H))
    w3h, w3l = _split(_pad2(fw3, H, nf))
    b1 = _pad2(fb1[None], 1, H)
    f1 = _pad2(ff1[None], 1, H)
    b2 = _pad2(fb2[None], 1, H)
    f2 = _pad2(ff2[None], 1, H)
    tr = min(512, L)
    full = lambda r, c: pl.BlockSpec((r, c), lambda i: (0, 0))
    return pl.pallas_call(
        _filter_kernel,
        out_shape=[jax.ShapeDtypeStruct((L, nf), F32), jax.ShapeDtypeStruct((1, nf), F32)],
        grid=(L // tr,),
        in_specs=[pl.BlockSpec((tr, H), lambda i: (i, 0)),
                  full(H, H), full(H, H), full(1, H), full(1, H),
                  full(H, H), full(H, H), full(1, H), full(1, H),
                  full(H, nf), full(H, nf), full(1, nf)],
        out_specs=[pl.BlockSpec((tr, nf), lambda i: (i, 0)), full(1, nf)],
        compiler_params=_params(1),
        name="filter",
    )(z, w1h, w1l, b1, f1, w2h, w2l, b2, f2, w3h, w3l, decay.reshape(1, nf).astype(F32))


def _np_bf16(m64):
    return jnp.asarray(m64.astype(np.float32).astype(BF16))


def _dft_constants(L):
    N = 2 * L
    n2 = LANES
    n1 = N // n2
    h1 = n1 // 2
    k1 = np.arange(n1)[:, None]
    s1 = np.arange(h1)[None, :]
    ang = -2.0 * np.pi * ((k1 * s1) % n1) / n1
    wr, wi = np.cos(ang), np.sin(ang)
    w1_filt = np.block([[wr, wr], [wi, wi], [wr, -wr], [wi, -wi]])
    w1_cplx = np.block([[wr, -wi], [wi, wr]])
    vr, vi = wr.T / N, -wi.T / N
    w3 = np.block([[vr, -vi], [vi, vr]])
    k2 = np.arange(n2)[:, None]
    s2 = np.arange(n2)[None, :]
    a2 = -2.0 * np.pi * ((k2 * s2) % n2) / n2
    w2r, w2i = jnp.asarray(np.cos(a2), F32), jnp.asarray(np.sin(a2), F32)
    at = -2.0 * np.pi * ((np.arange(n1)[:, None] * s2) % N) / N
    twr, twi = jnp.asarray(np.cos(at), F32), jnp.asarray(np.sin(at), F32)
    mr = w2r[None] * twr[:, None, :] - w2i[None] * twi[:, None, :]
    mi = w2r[None] * twi[:, None, :] + w2i[None] * twr[:, None, :]
    fwd = jnp.concatenate([jnp.concatenate([mr, -mi], axis=2),
                           jnp.concatenate([mi, mr], axis=2)], axis=1)
    fwd = fwd.astype(BF16)
    return dict(n1=n1, w1_filt=_np_bf16(w1_filt), w1_cplx=_np_bf16(w1_cplx), w3=_np_bf16(w3),
                fwd=fwd)


SCH = 16


def _dft1_kernel(x_ref, w_ref, a_ref, *, n1):
    w = w_ref[...]
    for j in range(SCH):
        rhs = jnp.concatenate([x_ref[0, 0, :, j, :], x_ref[0, 1, :, j, :]], axis=0)
        res = jnp.dot(w, rhs.astype(BF16), preferred_element_type=F32)
        a_ref[0, :, j, :] = _pack2(res[:n1], res[n1:])


def _dft1_data(x, consts):
    B, L, C = x.shape
    n1 = consts["n1"]
    h1 = n1 // 2
    xv = x.reshape(B // 2, 2, h1, LANES, C)
    return pl.pallas_call(
        functools.partial(_dft1_kernel, n1=n1),
        out_shape=jax.ShapeDtypeStruct((B // 2, n1, LANES, C), F32),
        grid=(B // 2, LANES // SCH),
        in_specs=[pl.BlockSpec((1, 2, h1, SCH, C), lambda p, j: (p, 0, 0, j, 0)),
                  pl.BlockSpec((2 * n1, n1), lambda p, j: (0, 0))],
        out_specs=pl.BlockSpec((1, n1, SCH, C), lambda p, j: (p, 0, j, 0)),
        compiler_params=_params(2),
        name="dft1",
    )(xv, consts["w1_cplx"])


def _dft1f_kernel(x_ref, w_ref, a_ref, *, n1):
    C = HYENA_WIDTH
    w = w_ref[...]
    for j in range(SCH):
        rhs = jnp.concatenate([x_ref[:, j, :C], x_ref[:, j, C:]], axis=0)
        res = jnp.dot(w, rhs.astype(BF16), preferred_element_type=F32)
        a_ref[0, :, 0, j, :] = _pack2(res[:n1], res[n1:2 * n1])
        a_ref[0, :, 1, j, :] = _pack2(res[2 * n1:3 * n1], res[3 * n1:])


def _dft1_filter(kraw, consts):
    L, nf = kraw.shape
    C = HYENA_WIDTH
    n_ord = nf // (2 * C)
    n1 = consts["n1"]
    h1 = n1 // 2
    kv = kraw.reshape(h1, LANES, nf)
    return pl.pallas_call(
        functools.partial(_dft1f_kernel, n1=n1),
        out_shape=jax.ShapeDtypeStruct((n_ord, n1, 2, LANES, C), F32),
        grid=(n_ord, LANES // SCH),
        in_specs=[pl.BlockSpec((h1, SCH, 2 * C), lambda o, j: (0, j, o)),
                  pl.BlockSpec((4 * n1, n1), lambda o, j: (0, 0))],
        out_specs=pl.BlockSpec((1, n1, 2, SCH, C), lambda o, j: (o, 0, 0, j, 0)),
        compiler_params=_params(2),
        name="dft1f",
    )(kv, consts["w1_filt"])


KCH = 16


def _midf_kernel(a_ref, f_ref, inv_ref, b0_ref, h_ref):
    n2 = LANES
    sc = inv_ref[0]
    for k in range(KCH):
        p = jnp.concatenate(_unpack2(a_ref[0, k, :n2, :]), axis=0).astype(BF16)
        q = jnp.concatenate(_unpack2(a_ref[0, k, n2:, :]), axis=0).astype(BF16)
        h_re = jnp.dot(f_ref[k, :n2, :], p, preferred_element_type=F32)
        h_im = jnp.dot(f_ref[k, n2:, :], q, preferred_element_type=F32)
        h_ref[0, k] = _pack2((h_re - b0_ref[0]) * sc, h_im * sc)


def _filter_spectrum(af, inv_den, bwd0, consts):
    n_ord, n1, _, n2, C = af.shape
    a = af.reshape(n_ord, n1, 2 * n2, C)
    tab = pl.BlockSpec((KCH, 2 * n2, 2 * n2), lambda k, o: (k, 0, 0))
    vec = pl.BlockSpec((1, 1, C), lambda k, o: (o, 0, 0))
    return pl.pallas_call(
        _midf_kernel,
        out_shape=jax.ShapeDtypeStruct((n_ord, n1, n2, C), F32),
        grid=(n1 // KCH, n_ord),
        in_specs=[pl.BlockSpec((1, KCH, 2 * n2, C), lambda k, o: (o, k, 0, 0)), tab, vec, vec],
        out_specs=pl.BlockSpec((1, KCH, n2, C), lambda k, o: (o, k, 0, 0)),
        compiler_params=_params(2),
        name="midf",
    )(a, consts["fwd"], inv_den, bwd0)


def _mid_kernel(a_ref, f_ref, h_ref, b_ref):
    n2 = LANES
    for k in range(KCH):
        a = jnp.concatenate(_unpack2(a_ref[0, k]), axis=0).astype(BF16)
        x = jnp.dot(f_ref[k], a, preferred_element_type=F32)
        xr, xi = x[:n2], x[n2:]
        hr, hi = _unpack2(h_ref[0, k])
        y = jnp.concatenate([xr * hr - xi * hi, xr * hi + xi * hr], axis=0)
        b = lax.dot_general(f_ref[k], y.astype(BF16), (((0,), (0,)), ((), ())), preferred_element_type=F32)
        b_ref[0, k] = _pack2(b[:n2], b[n2:])


def _mid(a, hspec, order, consts):
    P, n1, n2, C = a.shape
    tab = pl.BlockSpec((KCH, 2 * n2, 2 * n2), lambda k, p: (k, 0, 0))
    return pl.pallas_call(
        _mid_kernel,
        out_shape=jax.ShapeDtypeStruct((P, n1, n2, C), F32),
        grid=(n1 // KCH, P),
        in_specs=[pl.BlockSpec((1, KCH, n2, C), lambda k, p: (p, k, 0, 0)),
                  tab,
                  pl.BlockSpec((1, KCH, n2, C), lambda k, p: (order, k, 0, 0))],
        out_specs=pl.BlockSpec((1, KCH, n2, C), lambda k, p: (p, k, 0, 0)),
        compiler_params=_params(2),
        name="mid",
    )(a, consts["fwd"], hspec)


def _dft3_kernel(b_ref, w_ref, v_ref, g_ref, skip_ref, *rest, h1, chain):
    if chain:
        w1_ref, z_ref, a_ref, slab_ref = rest
        w1 = w1_ref[...]
    else:
        z_ref, slab_ref = rest
    w = w_ref[...]
    skip = skip_ref[0]
    n1 = 2 * h1
    for j in range(SCH):
        slab_ref[...] = b_ref[0, :, j, :]
        rhs = jnp.concatenate(_unpack2(slab_ref[...]), axis=0)
        y = jnp.dot(w, rhs.astype(BF16), preferred_element_type=F32)
        z = [g_ref[0, r, :, j, :] * (y[r * h1:(r + 1) * h1] + v_ref[0, r, :, j, :] * skip) for r in range(2)]
        for r in range(2):
            z_ref[0, r, :, j, :] = z[r]
        if chain:
            res = jnp.dot(w1, jnp.concatenate(z, axis=0).astype(BF16), preferred_element_type=F32)
            a_ref[0, :, j, :] = _pack2(res[:n1], res[n1:])


def _dft3_gate(b5, v, gate, skip, consts, chain):
    P, n1, n2, C = b5.shape
    h1 = n1 // 2
    B, L, _ = v.shape
    five = lambda t: t.reshape(P, 2, h1, n2, C)
    dat = pl.BlockSpec((1, 2, h1, SCH, C), lambda p, j: (p, 0, 0, j, 0))
    packed = pl.BlockSpec((1, n1, SCH, C), lambda p, j: (p, 0, j, 0))
    in_specs = [packed, pl.BlockSpec((n1, 2 * n1), lambda p, j: (0, 0)), dat, dat,
                pl.BlockSpec((1, C), lambda p, j: (0, 0))]
    args = [b5, consts["w3"], five(v), five(gate), skip.reshape(1, C).astype(F32)]
    out_shape = [jax.ShapeDtypeStruct((P, 2, h1, n2, C), F32)]
    out_specs = [dat]
    if chain:
        in_specs.append(pl.BlockSpec((2 * n1, n1), lambda p, j: (0, 0)))
        args.append(consts["w1_cplx"])
        out_shape.append(jax.ShapeDtypeStruct((P, n1, n2, C), F32))
        out_specs.append(packed)
    outs = pl.pallas_call(
        functools.partial(_dft3_kernel, h1=h1, chain=chain),
        out_shape=out_shape,
        grid=(P, n2 // SCH),
        in_specs=in_specs,
        out_specs=out_specs,
        scratch_shapes=[pltpu.VMEM((n1, C), F32)],
        compiler_params=_params(2),
        name="dft3",
    )(*args)
    z = outs[0].reshape(B, L, C)
    return (z, outs[1]) if chain else (z, None)


def _hyena(v, x1, x2, fw1, fb1, ff1, fw2, fb2, ff2, fw3, decay, skip):
    B, L, C = v.shape
    consts = _dft_constants(L)
    kraw, ksum = _filters(L, fw1, fb1, ff1, fw2, fb2, ff2, fw3, decay)
    ks = ksum.reshape(2, 2, C)
    inv_den = (1.0 / (ks[:, 0] + ks[:, 1])).reshape(2, 1, C)
    bwd0 = kraw[0].reshape(2, 2, C)[:, 1].reshape(2, 1, C)
    hspec = _filter_spectrum(_dft1_filter(kraw, consts), inv_den, bwd0, consts)
    gates = (x1, x2)
    z, a5 = v, _dft1_data(v, consts)
    for o, gate in enumerate(gates):
        b5 = _mid(a5, hspec, o, consts)
        z, a5 = _dft3_gate(b5, z, gate, skip[o], consts, chain=o + 1 < len(gates))
    return z


def _merge_kernel(attn_ref, hy_ref, ga_ref, gh_ref, x_ref, mod_ref, wa_ref, wh_ref, wo_ref,
                  g1_ref, b1_ref, rwh_ref, rwl_ref, rb_ref, tri_ref,
                  x1_ref, h2_ref, route_ref, wts_ref, cnt_ref, carry_ref):
    @pl.when((pl.program_id(0) == 0) & (pl.program_id(1) == 0))
    def _():
        carry_ref[...] = jnp.zeros_like(carry_ref)

    logits = _merge_dense(attn_ref, hy_ref, ga_ref, gh_ref, x_ref, mod_ref, wa_ref, wh_ref, wo_ref,
                          g1_ref, b1_ref, rwh_ref, rwl_ref, rb_ref, x1_ref, h2_ref)
    _route_rows(logits, tri_ref, route_ref, wts_ref, carry_ref)
    cnt_ref[...] = carry_ref[...]


def _merge_dense(attn_ref, hy_ref, ga_ref, gh_ref, x_ref, mod_ref, wa_ref, wh_ref, wo_ref,
                 g1_ref, b1_ref, rwh_ref, rwl_ref, rb_ref, x1_ref, h2_ref):
    a = jnp.dot(attn_ref[0], wa_ref[...], preferred_element_type=F32)
    hy = jnp.dot(hy_ref[0].astype(BF16), wh_ref[...], preferred_element_type=F32)
    merged = ga_ref[0].astype(F32) * a + gh_ref[0].astype(F32) * hy
    y = jnp.dot(merged.astype(BF16), wo_ref[...], preferred_element_type=F32)
    gate1 = mod_ref[0, 2:3, :]
    shift2 = mod_ref[0, 3:4, :]
    scale2 = mod_ref[0, 4:5, :]
    x1 = _layer_norm(DN_ALPHA * x_ref[0] + gate1 * y, g1_ref[...], b1_ref[...])
    x1_ref[0] = x1
    h2 = x1 * (1.0 + scale2) + shift2
    half = h2.shape[1] // 2
    h2_ref[0] = _pack2(h2[:, :half], h2[:, half:])
    return _dot3(h2, rwh_ref[...], rwl_ref[...]) + rb_ref[...]


def _route_rows(logits, tri_ref, route_ref, wts_ref, carry_ref):
    tm = logits.shape[0]
    lane = lax.broadcasted_iota(jnp.int32, (tm, LANES), 1)
    lanef = lane.astype(F32)
    big = float(LANES)

    def first_lane(mask):
        return jnp.min(jnp.where(mask, lanef, big), axis=1, keepdims=True).astype(jnp.int32)

    gmask = lane < N_GROUPS
    gl = jnp.where(gmask, logits, NEG)
    gmax = jnp.max(gl, axis=1, keepdims=True)
    gidx = first_lane(gl == gmax)
    pg = 1.0 / jnp.sum(jnp.exp(gl - gmax), axis=1, keepdims=True)
    lo = ROUTE_OFF + gidx * EXPERTS_PER_GROUP
    emask = (lane >= lo) & (lane < lo + EXPERTS_PER_GROUP)
    el = jnp.where(emask, logits, NEG)
    v1 = jnp.max(el, axis=1, keepdims=True)
    i1 = first_lane(el == v1)
    el2 = jnp.where(emask & (lane != i1), logits, NEG)
    v2 = jnp.max(el2, axis=1, keepdims=True)
    i2 = first_lane(el2 == v2)
    e21 = jnp.exp(v2 - v1)
    w1 = pg / (1.0 + e21)
    w2 = pg * e21 / (1.0 + e21)

    sel1 = lane == i1
    sel2 = lane == i2
    onehot = jnp.where(sel1 | sel2, 1.0, 0.0)
    prefix = jnp.dot(tri_ref[...], onehot.astype(BF16), preferred_element_type=F32) + carry_ref[...]
    r1 = jnp.sum(jnp.where(sel1, prefix, 0.0), axis=1, keepdims=True)
    r2 = jnp.sum(jnp.where(sel2, prefix, 0.0), axis=1, keepdims=True)
    carry_ref[...] += jnp.sum(onehot, axis=0, keepdims=True)

    e1 = (i1 - ROUTE_OFF).astype(F32)
    e2 = (i2 - ROUTE_OFF).astype(F32)
    table = jnp.where(lane == 0, e1, jnp.where(lane == 1, e2, jnp.where(lane == 2, r1, jnp.where(lane == 3, r2, 0.0))))
    route_ref[...] = table.T[:SUBLANES].astype(jnp.int32)
    wts_ref[0] = jnp.where(lane == 0, w1, jnp.where(lane == 1, w2, 0.0))


def _merge(attn, hy, ga, gh, x, mod, w_attn_o, w_hy_o, w_out, ln1_g, ln1_b, rg_w, rg_b, re_w, re_b):
    B, S, D = x.shape
    tm = min(512, S)
    spare = LANES - N_GROUPS - N_EXPERTS
    rw = jnp.concatenate([rg_w, re_w, jnp.zeros((D, spare), F32)], axis=1)
    rb = jnp.concatenate([rg_b, re_b, jnp.zeros((spare,), F32)]).reshape(1, LANES)
    rwh, rwl = _split(rw)
    tri = (jnp.arange(tm)[:, None] > jnp.arange(tm)[None, :]).astype(BF16)
    row = lambda b, i: (b, i, 0)
    full = lambda r, c: pl.BlockSpec((r, c), lambda b, i: (0, 0))
    per_b = S // tm
    outs = [jax.ShapeDtypeStruct((B, S, D), F32), jax.ShapeDtypeStruct((B, S, D // 2), F32),
            jax.ShapeDtypeStruct((SUBLANES, B * S), jnp.int32), jax.ShapeDtypeStruct((B, S, LANES), F32),
            jax.ShapeDtypeStruct((1, LANES), F32)]
    return pl.pallas_call(
        _merge_kernel,
        out_shape=outs,
        grid=(B, per_b),
        in_specs=[pl.BlockSpec((1, tm, ATTN_WIDTH), row), pl.BlockSpec((1, tm, HYENA_WIDTH), row),
                  pl.BlockSpec((1, tm, D), row), pl.BlockSpec((1, tm, D), row), pl.BlockSpec((1, tm, D), row),
                  pl.BlockSpec((1, 6, D), lambda b, i: (b, 0, 0)),
                  full(ATTN_WIDTH, D), full(HYENA_WIDTH, D), full(D, D),
                  full(1, D), full(1, D), full(D, LANES), full(D, LANES), full(1, LANES), full(tm, tm)],
        out_specs=[pl.BlockSpec((1, tm, D), row), pl.BlockSpec((1, tm, D // 2), row),
                   pl.BlockSpec((SUBLANES, tm), lambda b, i: (0, b * per_b + i)),
                   pl.BlockSpec((1, tm, LANES), row), full(1, LANES)],
        scratch_shapes=[pltpu.VMEM((1, LANES), F32)],
        compiler_params=_params(2),
        name="merge",
    )(attn, hy, ga, gh, x, mod, w_attn_o.astype(BF16), w_hy_o.astype(BF16), w_out.astype(BF16),
      ln1_g.reshape(1, D), ln1_b.reshape(1, D), rwh, rwl, rb, tri)


SC_ROWS = 64


def _sc_workers():
    info = plsc.get_sparse_core_info()
    return info.num_cores, info.num_cores * info.num_subcores


def _sc_split(n):
    _, workers = _sc_workers()
    per_worker = n // workers
    chunks = per_worker // SC_ROWS
    assert per_worker * workers == n and chunks * SC_ROWS == per_worker and chunks % 2 == 0
    return workers, per_worker, chunks


def _sc_scatter_rows(src, idx0, idx1, n_out):
    n, width = src.shape
    nc, _ = _sc_workers()
    workers, per_worker, chunks = _sc_split(n)
    mesh = plsc.VectorSubcoreMesh(core_axis_name="c", subcore_axis_name="s")

    def body(src_hbm, i0_hbm, i1_hbm, out_hbm, i0_v, i1_v, rows_v, sem, ssem):
        wid = lax.axis_index("s") * nc + lax.axis_index("c")
        base = wid * per_worker
        pltpu.sync_copy(i0_hbm.at[wid], i0_v)
        pltpu.sync_copy(i1_hbm.at[wid], i1_v)

        def load(chunk, buf):
            return pltpu.make_async_copy(src_hbm.at[pl.ds(base + chunk * SC_ROWS, SC_ROWS)], rows_v.at[buf], sem)

        load(0, 0).start()

        @pl.loop(0, chunks, step=2)
        def _(c):
            for b in range(2):
                chunk = c + b
                load(chunk, b).wait()

                @pl.when(chunk + 1 < chunks)
                def _():
                    load(chunk + 1, 1 - b).start()

                first = pltpu.make_async_copy(rows_v.at[b], out_hbm.at[i0_v.at[chunk]], ssem)
                second = pltpu.make_async_copy(rows_v.at[b], out_hbm.at[i1_v.at[chunk]], ssem)
                first.start()
                second.start()
                first.wait()
                second.wait()

    shaped = lambda i: i.reshape(workers, chunks, SC_ROWS)
    return pl.kernel(
        body,
        out_type=jax.ShapeDtypeStruct((n_out, width), src.dtype),
        mesh=mesh,
        scratch_types=[pltpu.VMEM((chunks, SC_ROWS), jnp.int32),
                       pltpu.VMEM((chunks, SC_ROWS), jnp.int32),
                       pltpu.VMEM((2, SC_ROWS, width), src.dtype),
                       pltpu.SemaphoreType.DMA, pltpu.SemaphoreType.DMA],
        name="sc_scatter",
    )(src, shaped(idx0), shaped(idx1))


def _expert_kernel(first_ref, nb_ref, sz_ref, tot_ref, w1_ref, w3_ref, w2_ref, xb_ref, yb_ref,
                   xbuf, ybuf, c1_ref, c3_ref, c2_ref, lsem, ssem):
    e = pl.program_id(0)
    nb = nb_ref[e]
    first = first_ref[e]
    total = tot_ref[0]
    rows = xbuf.shape[1]

    def load(g, slot):
        src = xb_ref.at[pl.ds(pl.multiple_of(g * rows, rows), rows)]
        return pltpu.make_async_copy(src, xbuf.at[slot], lsem.at[slot])

    def store(g, slot):
        dst = yb_ref.at[pl.ds(pl.multiple_of(g * rows, rows), rows)]
        return pltpu.make_async_copy(ybuf.at[slot], dst, ssem.at[slot])

    @pl.when((e == 0) & (total > 0))
    def _():
        load(0, 0).start()

    @pl.when(nb > 0)
    def _():
        c1_ref[...] = w1_ref[0].astype(BF16)
        c3_ref[...] = w3_ref[0].astype(BF16)
        c2_ref[...] = w2_ref[0].astype(BF16)

        def block(j, carry):
            g = first + j
            slot = lax.rem(g, 2)
            load(g, slot).wait()

            @pl.when(g + 1 < total)
            def _():
                load(g + 1, 1 - slot).start()

            @pl.when(g >= 2)
            def _():
                store(g - 2, slot).wait()

            n_valid = sz_ref[e] - j * rows
            pieces = jnp.minimum((n_valid + EXPERT_ROWS - 1) // EXPERT_ROWS, rows // EXPERT_ROWS)

            def swiglu(n):
                rid = lax.broadcasted_iota(jnp.int32, (n, 1), 0)
                xa, xb = _unpack2(jnp.where(rid < n_valid, xbuf[slot, :n, :], 0.0))
                x = jnp.concatenate([xa, xb], axis=1).astype(BF16)
                a = jnp.dot(x, c1_ref[...], preferred_element_type=F32)
                gate = jnp.dot(x, c3_ref[...], preferred_element_type=F32)
                hmid = (a * _sigmoid(a) * gate).astype(BF16)
                y = jnp.dot(hmid, c2_ref[...], preferred_element_type=F32)
                half = y.shape[1] // 2
                ybuf[slot, :n, :] = _pack2(y[:, :half], y[:, half:])

            for q in range(1, rows // EXPERT_ROWS + 1):
                pl.when(pieces == q)(functools.partial(swiglu, q * EXPERT_ROWS))
            store(g, slot).start()
            return carry

        lax.fori_loop(0, nb, block, 0)

    @pl.when(e == pl.num_programs(0) - 1)
    def _():
        for back in (2, 1):
            @pl.when(total >= back)
            def _():
                g = total - back
                store(g, lax.rem(g, 2)).wait()


def _experts(xb, first_blk, n_blk, sizes, w1, w3, w2):
    P, W = xb.shape
    E, D, DE = w1.shape
    total = jnp.sum(n_blk, keepdims=True)
    wspec = lambda r, c: pl.BlockSpec((1, r, c), lambda e, *_: (e, 0, 0))
    grid_spec = pltpu.PrefetchScalarGridSpec(
        num_scalar_prefetch=4,
        grid=(E,),
        in_specs=[wspec(D, DE), wspec(D, DE), wspec(DE, D), pl.BlockSpec(memory_space=pl.ANY)],
        out_specs=pl.BlockSpec(memory_space=pl.ANY),
        scratch_shapes=[pltpu.VMEM((2, MOE_BLOCK, W), F32), pltpu.VMEM((2, MOE_BLOCK, W), F32),
                        pltpu.VMEM((D, DE), BF16), pltpu.VMEM((D, DE), BF16), pltpu.VMEM((DE, D), BF16),
                        pltpu.SemaphoreType.DMA((2,)), pltpu.SemaphoreType.DMA((2,))],
    )
    return pl.pallas_call(
        _expert_kernel,
        out_shape=jax.ShapeDtypeStruct((P, W), F32),
        grid_spec=grid_spec,
        compiler_params=_params(1),
        name="experts",
    )(first_blk, n_blk, sizes, total, w1, w3, w2, xb)


def _sc_gather_rows(table, idx):
    n, width = idx.shape[0], table.shape[1]
    nc, _ = _sc_workers()
    workers, per_worker, chunks = _sc_split(n)
    mesh = plsc.VectorSubcoreMesh(core_axis_name="c", subcore_axis_name="s")

    def body(table_hbm, idx_hbm, out_hbm, idx_v, rows_v, sem):
        wid = lax.axis_index("s") * nc + lax.axis_index("c")
        base = wid * per_worker
        pltpu.sync_copy(idx_hbm.at[wid], idx_v)

        def gather(chunk, buf):
            return pltpu.make_async_copy(table_hbm.at[idx_v.at[chunk]], rows_v.at[buf], sem)

        gather(0, 0).start()

        @pl.loop(0, chunks, step=2)
        def _(c):
            for b in range(2):
                chunk = c + b
                gather(chunk, b).wait()

                @pl.when(chunk + 1 < chunks)
                def _():
                    gather(chunk + 1, 1 - b).start()

                pltpu.sync_copy(rows_v.at[b], out_hbm.at[pl.ds(base + chunk * SC_ROWS, SC_ROWS)])

    return pl.kernel(
        body,
        out_type=jax.ShapeDtypeStruct((n, width), table.dtype),
        mesh=mesh,
        scratch_types=[pltpu.VMEM((chunks, SC_ROWS), jnp.int32),
                       pltpu.VMEM((2, SC_ROWS, width), table.dtype),
                       pltpu.SemaphoreType.DMA],
        name="sc_gather",
    )(table, idx.reshape(workers, chunks, SC_ROWS))


def _combine_dense_kernel(r0_ref, r1_ref, wts_ref, x1_ref, mod_ref, g_ref, b_ref, *rest):
    o_ref = rest[-1]
    w = wts_ref[...]
    y0 = jnp.concatenate(_unpack2(r0_ref[0]), axis=1)
    y1 = jnp.concatenate(_unpack2(r1_ref[0]), axis=1)
    y = w[:, 0:1] * y0 + w[:, 1:2] * y1
    gate2 = mod_ref[0, 5:6, :]
    o_ref[...] = _layer_norm(DN_ALPHA * x1_ref[...] + gate2 * y, g_ref[...], b_ref[...])


def _combine_dense(rows, wts, x1, mod, ln2_g, ln2_b, S, b, out):
    T, D = x1.shape
    tm = min(512, S)
    per_b = S // tm
    here = lambda i: (b * per_b + i, 0)
    in_specs = [pl.BlockSpec((1, tm, rows.shape[2]), lambda i: (0, i, 0)),
                pl.BlockSpec((1, tm, rows.shape[2]), lambda i: (1, i, 0)),
                pl.BlockSpec((tm, LANES), here),
                pl.BlockSpec((tm, D), here),
                pl.BlockSpec((1, 6, D), lambda i: (b, 0, 0)),
                pl.BlockSpec((1, D), lambda i: (0, 0)),
                pl.BlockSpec((1, D), lambda i: (0, 0))]
    args = [rows, rows, wts, x1, mod, ln2_g.reshape(1, D), ln2_b.reshape(1, D)]
    aliases = {}
    if out is not None:
        in_specs.append(pl.BlockSpec(memory_space=pl.ANY))
        aliases = {len(args): 0}
        args.append(out)
    return pl.pallas_call(
        _combine_dense_kernel,
        out_shape=jax.ShapeDtypeStruct((T, D), F32),
        grid=(per_b,),
        in_specs=in_specs,
        out_specs=pl.BlockSpec((tm, D), here),
        input_output_aliases=aliases,
        compiler_params=_params(1),
        name="combine",
    )(*args)


def _moe(h2, x1, route, wts, counts, mod, w1, w3, w2, ln2_g, ln2_b):
    B, S, D = x1.shape
    T = B * S
    P = 2 * T + N_EXPERTS * MOE_BLOCK
    sizes = counts[0, ROUTE_OFF:ROUTE_OFF + N_EXPERTS].astype(jnp.int32)
    n_blk = (sizes + MOE_BLOCK - 1) // MOE_BLOCK
    psizes = n_blk * MOE_BLOCK
    poffs = jnp.cumsum(psizes) - psizes
    sel = route[0:2, None, :] == jnp.arange(N_EXPERTS, dtype=jnp.int32)[None, :, None]
    dest = route[2:4] + jnp.sum(jnp.where(sel, poffs[None, :, None], 0), axis=1)
    xb = _sc_scatter_rows(h2.reshape(T, D // 2), dest[0], dest[1], P)
    yb = _experts(xb, poffs // MOE_BLOCK, n_blk, sizes, w1, w3, w2)
    out = None
    for b in range(B):
        slot_major = dest[:, b * S:(b + 1) * S].reshape(2 * S)
        rows = _sc_gather_rows(yb, slot_major).reshape(2, S, yb.shape[1])
        out = _combine_dense(rows, wts.reshape(T, LANES), x1.reshape(T, D), mod, ln2_g, ln2_b, S, b, out)
    return out.reshape(B, S, D)


def _layer(x, c, w_ada, b_ada, w_in, conv_w, conv_b, fw1, fb1, ff1, fw2, fb2, ff2, fw3, decay, skip,
           w_hy_o, w_attn_o, attn_sink, w_out, ln1_g, ln1_b, rg_w, rg_b, re_w, re_b, ew1, ew3, ew2,
           ln2_g, ln2_b):
    mod = _ada(c, w_ada, b_ada)
    q, kv, hv, hx1, hx2, ga, gh = _in_proj(x, mod, w_in, conv_w, conv_b)
    attn = _attention(q, kv, attn_sink)
    hy = _hyena(hv, hx1, hx2, fw1, fb1, ff1, fw2, fb2, ff2, fw3, decay, skip)
    x1, h2, route, wts, counts = _merge(attn, hy, ga, gh, x, mod, w_attn_o, w_hy_o, w_out,
                                        ln1_g, ln1_b, rg_w, rg_b, re_w, re_b)
    return _moe(h2, x1, route, wts, counts, mod, ew1, ew3, ew2, ln2_g, ln2_b)


def kernel(x, c, w_ada, b_ada, w_in, conv_w, conv_b, filt_w1, filt_b1, filt_freq1, filt_w2, filt_b2, filt_freq2, filt_w3, filt_decay, hy_skip, w_hy_o, w_attn_o, attn_sink, w_out, ln1_g, ln1_b, router_group_w, router_group_b, router_expert_w, router_expert_b, exp_w1, exp_w3, exp_w2, ln2_g, ln2_b):
    for l in range(w_ada.shape[0]):
        x = _layer(x, c, w_ada[l], b_ada[l], w_in[l], conv_w[l], conv_b[l], filt_w1[l], filt_b1[l],
                   filt_freq1[l], filt_w2[l], filt_b2[l], filt_freq2[l], filt_w3[l], filt_decay[l],
                   hy_skip[l], w_hy_o[l], w_attn_o[l], attn_sink[l], w_out[l], ln1_g[l], ln1_b[l],
                   router_group_w[l], router_group_b[l], router_expert_w[l], router_expert_b[l],
                   exp_w1[l], exp_w3[l], exp_w2[l], ln2_g[l], ln2_b[l])
    return x
```

```python
import functools
import math

import numpy as np
import jax
import jax.numpy as jnp
from jax import lax
from jax.experimental import pallas as pl
from jax.experimental.pallas import tpu as pltpu
from jax.experimental.pallas import tpu_sc as plsc

F32 = jnp.float32
BF16 = jnp.bfloat16

N_HEADS = 8
N_KV_HEADS = 2
HEAD_DIM = 64
ATTN_WIDTH = N_HEADS * HEAD_DIM
KV_WIDTH = N_KV_HEADS * HEAD_DIM
WINDOW = 128
HYENA_WIDTH = 512
FILTER_EMB = 33
FILTER_BANDS = (FILTER_EMB - 1) // 2
WINDOW_SHIFT = 0.05
N_GROUPS = 8
EXPERTS_PER_GROUP = 8
N_EXPERTS = N_GROUPS * EXPERTS_PER_GROUP
MOE_BLOCK = 1024
EXPERT_ROWS = 128
LN_EPS = 1e-5
DEPTH = 1
DN_ALPHA = (2.0 * DEPTH) ** 0.25
NEG = -1e30

LANES = 128
SUBLANES = 8
ROUTE_OFF = N_GROUPS
VMEM_LIMIT = 56 * 1024 * 1024


def _params(n_axes, vmem=VMEM_LIMIT):
    return pltpu.CompilerParams(dimension_semantics=("arbitrary",) * n_axes, vmem_limit_bytes=vmem)


def _split(a):
    hi = a.astype(BF16)
    lo = (a - hi.astype(F32)).astype(BF16)
    return hi, lo


def _dot3(a, b_hi, b_lo):
    a_hi, a_lo = _split(a)
    acc = jnp.dot(a_hi, b_hi, preferred_element_type=F32)
    acc = acc + jnp.dot(a_hi, b_lo, preferred_element_type=F32)
    acc = acc + jnp.dot(a_lo, b_hi, preferred_element_type=F32)
    return acc


def _pack2(a, b):
    ia = lax.bitcast_convert_type(a.astype(BF16).astype(F32), jnp.int32)
    ib = lax.bitcast_convert_type(b.astype(BF16).astype(F32), jnp.int32)
    return lax.bitcast_convert_type(ia | lax.shift_right_logical(ib, 16), F32)


def _unpack2(p):
    p = lax.bitcast_convert_type(p, jnp.int32)
    a = lax.bitcast_convert_type(p & jnp.int32(-65536), F32)
    b = lax.bitcast_convert_type(lax.shift_left(p, 16), F32)
    return a, b


def _sigmoid(x):
    return 0.5 * jnp.tanh(0.5 * x) + 0.5


def _layer_norm(r, g, b):
    mu = jnp.mean(r, axis=-1, keepdims=True)
    d = r - mu
    var = jnp.mean(d * d, axis=-1, keepdims=True)
    return d * lax.rsqrt(var + LN_EPS) * g + b


def _ada_kernel(c_ref, w_ref, b_ref, o_ref):
    c = c_ref[...]
    s = c * _sigmoid(c)
    wh, wl = _split(w_ref[...])
    o_ref[...] = _dot3(s, wh, wl) + b_ref[...]


def _ada(c, w_ada, b_ada):
    B, D = c.shape
    n_out = w_ada.shape[1]
    rows = SUBLANES
    cp = jnp.pad(c, ((0, rows - B), (0, 0)))
    tn = 1024
    out = pl.pallas_call(
        _ada_kernel,
        out_shape=jax.ShapeDtypeStruct((rows, n_out), F32),
        grid=(n_out // tn,),
        in_specs=[pl.BlockSpec((rows, D), lambda j: (0, 0)),
                  pl.BlockSpec((D, tn), lambda j: (0, j)),
                  pl.BlockSpec((1, tn), lambda j: (0, j))],
        out_specs=pl.BlockSpec((rows, tn), lambda j: (0, j)),
        compiler_params=_params(1),
        name="ada",
    )(cp, w_ada, b_ada.reshape(1, n_out))
    return out[:B].reshape(B, 6, D)


def _inproj_kernel(x_ref, xp_ref, xn_ref, mod_ref, w_ref, cw_ref, cb_ref,
                   q_ref, kv_ref, v_ref, x1_ref, x2_ref, ga_ref, gh_ref):
    i = pl.program_id(1)
    n = pl.num_programs(1)
    C = HYENA_WIDTH
    x = x_ref[0]
    tm, D = x.shape
    shift = mod_ref[0, 0:1, :]
    scale = mod_ref[0, 1:2, :]
    h = (x * (1.0 + scale) + shift).astype(BF16)

    def seg(lo, hi):
        return jnp.dot(h, w_ref[:, lo:hi], preferred_element_type=F32)

    o_q = 0
    o_kv = o_q + ATTN_WIDTH
    o_hy = o_kv + 2 * KV_WIDTH
    o_ga = o_hy + 3 * C
    o_gh = o_ga + D
    ga_ref[0] = _sigmoid(seg(o_ga, o_ga + D)).astype(BF16)
    gh_ref[0] = _sigmoid(seg(o_gh, o_gh + D)).astype(BF16)

    u = seg(o_hy, o_hy + 3 * C)
    xe = jnp.concatenate([xp_ref[0], xn_ref[0]], axis=0)
    he = (xe * (1.0 + scale) + shift).astype(BF16)
    ue = jnp.dot(he, w_ref[:, o_hy:o_hy + 3 * C], preferred_element_type=F32)
    prow = jnp.where(i > 0, ue[SUBLANES - 1:SUBLANES], 0.0)
    nrow = jnp.where(i < n - 1, ue[SUBLANES:SUBLANES + 1], 0.0)
    rid = lax.broadcasted_iota(jnp.int32, (tm, 1), 0)
    up = jnp.where(rid == 0, prow, pltpu.roll(u, 1, 0))
    dn = jnp.where(rid == tm - 1, nrow, pltpu.roll(u, tm - 1, 0))
    conv = cw_ref[0:1, :] * up + cw_ref[1:2, :] * u + cw_ref[2:3, :] * dn + cb_ref[...]
    v_ref[0] = conv[:, :C]
    x1_ref[0] = conv[:, C:2 * C]
    x2_ref[0] = conv[:, 2 * C:]

    q_ref[0] = (seg(o_q, o_q + ATTN_WIDTH) * (HEAD_DIM ** -0.5)).astype(BF16)
    kv_ref[0] = seg(o_kv, o_kv + 2 * KV_WIDTH).astype(BF16)


def _in_proj(x, mod, w_in, conv_w, conv_b):
    B, S, D = x.shape
    C = HYENA_WIDTH
    tm = min(1024, S)
    r8 = tm // SUBLANES
    nb8 = S // SUBLANES
    wb = w_in.astype(BF16)
    nw = wb.shape[1]
    row = lambda b, i: (b, i, 0)
    shapes = [(ATTN_WIDTH, BF16), (2 * KV_WIDTH, BF16), (C, F32), (C, F32), (C, F32), (D, BF16), (D, BF16)]
    return pl.pallas_call(
        _inproj_kernel,
        out_shape=[jax.ShapeDtypeStruct((B, S, w), dt) for w, dt in shapes],
        grid=(B, S // tm),
        in_specs=[pl.BlockSpec((1, tm, D), row),
                  pl.BlockSpec((1, SUBLANES, D), lambda b, i: (b, jnp.maximum(i * r8 - 1, 0), 0)),
                  pl.BlockSpec((1, SUBLANES, D), lambda b, i: (b, jnp.minimum((i + 1) * r8, nb8 - 1), 0)),
                  pl.BlockSpec((1, 6, D), lambda b, i: (b, 0, 0)),
                  pl.BlockSpec((D, nw), lambda b, i: (0, 0)),
                  pl.BlockSpec((3, 3 * C), lambda b, i: (0, 0)),
                  pl.BlockSpec((1, 3 * C), lambda b, i: (0, 0))],
        out_specs=[pl.BlockSpec((1, tm, w), row) for w, _ in shapes],
        compiler_params=_params(2),
        name="in_proj",
    )(x, x, x, mod, wb, conv_w.astype(F32), conv_b.reshape(1, 3 * C).astype(F32))


ATT_TQ = 512
ATT_QB = 128
ATT_STACK = 4


def _attn_kernel(sink_ref, q_ref, kvp_ref, kvc_ref, kvn_ref, bias_ref, o_ref, kv_scr, vx_scr, *, seq_len):
    i = pl.program_id(1)
    H = WINDOW
    TQ = q_ref.shape[1]
    Q = min(ATT_QB, TQ)
    band = Q + 2 * H
    G = N_HEADS // N_KV_HEADS
    kv_scr[0:H] = kvp_ref[0]
    kv_scr[H:H + TQ] = kvc_ref[0]
    kv_scr[H + TQ:] = kvn_ref[0]
    for kv in range(N_KV_HEADS):
        vx_scr[:, kv * LANES:kv * LANES + HEAD_DIM] = kv_scr[:, KV_WIDTH + kv * HEAD_DIM:KV_WIDTH + (kv + 1) * HEAD_DIM]
        vx_scr[:, kv * LANES + HEAD_DIM:(kv + 1) * LANES] = jnp.ones((TQ + 2 * H, LANES - HEAD_DIM), BF16)
    col = lax.broadcasted_iota(jnp.int32, (1, band), 1)
    rhead = lax.broadcasted_iota(jnp.int32, (ATT_STACK * Q, 1), 0) // Q
    for j in range(TQ // Q):
        kpos = i * TQ + j * Q - H + col
        colbias = jnp.where((kpos >= 0) & (kpos < seq_len), 0.0, NEG)
        for kv in range(N_KV_HEADS):
            kk = kv_scr[j * Q:j * Q + band, kv * HEAD_DIM:(kv + 1) * HEAD_DIM]
            vx = vx_scr[j * Q:j * Q + band, kv * LANES:(kv + 1) * LANES]
            for sub in range(G // ATT_STACK):
                first = sub * ATT_STACK
                heads = [kv * G + first + g for g in range(ATT_STACK)]
                qg = jnp.concatenate([q_ref[0, j * Q:(j + 1) * Q, h * HEAD_DIM:(h + 1) * HEAD_DIM] for h in heads],
                                     axis=0)
                s = lax.dot_general(qg, kk, (((1,), (1,)), ((), ())), preferred_element_type=F32)
                s = s + bias_ref[kv, first * Q:(first + ATT_STACK) * Q, :] + colbias
                snk = sink_ref[heads[-1]]
                for g in range(ATT_STACK - 2, -1, -1):
                    snk = jnp.where(rhead == g, sink_ref[heads[g]], snk)
                m = jnp.maximum(jnp.max(s, axis=1, keepdims=True), snk)
                p = jnp.exp(s - m).astype(BF16)
                ox = jnp.dot(p, vx, preferred_element_type=F32)
                den = ox[:, HEAD_DIM:HEAD_DIM + 1] + jnp.exp(snk - m)
                o = ox[:, :HEAD_DIM] / den
                for g, h in enumerate(heads):
                    o_ref[0, j * Q:(j + 1) * Q, h * HEAD_DIM:(h + 1) * HEAD_DIM] = o[g * Q:(g + 1) * Q].astype(BF16)


def _attention(q, kv, sink):
    B, S, _ = q.shape
    H = WINDOW
    TQ = min(ATT_TQ, S)
    Q = min(ATT_QB, TQ)
    r = TQ // H
    nq = S // H
    G = N_HEADS // N_KV_HEADS
    assert G % ATT_STACK == 0
    a = jnp.arange(Q)[:, None]
    j = jnp.arange(Q + 2 * H)[None, :]
    rel = jnp.abs(j - H - a).astype(F32)
    slopes = 2.0 ** (-8.0 * jnp.arange(1, N_HEADS + 1, dtype=F32) / N_HEADS)
    bias = jnp.where(rel[None] <= WINDOW, -slopes[:, None, None] * rel[None], NEG).astype(F32)
    bias = bias.reshape(N_KV_HEADS, G * Q, Q + 2 * H)
    cur = lambda b, i: (b, i, 0)
    return pl.pallas_call(
        functools.partial(_attn_kernel, seq_len=S),
        out_shape=jax.ShapeDtypeStruct((B, S, ATTN_WIDTH), BF16),
        grid=(B, S // TQ),
        in_specs=[pl.BlockSpec(memory_space=pltpu.SMEM),
                  pl.BlockSpec((1, TQ, ATTN_WIDTH), cur),
                  pl.BlockSpec((1, H, 2 * KV_WIDTH), lambda b, i: (b, jnp.maximum(i * r - 1, 0), 0)),
                  pl.BlockSpec((1, TQ, 2 * KV_WIDTH), cur),
                  pl.BlockSpec((1, H, 2 * KV_WIDTH), lambda b, i: (b, jnp.minimum((i + 1) * r, nq - 1), 0)),
                  pl.BlockSpec((N_KV_HEADS, G * Q, Q + 2 * H), lambda b, i: (0, 0, 0))],
        out_specs=pl.BlockSpec((1, TQ, ATTN_WIDTH), cur),
        scratch_shapes=[pltpu.VMEM((TQ + 2 * H, 2 * KV_WIDTH), BF16),
                        pltpu.VMEM((TQ + 2 * H, N_KV_HEADS * LANES), BF16)],
        compiler_params=_params(2),
        name="attn",
    )(sink.astype(F32), q, kv, kv, kv, bias)


def _filter_kernel(z_ref, w1h, w1l, b1_ref, f1_ref, w2h, w2l, b2_ref, f2_ref, w3h, w3l, dec_ref,
                   k_ref, s_ref):
    i = pl.program_id(0)
    z = z_ref[...]
    h1 = jnp.sin(f1_ref[...] * (_dot3(z, w1h[...], w1l[...]) + b1_ref[...]))
    h2 = jnp.sin(f2_ref[...] * (_dot3(h1, w2h[...], w2l[...]) + b2_ref[...]))
    k = _dot3(h2, w3h[...], w3l[...])
    t = z[:, 0:1]
    k = k * (jnp.exp(-t * jnp.abs(dec_ref[...])) + WINDOW_SHIFT)
    k_ref[...] = k

    @pl.when(i == 0)
    def _():
        s_ref[...] = jnp.zeros_like(s_ref)

    s_ref[...] += jnp.sum(jnp.abs(k), axis=0, keepdims=True)


def _filter_embedding(L):
    t = np.linspace(0.0, 1.0, L, dtype=np.float32).astype(np.float64)[:, None]
    w = (2.0 * math.pi * np.arange(L, dtype=np.float32) / np.float32(L)).astype(np.float64)[:, None]
    bands = np.linspace(1e-4, FILTER_BANDS - 1, FILTER_BANDS, dtype=np.float32).astype(np.float64)[None, :]
    bw = (bands.astype(np.float32) * w.astype(np.float32)).astype(np.float64)
    z = np.concatenate([t, np.cos(bw), -np.sin(bw)], axis=-1)
    zp = np.zeros((L, LANES), np.float32)
    zp[:, :FILTER_EMB] = z.astype(np.float32)
    return jnp.asarray(zp)


def _pad2(a, r, c):
    return jnp.zeros((r, c), F32).at[:a.shape[0], :a.shape[1]].set(a.astype(F32))


def _filters(L, fw1, fb1, ff1, fw2, fb2, ff2, fw3, decay):
    H = LANES
    nf = fw3.shape[1]
    z = _filter_embedding(L)
    w1h, w1l = _split(_pad2(fw1, H, H))
    w2h, w2l = _split(_pad2(fw2, H, H))
    w3h, w3l = _split(_pad2(fw3, H, nf))
    b1 = _pad2(fb1[None], 1, H)
    f1 = _pad2(ff1[None], 1, H)
    b2 = _pad2(fb2[None], 1, H)
    f2 = _pad2(ff2[None], 1, H)
    tr = min(512, L)
    full = lambda r, c: pl.BlockSpec((r, c), lambda i: (0, 0))
    return pl.pallas_call(
        _filter_kernel,
        out_shape=[jax.ShapeDtypeStruct((L, nf), F32), jax.ShapeDtypeStruct((1, nf), F32)],
        grid=(L // tr,),
        in_specs=[pl.BlockSpec((tr, H), lambda i: (i, 0)),
                  full(H, H), full(H, H), full(1, H), full(1, H),
                  full(H, H), full(H, H), full(1, H), full(1, H),
                  full(H, nf), full(H, nf), full(1, nf)],
        out_specs=[pl.BlockSpec((tr, nf), lambda i: (i, 0)), full(1, nf)],
        compiler_params=_params(1),
        name="filter",
    )(z, w1h, w1l, b1, f1, w2h, w2l, b2, f2, w3h, w3l, decay.reshape(1, nf).astype(F32))


def _np_bf16(m64):
    return jnp.asarray(m64.astype(np.float32).astype(BF16))


def _dft_constants(L):
    N = 2 * L
    n2 = LANES
    n1 = N // n2
    h1 = n1 // 2
    k1 = np.arange(n1)[:, None]
    s1 = np.arange(h1)[None, :]
    ang = -2.0 * np.pi * ((k1 * s1) % n1) / n1
    wr, wi = np.cos(ang), np.sin(ang)
    w1_filt = np.block([[wr, wr], [wi, wi], [wr, -wr], [wi, -wi]])
    w1_cplx = np.block([[wr, -wi], [wi, wr]])
    vr, vi = wr.T / N, -wi.T / N
    w3 = np.block([[vr, -vi], [vi, vr]])
    k2 = np.arange(n2)[:, None]
    s2 = np.arange(n2)[None, :]
    a2 = -2.0 * np.pi * ((k2 * s2) % n2) / n2
    w2r, w2i = jnp.asarray(np.cos(a2), F32), jnp.asarray(np.sin(a2), F32)
    at = -2.0 * np.pi * ((np.arange(n1)[:, None] * s2) % N) / N
    twr, twi = jnp.asarray(np.cos(at), F32), jnp.asarray(np.sin(at), F32)
    mr = w2r[None] * twr[:, None, :] - w2i[None] * twi[:, None, :]
    mi = w2r[None] * twi[:, None, :] + w2i[None] * twr[:, None, :]
    fwd = jnp.concatenate([jnp.concatenate([mr, -mi], axis=2),
                           jnp.concatenate([mi, mr], axis=2)], axis=1)
    fwd = fwd.astype(BF16)
    return dict(n1=n1, w1_filt=_np_bf16(w1_filt), w1_cplx=_np_bf16(w1_cplx), w3=_np_bf16(w3),
                fwd=fwd)


SCH = 16


def _dft1_kernel(x_ref, w_ref, a_ref, *, n1):
    w = w_ref[...]
    for j in range(SCH):
        rhs = jnp.concatenate([x_ref[0, 0, :, j, :], x_ref[0, 1, :, j, :]], axis=0)
        res = jnp.dot(w, rhs.astype(BF16), preferred_element_type=F32)
        a_ref[0, :, j, :] = _pack2(res[:n1], res[n1:])


def _dft1_data(x, consts):
    B, L, C = x.shape
    n1 = consts["n1"]
    h1 = n1 // 2
    xv = x.reshape(B // 2, 2, h1, LANES, C)
    return pl.pallas_call(
        functools.partial(_dft1_kernel, n1=n1),
        out_shape=jax.ShapeDtypeStruct((B // 2, n1, LANES, C), F32),
        grid=(B // 2, LANES // SCH),
        in_specs=[pl.BlockSpec((1, 2, h1, SCH, C), lambda p, j: (p, 0, 0, j, 0)),
                  pl.BlockSpec((2 * n1, n1), lambda p, j: (0, 0))],
        out_specs=pl.BlockSpec((1, n1, SCH, C), lambda p, j: (p, 0, j, 0)),
        compiler_params=_params(2),
        name="dft1",
    )(xv, consts["w1_cplx"])


def _dft1f_kernel(x_ref, w_ref, a_ref, *, n1):
    C = HYENA_WIDTH
    w = w_ref[...]
    for j in range(SCH):
        rhs = jnp.concatenate([x_ref[:, j, :C], x_ref[:, j, C:]], axis=0)
        res = jnp.dot(w, rhs.astype(BF16), preferred_element_type=F32)
        a_ref[0, :, 0, j, :] = _pack2(res[:n1], res[n1:2 * n1])
        a_ref[0, :, 1, j, :] = _pack2(res[2 * n1:3 * n1], res[3 * n1:])


def _dft1_filter(kraw, consts):
    L, nf = kraw.shape
    C = HYENA_WIDTH
    n_ord = nf // (2 * C)
    n1 = consts["n1"]
    h1 = n1 // 2
    kv = kraw.reshape(h1, LANES, nf)
    return pl.pallas_call(
        functools.partial(_dft1f_kernel, n1=n1),
        out_shape=jax.ShapeDtypeStruct((n_ord, n1, 2, LANES, C), F32),
        grid=(n_ord, LANES // SCH),
        in_specs=[pl.BlockSpec((h1, SCH, 2 * C), lambda o, j: (0, j, o)),
                  pl.BlockSpec((4 * n1, n1), lambda o, j: (0, 0))],
        out_specs=pl.BlockSpec((1, n1, 2, SCH, C), lambda o, j: (o, 0, 0, j, 0)),
        compiler_params=_params(2),
        name="dft1f",
    )(kv, consts["w1_filt"])


KCH = 16


def _mid_kernel(a_ref, af_ref, f_ref, inv_ref, b0_ref, b_ref, hr_scr, hi_scr):
    n2 = LANES

    @pl.when(pl.program_id(1) == 0)
    def _():
        sc = inv_ref[0]
        for k in range(KCH):
            p = jnp.concatenate(_unpack2(af_ref[0, k, :n2, :]), axis=0).astype(BF16)
            q = jnp.concatenate(_unpack2(af_ref[0, k, n2:, :]), axis=0).astype(BF16)
            hr_scr[k] = (jnp.dot(f_ref[k, :n2, :], p, preferred_element_type=F32) - b0_ref[0]) * sc
            hi_scr[k] = jnp.dot(f_ref[k, n2:, :], q, preferred_element_type=F32) * sc

    for k in range(KCH):
        a = jnp.concatenate(_unpack2(a_ref[0, k]), axis=0).astype(BF16)
        x = jnp.dot(f_ref[k], a, preferred_element_type=F32)
        xr, xi = x[:n2], x[n2:]
        hr, hi = hr_scr[k], hi_scr[k]
        y = jnp.concatenate([xr * hr - xi * hi, xr * hi + xi * hr], axis=0)
        b = lax.dot_general(f_ref[k], y.astype(BF16), (((0,), (0,)), ((), ())), preferred_element_type=F32)
        b_ref[0, k] = _pack2(b[:n2], b[n2:])


def _mid(a, af, inv_den, bwd0, order, consts):
    P, n1, n2, C = a.shape
    afl = af.reshape(af.shape[0], n1, 2 * n2, C)
    tab = pl.BlockSpec((KCH, 2 * n2, 2 * n2), lambda k, p: (k, 0, 0))
    vec = pl.BlockSpec((1, 1, C), lambda k, p: (order, 0, 0))
    return pl.pallas_call(
        _mid_kernel,
        out_shape=jax.ShapeDtypeStruct((P, n1, n2, C), F32),
        grid=(n1 // KCH, P),
        in_specs=[pl.BlockSpec((1, KCH, n2, C), lambda k, p: (p, k, 0, 0)),
                  pl.BlockSpec((1, KCH, 2 * n2, C), lambda k, p: (order, k, 0, 0)),
                  tab, vec, vec],
        out_specs=pl.BlockSpec((1, KCH, n2, C), lambda k, p: (p, k, 0, 0)),
        scratch_shapes=[pltpu.VMEM((KCH, n2, C), F32), pltpu.VMEM((KCH, n2, C), F32)],
        compiler_params=_params(2),
        name="mid",
    )(a, afl, consts["fwd"], inv_den, bwd0)


def _dft3_kernel(b_ref, w_ref, v_ref, g_ref, skip_ref, *rest, h1, chain):
    if chain:
        w1_ref, z_ref, a_ref, slab_ref = rest
        w1 = w1_ref[...]
    else:
        z_ref, slab_ref = rest
    w = w_ref[...]
    skip = skip_ref[0]
    n1 = 2 * h1
    for j in range(SCH):
        slab_ref[...] = b_ref[0, :, j, :]
        rhs = jnp.concatenate(_unpack2(slab_ref[...]), axis=0)
        y = jnp.dot(w, rhs.astype(BF16), preferred_element_type=F32)
        z = [g_ref[0, r, :, j, :] * (y[r * h1:(r + 1) * h1] + v_ref[0, r, :, j, :] * skip) for r in range(2)]
        for r in range(2):
            z_ref[0, r, :, j, :] = z[r]
        if chain:
            res = jnp.dot(w1, jnp.concatenate(z, axis=0).astype(BF16), preferred_element_type=F32)
            a_ref[0, :, j, :] = _pack2(res[:n1], res[n1:])


def _dft3_gate(b5, v, gate, skip, consts, chain):
    P, n1, n2, C = b5.shape
    h1 = n1 // 2
    B, L, _ = v.shape
    five = lambda t: t.reshape(P, 2, h1, n2, C)
    dat = pl.BlockSpec((1, 2, h1, SCH, C), lambda p, j: (p, 0, 0, j, 0))
    packed = pl.BlockSpec((1, n1, SCH, C), lambda p, j: (p, 0, j, 0))
    in_specs = [packed, pl.BlockSpec((n1, 2 * n1), lambda p, j: (0, 0)), dat, dat,
                pl.BlockSpec((1, C), lambda p, j: (0, 0))]
    args = [b5, consts["w3"], five(v), five(gate), skip.reshape(1, C).astype(F32)]
    out_shape = [jax.ShapeDtypeStruct((P, 2, h1, n2, C), F32)]
    out_specs = [dat]
    if chain:
        in_specs.append(pl.BlockSpec((2 * n1, n1), lambda p, j: (0, 0)))
        args.append(consts["w1_cplx"])
        out_shape.append(jax.ShapeDtypeStruct((P, n1, n2, C), F32))
        out_specs.append(packed)
    outs = pl.pallas_call(
        functools.partial(_dft3_kernel, h1=h1, chain=chain),
        out_shape=out_shape,
        grid=(P, n2 // SCH),
        in_specs=in_specs,
        out_specs=out_specs,
        scratch_shapes=[pltpu.VMEM((n1, C), F32)],
        compiler_params=_params(2),
        name="dft3",
    )(*args)
    z = outs[0].reshape(B, L, C)
    return (z, outs[1]) if chain else (z, None)


def _hyena(v, x1, x2, fw1, fb1, ff1, fw2, fb2, ff2, fw3, decay, skip):
    B, L, C = v.shape
    consts = _dft_constants(L)
    kraw, ksum = _filters(L, fw1, fb1, ff1, fw2, fb2, ff2, fw3, decay)
    ks = ksum.reshape(2, 2, C)
    inv_den = (1.0 / (ks[:, 0] + ks[:, 1])).reshape(2, 1, C)
    bwd0 = kraw[0].reshape(2, 2, C)[:, 1].reshape(2, 1, C)
    af = _dft1_filter(kraw, consts)
    gates = (x1, x2)
    z, a5 = v, _dft1_data(v, consts)
    for o, gate in enumerate(gates):
        b5 = _mid(a5, af, inv_den, bwd0, o, consts)
        z, a5 = _dft3_gate(b5, z, gate, skip[o], consts, chain=o + 1 < len(gates))
    return z


def _merge_kernel(attn_ref, hy_ref, ga_ref, gh_ref, x_ref, mod_ref, wa_ref, wh_ref, wo_ref,
                  g1_ref, b1_ref, rwh_ref, rwl_ref, rb_ref, tri_ref,
                  x1_ref, h2_ref, route_ref, wts_ref, cnt_ref, carry_ref):
    @pl.when((pl.program_id(0) == 0) & (pl.program_id(1) == 0))
    def _():
        carry_ref[...] = jnp.zeros_like(carry_ref)

    logits = _merge_dense(attn_ref, hy_ref, ga_ref, gh_ref, x_ref, mod_ref, wa_ref, wh_ref, wo_ref,
                          g1_ref, b1_ref, rwh_ref, rwl_ref, rb_ref, x1_ref, h2_ref)
    _route_rows(logits, tri_ref, route_ref, wts_ref, carry_ref)
    cnt_ref[...] = carry_ref[...]


def _merge_dense(attn_ref, hy_ref, ga_ref, gh_ref, x_ref, mod_ref, wa_ref, wh_ref, wo_ref,
                 g1_ref, b1_ref, rwh_ref, rwl_ref, rb_ref, x1_ref, h2_ref):
    a = jnp.dot(attn_ref[0], wa_ref[...], preferred_element_type=F32)
    hy = jnp.dot(hy_ref[0].astype(BF16), wh_ref[...], preferred_element_type=F32)
    merged = ga_ref[0].astype(F32) * a + gh_ref[0].astype(F32) * hy
    y = jnp.dot(merged.astype(BF16), wo_ref[...], preferred_element_type=F32)
    gate1 = mod_ref[0, 2:3, :]
    shift2 = mod_ref[0, 3:4, :]
    scale2 = mod_ref[0, 4:5, :]
    x1 = _layer_norm(DN_ALPHA * x_ref[0] + gate1 * y, g1_ref[...], b1_ref[...])
    x1_ref[0] = x1
    h2 = x1 * (1.0 + scale2) + shift2
    half = h2.shape[1] // 2
    h2_ref[0] = _pack2(h2[:, :half], h2[:, half:])
    return _dot3(h2, rwh_ref[...], rwl_ref[...]) + rb_ref[...]


def _route_rows(logits, tri_ref, route_ref, wts_ref, carry_ref):
    tm = logits.shape[0]
    lane = lax.broadcasted_iota(jnp.int32, (tm, LANES), 1)
    lanef = lane.astype(F32)
    big = float(LANES)

    def first_lane(mask):
        return jnp.min(jnp.where(mask, lanef, big), axis=1, keepdims=True).astype(jnp.int32)

    gmask = lane < N_GROUPS
    gl = jnp.where(gmask, logits, NEG)
    gmax = jnp.max(gl, axis=1, keepdims=True)
    gidx = first_lane(gl == gmax)
    pg = 1.0 / jnp.sum(jnp.exp(gl - gmax), axis=1, keepdims=True)
    lo = ROUTE_OFF + gidx * EXPERTS_PER_GROUP
    emask = (lane >= lo) & (lane < lo + EXPERTS_PER_GROUP)
    el = jnp.where(emask, logits, NEG)
    v1 = jnp.max(el, axis=1, keepdims=True)
    i1 = first_lane(el == v1)
    el2 = jnp.where(emask & (lane != i1), logits, NEG)
    v2 = jnp.max(el2, axis=1, keepdims=True)
    i2 = first_lane(el2 == v2)
    e21 = jnp.exp(v2 - v1)
    w1 = pg / (1.0 + e21)
    w2 = pg * e21 / (1.0 + e21)

    sel1 = lane == i1
    sel2 = lane == i2
    onehot = jnp.where(sel1 | sel2, 1.0, 0.0)
    prefix = jnp.dot(tri_ref[...], onehot.astype(BF16), preferred_element_type=F32) + carry_ref[...]
    r1 = jnp.sum(jnp.where(sel1, prefix, 0.0), axis=1, keepdims=True)
    r2 = jnp.sum(jnp.where(sel2, prefix, 0.0), axis=1, keepdims=True)
    carry_ref[...] += jnp.sum(onehot, axis=0, keepdims=True)

    e1 = (i1 - ROUTE_OFF).astype(F32)
    e2 = (i2 - ROUTE_OFF).astype(F32)
    table = jnp.where(lane == 0, e1, jnp.where(lane == 1, e2, jnp.where(lane == 2, r1, jnp.where(lane == 3, r2, 0.0))))
    route_ref[...] = table.T[:SUBLANES].astype(jnp.int32)
    wts_ref[0] = jnp.where(lane == 0, w1, jnp.where(lane == 1, w2, 0.0))


def _merge(attn, hy, ga, gh, x, mod, w_attn_o, w_hy_o, w_out, ln1_g, ln1_b, rg_w, rg_b, re_w, re_b):
    B, S, D = x.shape
    tm = min(512, S)
    spare = LANES - N_GROUPS - N_EXPERTS
    rw = jnp.concatenate([rg_w, re_w, jnp.zeros((D, spare), F32)], axis=1)
    rb = jnp.concatenate([rg_b, re_b, jnp.zeros((spare,), F32)]).reshape(1, LANES)
    rwh, rwl = _split(rw)
    tri = (jnp.arange(tm)[:, None] > jnp.arange(tm)[None, :]).astype(BF16)
    row = lambda b, i: (b, i, 0)
    full = lambda r, c: pl.BlockSpec((r, c), lambda b, i: (0, 0))
    per_b = S // tm
    outs = [jax.ShapeDtypeStruct((B, S, D), F32), jax.ShapeDtypeStruct((B, S, D // 2), F32),
            jax.ShapeDtypeStruct((SUBLANES, B * S), jnp.int32), jax.ShapeDtypeStruct((B, S, LANES), F32),
            jax.ShapeDtypeStruct((1, LANES), F32)]
    return pl.pallas_call(
        _merge_kernel,
        out_shape=outs,
        grid=(B, per_b),
        in_specs=[pl.BlockSpec((1, tm, ATTN_WIDTH), row), pl.BlockSpec((1, tm, HYENA_WIDTH), row),
                  pl.BlockSpec((1, tm, D), row), pl.BlockSpec((1, tm, D), row), pl.BlockSpec((1, tm, D), row),
                  pl.BlockSpec((1, 6, D), lambda b, i: (b, 0, 0)),
                  full(ATTN_WIDTH, D), full(HYENA_WIDTH, D), full(D, D),
                  full(1, D), full(1, D), full(D, LANES), full(D, LANES), full(1, LANES), full(tm, tm)],
        out_specs=[pl.BlockSpec((1, tm, D), row), pl.BlockSpec((1, tm, D // 2), row),
                   pl.BlockSpec((SUBLANES, tm), lambda b, i: (0, b * per_b + i)),
                   pl.BlockSpec((1, tm, LANES), row), full(1, LANES)],
        scratch_shapes=[pltpu.VMEM((1, LANES), F32)],
        compiler_params=_params(2),
        name="merge",
    )(attn, hy, ga, gh, x, mod, w_attn_o.astype(BF16), w_hy_o.astype(BF16), w_out.astype(BF16),
      ln1_g.reshape(1, D), ln1_b.reshape(1, D), rwh, rwl, rb, tri)


SC_ROWS = 64


def _sc_workers():
    info = plsc.get_sparse_core_info()
    return info.num_cores, info.num_cores * info.num_subcores


def _sc_split(n):
    _, workers = _sc_workers()
    per_worker = n // workers
    chunks = per_worker // SC_ROWS
    assert per_worker * workers == n and chunks * SC_ROWS == per_worker and chunks % 2 == 0
    return workers, per_worker, chunks


def _sc_scatter_rows(src, idx0, idx1, n_out):
    n, width = src.shape
    nc, _ = _sc_workers()
    workers, per_worker, chunks = _sc_split(n)
    mesh = plsc.VectorSubcoreMesh(core_axis_name="c", subcore_axis_name="s")

    def body(src_hbm, i0_hbm, i1_hbm, out_hbm, i0_v, i1_v, rows_v, sem, ssem):
        wid = lax.axis_index("s") * nc + lax.axis_index("c")
        base = wid * per_worker
        pltpu.sync_copy(i0_hbm.at[wid], i0_v)
        pltpu.sync_copy(i1_hbm.at[wid], i1_v)

        def load(chunk, buf):
            return pltpu.make_async_copy(src_hbm.at[pl.ds(base + chunk * SC_ROWS, SC_ROWS)], rows_v.at[buf], sem)

        load(0, 0).start()

        @pl.loop(0, chunks, step=2)
        def _(c):
            for b in range(2):
                chunk = c + b
                load(chunk, b).wait()

                @pl.when(chunk + 1 < chunks)
                def _():
                    load(chunk + 1, 1 - b).start()

                first = pltpu.make_async_copy(rows_v.at[b], out_hbm.at[i0_v.at[chunk]], ssem)
                second = pltpu.make_async_copy(rows_v.at[b], out_hbm.at[i1_v.at[chunk]], ssem)
                first.start()
                second.start()
                first.wait()
                second.wait()

    shaped = lambda i: i.reshape(workers, chunks, SC_ROWS)
    return pl.kernel(
        body,
        out_type=jax.ShapeDtypeStruct((n_out, width), src.dtype),
        mesh=mesh,
        scratch_types=[pltpu.VMEM((chunks, SC_ROWS), jnp.int32),
                       pltpu.VMEM((chunks, SC_ROWS), jnp.int32),
                       pltpu.VMEM((2, SC_ROWS, width), src.dtype),
                       pltpu.SemaphoreType.DMA, pltpu.SemaphoreType.DMA],
        name="sc_scatter",
    )(src, shaped(idx0), shaped(idx1))


def _expert_kernel(first_ref, nb_ref, sz_ref, tot_ref, w1_ref, w3_ref, w2_ref, xb_ref, yb_ref,
                   xbuf, ybuf, c1_ref, c3_ref, c2_ref, lsem, ssem):
    e = pl.program_id(0)
    nb = nb_ref[e]
    first = first_ref[e]
    total = tot_ref[0]
    rows = xbuf.shape[1]

    def load(g, slot):
        src = xb_ref.at[pl.ds(pl.multiple_of(g * rows, rows), rows)]
        return pltpu.make_async_copy(src, xbuf.at[slot], lsem.at[slot])

    def store(g, slot):
        dst = yb_ref.at[pl.ds(pl.multiple_of(g * rows, rows), rows)]
        return pltpu.make_async_copy(ybuf.at[slot], dst, ssem.at[slot])

    @pl.when((e == 0) & (total > 0))
    def _():
        load(0, 0).start()

    @pl.when(nb > 0)
    def _():
        c1_ref[...] = w1_ref[0].astype(BF16)
        c3_ref[...] = w3_ref[0].astype(BF16)
        c2_ref[...] = w2_ref[0].astype(BF16)

        def block(j, carry):
            g = first + j
            slot = lax.rem(g, 2)
            load(g, slot).wait()

            @pl.when(g + 1 < total)
            def _():
                load(g + 1, 1 - slot).start()

            @pl.when(g >= 2)
            def _():
                store(g - 2, slot).wait()

            n_valid = sz_ref[e] - j * rows
            pieces = jnp.minimum((n_valid + EXPERT_ROWS - 1) // EXPERT_ROWS, rows // EXPERT_ROWS)

            def swiglu(n):
                rid = lax.broadcasted_iota(jnp.int32, (n, 1), 0)
                xa, xb = _unpack2(jnp.where(rid < n_valid, xbuf[slot, :n, :], 0.0))
                x = jnp.concatenate([xa, xb], axis=1).astype(BF16)
                a = jnp.dot(x, c1_ref[...], preferred_element_type=F32)
                gate = jnp.dot(x, c3_ref[...], preferred_element_type=F32)
                hmid = (a * _sigmoid(a) * gate).astype(BF16)
                y = jnp.dot(hmid, c2_ref[...], preferred_element_type=F32)
                half = y.shape[1] // 2
                ybuf[slot, :n, :] = _pack2(y[:, :half], y[:, half:])

            for q in range(1, rows // EXPERT_ROWS + 1):
                pl.when(pieces == q)(functools.partial(swiglu, q * EXPERT_ROWS))
            store(g, slot).start()
            return carry

        lax.fori_loop(0, nb, block, 0)

    @pl.when(e == pl.num_programs(0) - 1)
    def _():
        for back in (2, 1):
            @pl.when(total >= back)
            def _():
                g = total - back
                store(g, lax.rem(g, 2)).wait()


def _experts(xb, first_blk, n_blk, sizes, w1, w3, w2):
    P, W = xb.shape
    E, D, DE = w1.shape
    total = jnp.sum(n_blk, keepdims=True)
    wspec = lambda r, c: pl.BlockSpec((1, r, c), lambda e, *_: (e, 0, 0))
    grid_spec = pltpu.PrefetchScalarGridSpec(
        num_scalar_prefetch=4,
        grid=(E,),
        in_specs=[wspec(D, DE), wspec(D, DE), wspec(DE, D), pl.BlockSpec(memory_space=pl.ANY)],
        out_specs=pl.BlockSpec(memory_space=pl.ANY),
        scratch_shapes=[pltpu.VMEM((2, MOE_BLOCK, W), F32), pltpu.VMEM((2, MOE_BLOCK, W), F32),
                        pltpu.VMEM((D, DE), BF16), pltpu.VMEM((D, DE), BF16), pltpu.VMEM((DE, D), BF16),
                        pltpu.SemaphoreType.DMA((2,)), pltpu.SemaphoreType.DMA((2,))],
    )
    return pl.pallas_call(
        _expert_kernel,
        out_shape=jax.ShapeDtypeStruct((P, W), F32),
        grid_spec=grid_spec,
        compiler_params=_params(1),
        name="experts",
    )(first_blk, n_blk, sizes, total, w1, w3, w2, xb)


def _sc_gather_rows(table, idx):
    n, width = idx.shape[0], table.shape[1]
    nc, _ = _sc_workers()
    workers, per_worker, chunks = _sc_split(n)
    mesh = plsc.VectorSubcoreMesh(core_axis_name="c", subcore_axis_name="s")

    def body(table_hbm, idx_hbm, out_hbm, idx_v, rows_v, sem):
        wid = lax.axis_index("s") * nc + lax.axis_index("c")
        base = wid * per_worker
        pltpu.sync_copy(idx_hbm.at[wid], idx_v)

        def gather(chunk, buf):
            return pltpu.make_async_copy(table_hbm.at[idx_v.at[chunk]], rows_v.at[buf], sem)

        gather(0, 0).start()

        @pl.loop(0, chunks, step=2)
        def _(c):
            for b in range(2):
                chunk = c + b
                gather(chunk, b).wait()

                @pl.when(chunk + 1 < chunks)
                def _():
                    gather(chunk + 1, 1 - b).start()

                pltpu.sync_copy(rows_v.at[b], out_hbm.at[pl.ds(base + chunk * SC_ROWS, SC_ROWS)])

    return pl.kernel(
        body,
        out_type=jax.ShapeDtypeStruct((n, width), table.dtype),
        mesh=mesh,
        scratch_types=[pltpu.VMEM((chunks, SC_ROWS), jnp.int32),
                       pltpu.VMEM((2, SC_ROWS, width), table.dtype),
                       pltpu.SemaphoreType.DMA],
        name="sc_gather",
    )(table, idx.reshape(workers, chunks, SC_ROWS))


def _combine_dense_kernel(r0_ref, r1_ref, wts_ref, x1_ref, mod_ref, g_ref, b_ref, *rest):
    o_ref = rest[-1]
    w = wts_ref[...]
    y0 = jnp.concatenate(_unpack2(r0_ref[0]), axis=1)
    y1 = jnp.concatenate(_unpack2(r1_ref[0]), axis=1)
    y = w[:, 0:1] * y0 + w[:, 1:2] * y1
    gate2 = mod_ref[0, 5:6, :]
    o_ref[...] = _layer_norm(DN_ALPHA * x1_ref[...] + gate2 * y, g_ref[...], b_ref[...])


def _combine_dense(rows, wts, x1, mod, ln2_g, ln2_b, S, b, out):
    T, D = x1.shape
    tm = min(512, S)
    per_b = S // tm
    here = lambda i: (b * per_b + i, 0)
    in_specs = [pl.BlockSpec((1, tm, rows.shape[2]), lambda i: (0, i, 0)),
                pl.BlockSpec((1, tm, rows.shape[2]), lambda i: (1, i, 0)),
                pl.BlockSpec((tm, LANES), here),
                pl.BlockSpec((tm, D), here),
                pl.BlockSpec((1, 6, D), lambda i: (b, 0, 0)),
                pl.BlockSpec((1, D), lambda i: (0, 0)),
                pl.BlockSpec((1, D), lambda i: (0, 0))]
    args = [rows, rows, wts, x1, mod, ln2_g.reshape(1, D), ln2_b.reshape(1, D)]
    aliases = {}
    if out is not None:
        in_specs.append(pl.BlockSpec(memory_space=pl.ANY))
        aliases = {len(args): 0}
        args.append(out)
    return pl.pallas_call(
        _combine_dense_kernel,
        out_shape=jax.ShapeDtypeStruct((T, D), F32),
        grid=(per_b,),
        in_specs=in_specs,
        out_specs=pl.BlockSpec((tm, D), here),
        input_output_aliases=aliases,
        compiler_params=_params(1),
        name="combine",
    )(*args)


def _moe(h2, x1, route, wts, counts, mod, w1, w3, w2, ln2_g, ln2_b):
    B, S, D = x1.shape
    T = B * S
    P = 2 * T + N_EXPERTS * MOE_BLOCK
    sizes = counts[0, ROUTE_OFF:ROUTE_OFF + N_EXPERTS].astype(jnp.int32)
    n_blk = (sizes + MOE_BLOCK - 1) // MOE_BLOCK
    psizes = n_blk * MOE_BLOCK
    poffs = jnp.cumsum(psizes) - psizes
    sel = route[0:2, None, :] == jnp.arange(N_EXPERTS, dtype=jnp.int32)[None, :, None]
    dest = route[2:4] + jnp.sum(jnp.where(sel, poffs[None, :, None], 0), axis=1)
    xb = _sc_scatter_rows(h2.reshape(T, D // 2), dest[0], dest[1], P)
    yb = _experts(xb, poffs // MOE_BLOCK, n_blk, sizes, w1, w3, w2)
    out = None
    for b in range(B):
        slot_major = dest[:, b * S:(b + 1) * S].reshape(2 * S)
        rows = _sc_gather_rows(yb, slot_major).reshape(2, S, yb.shape[1])
        out = _combine_dense(rows, wts.reshape(T, LANES), x1.reshape(T, D), mod, ln2_g, ln2_b, S, b, out)
    return out.reshape(B, S, D)


def _layer(x, c, w_ada, b_ada, w_in, conv_w, conv_b, fw1, fb1, ff1, fw2, fb2, ff2, fw3, decay, skip,
           w_hy_o, w_attn_o, attn_sink, w_out, ln1_g, ln1_b, rg_w, rg_b, re_w, re_b, ew1, ew3, ew2,
           ln2_g, ln2_b):
    mod = _ada(c, w_ada, b_ada)
    q, kv, hv, hx1, hx2, ga, gh = _in_proj(x, mod, w_in, conv_w, conv_b)
    attn = _attention(q, kv, attn_sink)
    hy = _hyena(hv, hx1, hx2, fw1, fb1, ff1, fw2, fb2, ff2, fw3, decay, skip)
    x1, h2, route, wts, counts = _merge(attn, hy, ga, gh, x, mod, w_attn_o, w_hy_o, w_out,
                                        ln1_g, ln1_b, rg_w, rg_b, re_w, re_b)
    return _moe(h2, x1, route, wts, counts, mod, ew1, ew3, ew2, ln2_g, ln2_b)


def kernel(x, c, w_ada, b_ada, w_in, conv_w, conv_b, filt_w1, filt_b1, filt_freq1, filt_w2, filt_b2, filt_freq2, filt_w3, filt_decay, hy_skip, w_hy_o, w_attn_o, attn_sink, w_out, ln1_g, ln1_b, router_group_w, router_group_b, router_expert_w, router_expert_b, exp_w1, exp_w3, exp_w2, ln2_g, ln2_b):
    for l in range(w_ada.shape[0]):
        x = _layer(x, c, w_ada[l], b_ada[l], w_in[l], conv_w[l], conv_b[l], filt_w1[l], filt_b1[l],
                   filt_freq1[l], filt_w2[l], filt_b2[l], filt_freq2[l], filt_w3[l], filt_decay[l],
                   hy_skip[l], w_hy_o[l], w_attn_o[l], attn_sink[l], w_out[l], ln1_g[l], ln1_b[l],
                   router_group_w[l], router_group_b[l], router_expert_w[l], router_expert_b[l],
                   exp_w1[l], exp_w3[l], exp_w2[l], ln2_g[l], ln2_b[l])
    return x
```

```python
import functools
import math

import numpy as np
import jax
import jax.numpy as jnp
from jax import lax
from jax.experimental import pallas as pl
from jax.experimental.pallas import tpu as pltpu
from jax.experimental.pallas import tpu_sc as plsc

F32 = jnp.float32
BF16 = jnp.bfloat16

N_HEADS = 8
N_KV_HEADS = 2
HEAD_DIM = 64
ATTN_WIDTH = N_HEADS * HEAD_DIM
KV_WIDTH = N_KV_HEADS * HEAD_DIM
WINDOW = 128
HYENA_WIDTH = 512
FILTER_EMB = 33
FILTER_BANDS = (FILTER_EMB - 1) // 2
WINDOW_SHIFT = 0.05
N_GROUPS = 8
EXPERTS_PER_GROUP = 8
N_EXPERTS = N_GROUPS * EXPERTS_PER_GROUP
MOE_BLOCK = 1024
EXPERT_ROWS = 128
LN_EPS = 1e-5
DEPTH = 1
DN_ALPHA = (2.0 * DEPTH) ** 0.25
NEG = -1e30

LANES = 128
SUBLANES = 8
ROUTE_OFF = N_GROUPS
VMEM_LIMIT = 56 * 1024 * 1024


def _params(n_axes, vmem=VMEM_LIMIT):
    return pltpu.CompilerParams(dimension_semantics=("arbitrary",) * n_axes, vmem_limit_bytes=vmem)


def _split(a):
    hi = a.astype(BF16)
    lo = (a - hi.astype(F32)).astype(BF16)
    return hi, lo


def _dot3(a, b_hi, b_lo):
    a_hi, a_lo = _split(a)
    acc = jnp.dot(a_hi, b_hi, preferred_element_type=F32)
    acc = acc + jnp.dot(a_hi, b_lo, preferred_element_type=F32)
    acc = acc + jnp.dot(a_lo, b_hi, preferred_element_type=F32)
    return acc


def _pack2(a, b):
    ia = lax.bitcast_convert_type(a.astype(BF16).astype(F32), jnp.int32)
    ib = lax.bitcast_convert_type(b.astype(BF16).astype(F32), jnp.int32)
    return lax.bitcast_convert_type(ia | lax.shift_right_logical(ib, 16), F32)


def _unpack2(p):
    p = lax.bitcast_convert_type(p, jnp.int32)
    a = lax.bitcast_convert_type(p & jnp.int32(-65536), F32)
    b = lax.bitcast_convert_type(lax.shift_left(p, 16), F32)
    return a, b


def _sigmoid(x):
    return 0.5 * jnp.tanh(0.5 * x) + 0.5


def _layer_norm(r, g, b):
    mu = jnp.mean(r, axis=-1, keepdims=True)
    d = r - mu
    var = jnp.mean(d * d, axis=-1, keepdims=True)
    return d * lax.rsqrt(var + LN_EPS) * g + b


def _ada_kernel(c_ref, w_ref, b_ref, o_ref):
    c = c_ref[...]
    s = c * _sigmoid(c)
    wh, wl = _split(w_ref[...])
    o_ref[...] = _dot3(s, wh, wl) + b_ref[...]


def _ada(c, w_ada, b_ada):
    B, D = c.shape
    n_out = w_ada.shape[1]
    rows = SUBLANES
    cp = jnp.pad(c, ((0, rows - B), (0, 0)))
    tn = 1024
    out = pl.pallas_call(
        _ada_kernel,
        out_shape=jax.ShapeDtypeStruct((rows, n_out), F32),
        grid=(n_out // tn,),
        in_specs=[pl.BlockSpec((rows, D), lambda j: (0, 0)),
                  pl.BlockSpec((D, tn), lambda j: (0, j)),
                  pl.BlockSpec((1, tn), lambda j: (0, j))],
        out_specs=pl.BlockSpec((rows, tn), lambda j: (0, j)),
        compiler_params=_params(1),
        name="ada",
    )(cp, w_ada, b_ada.reshape(1, n_out))
    return out[:B].reshape(B, 6, D)


def _inproj_kernel(x_ref, xp_ref, xn_ref, mod_ref, w_ref, cw_ref, cb_ref,
                   q_ref, kv_ref, v_ref, x1_ref, x2_ref, ga_ref, gh_ref):
    i = pl.program_id(1)
    n = pl.num_programs(1)
    C = HYENA_WIDTH
    x = x_ref[0]
    tm, D = x.shape
    shift = mod_ref[0, 0:1, :]
    scale = mod_ref[0, 1:2, :]
    h = (x * (1.0 + scale) + shift).astype(BF16)

    def seg(lo, hi):
        return jnp.dot(h, w_ref[:, lo:hi], preferred_element_type=F32)

    o_q = 0
    o_kv = o_q + ATTN_WIDTH
    o_hy = o_kv + 2 * KV_WIDTH
    o_ga = o_hy + 3 * C
    o_gh = o_ga + D
    ga_ref[0] = _sigmoid(seg(o_ga, o_ga + D)).astype(BF16)
    gh_ref[0] = _sigmoid(seg(o_gh, o_gh + D)).astype(BF16)

    u = seg(o_hy, o_hy + 3 * C)
    xe = jnp.concatenate([xp_ref[0], xn_ref[0]], axis=0)
    he = (xe * (1.0 + scale) + shift).astype(BF16)
    ue = jnp.dot(he, w_ref[:, o_hy:o_hy + 3 * C], preferred_element_type=F32)
    prow = jnp.where(i > 0, ue[SUBLANES - 1:SUBLANES], 0.0)
    nrow = jnp.where(i < n - 1, ue[SUBLANES:SUBLANES + 1], 0.0)
    rid = lax.broadcasted_iota(jnp.int32, (tm, 1), 0)
    up = jnp.where(rid == 0, prow, pltpu.roll(u, 1, 0))
    dn = jnp.where(rid == tm - 1, nrow, pltpu.roll(u, tm - 1, 0))
    conv = cw_ref[0:1, :] * up + cw_ref[1:2, :] * u + cw_ref[2:3, :] * dn + cb_ref[...]
    v_ref[0] = conv[:, :C]
    x1_ref[0] = conv[:, C:2 * C]
    x2_ref[0] = conv[:, 2 * C:]

    q_ref[0] = (seg(o_q, o_q + ATTN_WIDTH) * (HEAD_DIM ** -0.5)).astype(BF16)
    kv_ref[0] = seg(o_kv, o_kv + 2 * KV_WIDTH).astype(BF16)


def _in_proj(x, mod, w_in, conv_w, conv_b):
    B, S, D = x.shape
    C = HYENA_WIDTH
    tm = min(1024, S)
    r8 = tm // SUBLANES
    nb8 = S // SUBLANES
    wb = w_in.astype(BF16)
    nw = wb.shape[1]
    row = lambda b, i: (b, i, 0)
    shapes = [(ATTN_WIDTH, BF16), (2 * KV_WIDTH, BF16), (C, F32), (C, F32), (C, F32), (D, BF16), (D, BF16)]
    return pl.pallas_call(
        _inproj_kernel,
        out_shape=[jax.ShapeDtypeStruct((B, S, w), dt) for w, dt in shapes],
        grid=(B, S // tm),
        in_specs=[pl.BlockSpec((1, tm, D), row),
                  pl.BlockSpec((1, SUBLANES, D), lambda b, i: (b, jnp.maximum(i * r8 - 1, 0), 0)),
                  pl.BlockSpec((1, SUBLANES, D), lambda b, i: (b, jnp.minimum((i + 1) * r8, nb8 - 1), 0)),
                  pl.BlockSpec((1, 6, D), lambda b, i: (b, 0, 0)),
                  pl.BlockSpec((D, nw), lambda b, i: (0, 0)),
                  pl.BlockSpec((3, 3 * C), lambda b, i: (0, 0)),
                  pl.BlockSpec((1, 3 * C), lambda b, i: (0, 0))],
        out_specs=[pl.BlockSpec((1, tm, w), row) for w, _ in shapes],
        compiler_params=_params(2),
        name="in_proj",
    )(x, x, x, mod, wb, conv_w.astype(F32), conv_b.reshape(1, 3 * C).astype(F32))


ATT_TQ = 512
ATT_QB = 128
ATT_STACK = 4


def _attn_kernel(sink_ref, q_ref, kvp_ref, kvc_ref, kvn_ref, bias_ref, o_ref, kv_scr, vx_scr, *, seq_len):
    i = pl.program_id(1)
    H = WINDOW
    TQ = q_ref.shape[1]
    Q = min(ATT_QB, TQ)
    band = Q + 2 * H
    G = N_HEADS // N_KV_HEADS
    kv_scr[0:H] = kvp_ref[0]
    kv_scr[H:H + TQ] = kvc_ref[0]
    kv_scr[H + TQ:] = kvn_ref[0]
    for kv in range(N_KV_HEADS):
        vx_scr[:, kv * LANES:kv * LANES + HEAD_DIM] = kv_scr[:, KV_WIDTH + kv * HEAD_DIM:KV_WIDTH + (kv + 1) * HEAD_DIM]
        vx_scr[:, kv * LANES + HEAD_DIM:(kv + 1) * LANES] = jnp.ones((TQ + 2 * H, LANES - HEAD_DIM), BF16)
    col = lax.broadcasted_iota(jnp.int32, (1, band), 1)
    rhead = lax.broadcasted_iota(jnp.int32, (ATT_STACK * Q, 1), 0) // Q
    for j in range(TQ // Q):
        kpos = i * TQ + j * Q - H + col
        colbias = jnp.where((kpos >= 0) & (kpos < seq_len), 0.0, NEG)
        for kv in range(N_KV_HEADS):
            kk = kv_scr[j * Q:j * Q + band, kv * HEAD_DIM:(kv + 1) * HEAD_DIM]
            vx = vx_scr[j * Q:j * Q + band, kv * LANES:(kv + 1) * LANES]
            for sub in range(G // ATT_STACK):
                first = sub * ATT_STACK
                heads = [kv * G + first + g for g in range(ATT_STACK)]
                qg = jnp.concatenate([q_ref[0, j * Q:(j + 1) * Q, h * HEAD_DIM:(h + 1) * HEAD_DIM] for h in heads],
                                     axis=0)
                s = lax.dot_general(qg, kk, (((1,), (1,)), ((), ())), preferred_element_type=F32)
                s = s + bias_ref[kv, first * Q:(first + ATT_STACK) * Q, :] + colbias
                snk = sink_ref[heads[-1]]
                for g in range(ATT_STACK - 2, -1, -1):
                    snk = jnp.where(rhead == g, sink_ref[heads[g]], snk)
                m = jnp.maximum(jnp.max(s, axis=1, keepdims=True), snk)
                p = jnp.exp(s - m).astype(BF16)
                ox = jnp.dot(p, vx, preferred_element_type=F32)
                den = ox[:, HEAD_DIM:HEAD_DIM + 1] + jnp.exp(snk - m)
                o = ox[:, :HEAD_DIM] / den
                for g, h in enumerate(heads):
                    o_ref[0, j * Q:(j + 1) * Q, h * HEAD_DIM:(h + 1) * HEAD_DIM] = o[g * Q:(g + 1) * Q].astype(BF16)


def _attention(q, kv, sink):
    B, S, _ = q.shape
    H = WINDOW
    TQ = min(ATT_TQ, S)
    Q = min(ATT_QB, TQ)
    r = TQ // H
    nq = S // H
    G = N_HEADS // N_KV_HEADS
    assert G % ATT_STACK == 0
    a = jnp.arange(Q)[:, None]
    j = jnp.arange(Q + 2 * H)[None, :]
    rel = jnp.abs(j - H - a).astype(F32)
    slopes = 2.0 ** (-8.0 * jnp.arange(1, N_HEADS + 1, dtype=F32) / N_HEADS)
    bias = jnp.where(rel[None] <= WINDOW, -slopes[:, None, None] * rel[None], NEG).astype(F32)
    bias = bias.reshape(N_KV_HEADS, G * Q, Q + 2 * H)
    cur = lambda b, i: (b, i, 0)
    return pl.pallas_call(
        functools.partial(_attn_kernel, seq_len=S),
        out_shape=jax.ShapeDtypeStruct((B, S, ATTN_WIDTH), BF16),
        grid=(B, S // TQ),
        in_specs=[pl.BlockSpec(memory_space=pltpu.SMEM),
                  pl.BlockSpec((1, TQ, ATTN_WIDTH), cur),
                  pl.BlockSpec((1, H, 2 * KV_WIDTH), lambda b, i: (b, jnp.maximum(i * r - 1, 0), 0)),
                  pl.BlockSpec((1, TQ, 2 * KV_WIDTH), cur),
                  pl.BlockSpec((1, H, 2 * KV_WIDTH), lambda b, i: (b, jnp.minimum((i + 1) * r, nq - 1), 0)),
                  pl.BlockSpec((N_KV_HEADS, G * Q, Q + 2 * H), lambda b, i: (0, 0, 0))],
        out_specs=pl.BlockSpec((1, TQ, ATTN_WIDTH), cur),
        scratch_shapes=[pltpu.VMEM((TQ + 2 * H, 2 * KV_WIDTH), BF16),
                        pltpu.VMEM((TQ + 2 * H, N_KV_HEADS * LANES), BF16)],
        compiler_params=_params(2),
        name="attn",
    )(sink.astype(F32), q, kv, kv, kv, bias)


def _filter_kernel(z_ref, w1h, w1l, b1_ref, f1_ref, w2h, w2l, b2_ref, f2_ref, w3h, w3l, dec_ref, wf_ref,
                   a_ref, s_ref, k0_ref, k_scr, *, n1):
    i = pl.program_id(0)
    C = HYENA_WIDTH
    hh, ss, _ = z_ref.shape
    z = z_ref[...].reshape(hh * ss, z_ref.shape[2])
    h1 = jnp.sin(f1_ref[...] * (_dot3(z, w1h[...], w1l[...]) + b1_ref[...]))
    h2 = jnp.sin(f2_ref[...] * (_dot3(h1, w2h[...], w2l[...]) + b2_ref[...]))
    k = _dot3(h2, w3h[...], w3l[...])
    t = z[:, 0:1]
    k = k * (jnp.exp(-t * jnp.abs(dec_ref[...])) + WINDOW_SHIFT)
    k_scr[...] = k.reshape(hh, ss, k.shape[1])

    @pl.when(i == 0)
    def _():
        s_ref[...] = jnp.zeros_like(s_ref)
        k0_ref[...] = k[0:1, :]

    s_ref[...] += jnp.sum(jnp.abs(k), axis=0, keepdims=True)

    w = wf_ref[...]
    for o in range(a_ref.shape[0]):
        for j in range(ss):
            base = o * 2 * C
            rhs = jnp.concatenate([k_scr[:, j, base:base + C], k_scr[:, j, base + C:base + 2 * C]], axis=0)
            res = jnp.dot(w, rhs.astype(BF16), preferred_element_type=F32)
            a_ref[o, :, 0, j, :] = _pack2(res[:n1], res[n1:2 * n1])
            a_ref[o, :, 1, j, :] = _pack2(res[2 * n1:3 * n1], res[3 * n1:])


def _filter_embedding(L):
    t = np.linspace(0.0, 1.0, L, dtype=np.float32).astype(np.float64)[:, None]
    w = (2.0 * math.pi * np.arange(L, dtype=np.float32) / np.float32(L)).astype(np.float64)[:, None]
    bands = np.linspace(1e-4, FILTER_BANDS - 1, FILTER_BANDS, dtype=np.float32).astype(np.float64)[None, :]
    bw = (bands.astype(np.float32) * w.astype(np.float32)).astype(np.float64)
    z = np.concatenate([t, np.cos(bw), -np.sin(bw)], axis=-1)
    zp = np.zeros((L, LANES), np.float32)
    zp[:, :FILTER_EMB] = z.astype(np.float32)
    return jnp.asarray(zp)


def _pad2(a, r, c):
    return jnp.zeros((r, c), F32).at[:a.shape[0], :a.shape[1]].set(a.astype(F32))


FILT_S2 = 8


def _filters(L, fw1, fb1, ff1, fw2, fb2, ff2, fw3, decay, consts):
    H = LANES
    C = HYENA_WIDTH
    nf = fw3.shape[1]
    n_ord = nf // (2 * C)
    n1 = consts["n1"]
    hh = n1 // 2
    z = _filter_embedding(L).reshape(hh, LANES, H)
    w1h, w1l = _split(_pad2(fw1, H, H))
    w2h, w2l = _split(_pad2(fw2, H, H))
    w3h, w3l = _split(_pad2(fw3, H, nf))
    b1 = _pad2(fb1[None], 1, H)
    f1 = _pad2(ff1[None], 1, H)
    b2 = _pad2(fb2[None], 1, H)
    f2 = _pad2(ff2[None], 1, H)
    full = lambda r, c: pl.BlockSpec((r, c), lambda i: (0, 0))
    return pl.pallas_call(
        functools.partial(_filter_kernel, n1=n1),
        out_shape=[jax.ShapeDtypeStruct((n_ord, n1, 2, LANES, C), F32), jax.ShapeDtypeStruct((1, nf), F32),
                   jax.ShapeDtypeStruct((1, nf), F32)],
        grid=(LANES // FILT_S2,),
        in_specs=[pl.BlockSpec((hh, FILT_S2, H), lambda i: (0, i, 0)),
                  full(H, H), full(H, H), full(1, H), full(1, H),
                  full(H, H), full(H, H), full(1, H), full(1, H),
                  full(H, nf), full(H, nf), full(1, nf), full(4 * n1, n1)],
        out_specs=[pl.BlockSpec((n_ord, n1, 2, FILT_S2, C), lambda i: (0, 0, 0, i, 0)), full(1, nf), full(1, nf)],
        scratch_shapes=[pltpu.VMEM((hh, FILT_S2, nf), F32)],
        compiler_params=_params(1),
        name="filter",
    )(z, w1h, w1l, b1, f1, w2h, w2l, b2, f2, w3h, w3l, decay.reshape(1, nf).astype(F32), consts["w1_filt"])


def _np_bf16(m64):
    return jnp.asarray(m64.astype(np.float32).astype(BF16))


def _dft_constants(L):
    N = 2 * L
    n2 = LANES
    n1 = N // n2
    h1 = n1 // 2
    k1 = np.arange(n1)[:, None]
    s1 = np.arange(h1)[None, :]
    ang = -2.0 * np.pi * ((k1 * s1) % n1) / n1
    wr, wi = np.cos(ang), np.sin(ang)
    w1_filt = np.block([[wr, wr], [wi, wi], [wr, -wr], [wi, -wi]])
    w1_cplx = np.block([[wr, -wi], [wi, wr]])
    vr, vi = wr.T / N, -wi.T / N
    w3 = np.block([[vr, -vi], [vi, vr]])
    k2 = np.arange(n2)[:, None]
    s2 = np.arange(n2)[None, :]
    a2 = -2.0 * np.pi * ((k2 * s2) % n2) / n2
    w2r, w2i = jnp.asarray(np.cos(a2), F32), jnp.asarray(np.sin(a2), F32)
    at = -2.0 * np.pi * ((np.arange(n1)[:, None] * s2) % N) / N
    twr, twi = jnp.asarray(np.cos(at), F32), jnp.asarray(np.sin(at), F32)
    mr = w2r[None] * twr[:, None, :] - w2i[None] * twi[:, None, :]
    mi = w2r[None] * twi[:, None, :] + w2i[None] * twr[:, None, :]
    fwd = jnp.concatenate([jnp.concatenate([mr, -mi], axis=2),
                           jnp.concatenate([mi, mr], axis=2)], axis=1)
    fwd = fwd.astype(BF16)
    return dict(n1=n1, w1_filt=_np_bf16(w1_filt), w1_cplx=_np_bf16(w1_cplx), w3=_np_bf16(w3),
                fwd=fwd)


SCH = 16


def _dft1_kernel(x_ref, w_ref, a_ref, *, n1):
    w = w_ref[...]
    for j in range(SCH):
        rhs = jnp.concatenate([x_ref[0, 0, :, j, :], x_ref[0, 1, :, j, :]], axis=0)
        res = jnp.dot(w, rhs.astype(BF16), preferred_element_type=F32)
        a_ref[0, :, j, :] = _pack2(res[:n1], res[n1:])


def _dft1_data(x, consts):
    B, L, C = x.shape
    n1 = consts["n1"]
    h1 = n1 // 2
    xv = x.reshape(B // 2, 2, h1, LANES, C)
    return pl.pallas_call(
        functools.partial(_dft1_kernel, n1=n1),
        out_shape=jax.ShapeDtypeStruct((B // 2, n1, LANES, C), F32),
        grid=(B // 2, LANES // SCH),
        in_specs=[pl.BlockSpec((1, 2, h1, SCH, C), lambda p, j: (p, 0, 0, j, 0)),
                  pl.BlockSpec((2 * n1, n1), lambda p, j: (0, 0))],
        out_specs=pl.BlockSpec((1, n1, SCH, C), lambda p, j: (p, 0, j, 0)),
        compiler_params=_params(2),
        name="dft1",
    )(xv, consts["w1_cplx"])


KCH = 16


def _mid_kernel(a_ref, af_ref, f_ref, inv_ref, b0_ref, b_ref, hr_scr, hi_scr):
    n2 = LANES

    @pl.when(pl.program_id(1) == 0)
    def _():
        sc = inv_ref[0]
        for k in range(KCH):
            p = jnp.concatenate(_unpack2(af_ref[0, k, :n2, :]), axis=0).astype(BF16)
            q = jnp.concatenate(_unpack2(af_ref[0, k, n2:, :]), axis=0).astype(BF16)
            hr_scr[k] = (jnp.dot(f_ref[k, :n2, :], p, preferred_element_type=F32) - b0_ref[0]) * sc
            hi_scr[k] = jnp.dot(f_ref[k, n2:, :], q, preferred_element_type=F32) * sc

    for k in range(KCH):
        a = jnp.concatenate(_unpack2(a_ref[0, k]), axis=0).astype(BF16)
        x = jnp.dot(f_ref[k], a, preferred_element_type=F32)
        xr, xi = x[:n2], x[n2:]
        hr, hi = hr_scr[k], hi_scr[k]
        y = jnp.concatenate([xr * hr - xi * hi, xr * hi + xi * hr], axis=0)
        b = lax.dot_general(f_ref[k], y.astype(BF16), (((0,), (0,)), ((), ())), preferred_element_type=F32)
        b_ref[0, k] = _pack2(b[:n2], b[n2:])


def _mid(a, af, inv_den, bwd0, order, consts):
    P, n1, n2, C = a.shape
    afl = af.reshape(af.shape[0], n1, 2 * n2, C)
    tab = pl.BlockSpec((KCH, 2 * n2, 2 * n2), lambda k, p: (k, 0, 0))
    vec = pl.BlockSpec((1, 1, C), lambda k, p: (order, 0, 0))
    return pl.pallas_call(
        _mid_kernel,
        out_shape=jax.ShapeDtypeStruct((P, n1, n2, C), F32),
        grid=(n1 // KCH, P),
        in_specs=[pl.BlockSpec((1, KCH, n2, C), lambda k, p: (p, k, 0, 0)),
                  pl.BlockSpec((1, KCH, 2 * n2, C), lambda k, p: (order, k, 0, 0)),
                  tab, vec, vec],
        out_specs=pl.BlockSpec((1, KCH, n2, C), lambda k, p: (p, k, 0, 0)),
        scratch_shapes=[pltpu.VMEM((KCH, n2, C), F32), pltpu.VMEM((KCH, n2, C), F32)],
        compiler_params=_params(2),
        name="mid",
    )(a, afl, consts["fwd"], inv_den, bwd0)


def _dft3_kernel(b_ref, w_ref, v_ref, g_ref, skip_ref, *rest, h1, chain):
    if chain:
        w1_ref, z_ref, a_ref, slab_ref = rest
        w1 = w1_ref[...]
    else:
        z_ref, slab_ref = rest
    w = w_ref[...]
    skip = skip_ref[0]
    n1 = 2 * h1
    for j in range(SCH):
        slab_ref[...] = b_ref[0, :, j, :]
        rhs = jnp.concatenate(_unpack2(slab_ref[...]), axis=0)
        y = jnp.dot(w, rhs.astype(BF16), preferred_element_type=F32)
        z = [g_ref[0, r, :, j, :] * (y[r * h1:(r + 1) * h1] + v_ref[0, r, :, j, :] * skip) for r in range(2)]
        for r in range(2):
            z_ref[0, r, :, j, :] = z[r]
        if chain:
            res = jnp.dot(w1, jnp.concatenate(z, axis=0).astype(BF16), preferred_element_type=F32)
            a_ref[0, :, j, :] = _pack2(res[:n1], res[n1:])


def _dft3_gate(b5, v, gate, skip, consts, chain):
    P, n1, n2, C = b5.shape
    h1 = n1 // 2
    B, L, _ = v.shape
    five = lambda t: t.reshape(P, 2, h1, n2, C)
    dat = pl.BlockSpec((1, 2, h1, SCH, C), lambda p, j: (p, 0, 0, j, 0))
    packed = pl.BlockSpec((1, n1, SCH, C), lambda p, j: (p, 0, j, 0))
    in_specs = [packed, pl.BlockSpec((n1, 2 * n1), lambda p, j: (0, 0)), dat, dat,
                pl.BlockSpec((1, C), lambda p, j: (0, 0))]
    args = [b5, consts["w3"], five(v), five(gate), skip.reshape(1, C).astype(F32)]
    out_shape = [jax.ShapeDtypeStruct((P, 2, h1, n2, C), F32)]
    out_specs = [dat]
    if chain:
        in_specs.append(pl.BlockSpec((2 * n1, n1), lambda p, j: (0, 0)))
        args.append(consts["w1_cplx"])
        out_shape.append(jax.ShapeDtypeStruct((P, n1, n2, C), F32))
        out_specs.append(packed)
    outs = pl.pallas_call(
        functools.partial(_dft3_kernel, h1=h1, chain=chain),
        out_shape=out_shape,
        grid=(P, n2 // SCH),
        in_specs=in_specs,
        out_specs=out_specs,
        scratch_shapes=[pltpu.VMEM((n1, C), F32)],
        compiler_params=_params(2),
        name="dft3",
    )(*args)
    z = outs[0].reshape(B, L, C)
    return (z, outs[1]) if chain else (z, None)


def _hyena(v, x1, x2, fw1, fb1, ff1, fw2, fb2, ff2, fw3, decay, skip):
    B, L, C = v.shape
    consts = _dft_constants(L)
    af, ksum, k0 = _filters(L, fw1, fb1, ff1, fw2, fb2, ff2, fw3, decay, consts)
    ks = ksum.reshape(2, 2, C)
    inv_den = (1.0 / (ks[:, 0] + ks[:, 1])).reshape(2, 1, C)
    bwd0 = k0.reshape(2, 2, C)[:, 1].reshape(2, 1, C)
    gates = (x1, x2)
    z, a5 = v, _dft1_data(v, consts)
    for o, gate in enumerate(gates):
        b5 = _mid(a5, af, inv_den, bwd0, o, consts)
        z, a5 = _dft3_gate(b5, z, gate, skip[o], consts, chain=o + 1 < len(gates))
    return z


def _merge_kernel(attn_ref, hy_ref, ga_ref, gh_ref, x_ref, mod_ref, wa_ref, wh_ref, wo_ref,
                  g1_ref, b1_ref, rwh_ref, rwl_ref, rb_ref, tri_ref,
                  x1_ref, h2_ref, route_ref, wts_ref, cnt_ref, carry_ref):
    @pl.when((pl.program_id(0) == 0) & (pl.program_id(1) == 0))
    def _():
        carry_ref[...] = jnp.zeros_like(carry_ref)

    logits = _merge_dense(attn_ref, hy_ref, ga_ref, gh_ref, x_ref, mod_ref, wa_ref, wh_ref, wo_ref,
                          g1_ref, b1_ref, rwh_ref, rwl_ref, rb_ref, x1_ref, h2_ref)
    _route_rows(logits, tri_ref, route_ref, wts_ref, carry_ref)
    cnt_ref[...] = carry_ref[...]


def _merge_dense(attn_ref, hy_ref, ga_ref, gh_ref, x_ref, mod_ref, wa_ref, wh_ref, wo_ref,
                 g1_ref, b1_ref, rwh_ref, rwl_ref, rb_ref, x1_ref, h2_ref):
    a = jnp.dot(attn_ref[0], wa_ref[...], preferred_element_type=F32)
    hy = jnp.dot(hy_ref[0].astype(BF16), wh_ref[...], preferred_element_type=F32)
    merged = ga_ref[0].astype(F32) * a + gh_ref[0].astype(F32) * hy
    y = jnp.dot(merged.astype(BF16), wo_ref[...], preferred_element_type=F32)
    gate1 = mod_ref[0, 2:3, :]
    shift2 = mod_ref[0, 3:4, :]
    scale2 = mod_ref[0, 4:5, :]
    x1 = _layer_norm(DN_ALPHA * x_ref[0] + gate1 * y, g1_ref[...], b1_ref[...])
    x1_ref[0] = x1
    h2 = x1 * (1.0 + scale2) + shift2
    half = h2.shape[1] // 2
    h2_ref[0] = _pack2(h2[:, :half], h2[:, half:])
    return _dot3(h2, rwh_ref[...], rwl_ref[...]) + rb_ref[...]


def _route_rows(logits, tri_ref, route_ref, wts_ref, carry_ref):
    tm = logits.shape[0]
    lane = lax.broadcasted_iota(jnp.int32, (tm, LANES), 1)
    lanef = lane.astype(F32)
    big = float(LANES)

    def first_lane(mask):
        return jnp.min(jnp.where(mask, lanef, big), axis=1, keepdims=True).astype(jnp.int32)

    gmask = lane < N_GROUPS
    gl = jnp.where(gmask, logits, NEG)
    gmax = jnp.max(gl, axis=1, keepdims=True)
    gidx = first_lane(gl == gmax)
    pg = 1.0 / jnp.sum(jnp.exp(gl - gmax), axis=1, keepdims=True)
    lo = ROUTE_OFF + gidx * EXPERTS_PER_GROUP
    emask = (lane >= lo) & (lane < lo + EXPERTS_PER_GROUP)
    el = jnp.where(emask, logits, NEG)
    v1 = jnp.max(el, axis=1, keepdims=True)
    i1 = first_lane(el == v1)
    el2 = jnp.where(emask & (lane != i1), logits, NEG)
    v2 = jnp.max(el2, axis=1, keepdims=True)
    i2 = first_lane(el2 == v2)
    e21 = jnp.exp(v2 - v1)
    w1 = pg / (1.0 + e21)
    w2 = pg * e21 / (1.0 + e21)

    sel1 = lane == i1
    sel2 = lane == i2
    onehot = jnp.where(sel1 | sel2, 1.0, 0.0)
    prefix = jnp.dot(tri_ref[...], onehot.astype(BF16), preferred_element_type=F32) + carry_ref[...]
    r1 = jnp.sum(jnp.where(sel1, prefix, 0.0), axis=1, keepdims=True)
    r2 = jnp.sum(jnp.where(sel2, prefix, 0.0), axis=1, keepdims=True)
    carry_ref[...] += jnp.sum(onehot, axis=0, keepdims=True)

    e1 = (i1 - ROUTE_OFF).astype(F32)
    e2 = (i2 - ROUTE_OFF).astype(F32)
    table = jnp.where(lane == 0, e1, jnp.where(lane == 1, e2, jnp.where(lane == 2, r1, jnp.where(lane == 3, r2, 0.0))))
    route_ref[...] = table.T[:SUBLANES].astype(jnp.int32)
    wts_ref[0] = jnp.where(lane == 0, w1, jnp.where(lane == 1, w2, 0.0))


def _merge(attn, hy, ga, gh, x, mod, w_attn_o, w_hy_o, w_out, ln1_g, ln1_b, rg_w, rg_b, re_w, re_b):
    B, S, D = x.shape
    tm = min(512, S)
    spare = LANES - N_GROUPS - N_EXPERTS
    rw = jnp.concatenate([rg_w, re_w, jnp.zeros((D, spare), F32)], axis=1)
    rb = jnp.concatenate([rg_b, re_b, jnp.zeros((spare,), F32)]).reshape(1, LANES)
    rwh, rwl = _split(rw)
    tri = (jnp.arange(tm)[:, None] > jnp.arange(tm)[None, :]).astype(BF16)
    row = lambda b, i: (b, i, 0)
    full = lambda r, c: pl.BlockSpec((r, c), lambda b, i: (0, 0))
    per_b = S // tm
    outs = [jax.ShapeDtypeStruct((B, S, D), F32), jax.ShapeDtypeStruct((B, S, D // 2), F32),
            jax.ShapeDtypeStruct((SUBLANES, B * S), jnp.int32), jax.ShapeDtypeStruct((B, S, LANES), F32),
            jax.ShapeDtypeStruct((1, LANES), F32)]
    return pl.pallas_call(
        _merge_kernel,
        out_shape=outs,
        grid=(B, per_b),
        in_specs=[pl.BlockSpec((1, tm, ATTN_WIDTH), row), pl.BlockSpec((1, tm, HYENA_WIDTH), row),
                  pl.BlockSpec((1, tm, D), row), pl.BlockSpec((1, tm, D), row), pl.BlockSpec((1, tm, D), row),
                  pl.BlockSpec((1, 6, D), lambda b, i: (b, 0, 0)),
                  full(ATTN_WIDTH, D), full(HYENA_WIDTH, D), full(D, D),
                  full(1, D), full(1, D), full(D, LANES), full(D, LANES), full(1, LANES), full(tm, tm)],
        out_specs=[pl.BlockSpec((1, tm, D), row), pl.BlockSpec((1, tm, D // 2), row),
                   pl.BlockSpec((SUBLANES, tm), lambda b, i: (0, b * per_b + i)),
                   pl.BlockSpec((1, tm, LANES), row), full(1, LANES)],
        scratch_shapes=[pltpu.VMEM((1, LANES), F32)],
        compiler_params=_params(2),
        name="merge",
    )(attn, hy, ga, gh, x, mod, w_attn_o.astype(BF16), w_hy_o.astype(BF16), w_out.astype(BF16),
      ln1_g.reshape(1, D), ln1_b.reshape(1, D), rwh, rwl, rb, tri)


SC_ROWS = 64


def _sc_workers():
    info = plsc.get_sparse_core_info()
    return info.num_cores, info.num_cores * info.num_subcores


def _sc_split(n):
    _, workers = _sc_workers()
    per_worker = n // workers
    chunks = per_worker // SC_ROWS
    assert per_worker * workers == n and chunks * SC_ROWS == per_worker and chunks % 2 == 0
    return workers, per_worker, chunks


def _sc_scatter_rows(src, idx0, idx1, n_out):
    n, width = src.shape
    nc, _ = _sc_workers()
    workers, per_worker, chunks = _sc_split(n)
    mesh = plsc.VectorSubcoreMesh(core_axis_name="c", subcore_axis_name="s")

    def body(src_hbm, i0_hbm, i1_hbm, out_hbm, i0_v, i1_v, rows_v, sem, ssem):
        wid = lax.axis_index("s") * nc + lax.axis_index("c")
        base = wid * per_worker
        pltpu.sync_copy(i0_hbm.at[wid], i0_v)
        pltpu.sync_copy(i1_hbm.at[wid], i1_v)

        def load(chunk, buf):
            return pltpu.make_async_copy(src_hbm.at[pl.ds(base + chunk * SC_ROWS, SC_ROWS)], rows_v.at[buf], sem)

        load(0, 0).start()

        @pl.loop(0, chunks, step=2)
        def _(c):
            for b in range(2):
                chunk = c + b
                load(chunk, b).wait()

                @pl.when(chunk + 1 < chunks)
                def _():
                    load(chunk + 1, 1 - b).start()

                first = pltpu.make_async_copy(rows_v.at[b], out_hbm.at[i0_v.at[chunk]], ssem)
                second = pltpu.make_async_copy(rows_v.at[b], out_hbm.at[i1_v.at[chunk]], ssem)
                first.start()
                second.start()
                first.wait()
                second.wait()

    shaped = lambda i: i.reshape(workers, chunks, SC_ROWS)
    return pl.kernel(
        body,
        out_type=jax.ShapeDtypeStruct((n_out, width), src.dtype),
        mesh=mesh,
        scratch_types=[pltpu.VMEM((chunks, SC_ROWS), jnp.int32),
                       pltpu.VMEM((chunks, SC_ROWS), jnp.int32),
                       pltpu.VMEM((2, SC_ROWS, width), src.dtype),
                       pltpu.SemaphoreType.DMA, pltpu.SemaphoreType.DMA],
        name="sc_scatter",
    )(src, shaped(idx0), shaped(idx1))


def _expert_kernel(first_ref, nb_ref, sz_ref, tot_ref, w1_ref, w3_ref, w2_ref, xb_ref, yb_ref,
                   xbuf, ybuf, c1_ref, c3_ref, c2_ref, lsem, ssem):
    e = pl.program_id(0)
    nb = nb_ref[e]
    first = first_ref[e]
    total = tot_ref[0]
    rows = xbuf.shape[1]

    def load(g, slot):
        src = xb_ref.at[pl.ds(pl.multiple_of(g * rows, rows), rows)]
        return pltpu.make_async_copy(src, xbuf.at[slot], lsem.at[slot])

    def store(g, slot):
        dst = yb_ref.at[pl.ds(pl.multiple_of(g * rows, rows), rows)]
        return pltpu.make_async_copy(ybuf.at[slot], dst, ssem.at[slot])

    @pl.when((e == 0) & (total > 0))
    def _():
        load(0, 0).start()

    @pl.when(nb > 0)
    def _():
        c1_ref[...] = w1_ref[0].astype(BF16)
        c3_ref[...] = w3_ref[0].astype(BF16)
        c2_ref[...] = w2_ref[0].astype(BF16)

        def block(j, carry):
            g = first + j
            slot = lax.rem(g, 2)
            load(g, slot).wait()

            @pl.when(g + 1 < total)
            def _():
                load(g + 1, 1 - slot).start()

            @pl.when(g >= 2)
            def _():
                store(g - 2, slot).wait()

            n_valid = sz_ref[e] - j * rows
            pieces = jnp.minimum((n_valid + EXPERT_ROWS - 1) // EXPERT_ROWS, rows // EXPERT_ROWS)

            def swiglu(n):
                rid = lax.broadcasted_iota(jnp.int32, (n, 1), 0)
                xa, xb = _unpack2(jnp.where(rid < n_valid, xbuf[slot, :n, :], 0.0))
                x = jnp.concatenate([xa, xb], axis=1).astype(BF16)
                a = jnp.dot(x, c1_ref[...], preferred_element_type=F32)
                gate = jnp.dot(x, c3_ref[...], preferred_element_type=F32)
                hmid = (a * _sigmoid(a) * gate).astype(BF16)
                y = jnp.dot(hmid, c2_ref[...], preferred_element_type=F32)
                half = y.shape[1] // 2
                ybuf[slot, :n, :] = _pack2(y[:, :half], y[:, half:])

            for q in range(1, rows // EXPERT_ROWS + 1):
                pl.when(pieces == q)(functools.partial(swiglu, q * EXPERT_ROWS))
            store(g, slot).start()
            return carry

        lax.fori_loop(0, nb, block, 0)

    @pl.when(e == pl.num_programs(0) - 1)
    def _():
        for back in (2, 1):
            @pl.when(total >= back)
            def _():
                g = total - back
                store(g, lax.rem(g, 2)).wait()


def _experts(xb, first_blk, n_blk, sizes, w1, w3, w2):
    P, W = xb.shape
    E, D, DE = w1.shape
    total = jnp.sum(n_blk, keepdims=True)
    wspec = lambda r, c: pl.BlockSpec((1, r, c), lambda e, *_: (e, 0, 0))
    grid_spec = pltpu.PrefetchScalarGridSpec(
        num_scalar_prefetch=4,
        grid=(E,),
        in_specs=[wspec(D, DE), wspec(D, DE), wspec(DE, D), pl.BlockSpec(memory_space=pl.ANY)],
        out_specs=pl.BlockSpec(memory_space=pl.ANY),
        scratch_shapes=[pltpu.VMEM((2, MOE_BLOCK, W), F32), pltpu.VMEM((2, MOE_BLOCK, W), F32),
                        pltpu.VMEM((D, DE), BF16), pltpu.VMEM((D, DE), BF16), pltpu.VMEM((DE, D), BF16),
                        pltpu.SemaphoreType.DMA((2,)), pltpu.SemaphoreType.DMA((2,))],
    )
    return pl.pallas_call(
        _expert_kernel,
        out_shape=jax.ShapeDtypeStruct((P, W), F32),
        grid_spec=grid_spec,
        compiler_params=_params(1),
        name="experts",
    )(first_blk, n_blk, sizes, total, w1, w3, w2, xb)


def _sc_gather_rows(table, idx):
    n, width = idx.shape[0], table.shape[1]
    nc, _ = _sc_workers()
    workers, per_worker, chunks = _sc_split(n)
    mesh = plsc.VectorSubcoreMesh(core_axis_name="c", subcore_axis_name="s")

    def body(table_hbm, idx_hbm, out_hbm, idx_v, rows_v, sem):
        wid = lax.axis_index("s") * nc + lax.axis_index("c")
        base = wid * per_worker
        pltpu.sync_copy(idx_hbm.at[wid], idx_v)

        def gather(chunk, buf):
            return pltpu.make_async_copy(table_hbm.at[idx_v.at[chunk]], rows_v.at[buf], sem)

        gather(0, 0).start()

        @pl.loop(0, chunks, step=2)
        def _(c):
            for b in range(2):
                chunk = c + b
                gather(chunk, b).wait()

                @pl.when(chunk + 1 < chunks)
                def _():
                    gather(chunk + 1, 1 - b).start()

                pltpu.sync_copy(rows_v.at[b], out_hbm.at[pl.ds(base + chunk * SC_ROWS, SC_ROWS)])

    return pl.kernel(
        body,
        out_type=jax.ShapeDtypeStruct((n, width), table.dtype),
        mesh=mesh,
        scratch_types=[pltpu.VMEM((chunks, SC_ROWS), jnp.int32),
                       pltpu.VMEM((2, SC_ROWS, width), table.dtype),
                       pltpu.SemaphoreType.DMA],
        name="sc_gather",
    )(table, idx.reshape(workers, chunks, SC_ROWS))


def _combine_dense_kernel(r0_ref, r1_ref, wts_ref, x1_ref, mod_ref, g_ref, b_ref, *rest):
    o_ref = rest[-1]
    w = wts_ref[...]
    y0 = jnp.concatenate(_unpack2(r0_ref[0]), axis=1)
    y1 = jnp.concatenate(_unpack2(r1_ref[0]), axis=1)
    y = w[:, 0:1] * y0 + w[:, 1:2] * y1
    gate2 = mod_ref[0, 5:6, :]
    o_ref[...] = _layer_norm(DN_ALPHA * x1_ref[...] + gate2 * y, g_ref[...], b_ref[...])


def _combine_dense(rows, wts, x1, mod, ln2_g, ln2_b, S, b, out):
    T, D = x1.shape
    tm = min(512, S)
    per_b = S // tm
    here = lambda i: (b * per_b + i, 0)
    in_specs = [pl.BlockSpec((1, tm, rows.shape[2]), lambda i: (0, i, 0)),
                pl.BlockSpec((1, tm, rows.shape[2]), lambda i: (1, i, 0)),
                pl.BlockSpec((tm, LANES), here),
                pl.BlockSpec((tm, D), here),
                pl.BlockSpec((1, 6, D), lambda i: (b, 0, 0)),
                pl.BlockSpec((1, D), lambda i: (0, 0)),
                pl.BlockSpec((1, D), lambda i: (0, 0))]
    args = [rows, rows, wts, x1, mod, ln2_g.reshape(1, D), ln2_b.reshape(1, D)]
    aliases = {}
    if out is not None:
        in_specs.append(pl.BlockSpec(memory_space=pl.ANY))
        aliases = {len(args): 0}
        args.append(out)
    return pl.pallas_call(
        _combine_dense_kernel,
        out_shape=jax.ShapeDtypeStruct((T, D), F32),
        grid=(per_b,),
        in_specs=in_specs,
        out_specs=pl.BlockSpec((tm, D), here),
        input_output_aliases=aliases,
        compiler_params=_params(1),
        name="combine",
    )(*args)


def _moe(h2, x1, route, wts, counts, mod, w1, w3, w2, ln2_g, ln2_b):
    B, S, D = x1.shape
    T = B * S
    P = 2 * T + N_EXPERTS * MOE_BLOCK
    sizes = counts[0, ROUTE_OFF:ROUTE_OFF + N_EXPERTS].astype(jnp.int32)
    n_blk = (sizes + MOE_BLOCK - 1) // MOE_BLOCK
    psizes = n_blk * MOE_BLOCK
    poffs = jnp.cumsum(psizes) - psizes
    sel = route[0:2, None, :] == jnp.arange(N_EXPERTS, dtype=jnp.int32)[None, :, None]
    dest = route[2:4] + jnp.sum(jnp.where(sel, poffs[None, :, None], 0), axis=1)
    xb = _sc_scatter_rows(h2.reshape(T, D // 2), dest[0], dest[1], P)
    yb = _experts(xb, poffs // MOE_BLOCK, n_blk, sizes, w1, w3, w2)
    out = None
    for b in range(B):
        slot_major = dest[:, b * S:(b + 1) * S].reshape(2 * S)
        rows = _sc_gather_rows(yb, slot_major).reshape(2, S, yb.shape[1])
        out = _combine_dense(rows, wts.reshape(T, LANES), x1.reshape(T, D), mod, ln2_g, ln2_b, S, b, out)
    return out.reshape(B, S, D)


def _layer(x, c, w_ada, b_ada, w_in, conv_w, conv_b, fw1, fb1, ff1, fw2, fb2, ff2, fw3, decay, skip,
           w_hy_o, w_attn_o, attn_sink, w_out, ln1_g, ln1_b, rg_w, rg_b, re_w, re_b, ew1, ew3, ew2,
           ln2_g, ln2_b):
    mod = _ada(c, w_ada, b_ada)
    q, kv, hv, hx1, hx2, ga, gh = _in_proj(x, mod, w_in, conv_w, conv_b)
    attn = _attention(q, kv, attn_sink)
    hy = _hyena(hv, hx1, hx2, fw1, fb1, ff1, fw2, fb2, ff2, fw3, decay, skip)
    x1, h2, route, wts, counts = _merge(attn, hy, ga, gh, x, mod, w_attn_o, w_hy_o, w_out,
                                        ln1_g, ln1_b, rg_w, rg_b, re_w, re_b)
    return _moe(h2, x1, route, wts, counts, mod, ew1, ew3, ew2, ln2_g, ln2_b)


def kernel(x, c, w_ada, b_ada, w_in, conv_w, conv_b, filt_w1, filt_b1, filt_freq1, filt_w2, filt_b2, filt_freq2, filt_w3, filt_decay, hy_skip, w_hy_o, w_attn_o, attn_sink, w_out, ln1_g, ln1_b, router_group_w, router_group_b, router_expert_w, router_expert_b, exp_w1, exp_w3, exp_w2, ln2_g, ln2_b):
    for l in range(w_ada.shape[0]):
        x = _layer(x, c, w_ada[l], b_ada[l], w_in[l], conv_w[l], conv_b[l], filt_w1[l], filt_b1[l],
                   filt_freq1[l], filt_w2[l], filt_b2[l], filt_freq2[l], filt_w3[l], filt_decay[l],
                   hy_skip[l], w_hy_o[l], w_attn_o[l], attn_sink[l], w_out[l], ln1_g[l], ln1_b[l],
                   router_group_w[l], router_group_b[l], router_expert_w[l], router_expert_b[l],
                   exp_w1[l], exp_w3[l], exp_w2[l], ln2_g[l], ln2_b[l])
    return x
```
